```python
import jax, jax.numpy as jnp
from jax import lax
import numpy as np

D_MODEL = 1024
BATCH = 8
SEQ = 2048
DEPTH = 4

MEM_LEN = 256
EPS = 1e-6
ATTN_HEADS = 4
ATTN_HEAD_DIM = 64
ATTN_WIDTH = ATTN_HEADS * ATTN_HEAD_DIM
DILATED_PATTERNS = ((128, 1), (512, 4), (2048, 16))
WINDOW_BLOCK = 128
ROPE_THETA = 500000.0
ROPE_DIM = ATTN_HEAD_DIM // 4
CONV_GROUPS = 4
CONV_WIDTH = D_MODEL // 4
CONV_K = 3
GDN_HEADS = 4
GDN_HEAD_DIM = 128
GDN_WIDTH = GDN_HEADS * GDN_HEAD_DIM
GDN_CONV_K = 4
GDN_CHUNK = 64
MIX_WIDTH = ATTN_WIDTH + CONV_WIDTH + GDN_WIDTH
IN_SPLITS = (ATTN_WIDTH, ATTN_WIDTH, ATTN_WIDTH,
             CONV_WIDTH, CONV_WIDTH, CONV_WIDTH,
             GDN_WIDTH, GDN_WIDTH, GDN_WIDTH, GDN_HEADS, GDN_HEADS, GDN_WIDTH)
IN_WIDTH = 3 * ATTN_WIDTH + 3 * CONV_WIDTH + 4 * GDN_WIDTH + 2 * GDN_HEADS
XATTN_HEADS = 4
XATTN_HEAD_DIM = D_MODEL // XATTN_HEADS
XATTN_WIDTH = XATTN_HEADS * XATTN_HEAD_DIM
FFN_HIDDEN = -(-8 * D_MODEL // (3 * 256)) * 256

kernel_name = 'hybrid_dilated_conv_deltanet_block'


def rms_norm(x, w):
    x32 = x.astype(jnp.float32)
    y = x32 * lax.rsqrt(jnp.mean(x32 * x32, axis=-1, keepdims=True) + EPS)
    return (y * w.astype(jnp.float32)).astype(x.dtype)


def causal_depthwise_conv(x, w):
    K, C = w.shape
    return lax.conv_general_dilated(x, w[:, None, :].astype(x.dtype), window_strides=(1,),
                                    padding=[(K - 1, 0)], dimension_numbers=('NWC', 'WIO', 'NWC'),
                                    feature_group_count=C)


def rotary_tables(positions):
    inv_freq = jnp.float32(ROPE_THETA) ** (-jnp.arange(0, ROPE_DIM, 2, dtype=jnp.float32) / ROPE_DIM)
    ang = positions.astype(jnp.float32)[..., None] * inv_freq
    return jnp.cos(ang)[:, :, None, :], jnp.sin(ang)[:, :, None, :]


def apply_partial_rotary(x, cos, sin):
    half = ROPE_DIM // 2
    x1 = x[..., :half].astype(jnp.float32)
    x2 = x[..., half:ROPE_DIM].astype(jnp.float32)
    rot = jnp.concatenate([x1 * cos - x2 * sin, x2 * cos + x1 * sin], axis=-1).astype(x.dtype)
    return jnp.concatenate([rot, x[..., ROPE_DIM:]], axis=-1)


def dilated_window_attention(q, k, v, dilation, n_back):
    B, S, H, Dh = q.shape
    QB = WINDOW_BLOCK
    L = S // dilation
    nb = -(-L // QB)
    Lp = nb * QB

    def to_residue(t):
        t = t.reshape(B, L, dilation, H, Dh).transpose(0, 2, 1, 3, 4).reshape(B * dilation, L, H, Dh)
        t = jnp.pad(t, ((0, 0), (0, Lp - L), (0, 0), (0, 0)))
        return t.reshape(B * dilation, nb, QB, H, Dh)

    def with_prev(t):
        prev = jnp.pad(t[:, :-1], ((0, 0), (1, 0), (0, 0), (0, 0), (0, 0)))
        return jnp.concatenate([prev, t], axis=2)

    qb = to_residue(q)
    kw = with_prev(to_residue(k))
    vw = with_prev(to_residue(v))
    s = jnp.einsum('bnqhd,bnkhd->bnhqk', qb, kw, preferred_element_type=jnp.float32)
    qi = jnp.arange(nb)[:, None, None] * QB + jnp.arange(QB)[None, :, None]
    kj = jnp.arange(nb)[:, None, None] * QB - QB + jnp.arange(2 * QB)[None, None, :]
    dist = qi - kj
    mask = (dist >= 0) & (dist <= n_back) & (kj >= 0)
    s = jnp.where(mask[None, :, None], s, -jnp.inf)
    m = jnp.max(s, axis=-1, keepdims=True)
    p = jnp.exp(s - m)
    l = jnp.sum(p, axis=-1, keepdims=True)
    o = jnp.einsum('bnhqk,bnkhd->bnqhd', (p / l).astype(v.dtype), vw)
    lse = (m + jnp.log(l))[..., 0]
    o = o.reshape(B, dilation, Lp, H, Dh)[:, :, :L].transpose(0, 2, 1, 3, 4).reshape(B, S, H, Dh)
    lse = lse.transpose(0, 1, 3, 2).reshape(B, dilation, Lp, H)[:, :, :L]
    lse = lse.transpose(0, 2, 1, 3).reshape(B, S, H)
    return o, lse


def dilated_attention_mixer(q, k, v, cos, sin):
    B, S, _ = q.shape
    q = q.reshape(B, S, ATTN_HEADS, ATTN_HEAD_DIM)
    k = k.reshape(B, S, ATTN_HEADS, ATTN_HEAD_DIM)
    v = v.reshape(B, S, ATTN_HEADS, ATTN_HEAD_DIM)
    q = apply_partial_rotary(q, cos, sin) * (ATTN_HEAD_DIM ** -0.5)
    k = apply_partial_rotary(k, cos, sin)
    outs, lses = [], []
    for window, dilation in DILATED_PATTERNS:
        o, lse = dilated_window_attention(q, k, v, dilation, window // dilation)
        outs.append(o.astype(jnp.float32))
        lses.append(lse)
    wts = jax.nn.softmax(jnp.stack(lses, axis=0), axis=0)
    o = jnp.einsum('pbsh,pbshd->bshd', wts, jnp.stack(outs, axis=0))
    return o.reshape(B, S, ATTN_WIDTH).astype(q.dtype)


def short_conv_mixer(b_gate, c_gate, xv, conv_w):
    return b_gate * causal_depthwise_conv(c_gate * xv, conv_w)


def gated_delta_rule(q, k, v, g, beta):
    B, S, H, Dk = q.shape
    Dv = v.shape[-1]
    C = GDN_CHUNK
    N = S // C

    def chunks(t):
        return jnp.moveaxis(t.reshape(B, N, C, H, *t.shape[3:]), 3, 1)

    q, k, v, g, beta = chunks(q), chunks(k), chunks(v), chunks(g), chunks(beta)
    decay = jnp.cumsum(g, axis=-1)
    causal = jnp.tril(jnp.ones((C, C), dtype=bool))
    strict = jnp.tril(jnp.ones((C, C), dtype=bool), -1)
    rel = jnp.exp(jnp.where(causal, decay[..., :, None] - decay[..., None, :], -jnp.inf))
    k_beta = k * beta[..., None]
    a = jnp.where(strict, jnp.einsum('bhnik,bhnjk->bhnij', k_beta, k) * rel, 0.0)
    eye = jnp.broadcast_to(jnp.eye(C, dtype=jnp.float32), a.shape)
    rhs = jnp.concatenate([v * beta[..., None], k_beta * jnp.exp(decay)[..., None]], axis=-1)
    sol = lax.linalg.triangular_solve(eye + a, rhs, left_side=True, lower=True, unit_diagonal=True)
    u, w = sol[..., :Dv], sol[..., Dv:]
    attn = jnp.where(causal, jnp.einsum('bhnik,bhnjk->bhnij', q, k) * rel, 0.0)
    q_dec = q * jnp.exp(decay)[..., None]
    k_dec = k * jnp.exp(decay[..., -1:] - decay)[..., None]
    chunk_decay = jnp.exp(decay[..., -1])

    def step(state, inp):
        q_i, k_i, u_i, w_i, attn_i, cd_i = inp
        v_new = u_i - jnp.einsum('bhck,bhkv->bhcv', w_i, state)
        o_i = (jnp.einsum('bhck,bhkv->bhcv', q_i, state)
               + jnp.einsum('bhij,bhjv->bhiv', attn_i, v_new))
        state = state * cd_i[..., None, None] + jnp.einsum('bhck,bhcv->bhkv', k_i, v_new)
        return state, o_i

    xs = tuple(jnp.moveaxis(t, 2, 0) for t in (q_dec, k_dec, u, w, attn, chunk_decay))
    state0 = jnp.zeros((B, H, Dk, Dv), jnp.float32)
    _, o = lax.scan(step, state0, xs)
    return o.transpose(1, 0, 3, 2, 4).reshape(B, S, H, Dv)


def gated_deltanet_mixer(q, k, v, a, b, gate, conv_w, a_log, dt_bias, norm_w):
    B, S, _ = q.shape
    qkv = jax.nn.silu(causal_depthwise_conv(jnp.concatenate([q, k, v], axis=-1), conv_w))
    q, k, v = jnp.split(qkv.astype(jnp.float32), 3, axis=-1)
    q = q.reshape(B, S, GDN_HEADS, GDN_HEAD_DIM)
    k = k.reshape(B, S, GDN_HEADS, GDN_HEAD_DIM)
    v = v.reshape(B, S, GDN_HEADS, GDN_HEAD_DIM)
    q = q * lax.rsqrt(jnp.sum(q * q, axis=-1, keepdims=True) + EPS) * (GDN_HEAD_DIM ** -0.5)
    k = k * lax.rsqrt(jnp.sum(k * k, axis=-1, keepdims=True) + EPS)
    g = -jnp.exp(a_log.astype(jnp.float32)) * jax.nn.softplus(a.astype(jnp.float32) + dt_bias.astype(jnp.float32))
    beta = jax.nn.sigmoid(b.astype(jnp.float32))
    o = gated_delta_rule(q, k, v, g, beta)
    gate = jax.nn.silu(gate.astype(jnp.float32)).reshape(B, S, GDN_HEADS, GDN_HEAD_DIM)
    o = rms_norm(o, norm_w) * gate
    return o.reshape(B, S, GDN_WIDTH).astype(gate.dtype if gate.dtype == q.dtype else gate.dtype)


def memory_cross_attention(h, mem, w_q, w_kv, w_o):
    B, S, _ = h.shape
    M = mem.shape[1]
    q = (h @ w_q).reshape(B, S, XATTN_HEADS, XATTN_HEAD_DIM)
    k, v = jnp.split(mem @ w_kv, 2, axis=-1)
    k = k.reshape(B, M, XATTN_HEADS, XATTN_HEAD_DIM)
    v = v.reshape(B, M, XATTN_HEADS, XATTN_HEAD_DIM)
    s = jnp.einsum('bshd,bmhd->bhsm', q, k, preferred_element_type=jnp.float32) * (XATTN_HEAD_DIM ** -0.5)
    p = jax.nn.softmax(s, axis=-1)
    o = jnp.einsum('bhsm,bmhd->bshd', p.astype(v.dtype), v).reshape(B, S, XATTN_WIDTH)
    return o @ w_o


def swiglu_ffn(h, w_gate_up, w_down):
    gate, up = jnp.split(h @ w_gate_up, 2, axis=-1)
    return (jax.nn.silu(gate) * up) @ w_down


def _fwd_setup_inputs(seed: int = 0) -> dict:
    key = jax.random.key(seed)
    ks = jax.random.split(key, 24)
    f32 = jnp.float32

    def dense(k, shape, fan_in):
        return jax.random.normal(k, shape, f32) * (fan_in ** -0.5)

    def gain(k, shape):
        return 1.0 + 0.02 * jax.random.normal(k, shape, f32)

    x = jax.random.normal(ks[0], (BATCH, SEQ, D_MODEL), f32)
    mem = jax.random.normal(ks[1], (BATCH, MEM_LEN, D_MODEL), f32)
    positions = (jax.random.randint(ks[2], (BATCH, 1), 0, 4096, dtype=jnp.int32)
                 + jnp.arange(SEQ, dtype=jnp.int32)[None, :])
    dt = jnp.exp(jax.random.uniform(ks[3], (DEPTH, GDN_HEADS), f32, np.log(1e-3), np.log(1e-1)))
    return {
        'x': x,
        'mem': mem,
        'positions': positions,
        'norm_mix_pre': gain(ks[4], (DEPTH, D_MODEL)),
        'norm_mix_post': gain(ks[5], (DEPTH, D_MODEL)),
        'w_in': dense(ks[6], (DEPTH, D_MODEL, IN_WIDTH), D_MODEL),
        'conv_short': dense(ks[7], (DEPTH, CONV_K, CONV_WIDTH), CONV_K),
        'conv_gdn': dense(ks[8], (DEPTH, GDN_CONV_K, 3 * GDN_WIDTH), GDN_CONV_K),
        'gdn_a_log': jnp.log(jax.random.uniform(ks[9], (DEPTH, GDN_HEADS), f32, 1.0, 16.0)),
        'gdn_dt_bias': dt + jnp.log(-jnp.expm1(-dt)),
        'gdn_norm': gain(ks[10], (DEPTH, GDN_HEAD_DIM)),
        'w_out': dense(ks[11], (DEPTH, MIX_WIDTH, D_MODEL), MIX_WIDTH),
        'norm_mem': gain(ks[12], (DEPTH, D_MODEL)),
        'norm_xattn_pre': gain(ks[13], (DEPTH, D_MODEL)),
        'norm_xattn_post': gain(ks[14], (DEPTH, D_MODEL)),
        'w_xq': dense(ks[15], (DEPTH, D_MODEL, XATTN_WIDTH), D_MODEL),
        'w_xkv': dense(ks[16], (DEPTH, D_MODEL, 2 * XATTN_WIDTH), D_MODEL),
        'w_xo': dense(ks[17], (DEPTH, XATTN_WIDTH, D_MODEL), XATTN_WIDTH),
        'norm_ffn_pre': gain(ks[18], (DEPTH, D_MODEL)),
        'norm_ffn_post': gain(ks[19], (DEPTH, D_MODEL)),
        'w_gate_up': dense(ks[20], (DEPTH, D_MODEL, 2 * FFN_HIDDEN), D_MODEL),
        'w_down': dense(ks[21], (DEPTH, FFN_HIDDEN, D_MODEL), FFN_HIDDEN),
    }


def _fwd_reference(x, mem, positions, norm_mix_pre, norm_mix_post, w_in, conv_short, conv_gdn,
              gdn_a_log, gdn_dt_bias, gdn_norm, w_out, norm_mem, norm_xattn_pre, norm_xattn_post,
              w_xq, w_xkv, w_xo, norm_ffn_pre, norm_ffn_post, w_gate_up, w_down):
    cos, sin = rotary_tables(positions)
    split_idx = [int(i) for i in np.cumsum(IN_SPLITS)[:-1]]
    h = x
    for l in range(DEPTH):
        hn = rms_norm(h, norm_mix_pre[l])
        proj = hn @ w_in[l]
        (aq, ak, av, cb, cc, cx, gq, gk, gv, ga, gb, gg) = jnp.split(proj, split_idx, axis=-1)
        y_attn = dilated_attention_mixer(aq, ak, av, cos, sin)
        y_conv = short_conv_mixer(cb, cc, cx, conv_short[l])
        y_gdn = gated_deltanet_mixer(gq, gk, gv, ga, gb, gg, conv_gdn[l],
                                     gdn_a_log[l], gdn_dt_bias[l], gdn_norm[l]).astype(proj.dtype)
        mix = jnp.concatenate([y_attn.astype(proj.dtype), y_conv, y_gdn], axis=-1) @ w_out[l]
        h = h + rms_norm(mix, norm_mix_post[l])
        hn = rms_norm(h, norm_xattn_pre[l])
        xa = memory_cross_attention(hn, rms_norm(mem, norm_mem[l]), w_xq[l], w_xkv[l], w_xo[l])
        h = h + rms_norm(xa, norm_xattn_post[l])
        hn = rms_norm(h, norm_ffn_pre[l])
        h = h + rms_norm(swiglu_ffn(hn, w_gate_up[l], w_down[l]), norm_ffn_post[l])
    return h


import jax as _jax
import jax.numpy as _jnp

TWIN_FORMAT = 'train_step'
FWD_PARAMS = ['x', 'mem', 'positions', 'norm_mix_pre', 'norm_mix_post', 'w_in', 'conv_short', 'conv_gdn', 'gdn_a_log', 'gdn_dt_bias', 'gdn_norm', 'w_out', 'norm_mem', 'norm_xattn_pre', 'norm_xattn_post', 'w_xq', 'w_xkv', 'w_xo', 'norm_ffn_pre', 'norm_ffn_post', 'w_gate_up', 'w_down']
TWIN_WEIGHTS = ['norm_mix_pre', 'norm_mix_post', 'w_in', 'conv_short', 'conv_gdn', 'gdn_a_log', 'gdn_dt_bias', 'gdn_norm', 'w_out', 'norm_mem', 'norm_xattn_pre', 'norm_xattn_post', 'w_xq', 'w_xkv', 'w_xo', 'norm_ffn_pre', 'norm_ffn_post', 'w_gate_up', 'w_down']
TWIN_DIFF_INPUT = 'x'
TWIN_INPUTS = ['x', 'mem', 'positions', 'norm_mix_pre', 'norm_mix_post', 'w_in', 'conv_short', 'conv_gdn', 'gdn_a_log', 'gdn_dt_bias', 'gdn_norm', 'w_out', 'norm_mem', 'norm_xattn_pre', 'norm_xattn_post', 'w_xq', 'w_xkv', 'w_xo', 'norm_ffn_pre', 'norm_ffn_post', 'w_gate_up', 'w_down', 'loss_target', 'm_norm_mix_pre', 'm_norm_mix_post', 'm_w_in', 'm_conv_short', 'm_conv_gdn', 'm_gdn_a_log', 'm_gdn_dt_bias', 'm_gdn_norm', 'm_w_out', 'm_norm_mem', 'm_norm_xattn_pre', 'm_norm_xattn_post', 'm_w_xq', 'm_w_xkv', 'm_w_xo', 'm_norm_ffn_pre', 'm_norm_ffn_post', 'm_w_gate_up', 'm_w_down', 'v_norm_mix_pre', 'v_norm_mix_post', 'v_w_in', 'v_conv_short', 'v_conv_gdn', 'v_gdn_a_log', 'v_gdn_dt_bias', 'v_gdn_norm', 'v_w_out', 'v_norm_mem', 'v_norm_xattn_pre', 'v_norm_xattn_post', 'v_w_xq', 'v_w_xkv', 'v_w_xo', 'v_norm_ffn_pre', 'v_norm_ffn_post', 'v_w_gate_up', 'v_w_down']
TWIN_OUTPUTS = ['loss', 'grad_x', 'grad_norm_mix_pre', 'grad_norm_mix_post', 'grad_w_in', 'grad_conv_short', 'grad_conv_gdn', 'grad_gdn_a_log', 'grad_gdn_dt_bias', 'grad_gdn_norm', 'grad_w_out', 'grad_norm_mem', 'grad_norm_xattn_pre', 'grad_norm_xattn_post', 'grad_w_xq', 'grad_w_xkv', 'grad_w_xo', 'grad_norm_ffn_pre', 'grad_norm_ffn_post', 'grad_w_gate_up', 'grad_w_down', 'delta_norm_mix_pre', 'delta_norm_mix_post', 'delta_w_in', 'delta_conv_short', 'delta_conv_gdn', 'delta_gdn_a_log', 'delta_gdn_dt_bias', 'delta_gdn_norm', 'delta_w_out', 'delta_norm_mem', 'delta_norm_xattn_pre', 'delta_norm_xattn_post', 'delta_w_xq', 'delta_w_xkv', 'delta_w_xo', 'delta_norm_ffn_pre', 'delta_norm_ffn_post', 'delta_w_gate_up', 'delta_w_down', 'new_m_norm_mix_pre', 'new_m_norm_mix_post', 'new_m_w_in', 'new_m_conv_short', 'new_m_conv_gdn', 'new_m_gdn_a_log', 'new_m_gdn_dt_bias', 'new_m_gdn_norm', 'new_m_w_out', 'new_m_norm_mem', 'new_m_norm_xattn_pre', 'new_m_norm_xattn_post', 'new_m_w_xq', 'new_m_w_xkv', 'new_m_w_xo', 'new_m_norm_ffn_pre', 'new_m_norm_ffn_post', 'new_m_w_gate_up', 'new_m_w_down', 'new_v_norm_mix_pre', 'new_v_norm_mix_post', 'new_v_w_in', 'new_v_conv_short', 'new_v_conv_gdn', 'new_v_gdn_a_log', 'new_v_gdn_dt_bias', 'new_v_gdn_norm', 'new_v_w_out', 'new_v_norm_mem', 'new_v_norm_xattn_pre', 'new_v_norm_xattn_post', 'new_v_w_xq', 'new_v_w_xkv', 'new_v_w_xo', 'new_v_norm_ffn_pre', 'new_v_norm_ffn_post', 'new_v_w_gate_up', 'new_v_w_down']
TWIN_LEAF_KINDS = {'loss': 'loss', 'grad_x': 'grad_x', 'grad_norm_mix_pre': 'grad_w', 'grad_norm_mix_post': 'grad_w', 'grad_w_in': 'grad_w', 'grad_conv_short': 'grad_w', 'grad_conv_gdn': 'grad_w', 'grad_gdn_a_log': 'grad_w', 'grad_gdn_dt_bias': 'grad_w', 'grad_gdn_norm': 'grad_w', 'grad_w_out': 'grad_w', 'grad_norm_mem': 'grad_w', 'grad_norm_xattn_pre': 'grad_w', 'grad_norm_xattn_post': 'grad_w', 'grad_w_xq': 'grad_w', 'grad_w_xkv': 'grad_w', 'grad_w_xo': 'grad_w', 'grad_norm_ffn_pre': 'grad_w', 'grad_norm_ffn_post': 'grad_w', 'grad_w_gate_up': 'grad_w', 'grad_w_down': 'grad_w', 'delta_norm_mix_pre': 'delta_w', 'delta_norm_mix_post': 'delta_w', 'delta_w_in': 'delta_w', 'delta_conv_short': 'delta_w', 'delta_conv_gdn': 'delta_w', 'delta_gdn_a_log': 'delta_w', 'delta_gdn_dt_bias': 'delta_w', 'delta_gdn_norm': 'delta_w', 'delta_w_out': 'delta_w', 'delta_norm_mem': 'delta_w', 'delta_norm_xattn_pre': 'delta_w', 'delta_norm_xattn_post': 'delta_w', 'delta_w_xq': 'delta_w', 'delta_w_xkv': 'delta_w', 'delta_w_xo': 'delta_w', 'delta_norm_ffn_pre': 'delta_w', 'delta_norm_ffn_post': 'delta_w', 'delta_w_gate_up': 'delta_w', 'delta_w_down': 'delta_w', 'new_m_norm_mix_pre': 'new_m', 'new_m_norm_mix_post': 'new_m', 'new_m_w_in': 'new_m', 'new_m_conv_short': 'new_m', 'new_m_conv_gdn': 'new_m', 'new_m_gdn_a_log': 'new_m', 'new_m_gdn_dt_bias': 'new_m', 'new_m_gdn_norm': 'new_m', 'new_m_w_out': 'new_m', 'new_m_norm_mem': 'new_m', 'new_m_norm_xattn_pre': 'new_m', 'new_m_norm_xattn_post': 'new_m', 'new_m_w_xq': 'new_m', 'new_m_w_xkv': 'new_m', 'new_m_w_xo': 'new_m', 'new_m_norm_ffn_pre': 'new_m', 'new_m_norm_ffn_post': 'new_m', 'new_m_w_gate_up': 'new_m', 'new_m_w_down': 'new_m', 'new_v_norm_mix_pre': 'new_v', 'new_v_norm_mix_post': 'new_v', 'new_v_w_in': 'new_v', 'new_v_conv_short': 'new_v', 'new_v_conv_gdn': 'new_v', 'new_v_gdn_a_log': 'new_v', 'new_v_gdn_dt_bias': 'new_v', 'new_v_gdn_norm': 'new_v', 'new_v_w_out': 'new_v', 'new_v_norm_mem': 'new_v', 'new_v_norm_xattn_pre': 'new_v', 'new_v_norm_xattn_post': 'new_v', 'new_v_w_xq': 'new_v', 'new_v_w_xkv': 'new_v', 'new_v_w_xo': 'new_v', 'new_v_norm_ffn_pre': 'new_v', 'new_v_norm_ffn_post': 'new_v', 'new_v_w_gate_up': 'new_v', 'new_v_w_down': 'new_v'}


def _forward(args):
    return _fwd_reference(*[args[k] for k in FWD_PARAMS])


def _output_shape():
    out = _jax.eval_shape(lambda: _forward(_fwd_setup_inputs(0)))
    return out.shape, out.dtype

N_MICROBATCH = 1
ADAM_LR = 0.001
ADAM_B1 = 0.9
ADAM_B2 = 0.999
ADAM_EPS = 1e-08
ADAM_WD = 0.01
ADAM_STEP = 10
PER_EXAMPLE_BATCH_AXIS = {'x': 0, 'mem': 0, 'positions': 0, 'loss_target': 0}
SHARED_INPUTS = []
_WEIGHT_DTYPES = {'norm_mix_pre': _jnp.float32, 'norm_mix_post': _jnp.float32, 'w_in': _jnp.float32, 'conv_short': _jnp.float32, 'conv_gdn': _jnp.float32, 'gdn_a_log': _jnp.float32, 'gdn_dt_bias': _jnp.float32, 'gdn_norm': _jnp.float32, 'w_out': _jnp.float32, 'norm_mem': _jnp.float32, 'norm_xattn_pre': _jnp.float32, 'norm_xattn_post': _jnp.float32, 'w_xq': _jnp.float32, 'w_xkv': _jnp.float32, 'w_xo': _jnp.float32, 'norm_ffn_pre': _jnp.float32, 'norm_ffn_post': _jnp.float32, 'w_gate_up': _jnp.float32, 'w_down': _jnp.float32}
MOMENT_SCALE = {'norm_mix_pre': 3.459603e+00, 'norm_mix_post': 1.624245e+01, 'w_in': 1.778864e+00, 'conv_short': 2.169827e+00, 'conv_gdn': 1.452899e+00, 'gdn_a_log': 4.522546e+00, 'gdn_dt_bias': 4.363860e+00, 'gdn_norm': 5.523733e+00, 'w_out': 3.296563e+00, 'norm_mem': 7.724940e+00, 'norm_xattn_pre': 2.167866e+00, 'norm_xattn_post': 1.834533e+01, 'w_xq': 2.212934e+00, 'w_xkv': 5.359139e+00, 'w_xo': 7.425032e+00, 'norm_ffn_pre': 2.835614e+00, 'norm_ffn_post': 1.596097e+01, 'w_gate_up': 1.170472e+00, 'w_down': 2.203234e+00}


def _to_microbatches(a, axis):
    t = _jnp.moveaxis(a, axis, 0)
    t = t.reshape((N_MICROBATCH, t.shape[0] // N_MICROBATCH) + t.shape[1:])
    return _jnp.moveaxis(t, 1, axis + 1)


def setup_inputs(seed: int = 0) -> dict:
    inp = _fwd_setup_inputs(seed)
    key = _jax.random.fold_in(_jax.random.key(seed), 7919)
    shape, _ = _output_shape()
    out = dict(inp)
    out["loss_target"] = _jax.random.normal(_jax.random.fold_in(key, 0), shape, _jnp.float32)
    for i, name in enumerate(TWIN_WEIGHTS):
        w = inp[name].astype(_jnp.float32)
        if MOMENT_SCALE is None:
            s = _jnp.sqrt(_jnp.mean(_jnp.square(w)) + 1e-30)
        else:
            s = MOMENT_SCALE[name]
        km, kv = _jax.random.split(_jax.random.fold_in(key, i + 1))
        out[name] = w
        out["m_" + name] = s * _jax.random.normal(km, w.shape, _jnp.float32)
        out["v_" + name] = (s * s) * _jax.random.uniform(kv, w.shape, _jnp.float32, 0.5, 1.5)
    if N_MICROBATCH > 1:
        for name, axis in PER_EXAMPLE_BATCH_AXIS.items():
            out[name] = _to_microbatches(out[name], axis)
    return {'x': out['x'], 'mem': out['mem'], 'positions': out['positions'], 'norm_mix_pre': out['norm_mix_pre'], 'norm_mix_post': out['norm_mix_post'], 'w_in': out['w_in'], 'conv_short': out['conv_short'], 'conv_gdn': out['conv_gdn'], 'gdn_a_log': out['gdn_a_log'], 'gdn_dt_bias': out['gdn_dt_bias'], 'gdn_norm': out['gdn_norm'], 'w_out': out['w_out'], 'norm_mem': out['norm_mem'], 'norm_xattn_pre': out['norm_xattn_pre'], 'norm_xattn_post': out['norm_xattn_post'], 'w_xq': out['w_xq'], 'w_xkv': out['w_xkv'], 'w_xo': out['w_xo'], 'norm_ffn_pre': out['norm_ffn_pre'], 'norm_ffn_post': out['norm_ffn_post'], 'w_gate_up': out['w_gate_up'], 'w_down': out['w_down'], 'loss_target': out['loss_target'], 'm_norm_mix_pre': out['m_norm_mix_pre'], 'm_norm_mix_post': out['m_norm_mix_post'], 'm_w_in': out['m_w_in'], 'm_conv_short': out['m_conv_short'], 'm_conv_gdn': out['m_conv_gdn'], 'm_gdn_a_log': out['m_gdn_a_log'], 'm_gdn_dt_bias': out['m_gdn_dt_bias'], 'm_gdn_norm': out['m_gdn_norm'], 'm_w_out': out['m_w_out'], 'm_norm_mem': out['m_norm_mem'], 'm_norm_xattn_pre': out['m_norm_xattn_pre'], 'm_norm_xattn_post': out['m_norm_xattn_post'], 'm_w_xq': out['m_w_xq'], 'm_w_xkv': out['m_w_xkv'], 'm_w_xo': out['m_w_xo'], 'm_norm_ffn_pre': out['m_norm_ffn_pre'], 'm_norm_ffn_post': out['m_norm_ffn_post'], 'm_w_gate_up': out['m_w_gate_up'], 'm_w_down': out['m_w_down'], 'v_norm_mix_pre': out['v_norm_mix_pre'], 'v_norm_mix_post': out['v_norm_mix_post'], 'v_w_in': out['v_w_in'], 'v_conv_short': out['v_conv_short'], 'v_conv_gdn': out['v_conv_gdn'], 'v_gdn_a_log': out['v_gdn_a_log'], 'v_gdn_dt_bias': out['v_gdn_dt_bias'], 'v_gdn_norm': out['v_gdn_norm'], 'v_w_out': out['v_w_out'], 'v_norm_mem': out['v_norm_mem'], 'v_norm_xattn_pre': out['v_norm_xattn_pre'], 'v_norm_xattn_post': out['v_norm_xattn_post'], 'v_w_xq': out['v_w_xq'], 'v_w_xkv': out['v_w_xkv'], 'v_w_xo': out['v_w_xo'], 'v_norm_ffn_pre': out['v_norm_ffn_pre'], 'v_norm_ffn_post': out['v_norm_ffn_post'], 'v_w_gate_up': out['v_w_gate_up'], 'v_w_down': out['v_w_down']}


def _loss(weights, diff, rest, loss_target):
    with _jax.named_scope("forward"):
        args = {**rest, TWIN_DIFF_INPUT: diff, **{k: w.astype(_WEIGHT_DTYPES[k]) for k, w in weights.items()}}
        y = _forward(args)
    with _jax.named_scope("loss_head"):
        err = _jnp.square(y.astype(_jnp.float32) - loss_target)
        return 0.5 * _jnp.sum(_jnp.mean(err, axis=-1)) if err.ndim else 0.5 * err


def _adamw(w, g, m, v):
    m = ADAM_B1 * m + (1.0 - ADAM_B1) * g
    v = ADAM_B2 * v + (1.0 - ADAM_B2) * _jnp.square(g)
    m_hat = m / (1.0 - ADAM_B1 ** ADAM_STEP)
    v_hat = v / (1.0 - ADAM_B2 ** ADAM_STEP)
    delta = -ADAM_LR * (m_hat / (_jnp.sqrt(v_hat) + ADAM_EPS) + ADAM_WD * w)
    return delta, m, v


def reference(x, mem, positions, norm_mix_pre, norm_mix_post, w_in, conv_short, conv_gdn, gdn_a_log, gdn_dt_bias, gdn_norm, w_out, norm_mem, norm_xattn_pre, norm_xattn_post, w_xq, w_xkv, w_xo, norm_ffn_pre, norm_ffn_post, w_gate_up, w_down, loss_target, m_norm_mix_pre, m_norm_mix_post, m_w_in, m_conv_short, m_conv_gdn, m_gdn_a_log, m_gdn_dt_bias, m_gdn_norm, m_w_out, m_norm_mem, m_norm_xattn_pre, m_norm_xattn_post, m_w_xq, m_w_xkv, m_w_xo, m_norm_ffn_pre, m_norm_ffn_post, m_w_gate_up, m_w_down, v_norm_mix_pre, v_norm_mix_post, v_w_in, v_conv_short, v_conv_gdn, v_gdn_a_log, v_gdn_dt_bias, v_gdn_norm, v_w_out, v_norm_mem, v_norm_xattn_pre, v_norm_xattn_post, v_w_xq, v_w_xkv, v_w_xo, v_norm_ffn_pre, v_norm_ffn_post, v_w_gate_up, v_w_down):
    given = dict(x=x, mem=mem, positions=positions, norm_mix_pre=norm_mix_pre, norm_mix_post=norm_mix_post, w_in=w_in, conv_short=conv_short, conv_gdn=conv_gdn, gdn_a_log=gdn_a_log, gdn_dt_bias=gdn_dt_bias, gdn_norm=gdn_norm, w_out=w_out, norm_mem=norm_mem, norm_xattn_pre=norm_xattn_pre, norm_xattn_post=norm_xattn_post, w_xq=w_xq, w_xkv=w_xkv, w_xo=w_xo, norm_ffn_pre=norm_ffn_pre, norm_ffn_post=norm_ffn_post, w_gate_up=w_gate_up, w_down=w_down, loss_target=loss_target, m_norm_mix_pre=m_norm_mix_pre, m_norm_mix_post=m_norm_mix_post, m_w_in=m_w_in, m_conv_short=m_conv_short, m_conv_gdn=m_conv_gdn, m_gdn_a_log=m_gdn_a_log, m_gdn_dt_bias=m_gdn_dt_bias, m_gdn_norm=m_gdn_norm, m_w_out=m_w_out, m_norm_mem=m_norm_mem, m_norm_xattn_pre=m_norm_xattn_pre, m_norm_xattn_post=m_norm_xattn_post, m_w_xq=m_w_xq, m_w_xkv=m_w_xkv, m_w_xo=m_w_xo, m_norm_ffn_pre=m_norm_ffn_pre, m_norm_ffn_post=m_norm_ffn_post, m_w_gate_up=m_w_gate_up, m_w_down=m_w_down, v_norm_mix_pre=v_norm_mix_pre, v_norm_mix_post=v_norm_mix_post, v_w_in=v_w_in, v_conv_short=v_conv_short, v_conv_gdn=v_conv_gdn, v_gdn_a_log=v_gdn_a_log, v_gdn_dt_bias=v_gdn_dt_bias, v_gdn_norm=v_gdn_norm, v_w_out=v_w_out, v_norm_mem=v_norm_mem, v_norm_xattn_pre=v_norm_xattn_pre, v_norm_xattn_post=v_norm_xattn_post, v_w_xq=v_w_xq, v_w_xkv=v_w_xkv, v_w_xo=v_w_xo, v_norm_ffn_pre=v_norm_ffn_pre, v_norm_ffn_post=v_norm_ffn_post, v_w_gate_up=v_w_gate_up, v_w_down=v_w_down)
    weights = {n: given[n] for n in TWIN_WEIGHTS}
    shared = {n: given[n] for n in SHARED_INPUTS}
    per_example = {n: given[n] for n in ['x', 'mem', 'positions']}
    grad_fn = _jax.value_and_grad(_loss, argnums=(0, 1))

    def one_microbatch(ex, loss_target):
        ex = dict(ex)
        diff = ex.pop(TWIN_DIFF_INPUT)
        return grad_fn(weights, diff, {**shared, **ex}, loss_target)

    if N_MICROBATCH == 1:
        loss, (grad_w, grad_x) = one_microbatch(per_example, given["loss_target"])
    else:
        def body(carry, xs):
            loss_sum, grad_sum = carry
            l_k, (gw_k, gx_k) = one_microbatch(xs[0], xs[1])
            with _jax.named_scope("update"):
                return (loss_sum + l_k, _jax.tree.map(_jnp.add, grad_sum, gw_k)), gx_k

        init = (_jnp.zeros((), _jnp.float32), _jax.tree.map(_jnp.zeros_like, weights))
        (loss, grad_w), grad_x = _jax.lax.scan(body, init, (per_example, given["loss_target"]))
    with _jax.named_scope("update"):
        delta_w, new_m, new_v = {}, {}, {}
        for n in TWIN_WEIGHTS:
            delta_w[n], new_m[n], new_v[n] = _adamw(weights[n], grad_w[n], given["m_" + n], given["v_" + n])
    return (loss, grad_x, *[grad_w[n] for n in TWIN_WEIGHTS], *[delta_w[n] for n in TWIN_WEIGHTS],
            *[new_m[n] for n in TWIN_WEIGHTS], *[new_v[n] for n in TWIN_WEIGHTS])
```

```python
import functools

import jax
import jax.numpy as jnp
from jax import lax
from jax.experimental import pallas as pl
from jax.experimental.pallas import tpu as pltpu

f32 = jnp.float32
bf16 = jnp.bfloat16
HIGHEST = lax.Precision.HIGHEST
MESH = pl.DeviceIdType.MESH

N_DEV = 8
DEPTH = 4
D_MODEL = 1024
EPS = 1e-6
ATTN_HEADS, ATTN_HEAD_DIM = 4, 64
ATTN_WIDTH = ATTN_HEADS * ATTN_HEAD_DIM
DILATIONS = (1, 4, 16)
QB = 128
ROPE_THETA = 500000.0
ROPE_DIM = ATTN_HEAD_DIM // 4
CONV_WIDTH = 256
GDN_HEADS, GDN_HEAD_DIM = 4, 128
GDN_WIDTH = GDN_HEADS * GDN_HEAD_DIM
GDN_CHUNK = 64
XATTN_HEADS, XATTN_HEAD_DIM = 4, 256
FFN_HIDDEN = 2816
IN_WIDTH = 3592
AB_AT = 3 * ATTN_WIDTH + 3 * CONV_WIDTH + 3 * GDN_WIDTH
MAIN_WIDTH = IN_WIDTH - 2 * GDN_HEADS
LANES = 128
ROW_TILE = 256
VMEM_LIMIT = 56 * 1024 * 1024

ADAM_LR, ADAM_B1, ADAM_B2, ADAM_EPS, ADAM_WD, ADAM_STEP = 0.001, 0.9, 0.999, 1e-08, 0.01, 10

BIG = (("w_in", 449), ("w_out", 128), ("w_xq", 128), ("w_xkv", 256), ("w_xo", 128), ("w_gate_up", 704), ("w_down", 352))
BIG_ROWS = 2160
COL_SHARDED = ("w_in", "w_xkv", "w_gate_up")
NORMS = ("norm_mix_pre", "norm_mix_post", "norm_mem", "norm_xattn_pre", "norm_xattn_post", "norm_ffn_pre", "norm_ffn_post")
WEIGHTS = ("norm_mix_pre", "norm_mix_post", "w_in", "conv_short", "conv_gdn", "gdn_a_log", "gdn_dt_bias", "gdn_norm", "w_out",
           "norm_mem", "norm_xattn_pre", "norm_xattn_post", "w_xq", "w_xkv", "w_xo", "norm_ffn_pre", "norm_ffn_post",
           "w_gate_up", "w_down")


def _params(n_grid):
    return pltpu.CompilerParams(dimension_semantics=("arbitrary",) * n_grid, vmem_limit_bytes=VMEM_LIMIT)


def _pick(n, cands):
    for c in cands:
        if n % c == 0:
            return c
    return n


def _mm(a, b, ta, tb, out_dtype, name):
    m, k = (a.shape[1], a.shape[0]) if ta else a.shape
    n = b.shape[0] if tb else b.shape[1]
    tm, tn = _pick(m, (512, 256, 128)), _pick(n, (512, 256, 128))
    tk = k if k <= 1024 else _pick(k, (1024, 512, 256, 128))
    nk = k // tk
    a_spec = pl.BlockSpec((tk, tm), lambda i, j, kk: (kk, i)) if ta else pl.BlockSpec((tm, tk), lambda i, j, kk: (i, kk))
    b_spec = pl.BlockSpec((tn, tk), lambda i, j, kk: (j, kk)) if tb else pl.BlockSpec((tk, tn), lambda i, j, kk: (kk, j))
    dims = (((0 if ta else 1,), (1 if tb else 0,)), ((), ()))

    def body(a_ref, b_ref, o_ref, acc_ref):
        kk = pl.program_id(2)
        p = lax.dot_general(a_ref[...].astype(bf16), b_ref[...].astype(bf16), dims, preferred_element_type=f32)

        @pl.when(kk == 0)
        def _():
            acc_ref[...] = p

        @pl.when(kk > 0)
        def _():
            acc_ref[...] += p

        @pl.when(kk == nk - 1)
        def _():
            o_ref[...] = acc_ref[...].astype(o_ref.dtype)

    return pl.pallas_call(
        body, name=name, grid=(m // tm, n // tn, nk), in_specs=[a_spec, b_spec],
        out_specs=pl.BlockSpec((tm, tn), lambda i, j, kk: (i, j)), out_shape=jax.ShapeDtypeStruct((m, n), out_dtype),
        scratch_shapes=[pltpu.VMEM((tm, tn), f32)], compiler_params=_params(3))(a, b)


def _linear(x, w, name):
    @jax.custom_vjp
    def lin(x, w):
        return _mm(x, w, False, False, f32, name + "_y")

    def lin_f(x, w):
        return _mm(x, w, False, False, f32, name + "_y"), (x, w)

    def lin_b(res, dy):
        x, w = res
        return _mm(dy, w, False, True, f32, name + "_dx"), _mm(x, dy, True, False, bf16, name + "_dw")

    lin.defvjp(lin_f, lin_b)
    return lin(x, w)


def _block_op(name, f, grid, in_specs, out_defs, arrays, diff, acc=None, gdefs=None):
    acc, gdefs = acc or {}, gdefs or {}
    n_in, n_out, n_grid = len(in_specs), len(out_defs), len(grid)

    def fwd_call(*xs):
        def body(*refs):
            outs = f(*[r[...] for r in refs[:n_in]])
            for r, o in zip(refs[n_in:], outs):
                r[...] = o.astype(r.dtype)

        return pl.pallas_call(
            body, name=name + "_fwd", grid=grid, in_specs=in_specs, out_specs=[d[1] for d in out_defs],
            out_shape=[d[0] for d in out_defs], compiler_params=_params(n_grid))(*xs)

    def bwd_call(*xs_and_cts):
        def body(*refs):
            xs = [r[...] for r in refs[:n_in]]
            cts = tuple(r[...] for r in refs[n_in:n_in + n_out])

            def of_diff(*dx):
                full = list(xs)
                for i, v in zip(diff, dx):
                    full[i] = v
                return tuple(f(*full))

            _, vjp = jax.vjp(of_diff, *[xs[i] for i in diff])
            grads = vjp(cts)
            for i, g, r in zip(diff, grads, refs[n_in + n_out:]):
                if i in acc:
                    first = functools.reduce(jnp.logical_and, [pl.program_id(a) == 0 for a in acc[i]])

                    @pl.when(first)
                    def _(r=r):
                        r[...] = jnp.zeros_like(r)

                    r[...] += g.astype(r.dtype)
                else:
                    r[...] = g.astype(r.dtype)

        g_defs = [gdefs.get(i, (jax.ShapeDtypeStruct(arrays[i].shape, f32), in_specs[i])) for i in diff]
        return pl.pallas_call(
            body, name=name + "_bwd", grid=grid, in_specs=list(in_specs) + [d[1] for d in out_defs],
            out_specs=[d[1] for d in g_defs], out_shape=[d[0] for d in g_defs], compiler_params=_params(n_grid))(*xs_and_cts)

    return fwd_call, bwd_call


def _simple_op(name, f, grid, in_specs, out_defs, arrays, diff, acc=None):
    fwd_call, bwd_call = _block_op(name, f, grid, in_specs, out_defs, arrays, diff, acc)

    @jax.custom_vjp
    def op(*xs):
        return tuple(fwd_call(*xs))

    def op_f(*xs):
        return tuple(fwd_call(*xs)), xs

    def op_b(xs, cts):
        grads = bwd_call(*xs, *cts)
        out = [jnp.zeros_like(x) for x in xs]
        for i, g in zip(diff, grads):
            out[i] = g
        return tuple(out)

    op.defvjp(op_f, op_b)
    return op(*arrays)


def _rows(width, tile=ROW_TILE):
    return pl.BlockSpec((tile, width), lambda i: (i, 0))


def _whole(shape):
    return pl.BlockSpec(shape, lambda *_: (0,) * len(shape))


def _sds(shape):
    return jax.ShapeDtypeStruct(shape, f32)


def _rms(x, w):
    return x * lax.rsqrt(jnp.mean(x * x, axis=-1, keepdims=True) + EPS) * w


def rms_norm(x, w, name):
    r, d = x.shape
    return _simple_op(name, lambda x, w: (_rms(x, w),), (r // ROW_TILE,), [_rows(d), _whole((1, d))],
                      [(_sds((r, d)), _rows(d))], (x, w), (0, 1), {1: (0,)})[0]


def add_norm(h, y, w, name):
    r, d = h.shape
    return _simple_op(name, lambda h, y, w: (h + _rms(y, w),), (r // ROW_TILE,), [_rows(d), _rows(d), _whole((1, d))],
                      [(_sds((r, d)), _rows(d))], (h, y, w), (0, 1, 2), {2: (0,)})[0]


def _swap8(x):
    def raw(x):
        lane = lax.broadcasted_iota(jnp.int32, x.shape, 1) % ATTN_HEAD_DIM
        half = ROPE_DIM // 2
        up = pltpu.roll(x, x.shape[1] - half, axis=1)
        down = pltpu.roll(x, half, axis=1)
        return jnp.where(lane < half, up, jnp.where(lane < ROPE_DIM, down, 0.0))

    @jax.custom_vjp
    def swap(x):
        return raw(x)

    swap.defvjp(lambda x: (raw(x), None), lambda _, g: (raw(g),))
    return swap(x)


def rope(x, cos_t, sin_t, scale, name):
    r, d = x.shape
    return _simple_op(name, lambda x, c, s: ((x * c + _swap8(x) * s) * scale,), (r // ROW_TILE,), [_rows(d)] * 3,
                      [(_sds((r, d)), _rows(d))], (x, cos_t, sin_t), (0,))[0]


def _shift_rows(x, k):
    n = x.shape[0]

    def down(x):
        row = lax.broadcasted_iota(jnp.int32, x.shape, 0)
        return jnp.where(row >= k, pltpu.roll(x, k, axis=0), 0.0)

    def up(x):
        row = lax.broadcasted_iota(jnp.int32, x.shape, 0)
        return jnp.where(row < n - k, pltpu.roll(x, n - k, axis=0), 0.0)

    @jax.custom_vjp
    def shift(x):
        return down(x)

    shift.defvjp(lambda x: (down(x), None), lambda _, g: (up(g),))
    return shift(x)


def _causal_conv(x, w):
    taps = w.shape[0]
    y = x * w[taps - 1:taps, :]
    for j in range(taps - 1):
        y = y + _shift_rows(x, taps - 1 - j) * w[j:j + 1, :]
    return y


def _cols(rows, at=0):
    return pl.BlockSpec((rows, LANES), lambda j: (0, at + j))


def short_conv(cb, cc, cx, w, name):
    s, c = cb.shape
    taps = w.shape[0]
    return _simple_op(name, lambda b, c_, x, w: (b * _causal_conv(c_ * x, w),), (c // LANES,),
                      [_cols(s)] * 3 + [_cols(taps)], [(_sds((s, c)), _cols(s))], (cb, cc, cx, w), (0, 1, 2, 3))[0]


def gdn_pre(qkv, w, name):
    s, c = qkv.shape
    taps = w.shape[0]

    def f(x, w):
        j = pl.program_id(0)
        y = jax.nn.silu(_causal_conv(x, w))
        normed = y * lax.rsqrt(jnp.sum(y * y, axis=-1, keepdims=True) + EPS)
        scale = jnp.where(j < GDN_HEADS, GDN_HEAD_DIM ** -0.5, 1.0).astype(f32)
        return (jnp.where(j < 2 * GDN_HEADS, normed * scale, y),)

    return _simple_op(name, f, (c // LANES,), [_cols(s), _cols(taps)], [(_sds((s, c)), _cols(s))], (qkv, w), (0, 1))[0]


def gate_beta(ab, pv, name):
    s = ab.shape[0]

    def f(ab, pv):
        row = lax.broadcasted_iota(jnp.int32, (LANES, GDN_WIDTH), 0)
        head = lax.broadcasted_iota(jnp.int32, (LANES, GDN_WIDTH), 1) // GDN_HEAD_DIM
        spread_a = (row == head).astype(f32)
        spread_b = (row == head + GDN_HEADS).astype(f32)
        a = jnp.dot(ab, spread_a, precision=HIGHEST, preferred_element_type=f32)
        b = jnp.dot(ab, spread_b, precision=HIGHEST, preferred_element_type=f32)
        p = jnp.dot(pv, spread_a, precision=HIGHEST, preferred_element_type=f32)
        g = -jnp.exp(p[0:1, :]) * jax.nn.softplus(a + p[1:2, :])
        return g, jax.nn.sigmoid(b)

    outs = [(_sds((s, GDN_WIDTH)), _rows(GDN_WIDTH))] * 2
    return _simple_op(name, f, (s // ROW_TILE,), [_rows(LANES), _whole((8, LANES))], outs, (ab, pv), (0, 1), {1: (0,)})


def gdn_post(o, gate, w, name):
    s, c = o.shape
    spec = pl.BlockSpec((ROW_TILE, LANES), lambda i, j: (i, j))
    return _simple_op(name, lambda o, g, w: (_rms(o, w) * jax.nn.silu(g),), (s // ROW_TILE, c // LANES),
                      [spec, spec, _whole((1, LANES))], [(_sds((s, c)), spec)], (o, gate, w), (0, 1, 2), {2: (0, 1)})[0]


def swiglu(gu, name):
    s, two_f = gu.shape
    hidden = two_f // 2
    tile = 256
    nb = hidden // tile
    g_spec = pl.BlockSpec((ROW_TILE, tile), lambda i, j: (i, j))
    u_spec = pl.BlockSpec((ROW_TILE, tile), lambda i, j: (i, j + nb))
    fwd_call, bwd_call = _block_op(name, lambda g, u: (jax.nn.silu(g) * u,), (s // ROW_TILE, nb), [g_spec, u_spec],
                                   [(_sds((s, hidden)), g_spec)], (gu, gu), (0, 1),
                                   gdefs={0: (_sds((s, hidden)), g_spec), 1: (_sds((s, hidden)), g_spec)})

    @jax.custom_vjp
    def op(gu):
        return fwd_call(gu, gu)[0]

    def op_b(gu, ct):
        dg, du = bwd_call(gu, gu, ct)
        return (jnp.concatenate([dg, du], axis=1),)

    op.defvjp(lambda gu: (fwd_call(gu, gu)[0], gu), op_b)
    return op(gu)


def attn_merge(outs, lses, name):
    s, c = outs[0].shape

    def f(o1, o2, o3, l1, l2, l3):
        m = lax.stop_gradient(jnp.maximum(jnp.maximum(l1, l2), l3))
        e1, e2, e3 = jnp.exp(l1 - m), jnp.exp(l2 - m), jnp.exp(l3 - m)
        return ((e1 * o1 + e2 * o2 + e3 * o3) / (e1 + e2 + e3),)

    return _simple_op(name, f, (s // ROW_TILE,), [_rows(c)] * 6, [(_sds((s, c)), _rows(c))], (*outs, *lses), tuple(range(6)))[0]


def loss_rows(y, target, name):
    s, d = y.shape
    nt = s // ROW_TILE

    def f(y, t):
        e = y - t
        part = 0.5 * jnp.sum(jnp.mean(e * e, axis=-1, keepdims=True), axis=0, keepdims=True)
        return (jnp.broadcast_to(part * (1.0 / (8 * LANES)), (8, LANES)),)

    out = _simple_op(name, f, (nt,), [_rows(d)] * 2, [(_sds((nt * 8, LANES)), pl.BlockSpec((8, LANES), lambda i: (i, 0)))],
                     (y, target), (0,))[0]
    return jnp.sum(out)


def _mxu(a, b, form):
    dims = {"nn": ((1,), (0,)), "nt": ((1,), (1,)), "tn": ((0,), (0,))}

    def raw(a, b, form):
        return lax.dot_general(a.astype(bf16), b.astype(bf16), (dims[form], ((), ())), preferred_element_type=f32)

    @jax.custom_vjp
    def prod(a, b):
        return raw(a, b, form)

    def prod_b(res, ct):
        a, b = res
        if form == "nn":
            return raw(ct, b, "nt"), raw(a, ct, "tn")
        if form == "nt":
            return raw(ct, b, "nn"), raw(ct, a, "tn")
        return raw(b, ct, "nt"), raw(a, ct, "nn")

    prod.defvjp(lambda a, b: (raw(a, b, form), (a, b)), prod_b)
    return prod(a, b)


def band_attention(q, k, v, nb, name):
    h, r, qb, dh = q.shape

    def f(q, kp, kc, vp, vc):
        has_prev = (pl.program_id(1) % nb) > 0
        i = lax.broadcasted_iota(jnp.int32, (qb, qb), 0)
        j = lax.broadcasted_iota(jnp.int32, (qb, qb), 1)
        sp = jnp.where(jnp.logical_and(j >= i, has_prev), _mxu(q, kp, "nt"), -jnp.inf)
        sc = jnp.where(j <= i, _mxu(q, kc, "nt"), -jnp.inf)
        m = lax.stop_gradient(jnp.maximum(jnp.max(sp, axis=-1, keepdims=True), jnp.max(sc, axis=-1, keepdims=True)))
        pp, pc = jnp.exp(sp - m), jnp.exp(sc - m)
        l = jnp.sum(pp, axis=-1, keepdims=True) + jnp.sum(pc, axis=-1, keepdims=True)
        o = _mxu(pp / l, vp, "nn") + _mxu(pc / l, vc, "nn")
        return o, jnp.broadcast_to(m + jnp.log(l), (qb, dh))

    blk = (None, None, qb, dh)
    cur = pl.BlockSpec(blk, lambda a, b: (a, b, 0, 0))
    prev = pl.BlockSpec(blk, lambda a, b: (a, jnp.maximum(b - 1, 0), 0, 0))
    shape = _sds((h, r, qb, dh))
    fwd_call, bwd_call = _block_op(name, f, (h, r), [cur, prev, cur, prev, cur], [(shape, cur), (shape, cur)],
                                   (q, k, k, v, v), (0, 1, 2, 3, 4), gdefs={1: (shape, cur), 3: (shape, cur)})

    def to_prev(g):
        return jnp.concatenate([g[:, 1:], jnp.zeros_like(g[:, :1])], axis=1)

    @jax.custom_vjp
    def op(q, k, v):
        return tuple(fwd_call(q, k, k, v, v))

    def op_b(res, cts):
        q, k, v = res
        dq, dkp, dkc, dvp, dvc = bwd_call(q, k, k, v, v, *cts)
        return dq, dkc + to_prev(dkp), dvc + to_prev(dvp)

    op.defvjp(lambda q, k, v: (tuple(fwd_call(q, k, k, v, v)), (q, k, v)), op_b)
    return op(q, k, v)


def dilated_attention(q, k, v, name):
    s = q.shape[0]
    outs, lses = [], []
    for d in DILATIONS:
        length = s // d
        nb = length // QB

        def to_residue(t):
            t = t.reshape(length, d, ATTN_HEADS, ATTN_HEAD_DIM).transpose(2, 1, 0, 3)
            return t.reshape(ATTN_HEADS, d * nb, QB, ATTN_HEAD_DIM)

        def from_residue(t):
            t = t.reshape(ATTN_HEADS, d, length, ATTN_HEAD_DIM).transpose(2, 1, 0, 3)
            return t.reshape(s, ATTN_WIDTH)

        o, lse = band_attention(to_residue(q), to_residue(k), to_residue(v), nb, f"{name}_d{d}")
        outs.append(from_residue(o))
        lses.append(from_residue(lse))
    return attn_merge(outs, lses, name + "_merge")


def cross_attention(q, kv, name):
    s = q.shape[0]
    m = kv.shape[0]
    width = XATTN_HEADS * XATTN_HEAD_DIM
    tq = 512

    def f(q, k, v):
        sc = _mxu(q, k, "nt") * (XATTN_HEAD_DIM ** -0.5)
        mx = lax.stop_gradient(jnp.max(sc, axis=-1, keepdims=True))
        p = jnp.exp(sc - mx)
        return (_mxu(p / jnp.sum(p, axis=-1, keepdims=True), v, "nn"),)

    q_spec = pl.BlockSpec((tq, XATTN_HEAD_DIM), lambda a, i: (i, a))
    k_spec = pl.BlockSpec((m, XATTN_HEAD_DIM), lambda a, i: (0, a))
    v_spec = pl.BlockSpec((m, XATTN_HEAD_DIM), lambda a, i: (0, a + XATTN_HEADS))
    half = _sds((m, width))
    fwd_call, bwd_call = _block_op(name, f, (XATTN_HEADS, s // tq), [q_spec, k_spec, v_spec], [(_sds((s, width)), q_spec)],
                                   (q, kv, kv), (0, 1, 2), acc={1: (1,), 2: (1,)}, gdefs={1: (half, k_spec), 2: (half, k_spec)})

    @jax.custom_vjp
    def op(q, kv):
        return fwd_call(q, kv, kv)[0]

    def op_b(res, ct):
        q, kv = res
        dq, dk, dv = bwd_call(q, kv, kv, ct)
        return dq, jnp.concatenate([dk, dv], axis=1)

    op.defvjp(lambda q, kv: (fwd_call(q, kv, kv)[0], (q, kv)), op_b)
    return op(q, kv)


def _hi(a, b):
    return jnp.dot(a, b, precision=HIGHEST, preferred_element_type=f32)


def _delta_chunk(q, k, v, g, beta, s0):
    c = q.shape[0]
    i = lax.broadcasted_iota(jnp.int32, (c, c), 0)
    j = lax.broadcasted_iota(jnp.int32, (c, c), 1)
    causal, strict = i >= j, i > j
    dec = _hi(causal.astype(f32), g)
    dec_i = dec[:, :c]
    rel = jnp.exp(jnp.where(causal, dec_i - dec_i.T, -jnp.inf))
    k_beta = k * beta
    a = jnp.where(strict, _mxu(k_beta, k, "nt") * rel, 0.0)
    eye = (i == j).astype(f32)
    inv = eye - a
    power = -a
    for _ in range(c.bit_length() - 2):
        power = _hi(power, power)
        inv = inv + _hi(inv, power)
    e_dec = jnp.exp(dec)
    u = _hi(inv, v * beta)
    w = _hi(inv, k_beta * e_dec)
    attn = jnp.where(causal, _mxu(q, k, "nt") * rel, 0.0)
    total = jnp.sum(g, axis=0, keepdims=True)
    v_new = u - _mxu(w, s0, "nn")
    o = _mxu(q * e_dec, s0, "nn") + _mxu(attn, v_new, "nn")
    s1 = s0 * jnp.exp(total) + _mxu(k * jnp.exp(total - dec), v_new, "tn")
    return o, s1


def gated_delta_rule(q, k, v, g, beta, name):
    s, width = q.shape
    c, dk = GDN_CHUNK, GDN_HEAD_DIM
    n = s // c
    heads = [slice(hd * dk, (hd + 1) * dk) for hd in range(width // dk)]

    def fwd_call(q, k, v, g, beta):
        def body(q_ref, k_ref, v_ref, g_ref, b_ref, o_ref, s_in_ref, state):
            @pl.when(pl.program_id(0) == 0)
            def _():
                state[...] = jnp.zeros_like(state)

            s_in_ref[...] = state[...]
            for hd in heads:
                o, s1 = _delta_chunk(q_ref[:, hd], k_ref[:, hd], v_ref[:, hd], g_ref[:, hd], b_ref[:, hd], state[:, hd])
                o_ref[:, hd] = o
                state[:, hd] = s1

        blk = pl.BlockSpec((c, width), lambda t: (t, 0))
        st = pl.BlockSpec((dk, width), lambda t: (t, 0))
        return pl.pallas_call(
            body, name=name + "_fwd", grid=(n,), in_specs=[blk] * 5, out_specs=[blk, st],
            out_shape=[_sds((s, width)), _sds((n * dk, width))], scratch_shapes=[pltpu.VMEM((dk, width), f32)],
            compiler_params=_params(1))(q, k, v, g, beta)

    def bwd_call(q, k, v, g, beta, s_in, do):
        def body(q_ref, k_ref, v_ref, g_ref, b_ref, s_ref, do_ref, dq_ref, dk_ref, dv_ref, dg_ref, db_ref, dstate):
            @pl.when(pl.program_id(0) == 0)
            def _():
                dstate[...] = jnp.zeros_like(dstate)

            for hd in heads:
                _, vjp = jax.vjp(_delta_chunk, q_ref[:, hd], k_ref[:, hd], v_ref[:, hd], g_ref[:, hd], b_ref[:, hd], s_ref[:, hd])
                dq, dk_, dv, dg, db, ds0 = vjp((do_ref[:, hd], dstate[:, hd]))
                dq_ref[:, hd], dk_ref[:, hd], dv_ref[:, hd], dg_ref[:, hd], db_ref[:, hd] = dq, dk_, dv, dg, db
                dstate[:, hd] = ds0

        blk = pl.BlockSpec((c, width), lambda t: (n - 1 - t, 0))
        st = pl.BlockSpec((dk, width), lambda t: (n - 1 - t, 0))
        return pl.pallas_call(
            body, name=name + "_bwd", grid=(n,), in_specs=[blk] * 5 + [st, blk], out_specs=[blk] * 5,
            out_shape=[_sds((s, width))] * 5, scratch_shapes=[pltpu.VMEM((dk, width), f32)],
            compiler_params=_params(1))(q, k, v, g, beta, s_in, do)

    @jax.custom_vjp
    def op(q, k, v, g, beta):
        return fwd_call(q, k, v, g, beta)[0]

    def op_f(q, k, v, g, beta):
        o, s_in = fwd_call(q, k, v, g, beta)
        return o, (q, k, v, g, beta, s_in)

    op.defvjp(op_f, lambda res, do: tuple(bwd_call(*res, do)))
    return op(q, k, v, g, beta)


def adamw(w, g, m, v, name):
    shape = w.shape
    cols = shape[-1]
    rows = w.size // cols
    tile = _pick(rows, (512, 256, 128, 64, 32, 16, 8))
    spec = pl.BlockSpec((tile, cols), lambda i: (i, 0))

    def body(w_ref, g_ref, m_ref, v_ref, d_ref, nm_ref, nv_ref):
        grad = g_ref[...]
        nm = ADAM_B1 * m_ref[...] + (1.0 - ADAM_B1) * grad
        nv = ADAM_B2 * v_ref[...] + (1.0 - ADAM_B2) * (grad * grad)
        m_hat = nm / (1.0 - ADAM_B1 ** ADAM_STEP)
        v_hat = nv / (1.0 - ADAM_B2 ** ADAM_STEP)
        d_ref[...] = -ADAM_LR * (m_hat / (jnp.sqrt(v_hat) + ADAM_EPS) + ADAM_WD * w_ref[...])
        nm_ref[...] = nm
        nv_ref[...] = nv

    outs = pl.pallas_call(
        body, name=name, grid=(rows // tile,), in_specs=[spec] * 4, out_specs=[spec] * 3,
        out_shape=[_sds((rows, cols))] * 3, compiler_params=_params(1),
    )(*[t.reshape(rows, cols) for t in (w, g, m, v)])
    return tuple(t.reshape(shape) for t in outs)


def _place():
    return lax.axis_index("x"), lax.axis_index("y"), lax.axis_index("c")


def _flip(p, bits):
    return tuple(1 - v if (bits >> s) & 1 else v for v, s in zip(p, (2, 1, 0)))


def _slot(p):
    return 4 * p[0] + 2 * p[1] + p[2]


def all_gather_rows(shard, name):
    rows, width = shard.shape

    def body(x_ref, out_ref, send_sems, recv_sems, local_sem):
        me = _place()
        sibling = _flip(me, 1)
        chips = [_flip(me, 4), _flip(me, 2), _flip(me, 6)]

        def copy(k, block, to, src=None):
            return pltpu.make_async_remote_copy(
                src_ref=out_ref.at[_slot(block)] if src is None else src, dst_ref=out_ref.at[_slot(block)],
                send_sem=send_sems.at[k], recv_sem=recv_sems.at[k], device_id=to, device_id_type=MESH)

        mine = pltpu.make_async_copy(x_ref, out_ref.at[_slot(me)], local_sem)
        mine.start()
        first = [copy(0, me, sibling, src=x_ref)] + [copy(1 + j, me, chip, src=x_ref) for j, chip in enumerate(chips)]
        for cp in first:
            cp.start()
        passed = [copy(4 + j, chip, sibling) for j, chip in enumerate(chips)]
        for j, chip in enumerate(chips):
            copy(1 + j, chip, me).wait_recv()
            passed[j].start()
        copy(0, sibling, me).wait_recv()
        for j, chip in enumerate(chips):
            copy(4 + j, _flip(chip, 1), me).wait_recv()
        for cp in first + passed:
            cp.wait_send()
        mine.wait()

    return pl.pallas_call(
        body, name=name, out_shape=jax.ShapeDtypeStruct((N_DEV, rows, width), shard.dtype),
        in_specs=[pl.BlockSpec(memory_space=pl.ANY)], out_specs=pl.BlockSpec(memory_space=pl.ANY),
        scratch_shapes=[pltpu.SemaphoreType.DMA((7,)), pltpu.SemaphoreType.DMA((7,)), pltpu.SemaphoreType.DMA],
    )(shard)


def all_to_all_rows(blocks, name):
    _, rows, width = blocks.shape

    def body(x_ref, out_ref, send_sems, recv_sems, local_sem):
        me = _place()
        mine = pltpu.make_async_copy(x_ref.at[_slot(me)], out_ref.at[_slot(me)], local_sem)
        mine.start()
        copies = []
        for k in range(1, N_DEV):
            peer = _flip(me, k)
            copies.append(pltpu.make_async_remote_copy(
                src_ref=x_ref.at[_slot(peer)], dst_ref=out_ref.at[_slot(me)], send_sem=send_sems.at[k - 1],
                recv_sem=recv_sems.at[k - 1], device_id=peer, device_id_type=MESH))
        for cp in copies:
            cp.start()
        for k in range(1, N_DEV):
            peer = _flip(me, k)
            pltpu.make_async_remote_copy(
                src_ref=x_ref.at[_slot(peer)], dst_ref=out_ref.at[_slot(peer)], send_sem=send_sems.at[k - 1],
                recv_sem=recv_sems.at[k - 1], device_id=peer, device_id_type=MESH).wait_recv()
        for cp in copies:
            cp.wait_send()
        mine.wait()

    return pl.pallas_call(
        body, name=name, out_shape=jax.ShapeDtypeStruct(blocks.shape, blocks.dtype),
        in_specs=[pl.BlockSpec(memory_space=pl.ANY)], out_specs=pl.BlockSpec(memory_space=pl.ANY),
        scratch_shapes=[pltpu.SemaphoreType.DMA((7,)), pltpu.SemaphoreType.DMA((7,)), pltpu.SemaphoreType.DMA],
    )(blocks)


def sum_slots(blocks, name):
    n, rows, width = blocks.shape
    tile = _pick(rows, (480, 256, 128, 64, 32, 16))

    def body(x_ref, o_ref):
        total = x_ref[0].astype(f32)
        for s in range(1, n):
            total = total + x_ref[s].astype(f32)
        o_ref[...] = total

    return pl.pallas_call(
        body, name=name, grid=(rows // tile,), in_specs=[pl.BlockSpec((n, tile, width), lambda i: (0, i, 0))],
        out_specs=pl.BlockSpec((tile, width), lambda i: (i, 0)), out_shape=_sds((rows, width)), compiler_params=_params(1))(blocks)


def all_reduce_small(x, name):
    rows, width = x.shape

    def body(x_ref, o_ref, land, send_sems, recv_sems):
        me = _place()
        copies = []
        for k in range(1, N_DEV):
            peer = _flip(me, k)
            copies.append(pltpu.make_async_remote_copy(
                src_ref=x_ref, dst_ref=land.at[_slot(me)], send_sem=send_sems.at[k - 1], recv_sem=recv_sems.at[k - 1],
                device_id=peer, device_id_type=MESH))
        for cp in copies:
            cp.start()
        land[_slot(me)] = x_ref[...]
        for k in range(1, N_DEV):
            peer = _flip(me, k)
            pltpu.make_async_remote_copy(
                src_ref=x_ref, dst_ref=land.at[_slot(peer)], send_sem=send_sems.at[k - 1], recv_sem=recv_sems.at[k - 1],
                device_id=peer, device_id_type=MESH).wait_recv()
        total = land[0]
        for s in range(1, N_DEV):
            total = total + land[s]
        o_ref[...] = total
        for cp in copies:
            cp.wait_send()

    return pl.pallas_call(
        body, name=name, out_shape=_sds((rows, width)), in_specs=[pl.BlockSpec(memory_space=pltpu.VMEM)],
        out_specs=pl.BlockSpec(memory_space=pltpu.VMEM),
        scratch_shapes=[pltpu.VMEM((N_DEV, rows, width), f32), pltpu.SemaphoreType.DMA((7,)), pltpu.SemaphoreType.DMA((7,))],
    )(x)


def _pack_big(shards):
    parts = [shards[name].astype(bf16).reshape(DEPTH, rows, D_MODEL) for name, rows in BIG]
    parts.append(jnp.zeros((DEPTH, BIG_ROWS - sum(r for _, r in BIG), D_MODEL), bf16))
    return jnp.concatenate(parts, axis=1).reshape(DEPTH * BIG_ROWS, D_MODEL)


def _unpack_gathered(gathered):
    gathered = gathered.reshape(N_DEV, DEPTH, BIG_ROWS, D_MODEL)
    full, at = {}, 0
    for name, rows in BIG:
        part = gathered[:, :, at:at + rows, :]
        at += rows
        if name in COL_SHARDED:
            part = part.reshape(N_DEV, DEPTH, D_MODEL, rows)
            full[name] = part.transpose(1, 2, 0, 3).reshape(DEPTH, D_MODEL, N_DEV * rows)
        else:
            full[name] = part.transpose(1, 0, 2, 3).reshape(DEPTH, N_DEV * rows, D_MODEL)
    w_in = full.pop("w_in")
    full["w_main"] = jnp.concatenate([w_in[:, :, :AB_AT], w_in[:, :, AB_AT + 2 * GDN_HEADS:]], axis=2)
    full["w_ab"] = jnp.pad(w_in[:, :, AB_AT:AB_AT + 2 * GDN_HEADS], ((0, 0), (0, 0), (0, LANES - 2 * GDN_HEADS)))
    return full


def _unpack_shard(flat):
    flat = flat.reshape(DEPTH, BIG_ROWS, D_MODEL)
    out, at = {}, 0
    for name, rows in BIG:
        part = flat[:, at:at + rows, :]
        at += rows
        out[name] = part.reshape(DEPTH, D_MODEL, rows) if name in COL_SHARDED else part
    return out


def _rows_of(flat_len):
    return -(-flat_len // (8 * D_MODEL)) * 8


def _pack_small(parts):
    flat = jnp.concatenate([p.reshape(-1) for p in parts])
    rows = _rows_of(flat.shape[0])
    flat = jnp.pad(flat, (0, rows * D_MODEL - flat.shape[0]))
    return flat.reshape(rows, D_MODEL)


def _unpack_small(packed, like):
    flat, out, at = packed.reshape(-1), [], 0
    for p in like:
        out.append(flat[at:at + p.size].reshape(p.shape))
        at += p.size
    return out


def _rope_tables(positions):
    inv_freq = jnp.float32(ROPE_THETA) ** (-jnp.arange(0, ROPE_DIM, 2, dtype=f32) / ROPE_DIM)
    ang = positions.astype(f32)[:, None] * inv_freq
    cos, sin = jnp.cos(ang), jnp.sin(ang)
    rest = ATTN_HEAD_DIM - ROPE_DIM
    cos_h = jnp.concatenate([cos, cos, jnp.ones((cos.shape[0], rest), f32)], axis=1)
    sin_h = jnp.concatenate([-sin, sin, jnp.zeros((sin.shape[0], rest), f32)], axis=1)
    return jnp.tile(cos_h, (1, ATTN_HEADS)), jnp.tile(sin_h, (1, ATTN_HEADS))


def _layer(h, p, mem, cos_t, sin_t):
    hn = rms_norm(h, p["norm_mix_pre"][None], "norm_mix_pre")
    proj = _linear(hn, p["w_main"], "w_main")
    ab = _linear(hn, p["w_ab"], "w_ab")
    aw, cw, gw = ATTN_WIDTH, CONV_WIDTH, GDN_WIDTH
    aq, ak, av = proj[:, :aw], proj[:, aw:2 * aw], proj[:, 2 * aw:3 * aw]
    at = 3 * aw
    cb, cc, cx = proj[:, at:at + cw], proj[:, at + cw:at + 2 * cw], proj[:, at + 2 * cw:at + 3 * cw]
    at += 3 * cw
    gqkv, gate = proj[:, at:at + 3 * gw], proj[:, at + 3 * gw:]
    y_attn = dilated_attention(rope(aq, cos_t, sin_t, ATTN_HEAD_DIM ** -0.5, "rope_q"), rope(ak, cos_t, sin_t, 1.0, "rope_k"),
                               av, "attn")
    y_conv = short_conv(cb, cc, cx, p["conv_short"], "short_conv")
    qkv = gdn_pre(gqkv, p["conv_gdn"], "gdn_pre")
    pv = jnp.zeros((8, LANES), f32).at[0, :GDN_HEADS].set(p["gdn_a_log"]).at[1, :GDN_HEADS].set(p["gdn_dt_bias"])
    g, beta = gate_beta(ab, pv, "gate_beta")
    o = gated_delta_rule(qkv[:, :gw], qkv[:, gw:2 * gw], qkv[:, 2 * gw:], g, beta, "delta_rule")
    y_gdn = gdn_post(o, gate, p["gdn_norm"][None], "gdn_post")
    mix = _linear(jnp.concatenate([y_attn, y_conv, y_gdn], axis=1), p["w_out"], "w_out")
    h = add_norm(h, mix, p["norm_mix_post"][None], "norm_mix_post")

    hn = rms_norm(h, p["norm_xattn_pre"][None], "norm_xattn_pre")
    qx = _linear(hn, p["w_xq"], "w_xq")
    kv = _linear(rms_norm(mem, p["norm_mem"][None], "norm_mem"), p["w_xkv"], "w_xkv")
    xa = _linear(cross_attention(qx, kv, "xattn"), p["w_xo"], "w_xo")
    h = add_norm(h, xa, p["norm_xattn_post"][None], "norm_xattn_post")

    hn = rms_norm(h, p["norm_ffn_pre"][None], "norm_ffn_pre")
    act = swiglu(_linear(hn, p["w_gate_up"], "w_gate_up"), "swiglu")
    return add_norm(h, _linear(act, p["w_down"], "w_down"), p["norm_ffn_post"][None], "norm_ffn_post")


def _local_loss(x, big, small, mem, cos_t, sin_t, target):
    def step(h, p):
        return _layer(h, p, mem, cos_t, sin_t), None

    h, _ = lax.scan(step, x, {**big, **small})
    return loss_rows(h, target, "loss")


def kernel(x, mem, positions, norm_mix_pre, norm_mix_post, w_in, conv_short, conv_gdn, gdn_a_log, gdn_dt_bias, gdn_norm, w_out, norm_mem, norm_xattn_pre, norm_xattn_post, w_xq, w_xkv, w_xo, norm_ffn_pre, norm_ffn_post, w_gate_up, w_down, loss_target, m_norm_mix_pre, m_norm_mix_post, m_w_in, m_conv_short, m_conv_gdn, m_gdn_a_log, m_gdn_dt_bias, m_gdn_norm, m_w_out, m_norm_mem, m_norm_xattn_pre, m_norm_xattn_post, m_w_xq, m_w_xkv, m_w_xo, m_norm_ffn_pre, m_norm_ffn_post, m_w_gate_up, m_w_down, v_norm_mix_pre, v_norm_mix_post, v_w_in, v_conv_short, v_conv_gdn, v_gdn_a_log, v_gdn_dt_bias, v_gdn_norm, v_w_out, v_norm_mem, v_norm_xattn_pre, v_norm_xattn_post, v_w_xq, v_w_xkv, v_w_xo, v_norm_ffn_pre, v_norm_ffn_post, v_w_gate_up, v_w_down):
    given = dict(locals())
    weights = {n: given[n] for n in WEIGHTS}
    me = _slot(_place())

    big = _unpack_gathered(all_gather_rows(_pack_big(weights), "gather_weights"))

    def in_place(shard):
        full = jnp.zeros(shard.shape[:-1] + (shard.shape[-1] * N_DEV,), f32)
        return lax.dynamic_update_slice_in_dim(full, shard, me * shard.shape[-1], axis=shard.ndim - 1)

    placed = [in_place(conv_short), in_place(conv_gdn)]
    conv_short_full, conv_gdn_full = _unpack_small(all_reduce_small(_pack_small(placed), "gather_conv"), placed)
    small = {n: weights[n] for n in NORMS + ("gdn_a_log", "gdn_dt_bias", "gdn_norm")}
    small["conv_short"], small["conv_gdn"] = conv_short_full, conv_gdn_full

    cos_t, sin_t = _rope_tables(positions[0])
    loss, (grad_x, grad_big, grad_small) = jax.value_and_grad(_local_loss, argnums=(0, 1, 2))(
        x[0], big, small, mem[0], cos_t, sin_t, loss_target[0])

    _, regather = jax.vjp(_unpack_gathered, jnp.zeros((N_DEV, DEPTH * BIG_ROWS, D_MODEL), bf16))
    grad_blocks = regather(grad_big)[0]
    grads = _unpack_shard(sum_slots(all_to_all_rows(grad_blocks, "exchange_grads"), "sum_grads"))

    names = sorted(grad_small)
    parts = [grad_small[n] for n in names] + [loss.reshape(1)]
    reduced = _unpack_small(all_reduce_small(_pack_small(parts), "reduce_small"), parts)
    loss = reduced[-1][0]
    for n, g in zip(names, reduced[:-1]):
        if n in ("conv_short", "conv_gdn"):
            width = weights[n].shape[-1]
            g = lax.dynamic_slice_in_dim(g, me * width, width, axis=g.ndim - 1)
        grads[n] = g

    delta, new_m, new_v = {}, {}, {}
    for n in WEIGHTS:
        delta[n], new_m[n], new_v[n] = adamw(weights[n], grads[n], given["m_" + n], given["v_" + n], "adamw_" + n)
    return (loss, grad_x[None], *[grads[n] for n in WEIGHTS], *[delta[n] for n in WEIGHTS],
            *[new_m[n] for n in WEIGHTS], *[new_v[n] for n in WEIGHTS])
```

```python
import functools

import jax
import jax.numpy as jnp
from jax import lax
from jax.experimental import pallas as pl
from jax.experimental.pallas import tpu as pltpu

f32 = jnp.float32
bf16 = jnp.bfloat16
HIGHEST = lax.Precision.HIGHEST
MESH = pl.DeviceIdType.MESH

N_DEV = 8
DEPTH = 4
D_MODEL = 1024
EPS = 1e-6
ATTN_HEADS, ATTN_HEAD_DIM = 4, 64
ATTN_WIDTH = ATTN_HEADS * ATTN_HEAD_DIM
DILATIONS = (1, 4, 16)
QB = 128
ROPE_THETA = 500000.0
ROPE_DIM = ATTN_HEAD_DIM // 4
CONV_WIDTH = 256
GDN_HEADS, GDN_HEAD_DIM = 4, 128
GDN_WIDTH = GDN_HEADS * GDN_HEAD_DIM
GDN_CHUNK = 64
XATTN_HEADS, XATTN_HEAD_DIM = 4, 256
FFN_HIDDEN = 2816
IN_WIDTH = 3592
AB_AT = 3 * ATTN_WIDTH + 3 * CONV_WIDTH + 3 * GDN_WIDTH
MAIN_WIDTH = IN_WIDTH - 2 * GDN_HEADS
LANES = 128
ROW_TILE = 256
VMEM_LIMIT = 56 * 1024 * 1024

ADAM_LR, ADAM_B1, ADAM_B2, ADAM_EPS, ADAM_WD, ADAM_STEP = 0.001, 0.9, 0.999, 1e-08, 0.01, 10

BIG = (("w_in", 449), ("w_out", 128), ("w_xq", 128), ("w_xkv", 256), ("w_xo", 128), ("w_gate_up", 704), ("w_down", 352))
BIG_ROWS = 2160
COL_SHARDED = ("w_in", "w_xkv", "w_gate_up")
NORMS = ("norm_mix_pre", "norm_mix_post", "norm_mem", "norm_xattn_pre", "norm_xattn_post", "norm_ffn_pre", "norm_ffn_post")
WEIGHTS = ("norm_mix_pre", "norm_mix_post", "w_in", "conv_short", "conv_gdn", "gdn_a_log", "gdn_dt_bias", "gdn_norm", "w_out",
           "norm_mem", "norm_xattn_pre", "norm_xattn_post", "w_xq", "w_xkv", "w_xo", "norm_ffn_pre", "norm_ffn_post",
           "w_gate_up", "w_down")


def _params(n_grid):
    return pltpu.CompilerParams(dimension_semantics=("arbitrary",) * n_grid, vmem_limit_bytes=VMEM_LIMIT)


def _pick(n, cands):
    for c in cands:
        if n % c == 0:
            return c
    return n


def _mm(a, b, ta, tb, out_dtype, name):
    m, k = (a.shape[1], a.shape[0]) if ta else a.shape
    n = b.shape[0] if tb else b.shape[1]
    tm, tn = _pick(m, (1024, 512, 256, 128)), _pick(n, (1024, 512, 256, 128))
    tk = k if k <= 1024 else _pick(k, (1024, 512, 256, 128))
    nk = k // tk
    a_spec = pl.BlockSpec((tk, tm), lambda i, j, kk: (kk, i)) if ta else pl.BlockSpec((tm, tk), lambda i, j, kk: (i, kk))
    b_spec = pl.BlockSpec((tn, tk), lambda i, j, kk: (j, kk)) if tb else pl.BlockSpec((tk, tn), lambda i, j, kk: (kk, j))
    dims = (((0 if ta else 1,), (1 if tb else 0,)), ((), ()))

    def body(a_ref, b_ref, o_ref, acc_ref):
        kk = pl.program_id(2)
        p = lax.dot_general(a_ref[...].astype(bf16), b_ref[...].astype(bf16), dims, preferred_element_type=f32)

        @pl.when(kk == 0)
        def _():
            acc_ref[...] = p

        @pl.when(kk > 0)
        def _():
            acc_ref[...] += p

        @pl.when(kk == nk - 1)
        def _():
            o_ref[...] = acc_ref[...].astype(o_ref.dtype)

    return pl.pallas_call(
        body, name=name, grid=(m // tm, n // tn, nk), in_specs=[a_spec, b_spec],
        out_specs=pl.BlockSpec((tm, tn), lambda i, j, kk: (i, j)), out_shape=jax.ShapeDtypeStruct((m, n), out_dtype),
        scratch_shapes=[pltpu.VMEM((tm, tn), f32)], compiler_params=_params(3))(a, b)


def _linear(x, w, name):
    @jax.custom_vjp
    def lin(x, w):
        return _mm(x, w, False, False, f32, name + "_y")

    def lin_f(x, w):
        return _mm(x, w, False, False, f32, name + "_y"), (x, w)

    def lin_b(res, dy):
        x, w = res
        return _mm(dy, w, False, True, f32, name + "_dx"), _mm(x, dy, True, False, bf16, name + "_dw")

    lin.defvjp(lin_f, lin_b)
    return lin(x, w)


def _split_cols(x, widths):
    edges = [sum(widths[:i]) for i in range(len(widths) + 1)]

    def cut(x):
        return tuple(x[:, a:b] for a, b in zip(edges[:-1], edges[1:]))

    @jax.custom_vjp
    def split(x):
        return cut(x)

    split.defvjp(lambda x: (cut(x), None), lambda _, cts: (jnp.concatenate(cts, axis=1),))
    return split(x)


def _block_op(name, f, grid, in_specs, out_defs, arrays, diff, acc=None, gdefs=None):
    acc, gdefs = acc or {}, gdefs or {}
    n_in, n_out, n_grid = len(in_specs), len(out_defs), len(grid)

    def fwd_call(*xs):
        def body(*refs):
            outs = f(*[r[...] for r in refs[:n_in]])
            for r, o in zip(refs[n_in:], outs):
                r[...] = o.astype(r.dtype)

        return pl.pallas_call(
            body, name=name + "_fwd", grid=grid, in_specs=in_specs, out_specs=[d[1] for d in out_defs],
            out_shape=[d[0] for d in out_defs], compiler_params=_params(n_grid))(*xs)

    def bwd_call(*xs_and_cts):
        def body(*refs):
            xs = [r[...] for r in refs[:n_in]]
            cts = tuple(r[...] for r in refs[n_in:n_in + n_out])

            def of_diff(*dx):
                full = list(xs)
                for i, v in zip(diff, dx):
                    full[i] = v
                return tuple(f(*full))

            _, vjp = jax.vjp(of_diff, *[xs[i] for i in diff])
            grads = vjp(cts)
            for i, g, r in zip(diff, grads, refs[n_in + n_out:]):
                if i in acc:
                    first = functools.reduce(jnp.logical_and, [pl.program_id(a) == 0 for a in acc[i]])

                    @pl.when(first)
                    def _(r=r):
                        r[...] = jnp.zeros_like(r)

                    r[...] += g.astype(r.dtype)
                else:
                    r[...] = g.astype(r.dtype)

        g_defs = [gdefs.get(i, (jax.ShapeDtypeStruct(arrays[i].shape, f32), in_specs[i])) for i in diff]
        return pl.pallas_call(
            body, name=name + "_bwd", grid=grid, in_specs=list(in_specs) + [d[1] for d in out_defs],
            out_specs=[d[1] for d in g_defs], out_shape=[d[0] for d in g_defs], compiler_params=_params(n_grid))(*xs_and_cts)

    return fwd_call, bwd_call


def _simple_op(name, f, grid, in_specs, out_defs, arrays, diff, acc=None):
    fwd_call, bwd_call = _block_op(name, f, grid, in_specs, out_defs, arrays, diff, acc)

    @jax.custom_vjp
    def op(*xs):
        return tuple(fwd_call(*xs))

    def op_f(*xs):
        return tuple(fwd_call(*xs)), xs

    def op_b(xs, cts):
        grads = bwd_call(*xs, *cts)
        out = [jnp.zeros_like(x) for x in xs]
        for i, g in zip(diff, grads):
            out[i] = g
        return tuple(out)

    op.defvjp(op_f, op_b)
    return op(*arrays)


def _rows(width, tile=ROW_TILE):
    return pl.BlockSpec((tile, width), lambda i: (i, 0))


def _whole(shape):
    return pl.BlockSpec(shape, lambda *_: (0,) * len(shape))


def _sds(shape):
    return jax.ShapeDtypeStruct(shape, f32)


def _rms(x, w):
    return x * lax.rsqrt(jnp.mean(x * x, axis=-1, keepdims=True) + EPS) * w


def rms_norm(x, w, name):
    r, d = x.shape
    return _simple_op(name, lambda x, w: (_rms(x, w),), (r // ROW_TILE,), [_rows(d), _whole((1, d))],
                      [(_sds((r, d)), _rows(d))], (x, w), (0, 1), {1: (0,)})[0]


def add_norm(h, y, w, name):
    r, d = h.shape
    return _simple_op(name, lambda h, y, w: (h + _rms(y, w),), (r // ROW_TILE,), [_rows(d), _rows(d), _whole((1, d))],
                      [(_sds((r, d)), _rows(d))], (h, y, w), (0, 1, 2), {2: (0,)})[0]


def _swap8(x):
    def raw(x):
        lane = lax.broadcasted_iota(jnp.int32, x.shape, 1) % ATTN_HEAD_DIM
        half = ROPE_DIM // 2
        up = pltpu.roll(x, x.shape[1] - half, axis=1)
        down = pltpu.roll(x, half, axis=1)
        return jnp.where(lane < half, up, jnp.where(lane < ROPE_DIM, down, 0.0))

    @jax.custom_vjp
    def swap(x):
        return raw(x)

    swap.defvjp(lambda x: (raw(x), None), lambda _, g: (raw(g),))
    return swap(x)


def rope(x, cos_t, sin_t, scale, name):
    r, d = x.shape
    return _simple_op(name, lambda x, c, s: ((x * c + _swap8(x) * s) * scale,), (r // ROW_TILE,), [_rows(d)] * 3,
                      [(_sds((r, d)), _rows(d))], (x, cos_t, sin_t), (0,))[0]


def _shift_rows(x, k):
    n = x.shape[0]

    def down(x):
        row = lax.broadcasted_iota(jnp.int32, x.shape, 0)
        return jnp.where(row >= k, pltpu.roll(x, k, axis=0), 0.0)

    def up(x):
        row = lax.broadcasted_iota(jnp.int32, x.shape, 0)
        return jnp.where(row < n - k, pltpu.roll(x, n - k, axis=0), 0.0)

    @jax.custom_vjp
    def shift(x):
        return down(x)

    shift.defvjp(lambda x: (down(x), None), lambda _, g: (up(g),))
    return shift(x)


def _causal_conv(x, w):
    taps = w.shape[0]
    y = x * w[taps - 1:taps, :]
    for j in range(taps - 1):
        y = y + _shift_rows(x, taps - 1 - j) * w[j:j + 1, :]
    return y


def _cols(rows, at=0):
    return pl.BlockSpec((rows, LANES), lambda j: (0, at + j))


def short_conv(cb, cc, cx, w, name):
    s, c = cb.shape
    taps = w.shape[0]
    return _simple_op(name, lambda b, c_, x, w: (b * _causal_conv(c_ * x, w),), (c // LANES,),
                      [_cols(s)] * 3 + [_cols(taps)], [(_sds((s, c)), _cols(s))], (cb, cc, cx, w), (0, 1, 2, 3))[0]


def gdn_pre(qkv, w, name):
    s, c = qkv.shape
    taps = w.shape[0]

    def f(x, w):
        j = pl.program_id(0)
        y = jax.nn.silu(_causal_conv(x, w))
        normed = y * lax.rsqrt(jnp.sum(y * y, axis=-1, keepdims=True) + EPS)
        scale = jnp.where(j < GDN_HEADS, GDN_HEAD_DIM ** -0.5, 1.0).astype(f32)
        return (jnp.where(j < 2 * GDN_HEADS, normed * scale, y),)

    return _simple_op(name, f, (c // LANES,), [_cols(s), _cols(taps)], [(_sds((s, c)), _cols(s))], (qkv, w), (0, 1))[0]


def gate_beta(ab, pv, name):
    s = ab.shape[0]

    def f(ab, pv):
        row = lax.broadcasted_iota(jnp.int32, (LANES, GDN_WIDTH), 0)
        head = lax.broadcasted_iota(jnp.int32, (LANES, GDN_WIDTH), 1) // GDN_HEAD_DIM
        spread_a = (row == head).astype(f32)
        spread_b = (row == head + GDN_HEADS).astype(f32)
        a = jnp.dot(ab, spread_a, precision=HIGHEST, preferred_element_type=f32)
        b = jnp.dot(ab, spread_b, precision=HIGHEST, preferred_element_type=f32)
        p = jnp.dot(pv, spread_a, precision=HIGHEST, preferred_element_type=f32)
        g = -jnp.exp(p[0:1, :]) * jax.nn.softplus(a + p[1:2, :])
        return g, jax.nn.sigmoid(b)

    outs = [(_sds((s, GDN_WIDTH)), _rows(GDN_WIDTH))] * 2
    return _simple_op(name, f, (s // ROW_TILE,), [_rows(LANES), _whole((8, LANES))], outs, (ab, pv), (0, 1), {1: (0,)})


def gdn_post(o, gate, w, name):
    s, c = o.shape
    spec = pl.BlockSpec((ROW_TILE, LANES), lambda i, j: (i, j))
    return _simple_op(name, lambda o, g, w: (_rms(o, w) * jax.nn.silu(g),), (s // ROW_TILE, c // LANES),
                      [spec, spec, _whole((1, LANES))], [(_sds((s, c)), spec)], (o, gate, w), (0, 1, 2), {2: (0, 1)})[0]


def swiglu(gu, name):
    s, two_f = gu.shape
    hidden = two_f // 2
    tile = 256
    nb = hidden // tile
    g_spec = pl.BlockSpec((ROW_TILE, tile), lambda i, j: (i, j))
    u_spec = pl.BlockSpec((ROW_TILE, tile), lambda i, j: (i, j + nb))
    fwd_call, bwd_call = _block_op(name, lambda g, u: (jax.nn.silu(g) * u,), (s // ROW_TILE, nb), [g_spec, u_spec],
                                   [(_sds((s, hidden)), g_spec)], (gu, gu), (0, 1),
                                   gdefs={0: (_sds((s, hidden)), g_spec), 1: (_sds((s, hidden)), g_spec)})

    @jax.custom_vjp
    def op(gu):
        return fwd_call(gu, gu)[0]

    def op_b(gu, ct):
        dg, du = bwd_call(gu, gu, ct)
        return (jnp.concatenate([dg, du], axis=1),)

    op.defvjp(lambda gu: (fwd_call(gu, gu)[0], gu), op_b)
    return op(gu)


def attn_merge(outs, lses, name):
    s, c = outs[0].shape

    def f(o1, o2, o3, l1, l2, l3):
        m = lax.stop_gradient(jnp.maximum(jnp.maximum(l1, l2), l3))
        e1, e2, e3 = jnp.exp(l1 - m), jnp.exp(l2 - m), jnp.exp(l3 - m)
        return ((e1 * o1 + e2 * o2 + e3 * o3) / (e1 + e2 + e3),)

    return _simple_op(name, f, (s // ROW_TILE,), [_rows(c)] * 6, [(_sds((s, c)), _rows(c))], (*outs, *lses), tuple(range(6)))[0]


def loss_rows(y, target, name):
    s, d = y.shape
    nt = s // ROW_TILE

    def f(y, t):
        e = y - t
        part = 0.5 * jnp.sum(jnp.mean(e * e, axis=-1, keepdims=True), axis=0, keepdims=True)
        return (jnp.broadcast_to(part * (1.0 / (8 * LANES)), (8, LANES)),)

    out = _simple_op(name, f, (nt,), [_rows(d)] * 2, [(_sds((nt * 8, LANES)), pl.BlockSpec((8, LANES), lambda i: (i, 0)))],
                     (y, target), (0,))[0]
    return jnp.sum(out)


def _mxu(a, b, form):
    dims = {"nn": ((1,), (0,)), "nt": ((1,), (1,)), "tn": ((0,), (0,))}

    def raw(a, b, form):
        return lax.dot_general(a.astype(bf16), b.astype(bf16), (dims[form], ((), ())), preferred_element_type=f32)

    @jax.custom_vjp
    def prod(a, b):
        return raw(a, b, form)

    def prod_b(res, ct):
        a, b = res
        if form == "nn":
            return raw(ct, b, "nt"), raw(a, ct, "tn")
        if form == "nt":
            return raw(ct, b, "nn"), raw(ct, a, "tn")
        return raw(b, ct, "nt"), raw(a, ct, "nn")

    prod.defvjp(lambda a, b: (raw(a, b, form), (a, b)), prod_b)
    return prod(a, b)


def band_attention(q, k, v, nb, name):
    r, qb, width = q.shape
    dh = ATTN_HEAD_DIM

    def f(q, kp, kc, vp, vc):
        has_prev = (pl.program_id(0) % nb) > 0
        i = lax.broadcasted_iota(jnp.int32, (qb, qb), 0)
        j = lax.broadcasted_iota(jnp.int32, (qb, qb), 1)
        see_prev, see_cur = jnp.logical_and(j >= i, has_prev), j <= i
        outs, lses = [], []
        for hd in range(width // dh):
            at = slice(hd * dh, (hd + 1) * dh)
            sp = jnp.where(see_prev, _mxu(q[:, at], kp[:, at], "nt"), -jnp.inf)
            sc = jnp.where(see_cur, _mxu(q[:, at], kc[:, at], "nt"), -jnp.inf)
            m = lax.stop_gradient(jnp.maximum(jnp.max(sp, axis=-1, keepdims=True), jnp.max(sc, axis=-1, keepdims=True)))
            pp, pc = jnp.exp(sp - m), jnp.exp(sc - m)
            l = jnp.sum(pp, axis=-1, keepdims=True) + jnp.sum(pc, axis=-1, keepdims=True)
            outs.append(_mxu(pp / l, vp[:, at], "nn") + _mxu(pc / l, vc[:, at], "nn"))
            lses.append(jnp.broadcast_to(m + jnp.log(l), (qb, dh)))
        return jnp.concatenate(outs, axis=1), jnp.concatenate(lses, axis=1)

    blk = (None, qb, width)
    cur = pl.BlockSpec(blk, lambda b: (b, 0, 0))
    prev = pl.BlockSpec(blk, lambda b: (jnp.maximum(b - 1, 0), 0, 0))
    shape = _sds((r, qb, width))
    fwd_call, bwd_call = _block_op(name, f, (r,), [cur, prev, cur, prev, cur], [(shape, cur), (shape, cur)],
                                   (q, k, k, v, v), (0, 1, 2, 3, 4), gdefs={1: (shape, cur), 3: (shape, cur)})

    def to_prev(g):
        return jnp.concatenate([g[1:], jnp.zeros_like(g[:1])], axis=0)

    @jax.custom_vjp
    def op(q, k, v):
        return tuple(fwd_call(q, k, k, v, v))

    def op_b(res, cts):
        q, k, v = res
        dq, dkp, dkc, dvp, dvc = bwd_call(q, k, k, v, v, *cts)
        return dq, dkc + to_prev(dkp), dvc + to_prev(dvp)

    op.defvjp(lambda q, k, v: (tuple(fwd_call(q, k, k, v, v)), (q, k, v)), op_b)
    return op(q, k, v)


def dilated_attention(q, k, v, name):
    s = q.shape[0]
    outs, lses = [], []
    for d in DILATIONS:
        length = s // d
        nb = length // QB

        def to_residue(t):
            return t.reshape(length, d, ATTN_WIDTH).transpose(1, 0, 2).reshape(d * nb, QB, ATTN_WIDTH)

        def from_residue(t):
            return t.reshape(d, length, ATTN_WIDTH).transpose(1, 0, 2).reshape(s, ATTN_WIDTH)

        o, lse = band_attention(to_residue(q), to_residue(k), to_residue(v), nb, f"{name}_d{d}")
        outs.append(from_residue(o))
        lses.append(from_residue(lse))
    return attn_merge(outs, lses, name + "_merge")


def cross_attention(q, kv, name):
    s = q.shape[0]
    m = kv.shape[0]
    width = XATTN_HEADS * XATTN_HEAD_DIM
    tq = 512

    def f(q, k, v):
        sc = _mxu(q, k, "nt") * (XATTN_HEAD_DIM ** -0.5)
        mx = lax.stop_gradient(jnp.max(sc, axis=-1, keepdims=True))
        p = jnp.exp(sc - mx)
        return (_mxu(p / jnp.sum(p, axis=-1, keepdims=True), v, "nn"),)

    q_spec = pl.BlockSpec((tq, XATTN_HEAD_DIM), lambda a, i: (i, a))
    k_spec = pl.BlockSpec((m, XATTN_HEAD_DIM), lambda a, i: (0, a))
    v_spec = pl.BlockSpec((m, XATTN_HEAD_DIM), lambda a, i: (0, a + XATTN_HEADS))
    half = _sds((m, width))
    fwd_call, bwd_call = _block_op(name, f, (XATTN_HEADS, s // tq), [q_spec, k_spec, v_spec], [(_sds((s, width)), q_spec)],
                                   (q, kv, kv), (0, 1, 2), acc={1: (1,), 2: (1,)}, gdefs={1: (half, k_spec), 2: (half, k_spec)})

    @jax.custom_vjp
    def op(q, kv):
        return fwd_call(q, kv, kv)[0]

    def op_b(res, ct):
        q, kv = res
        dq, dk, dv = bwd_call(q, kv, kv, ct)
        return dq, jnp.concatenate([dk, dv], axis=1)

    op.defvjp(lambda q, kv: (fwd_call(q, kv, kv)[0], (q, kv)), op_b)
    return op(q, kv)


def _hi(a, b, form="nn"):
    dims = {"nn": ((1,), (0,)), "nt": ((1,), (1,)), "tn": ((0,), (0,))}[form]
    return lax.dot_general(a, b, (dims, ((), ())), precision=lax.Precision.HIGH, preferred_element_type=f32)


def _running_sum(g):
    def raw(x, form):
        c = x.shape[0]
        tri = (lax.broadcasted_iota(jnp.int32, (c, c), 0) >= lax.broadcasted_iota(jnp.int32, (c, c), 1)).astype(bf16)
        hi = x.astype(bf16)
        rest = x - hi.astype(f32)
        mid = rest.astype(bf16)
        low = (rest - mid.astype(f32)).astype(bf16)
        dims = (((1,) if form == "nn" else (0,), (0,)), ((), ()))
        return sum(lax.dot_general(tri, part, dims, preferred_element_type=f32) for part in (hi, mid, low))

    @jax.custom_vjp
    def run(x):
        return raw(x, "nn")

    run.defvjp(lambda x: (raw(x, "nn"), None), lambda _, ct: (raw(ct, "tn"),))
    return run(g)


def _unit_lower_inverse(a):
    c = a.shape[0]
    eye = (lax.broadcasted_iota(jnp.int32, (c, c), 0) == lax.broadcasted_iota(jnp.int32, (c, c), 1)).astype(f32)
    inv, power = eye - a, -a
    for _ in range(c.bit_length() - 2):
        power = _hi(power, power)
        inv = inv + _hi(inv, power)
    return inv


def _known_inverse(a, t):
    @jax.custom_vjp
    def inv(a, t):
        return t

    def inv_b(t, ct):
        return -_hi(_hi(t, ct, "tn"), t, "nt"), jnp.zeros_like(t)

    inv.defvjp(lambda a, t: (t, t), inv_b)
    return inv(a, t)


def _delta_chunk(q, k, v, g, beta, s0, known_inv=None):
    c = q.shape[0]
    i = lax.broadcasted_iota(jnp.int32, (c, c), 0)
    j = lax.broadcasted_iota(jnp.int32, (c, c), 1)
    causal, strict = i >= j, i > j
    dec = _running_sum(g)
    dec_i = dec[:, :c]
    rel = jnp.exp(jnp.where(causal, dec_i - dec_i.T, -jnp.inf))
    k_beta = k * beta
    a = jnp.where(strict, _mxu(k_beta, k, "nt") * rel, 0.0)
    inv = _unit_lower_inverse(a) if known_inv is None else _known_inverse(a, known_inv)
    e_dec = jnp.exp(dec)
    u = _hi(inv, v * beta)
    w = _hi(inv, k_beta * e_dec)
    attn = jnp.where(causal, _mxu(q, k, "nt") * rel, 0.0)
    total = jnp.sum(g, axis=0, keepdims=True)
    v_new = u - _mxu(w, s0, "nn")
    o = _mxu(q * e_dec, s0, "nn") + _mxu(attn, v_new, "nn")
    s1 = s0 * jnp.exp(total) + _mxu(k * jnp.exp(total - dec), v_new, "tn")
    return o, s1, inv


def gated_delta_rule(q, k, v, g, beta, name):
    s, width = q.shape
    c, dk = GDN_CHUNK, GDN_HEAD_DIM
    n, n_heads = s // c, width // dk
    heads = [slice(hd * dk, (hd + 1) * dk) for hd in range(n_heads)]
    inv_at = [slice(hd * c, (hd + 1) * c) for hd in range(n_heads)]

    def fwd_call(q, k, v, g, beta):
        def body(q_ref, k_ref, v_ref, g_ref, b_ref, o_ref, s_in_ref, inv_ref, state):
            @pl.when(pl.program_id(0) == 0)
            def _():
                state[...] = jnp.zeros_like(state)

            s_in_ref[...] = state[...]
            ins = [[r[:, hd] for r in (q_ref, k_ref, v_ref, g_ref, b_ref, state)] for hd in heads]
            outs = [_delta_chunk(*x) for x in ins]
            for hd, at, (o, s1, inv) in zip(heads, inv_at, outs):
                o_ref[:, hd], state[:, hd], inv_ref[:, at] = o, s1, inv

        blk = pl.BlockSpec((c, width), lambda t: (t, 0))
        st = pl.BlockSpec((dk, width), lambda t: (t, 0))
        iv = pl.BlockSpec((c, n_heads * c), lambda t: (t, 0))
        return pl.pallas_call(
            body, name=name + "_fwd", grid=(n,), in_specs=[blk] * 5, out_specs=[blk, st, iv],
            out_shape=[_sds((s, width)), _sds((n * dk, width)), _sds((s, n_heads * c))],
            scratch_shapes=[pltpu.VMEM((dk, width), f32)], compiler_params=_params(1))(q, k, v, g, beta)

    def bwd_call(q, k, v, g, beta, s_in, inv, do):
        def body(q_ref, k_ref, v_ref, g_ref, b_ref, s_ref, inv_ref, do_ref, dq_ref, dk_ref, dv_ref, dg_ref, db_ref, dstate):
            @pl.when(pl.program_id(0) == 0)
            def _():
                dstate[...] = jnp.zeros_like(dstate)

            ins = [[r[:, hd] for r in (q_ref, k_ref, v_ref, g_ref, b_ref, s_ref)] for hd in heads]
            known = [inv_ref[:, at] for at in inv_at]
            cts = [(do_ref[:, hd], dstate[:, hd]) for hd in heads]
            grads = []
            for x, t, ct in zip(ins, known, cts):
                _, vjp = jax.vjp(lambda *y, t=t: _delta_chunk(*y, known_inv=t)[:2], *x)
                grads.append(vjp(ct))
            for hd, (dq, dk_, dv, dg, db, ds0) in zip(heads, grads):
                dq_ref[:, hd], dk_ref[:, hd], dv_ref[:, hd], dg_ref[:, hd], db_ref[:, hd], dstate[:, hd] = dq, dk_, dv, dg, db, ds0

        blk = pl.BlockSpec((c, width), lambda t: (n - 1 - t, 0))
        st = pl.BlockSpec((dk, width), lambda t: (n - 1 - t, 0))
        iv = pl.BlockSpec((c, n_heads * c), lambda t: (n - 1 - t, 0))
        return pl.pallas_call(
            body, name=name + "_bwd", grid=(n,), in_specs=[blk] * 5 + [st, iv, blk], out_specs=[blk] * 5,
            out_shape=[_sds((s, width))] * 5, scratch_shapes=[pltpu.VMEM((dk, width), f32)],
            compiler_params=_params(1))(q, k, v, g, beta, s_in, inv, do)

    @jax.custom_vjp
    def op(q, k, v, g, beta):
        return fwd_call(q, k, v, g, beta)[0]

    def op_f(q, k, v, g, beta):
        o, s_in, inv = fwd_call(q, k, v, g, beta)
        return o, (q, k, v, g, beta, s_in, inv)

    op.defvjp(op_f, lambda res, do: tuple(bwd_call(*res, do)))
    return op(q, k, v, g, beta)


def adamw(w, g, m, v, name):
    shape = w.shape
    cols = shape[-1]
    rows = w.size // cols
    tile = _pick(rows, (512, 256, 128, 64, 32, 16, 8))
    spec = pl.BlockSpec((tile, cols), lambda i: (i, 0))

    def body(w_ref, g_ref, m_ref, v_ref, d_ref, nm_ref, nv_ref):
        grad = g_ref[...]
        nm = ADAM_B1 * m_ref[...] + (1.0 - ADAM_B1) * grad
        nv = ADAM_B2 * v_ref[...] + (1.0 - ADAM_B2) * (grad * grad)
        m_hat = nm / (1.0 - ADAM_B1 ** ADAM_STEP)
        v_hat = nv / (1.0 - ADAM_B2 ** ADAM_STEP)
        d_ref[...] = -ADAM_LR * (m_hat / (jnp.sqrt(v_hat) + ADAM_EPS) + ADAM_WD * w_ref[...])
        nm_ref[...] = nm
        nv_ref[...] = nv

    outs = pl.pallas_call(
        body, name=name, grid=(rows // tile,), in_specs=[spec] * 4, out_specs=[spec] * 3,
        out_shape=[_sds((rows, cols))] * 3, compiler_params=_params(1),
    )(*[t.reshape(rows, cols) for t in (w, g, m, v)])
    return tuple(t.reshape(shape) for t in outs)


def _place():
    return lax.axis_index("x"), lax.axis_index("y"), lax.axis_index("c")


def _flip(p, bits):
    return tuple(1 - v if (bits >> s) & 1 else v for v, s in zip(p, (2, 1, 0)))


def _slot(p):
    return 4 * p[0] + 2 * p[1] + p[2]


def all_gather_rows(shard, name):
    rows, width = shard.shape

    def body(x_ref, out_ref, send_sems, recv_sems, local_sem):
        me = _place()
        sibling = _flip(me, 1)
        chips = [_flip(me, 4), _flip(me, 2), _flip(me, 6)]

        def copy(k, block, to, src=None):
            return pltpu.make_async_remote_copy(
                src_ref=out_ref.at[_slot(block)] if src is None else src, dst_ref=out_ref.at[_slot(block)],
                send_sem=send_sems.at[k], recv_sem=recv_sems.at[k], device_id=to, device_id_type=MESH)

        mine = pltpu.make_async_copy(x_ref, out_ref.at[_slot(me)], local_sem)
        mine.start()
        first = [copy(0, me, sibling, src=x_ref)] + [copy(1 + j, me, chip, src=x_ref) for j, chip in enumerate(chips)]
        for cp in first:
            cp.start()
        passed = [copy(4 + j, chip, sibling) for j, chip in enumerate(chips)]
        for j, chip in enumerate(chips):
            copy(1 + j, chip, me).wait_recv()
            passed[j].start()
        copy(0, sibling, me).wait_recv()
        for j, chip in enumerate(chips):
            copy(4 + j, _flip(chip, 1), me).wait_recv()
        for cp in first + passed:
            cp.wait_send()
        mine.wait()

    return pl.pallas_call(
        body, name=name, out_shape=jax.ShapeDtypeStruct((N_DEV, rows, width), shard.dtype),
        in_specs=[pl.BlockSpec(memory_space=pl.ANY)], out_specs=pl.BlockSpec(memory_space=pl.ANY),
        scratch_shapes=[pltpu.SemaphoreType.DMA((7,)), pltpu.SemaphoreType.DMA((7,)), pltpu.SemaphoreType.DMA],
    )(shard)


def all_to_all_rows(blocks, name):
    _, rows, width = blocks.shape

    def body(x_ref, out_ref, send_sems, recv_sems, local_sem):
        me = _place()
        mine = pltpu.make_async_copy(x_ref.at[_slot(me)], out_ref.at[_slot(me)], local_sem)
        mine.start()
        copies = []
        for k in range(1, N_DEV):
            peer = _flip(me, k)
            copies.append(pltpu.make_async_remote_copy(
                src_ref=x_ref.at[_slot(peer)], dst_ref=out_ref.at[_slot(me)], send_sem=send_sems.at[k - 1],
                recv_sem=recv_sems.at[k - 1], device_id=peer, device_id_type=MESH))
        for cp in copies:
            cp.start()
        for k in range(1, N_DEV):
            peer = _flip(me, k)
            pltpu.make_async_remote_copy(
                src_ref=x_ref.at[_slot(peer)], dst_ref=out_ref.at[_slot(peer)], send_sem=send_sems.at[k - 1],
                recv_sem=recv_sems.at[k - 1], device_id=peer, device_id_type=MESH).wait_recv()
        for cp in copies:
            cp.wait_send()
        mine.wait()

    return pl.pallas_call(
        body, name=name, out_shape=jax.ShapeDtypeStruct(blocks.shape, blocks.dtype),
        in_specs=[pl.BlockSpec(memory_space=pl.ANY)], out_specs=pl.BlockSpec(memory_space=pl.ANY),
        scratch_shapes=[pltpu.SemaphoreType.DMA((7,)), pltpu.SemaphoreType.DMA((7,)), pltpu.SemaphoreType.DMA],
    )(blocks)


def sum_slots(blocks, name):
    n, rows, width = blocks.shape
    tile = _pick(rows, (480, 256, 128, 64, 32, 16))

    def body(x_ref, o_ref):
        total = x_ref[0].astype(f32)
        for s in range(1, n):
            total = total + x_ref[s].astype(f32)
        o_ref[...] = total

    return pl.pallas_call(
        body, name=name, grid=(rows // tile,), in_specs=[pl.BlockSpec((n, tile, width), lambda i: (0, i, 0))],
        out_specs=pl.BlockSpec((tile, width), lambda i: (i, 0)), out_shape=_sds((rows, width)), compiler_params=_params(1))(blocks)


def all_reduce_small(x, name):
    rows, width = x.shape

    def body(x_ref, o_ref, land, send_sems, recv_sems):
        me = _place()
        copies = []
        for k in range(1, N_DEV):
            peer = _flip(me, k)
            copies.append(pltpu.make_async_remote_copy(
                src_ref=x_ref, dst_ref=land.at[_slot(me)], send_sem=send_sems.at[k - 1], recv_sem=recv_sems.at[k - 1],
                device_id=peer, device_id_type=MESH))
        for cp in copies:
            cp.start()
        land[_slot(me)] = x_ref[...]
        for k in range(1, N_DEV):
            peer = _flip(me, k)
            pltpu.make_async_remote_copy(
                src_ref=x_ref, dst_ref=land.at[_slot(peer)], send_sem=send_sems.at[k - 1], recv_sem=recv_sems.at[k - 1],
                device_id=peer, device_id_type=MESH).wait_recv()
        total = land[0]
        for s in range(1, N_DEV):
            total = total + land[s]
        o_ref[...] = total
        for cp in copies:
            cp.wait_send()

    return pl.pallas_call(
        body, name=name, out_shape=_sds((rows, width)), in_specs=[pl.BlockSpec(memory_space=pltpu.VMEM)],
        out_specs=pl.BlockSpec(memory_space=pltpu.VMEM),
        scratch_shapes=[pltpu.VMEM((N_DEV, rows, width), f32), pltpu.SemaphoreType.DMA((7,)), pltpu.SemaphoreType.DMA((7,))],
    )(x)


def _slab(rows):
    return -(-rows // 16) * 16


def _pack_big(shards):
    parts = []
    for name, rows in BIG:
        part = shards[name].astype(bf16).reshape(DEPTH, rows, D_MODEL)
        parts.append(jnp.pad(part, ((0, 0), (0, _slab(rows) - rows), (0, 0))))
    return jnp.concatenate(parts, axis=1).reshape(DEPTH * BIG_ROWS, D_MODEL)


def _unpack_gathered(gathered):
    gathered = gathered.reshape(N_DEV, DEPTH, BIG_ROWS, D_MODEL)
    layers = []
    for layer in range(DEPTH):
        full, at = {}, 0
        for name, rows in BIG:
            part = gathered[:, layer, at:at + rows, :]
            at += _slab(rows)
            if name in COL_SHARDED:
                full[name] = part.reshape(N_DEV, D_MODEL, rows).transpose(1, 0, 2).reshape(D_MODEL, N_DEV * rows)
            else:
                full[name] = part.reshape(N_DEV * rows, D_MODEL)
        w_in = full.pop("w_in")
        full["w_main"] = jnp.concatenate([w_in[:, :AB_AT], w_in[:, AB_AT + 2 * GDN_HEADS:]], axis=1)
        full["w_ab"] = jnp.pad(w_in[:, AB_AT:AB_AT + 2 * GDN_HEADS], ((0, 0), (0, LANES - 2 * GDN_HEADS)))
        layers.append(full)
    return layers


def _pack_grads(layers):
    packed = []
    for grads in layers:
        grads = dict(grads)
        main, ab = grads.pop("w_main"), grads.pop("w_ab")
        grads["w_in"] = jnp.concatenate([main[:, :AB_AT], ab[:, :2 * GDN_HEADS], main[:, AB_AT:]], axis=1)
        parts = []
        for name, rows in BIG:
            g = grads[name]
            if name in COL_SHARDED:
                g = g.reshape(D_MODEL, N_DEV, rows).transpose(1, 0, 2)
            parts.append(jnp.pad(g.reshape(N_DEV, rows, D_MODEL), ((0, 0), (0, _slab(rows) - rows), (0, 0))))
        packed.append(jnp.concatenate(parts, axis=1))
    return jnp.stack(packed, axis=1).reshape(N_DEV, DEPTH * BIG_ROWS, D_MODEL)


def _unpack_shard(flat):
    flat = flat.reshape(DEPTH, BIG_ROWS, D_MODEL)
    out, at = {}, 0
    for name, rows in BIG:
        part = flat[:, at:at + rows, :]
        at += _slab(rows)
        out[name] = part.reshape(DEPTH, D_MODEL, rows) if name in COL_SHARDED else part
    return out


def _rows_of(flat_len):
    return -(-flat_len // (8 * D_MODEL)) * 8


def _pack_small(parts):
    flat = jnp.concatenate([p.reshape(-1) for p in parts])
    rows = _rows_of(flat.shape[0])
    flat = jnp.pad(flat, (0, rows * D_MODEL - flat.shape[0]))
    return flat.reshape(rows, D_MODEL)


def _unpack_small(packed, like):
    flat, out, at = packed.reshape(-1), [], 0
    for p in like:
        out.append(flat[at:at + p.size].reshape(p.shape))
        at += p.size
    return out


def _rope_tables(positions):
    inv_freq = jnp.float32(ROPE_THETA) ** (-jnp.arange(0, ROPE_DIM, 2, dtype=f32) / ROPE_DIM)
    ang = positions.astype(f32)[:, None] * inv_freq
    cos, sin = jnp.cos(ang), jnp.sin(ang)
    rest = ATTN_HEAD_DIM - ROPE_DIM
    cos_h = jnp.concatenate([cos, cos, jnp.ones((cos.shape[0], rest), f32)], axis=1)
    sin_h = jnp.concatenate([-sin, sin, jnp.zeros((sin.shape[0], rest), f32)], axis=1)
    return jnp.tile(cos_h, (1, ATTN_HEADS)), jnp.tile(sin_h, (1, ATTN_HEADS))


def _layer(h, p, mem, cos_t, sin_t):
    hn = rms_norm(h, p["norm_mix_pre"][None], "norm_mix_pre")
    proj = _linear(hn, p["w_main"], "w_main")
    ab = _linear(hn, p["w_ab"], "w_ab")
    aw, cw, gw = ATTN_WIDTH, CONV_WIDTH, GDN_WIDTH
    aq, ak, av, cb, cc, cx, gqkv, gate = _split_cols(proj, (aw, aw, aw, cw, cw, cw, 3 * gw, gw))
    y_attn = dilated_attention(rope(aq, cos_t, sin_t, ATTN_HEAD_DIM ** -0.5, "rope_q"), rope(ak, cos_t, sin_t, 1.0, "rope_k"),
                               av, "attn")
    y_conv = short_conv(cb, cc, cx, p["conv_short"], "short_conv")
    qkv = gdn_pre(gqkv, p["conv_gdn"], "gdn_pre")
    pv = jnp.zeros((8, LANES), f32).at[0, :GDN_HEADS].set(p["gdn_a_log"]).at[1, :GDN_HEADS].set(p["gdn_dt_bias"])
    g, beta = gate_beta(ab, pv, "gate_beta")
    o = gated_delta_rule(*_split_cols(qkv, (gw, gw, gw)), g, beta, "delta_rule")
    y_gdn = gdn_post(o, gate, p["gdn_norm"][None], "gdn_post")
    mix = _linear(jnp.concatenate([y_attn, y_conv, y_gdn], axis=1), p["w_out"], "w_out")
    h = add_norm(h, mix, p["norm_mix_post"][None], "norm_mix_post")

    hn = rms_norm(h, p["norm_xattn_pre"][None], "norm_xattn_pre")
    qx = _linear(hn, p["w_xq"], "w_xq")
    kv = _linear(rms_norm(mem, p["norm_mem"][None], "norm_mem"), p["w_xkv"], "w_xkv")
    xa = _linear(cross_attention(qx, kv, "xattn"), p["w_xo"], "w_xo")
    h = add_norm(h, xa, p["norm_xattn_post"][None], "norm_xattn_post")

    hn = rms_norm(h, p["norm_ffn_pre"][None], "norm_ffn_pre")
    act = swiglu(_linear(hn, p["w_gate_up"], "w_gate_up"), "swiglu")
    return add_norm(h, _linear(act, p["w_down"], "w_down"), p["norm_ffn_post"][None], "norm_ffn_post")


def _local_loss(x, big, small, mem, cos_t, sin_t, target):
    h = x
    for layer in range(DEPTH):
        h = _layer(h, {**big[layer], **{n: t[layer] for n, t in small.items()}}, mem, cos_t, sin_t)
    return loss_rows(h, target, "loss")


def kernel(x, mem, positions, norm_mix_pre, norm_mix_post, w_in, conv_short, conv_gdn, gdn_a_log, gdn_dt_bias, gdn_norm, w_out, norm_mem, norm_xattn_pre, norm_xattn_post, w_xq, w_xkv, w_xo, norm_ffn_pre, norm_ffn_post, w_gate_up, w_down, loss_target, m_norm_mix_pre, m_norm_mix_post, m_w_in, m_conv_short, m_conv_gdn, m_gdn_a_log, m_gdn_dt_bias, m_gdn_norm, m_w_out, m_norm_mem, m_norm_xattn_pre, m_norm_xattn_post, m_w_xq, m_w_xkv, m_w_xo, m_norm_ffn_pre, m_norm_ffn_post, m_w_gate_up, m_w_down, v_norm_mix_pre, v_norm_mix_post, v_w_in, v_conv_short, v_conv_gdn, v_gdn_a_log, v_gdn_dt_bias, v_gdn_norm, v_w_out, v_norm_mem, v_norm_xattn_pre, v_norm_xattn_post, v_w_xq, v_w_xkv, v_w_xo, v_norm_ffn_pre, v_norm_ffn_post, v_w_gate_up, v_w_down):
    given = dict(locals())
    weights = {n: given[n] for n in WEIGHTS}
    me = _slot(_place())

    big = _unpack_gathered(all_gather_rows(_pack_big(weights), "gather_weights"))

    def in_place(shard):
        full = jnp.zeros(shard.shape[:-1] + (shard.shape[-1] * N_DEV,), f32)
        return lax.dynamic_update_slice_in_dim(full, shard, me * shard.shape[-1], axis=shard.ndim - 1)

    placed = [in_place(conv_short), in_place(conv_gdn)]
    conv_short_full, conv_gdn_full = _unpack_small(all_reduce_small(_pack_small(placed), "gather_conv"), placed)
    small = {n: weights[n] for n in NORMS + ("gdn_a_log", "gdn_dt_bias", "gdn_norm")}
    small["conv_short"], small["conv_gdn"] = conv_short_full, conv_gdn_full

    cos_t, sin_t = _rope_tables(positions[0])
    loss, (grad_x, grad_big, grad_small) = jax.value_and_grad(_local_loss, argnums=(0, 1, 2))(
        x[0], big, small, mem[0], cos_t, sin_t, loss_target[0])

    grads = _unpack_shard(sum_slots(all_to_all_rows(_pack_grads(grad_big), "exchange_grads"), "sum_grads"))

    names = sorted(grad_small)
    parts = [grad_small[n] for n in names] + [loss.reshape(1)]
    reduced = _unpack_small(all_reduce_small(_pack_small(parts), "reduce_small"), parts)
    loss = reduced[-1][0]
    for n, g in zip(names, reduced[:-1]):
        if n in ("conv_short", "conv_gdn"):
            width = weights[n].shape[-1]
            g = lax.dynamic_slice_in_dim(g, me * width, width, axis=g.ndim - 1)
        grads[n] = g

    delta, new_m, new_v = {}, {}, {}
    for n in WEIGHTS:
        delta[n], new_m[n], new_v[n] = adamw(weights[n], grads[n], given["m_" + n], given["v_" + n], "adamw_" + n)
    return (loss, grad_x[None], *[grads[n] for n in WEIGHTS], *[delta[n] for n in WEIGHTS],
            *[new_m[n] for n in WEIGHTS], *[new_v[n] for n in WEIGHTS])
```

```python
import functools

import jax
import jax.numpy as jnp
from jax import lax
from jax.experimental import pallas as pl
from jax.experimental.pallas import tpu as pltpu

f32 = jnp.float32
bf16 = jnp.bfloat16
HIGHEST = lax.Precision.HIGHEST
MESH = pl.DeviceIdType.MESH

N_DEV = 8
DEPTH = 4
D_MODEL = 1024
EPS = 1e-6
ATTN_HEADS, ATTN_HEAD_DIM = 4, 64
ATTN_WIDTH = ATTN_HEADS * ATTN_HEAD_DIM
DILATIONS = (1, 4, 16)
QB = 128
ROPE_THETA = 500000.0
ROPE_DIM = ATTN_HEAD_DIM // 4
CONV_WIDTH = 256
GDN_HEADS, GDN_HEAD_DIM = 4, 128
GDN_WIDTH = GDN_HEADS * GDN_HEAD_DIM
GDN_CHUNK = 64
XATTN_HEADS, XATTN_HEAD_DIM = 4, 256
FFN_HIDDEN = 2816
IN_WIDTH = 3592
AB_AT = 3 * ATTN_WIDTH + 3 * CONV_WIDTH + 3 * GDN_WIDTH
MAIN_WIDTH = IN_WIDTH - 2 * GDN_HEADS
LANES = 128
ROW_TILE = 256
VMEM_LIMIT = 56 * 1024 * 1024

ADAM_LR, ADAM_B1, ADAM_B2, ADAM_EPS, ADAM_WD, ADAM_STEP = 0.001, 0.9, 0.999, 1e-08, 0.01, 10

BIG = (("w_in", 449), ("w_out", 128), ("w_xq", 128), ("w_xkv", 256), ("w_xo", 128), ("w_gate_up", 704), ("w_down", 352))
BIG_ROWS = 2160
COL_SHARDED = ("w_in", "w_xkv", "w_gate_up")
NORMS = ("norm_mix_pre", "norm_mix_post", "norm_mem", "norm_xattn_pre", "norm_xattn_post", "norm_ffn_pre", "norm_ffn_post")
WEIGHTS = ("norm_mix_pre", "norm_mix_post", "w_in", "conv_short", "conv_gdn", "gdn_a_log", "gdn_dt_bias", "gdn_norm", "w_out",
           "norm_mem", "norm_xattn_pre", "norm_xattn_post", "w_xq", "w_xkv", "w_xo", "norm_ffn_pre", "norm_ffn_post",
           "w_gate_up", "w_down")


def _params(n_grid):
    return pltpu.CompilerParams(dimension_semantics=("arbitrary",) * n_grid, vmem_limit_bytes=VMEM_LIMIT)


def _pick(n, cands):
    for c in cands:
        if n % c == 0:
            return c
    return n


def _mm(a, b, ta, tb, out_dtype, name):
    m, k = (a.shape[1], a.shape[0]) if ta else a.shape
    n = b.shape[0] if tb else b.shape[1]
    tm, tn = _pick(m, (1024, 512, 256, 128)), _pick(n, (1024, 512, 256, 128))
    tk = k if k <= 1024 else _pick(k, (1024, 512, 256, 128))
    nk = k // tk
    a_spec = pl.BlockSpec((tk, tm), lambda i, j, kk: (kk, i)) if ta else pl.BlockSpec((tm, tk), lambda i, j, kk: (i, kk))
    b_spec = pl.BlockSpec((tn, tk), lambda i, j, kk: (j, kk)) if tb else pl.BlockSpec((tk, tn), lambda i, j, kk: (kk, j))
    dims = (((0 if ta else 1,), (1 if tb else 0,)), ((), ()))

    def body(a_ref, b_ref, o_ref, acc_ref):
        kk = pl.program_id(2)
        p = lax.dot_general(a_ref[...].astype(bf16), b_ref[...].astype(bf16), dims, preferred_element_type=f32)

        @pl.when(kk == 0)
        def _():
            acc_ref[...] = p

        @pl.when(kk > 0)
        def _():
            acc_ref[...] += p

        @pl.when(kk == nk - 1)
        def _():
            o_ref[...] = acc_ref[...].astype(o_ref.dtype)

    return pl.pallas_call(
        body, name=name, grid=(m // tm, n // tn, nk), in_specs=[a_spec, b_spec],
        out_specs=pl.BlockSpec((tm, tn), lambda i, j, kk: (i, j)), out_shape=jax.ShapeDtypeStruct((m, n), out_dtype),
        scratch_shapes=[pltpu.VMEM((tm, tn), f32)], compiler_params=_params(3))(a, b)


def _linear(x, w, name):
    @jax.custom_vjp
    def lin(x, w):
        return _mm(x, w, False, False, f32, name + "_y")

    def lin_f(x, w):
        return _mm(x, w, False, False, f32, name + "_y"), (x, w)

    def lin_b(res, dy):
        x, w = res
        return _mm(dy, w, False, True, f32, name + "_dx"), _mm(x, dy, True, False, bf16, name + "_dw")

    lin.defvjp(lin_f, lin_b)
    return lin(x, w)


def _split_cols(x, widths):
    edges = [sum(widths[:i]) for i in range(len(widths) + 1)]

    def cut(x):
        return tuple(x[:, a:b] for a, b in zip(edges[:-1], edges[1:]))

    @jax.custom_vjp
    def split(x):
        return cut(x)

    split.defvjp(lambda x: (cut(x), None), lambda _, cts: (jnp.concatenate(cts, axis=1),))
    return split(x)


def _block_op(name, f, grid, in_specs, out_defs, arrays, diff, acc=None, gdefs=None):
    acc, gdefs = acc or {}, gdefs or {}
    n_in, n_out, n_grid = len(in_specs), len(out_defs), len(grid)

    def fwd_call(*xs):
        def body(*refs):
            outs = f(*[r[...] for r in refs[:n_in]])
            for r, o in zip(refs[n_in:], outs):
                r[...] = o.astype(r.dtype)

        return pl.pallas_call(
            body, name=name + "_fwd", grid=grid, in_specs=in_specs, out_specs=[d[1] for d in out_defs],
            out_shape=[d[0] for d in out_defs], compiler_params=_params(n_grid))(*xs)

    def bwd_call(*xs_and_cts):
        def body(*refs):
            xs = [r[...] for r in refs[:n_in]]
            cts = tuple(r[...] for r in refs[n_in:n_in + n_out])

            def of_diff(*dx):
                full = list(xs)
                for i, v in zip(diff, dx):
                    full[i] = v
                return tuple(f(*full))

            _, vjp = jax.vjp(of_diff, *[xs[i] for i in diff])
            grads = vjp(cts)
            for i, g, r in zip(diff, grads, refs[n_in + n_out:]):
                if i in acc:
                    first = functools.reduce(jnp.logical_and, [pl.program_id(a) == 0 for a in acc[i]])

                    @pl.when(first)
                    def _(r=r):
                        r[...] = jnp.zeros_like(r)

                    r[...] += g.astype(r.dtype)
                else:
                    r[...] = g.astype(r.dtype)

        g_defs = [gdefs.get(i, (jax.ShapeDtypeStruct(arrays[i].shape, f32), in_specs[i])) for i in diff]
        return pl.pallas_call(
            body, name=name + "_bwd", grid=grid, in_specs=list(in_specs) + [d[1] for d in out_defs],
            out_specs=[d[1] for d in g_defs], out_shape=[d[0] for d in g_defs], compiler_params=_params(n_grid))(*xs_and_cts)

    return fwd_call, bwd_call


def _simple_op(name, f, grid, in_specs, out_defs, arrays, diff, acc=None):
    fwd_call, bwd_call = _block_op(name, f, grid, in_specs, out_defs, arrays, diff, acc)

    @jax.custom_vjp
    def op(*xs):
        return tuple(fwd_call(*xs))

    def op_f(*xs):
        return tuple(fwd_call(*xs)), xs

    def op_b(xs, cts):
        grads = bwd_call(*xs, *cts)
        out = [jnp.zeros_like(x) for x in xs]
        for i, g in zip(diff, grads):
            out[i] = g
        return tuple(out)

    op.defvjp(op_f, op_b)
    return op(*arrays)


def _rows(width, tile=ROW_TILE):
    return pl.BlockSpec((tile, width), lambda i: (i, 0))


def _whole(shape):
    return pl.BlockSpec(shape, lambda *_: (0,) * len(shape))


def _sds(shape):
    return jax.ShapeDtypeStruct(shape, f32)


def _rms(x, w):
    return x * lax.rsqrt(jnp.mean(x * x, axis=-1, keepdims=True) + EPS) * w


def rms_norm(x, w, name):
    r, d = x.shape
    return _simple_op(name, lambda x, w: (_rms(x, w),), (r // ROW_TILE,), [_rows(d), _whole((1, d))],
                      [(_sds((r, d)), _rows(d))], (x, w), (0, 1), {1: (0,)})[0]


def add_norm(h, y, w, name):
    r, d = h.shape
    return _simple_op(name, lambda h, y, w: (h + _rms(y, w),), (r // ROW_TILE,), [_rows(d), _rows(d), _whole((1, d))],
                      [(_sds((r, d)), _rows(d))], (h, y, w), (0, 1, 2), {2: (0,)})[0]


def _swap8(x):
    def raw(x):
        lane = lax.broadcasted_iota(jnp.int32, x.shape, 1) % ATTN_HEAD_DIM
        half = ROPE_DIM // 2
        up = pltpu.roll(x, x.shape[1] - half, axis=1)
        down = pltpu.roll(x, half, axis=1)
        return jnp.where(lane < half, up, jnp.where(lane < ROPE_DIM, down, 0.0))

    @jax.custom_vjp
    def swap(x):
        return raw(x)

    swap.defvjp(lambda x: (raw(x), None), lambda _, g: (raw(g),))
    return swap(x)


def rope(x, cos_t, sin_t, scale, name):
    r, d = x.shape
    return _simple_op(name, lambda x, c, s: ((x * c + _swap8(x) * s) * scale,), (r // ROW_TILE,), [_rows(d)] * 3,
                      [(_sds((r, d)), _rows(d))], (x, cos_t, sin_t), (0,))[0]


def _shift_rows(x, k):
    n = x.shape[0]

    def down(x):
        row = lax.broadcasted_iota(jnp.int32, x.shape, 0)
        return jnp.where(row >= k, pltpu.roll(x, k, axis=0), 0.0)

    def up(x):
        row = lax.broadcasted_iota(jnp.int32, x.shape, 0)
        return jnp.where(row < n - k, pltpu.roll(x, n - k, axis=0), 0.0)

    @jax.custom_vjp
    def shift(x):
        return down(x)

    shift.defvjp(lambda x: (down(x), None), lambda _, g: (up(g),))
    return shift(x)


def _causal_conv(x, w):
    taps = w.shape[0]
    y = x * w[taps - 1:taps, :]
    for j in range(taps - 1):
        y = y + _shift_rows(x, taps - 1 - j) * w[j:j + 1, :]
    return y


def _cols(rows, at=0):
    return pl.BlockSpec((rows, LANES), lambda j: (0, at + j))


def short_conv(cb, cc, cx, w, name):
    s, c = cb.shape
    taps = w.shape[0]
    return _simple_op(name, lambda b, c_, x, w: (b * _causal_conv(c_ * x, w),), (c // LANES,),
                      [_cols(s)] * 3 + [_cols(taps)], [(_sds((s, c)), _cols(s))], (cb, cc, cx, w), (0, 1, 2, 3))[0]


def gdn_pre(qkv, w, name):
    s, c = qkv.shape
    taps = w.shape[0]

    def f(x, w):
        j = pl.program_id(0)
        y = jax.nn.silu(_causal_conv(x, w))
        normed = y * lax.rsqrt(jnp.sum(y * y, axis=-1, keepdims=True) + EPS)
        scale = jnp.where(j < GDN_HEADS, GDN_HEAD_DIM ** -0.5, 1.0).astype(f32)
        return (jnp.where(j < 2 * GDN_HEADS, normed * scale, y),)

    return _simple_op(name, f, (c // LANES,), [_cols(s), _cols(taps)], [(_sds((s, c)), _cols(s))], (qkv, w), (0, 1))[0]


def gate_beta(ab, pv, name):
    s = ab.shape[0]

    def f(ab, pv):
        row = lax.broadcasted_iota(jnp.int32, (LANES, GDN_WIDTH), 0)
        head = lax.broadcasted_iota(jnp.int32, (LANES, GDN_WIDTH), 1) // GDN_HEAD_DIM
        spread_a = (row == head).astype(f32)
        spread_b = (row == head + GDN_HEADS).astype(f32)
        a = jnp.dot(ab, spread_a, precision=HIGHEST, preferred_element_type=f32)
        b = jnp.dot(ab, spread_b, precision=HIGHEST, preferred_element_type=f32)
        p = jnp.dot(pv, spread_a, precision=HIGHEST, preferred_element_type=f32)
        g = -jnp.exp(p[0:1, :]) * jax.nn.softplus(a + p[1:2, :])
        return g, jax.nn.sigmoid(b)

    outs = [(_sds((s, GDN_WIDTH)), _rows(GDN_WIDTH))] * 2
    return _simple_op(name, f, (s // ROW_TILE,), [_rows(LANES), _whole((8, LANES))], outs, (ab, pv), (0, 1), {1: (0,)})


def gdn_post(o, gate, w, name):
    s, c = o.shape
    spec = pl.BlockSpec((ROW_TILE, LANES), lambda i, j: (i, j))
    return _simple_op(name, lambda o, g, w: (_rms(o, w) * jax.nn.silu(g),), (s // ROW_TILE, c // LANES),
                      [spec, spec, _whole((1, LANES))], [(_sds((s, c)), spec)], (o, gate, w), (0, 1, 2), {2: (0, 1)})[0]


def swiglu(gu, name):
    s, two_f = gu.shape
    hidden = two_f // 2
    tile = 256
    nb = hidden // tile
    g_spec = pl.BlockSpec((ROW_TILE, tile), lambda i, j: (i, j))
    u_spec = pl.BlockSpec((ROW_TILE, tile), lambda i, j: (i, j + nb))
    fwd_call, bwd_call = _block_op(name, lambda g, u: (jax.nn.silu(g) * u,), (s // ROW_TILE, nb), [g_spec, u_spec],
                                   [(_sds((s, hidden)), g_spec)], (gu, gu), (0, 1),
                                   gdefs={0: (_sds((s, hidden)), g_spec), 1: (_sds((s, hidden)), g_spec)})

    @jax.custom_vjp
    def op(gu):
        return fwd_call(gu, gu)[0]

    def op_b(gu, ct):
        dg, du = bwd_call(gu, gu, ct)
        return (jnp.concatenate([dg, du], axis=1),)

    op.defvjp(lambda gu: (fwd_call(gu, gu)[0], gu), op_b)
    return op(gu)


def attn_merge(outs, lses, name):
    s, c = outs[0].shape

    def f(o1, o2, o3, l1, l2, l3):
        m = lax.stop_gradient(jnp.maximum(jnp.maximum(l1, l2), l3))
        e1, e2, e3 = jnp.exp(l1 - m), jnp.exp(l2 - m), jnp.exp(l3 - m)
        return ((e1 * o1 + e2 * o2 + e3 * o3) / (e1 + e2 + e3),)

    return _simple_op(name, f, (s // ROW_TILE,), [_rows(c)] * 6, [(_sds((s, c)), _rows(c))], (*outs, *lses), tuple(range(6)))[0]


def loss_rows(y, target, name):
    s, d = y.shape
    nt = s // ROW_TILE

    def f(y, t):
        e = y - t
        part = 0.5 * jnp.sum(jnp.mean(e * e, axis=-1, keepdims=True), axis=0, keepdims=True)
        return (jnp.broadcast_to(part * (1.0 / (8 * LANES)), (8, LANES)),)

    out = _simple_op(name, f, (nt,), [_rows(d)] * 2, [(_sds((nt * 8, LANES)), pl.BlockSpec((8, LANES), lambda i: (i, 0)))],
                     (y, target), (0,))[0]
    return jnp.sum(out)


def _mxu(a, b, form):
    dims = {"nn": ((1,), (0,)), "nt": ((1,), (1,)), "tn": ((0,), (0,))}

    def raw(a, b, form):
        return lax.dot_general(a.astype(bf16), b.astype(bf16), (dims[form], ((), ())), preferred_element_type=f32)

    @jax.custom_vjp
    def prod(a, b):
        return raw(a, b, form)

    def prod_b(res, ct):
        a, b = res
        if form == "nn":
            return raw(ct, b, "nt"), raw(a, ct, "tn")
        if form == "nt":
            return raw(ct, b, "nn"), raw(ct, a, "tn")
        return raw(b, ct, "nt"), raw(a, ct, "nn")

    prod.defvjp(lambda a, b: (raw(a, b, form), (a, b)), prod_b)
    return prod(a, b)


def band_attention(q, k, v, nb, name):
    r, qb, width = q.shape
    dh = ATTN_HEAD_DIM

    def f(q, kp, kc, vp, vc):
        has_prev = (pl.program_id(0) % nb) > 0
        i = lax.broadcasted_iota(jnp.int32, (qb, qb), 0)
        j = lax.broadcasted_iota(jnp.int32, (qb, qb), 1)
        see_prev, see_cur = jnp.logical_and(j >= i, has_prev), j <= i
        outs, lses = [], []
        for hd in range(width // dh):
            at = slice(hd * dh, (hd + 1) * dh)
            sp = jnp.where(see_prev, _mxu(q[:, at], kp[:, at], "nt"), -jnp.inf)
            sc = jnp.where(see_cur, _mxu(q[:, at], kc[:, at], "nt"), -jnp.inf)
            m = lax.stop_gradient(jnp.maximum(jnp.max(sp, axis=-1, keepdims=True), jnp.max(sc, axis=-1, keepdims=True)))
            pp, pc = jnp.exp(sp - m), jnp.exp(sc - m)
            l = jnp.sum(pp, axis=-1, keepdims=True) + jnp.sum(pc, axis=-1, keepdims=True)
            outs.append(_mxu(pp / l, vp[:, at], "nn") + _mxu(pc / l, vc[:, at], "nn"))
            lses.append(jnp.broadcast_to(m + jnp.log(l), (qb, dh)))
        return jnp.concatenate(outs, axis=1), jnp.concatenate(lses, axis=1)

    blk = (None, qb, width)
    cur = pl.BlockSpec(blk, lambda b: (b, 0, 0))
    prev = pl.BlockSpec(blk, lambda b: (jnp.maximum(b - 1, 0), 0, 0))
    shape = _sds((r, qb, width))
    fwd_call, bwd_call = _block_op(name, f, (r,), [cur, prev, cur, prev, cur], [(shape, cur), (shape, cur)],
                                   (q, k, k, v, v), (0, 1, 2, 3, 4), gdefs={1: (shape, cur), 3: (shape, cur)})

    def to_prev(g):
        return jnp.concatenate([g[1:], jnp.zeros_like(g[:1])], axis=0)

    @jax.custom_vjp
    def op(q, k, v):
        return tuple(fwd_call(q, k, k, v, v))

    def op_b(res, cts):
        q, k, v = res
        dq, dkp, dkc, dvp, dvc = bwd_call(q, k, k, v, v, *cts)
        return dq, dkc + to_prev(dkp), dvc + to_prev(dvp)

    op.defvjp(lambda q, k, v: (tuple(fwd_call(q, k, k, v, v)), (q, k, v)), op_b)
    return op(q, k, v)


def dilated_attention(q, k, v, name):
    s = q.shape[0]
    outs, lses = [], []
    for d in DILATIONS:
        length = s // d
        nb = length // QB

        def to_residue(t):
            return t.reshape(length, d, ATTN_WIDTH).transpose(1, 0, 2).reshape(d * nb, QB, ATTN_WIDTH)

        def from_residue(t):
            return t.reshape(d, length, ATTN_WIDTH).transpose(1, 0, 2).reshape(s, ATTN_WIDTH)

        o, lse = band_attention(to_residue(q), to_residue(k), to_residue(v), nb, f"{name}_d{d}")
        outs.append(from_residue(o))
        lses.append(from_residue(lse))
    return attn_merge(outs, lses, name + "_merge")


def cross_attention(q, kv, name):
    s = q.shape[0]
    m = kv.shape[0]
    width = XATTN_HEADS * XATTN_HEAD_DIM
    tq = 512

    def f(q, k, v):
        sc = _mxu(q, k, "nt") * (XATTN_HEAD_DIM ** -0.5)
        mx = lax.stop_gradient(jnp.max(sc, axis=-1, keepdims=True))
        p = jnp.exp(sc - mx)
        return (_mxu(p / jnp.sum(p, axis=-1, keepdims=True), v, "nn"),)

    q_spec = pl.BlockSpec((tq, XATTN_HEAD_DIM), lambda a, i: (i, a))
    k_spec = pl.BlockSpec((m, XATTN_HEAD_DIM), lambda a, i: (0, a))
    v_spec = pl.BlockSpec((m, XATTN_HEAD_DIM), lambda a, i: (0, a + XATTN_HEADS))
    half = _sds((m, width))
    fwd_call, bwd_call = _block_op(name, f, (XATTN_HEADS, s // tq), [q_spec, k_spec, v_spec], [(_sds((s, width)), q_spec)],
                                   (q, kv, kv), (0, 1, 2), acc={1: (1,), 2: (1,)}, gdefs={1: (half, k_spec), 2: (half, k_spec)})

    @jax.custom_vjp
    def op(q, kv):
        return fwd_call(q, kv, kv)[0]

    def op_b(res, ct):
        q, kv = res
        dq, dk, dv = bwd_call(q, kv, kv, ct)
        return dq, jnp.concatenate([dk, dv], axis=1)

    op.defvjp(lambda q, kv: (fwd_call(q, kv, kv)[0], (q, kv)), op_b)
    return op(q, kv)


def _hi(a, b, form="nn"):
    dims = {"nn": ((1,), (0,)), "nt": ((1,), (1,)), "tn": ((0,), (0,))}[form]
    return lax.dot_general(a, b, (dims, ((), ())), precision=lax.Precision.HIGH, preferred_element_type=f32)


def _running_sum(g):
    def raw(x, form):
        c = x.shape[0]
        tri = (lax.broadcasted_iota(jnp.int32, (c, c), 0) >= lax.broadcasted_iota(jnp.int32, (c, c), 1)).astype(bf16)
        hi = x.astype(bf16)
        rest = x - hi.astype(f32)
        mid = rest.astype(bf16)
        low = (rest - mid.astype(f32)).astype(bf16)
        dims = (((1,) if form == "nn" else (0,), (0,)), ((), ()))
        return sum(lax.dot_general(tri, part, dims, preferred_element_type=f32) for part in (hi, mid, low))

    @jax.custom_vjp
    def run(x):
        return raw(x, "nn")

    run.defvjp(lambda x: (raw(x, "nn"), None), lambda _, ct: (raw(ct, "tn"),))
    return run(g)


def _unit_lower_inverse(a):
    c = a.shape[0]
    eye = (lax.broadcasted_iota(jnp.int32, (c, c), 0) == lax.broadcasted_iota(jnp.int32, (c, c), 1)).astype(f32)
    inv, power = eye - a, -a
    for _ in range(c.bit_length() - 2):
        power = _hi(power, power)
        inv = inv + _hi(inv, power)
    return inv


def _known_inverse(a, t):
    @jax.custom_vjp
    def inv(a, t):
        return t

    def inv_b(t, ct):
        return -_hi(_hi(t, ct, "tn"), t, "nt"), jnp.zeros_like(t)

    inv.defvjp(lambda a, t: (t, t), inv_b)
    return inv(a, t)


def _delta_chunk(q, k, v, g, beta, s0, known_inv=None):
    c = q.shape[0]
    i = lax.broadcasted_iota(jnp.int32, (c, c), 0)
    j = lax.broadcasted_iota(jnp.int32, (c, c), 1)
    causal, strict = i >= j, i > j
    dec = _running_sum(g)
    dec_i = dec[:, :c]
    rel = jnp.exp(jnp.where(causal, dec_i - dec_i.T, -jnp.inf))
    k_beta = k * beta
    a = jnp.where(strict, _mxu(k_beta, k, "nt") * rel, 0.0)
    inv = _unit_lower_inverse(a) if known_inv is None else _known_inverse(a, known_inv)
    e_dec = jnp.exp(dec)
    u = _hi(inv, v * beta)
    w = _hi(inv, k_beta * e_dec)
    attn = jnp.where(causal, _mxu(q, k, "nt") * rel, 0.0)
    total = jnp.sum(g, axis=0, keepdims=True)
    v_new = u - _mxu(w, s0, "nn")
    o = _mxu(q * e_dec, s0, "nn") + _mxu(attn, v_new, "nn")
    s1 = s0 * jnp.exp(total) + _mxu(k * jnp.exp(total - dec), v_new, "tn")
    return o, s1, inv


def _delta_rule_call(name, walk, n, in_specs, out_specs, out_shape, operands, exchange):
    n_in, n_out = len(in_specs), len(out_specs)

    def body(*refs):
        ins, outs = refs[:n_in], refs[n_in + bool(exchange):n_in + bool(exchange) + n_out]
        state = refs[n_in + n_out + 2 * bool(exchange)]
        step = pl.program_id(0)
        if exchange:
            start, finish = exchange.bind(refs[n_in], refs[n_in + 1 + n_out], *refs[n_in + n_out + 3:])
            pl.when(step == 0)(start)

        @pl.when(step == 0)
        def _():
            state[...] = jnp.zeros_like(state)

        walk(ins, outs, state)
        if exchange:
            pl.when(step == n - 1)(finish)

    more = [ANY] if exchange else []
    return pl.pallas_call(
        body, name=name, grid=(n,), in_specs=list(in_specs) + more, out_specs=list(out_specs) + more,
        out_shape=list(out_shape) + ([exchange.out_shape] if exchange else []),
        scratch_shapes=[pltpu.VMEM((GDN_HEAD_DIM, GDN_WIDTH), f32)] + (list(exchange.scratch) if exchange else []),
        compiler_params=_params(1))(*operands, *([exchange.operand] if exchange else []))


def _delta_heads():
    heads = [slice(hd * GDN_HEAD_DIM, (hd + 1) * GDN_HEAD_DIM) for hd in range(GDN_HEADS)]
    inv_at = [slice(hd * GDN_CHUNK, (hd + 1) * GDN_CHUNK) for hd in range(GDN_HEADS)]
    return heads, inv_at


def delta_rule_fwd(q, k, v, g, beta, name, exchange=None):
    s, width = q.shape
    c, dk = GDN_CHUNK, GDN_HEAD_DIM
    n = s // c
    heads, inv_at = _delta_heads()

    def walk(ins, outs, state):
        o_ref, s_in_ref, inv_ref = outs
        s_in_ref[...] = state[...]
        xs = [[r[:, hd] for r in (*ins, state)] for hd in heads]
        ys = [_delta_chunk(*x) for x in xs]
        for hd, at, (o, s1, inv) in zip(heads, inv_at, ys):
            o_ref[:, hd], state[:, hd], inv_ref[:, at] = o, s1, inv

    blk = pl.BlockSpec((c, width), lambda t: (t, 0))
    st = pl.BlockSpec((dk, width), lambda t: (t, 0))
    iv = pl.BlockSpec((c, GDN_HEADS * c), lambda t: (t, 0))
    return _delta_rule_call(name, walk, n, [blk] * 5, [blk, st, iv],
                            [_sds((s, width)), _sds((n * dk, width)), _sds((s, GDN_HEADS * c))], (q, k, v, g, beta), exchange)


def delta_rule_bwd(q, k, v, g, beta, s_in, inv, do, name, exchange=None):
    s, width = q.shape
    c, dk = GDN_CHUNK, GDN_HEAD_DIM
    n = s // c
    heads, inv_at = _delta_heads()

    def walk(ins, outs, dstate):
        q_ref, k_ref, v_ref, g_ref, b_ref, s_ref, inv_ref, do_ref = ins
        xs = [[r[:, hd] for r in (q_ref, k_ref, v_ref, g_ref, b_ref, s_ref)] for hd in heads]
        known = [inv_ref[:, at] for at in inv_at]
        cts = [(do_ref[:, hd], dstate[:, hd]) for hd in heads]
        grads = []
        for x, t, ct in zip(xs, known, cts):
            _, vjp = jax.vjp(lambda *y, t=t: _delta_chunk(*y, known_inv=t)[:2], *x)
            grads.append(vjp(ct))
        for hd, (*d_ins, ds0) in zip(heads, grads):
            for r, d in zip(outs, d_ins):
                r[:, hd] = d
            dstate[:, hd] = ds0

    blk = pl.BlockSpec((c, width), lambda t: (n - 1 - t, 0))
    st = pl.BlockSpec((dk, width), lambda t: (n - 1 - t, 0))
    iv = pl.BlockSpec((c, GDN_HEADS * c), lambda t: (n - 1 - t, 0))
    return _delta_rule_call(name, walk, n, [blk] * 5 + [st, iv, blk], [blk] * 5, [_sds((s, width))] * 5,
                            (q, k, v, g, beta, s_in, inv, do), exchange)


def adamw(w, g, m, v, name):
    shape = w.shape
    cols = shape[-1]
    rows = w.size // cols
    tile = _pick(rows, (512, 256, 128, 64, 32, 16, 8))
    spec = pl.BlockSpec((tile, cols), lambda i: (i, 0))

    def body(w_ref, g_ref, m_ref, v_ref, d_ref, nm_ref, nv_ref):
        grad = g_ref[...]
        nm = ADAM_B1 * m_ref[...] + (1.0 - ADAM_B1) * grad
        nv = ADAM_B2 * v_ref[...] + (1.0 - ADAM_B2) * (grad * grad)
        m_hat = nm / (1.0 - ADAM_B1 ** ADAM_STEP)
        v_hat = nv / (1.0 - ADAM_B2 ** ADAM_STEP)
        d_ref[...] = -ADAM_LR * (m_hat / (jnp.sqrt(v_hat) + ADAM_EPS) + ADAM_WD * w_ref[...])
        nm_ref[...] = nm
        nv_ref[...] = nv

    outs = pl.pallas_call(
        body, name=name, grid=(rows // tile,), in_specs=[spec] * 4, out_specs=[spec] * 3,
        out_shape=[_sds((rows, cols))] * 3, compiler_params=_params(1),
    )(*[t.reshape(rows, cols) for t in (w, g, m, v)])
    return tuple(t.reshape(shape) for t in outs)


def _place():
    return lax.axis_index("x"), lax.axis_index("y"), lax.axis_index("c")


def _flip(p, bits):
    return tuple(1 - v if (bits >> s) & 1 else v for v, s in zip(p, (2, 1, 0)))


def _slot(p):
    return 4 * p[0] + 2 * p[1] + p[2]


def _chip_of(p):
    return 2 * p[0] + p[1]


ANY = pl.BlockSpec(memory_space=pl.ANY)


class Gather:
    scratch = (pltpu.SemaphoreType.DMA((7,)), pltpu.SemaphoreType.DMA((7,)), pltpu.SemaphoreType.DMA)

    def __init__(self, shard):
        self.operand = shard
        self.out_shape = jax.ShapeDtypeStruct((N_DEV,) + shard.shape, shard.dtype)

    def bind(self, x_ref, out_ref, send_sems, recv_sems, local_sem):
        me = _place()
        sibling = _flip(me, 1)
        chips = [_flip(me, 4), _flip(me, 2), _flip(me, 6)]

        def copy(k, block, to, src=None):
            return pltpu.make_async_remote_copy(
                src_ref=out_ref.at[_slot(block)] if src is None else src, dst_ref=out_ref.at[_slot(block)],
                send_sem=send_sems.at[k], recv_sem=recv_sems.at[k], device_id=to, device_id_type=MESH)

        mine = pltpu.make_async_copy(x_ref, out_ref.at[_slot(me)], local_sem)
        first = [copy(0, me, sibling, src=x_ref)] + [copy(1 + j, me, chip, src=x_ref) for j, chip in enumerate(chips)]
        passed = [copy(4 + j, chip, sibling) for j, chip in enumerate(chips)]

        def start():
            mine.start()
            for cp in first:
                cp.start()

        def finish():
            for j, chip in enumerate(chips):
                copy(1 + j, chip, me).wait_recv()
                passed[j].start()
            copy(0, sibling, me).wait_recv()
            for j, chip in enumerate(chips):
                copy(4 + j, _flip(chip, 1), me).wait_recv()
            for cp in first + passed:
                cp.wait_send()
            mine.wait()

        return start, finish


class ChipExchange:
    scratch = (pltpu.SemaphoreType.DMA((3,)), pltpu.SemaphoreType.DMA((3,)), pltpu.SemaphoreType.DMA)

    def __init__(self, blocks):
        self.operand = blocks
        self.out_shape = jax.ShapeDtypeStruct(blocks.shape, blocks.dtype)

    def bind(self, x_ref, out_ref, send_sems, recv_sems, local_sem):
        me = _place()
        peers = [_flip(me, 4), _flip(me, 2), _flip(me, 6)]
        mine = pltpu.make_async_copy(x_ref.at[_chip_of(me)], out_ref.at[_chip_of(me)], local_sem)

        def copy(j, src_chip, dst_chip):
            return pltpu.make_async_remote_copy(
                src_ref=x_ref.at[src_chip], dst_ref=out_ref.at[dst_chip], send_sem=send_sems.at[j],
                recv_sem=recv_sems.at[j], device_id=peers[j], device_id_type=MESH)

        sends = [copy(j, _chip_of(peer), _chip_of(me)) for j, peer in enumerate(peers)]

        def start():
            mine.start()
            for cp in sends:
                cp.start()

        def finish():
            for j, peer in enumerate(peers):
                copy(j, _chip_of(me), _chip_of(peer)).wait_recv()
            for cp in sends:
                cp.wait_send()
            mine.wait()

        return start, finish


def exchange_alone(exchange, name):
    def body(x_ref, out_ref, *sems):
        start, finish = exchange.bind(x_ref, out_ref, *sems)
        start()
        finish()

    return pl.pallas_call(body, name=name, out_shape=exchange.out_shape, in_specs=[ANY], out_specs=ANY,
                          scratch_shapes=list(exchange.scratch))(exchange.operand)


def pair_exchange(blocks, name):
    _, rows, width = blocks.shape
    quad = jax.ShapeDtypeStruct((4, rows, width), blocks.dtype)

    def body(x_ref, mine_ref, theirs_ref, send_sems, recv_sems, local_sems):
        me = _place()
        sibling = _flip(me, 1)
        core = me[2]
        local = [pltpu.make_async_copy(x_ref.at[2 * q + core], mine_ref.at[q], local_sems.at[q]) for q in range(4)]
        remote = [pltpu.make_async_remote_copy(
            src_ref=x_ref.at[2 * q + 1 - core], dst_ref=theirs_ref.at[q], send_sem=send_sems.at[q], recv_sem=recv_sems.at[q],
            device_id=sibling, device_id_type=MESH) for q in range(4)]
        for cp in remote + local:
            cp.start()
        for cp in remote + local:
            cp.wait()

    return pl.pallas_call(
        body, name=name, out_shape=(quad, quad), in_specs=[ANY], out_specs=(ANY, ANY),
        scratch_shapes=[pltpu.SemaphoreType.DMA((4,)), pltpu.SemaphoreType.DMA((4,)), pltpu.SemaphoreType.DMA((4,))])(blocks)


def pair_add(mine, theirs, name):
    n, rows, width = mine.shape
    tile = _pick(rows, (432, 256, 128, 64, 32, 16))
    spec = pl.BlockSpec((None, tile, width), lambda q, i: (q, i, 0))

    def body(a_ref, b_ref, o_ref):
        o_ref[...] = (a_ref[...].astype(f32) + b_ref[...].astype(f32)).astype(o_ref.dtype)

    return pl.pallas_call(body, name=name, grid=(n, rows // tile), in_specs=[spec, spec], out_specs=spec,
                          out_shape=jax.ShapeDtypeStruct(mine.shape, mine.dtype), compiler_params=_params(2))(mine, theirs)


def sum_slots(blocks, name):
    n, rows, width = blocks.shape
    tile = _pick(rows, (432, 256, 128, 64, 32, 16))

    def body(x_ref, o_ref):
        total = x_ref[0].astype(f32)
        for s in range(1, n):
            total = total + x_ref[s].astype(f32)
        o_ref[...] = total

    return pl.pallas_call(
        body, name=name, grid=(rows // tile,), in_specs=[pl.BlockSpec((n, tile, width), lambda i: (0, i, 0))],
        out_specs=pl.BlockSpec((tile, width), lambda i: (i, 0)), out_shape=_sds((rows, width)), compiler_params=_params(1))(blocks)


def all_reduce_small(x, name):
    rows, width = x.shape

    def body(x_ref, o_ref, land, send_sems, recv_sems):
        me = _place()
        copies = []
        for k in range(1, N_DEV):
            peer = _flip(me, k)
            copies.append(pltpu.make_async_remote_copy(
                src_ref=x_ref, dst_ref=land.at[_slot(me)], send_sem=send_sems.at[k - 1], recv_sem=recv_sems.at[k - 1],
                device_id=peer, device_id_type=MESH))
        for cp in copies:
            cp.start()
        land[_slot(me)] = x_ref[...]
        for k in range(1, N_DEV):
            peer = _flip(me, k)
            pltpu.make_async_remote_copy(
                src_ref=x_ref, dst_ref=land.at[_slot(peer)], send_sem=send_sems.at[k - 1], recv_sem=recv_sems.at[k - 1],
                device_id=peer, device_id_type=MESH).wait_recv()
        total = land[0]
        for s in range(1, N_DEV):
            total = total + land[s]
        o_ref[...] = total
        for cp in copies:
            cp.wait_send()

    return pl.pallas_call(
        body, name=name, out_shape=_sds((rows, width)), in_specs=[pl.BlockSpec(memory_space=pltpu.VMEM)],
        out_specs=pl.BlockSpec(memory_space=pltpu.VMEM),
        scratch_shapes=[pltpu.VMEM((N_DEV, rows, width), f32), pltpu.SemaphoreType.DMA((7,)), pltpu.SemaphoreType.DMA((7,))],
    )(x)


def _slab(rows):
    return -(-rows // 16) * 16


def _pack_big(shards):
    parts = []
    for name, rows in BIG:
        part = shards[name].astype(bf16).reshape(DEPTH, rows, D_MODEL)
        parts.append(jnp.pad(part, ((0, 0), (0, _slab(rows) - rows), (0, 0))))
    return jnp.concatenate(parts, axis=1)


def _unpack_gathered(gathered):
    full, at = {}, 0
    for name, rows in BIG:
        part = gathered[:, at:at + rows, :]
        at += _slab(rows)
        if name in COL_SHARDED:
            full[name] = part.reshape(N_DEV, D_MODEL, rows).transpose(1, 0, 2).reshape(D_MODEL, N_DEV * rows)
        else:
            full[name] = part.reshape(N_DEV * rows, D_MODEL)
    w_in = full.pop("w_in")
    full["w_main"] = jnp.concatenate([w_in[:, :AB_AT], w_in[:, AB_AT + 2 * GDN_HEADS:]], axis=1)
    full["w_ab"] = jnp.pad(w_in[:, AB_AT:AB_AT + 2 * GDN_HEADS], ((0, 0), (0, LANES - 2 * GDN_HEADS)))
    return full


def _pack_grads(grads):
    grads = dict(grads)
    main, ab = grads.pop("w_main"), grads.pop("w_ab")
    grads["w_in"] = jnp.concatenate([main[:, :AB_AT], ab[:, :2 * GDN_HEADS], main[:, AB_AT:]], axis=1)
    parts = []
    for name, rows in BIG:
        g = grads[name]
        if name in COL_SHARDED:
            g = g.reshape(D_MODEL, N_DEV, rows).transpose(1, 0, 2)
        parts.append(jnp.pad(g.reshape(N_DEV, rows, D_MODEL), ((0, 0), (0, _slab(rows) - rows), (0, 0))))
    return jnp.concatenate(parts, axis=1)


def _unpack_shard(layers):
    flat = jnp.stack(layers)
    out, at = {}, 0
    for name, rows in BIG:
        part = flat[:, at:at + rows, :]
        at += _slab(rows)
        out[name] = part.reshape(DEPTH, D_MODEL, rows) if name in COL_SHARDED else part
    return out


def _rows_of(flat_len):
    return -(-flat_len // (8 * D_MODEL)) * 8


def _pack_small(parts):
    flat = jnp.concatenate([p.reshape(-1) for p in parts])
    rows = _rows_of(flat.shape[0])
    flat = jnp.pad(flat, (0, rows * D_MODEL - flat.shape[0]))
    return flat.reshape(rows, D_MODEL)


def _unpack_small(packed, like):
    flat, out, at = packed.reshape(-1), [], 0
    for p in like:
        out.append(flat[at:at + p.size].reshape(p.shape))
        at += p.size
    return out


def _rope_tables(positions):
    inv_freq = jnp.float32(ROPE_THETA) ** (-jnp.arange(0, ROPE_DIM, 2, dtype=f32) / ROPE_DIM)
    ang = positions.astype(f32)[:, None] * inv_freq
    cos, sin = jnp.cos(ang), jnp.sin(ang)
    rest = ATTN_HEAD_DIM - ROPE_DIM
    cos_h = jnp.concatenate([cos, cos, jnp.ones((cos.shape[0], rest), f32)], axis=1)
    sin_h = jnp.concatenate([-sin, sin, jnp.zeros((sin.shape[0], rest), f32)], axis=1)
    return jnp.tile(cos_h, (1, ATTN_HEADS)), jnp.tile(sin_h, (1, ATTN_HEADS))


HEAD_PARAMS = ("norm_mix_pre", "w_main", "w_ab", "conv_short", "conv_gdn", "gdn_a_log", "gdn_dt_bias")


def _layer_head(h, p, cos_t, sin_t):
    hn = rms_norm(h, p["norm_mix_pre"][None], "norm_mix_pre")
    proj = _linear(hn, p["w_main"], "w_main")
    ab = _linear(hn, p["w_ab"], "w_ab")
    aw, cw, gw = ATTN_WIDTH, CONV_WIDTH, GDN_WIDTH
    aq, ak, av, cb, cc, cx, gqkv, gate = _split_cols(proj, (aw, aw, aw, cw, cw, cw, 3 * gw, gw))
    y_attn = dilated_attention(rope(aq, cos_t, sin_t, ATTN_HEAD_DIM ** -0.5, "rope_q"), rope(ak, cos_t, sin_t, 1.0, "rope_k"),
                               av, "attn")
    y_conv = short_conv(cb, cc, cx, p["conv_short"], "short_conv")
    qkv = gdn_pre(gqkv, p["conv_gdn"], "gdn_pre")
    pv = jnp.zeros((8, LANES), f32).at[0, :GDN_HEADS].set(p["gdn_a_log"]).at[1, :GDN_HEADS].set(p["gdn_dt_bias"])
    g, beta = gate_beta(ab, pv, "gate_beta")
    return (*_split_cols(qkv, (gw, gw, gw)), g, beta), (gate, y_attn, y_conv)


def _layer_tail(h, o, gate, y_attn, y_conv, p, mem):
    y_gdn = gdn_post(o, gate, p["gdn_norm"][None], "gdn_post")
    mix = _linear(jnp.concatenate([y_attn, y_conv, y_gdn], axis=1), p["w_out"], "w_out")
    h = add_norm(h, mix, p["norm_mix_post"][None], "norm_mix_post")

    hn = rms_norm(h, p["norm_xattn_pre"][None], "norm_xattn_pre")
    qx = _linear(hn, p["w_xq"], "w_xq")
    kv = _linear(rms_norm(mem, p["norm_mem"][None], "norm_mem"), p["w_xkv"], "w_xkv")
    xa = _linear(cross_attention(qx, kv, "xattn"), p["w_xo"], "w_xo")
    h = add_norm(h, xa, p["norm_xattn_post"][None], "norm_xattn_post")

    hn = rms_norm(h, p["norm_ffn_pre"][None], "norm_ffn_pre")
    act = swiglu(_linear(hn, p["w_gate_up"], "w_gate_up"), "swiglu")
    return add_norm(h, _linear(act, p["w_down"], "w_down"), p["norm_ffn_post"][None], "norm_ffn_post")


def _forward_backward(x, packed, small, mem, cos_t, sin_t, target):
    h = x
    gathered = exchange_alone(Gather(packed[0]), "gather_first")
    saved = []
    for layer in range(DEPTH):
        p = {**_unpack_gathered(gathered), **{n: t[layer] for n, t in small.items()}}
        head_p = {n: p[n] for n in HEAD_PARAMS}
        tail_p = {n: t for n, t in p.items() if n not in HEAD_PARAMS}
        (rule_in, rest), head_vjp = jax.vjp(lambda h, hp: _layer_head(h, hp, cos_t, sin_t), h, head_p)
        carried = Gather(packed[layer + 1]) if layer + 1 < DEPTH else None
        o, s_in, inv, *landed = delta_rule_fwd(*rule_in, "delta_rule_fwd", carried)
        if carried:
            gathered = landed[0]
        h, tail_vjp = jax.vjp(lambda h, o, rest, tp: _layer_tail(h, o, *rest, tp, mem), h, o, rest, tail_p)
        saved.append((head_vjp, tail_vjp, rule_in, s_in, inv))

    loss, dh = jax.value_and_grad(lambda y: loss_rows(y, target, "loss"))(h)

    big_rows, small_grads, pending = [None] * DEPTH, [None] * DEPTH, None
    for layer in reversed(range(DEPTH)):
        head_vjp, tail_vjp, rule_in, s_in, inv = saved[layer]
        dh_tail, do, d_rest, d_tail_p = tail_vjp(dh)
        carried = ChipExchange(pending) if pending is not None else None
        *d_rule_in, = delta_rule_bwd(*rule_in, s_in, inv, do, "delta_rule_bwd", carried)
        if carried:
            big_rows[layer + 1] = sum_slots(d_rule_in.pop(), "sum_grads")
        dh_head, d_head_p = head_vjp((tuple(d_rule_in), d_rest))
        dh = dh_tail + dh_head
        d_p = {**d_head_p, **d_tail_p}
        small_grads[layer] = {n: d_p[n] for n in small}
        mine, theirs = pair_exchange(_pack_grads(d_p), "pair_exchange")
        pending = pair_add(mine, theirs, "pair_add")
    big_rows[0] = sum_slots(exchange_alone(ChipExchange(pending), "exchange_last"), "sum_grads")
    return loss, dh, big_rows, small_grads


def kernel(x, mem, positions, norm_mix_pre, norm_mix_post, w_in, conv_short, conv_gdn, gdn_a_log, gdn_dt_bias, gdn_norm, w_out, norm_mem, norm_xattn_pre, norm_xattn_post, w_xq, w_xkv, w_xo, norm_ffn_pre, norm_ffn_post, w_gate_up, w_down, loss_target, m_norm_mix_pre, m_norm_mix_post, m_w_in, m_conv_short, m_conv_gdn, m_gdn_a_log, m_gdn_dt_bias, m_gdn_norm, m_w_out, m_norm_mem, m_norm_xattn_pre, m_norm_xattn_post, m_w_xq, m_w_xkv, m_w_xo, m_norm_ffn_pre, m_norm_ffn_post, m_w_gate_up, m_w_down, v_norm_mix_pre, v_norm_mix_post, v_w_in, v_conv_short, v_conv_gdn, v_gdn_a_log, v_gdn_dt_bias, v_gdn_norm, v_w_out, v_norm_mem, v_norm_xattn_pre, v_norm_xattn_post, v_w_xq, v_w_xkv, v_w_xo, v_norm_ffn_pre, v_norm_ffn_post, v_w_gate_up, v_w_down):
    given = dict(locals())
    weights = {n: given[n] for n in WEIGHTS}
    me = _slot(_place())

    def in_place(shard):
        full = jnp.zeros(shard.shape[:-1] + (shard.shape[-1] * N_DEV,), f32)
        return lax.dynamic_update_slice_in_dim(full, shard, me * shard.shape[-1], axis=shard.ndim - 1)

    placed = [in_place(conv_short), in_place(conv_gdn)]
    conv_short_full, conv_gdn_full = _unpack_small(all_reduce_small(_pack_small(placed), "gather_conv"), placed)
    small = {n: weights[n] for n in NORMS + ("gdn_a_log", "gdn_dt_bias", "gdn_norm")}
    small["conv_short"], small["conv_gdn"] = conv_short_full, conv_gdn_full

    cos_t, sin_t = _rope_tables(positions[0])
    loss, grad_x, big_rows, small_layers = _forward_backward(
        x[0], _pack_big(weights), small, mem[0], cos_t, sin_t, loss_target[0])
    grads = _unpack_shard(big_rows)

    names = sorted(small)
    parts = [jnp.stack([layer[n] for layer in small_layers]) for n in names] + [loss.reshape(1)]
    reduced = _unpack_small(all_reduce_small(_pack_small(parts), "reduce_small"), parts)
    loss = reduced[-1][0]
    for n, g in zip(names, reduced[:-1]):
        if n in ("conv_short", "conv_gdn"):
            width = weights[n].shape[-1]
            g = lax.dynamic_slice_in_dim(g, me * width, width, axis=g.ndim - 1)
        grads[n] = g

    delta, new_m, new_v = {}, {}, {}
    for n in WEIGHTS:
        delta[n], new_m[n], new_v[n] = adamw(weights[n], grads[n], given["m_" + n], given["v_" + n], "adamw_" + n)
    return (loss, grad_x[None], *[grads[n] for n in WEIGHTS], *[delta[n] for n in WEIGHTS],
            *[new_m[n] for n in WEIGHTS], *[new_v[n] for n in WEIGHTS])
```

```python
import functools

import jax
import jax.numpy as jnp
from jax import lax
from jax.experimental import pallas as pl
from jax.experimental.pallas import tpu as pltpu

f32 = jnp.float32
bf16 = jnp.bfloat16
HIGHEST = lax.Precision.HIGHEST
MESH = pl.DeviceIdType.MESH

N_DEV = 8
DEPTH = 4
D_MODEL = 1024
EPS = 1e-6
ATTN_HEADS, ATTN_HEAD_DIM = 4, 64
ATTN_WIDTH = ATTN_HEADS * ATTN_HEAD_DIM
DILATIONS = (1, 4, 16)
QB = 128
ROPE_THETA = 500000.0
ROPE_DIM = ATTN_HEAD_DIM // 4
CONV_WIDTH = 256
GDN_HEADS, GDN_HEAD_DIM = 4, 128
GDN_WIDTH = GDN_HEADS * GDN_HEAD_DIM
GDN_CHUNK = 64
XATTN_HEADS, XATTN_HEAD_DIM = 4, 256
FFN_HIDDEN = 2816
IN_WIDTH = 3592
AB_AT = 3 * ATTN_WIDTH + 3 * CONV_WIDTH + 3 * GDN_WIDTH
MAIN_WIDTH = IN_WIDTH - 2 * GDN_HEADS
LANES = 128
ROW_TILE = 256
VMEM_LIMIT = 56 * 1024 * 1024

ADAM_LR, ADAM_B1, ADAM_B2, ADAM_EPS, ADAM_WD, ADAM_STEP = 0.001, 0.9, 0.999, 1e-08, 0.01, 10

BIG = (("w_in", 449), ("w_out", 128), ("w_xq", 128), ("w_xkv", 256), ("w_xo", 128), ("w_gate_up", 704), ("w_down", 352))
BIG_ROWS = 2160
COL_SHARDED = ("w_in", "w_xkv", "w_gate_up")
NORMS = ("norm_mix_pre", "norm_mix_post", "norm_mem", "norm_xattn_pre", "norm_xattn_post", "norm_ffn_pre", "norm_ffn_post")
WEIGHTS = ("norm_mix_pre", "norm_mix_post", "w_in", "conv_short", "conv_gdn", "gdn_a_log", "gdn_dt_bias", "gdn_norm", "w_out",
           "norm_mem", "norm_xattn_pre", "norm_xattn_post", "w_xq", "w_xkv", "w_xo", "norm_ffn_pre", "norm_ffn_post",
           "w_gate_up", "w_down")


def _params(n_grid):
    return pltpu.CompilerParams(dimension_semantics=("arbitrary",) * n_grid, vmem_limit_bytes=VMEM_LIMIT)


def _pick(n, cands):
    for c in cands:
        if n % c == 0:
            return c
    return n


def _mm(a, b, ta, tb, out_dtype, name):
    m, k = (a.shape[1], a.shape[0]) if ta else a.shape
    n = b.shape[0] if tb else b.shape[1]
    tm, tn = _pick(m, (1024, 512, 256, 128)), _pick(n, (1024, 512, 256, 128))
    tk = k if k <= 1024 else _pick(k, (1024, 512, 256, 128))
    nk = k // tk
    a_spec = pl.BlockSpec((tk, tm), lambda i, j, kk: (kk, i)) if ta else pl.BlockSpec((tm, tk), lambda i, j, kk: (i, kk))
    b_spec = pl.BlockSpec((tn, tk), lambda i, j, kk: (j, kk)) if tb else pl.BlockSpec((tk, tn), lambda i, j, kk: (kk, j))
    dims = (((0 if ta else 1,), (1 if tb else 0,)), ((), ()))

    def body(a_ref, b_ref, o_ref, acc_ref):
        kk = pl.program_id(2)
        p = lax.dot_general(a_ref[...].astype(bf16), b_ref[...].astype(bf16), dims, preferred_element_type=f32)

        @pl.when(kk == 0)
        def _():
            acc_ref[...] = p

        @pl.when(kk > 0)
        def _():
            acc_ref[...] += p

        @pl.when(kk == nk - 1)
        def _():
            o_ref[...] = acc_ref[...].astype(o_ref.dtype)

    return pl.pallas_call(
        body, name=name, grid=(m // tm, n // tn, nk), in_specs=[a_spec, b_spec],
        out_specs=pl.BlockSpec((tm, tn), lambda i, j, kk: (i, j)), out_shape=jax.ShapeDtypeStruct((m, n), out_dtype),
        scratch_shapes=[pltpu.VMEM((tm, tn), f32)], compiler_params=_params(3))(a, b)


def _linear(x, w, name):
    @jax.custom_vjp
    def lin(x, w):
        return _mm(x, w, False, False, f32, name + "_y")

    def lin_f(x, w):
        return _mm(x, w, False, False, f32, name + "_y"), (x, w)

    def lin_b(res, dy):
        x, w = res
        return _mm(dy, w, False, True, f32, name + "_dx"), _mm(x, dy, True, False, bf16, name + "_dw")

    lin.defvjp(lin_f, lin_b)
    return lin(x, w)


def _split_cols(x, widths):
    edges = [sum(widths[:i]) for i in range(len(widths) + 1)]

    def cut(x):
        return tuple(x[:, a:b] for a, b in zip(edges[:-1], edges[1:]))

    @jax.custom_vjp
    def split(x):
        return cut(x)

    split.defvjp(lambda x: (cut(x), None), lambda _, cts: (jnp.concatenate(cts, axis=1),))
    return split(x)


def _block_op(name, f, grid, in_specs, out_defs, arrays, diff, acc=None, gdefs=None):
    acc, gdefs = acc or {}, gdefs or {}
    n_in, n_out, n_grid = len(in_specs), len(out_defs), len(grid)

    def fwd_call(*xs):
        def body(*refs):
            outs = f(*[r[...] for r in refs[:n_in]])
            for r, o in zip(refs[n_in:], outs):
                r[...] = o.astype(r.dtype)

        return pl.pallas_call(
            body, name=name + "_fwd", grid=grid, in_specs=in_specs, out_specs=[d[1] for d in out_defs],
            out_shape=[d[0] for d in out_defs], compiler_params=_params(n_grid))(*xs)

    def bwd_call(*xs_and_cts):
        def body(*refs):
            xs = [r[...] for r in refs[:n_in]]
            cts = tuple(r[...] for r in refs[n_in:n_in + n_out])

            def of_diff(*dx):
                full = list(xs)
                for i, v in zip(diff, dx):
                    full[i] = v
                return tuple(f(*full))

            _, vjp = jax.vjp(of_diff, *[xs[i] for i in diff])
            grads = vjp(cts)
            for i, g, r in zip(diff, grads, refs[n_in + n_out:]):
                if i in acc:
                    first = functools.reduce(jnp.logical_and, [pl.program_id(a) == 0 for a in acc[i]])

                    @pl.when(first)
                    def _(r=r):
                        r[...] = jnp.zeros_like(r)

                    r[...] += g.astype(r.dtype)
                else:
                    r[...] = g.astype(r.dtype)

        g_defs = [gdefs.get(i, (jax.ShapeDtypeStruct(arrays[i].shape, f32), in_specs[i])) for i in diff]
        return pl.pallas_call(
            body, name=name + "_bwd", grid=grid, in_specs=list(in_specs) + [d[1] for d in out_defs],
            out_specs=[d[1] for d in g_defs], out_shape=[d[0] for d in g_defs], compiler_params=_params(n_grid))(*xs_and_cts)

    return fwd_call, bwd_call


def _simple_op(name, f, grid, in_specs, out_defs, arrays, diff, acc=None):
    fwd_call, bwd_call = _block_op(name, f, grid, in_specs, out_defs, arrays, diff, acc)

    @jax.custom_vjp
    def op(*xs):
        return tuple(fwd_call(*xs))

    def op_f(*xs):
        return tuple(fwd_call(*xs)), xs

    def op_b(xs, cts):
        grads = bwd_call(*xs, *cts)
        out = [jnp.zeros_like(x) for x in xs]
        for i, g in zip(diff, grads):
            out[i] = g
        return tuple(out)

    op.defvjp(op_f, op_b)
    return op(*arrays)


def _rows(width, tile=ROW_TILE):
    return pl.BlockSpec((tile, width), lambda i: (i, 0))


def _whole(shape):
    return pl.BlockSpec(shape, lambda *_: (0,) * len(shape))


def _sds(shape):
    return jax.ShapeDtypeStruct(shape, f32)


def _rms(x, w):
    return x * lax.rsqrt(jnp.mean(x * x, axis=-1, keepdims=True) + EPS) * w


def rms_norm(x, w, name):
    r, d = x.shape
    return _simple_op(name, lambda x, w: (_rms(x, w),), (r // ROW_TILE,), [_rows(d), _whole((1, d))],
                      [(_sds((r, d)), _rows(d))], (x, w), (0, 1), {1: (0,)})[0]


def add_norm(h, y, w, name):
    r, d = h.shape
    return _simple_op(name, lambda h, y, w: (h + _rms(y, w),), (r // ROW_TILE,), [_rows(d), _rows(d), _whole((1, d))],
                      [(_sds((r, d)), _rows(d))], (h, y, w), (0, 1, 2), {2: (0,)})[0]


def _swap8(x):
    def raw(x):
        lane = lax.broadcasted_iota(jnp.int32, x.shape, 1) % ATTN_HEAD_DIM
        half = ROPE_DIM // 2
        up = pltpu.roll(x, x.shape[1] - half, axis=1)
        down = pltpu.roll(x, half, axis=1)
        return jnp.where(lane < half, up, jnp.where(lane < ROPE_DIM, down, 0.0))

    @jax.custom_vjp
    def swap(x):
        return raw(x)

    swap.defvjp(lambda x: (raw(x), None), lambda _, g: (raw(g),))
    return swap(x)


def rope(x, cos_t, sin_t, scale, name):
    r, d = x.shape
    return _simple_op(name, lambda x, c, s: ((x * c + _swap8(x) * s) * scale,), (r // ROW_TILE,), [_rows(d)] * 3,
                      [(_sds((r, d)), _rows(d))], (x, cos_t, sin_t), (0,))[0]


def _shift_rows(x, k):
    n = x.shape[0]

    def down(x):
        row = lax.broadcasted_iota(jnp.int32, x.shape, 0)
        return jnp.where(row >= k, pltpu.roll(x, k, axis=0), 0.0)

    def up(x):
        row = lax.broadcasted_iota(jnp.int32, x.shape, 0)
        return jnp.where(row < n - k, pltpu.roll(x, n - k, axis=0), 0.0)

    @jax.custom_vjp
    def shift(x):
        return down(x)

    shift.defvjp(lambda x: (down(x), None), lambda _, g: (up(g),))
    return shift(x)


def _causal_conv(x, w):
    taps = w.shape[0]
    y = x * w[taps - 1:taps, :]
    for j in range(taps - 1):
        y = y + _shift_rows(x, taps - 1 - j) * w[j:j + 1, :]
    return y


def _cols(rows, at=0):
    return pl.BlockSpec((rows, LANES), lambda j: (0, at + j))


def short_conv(cb, cc, cx, w, name):
    s, c = cb.shape
    taps = w.shape[0]
    return _simple_op(name, lambda b, c_, x, w: (b * _causal_conv(c_ * x, w),), (c // LANES,),
                      [_cols(s)] * 3 + [_cols(taps)], [(_sds((s, c)), _cols(s))], (cb, cc, cx, w), (0, 1, 2, 3))[0]


def gdn_pre(qkv, w, name):
    s, c = qkv.shape
    taps = w.shape[0]

    def f(x, w):
        j = pl.program_id(0)
        y = jax.nn.silu(_causal_conv(x, w))
        normed = y * lax.rsqrt(jnp.sum(y * y, axis=-1, keepdims=True) + EPS)
        scale = jnp.where(j < GDN_HEADS, GDN_HEAD_DIM ** -0.5, 1.0).astype(f32)
        return (jnp.where(j < 2 * GDN_HEADS, normed * scale, y),)

    return _simple_op(name, f, (c // LANES,), [_cols(s), _cols(taps)], [(_sds((s, c)), _cols(s))], (qkv, w), (0, 1))[0]


def gate_beta(ab, pv, name):
    s = ab.shape[0]

    def f(ab, pv):
        row = lax.broadcasted_iota(jnp.int32, (LANES, GDN_WIDTH), 0)
        head = lax.broadcasted_iota(jnp.int32, (LANES, GDN_WIDTH), 1) // GDN_HEAD_DIM
        spread_a = (row == head).astype(f32)
        spread_b = (row == head + GDN_HEADS).astype(f32)
        a = jnp.dot(ab, spread_a, precision=HIGHEST, preferred_element_type=f32)
        b = jnp.dot(ab, spread_b, precision=HIGHEST, preferred_element_type=f32)
        p = jnp.dot(pv, spread_a, precision=HIGHEST, preferred_element_type=f32)
        g = -jnp.exp(p[0:1, :]) * jax.nn.softplus(a + p[1:2, :])
        return g, jax.nn.sigmoid(b)

    outs = [(_sds((s, GDN_WIDTH)), _rows(GDN_WIDTH))] * 2
    return _simple_op(name, f, (s // ROW_TILE,), [_rows(LANES), _whole((8, LANES))], outs, (ab, pv), (0, 1), {1: (0,)})


def gdn_post(o, gate, w, name):
    s, c = o.shape
    spec = pl.BlockSpec((ROW_TILE, LANES), lambda i, j: (i, j))
    return _simple_op(name, lambda o, g, w: (_rms(o, w) * jax.nn.silu(g),), (s // ROW_TILE, c // LANES),
                      [spec, spec, _whole((1, LANES))], [(_sds((s, c)), spec)], (o, gate, w), (0, 1, 2), {2: (0, 1)})[0]


def swiglu(gu, name):
    s, two_f = gu.shape
    hidden = two_f // 2
    tile = 256
    nb = hidden // tile
    g_spec = pl.BlockSpec((ROW_TILE, tile), lambda i, j: (i, j))
    u_spec = pl.BlockSpec((ROW_TILE, tile), lambda i, j: (i, j + nb))
    fwd_call, bwd_call = _block_op(name, lambda g, u: (jax.nn.silu(g) * u,), (s // ROW_TILE, nb), [g_spec, u_spec],
                                   [(_sds((s, hidden)), g_spec)], (gu, gu), (0, 1),
                                   gdefs={0: (_sds((s, hidden)), g_spec), 1: (_sds((s, hidden)), g_spec)})

    @jax.custom_vjp
    def op(gu):
        return fwd_call(gu, gu)[0]

    def op_b(gu, ct):
        dg, du = bwd_call(gu, gu, ct)
        return (jnp.concatenate([dg, du], axis=1),)

    op.defvjp(lambda gu: (fwd_call(gu, gu)[0], gu), op_b)
    return op(gu)


def attn_merge(outs, lses, name):
    s, c = outs[0].shape

    def f(o1, o2, o3, l1, l2, l3):
        m = lax.stop_gradient(jnp.maximum(jnp.maximum(l1, l2), l3))
        e1, e2, e3 = jnp.exp(l1 - m), jnp.exp(l2 - m), jnp.exp(l3 - m)
        return ((e1 * o1 + e2 * o2 + e3 * o3) / (e1 + e2 + e3),)

    return _simple_op(name, f, (s // ROW_TILE,), [_rows(c)] * 6, [(_sds((s, c)), _rows(c))], (*outs, *lses), tuple(range(6)))[0]


def loss_rows(y, target, name):
    s, d = y.shape
    nt = s // ROW_TILE

    def f(y, t):
        e = y - t
        part = 0.5 * jnp.sum(jnp.mean(e * e, axis=-1, keepdims=True), axis=0, keepdims=True)
        return (jnp.broadcast_to(part * (1.0 / (8 * LANES)), (8, LANES)),)

    out = _simple_op(name, f, (nt,), [_rows(d)] * 2, [(_sds((nt * 8, LANES)), pl.BlockSpec((8, LANES), lambda i: (i, 0)))],
                     (y, target), (0,))[0]
    return jnp.sum(out)


def _mxu(a, b, form):
    dims = {"nn": ((1,), (0,)), "nt": ((1,), (1,)), "tn": ((0,), (0,))}

    def raw(a, b, form):
        return lax.dot_general(a.astype(bf16), b.astype(bf16), (dims[form], ((), ())), preferred_element_type=f32)

    @jax.custom_vjp
    def prod(a, b):
        return raw(a, b, form)

    def prod_b(res, ct):
        a, b = res
        if form == "nn":
            return raw(ct, b, "nt"), raw(a, ct, "tn")
        if form == "nt":
            return raw(ct, b, "nn"), raw(ct, a, "tn")
        return raw(b, ct, "nt"), raw(a, ct, "nn")

    prod.defvjp(lambda a, b: (raw(a, b, form), (a, b)), prod_b)
    return prod(a, b)


def band_attention(q, k, v, nb, name):
    r, qb, width = q.shape
    dh = ATTN_HEAD_DIM

    def f(q, kp, kc, vp, vc):
        has_prev = (pl.program_id(0) % nb) > 0
        i = lax.broadcasted_iota(jnp.int32, (qb, qb), 0)
        j = lax.broadcasted_iota(jnp.int32, (qb, qb), 1)
        see_prev, see_cur = jnp.logical_and(j >= i, has_prev), j <= i
        outs, lses = [], []
        for hd in range(width // dh):
            at = slice(hd * dh, (hd + 1) * dh)
            sp = jnp.where(see_prev, _mxu(q[:, at], kp[:, at], "nt"), -jnp.inf)
            sc = jnp.where(see_cur, _mxu(q[:, at], kc[:, at], "nt"), -jnp.inf)
            m = lax.stop_gradient(jnp.maximum(jnp.max(sp, axis=-1, keepdims=True), jnp.max(sc, axis=-1, keepdims=True)))
            pp, pc = jnp.exp(sp - m), jnp.exp(sc - m)
            l = jnp.sum(pp, axis=-1, keepdims=True) + jnp.sum(pc, axis=-1, keepdims=True)
            outs.append(_mxu(pp / l, vp[:, at], "nn") + _mxu(pc / l, vc[:, at], "nn"))
            lses.append(jnp.broadcast_to(m + jnp.log(l), (qb, dh)))
        return jnp.concatenate(outs, axis=1), jnp.concatenate(lses, axis=1)

    blk = (None, qb, width)
    cur = pl.BlockSpec(blk, lambda b: (b, 0, 0))
    prev = pl.BlockSpec(blk, lambda b: (jnp.maximum(b - 1, 0), 0, 0))
    shape = _sds((r, qb, width))
    fwd_call, bwd_call = _block_op(name, f, (r,), [cur, prev, cur, prev, cur], [(shape, cur), (shape, cur)],
                                   (q, k, k, v, v), (0, 1, 2, 3, 4), gdefs={1: (shape, cur), 3: (shape, cur)})

    def to_prev(g):
        return jnp.concatenate([g[1:], jnp.zeros_like(g[:1])], axis=0)

    @jax.custom_vjp
    def op(q, k, v):
        return tuple(fwd_call(q, k, k, v, v))

    def op_b(res, cts):
        q, k, v = res
        dq, dkp, dkc, dvp, dvc = bwd_call(q, k, k, v, v, *cts)
        return dq, dkc + to_prev(dkp), dvc + to_prev(dvp)

    op.defvjp(lambda q, k, v: (tuple(fwd_call(q, k, k, v, v)), (q, k, v)), op_b)
    return op(q, k, v)


def dilated_attention(q, k, v, name):
    s = q.shape[0]
    outs, lses = [], []
    for d in DILATIONS:
        length = s // d
        nb = length // QB

        def to_residue(t):
            return t.reshape(length, d, ATTN_WIDTH).transpose(1, 0, 2).reshape(d * nb, QB, ATTN_WIDTH)

        def from_residue(t):
            return t.reshape(d, length, ATTN_WIDTH).transpose(1, 0, 2).reshape(s, ATTN_WIDTH)

        o, lse = band_attention(to_residue(q), to_residue(k), to_residue(v), nb, f"{name}_d{d}")
        outs.append(from_residue(o))
        lses.append(from_residue(lse))
    return attn_merge(outs, lses, name + "_merge")


def cross_attention(q, kv, name):
    s = q.shape[0]
    m = kv.shape[0]
    width = XATTN_HEADS * XATTN_HEAD_DIM
    tq = 512

    def f(q, k, v):
        sc = _mxu(q, k, "nt") * (XATTN_HEAD_DIM ** -0.5)
        mx = lax.stop_gradient(jnp.max(sc, axis=-1, keepdims=True))
        p = jnp.exp(sc - mx)
        return (_mxu(p / jnp.sum(p, axis=-1, keepdims=True), v, "nn"),)

    q_spec = pl.BlockSpec((tq, XATTN_HEAD_DIM), lambda a, i: (i, a))
    k_spec = pl.BlockSpec((m, XATTN_HEAD_DIM), lambda a, i: (0, a))
    v_spec = pl.BlockSpec((m, XATTN_HEAD_DIM), lambda a, i: (0, a + XATTN_HEADS))
    half = _sds((m, width))
    fwd_call, bwd_call = _block_op(name, f, (XATTN_HEADS, s // tq), [q_spec, k_spec, v_spec], [(_sds((s, width)), q_spec)],
                                   (q, kv, kv), (0, 1, 2), acc={1: (1,), 2: (1,)}, gdefs={1: (half, k_spec), 2: (half, k_spec)})

    @jax.custom_vjp
    def op(q, kv):
        return fwd_call(q, kv, kv)[0]

    def op_b(res, ct):
        q, kv = res
        dq, dk, dv = bwd_call(q, kv, kv, ct)
        return dq, jnp.concatenate([dk, dv], axis=1)

    op.defvjp(lambda q, kv: (fwd_call(q, kv, kv)[0], (q, kv)), op_b)
    return op(q, kv)


def _hi(a, b, form="nn"):
    dims = {"nn": ((1,), (0,)), "nt": ((1,), (1,)), "tn": ((0,), (0,))}[form]
    return lax.dot_general(a, b, (dims, ((), ())), precision=lax.Precision.HIGH, preferred_element_type=f32)


def _running_sum(g):
    def raw(x, form):
        c = x.shape[0]
        tri = (lax.broadcasted_iota(jnp.int32, (c, c), 0) >= lax.broadcasted_iota(jnp.int32, (c, c), 1)).astype(bf16)
        hi = x.astype(bf16)
        rest = x - hi.astype(f32)
        mid = rest.astype(bf16)
        low = (rest - mid.astype(f32)).astype(bf16)
        dims = (((1,) if form == "nn" else (0,), (0,)), ((), ()))
        return sum(lax.dot_general(tri, part, dims, preferred_element_type=f32) for part in (hi, mid, low))

    @jax.custom_vjp
    def run(x):
        return raw(x, "nn")

    run.defvjp(lambda x: (raw(x, "nn"), None), lambda _, ct: (raw(ct, "tn"),))
    return run(g)


def _unit_lower_inverse(a):
    c = a.shape[0]
    eye = (lax.broadcasted_iota(jnp.int32, (c, c), 0) == lax.broadcasted_iota(jnp.int32, (c, c), 1)).astype(f32)
    inv, power = eye - a, -a
    for _ in range(c.bit_length() - 2):
        power = _hi(power, power)
        inv = inv + _hi(inv, power)
    return inv


def _known_inverse(a, t):
    @jax.custom_vjp
    def inv(a, t):
        return t

    def inv_b(t, ct):
        return -_hi(_hi(t, ct, "tn"), t, "nt"), jnp.zeros_like(t)

    inv.defvjp(lambda a, t: (t, t), inv_b)
    return inv(a, t)


def _delta_chunk(q, k, v, g, beta, s0, known_inv=None):
    c = q.shape[0]
    i = lax.broadcasted_iota(jnp.int32, (c, c), 0)
    j = lax.broadcasted_iota(jnp.int32, (c, c), 1)
    causal, strict = i >= j, i > j
    dec = _running_sum(g)
    dec_i = dec[:, :c]
    rel = jnp.exp(jnp.where(causal, dec_i - dec_i.T, -jnp.inf))
    k_beta = k * beta
    a = jnp.where(strict, _mxu(k_beta, k, "nt") * rel, 0.0)
    inv = _unit_lower_inverse(a) if known_inv is None else _known_inverse(a, known_inv)
    e_dec = jnp.exp(dec)
    u = _hi(inv, v * beta)
    w = _hi(inv, k_beta * e_dec)
    attn = jnp.where(causal, _mxu(q, k, "nt") * rel, 0.0)
    total = jnp.sum(g, axis=0, keepdims=True)
    v_new = u - _mxu(w, s0, "nn")
    o = _mxu(q * e_dec, s0, "nn") + _mxu(attn, v_new, "nn")
    s1 = s0 * jnp.exp(total) + _mxu(k * jnp.exp(total - dec), v_new, "tn")
    return o, s1, inv


def _delta_rule_call(name, walk, n, in_specs, out_specs, out_shape, operands, exchange):
    n_in, n_out = len(in_specs), len(out_specs)

    def body(*refs):
        ins, outs = refs[:n_in], refs[n_in + bool(exchange):n_in + bool(exchange) + n_out]
        state = refs[n_in + n_out + 2 * bool(exchange)]
        step = pl.program_id(0)
        if exchange:
            start, finish = exchange.bind(refs[n_in], refs[n_in + 1 + n_out], *refs[n_in + n_out + 3:])
            pl.when(step == 0)(start)

        @pl.when(step == 0)
        def _():
            state[...] = jnp.zeros_like(state)

        walk(ins, outs, state)
        if exchange:
            pl.when(step == n - 1)(finish)

    more = [ANY] if exchange else []
    return pl.pallas_call(
        body, name=name, grid=(n,), in_specs=list(in_specs) + more, out_specs=list(out_specs) + more,
        out_shape=list(out_shape) + ([exchange.out_shape] if exchange else []),
        scratch_shapes=[pltpu.VMEM((GDN_HEAD_DIM, GDN_WIDTH), f32)] + (list(exchange.scratch) if exchange else []),
        compiler_params=_params(1))(*operands, *([exchange.operand] if exchange else []))


def _delta_heads():
    heads = [slice(hd * GDN_HEAD_DIM, (hd + 1) * GDN_HEAD_DIM) for hd in range(GDN_HEADS)]
    inv_at = [slice(hd * GDN_CHUNK, (hd + 1) * GDN_CHUNK) for hd in range(GDN_HEADS)]
    return heads, inv_at


def delta_rule_fwd(q, k, v, g, beta, name, exchange=None):
    s, width = q.shape
    c, dk = GDN_CHUNK, GDN_HEAD_DIM
    n = s // c
    heads, inv_at = _delta_heads()

    def walk(ins, outs, state):
        o_ref, s_in_ref, inv_ref = outs
        s_in_ref[...] = state[...]
        xs = [[r[:, hd] for r in (*ins, state)] for hd in heads]
        ys = [_delta_chunk(*x) for x in xs]
        for hd, at, (o, s1, inv) in zip(heads, inv_at, ys):
            o_ref[:, hd], state[:, hd], inv_ref[:, at] = o, s1, inv

    blk = pl.BlockSpec((c, width), lambda t: (t, 0))
    st = pl.BlockSpec((dk, width), lambda t: (t, 0))
    iv = pl.BlockSpec((c, GDN_HEADS * c), lambda t: (t, 0))
    return _delta_rule_call(name, walk, n, [blk] * 5, [blk, st, iv],
                            [_sds((s, width)), _sds((n * dk, width)), _sds((s, GDN_HEADS * c))], (q, k, v, g, beta), exchange)


def delta_rule_bwd(q, k, v, g, beta, s_in, inv, do, name, exchange=None):
    s, width = q.shape
    c, dk = GDN_CHUNK, GDN_HEAD_DIM
    n = s // c
    heads, inv_at = _delta_heads()

    def walk(ins, outs, dstate):
        q_ref, k_ref, v_ref, g_ref, b_ref, s_ref, inv_ref, do_ref = ins
        xs = [[r[:, hd] for r in (q_ref, k_ref, v_ref, g_ref, b_ref, s_ref)] for hd in heads]
        known = [inv_ref[:, at] for at in inv_at]
        cts = [(do_ref[:, hd], dstate[:, hd]) for hd in heads]
        grads = []
        for x, t, ct in zip(xs, known, cts):
            _, vjp = jax.vjp(lambda *y, t=t: _delta_chunk(*y, known_inv=t)[:2], *x)
            grads.append(vjp(ct))
        for hd, (*d_ins, ds0) in zip(heads, grads):
            for r, d in zip(outs, d_ins):
                r[:, hd] = d
            dstate[:, hd] = ds0

    blk = pl.BlockSpec((c, width), lambda t: (n - 1 - t, 0))
    st = pl.BlockSpec((dk, width), lambda t: (n - 1 - t, 0))
    iv = pl.BlockSpec((c, GDN_HEADS * c), lambda t: (n - 1 - t, 0))
    return _delta_rule_call(name, walk, n, [blk] * 5 + [st, iv, blk], [blk] * 5, [_sds((s, width))] * 5,
                            (q, k, v, g, beta, s_in, inv, do), exchange)


def adamw(w, g, m, v, name):
    shape = w.shape
    cols = shape[-1]
    rows = w.size // cols
    tile = _pick(rows, (512, 256, 128, 64, 32, 16, 8))
    spec = pl.BlockSpec((tile, cols), lambda i: (i, 0))

    def body(w_ref, g_ref, m_ref, v_ref, d_ref, nm_ref, nv_ref):
        grad = g_ref[...]
        nm = ADAM_B1 * m_ref[...] + (1.0 - ADAM_B1) * grad
        nv = ADAM_B2 * v_ref[...] + (1.0 - ADAM_B2) * (grad * grad)
        m_hat = nm / (1.0 - ADAM_B1 ** ADAM_STEP)
        v_hat = nv / (1.0 - ADAM_B2 ** ADAM_STEP)
        d_ref[...] = -ADAM_LR * (m_hat / (jnp.sqrt(v_hat) + ADAM_EPS) + ADAM_WD * w_ref[...])
        nm_ref[...] = nm
        nv_ref[...] = nv

    outs = pl.pallas_call(
        body, name=name, grid=(rows // tile,), in_specs=[spec] * 4, out_specs=[spec] * 3,
        out_shape=[_sds((rows, cols))] * 3, compiler_params=_params(1),
    )(*[t.reshape(rows, cols) for t in (w, g, m, v)])
    return tuple(t.reshape(shape) for t in outs)


def _place():
    return lax.axis_index("x"), lax.axis_index("y"), lax.axis_index("c")


def _flip(p, bits):
    return tuple(1 - v if (bits >> s) & 1 else v for v, s in zip(p, (2, 1, 0)))


def _slot(p):
    return 4 * p[0] + 2 * p[1] + p[2]


def _chip_of(p):
    return 2 * p[0] + p[1]


ANY = pl.BlockSpec(memory_space=pl.ANY)


class Gather:
    scratch = (pltpu.SemaphoreType.DMA((7,)), pltpu.SemaphoreType.DMA((7,)), pltpu.SemaphoreType.DMA)

    def __init__(self, shard):
        self.operand = shard
        self.out_shape = jax.ShapeDtypeStruct((N_DEV,) + shard.shape, shard.dtype)

    def bind(self, x_ref, out_ref, send_sems, recv_sems, local_sem):
        me = _place()
        sibling = _flip(me, 1)
        chips = [_flip(me, 4), _flip(me, 2), _flip(me, 6)]

        def copy(k, block, to, src=None):
            return pltpu.make_async_remote_copy(
                src_ref=out_ref.at[_slot(block)] if src is None else src, dst_ref=out_ref.at[_slot(block)],
                send_sem=send_sems.at[k], recv_sem=recv_sems.at[k], device_id=to, device_id_type=MESH)

        mine = pltpu.make_async_copy(x_ref, out_ref.at[_slot(me)], local_sem)
        first = [copy(0, me, sibling, src=x_ref)] + [copy(1 + j, me, chip, src=x_ref) for j, chip in enumerate(chips)]
        passed = [copy(4 + j, chip, sibling) for j, chip in enumerate(chips)]

        def start():
            mine.start()
            for cp in first:
                cp.start()

        def finish():
            for j, chip in enumerate(chips):
                copy(1 + j, chip, me).wait_recv()
                passed[j].start()
            copy(0, sibling, me).wait_recv()
            for j, chip in enumerate(chips):
                copy(4 + j, _flip(chip, 1), me).wait_recv()
            for cp in first + passed:
                cp.wait_send()
            mine.wait()

        return start, finish


class ChipExchange:
    scratch = (pltpu.SemaphoreType.DMA((3,)), pltpu.SemaphoreType.DMA((3,)), pltpu.SemaphoreType.DMA)

    def __init__(self, blocks):
        self.operand = blocks
        self.out_shape = jax.ShapeDtypeStruct(blocks.shape, blocks.dtype)

    def bind(self, x_ref, out_ref, send_sems, recv_sems, local_sem):
        me = _place()
        peers = [_flip(me, 4), _flip(me, 2), _flip(me, 6)]
        mine = pltpu.make_async_copy(x_ref.at[_chip_of(me)], out_ref.at[_chip_of(me)], local_sem)

        def copy(j, src_chip, dst_chip):
            return pltpu.make_async_remote_copy(
                src_ref=x_ref.at[src_chip], dst_ref=out_ref.at[dst_chip], send_sem=send_sems.at[j],
                recv_sem=recv_sems.at[j], device_id=peers[j], device_id_type=MESH)

        sends = [copy(j, _chip_of(peer), _chip_of(me)) for j, peer in enumerate(peers)]

        def start():
            mine.start()
            for cp in sends:
                cp.start()

        def finish():
            for j, peer in enumerate(peers):
                copy(j, _chip_of(me), _chip_of(peer)).wait_recv()
            for cp in sends:
                cp.wait_send()
            mine.wait()

        return start, finish


def exchange_alone(exchange, name):
    def body(x_ref, out_ref, *sems):
        start, finish = exchange.bind(x_ref, out_ref, *sems)
        start()
        finish()

    return pl.pallas_call(body, name=name, out_shape=exchange.out_shape, in_specs=[ANY], out_specs=ANY,
                          scratch_shapes=list(exchange.scratch))(exchange.operand)


def pair_exchange(blocks, name):
    _, rows, width = blocks.shape

    def body(x_ref, theirs_ref, send_sems, recv_sems):
        me = _place()
        remote = [pltpu.make_async_remote_copy(
            src_ref=x_ref.at[2 * q + 1 - me[2]], dst_ref=theirs_ref.at[q], send_sem=send_sems.at[q], recv_sem=recv_sems.at[q],
            device_id=_flip(me, 1), device_id_type=MESH) for q in range(4)]
        for cp in remote:
            cp.start()
        for cp in remote:
            cp.wait()

    return pl.pallas_call(
        body, name=name, out_shape=jax.ShapeDtypeStruct((4, rows, width), blocks.dtype), in_specs=[ANY], out_specs=ANY,
        scratch_shapes=[pltpu.SemaphoreType.DMA((4,)), pltpu.SemaphoreType.DMA((4,))])(blocks)


def pair_add(blocks, theirs, name):
    n, rows, width = theirs.shape
    tile = _pick(rows, (432, 256, 128, 64, 32, 16))
    spec = pl.BlockSpec((None, tile, width), lambda q, i: (q, i, 0))
    south = pl.BlockSpec((None, None, tile, width), lambda q, i: (q, 0, i, 0))
    north = pl.BlockSpec((None, None, tile, width), lambda q, i: (q, 1, i, 0))

    def body(s_ref, n_ref, b_ref, o_ref):
        mine = jnp.where(lax.axis_index("c") == 0, s_ref[...], n_ref[...])
        o_ref[...] = (mine.astype(f32) + b_ref[...].astype(f32)).astype(o_ref.dtype)

    by_core = blocks.reshape(n, 2, rows, width)
    return pl.pallas_call(body, name=name, grid=(n, rows // tile), in_specs=[south, north, spec], out_specs=spec,
                          out_shape=jax.ShapeDtypeStruct(theirs.shape, theirs.dtype), compiler_params=_params(2))(by_core, by_core, theirs)


def sum_slots(blocks, name):
    n, rows, width = blocks.shape
    tile = _pick(rows, (432, 256, 128, 64, 32, 16))

    def body(x_ref, o_ref):
        total = x_ref[0].astype(f32)
        for s in range(1, n):
            total = total + x_ref[s].astype(f32)
        o_ref[...] = total

    return pl.pallas_call(
        body, name=name, grid=(rows // tile,), in_specs=[pl.BlockSpec((n, tile, width), lambda i: (0, i, 0))],
        out_specs=pl.BlockSpec((tile, width), lambda i: (i, 0)), out_shape=_sds((rows, width)), compiler_params=_params(1))(blocks)


def all_reduce_small(x, name):
    rows, width = x.shape

    def body(x_ref, o_ref, land, send_sems, recv_sems):
        me = _place()
        copies = []
        for k in range(1, N_DEV):
            peer = _flip(me, k)
            copies.append(pltpu.make_async_remote_copy(
                src_ref=x_ref, dst_ref=land.at[_slot(me)], send_sem=send_sems.at[k - 1], recv_sem=recv_sems.at[k - 1],
                device_id=peer, device_id_type=MESH))
        for cp in copies:
            cp.start()
        land[_slot(me)] = x_ref[...]
        for k in range(1, N_DEV):
            peer = _flip(me, k)
            pltpu.make_async_remote_copy(
                src_ref=x_ref, dst_ref=land.at[_slot(peer)], send_sem=send_sems.at[k - 1], recv_sem=recv_sems.at[k - 1],
                device_id=peer, device_id_type=MESH).wait_recv()
        total = land[0]
        for s in range(1, N_DEV):
            total = total + land[s]
        o_ref[...] = total
        for cp in copies:
            cp.wait_send()

    return pl.pallas_call(
        body, name=name, out_shape=_sds((rows, width)), in_specs=[pl.BlockSpec(memory_space=pltpu.VMEM)],
        out_specs=pl.BlockSpec(memory_space=pltpu.VMEM),
        scratch_shapes=[pltpu.VMEM((N_DEV, rows, width), f32), pltpu.SemaphoreType.DMA((7,)), pltpu.SemaphoreType.DMA((7,))],
    )(x)


def _slab(rows):
    return -(-rows // 16) * 16


def _pack_big(shards):
    parts = []
    for name, rows in BIG:
        part = shards[name].astype(bf16).reshape(DEPTH, rows, D_MODEL)
        parts.append(jnp.pad(part, ((0, 0), (0, _slab(rows) - rows), (0, 0))))
    return jnp.concatenate(parts, axis=1)


def _unpack_gathered(gathered):
    full, at = {}, 0
    for name, rows in BIG:
        part = gathered[:, at:at + rows, :]
        at += _slab(rows)
        if name in COL_SHARDED:
            full[name] = part.reshape(N_DEV, D_MODEL, rows).transpose(1, 0, 2).reshape(D_MODEL, N_DEV * rows)
        else:
            full[name] = part.reshape(N_DEV * rows, D_MODEL)
    w_in = full.pop("w_in")
    full["w_main"] = jnp.concatenate([w_in[:, :AB_AT], w_in[:, AB_AT + 2 * GDN_HEADS:]], axis=1)
    full["w_ab"] = jnp.pad(w_in[:, AB_AT:AB_AT + 2 * GDN_HEADS], ((0, 0), (0, LANES - 2 * GDN_HEADS)))
    return full


def _pack_grads(grads):
    grads = dict(grads)
    main, ab = grads.pop("w_main"), grads.pop("w_ab")
    grads["w_in"] = jnp.concatenate([main[:, :AB_AT], ab[:, :2 * GDN_HEADS], main[:, AB_AT:]], axis=1)
    parts = []
    for name, rows in BIG:
        g = grads[name]
        if name in COL_SHARDED:
            g = g.reshape(D_MODEL, N_DEV, rows).transpose(1, 0, 2)
        parts.append(jnp.pad(g.reshape(N_DEV, rows, D_MODEL), ((0, 0), (0, _slab(rows) - rows), (0, 0))))
    return jnp.concatenate(parts, axis=1)


def _unpack_shard(layers):
    flat = jnp.stack(layers)
    out, at = {}, 0
    for name, rows in BIG:
        part = flat[:, at:at + rows, :]
        at += _slab(rows)
        out[name] = part.reshape(DEPTH, D_MODEL, rows) if name in COL_SHARDED else part
    return out


def _rows_of(flat_len):
    return -(-flat_len // (8 * D_MODEL)) * 8


def _pack_small(parts):
    flat = jnp.concatenate([p.reshape(-1) for p in parts])
    rows = _rows_of(flat.shape[0])
    flat = jnp.pad(flat, (0, rows * D_MODEL - flat.shape[0]))
    return flat.reshape(rows, D_MODEL)


def _unpack_small(packed, like):
    flat, out, at = packed.reshape(-1), [], 0
    for p in like:
        out.append(flat[at:at + p.size].reshape(p.shape))
        at += p.size
    return out


def _rope_tables(positions):
    inv_freq = jnp.float32(ROPE_THETA) ** (-jnp.arange(0, ROPE_DIM, 2, dtype=f32) / ROPE_DIM)
    ang = positions.astype(f32)[:, None] * inv_freq
    cos, sin = jnp.cos(ang), jnp.sin(ang)
    rest = ATTN_HEAD_DIM - ROPE_DIM
    cos_h = jnp.concatenate([cos, cos, jnp.ones((cos.shape[0], rest), f32)], axis=1)
    sin_h = jnp.concatenate([-sin, sin, jnp.zeros((sin.shape[0], rest), f32)], axis=1)
    return jnp.tile(cos_h, (1, ATTN_HEADS)), jnp.tile(sin_h, (1, ATTN_HEADS))


HEAD_PARAMS = ("norm_mix_pre", "w_main", "w_ab", "conv_short", "conv_gdn", "gdn_a_log", "gdn_dt_bias")


def _layer_head(h, p, cos_t, sin_t):
    hn = rms_norm(h, p["norm_mix_pre"][None], "norm_mix_pre")
    proj = _linear(hn, p["w_main"], "w_main")
    ab = _linear(hn, p["w_ab"], "w_ab")
    aw, cw, gw = ATTN_WIDTH, CONV_WIDTH, GDN_WIDTH
    aq, ak, av, cb, cc, cx, gqkv, gate = _split_cols(proj, (aw, aw, aw, cw, cw, cw, 3 * gw, gw))
    y_attn = dilated_attention(rope(aq, cos_t, sin_t, ATTN_HEAD_DIM ** -0.5, "rope_q"), rope(ak, cos_t, sin_t, 1.0, "rope_k"),
                               av, "attn")
    y_conv = short_conv(cb, cc, cx, p["conv_short"], "short_conv")
    qkv = gdn_pre(gqkv, p["conv_gdn"], "gdn_pre")
    pv = jnp.zeros((8, LANES), f32).at[0, :GDN_HEADS].set(p["gdn_a_log"]).at[1, :GDN_HEADS].set(p["gdn_dt_bias"])
    g, beta = gate_beta(ab, pv, "gate_beta")
    return (*_split_cols(qkv, (gw, gw, gw)), g, beta), (gate, y_attn, y_conv)


def _layer_tail(h, o, gate, y_attn, y_conv, p, mem):
    y_gdn = gdn_post(o, gate, p["gdn_norm"][None], "gdn_post")
    mix = _linear(jnp.concatenate([y_attn, y_conv, y_gdn], axis=1), p["w_out"], "w_out")
    h = add_norm(h, mix, p["norm_mix_post"][None], "norm_mix_post")

    hn = rms_norm(h, p["norm_xattn_pre"][None], "norm_xattn_pre")
    qx = _linear(hn, p["w_xq"], "w_xq")
    kv = _linear(rms_norm(mem, p["norm_mem"][None], "norm_mem"), p["w_xkv"], "w_xkv")
    xa = _linear(cross_attention(qx, kv, "xattn"), p["w_xo"], "w_xo")
    h = add_norm(h, xa, p["norm_xattn_post"][None], "norm_xattn_post")

    hn = rms_norm(h, p["norm_ffn_pre"][None], "norm_ffn_pre")
    act = swiglu(_linear(hn, p["w_gate_up"], "w_gate_up"), "swiglu")
    return add_norm(h, _linear(act, p["w_down"], "w_down"), p["norm_ffn_post"][None], "norm_ffn_post")


def _forward_backward(x, packed, small, mem, cos_t, sin_t, target):
    h = x
    gathered = exchange_alone(Gather(packed[0]), "gather_first")
    saved = []
    for layer in range(DEPTH):
        p = {**_unpack_gathered(gathered), **{n: t[layer] for n, t in small.items()}}
        head_p = {n: p[n] for n in HEAD_PARAMS}
        tail_p = {n: t for n, t in p.items() if n not in HEAD_PARAMS}
        (rule_in, rest), head_vjp = jax.vjp(lambda h, hp: _layer_head(h, hp, cos_t, sin_t), h, head_p)
        carried = Gather(packed[layer + 1]) if layer + 1 < DEPTH else None
        o, s_in, inv, *landed = delta_rule_fwd(*rule_in, "delta_rule_fwd", carried)
        if carried:
            gathered = landed[0]
        h, tail_vjp = jax.vjp(lambda h, o, rest, tp: _layer_tail(h, o, *rest, tp, mem), h, o, rest, tail_p)
        saved.append((head_vjp, tail_vjp, rule_in, s_in, inv))

    loss, dh = jax.value_and_grad(lambda y: loss_rows(y, target, "loss"))(h)

    big_rows, small_grads, pending = [None] * DEPTH, [None] * DEPTH, None
    for layer in reversed(range(DEPTH)):
        head_vjp, tail_vjp, rule_in, s_in, inv = saved[layer]
        dh_tail, do, d_rest, d_tail_p = tail_vjp(dh)
        carried = ChipExchange(pending) if pending is not None else None
        *d_rule_in, = delta_rule_bwd(*rule_in, s_in, inv, do, "delta_rule_bwd", carried)
        if carried:
            big_rows[layer + 1] = sum_slots(d_rule_in.pop(), "sum_grads")
        dh_head, d_head_p = head_vjp((tuple(d_rule_in), d_rest))
        dh = dh_tail + dh_head
        d_p = {**d_head_p, **d_tail_p}
        small_grads[layer] = {n: d_p[n] for n in small}
        blocks = _pack_grads(d_p)
        pending = pair_add(blocks, pair_exchange(blocks, "pair_exchange"), "pair_add")
    big_rows[0] = sum_slots(exchange_alone(ChipExchange(pending), "exchange_last"), "sum_grads")
    return loss, dh, big_rows, small_grads


def kernel(x, mem, positions, norm_mix_pre, norm_mix_post, w_in, conv_short, conv_gdn, gdn_a_log, gdn_dt_bias, gdn_norm, w_out, norm_mem, norm_xattn_pre, norm_xattn_post, w_xq, w_xkv, w_xo, norm_ffn_pre, norm_ffn_post, w_gate_up, w_down, loss_target, m_norm_mix_pre, m_norm_mix_post, m_w_in, m_conv_short, m_conv_gdn, m_gdn_a_log, m_gdn_dt_bias, m_gdn_norm, m_w_out, m_norm_mem, m_norm_xattn_pre, m_norm_xattn_post, m_w_xq, m_w_xkv, m_w_xo, m_norm_ffn_pre, m_norm_ffn_post, m_w_gate_up, m_w_down, v_norm_mix_pre, v_norm_mix_post, v_w_in, v_conv_short, v_conv_gdn, v_gdn_a_log, v_gdn_dt_bias, v_gdn_norm, v_w_out, v_norm_mem, v_norm_xattn_pre, v_norm_xattn_post, v_w_xq, v_w_xkv, v_w_xo, v_norm_ffn_pre, v_norm_ffn_post, v_w_gate_up, v_w_down):
    given = dict(locals())
    weights = {n: given[n] for n in WEIGHTS}
    me = _slot(_place())

    def in_place(shard):
        full = jnp.zeros(shard.shape[:-1] + (shard.shape[-1] * N_DEV,), f32)
        return lax.dynamic_update_slice_in_dim(full, shard, me * shard.shape[-1], axis=shard.ndim - 1)

    placed = [in_place(conv_short), in_place(conv_gdn)]
    conv_short_full, conv_gdn_full = _unpack_small(all_reduce_small(_pack_small(placed), "gather_conv"), placed)
    small = {n: weights[n] for n in NORMS + ("gdn_a_log", "gdn_dt_bias", "gdn_norm")}
    small["conv_short"], small["conv_gdn"] = conv_short_full, conv_gdn_full

    cos_t, sin_t = _rope_tables(positions[0])
    loss, grad_x, big_rows, small_layers = _forward_backward(
        x[0], _pack_big(weights), small, mem[0], cos_t, sin_t, loss_target[0])
    grads = _unpack_shard(big_rows)

    names = sorted(small)
    parts = [jnp.stack([layer[n] for layer in small_layers]) for n in names] + [loss.reshape(1)]
    reduced = _unpack_small(all_reduce_small(_pack_small(parts), "reduce_small"), parts)
    loss = reduced[-1][0]
    for n, g in zip(names, reduced[:-1]):
        if n in ("conv_short", "conv_gdn"):
            width = weights[n].shape[-1]
            g = lax.dynamic_slice_in_dim(g, me * width, width, axis=g.ndim - 1)
        grads[n] = g

    delta, new_m, new_v = {}, {}, {}
    for n in WEIGHTS:
        delta[n], new_m[n], new_v[n] = adamw(weights[n], grads[n], given["m_" + n], given["v_" + n], "adamw_" + n)
    return (loss, grad_x[None], *[grads[n] for n in WEIGHTS], *[delta[n] for n in WEIGHTS],
            *[new_m[n] for n in WEIGHTS], *[new_v[n] for n in WEIGHTS])
```

```python
import functools

import jax
import jax.numpy as jnp
from jax import lax
from jax.experimental import pallas as pl
from jax.experimental.pallas import tpu as pltpu

f32 = jnp.float32
bf16 = jnp.bfloat16
HIGHEST = lax.Precision.HIGHEST
MESH = pl.DeviceIdType.MESH

N_DEV = 8
DEPTH = 4
D_MODEL = 1024
EPS = 1e-6
ATTN_HEADS, ATTN_HEAD_DIM = 4, 64
ATTN_WIDTH = ATTN_HEADS * ATTN_HEAD_DIM
DILATIONS = (1, 4, 16)
QB = 128
ROPE_THETA = 500000.0
ROPE_DIM = ATTN_HEAD_DIM // 4
CONV_WIDTH = 256
GDN_HEADS, GDN_HEAD_DIM = 4, 128
GDN_WIDTH = GDN_HEADS * GDN_HEAD_DIM
GDN_CHUNK = 64
XATTN_HEADS, XATTN_HEAD_DIM = 4, 256
FFN_HIDDEN = 2816
IN_WIDTH = 3592
AB_AT = 3 * ATTN_WIDTH + 3 * CONV_WIDTH + 3 * GDN_WIDTH
MAIN_WIDTH = IN_WIDTH - 2 * GDN_HEADS
LANES = 128
ROW_TILE = 256
VMEM_LIMIT = 56 * 1024 * 1024

ADAM_LR, ADAM_B1, ADAM_B2, ADAM_EPS, ADAM_WD, ADAM_STEP = 0.001, 0.9, 0.999, 1e-08, 0.01, 10

BIG = (("w_in", 449), ("w_out", 128), ("w_xq", 128), ("w_xkv", 256), ("w_xo", 128), ("w_gate_up", 704), ("w_down", 352))
BIG_ROWS = 2160
COL_SHARDED = ("w_in", "w_xkv", "w_gate_up")
NORMS = ("norm_mix_pre", "norm_mix_post", "norm_mem", "norm_xattn_pre", "norm_xattn_post", "norm_ffn_pre", "norm_ffn_post")
WEIGHTS = ("norm_mix_pre", "norm_mix_post", "w_in", "conv_short", "conv_gdn", "gdn_a_log", "gdn_dt_bias", "gdn_norm", "w_out",
           "norm_mem", "norm_xattn_pre", "norm_xattn_post", "w_xq", "w_xkv", "w_xo", "norm_ffn_pre", "norm_ffn_post",
           "w_gate_up", "w_down")


def _params(n_grid):
    return pltpu.CompilerParams(dimension_semantics=("arbitrary",) * n_grid, vmem_limit_bytes=VMEM_LIMIT)


def _pick(n, cands):
    for c in cands:
        if n % c == 0:
            return c
    return n


MXU_FLOPS = 9.0e14
HBM_BYTES_PER_S = 2.5e12
VMEM_RMW_BYTES_PER_S = 7.0e12
STEP_S = 0.4e-6
MATMUL_VMEM = 44 * 1024 * 1024


def _tiles(m, n, k, sa, sb, so, tn=None):
    def divisors(d):
        return sorted({d // s for s in range(1, d // LANES + 1) if d % s == 0 and (d // s) % LANES == 0}, reverse=True)

    best = None
    for tk in divisors(k):
        nk = k // tk
        for tm in divisors(m):
            for tn_ in [tn] if tn else divisors(n):
                per_step = tm * tk * sa + tk * tn_ * sb + tm * tn_ * so
                vmem = 2 * per_step + (tm * tn_ * 4 if nk > 1 else 0)
                vmem += (tm * tk * 2 if sa == 4 else 0) + (tk * tn_ * 2 if sb == 4 else 0) + tm * tn_ * 4
                if vmem > MATMUL_VMEM:
                    continue
                moved = m * k * sa * (1 if nk == 1 else n // tn_) + k * n * sb * (1 if nk == 1 and n == tn_ else m // tm) + m * n * so
                busy = 2 * m * n * k / MXU_FLOPS + (m * n * 8 * nk / VMEM_RMW_BYTES_PER_S if nk > 1 else 0)
                cost = max(moved / HBM_BYTES_PER_S, busy) + per_step / HBM_BYTES_PER_S + (m // tm) * (n // tn_) * nk * STEP_S
                if best is None or cost < best[0]:
                    best = (cost, tm, tn_, tk)
    return best[1:]


def _mm(a, b, ta, tb, out_dtype, name, finish=None):
    m, k = (a.shape[1], a.shape[0]) if ta else a.shape
    n = b.shape[0] if tb else b.shape[1]
    tm, tn, tk = _tiles(m, n, k, a.dtype.itemsize, b.dtype.itemsize, jnp.dtype(out_dtype).itemsize, finish and finish[0])
    nk = k // tk
    if finish:
        assert nk == 1
        _, extra, results, function = finish
        dims = (((0 if ta else 1,), (1 if tb else 0,)), ((), ()))

        def finish_body(a_ref, b_ref, *refs):
            p = lax.dot_general(a_ref[...].astype(bf16), b_ref[...].astype(bf16), dims, preferred_element_type=f32)
            outs = function(p, *[r[...] for r in refs[:len(extra)]])
            for r, o in zip(refs[len(extra):], outs):
                r[...] = o.astype(r.dtype)

        return pl.pallas_call(
            finish_body, name=name, grid=(m // tm, n // tn),
            in_specs=[pl.BlockSpec((tk, tm), lambda i, j: (0, i)) if ta else pl.BlockSpec((tm, tk), lambda i, j: (i, 0)),
                      pl.BlockSpec((tn, tk), lambda i, j: (j, 0)) if tb else pl.BlockSpec((tk, tn), lambda i, j: (0, j))]
            + [pl.BlockSpec((tm, cols), lambda i, j: (i, j)) for _, cols in extra],
            out_specs=[pl.BlockSpec((tm, cols), lambda i, j: (i, j)) for _, cols in results],
            out_shape=[jax.ShapeDtypeStruct((m, n // tn * cols), dt) for dt, cols in results],
            compiler_params=_params(2))(a, b, *[x for x, _ in extra])
    a_spec = pl.BlockSpec((tk, tm), lambda i, j, kk: (kk, i)) if ta else pl.BlockSpec((tm, tk), lambda i, j, kk: (i, kk))
    b_spec = pl.BlockSpec((tn, tk), lambda i, j, kk: (j, kk)) if tb else pl.BlockSpec((tk, tn), lambda i, j, kk: (kk, j))
    dims = (((0 if ta else 1,), (1 if tb else 0,)), ((), ()))

    def body(a_ref, b_ref, o_ref, *acc):
        kk = pl.program_id(2)
        p = lax.dot_general(a_ref[...].astype(bf16), b_ref[...].astype(bf16), dims, preferred_element_type=f32)
        if nk == 1:
            o_ref[...] = p.astype(o_ref.dtype)
            return
        acc_ref, = acc

        @pl.when(kk == 0)
        def _():
            acc_ref[...] = p

        @pl.when(kk > 0)
        def _():
            acc_ref[...] += p

        @pl.when(kk == nk - 1)
        def _():
            o_ref[...] = acc_ref[...].astype(o_ref.dtype)

    return pl.pallas_call(
        body, name=name, grid=(m // tm, n // tn, nk), in_specs=[a_spec, b_spec],
        out_specs=pl.BlockSpec((tm, tn), lambda i, j, kk: (i, j)), out_shape=jax.ShapeDtypeStruct((m, n), out_dtype),
        scratch_shapes=[pltpu.VMEM((tm, tn), f32)] if nk > 1 else [], compiler_params=_params(3))(a, b)


def _linear(x, w, name):
    @jax.custom_vjp
    def lin(x, w):
        return _mm(x, w, False, False, f32, name + "_y")

    def lin_f(x, w):
        return _mm(x, w, False, False, f32, name + "_y"), (x, w)

    def lin_b(res, dy):
        x, w = res
        return _mm(dy, w, False, True, f32, name + "_dx"), _mm(x, dy, True, False, bf16, name + "_dw")

    lin.defvjp(lin_f, lin_b)
    return lin(x, w)


GATE_UP_TILE = 512


def _interleave_gate_up(w, undo=False):
    k, two_f = w.shape
    half = GATE_UP_TILE // 2
    nb = two_f // GATE_UP_TILE
    if undo:
        return w.reshape(k, nb, 2, half).transpose(0, 2, 1, 3).reshape(k, two_f)
    return w.reshape(k, 2, nb, half).transpose(0, 2, 1, 3).reshape(k, two_f)


def swiglu_ffn(hn, w_gate_up, w_down, name):
    half = GATE_UP_TILE // 2

    def act_of(p):
        return p, jax.nn.silu(p[:, :half]) * p[:, half:]

    def d_gate_up_of(d_act, gate_up):
        g, u = gate_up[:, :half].astype(f32), gate_up[:, half:].astype(f32)
        sig = jax.nn.sigmoid(g)
        return (jnp.concatenate([d_act * u * sig * (1.0 + g * (1.0 - sig)), d_act * g * sig], axis=1),)

    def forward(hn, w_gate_up, w_down):
        gate_up, act = _mm(hn, w_gate_up, False, False, bf16, name + "_act",
                           (GATE_UP_TILE, [], [(bf16, GATE_UP_TILE), (bf16, half)], act_of))
        return _mm(act, w_down, False, False, f32, name + "_y"), (hn, w_gate_up, w_down, gate_up, act)

    def backward(res, dy):
        hn, w_gate_up, w_down, gate_up, act = res
        d_gate_up, = _mm(dy, w_down, False, True, bf16, name + "_dact",
                         (half, [(gate_up, GATE_UP_TILE)], [(bf16, GATE_UP_TILE)], d_gate_up_of))
        return (_mm(d_gate_up, w_gate_up, False, True, f32, name + "_dx"), _mm(hn, d_gate_up, True, False, bf16, name + "_dw1"),
                _mm(act, dy, True, False, bf16, name + "_dw2"))

    @jax.custom_vjp
    def op(hn, w_gate_up, w_down):
        return forward(hn, w_gate_up, w_down)[0]

    op.defvjp(forward, backward)
    return op(hn, w_gate_up, w_down)


def _split_cols(x, widths):
    edges = [sum(widths[:i]) for i in range(len(widths) + 1)]

    def cut(x):
        return tuple(x[:, a:b] for a, b in zip(edges[:-1], edges[1:]))

    @jax.custom_vjp
    def split(x):
        return cut(x)

    split.defvjp(lambda x: (cut(x), None), lambda _, cts: (jnp.concatenate(cts, axis=1),))
    return split(x)


def _block_op(name, f, grid, in_specs, out_defs, arrays, diff, acc=None, gdefs=None):
    acc, gdefs = acc or {}, gdefs or {}
    n_in, n_out, n_grid = len(in_specs), len(out_defs), len(grid)

    def fwd_call(*xs):
        def body(*refs):
            outs = f(*[r[...] for r in refs[:n_in]])
            for r, o in zip(refs[n_in:], outs):
                r[...] = o.astype(r.dtype)

        return pl.pallas_call(
            body, name=name + "_fwd", grid=grid, in_specs=in_specs, out_specs=[d[1] for d in out_defs],
            out_shape=[d[0] for d in out_defs], compiler_params=_params(n_grid))(*xs)

    def bwd_call(*xs_and_cts):
        def body(*refs):
            xs = [r[...] for r in refs[:n_in]]
            cts = tuple(r[...] for r in refs[n_in:n_in + n_out])

            def of_diff(*dx):
                full = list(xs)
                for i, v in zip(diff, dx):
                    full[i] = v
                return tuple(f(*full))

            _, vjp = jax.vjp(of_diff, *[xs[i] for i in diff])
            grads = vjp(cts)
            for i, g, r in zip(diff, grads, refs[n_in + n_out:]):
                if i in acc:
                    first = functools.reduce(jnp.logical_and, [pl.program_id(a) == 0 for a in acc[i]])

                    @pl.when(first)
                    def _(r=r):
                        r[...] = jnp.zeros_like(r)

                    r[...] += g.astype(r.dtype)
                else:
                    r[...] = g.astype(r.dtype)

        g_defs = [gdefs.get(i, (jax.ShapeDtypeStruct(arrays[i].shape, f32), in_specs[i])) for i in diff]
        return pl.pallas_call(
            body, name=name + "_bwd", grid=grid, in_specs=list(in_specs) + [d[1] for d in out_defs],
            out_specs=[d[1] for d in g_defs], out_shape=[d[0] for d in g_defs], compiler_params=_params(n_grid))(*xs_and_cts)

    return fwd_call, bwd_call


def _simple_op(name, f, grid, in_specs, out_defs, arrays, diff, acc=None):
    fwd_call, bwd_call = _block_op(name, f, grid, in_specs, out_defs, arrays, diff, acc)

    @jax.custom_vjp
    def op(*xs):
        return tuple(fwd_call(*xs))

    def op_f(*xs):
        return tuple(fwd_call(*xs)), xs

    def op_b(xs, cts):
        grads = bwd_call(*xs, *cts)
        out = [jnp.zeros_like(x) for x in xs]
        for i, g in zip(diff, grads):
            out[i] = g
        return tuple(out)

    op.defvjp(op_f, op_b)
    return op(*arrays)


def _rows(width, tile=ROW_TILE):
    return pl.BlockSpec((tile, width), lambda i: (i, 0))


def _whole(shape):
    return pl.BlockSpec(shape, lambda *_: (0,) * len(shape))


def _sds(shape):
    return jax.ShapeDtypeStruct(shape, f32)


def _rms(x, w):
    return x * lax.rsqrt(jnp.mean(x * x, axis=-1, keepdims=True) + EPS) * w


def rms_norm(x, w, name):
    r, d = x.shape
    return _simple_op(name, lambda x, w: (_rms(x, w),), (r // ROW_TILE,), [_rows(d), _whole((1, d))],
                      [(_sds((r, d)), _rows(d))], (x, w), (0, 1), {1: (0,)})[0]


def add_norm(h, y, w, name):
    r, d = h.shape
    return _simple_op(name, lambda h, y, w: (h + _rms(y, w),), (r // ROW_TILE,), [_rows(d), _rows(d), _whole((1, d))],
                      [(_sds((r, d)), _rows(d))], (h, y, w), (0, 1, 2), {2: (0,)})[0]


def _swap8(x):
    def raw(x):
        lane = lax.broadcasted_iota(jnp.int32, x.shape, 1) % ATTN_HEAD_DIM
        half = ROPE_DIM // 2
        up = pltpu.roll(x, x.shape[1] - half, axis=1)
        down = pltpu.roll(x, half, axis=1)
        return jnp.where(lane < half, up, jnp.where(lane < ROPE_DIM, down, 0.0))

    @jax.custom_vjp
    def swap(x):
        return raw(x)

    swap.defvjp(lambda x: (raw(x), None), lambda _, g: (raw(g),))
    return swap(x)


def rope(x, cos_t, sin_t, scale, name):
    r, d = x.shape
    return _simple_op(name, lambda x, c, s: ((x * c + _swap8(x) * s) * scale,), (r // ROW_TILE,), [_rows(d)] * 3,
                      [(_sds((r, d)), _rows(d))], (x, cos_t, sin_t), (0,))[0]


def _shift_rows(x, k):
    n = x.shape[0]

    def down(x):
        row = lax.broadcasted_iota(jnp.int32, x.shape, 0)
        return jnp.where(row >= k, pltpu.roll(x, k, axis=0), 0.0)

    def up(x):
        row = lax.broadcasted_iota(jnp.int32, x.shape, 0)
        return jnp.where(row < n - k, pltpu.roll(x, n - k, axis=0), 0.0)

    @jax.custom_vjp
    def shift(x):
        return down(x)

    shift.defvjp(lambda x: (down(x), None), lambda _, g: (up(g),))
    return shift(x)


def _causal_conv(x, w):
    taps = w.shape[0]
    y = x * w[taps - 1:taps, :]
    for j in range(taps - 1):
        y = y + _shift_rows(x, taps - 1 - j) * w[j:j + 1, :]
    return y


def _cols(rows, at=0):
    return pl.BlockSpec((rows, LANES), lambda j: (0, at + j))


def short_conv(cb, cc, cx, w, name):
    s, c = cb.shape
    taps = w.shape[0]
    return _simple_op(name, lambda b, c_, x, w: (b * _causal_conv(c_ * x, w),), (c // LANES,),
                      [_cols(s)] * 3 + [_cols(taps)], [(_sds((s, c)), _cols(s))], (cb, cc, cx, w), (0, 1, 2, 3))[0]


def gdn_pre(qkv, w, name):
    s, c = qkv.shape
    taps = w.shape[0]

    def f(x, w):
        j = pl.program_id(0)
        y = jax.nn.silu(_causal_conv(x, w))
        normed = y * lax.rsqrt(jnp.sum(y * y, axis=-1, keepdims=True) + EPS)
        scale = jnp.where(j < GDN_HEADS, GDN_HEAD_DIM ** -0.5, 1.0).astype(f32)
        return (jnp.where(j < 2 * GDN_HEADS, normed * scale, y),)

    return _simple_op(name, f, (c // LANES,), [_cols(s), _cols(taps)], [(_sds((s, c)), _cols(s))], (qkv, w), (0, 1))[0]


def gate_beta(ab, pv, name):
    s = ab.shape[0]

    def f(ab, pv):
        row = lax.broadcasted_iota(jnp.int32, (LANES, GDN_WIDTH), 0)
        head = lax.broadcasted_iota(jnp.int32, (LANES, GDN_WIDTH), 1) // GDN_HEAD_DIM
        spread_a = (row == head).astype(f32)
        spread_b = (row == head + GDN_HEADS).astype(f32)
        a = jnp.dot(ab, spread_a, precision=HIGHEST, preferred_element_type=f32)
        b = jnp.dot(ab, spread_b, precision=HIGHEST, preferred_element_type=f32)
        p = jnp.dot(pv, spread_a, precision=HIGHEST, preferred_element_type=f32)
        g = -jnp.exp(p[0:1, :]) * jax.nn.softplus(a + p[1:2, :])
        return g, jax.nn.sigmoid(b)

    outs = [(_sds((s, GDN_WIDTH)), _rows(GDN_WIDTH))] * 2
    return _simple_op(name, f, (s // ROW_TILE,), [_rows(LANES), _whole((8, LANES))], outs, (ab, pv), (0, 1), {1: (0,)})


def gdn_post(o, gate, w, name):
    s, c = o.shape
    spec = pl.BlockSpec((ROW_TILE, LANES), lambda i, j: (i, j))
    return _simple_op(name, lambda o, g, w: (_rms(o, w) * jax.nn.silu(g),), (s // ROW_TILE, c // LANES),
                      [spec, spec, _whole((1, LANES))], [(_sds((s, c)), spec)], (o, gate, w), (0, 1, 2), {2: (0, 1)})[0]


def attn_merge(outs, lses, name):
    s, c = outs[0].shape

    def f(o1, o2, o3, l1, l2, l3):
        m = lax.stop_gradient(jnp.maximum(jnp.maximum(l1, l2), l3))
        e1, e2, e3 = jnp.exp(l1 - m), jnp.exp(l2 - m), jnp.exp(l3 - m)
        return ((e1 * o1 + e2 * o2 + e3 * o3) / (e1 + e2 + e3),)

    return _simple_op(name, f, (s // ROW_TILE,), [_rows(c)] * 6, [(_sds((s, c)), _rows(c))], (*outs, *lses), tuple(range(6)))[0]


def loss_rows(y, target, name):
    s, d = y.shape
    nt = s // ROW_TILE

    def f(y, t):
        e = y - t
        part = 0.5 * jnp.sum(jnp.mean(e * e, axis=-1, keepdims=True), axis=0, keepdims=True)
        return (jnp.broadcast_to(part * (1.0 / (8 * LANES)), (8, LANES)),)

    out = _simple_op(name, f, (nt,), [_rows(d)] * 2, [(_sds((nt * 8, LANES)), pl.BlockSpec((8, LANES), lambda i: (i, 0)))],
                     (y, target), (0,))[0]
    return jnp.sum(out)


def _mxu(a, b, form):
    dims = {"nn": ((1,), (0,)), "nt": ((1,), (1,)), "tn": ((0,), (0,))}

    def raw(a, b, form):
        return lax.dot_general(a.astype(bf16), b.astype(bf16), (dims[form], ((), ())), preferred_element_type=f32)

    @jax.custom_vjp
    def prod(a, b):
        return raw(a, b, form)

    def prod_b(res, ct):
        a, b = res
        if form == "nn":
            return raw(ct, b, "nt"), raw(a, ct, "tn")
        if form == "nt":
            return raw(ct, b, "nn"), raw(ct, a, "tn")
        return raw(b, ct, "nt"), raw(a, ct, "nn")

    prod.defvjp(lambda a, b: (raw(a, b, form), (a, b)), prod_b)
    return prod(a, b)


def band_attention(q, k, v, nb, name):
    r, qb, width = q.shape
    dh = ATTN_HEAD_DIM

    def f(q, kp, kc, vp, vc):
        has_prev = (pl.program_id(0) % nb) > 0
        i = lax.broadcasted_iota(jnp.int32, (qb, qb), 0)
        j = lax.broadcasted_iota(jnp.int32, (qb, qb), 1)
        see_prev, see_cur = jnp.logical_and(j >= i, has_prev), j <= i
        outs, lses = [], []
        for hd in range(width // dh):
            at = slice(hd * dh, (hd + 1) * dh)
            sp = jnp.where(see_prev, _mxu(q[:, at], kp[:, at], "nt"), -jnp.inf)
            sc = jnp.where(see_cur, _mxu(q[:, at], kc[:, at], "nt"), -jnp.inf)
            m = lax.stop_gradient(jnp.maximum(jnp.max(sp, axis=-1, keepdims=True), jnp.max(sc, axis=-1, keepdims=True)))
            pp, pc = jnp.exp(sp - m), jnp.exp(sc - m)
            l = jnp.sum(pp, axis=-1, keepdims=True) + jnp.sum(pc, axis=-1, keepdims=True)
            outs.append(_mxu(pp / l, vp[:, at], "nn") + _mxu(pc / l, vc[:, at], "nn"))
            lses.append(jnp.broadcast_to(m + jnp.log(l), (qb, dh)))
        return jnp.concatenate(outs, axis=1), jnp.concatenate(lses, axis=1)

    blk = (None, qb, width)
    cur = pl.BlockSpec(blk, lambda b: (b, 0, 0))
    prev = pl.BlockSpec(blk, lambda b: (jnp.maximum(b - 1, 0), 0, 0))
    shape = _sds((r, qb, width))
    fwd_call, bwd_call = _block_op(name, f, (r,), [cur, prev, cur, prev, cur], [(shape, cur), (shape, cur)],
                                   (q, k, k, v, v), (0, 1, 2, 3, 4), gdefs={1: (shape, cur), 3: (shape, cur)})

    def to_prev(g):
        return jnp.concatenate([g[1:], jnp.zeros_like(g[:1])], axis=0)

    @jax.custom_vjp
    def op(q, k, v):
        return tuple(fwd_call(q, k, k, v, v))

    def op_b(res, cts):
        q, k, v = res
        dq, dkp, dkc, dvp, dvc = bwd_call(q, k, k, v, v, *cts)
        return dq, dkc + to_prev(dkp), dvc + to_prev(dvp)

    op.defvjp(lambda q, k, v: (tuple(fwd_call(q, k, k, v, v)), (q, k, v)), op_b)
    return op(q, k, v)


def dilated_attention(q, k, v, name):
    s = q.shape[0]
    outs, lses = [], []
    for d in DILATIONS:
        length = s // d
        nb = length // QB

        def to_residue(t):
            return t.reshape(length, d, ATTN_WIDTH).transpose(1, 0, 2).reshape(d * nb, QB, ATTN_WIDTH)

        def from_residue(t):
            return t.reshape(d, length, ATTN_WIDTH).transpose(1, 0, 2).reshape(s, ATTN_WIDTH)

        o, lse = band_attention(to_residue(q), to_residue(k), to_residue(v), nb, f"{name}_d{d}")
        outs.append(from_residue(o))
        lses.append(from_residue(lse))
    return attn_merge(outs, lses, name + "_merge")


def cross_attention(q, kv, name):
    s = q.shape[0]
    m = kv.shape[0]
    width = XATTN_HEADS * XATTN_HEAD_DIM
    tq = 512

    def f(q, k, v):
        sc = _mxu(q, k, "nt") * (XATTN_HEAD_DIM ** -0.5)
        mx = lax.stop_gradient(jnp.max(sc, axis=-1, keepdims=True))
        p = jnp.exp(sc - mx)
        return (_mxu(p / jnp.sum(p, axis=-1, keepdims=True), v, "nn"),)

    q_spec = pl.BlockSpec((tq, XATTN_HEAD_DIM), lambda a, i: (i, a))
    k_spec = pl.BlockSpec((m, XATTN_HEAD_DIM), lambda a, i: (0, a))
    v_spec = pl.BlockSpec((m, XATTN_HEAD_DIM), lambda a, i: (0, a + XATTN_HEADS))
    half = _sds((m, width))
    fwd_call, bwd_call = _block_op(name, f, (XATTN_HEADS, s // tq), [q_spec, k_spec, v_spec], [(_sds((s, width)), q_spec)],
                                   (q, kv, kv), (0, 1, 2), acc={1: (1,), 2: (1,)}, gdefs={1: (half, k_spec), 2: (half, k_spec)})

    @jax.custom_vjp
    def op(q, kv):
        return fwd_call(q, kv, kv)[0]

    def op_b(res, ct):
        q, kv = res
        dq, dk, dv = bwd_call(q, kv, kv, ct)
        return dq, jnp.concatenate([dk, dv], axis=1)

    op.defvjp(lambda q, kv: (fwd_call(q, kv, kv)[0], (q, kv)), op_b)
    return op(q, kv)


def _hi(a, b, form="nn"):
    dims = {"nn": ((1,), (0,)), "nt": ((1,), (1,)), "tn": ((0,), (0,))}[form]
    return lax.dot_general(a, b, (dims, ((), ())), precision=lax.Precision.HIGH, preferred_element_type=f32)


def _running_sum(g):
    def raw(x, form):
        c = x.shape[0]
        tri = (lax.broadcasted_iota(jnp.int32, (c, c), 0) >= lax.broadcasted_iota(jnp.int32, (c, c), 1)).astype(bf16)
        hi = x.astype(bf16)
        rest = x - hi.astype(f32)
        mid = rest.astype(bf16)
        low = (rest - mid.astype(f32)).astype(bf16)
        dims = (((1,) if form == "nn" else (0,), (0,)), ((), ()))
        return sum(lax.dot_general(tri, part, dims, preferred_element_type=f32) for part in (hi, mid, low))

    @jax.custom_vjp
    def run(x):
        return raw(x, "nn")

    run.defvjp(lambda x: (raw(x, "nn"), None), lambda _, ct: (raw(ct, "tn"),))
    return run(g)


def _unit_lower_inverse(a):
    c = a.shape[0]
    eye = (lax.broadcasted_iota(jnp.int32, (c, c), 0) == lax.broadcasted_iota(jnp.int32, (c, c), 1)).astype(f32)
    inv, power = eye - a, -a
    for _ in range(c.bit_length() - 2):
        power = _hi(power, power)
        inv = inv + _hi(inv, power)
    return inv


def _known_inverse(a, t):
    @jax.custom_vjp
    def inv(a, t):
        return t

    def inv_b(t, ct):
        return -_hi(_hi(t, ct, "tn"), t, "nt"), jnp.zeros_like(t)

    inv.defvjp(lambda a, t: (t, t), inv_b)
    return inv(a, t)


def _delta_chunk(q, k, v, g, beta, s0, known_inv=None):
    c = q.shape[0]
    i = lax.broadcasted_iota(jnp.int32, (c, c), 0)
    j = lax.broadcasted_iota(jnp.int32, (c, c), 1)
    causal, strict = i >= j, i > j
    dec = _running_sum(g)
    dec_i = dec[:, :c]
    rel = jnp.exp(jnp.where(causal, dec_i - dec_i.T, -jnp.inf))
    k_beta = k * beta
    a = jnp.where(strict, _mxu(k_beta, k, "nt") * rel, 0.0)
    inv = _unit_lower_inverse(a) if known_inv is None else _known_inverse(a, known_inv)
    e_dec = jnp.exp(dec)
    u = _hi(inv, v * beta)
    w = _hi(inv, k_beta * e_dec)
    attn = jnp.where(causal, _mxu(q, k, "nt") * rel, 0.0)
    total = jnp.sum(g, axis=0, keepdims=True)
    v_new = u - _mxu(w, s0, "nn")
    o = _mxu(q * e_dec, s0, "nn") + _mxu(attn, v_new, "nn")
    s1 = s0 * jnp.exp(total) + _mxu(k * jnp.exp(total - dec), v_new, "tn")
    return o, s1, inv


def _delta_rule_call(name, walk, n, in_specs, out_specs, out_shape, operands, exchange):
    n_in, n_out = len(in_specs), len(out_specs)

    def body(*refs):
        ins, outs = refs[:n_in], refs[n_in + bool(exchange):n_in + bool(exchange) + n_out]
        state = refs[n_in + n_out + 2 * bool(exchange)]
        step = pl.program_id(0)
        if exchange:
            start, finish = exchange.bind(refs[n_in], refs[n_in + 1 + n_out], *refs[n_in + n_out + 3:])
            pl.when(step == 0)(start)

        @pl.when(step == 0)
        def _():
            state[...] = jnp.zeros_like(state)

        walk(ins, outs, state)
        if exchange:
            pl.when(step == n - 1)(finish)

    more = [ANY] if exchange else []
    return pl.pallas_call(
        body, name=name, grid=(n,), in_specs=list(in_specs) + more, out_specs=list(out_specs) + more,
        out_shape=list(out_shape) + ([exchange.out_shape] if exchange else []),
        scratch_shapes=[pltpu.VMEM((GDN_HEAD_DIM, GDN_WIDTH), f32)] + (list(exchange.scratch) if exchange else []),
        compiler_params=_params(1))(*operands, *([exchange.operand] if exchange else []))


def _delta_heads():
    heads = [slice(hd * GDN_HEAD_DIM, (hd + 1) * GDN_HEAD_DIM) for hd in range(GDN_HEADS)]
    inv_at = [slice(hd * GDN_CHUNK, (hd + 1) * GDN_CHUNK) for hd in range(GDN_HEADS)]
    return heads, inv_at


def delta_rule_fwd(q, k, v, g, beta, name, exchange=None):
    s, width = q.shape
    c, dk = GDN_CHUNK, GDN_HEAD_DIM
    n = s // c
    heads, inv_at = _delta_heads()

    def walk(ins, outs, state):
        o_ref, s_in_ref, inv_ref = outs
        s_in_ref[...] = state[...]
        xs = [[r[:, hd] for r in (*ins, state)] for hd in heads]
        ys = [_delta_chunk(*x) for x in xs]
        for hd, at, (o, s1, inv) in zip(heads, inv_at, ys):
            o_ref[:, hd], state[:, hd], inv_ref[:, at] = o, s1, inv

    blk = pl.BlockSpec((c, width), lambda t: (t, 0))
    st = pl.BlockSpec((dk, width), lambda t: (t, 0))
    iv = pl.BlockSpec((c, GDN_HEADS * c), lambda t: (t, 0))
    return _delta_rule_call(name, walk, n, [blk] * 5, [blk, st, iv],
                            [_sds((s, width)), _sds((n * dk, width)), _sds((s, GDN_HEADS * c))], (q, k, v, g, beta), exchange)


def delta_rule_bwd(q, k, v, g, beta, s_in, inv, do, name, exchange=None):
    s, width = q.shape
    c, dk = GDN_CHUNK, GDN_HEAD_DIM
    n = s // c
    heads, inv_at = _delta_heads()

    def walk(ins, outs, dstate):
        q_ref, k_ref, v_ref, g_ref, b_ref, s_ref, inv_ref, do_ref = ins
        xs = [[r[:, hd] for r in (q_ref, k_ref, v_ref, g_ref, b_ref, s_ref)] for hd in heads]
        known = [inv_ref[:, at] for at in inv_at]
        cts = [(do_ref[:, hd], dstate[:, hd]) for hd in heads]
        grads = []
        for x, t, ct in zip(xs, known, cts):
            _, vjp = jax.vjp(lambda *y, t=t: _delta_chunk(*y, known_inv=t)[:2], *x)
            grads.append(vjp(ct))
        for hd, (*d_ins, ds0) in zip(heads, grads):
            for r, d in zip(outs, d_ins):
                r[:, hd] = d
            dstate[:, hd] = ds0

    blk = pl.BlockSpec((c, width), lambda t: (n - 1 - t, 0))
    st = pl.BlockSpec((dk, width), lambda t: (n - 1 - t, 0))
    iv = pl.BlockSpec((c, GDN_HEADS * c), lambda t: (n - 1 - t, 0))
    return _delta_rule_call(name, walk, n, [blk] * 5 + [st, iv, blk], [blk] * 5, [_sds((s, width))] * 5,
                            (q, k, v, g, beta, s_in, inv, do), exchange)


def adamw(w, g, m, v, name):
    shape = w.shape
    cols = shape[-1]
    rows = w.size // cols
    tile = _pick(rows, (512, 256, 128, 64, 32, 16, 8))
    spec = pl.BlockSpec((tile, cols), lambda i: (i, 0))

    def body(w_ref, g_ref, m_ref, v_ref, d_ref, nm_ref, nv_ref):
        grad = g_ref[...]
        nm = ADAM_B1 * m_ref[...] + (1.0 - ADAM_B1) * grad
        nv = ADAM_B2 * v_ref[...] + (1.0 - ADAM_B2) * (grad * grad)
        m_hat = nm / (1.0 - ADAM_B1 ** ADAM_STEP)
        v_hat = nv / (1.0 - ADAM_B2 ** ADAM_STEP)
        d_ref[...] = -ADAM_LR * (m_hat / (jnp.sqrt(v_hat) + ADAM_EPS) + ADAM_WD * w_ref[...])
        nm_ref[...] = nm
        nv_ref[...] = nv

    outs = pl.pallas_call(
        body, name=name, grid=(rows // tile,), in_specs=[spec] * 4, out_specs=[spec] * 3,
        out_shape=[_sds((rows, cols))] * 3, compiler_params=_params(1),
    )(*[t.reshape(rows, cols) for t in (w, g, m, v)])
    return tuple(t.reshape(shape) for t in outs)


def _place():
    return lax.axis_index("x"), lax.axis_index("y"), lax.axis_index("c")


def _flip(p, bits):
    return tuple(1 - v if (bits >> s) & 1 else v for v, s in zip(p, (2, 1, 0)))


def _slot(p):
    return 4 * p[0] + 2 * p[1] + p[2]


def _chip_of(p):
    return 2 * p[0] + p[1]


ANY = pl.BlockSpec(memory_space=pl.ANY)


class Gather:
    scratch = (pltpu.SemaphoreType.DMA((7,)), pltpu.SemaphoreType.DMA((7,)), pltpu.SemaphoreType.DMA)

    def __init__(self, shard):
        self.operand = shard
        self.out_shape = jax.ShapeDtypeStruct((N_DEV,) + shard.shape, shard.dtype)

    def bind(self, x_ref, out_ref, send_sems, recv_sems, local_sem):
        me = _place()
        sibling = _flip(me, 1)
        chips = [_flip(me, 4), _flip(me, 2), _flip(me, 6)]

        def copy(k, block, to, src=None):
            return pltpu.make_async_remote_copy(
                src_ref=out_ref.at[_slot(block)] if src is None else src, dst_ref=out_ref.at[_slot(block)],
                send_sem=send_sems.at[k], recv_sem=recv_sems.at[k], device_id=to, device_id_type=MESH)

        mine = pltpu.make_async_copy(x_ref, out_ref.at[_slot(me)], local_sem)
        first = [copy(0, me, sibling, src=x_ref)] + [copy(1 + j, me, chip, src=x_ref) for j, chip in enumerate(chips)]
        passed = [copy(4 + j, chip, sibling) for j, chip in enumerate(chips)]

        def start():
            mine.start()
            for cp in first:
                cp.start()

        def finish():
            for j, chip in enumerate(chips):
                copy(1 + j, chip, me).wait_recv()
                passed[j].start()
            copy(0, sibling, me).wait_recv()
            for j, chip in enumerate(chips):
                copy(4 + j, _flip(chip, 1), me).wait_recv()
            for cp in first + passed:
                cp.wait_send()
            mine.wait()

        return start, finish


class ChipExchange:
    scratch = (pltpu.SemaphoreType.DMA((3,)), pltpu.SemaphoreType.DMA((3,)), pltpu.SemaphoreType.DMA)

    def __init__(self, blocks):
        self.operand = blocks
        self.out_shape = jax.ShapeDtypeStruct(blocks.shape, blocks.dtype)

    def bind(self, x_ref, out_ref, send_sems, recv_sems, local_sem):
        me = _place()
        peers = [_flip(me, 4), _flip(me, 2), _flip(me, 6)]
        mine = pltpu.make_async_copy(x_ref.at[_chip_of(me)], out_ref.at[_chip_of(me)], local_sem)

        def copy(j, src_chip, dst_chip):
            return pltpu.make_async_remote_copy(
                src_ref=x_ref.at[src_chip], dst_ref=out_ref.at[dst_chip], send_sem=send_sems.at[j],
                recv_sem=recv_sems.at[j], device_id=peers[j], device_id_type=MESH)

        sends = [copy(j, _chip_of(peer), _chip_of(me)) for j, peer in enumerate(peers)]

        def start():
            mine.start()
            for cp in sends:
                cp.start()

        def finish():
            for j, peer in enumerate(peers):
                copy(j, _chip_of(me), _chip_of(peer)).wait_recv()
            for cp in sends:
                cp.wait_send()
            mine.wait()

        return start, finish


def exchange_alone(exchange, name):
    def body(x_ref, out_ref, *sems):
        start, finish = exchange.bind(x_ref, out_ref, *sems)
        start()
        finish()

    return pl.pallas_call(body, name=name, out_shape=exchange.out_shape, in_specs=[ANY], out_specs=ANY,
                          scratch_shapes=list(exchange.scratch))(exchange.operand)


def pair_exchange(blocks, name):
    _, rows, width = blocks.shape

    def body(x_ref, theirs_ref, send_sems, recv_sems):
        me = _place()
        remote = [pltpu.make_async_remote_copy(
            src_ref=x_ref.at[2 * q + 1 - me[2]], dst_ref=theirs_ref.at[q], send_sem=send_sems.at[q], recv_sem=recv_sems.at[q],
            device_id=_flip(me, 1), device_id_type=MESH) for q in range(4)]
        for cp in remote:
            cp.start()
        for cp in remote:
            cp.wait()

    return pl.pallas_call(
        body, name=name, out_shape=jax.ShapeDtypeStruct((4, rows, width), blocks.dtype), in_specs=[ANY], out_specs=ANY,
        scratch_shapes=[pltpu.SemaphoreType.DMA((4,)), pltpu.SemaphoreType.DMA((4,))])(blocks)


def pair_add(blocks, theirs, name):
    n, rows, width = theirs.shape
    tile = _pick(rows, (432, 256, 128, 64, 32, 16))
    spec = pl.BlockSpec((None, tile, width), lambda q, i: (q, i, 0))
    south = pl.BlockSpec((None, None, tile, width), lambda q, i: (q, 0, i, 0))
    north = pl.BlockSpec((None, None, tile, width), lambda q, i: (q, 1, i, 0))

    def body(s_ref, n_ref, b_ref, o_ref):
        mine = jnp.where(lax.axis_index("c") == 0, s_ref[...], n_ref[...])
        o_ref[...] = (mine.astype(f32) + b_ref[...].astype(f32)).astype(o_ref.dtype)

    by_core = blocks.reshape(n, 2, rows, width)
    return pl.pallas_call(body, name=name, grid=(n, rows // tile), in_specs=[south, north, spec], out_specs=spec,
                          out_shape=jax.ShapeDtypeStruct(theirs.shape, theirs.dtype), compiler_params=_params(2))(by_core, by_core, theirs)


def sum_slots(blocks, name):
    n, rows, width = blocks.shape
    tile = _pick(rows, (432, 256, 128, 64, 32, 16))

    def body(x_ref, o_ref):
        total = x_ref[0].astype(f32)
        for s in range(1, n):
            total = total + x_ref[s].astype(f32)
        o_ref[...] = total

    return pl.pallas_call(
        body, name=name, grid=(rows // tile,), in_specs=[pl.BlockSpec((n, tile, width), lambda i: (0, i, 0))],
        out_specs=pl.BlockSpec((tile, width), lambda i: (i, 0)), out_shape=_sds((rows, width)), compiler_params=_params(1))(blocks)


def all_reduce_small(x, name):
    rows, width = x.shape

    def body(x_ref, o_ref, land, send_sems, recv_sems):
        me = _place()
        copies = []
        for k in range(1, N_DEV):
            peer = _flip(me, k)
            copies.append(pltpu.make_async_remote_copy(
                src_ref=x_ref, dst_ref=land.at[_slot(me)], send_sem=send_sems.at[k - 1], recv_sem=recv_sems.at[k - 1],
                device_id=peer, device_id_type=MESH))
        for cp in copies:
            cp.start()
        land[_slot(me)] = x_ref[...]
        for k in range(1, N_DEV):
            peer = _flip(me, k)
            pltpu.make_async_remote_copy(
                src_ref=x_ref, dst_ref=land.at[_slot(peer)], send_sem=send_sems.at[k - 1], recv_sem=recv_sems.at[k - 1],
                device_id=peer, device_id_type=MESH).wait_recv()
        total = land[0]
        for s in range(1, N_DEV):
            total = total + land[s]
        o_ref[...] = total
        for cp in copies:
            cp.wait_send()

    return pl.pallas_call(
        body, name=name, out_shape=_sds((rows, width)), in_specs=[pl.BlockSpec(memory_space=pltpu.VMEM)],
        out_specs=pl.BlockSpec(memory_space=pltpu.VMEM),
        scratch_shapes=[pltpu.VMEM((N_DEV, rows, width), f32), pltpu.SemaphoreType.DMA((7,)), pltpu.SemaphoreType.DMA((7,))],
    )(x)


def _slab(rows):
    return -(-rows // 16) * 16


def _pack_big(shards):
    parts = []
    for name, rows in BIG:
        part = shards[name].astype(bf16).reshape(DEPTH, rows, D_MODEL)
        parts.append(jnp.pad(part, ((0, 0), (0, _slab(rows) - rows), (0, 0))))
    return jnp.concatenate(parts, axis=1)


def _unpack_gathered(gathered):
    full, at = {}, 0
    for name, rows in BIG:
        part = gathered[:, at:at + rows, :]
        at += _slab(rows)
        if name in COL_SHARDED:
            full[name] = part.reshape(N_DEV, D_MODEL, rows).transpose(1, 0, 2).reshape(D_MODEL, N_DEV * rows)
        else:
            full[name] = part.reshape(N_DEV * rows, D_MODEL)
    full["w_gate_up"] = _interleave_gate_up(full["w_gate_up"])
    w_in = full.pop("w_in")
    full["w_main"] = jnp.concatenate([w_in[:, :AB_AT], w_in[:, AB_AT + 2 * GDN_HEADS:]], axis=1)
    full["w_ab"] = jnp.pad(w_in[:, AB_AT:AB_AT + 2 * GDN_HEADS], ((0, 0), (0, LANES - 2 * GDN_HEADS)))
    return full


def _pack_grads(grads):
    grads = dict(grads)
    main, ab = grads.pop("w_main"), grads.pop("w_ab")
    grads["w_in"] = jnp.concatenate([main[:, :AB_AT], ab[:, :2 * GDN_HEADS], main[:, AB_AT:]], axis=1)
    grads["w_gate_up"] = _interleave_gate_up(grads["w_gate_up"], undo=True)
    parts = []
    for name, rows in BIG:
        g = grads[name]
        if name in COL_SHARDED:
            g = g.reshape(D_MODEL, N_DEV, rows).transpose(1, 0, 2)
        parts.append(jnp.pad(g.reshape(N_DEV, rows, D_MODEL), ((0, 0), (0, _slab(rows) - rows), (0, 0))))
    return jnp.concatenate(parts, axis=1)


def _unpack_shard(layers):
    flat = jnp.stack(layers)
    out, at = {}, 0
    for name, rows in BIG:
        part = flat[:, at:at + rows, :]
        at += _slab(rows)
        out[name] = part.reshape(DEPTH, D_MODEL, rows) if name in COL_SHARDED else part
    return out


def _rows_of(flat_len):
    return -(-flat_len // (8 * D_MODEL)) * 8


def _pack_small(parts):
    flat = jnp.concatenate([p.reshape(-1) for p in parts])
    rows = _rows_of(flat.shape[0])
    flat = jnp.pad(flat, (0, rows * D_MODEL - flat.shape[0]))
    return flat.reshape(rows, D_MODEL)


def _unpack_small(packed, like):
    flat, out, at = packed.reshape(-1), [], 0
    for p in like:
        out.append(flat[at:at + p.size].reshape(p.shape))
        at += p.size
    return out


def _rope_tables(positions):
    inv_freq = jnp.float32(ROPE_THETA) ** (-jnp.arange(0, ROPE_DIM, 2, dtype=f32) / ROPE_DIM)
    ang = positions.astype(f32)[:, None] * inv_freq
    cos, sin = jnp.cos(ang), jnp.sin(ang)
    rest = ATTN_HEAD_DIM - ROPE_DIM
    cos_h = jnp.concatenate([cos, cos, jnp.ones((cos.shape[0], rest), f32)], axis=1)
    sin_h = jnp.concatenate([-sin, sin, jnp.zeros((sin.shape[0], rest), f32)], axis=1)
    return jnp.tile(cos_h, (1, ATTN_HEADS)), jnp.tile(sin_h, (1, ATTN_HEADS))


HEAD_PARAMS = ("norm_mix_pre", "w_main", "w_ab", "conv_short", "conv_gdn", "gdn_a_log", "gdn_dt_bias")


def _layer_head(h, p, cos_t, sin_t):
    hn = rms_norm(h, p["norm_mix_pre"][None], "norm_mix_pre")
    proj = _linear(hn, p["w_main"], "w_main")
    ab = _linear(hn, p["w_ab"], "w_ab")
    aw, cw, gw = ATTN_WIDTH, CONV_WIDTH, GDN_WIDTH
    aq, ak, av, cb, cc, cx, gqkv, gate = _split_cols(proj, (aw, aw, aw, cw, cw, cw, 3 * gw, gw))
    y_attn = dilated_attention(rope(aq, cos_t, sin_t, ATTN_HEAD_DIM ** -0.5, "rope_q"), rope(ak, cos_t, sin_t, 1.0, "rope_k"),
                               av, "attn")
    y_conv = short_conv(cb, cc, cx, p["conv_short"], "short_conv")
    qkv = gdn_pre(gqkv, p["conv_gdn"], "gdn_pre")
    pv = jnp.zeros((8, LANES), f32).at[0, :GDN_HEADS].set(p["gdn_a_log"]).at[1, :GDN_HEADS].set(p["gdn_dt_bias"])
    g, beta = gate_beta(ab, pv, "gate_beta")
    return (*_split_cols(qkv, (gw, gw, gw)), g, beta), (gate, y_attn, y_conv)


def _layer_tail(h, o, gate, y_attn, y_conv, p, mem):
    y_gdn = gdn_post(o, gate, p["gdn_norm"][None], "gdn_post")
    mix = _linear(jnp.concatenate([y_attn, y_conv, y_gdn], axis=1), p["w_out"], "w_out")
    h = add_norm(h, mix, p["norm_mix_post"][None], "norm_mix_post")

    hn = rms_norm(h, p["norm_xattn_pre"][None], "norm_xattn_pre")
    qx = _linear(hn, p["w_xq"], "w_xq")
    kv = _linear(rms_norm(mem, p["norm_mem"][None], "norm_mem"), p["w_xkv"], "w_xkv")
    xa = _linear(cross_attention(qx, kv, "xattn"), p["w_xo"], "w_xo")
    h = add_norm(h, xa, p["norm_xattn_post"][None], "norm_xattn_post")

    hn = rms_norm(h, p["norm_ffn_pre"][None], "norm_ffn_pre")
    return add_norm(h, swiglu_ffn(hn, p["w_gate_up"], p["w_down"], "ffn"), p["norm_ffn_post"][None], "norm_ffn_post")


def _forward_backward(x, packed, small, mem, cos_t, sin_t, target):
    h = x
    gathered = exchange_alone(Gather(packed[0]), "gather_first")
    saved = []
    for layer in range(DEPTH):
        p = {**_unpack_gathered(gathered), **{n: t[layer] for n, t in small.items()}}
        head_p = {n: p[n] for n in HEAD_PARAMS}
        tail_p = {n: t for n, t in p.items() if n not in HEAD_PARAMS}
        (rule_in, rest), head_vjp = jax.vjp(lambda h, hp: _layer_head(h, hp, cos_t, sin_t), h, head_p)
        carried = Gather(packed[layer + 1]) if layer + 1 < DEPTH else None
        o, s_in, inv, *landed = delta_rule_fwd(*rule_in, "delta_rule_fwd", carried)
        if carried:
            gathered = landed[0]
        h, tail_vjp = jax.vjp(lambda h, o, rest, tp: _layer_tail(h, o, *rest, tp, mem), h, o, rest, tail_p)
        saved.append((head_vjp, tail_vjp, rule_in, s_in, inv))

    loss, dh = jax.value_and_grad(lambda y: loss_rows(y, target, "loss"))(h)

    big_rows, small_grads, pending = [None] * DEPTH, [None] * DEPTH, None
    for layer in reversed(range(DEPTH)):
        head_vjp, tail_vjp, rule_in, s_in, inv = saved[layer]
        dh_tail, do, d_rest, d_tail_p = tail_vjp(dh)
        carried = ChipExchange(pending) if pending is not None else None
        *d_rule_in, = delta_rule_bwd(*rule_in, s_in, inv, do, "delta_rule_bwd", carried)
        if carried:
            big_rows[layer + 1] = sum_slots(d_rule_in.pop(), "sum_grads")
        dh_head, d_head_p = head_vjp((tuple(d_rule_in), d_rest))
        dh = dh_tail + dh_head
        d_p = {**d_head_p, **d_tail_p}
        small_grads[layer] = {n: d_p[n] for n in small}
        blocks = _pack_grads(d_p)
        pending = pair_add(blocks, pair_exchange(blocks, "pair_exchange"), "pair_add")
    big_rows[0] = sum_slots(exchange_alone(ChipExchange(pending), "exchange_last"), "sum_grads")
    return loss, dh, big_rows, small_grads


def kernel(x, mem, positions, norm_mix_pre, norm_mix_post, w_in, conv_short, conv_gdn, gdn_a_log, gdn_dt_bias, gdn_norm, w_out, norm_mem, norm_xattn_pre, norm_xattn_post, w_xq, w_xkv, w_xo, norm_ffn_pre, norm_ffn_post, w_gate_up, w_down, loss_target, m_norm_mix_pre, m_norm_mix_post, m_w_in, m_conv_short, m_conv_gdn, m_gdn_a_log, m_gdn_dt_bias, m_gdn_norm, m_w_out, m_norm_mem, m_norm_xattn_pre, m_norm_xattn_post, m_w_xq, m_w_xkv, m_w_xo, m_norm_ffn_pre, m_norm_ffn_post, m_w_gate_up, m_w_down, v_norm_mix_pre, v_norm_mix_post, v_w_in, v_conv_short, v_conv_gdn, v_gdn_a_log, v_gdn_dt_bias, v_gdn_norm, v_w_out, v_norm_mem, v_norm_xattn_pre, v_norm_xattn_post, v_w_xq, v_w_xkv, v_w_xo, v_norm_ffn_pre, v_norm_ffn_post, v_w_gate_up, v_w_down):
    given = dict(locals())
    weights = {n: given[n] for n in WEIGHTS}
    me = _slot(_place())

    def in_place(shard):
        full = jnp.zeros(shard.shape[:-1] + (shard.shape[-1] * N_DEV,), f32)
        return lax.dynamic_update_slice_in_dim(full, shard, me * shard.shape[-1], axis=shard.ndim - 1)

    placed = [in_place(conv_short), in_place(conv_gdn)]
    conv_short_full, conv_gdn_full = _unpack_small(all_reduce_small(_pack_small(placed), "gather_conv"), placed)
    small = {n: weights[n] for n in NORMS + ("gdn_a_log", "gdn_dt_bias", "gdn_norm")}
    small["conv_short"], small["conv_gdn"] = conv_short_full, conv_gdn_full

    cos_t, sin_t = _rope_tables(positions[0])
    loss, grad_x, big_rows, small_layers = _forward_backward(
        x[0], _pack_big(weights), small, mem[0], cos_t, sin_t, loss_target[0])
    grads = _unpack_shard(big_rows)

    names = sorted(small)
    parts = [jnp.stack([layer[n] for layer in small_layers]) for n in names] + [loss.reshape(1)]
    reduced = _unpack_small(all_reduce_small(_pack_small(parts), "reduce_small"), parts)
    loss = reduced[-1][0]
    for n, g in zip(names, reduced[:-1]):
        if n in ("conv_short", "conv_gdn"):
            width = weights[n].shape[-1]
            g = lax.dynamic_slice_in_dim(g, me * width, width, axis=g.ndim - 1)
        grads[n] = g

    delta, new_m, new_v = {}, {}, {}
    for n in WEIGHTS:
        delta[n], new_m[n], new_v[n] = adamw(weights[n], grads[n], given["m_" + n], given["v_" + n], "adamw_" + n)
    return (loss, grad_x[None], *[grads[n] for n in WEIGHTS], *[delta[n] for n in WEIGHTS],
            *[new_m[n] for n in WEIGHTS], *[new_v[n] for n in WEIGHTS])
```

```python
import functools

import jax
import jax.numpy as jnp
from jax import lax
from jax.experimental import pallas as pl
from jax.experimental.pallas import tpu as pltpu

f32 = jnp.float32
bf16 = jnp.bfloat16
HIGHEST = lax.Precision.HIGHEST
MESH = pl.DeviceIdType.MESH

N_DEV = 8
DEPTH = 4
D_MODEL = 1024
EPS = 1e-6
ATTN_HEADS, ATTN_HEAD_DIM = 4, 64
ATTN_WIDTH = ATTN_HEADS * ATTN_HEAD_DIM
DILATIONS = (1, 4, 16)
QB = 128
ROPE_THETA = 500000.0
ROPE_DIM = ATTN_HEAD_DIM // 4
CONV_WIDTH = 256
GDN_HEADS, GDN_HEAD_DIM = 4, 128
GDN_WIDTH = GDN_HEADS * GDN_HEAD_DIM
GDN_CHUNK = 64
XATTN_HEADS, XATTN_HEAD_DIM = 4, 256
FFN_HIDDEN = 2816
IN_WIDTH = 3592
AB_AT = 3 * ATTN_WIDTH + 3 * CONV_WIDTH + 3 * GDN_WIDTH
MAIN_WIDTH = IN_WIDTH - 2 * GDN_HEADS
LANES = 128
ROW_TILE = 256
VMEM_LIMIT = 56 * 1024 * 1024

ADAM_LR, ADAM_B1, ADAM_B2, ADAM_EPS, ADAM_WD, ADAM_STEP = 0.001, 0.9, 0.999, 1e-08, 0.01, 10

HEAD_BIG = (("w_in", 449),)
TAIL_BIG = (("w_out", 128), ("w_xq", 128), ("w_xkv", 256), ("w_xo", 128), ("w_gate_up", 704), ("w_down", 352))
COL_SHARDED = ("w_in", "w_xkv", "w_gate_up")
NORMS = ("norm_mix_pre", "norm_mix_post", "norm_mem", "norm_xattn_pre", "norm_xattn_post", "norm_ffn_pre", "norm_ffn_post")
WEIGHTS = ("norm_mix_pre", "norm_mix_post", "w_in", "conv_short", "conv_gdn", "gdn_a_log", "gdn_dt_bias", "gdn_norm", "w_out",
           "norm_mem", "norm_xattn_pre", "norm_xattn_post", "w_xq", "w_xkv", "w_xo", "norm_ffn_pre", "norm_ffn_post",
           "w_gate_up", "w_down")


def _params(n_grid):
    return pltpu.CompilerParams(dimension_semantics=("arbitrary",) * n_grid, vmem_limit_bytes=VMEM_LIMIT)


def _pick(n, cands):
    for c in cands:
        if n % c == 0:
            return c
    return n


MXU_FLOPS = 9.0e14
HBM_BYTES_PER_S = 2.5e12
VMEM_RMW_BYTES_PER_S = 7.0e12
STEP_S = 0.4e-6
MATMUL_VMEM = 44 * 1024 * 1024


def _tiles(m, n, k, sa, sb, so, tn=None):
    def divisors(d):
        return sorted({d // s for s in range(1, d // LANES + 1) if d % s == 0 and (d // s) % LANES == 0}, reverse=True)

    best = None
    for tk in divisors(k):
        nk = k // tk
        for tm in divisors(m):
            for tn_ in [tn] if tn else divisors(n):
                per_step = tm * tk * sa + tk * tn_ * sb + tm * tn_ * so
                vmem = 2 * per_step + (tm * tn_ * 4 if nk > 1 else 0)
                vmem += (tm * tk * 2 if sa == 4 else 0) + (tk * tn_ * 2 if sb == 4 else 0) + tm * tn_ * 4
                if vmem > MATMUL_VMEM:
                    continue
                moved = m * k * sa * (1 if nk == 1 else n // tn_) + k * n * sb * (1 if nk == 1 and n == tn_ else m // tm) + m * n * so
                busy = 2 * m * n * k / MXU_FLOPS + (m * n * 8 * nk / VMEM_RMW_BYTES_PER_S if nk > 1 else 0)
                cost = max(moved / HBM_BYTES_PER_S, busy) + per_step / HBM_BYTES_PER_S + (m // tm) * (n // tn_) * nk * STEP_S
                if best is None or cost < best[0]:
                    best = (cost, tm, tn_, tk)
    return best[1:]


def _mm(a, b, ta, tb, out_dtype, name, finish=None):
    m, k = (a.shape[1], a.shape[0]) if ta else a.shape
    n = b.shape[0] if tb else b.shape[1]
    tm, tn, tk = _tiles(m, n, k, a.dtype.itemsize, b.dtype.itemsize, jnp.dtype(out_dtype).itemsize, finish and finish[0])
    nk = k // tk
    if finish:
        assert nk == 1
        _, extra, results, function = finish
        dims = (((0 if ta else 1,), (1 if tb else 0,)), ((), ()))

        def finish_body(a_ref, b_ref, *refs):
            p = lax.dot_general(a_ref[...].astype(bf16), b_ref[...].astype(bf16), dims, preferred_element_type=f32)
            outs = function(p, *[r[...] for r in refs[:len(extra)]])
            for r, o in zip(refs[len(extra):], outs):
                r[...] = o.astype(r.dtype)

        return pl.pallas_call(
            finish_body, name=name, grid=(m // tm, n // tn),
            in_specs=[pl.BlockSpec((tk, tm), lambda i, j: (0, i)) if ta else pl.BlockSpec((tm, tk), lambda i, j: (i, 0)),
                      pl.BlockSpec((tn, tk), lambda i, j: (j, 0)) if tb else pl.BlockSpec((tk, tn), lambda i, j: (0, j))]
            + [pl.BlockSpec((tm, cols), lambda i, j: (i, j)) for _, cols in extra],
            out_specs=[pl.BlockSpec((tm, cols), lambda i, j: (i, j)) for _, cols in results],
            out_shape=[jax.ShapeDtypeStruct((m, n // tn * cols), dt) for dt, cols in results],
            compiler_params=_params(2))(a, b, *[x for x, _ in extra])
    a_spec = pl.BlockSpec((tk, tm), lambda i, j, kk: (kk, i)) if ta else pl.BlockSpec((tm, tk), lambda i, j, kk: (i, kk))
    b_spec = pl.BlockSpec((tn, tk), lambda i, j, kk: (j, kk)) if tb else pl.BlockSpec((tk, tn), lambda i, j, kk: (kk, j))
    dims = (((0 if ta else 1,), (1 if tb else 0,)), ((), ()))

    def body(a_ref, b_ref, o_ref, *acc):
        kk = pl.program_id(2)
        p = lax.dot_general(a_ref[...].astype(bf16), b_ref[...].astype(bf16), dims, preferred_element_type=f32)
        if nk == 1:
            o_ref[...] = p.astype(o_ref.dtype)
            return
        acc_ref, = acc

        @pl.when(kk == 0)
        def _():
            acc_ref[...] = p

        @pl.when(kk > 0)
        def _():
            acc_ref[...] += p

        @pl.when(kk == nk - 1)
        def _():
            o_ref[...] = acc_ref[...].astype(o_ref.dtype)

    return pl.pallas_call(
        body, name=name, grid=(m // tm, n // tn, nk), in_specs=[a_spec, b_spec],
        out_specs=pl.BlockSpec((tm, tn), lambda i, j, kk: (i, j)), out_shape=jax.ShapeDtypeStruct((m, n), out_dtype),
        scratch_shapes=[pltpu.VMEM((tm, tn), f32)] if nk > 1 else [], compiler_params=_params(3))(a, b)


def _linear(x, w, name):
    @jax.custom_vjp
    def lin(x, w):
        return _mm(x, w, False, False, f32, name + "_y")

    def lin_f(x, w):
        return _mm(x, w, False, False, f32, name + "_y"), (x, w)

    def lin_b(res, dy):
        x, w = res
        return _mm(dy, w, False, True, f32, name + "_dx"), _mm(x, dy, True, False, bf16, name + "_dw")

    lin.defvjp(lin_f, lin_b)
    return lin(x, w)


GATE_UP_TILE = 512


def _interleave_gate_up(w, undo=False):
    two_f = w.shape[1]
    half = GATE_UP_TILE // 2
    nb = two_f // GATE_UP_TILE
    if undo:
        order = [2 * j + side for side in range(2) for j in range(nb)]
    else:
        order = [side * nb + j for j in range(nb) for side in range(2)]
    return jnp.concatenate([w[:, b * half:(b + 1) * half] for b in order], axis=1)


def swiglu_ffn(hn, w_gate_up, w_down, name):
    half = GATE_UP_TILE // 2

    def act_of(p):
        return p, jax.nn.silu(p[:, :half]) * p[:, half:]

    def d_gate_up_of(d_act, gate_up):
        g, u = gate_up[:, :half].astype(f32), gate_up[:, half:].astype(f32)
        sig = jax.nn.sigmoid(g)
        return (jnp.concatenate([d_act * u * sig * (1.0 + g * (1.0 - sig)), d_act * g * sig], axis=1),)

    def forward(hn, w_gate_up, w_down):
        gate_up, act = _mm(hn, w_gate_up, False, False, bf16, name + "_act",
                           (GATE_UP_TILE, [], [(bf16, GATE_UP_TILE), (bf16, half)], act_of))
        return _mm(act, w_down, False, False, f32, name + "_y"), (hn, w_gate_up, w_down, gate_up, act)

    def backward(res, dy):
        hn, w_gate_up, w_down, gate_up, act = res
        d_gate_up, = _mm(dy, w_down, False, True, bf16, name + "_dact",
                         (half, [(gate_up, GATE_UP_TILE)], [(bf16, GATE_UP_TILE)], d_gate_up_of))
        return (_mm(d_gate_up, w_gate_up, False, True, f32, name + "_dx"), _mm(hn, d_gate_up, True, False, bf16, name + "_dw1"),
                _mm(act, dy, True, False, bf16, name + "_dw2"))

    @jax.custom_vjp
    def op(hn, w_gate_up, w_down):
        return forward(hn, w_gate_up, w_down)[0]

    op.defvjp(forward, backward)
    return op(hn, w_gate_up, w_down)


def _split_cols(x, widths):
    edges = [sum(widths[:i]) for i in range(len(widths) + 1)]

    def cut(x):
        return tuple(x[:, a:b] for a, b in zip(edges[:-1], edges[1:]))

    @jax.custom_vjp
    def split(x):
        return cut(x)

    split.defvjp(lambda x: (cut(x), None), lambda _, cts: (jnp.concatenate(cts, axis=1),))
    return split(x)


def _block_op(name, f, grid, in_specs, out_defs, arrays, diff, acc=None, gdefs=None):
    acc, gdefs = acc or {}, gdefs or {}
    n_in, n_out, n_grid = len(in_specs), len(out_defs), len(grid)

    def fwd_call(*xs):
        def body(*refs):
            outs = f(*[r[...] for r in refs[:n_in]])
            for r, o in zip(refs[n_in:], outs):
                r[...] = o.astype(r.dtype)

        return pl.pallas_call(
            body, name=name + "_fwd", grid=grid, in_specs=in_specs, out_specs=[d[1] for d in out_defs],
            out_shape=[d[0] for d in out_defs], compiler_params=_params(n_grid))(*xs)

    def bwd_call(*xs_and_cts):
        def body(*refs):
            xs = [r[...] for r in refs[:n_in]]
            cts = tuple(r[...] for r in refs[n_in:n_in + n_out])

            def of_diff(*dx):
                full = list(xs)
                for i, v in zip(diff, dx):
                    full[i] = v
                return tuple(f(*full))

            _, vjp = jax.vjp(of_diff, *[xs[i] for i in diff])
            grads = vjp(cts)
            for i, g, r in zip(diff, grads, refs[n_in + n_out:]):
                if i in acc:
                    first = functools.reduce(jnp.logical_and, [pl.program_id(a) == 0 for a in acc[i]])

                    @pl.when(first)
                    def _(r=r):
                        r[...] = jnp.zeros_like(r)

                    r[...] += g.astype(r.dtype)
                else:
                    r[...] = g.astype(r.dtype)

        g_defs = [gdefs.get(i, (jax.ShapeDtypeStruct(arrays[i].shape, f32), in_specs[i])) for i in diff]
        return pl.pallas_call(
            body, name=name + "_bwd", grid=grid, in_specs=list(in_specs) + [d[1] for d in out_defs],
            out_specs=[d[1] for d in g_defs], out_shape=[d[0] for d in g_defs], compiler_params=_params(n_grid))(*xs_and_cts)

    return fwd_call, bwd_call


def _simple_op(name, f, grid, in_specs, out_defs, arrays, diff, acc=None):
    fwd_call, bwd_call = _block_op(name, f, grid, in_specs, out_defs, arrays, diff, acc)

    @jax.custom_vjp
    def op(*xs):
        return tuple(fwd_call(*xs))

    def op_f(*xs):
        return tuple(fwd_call(*xs)), xs

    def op_b(xs, cts):
        grads = bwd_call(*xs, *cts)
        out = [jnp.zeros_like(x) for x in xs]
        for i, g in zip(diff, grads):
            out[i] = g
        return tuple(out)

    op.defvjp(op_f, op_b)
    return op(*arrays)


def _rows(width, tile=ROW_TILE):
    return pl.BlockSpec((tile, width), lambda i: (i, 0))


def _whole(shape):
    return pl.BlockSpec(shape, lambda *_: (0,) * len(shape))


def _sds(shape):
    return jax.ShapeDtypeStruct(shape, f32)


def _rms(x, w):
    return x * lax.rsqrt(jnp.mean(x * x, axis=-1, keepdims=True) + EPS) * w


def rms_norm(x, w, name):
    r, d = x.shape
    return _simple_op(name, lambda x, w: (_rms(x, w),), (r // ROW_TILE,), [_rows(d), _whole((1, d))],
                      [(_sds((r, d)), _rows(d))], (x, w), (0, 1), {1: (0,)})[0]


def add_norm(h, y, w, name):
    r, d = h.shape
    return _simple_op(name, lambda h, y, w: (h + _rms(y, w),), (r // ROW_TILE,), [_rows(d), _rows(d), _whole((1, d))],
                      [(_sds((r, d)), _rows(d))], (h, y, w), (0, 1, 2), {2: (0,)})[0]


def _swap8(x):
    def raw(x):
        lane = lax.broadcasted_iota(jnp.int32, x.shape, 1) % ATTN_HEAD_DIM
        half = ROPE_DIM // 2
        up = pltpu.roll(x, x.shape[1] - half, axis=1)
        down = pltpu.roll(x, half, axis=1)
        return jnp.where(lane < half, up, jnp.where(lane < ROPE_DIM, down, 0.0))

    @jax.custom_vjp
    def swap(x):
        return raw(x)

    swap.defvjp(lambda x: (raw(x), None), lambda _, g: (raw(g),))
    return swap(x)


def rope(x, cos_t, sin_t, scale, name):
    r, d = x.shape
    return _simple_op(name, lambda x, c, s: ((x * c + _swap8(x) * s) * scale,), (r // ROW_TILE,), [_rows(d)] * 3,
                      [(_sds((r, d)), _rows(d))], (x, cos_t, sin_t), (0,))[0]


def _shift_rows(x, k):
    n = x.shape[0]

    def down(x):
        row = lax.broadcasted_iota(jnp.int32, x.shape, 0)
        return jnp.where(row >= k, pltpu.roll(x, k, axis=0), 0.0)

    def up(x):
        row = lax.broadcasted_iota(jnp.int32, x.shape, 0)
        return jnp.where(row < n - k, pltpu.roll(x, n - k, axis=0), 0.0)

    @jax.custom_vjp
    def shift(x):
        return down(x)

    shift.defvjp(lambda x: (down(x), None), lambda _, g: (up(g),))
    return shift(x)


def _causal_conv(x, w):
    taps = w.shape[0]
    y = x * w[taps - 1:taps, :]
    for j in range(taps - 1):
        y = y + _shift_rows(x, taps - 1 - j) * w[j:j + 1, :]
    return y


def _cols(rows, at=0):
    return pl.BlockSpec((rows, LANES), lambda j: (0, at + j))


def short_conv(cb, cc, cx, w, name):
    s, c = cb.shape
    taps = w.shape[0]
    return _simple_op(name, lambda b, c_, x, w: (b * _causal_conv(c_ * x, w),), (c // LANES,),
                      [_cols(s)] * 3 + [_cols(taps)], [(_sds((s, c)), _cols(s))], (cb, cc, cx, w), (0, 1, 2, 3))[0]


def gdn_pre(qkv, w, name):
    s, c = qkv.shape
    taps = w.shape[0]

    def f(x, w):
        j = pl.program_id(0)
        y = jax.nn.silu(_causal_conv(x, w))
        normed = y * lax.rsqrt(jnp.sum(y * y, axis=-1, keepdims=True) + EPS)
        scale = jnp.where(j < GDN_HEADS, GDN_HEAD_DIM ** -0.5, 1.0).astype(f32)
        return (jnp.where(j < 2 * GDN_HEADS, normed * scale, y),)

    return _simple_op(name, f, (c // LANES,), [_cols(s), _cols(taps)], [(_sds((s, c)), _cols(s))], (qkv, w), (0, 1))[0]


def gate_beta(ab, pv, name):
    s = ab.shape[0]

    def f(ab, pv):
        row = lax.broadcasted_iota(jnp.int32, (LANES, GDN_WIDTH), 0)
        head = lax.broadcasted_iota(jnp.int32, (LANES, GDN_WIDTH), 1) // GDN_HEAD_DIM
        spread_a = (row == head).astype(f32)
        spread_b = (row == head + GDN_HEADS).astype(f32)
        a = jnp.dot(ab, spread_a, precision=HIGHEST, preferred_element_type=f32)
        b = jnp.dot(ab, spread_b, precision=HIGHEST, preferred_element_type=f32)
        p = jnp.dot(pv, spread_a, precision=HIGHEST, preferred_element_type=f32)
        g = -jnp.exp(p[0:1, :]) * jax.nn.softplus(a + p[1:2, :])
        return g, jax.nn.sigmoid(b)

    outs = [(_sds((s, GDN_WIDTH)), _rows(GDN_WIDTH))] * 2
    return _simple_op(name, f, (s // ROW_TILE,), [_rows(LANES), _whole((8, LANES))], outs, (ab, pv), (0, 1), {1: (0,)})


def gdn_post(o, gate, w, name):
    s, c = o.shape
    spec = pl.BlockSpec((ROW_TILE, LANES), lambda i, j: (i, j))
    return _simple_op(name, lambda o, g, w: (_rms(o, w) * jax.nn.silu(g),), (s // ROW_TILE, c // LANES),
                      [spec, spec, _whole((1, LANES))], [(_sds((s, c)), spec)], (o, gate, w), (0, 1, 2), {2: (0, 1)})[0]


def attn_merge(outs, lses, name):
    s, c = outs[0].shape

    def f(o1, o2, o3, l1, l2, l3):
        m = lax.stop_gradient(jnp.maximum(jnp.maximum(l1, l2), l3))
        e1, e2, e3 = jnp.exp(l1 - m), jnp.exp(l2 - m), jnp.exp(l3 - m)
        return ((e1 * o1 + e2 * o2 + e3 * o3) / (e1 + e2 + e3),)

    return _simple_op(name, f, (s // ROW_TILE,), [_rows(c)] * 6, [(_sds((s, c)), _rows(c))], (*outs, *lses), tuple(range(6)))[0]


def loss_rows(y, target, name):
    s, d = y.shape
    nt = s // ROW_TILE

    def f(y, t):
        e = y - t
        part = 0.5 * jnp.sum(jnp.mean(e * e, axis=-1, keepdims=True), axis=0, keepdims=True)
        return (jnp.broadcast_to(part * (1.0 / (8 * LANES)), (8, LANES)),)

    out = _simple_op(name, f, (nt,), [_rows(d)] * 2, [(_sds((nt * 8, LANES)), pl.BlockSpec((8, LANES), lambda i: (i, 0)))],
                     (y, target), (0,))[0]
    return jnp.sum(out)


def _mxu(a, b, form):
    dims = {"nn": ((1,), (0,)), "nt": ((1,), (1,)), "tn": ((0,), (0,))}

    def raw(a, b, form):
        return lax.dot_general(a.astype(bf16), b.astype(bf16), (dims[form], ((), ())), preferred_element_type=f32)

    @jax.custom_vjp
    def prod(a, b):
        return raw(a, b, form)

    def prod_b(res, ct):
        a, b = res
        if form == "nn":
            return raw(ct, b, "nt"), raw(a, ct, "tn")
        if form == "nt":
            return raw(ct, b, "nn"), raw(ct, a, "tn")
        return raw(b, ct, "nt"), raw(a, ct, "nn")

    prod.defvjp(lambda a, b: (raw(a, b, form), (a, b)), prod_b)
    return prod(a, b)


def band_attention(q, k, v, nb, name):
    r, qb, width = q.shape
    dh = ATTN_HEAD_DIM

    def f(q, kp, kc, vp, vc):
        has_prev = (pl.program_id(0) % nb) > 0
        i = lax.broadcasted_iota(jnp.int32, (qb, qb), 0)
        j = lax.broadcasted_iota(jnp.int32, (qb, qb), 1)
        see_prev, see_cur = jnp.logical_and(j >= i, has_prev), j <= i
        outs, lses = [], []
        for hd in range(width // dh):
            at = slice(hd * dh, (hd + 1) * dh)
            sp = jnp.where(see_prev, _mxu(q[:, at], kp[:, at], "nt"), -jnp.inf)
            sc = jnp.where(see_cur, _mxu(q[:, at], kc[:, at], "nt"), -jnp.inf)
            m = lax.stop_gradient(jnp.maximum(jnp.max(sp, axis=-1, keepdims=True), jnp.max(sc, axis=-1, keepdims=True)))
            pp, pc = jnp.exp(sp - m), jnp.exp(sc - m)
            l = jnp.sum(pp, axis=-1, keepdims=True) + jnp.sum(pc, axis=-1, keepdims=True)
            outs.append(_mxu(pp / l, vp[:, at], "nn") + _mxu(pc / l, vc[:, at], "nn"))
            lses.append(jnp.broadcast_to(m + jnp.log(l), (qb, dh)))
        return jnp.concatenate(outs, axis=1), jnp.concatenate(lses, axis=1)

    blk = (None, qb, width)
    cur = pl.BlockSpec(blk, lambda b: (b, 0, 0))
    prev = pl.BlockSpec(blk, lambda b: (jnp.maximum(b - 1, 0), 0, 0))
    shape = _sds((r, qb, width))
    fwd_call, bwd_call = _block_op(name, f, (r,), [cur, prev, cur, prev, cur], [(shape, cur), (shape, cur)],
                                   (q, k, k, v, v), (0, 1, 2, 3, 4), gdefs={1: (shape, cur), 3: (shape, cur)})

    def to_prev(g):
        return jnp.concatenate([g[1:], jnp.zeros_like(g[:1])], axis=0)

    @jax.custom_vjp
    def op(q, k, v):
        return tuple(fwd_call(q, k, k, v, v))

    def op_b(res, cts):
        q, k, v = res
        dq, dkp, dkc, dvp, dvc = bwd_call(q, k, k, v, v, *cts)
        return dq, dkc + to_prev(dkp), dvc + to_prev(dvp)

    op.defvjp(lambda q, k, v: (tuple(fwd_call(q, k, k, v, v)), (q, k, v)), op_b)
    return op(q, k, v)


def dilated_attention(q, k, v, name):
    s = q.shape[0]
    outs, lses = [], []
    for d in DILATIONS:
        length = s // d
        nb = length // QB

        def to_residue(t):
            return t.reshape(length, d, ATTN_WIDTH).transpose(1, 0, 2).reshape(d * nb, QB, ATTN_WIDTH)

        def from_residue(t):
            return t.reshape(d, length, ATTN_WIDTH).transpose(1, 0, 2).reshape(s, ATTN_WIDTH)

        o, lse = band_attention(to_residue(q), to_residue(k), to_residue(v), nb, f"{name}_d{d}")
        outs.append(from_residue(o))
        lses.append(from_residue(lse))
    return attn_merge(outs, lses, name + "_merge")


def cross_attention(q, kv, name):
    s = q.shape[0]
    m = kv.shape[0]
    width = XATTN_HEADS * XATTN_HEAD_DIM
    tq = 512

    def f(q, k, v):
        sc = _mxu(q, k, "nt") * (XATTN_HEAD_DIM ** -0.5)
        mx = lax.stop_gradient(jnp.max(sc, axis=-1, keepdims=True))
        p = jnp.exp(sc - mx)
        return (_mxu(p / jnp.sum(p, axis=-1, keepdims=True), v, "nn"),)

    q_spec = pl.BlockSpec((tq, XATTN_HEAD_DIM), lambda a, i: (i, a))
    k_spec = pl.BlockSpec((m, XATTN_HEAD_DIM), lambda a, i: (0, a))
    v_spec = pl.BlockSpec((m, XATTN_HEAD_DIM), lambda a, i: (0, a + XATTN_HEADS))
    half = _sds((m, width))
    fwd_call, bwd_call = _block_op(name, f, (XATTN_HEADS, s // tq), [q_spec, k_spec, v_spec], [(_sds((s, width)), q_spec)],
                                   (q, kv, kv), (0, 1, 2), acc={1: (1,), 2: (1,)}, gdefs={1: (half, k_spec), 2: (half, k_spec)})

    @jax.custom_vjp
    def op(q, kv):
        return fwd_call(q, kv, kv)[0]

    def op_b(res, ct):
        q, kv = res
        dq, dk, dv = bwd_call(q, kv, kv, ct)
        return dq, jnp.concatenate([dk, dv], axis=1)

    op.defvjp(lambda q, kv: (fwd_call(q, kv, kv)[0], (q, kv)), op_b)
    return op(q, kv)


def _hi(a, b, form="nn"):
    dims = {"nn": ((1,), (0,)), "nt": ((1,), (1,)), "tn": ((0,), (0,))}[form]
    return lax.dot_general(a, b, (dims, ((), ())), precision=lax.Precision.HIGH, preferred_element_type=f32)


def _running_sum(g):
    def raw(x, form):
        c = x.shape[0]
        tri = (lax.broadcasted_iota(jnp.int32, (c, c), 0) >= lax.broadcasted_iota(jnp.int32, (c, c), 1)).astype(bf16)
        hi = x.astype(bf16)
        rest = x - hi.astype(f32)
        mid = rest.astype(bf16)
        low = (rest - mid.astype(f32)).astype(bf16)
        dims = (((1,) if form == "nn" else (0,), (0,)), ((), ()))
        return sum(lax.dot_general(tri, part, dims, preferred_element_type=f32) for part in (hi, mid, low))

    @jax.custom_vjp
    def run(x):
        return raw(x, "nn")

    run.defvjp(lambda x: (raw(x, "nn"), None), lambda _, ct: (raw(ct, "tn"),))
    return run(g)


def _unit_lower_inverse(a):
    c = a.shape[0]
    eye = (lax.broadcasted_iota(jnp.int32, (c, c), 0) == lax.broadcasted_iota(jnp.int32, (c, c), 1)).astype(f32)
    inv, power = eye - a, -a
    for _ in range(c.bit_length() - 2):
        power = _hi(power, power)
        inv = inv + _hi(inv, power)
    return inv


def _known_inverse(a, t):
    @jax.custom_vjp
    def inv(a, t):
        return t

    def inv_b(t, ct):
        return -_hi(_hi(t, ct, "tn"), t, "nt"), jnp.zeros_like(t)

    inv.defvjp(lambda a, t: (t, t), inv_b)
    return inv(a, t)


def _delta_chunk(q, k, v, g, beta, s0, known_inv=None):
    c = q.shape[0]
    i = lax.broadcasted_iota(jnp.int32, (c, c), 0)
    j = lax.broadcasted_iota(jnp.int32, (c, c), 1)
    causal, strict = i >= j, i > j
    dec = _running_sum(g)
    dec_i = dec[:, :c]
    rel = jnp.exp(jnp.where(causal, dec_i - dec_i.T, -jnp.inf))
    k_beta = k * beta
    a = jnp.where(strict, _mxu(k_beta, k, "nt") * rel, 0.0)
    inv = _unit_lower_inverse(a) if known_inv is None else _known_inverse(a, known_inv)
    e_dec = jnp.exp(dec)
    u = _hi(inv, v * beta)
    w = _hi(inv, k_beta * e_dec)
    attn = jnp.where(causal, _mxu(q, k, "nt") * rel, 0.0)
    total = jnp.sum(g, axis=0, keepdims=True)
    v_new = u - _mxu(w, s0, "nn")
    o = _mxu(q * e_dec, s0, "nn") + _mxu(attn, v_new, "nn")
    s1 = s0 * jnp.exp(total) + _mxu(k * jnp.exp(total - dec), v_new, "tn")
    return o, s1, inv


def _delta_rule_call(name, walk, n, in_specs, out_specs, out_shape, operands, exchange):
    n_in, n_out = len(in_specs), len(out_specs)
    carried = len(exchange.operands) if exchange else 0

    def body(*refs):
        ins, refs = refs[:n_in], refs[n_in:]
        x_refs, refs = refs[:carried], refs[carried:]
        outs, refs = refs[:n_out], refs[n_out:]
        land_refs, (state, *sems) = refs[:carried], refs[carried:]
        step = pl.program_id(0)
        if exchange:
            start, finish = exchange.bind(x_refs, land_refs, sems)
            pl.when(step == 0)(start)

        @pl.when(step == 0)
        def _():
            state[...] = jnp.zeros_like(state)

        walk(ins, outs, state)
        if exchange:
            pl.when(step == n - 1)(finish)

    return pl.pallas_call(
        body, name=name, grid=(n,), in_specs=list(in_specs) + [ANY] * carried, out_specs=list(out_specs) + [ANY] * carried,
        out_shape=list(out_shape) + (exchange.out_shapes if exchange else []),
        scratch_shapes=[pltpu.VMEM((GDN_HEAD_DIM, GDN_WIDTH), f32)] + (exchange.scratch if exchange else []),
        compiler_params=_params(1))(*operands, *(exchange.operands if exchange else []))


def _delta_heads():
    heads = [slice(hd * GDN_HEAD_DIM, (hd + 1) * GDN_HEAD_DIM) for hd in range(GDN_HEADS)]
    inv_at = [slice(hd * GDN_CHUNK, (hd + 1) * GDN_CHUNK) for hd in range(GDN_HEADS)]
    return heads, inv_at


def delta_rule_fwd(q, k, v, g, beta, name, exchange=None):
    s, width = q.shape
    c, dk = GDN_CHUNK, GDN_HEAD_DIM
    n = s // c
    heads, inv_at = _delta_heads()

    def walk(ins, outs, state):
        o_ref, s_in_ref, inv_ref = outs
        s_in_ref[...] = state[...]
        xs = [[r[:, hd] for r in (*ins, state)] for hd in heads]
        ys = [_delta_chunk(*x) for x in xs]
        for hd, at, (o, s1, inv) in zip(heads, inv_at, ys):
            o_ref[:, hd], state[:, hd], inv_ref[:, at] = o, s1, inv

    blk = pl.BlockSpec((c, width), lambda t: (t, 0))
    st = pl.BlockSpec((dk, width), lambda t: (t, 0))
    iv = pl.BlockSpec((c, GDN_HEADS * c), lambda t: (t, 0))
    return _delta_rule_call(name, walk, n, [blk] * 5, [blk, st, iv],
                            [_sds((s, width)), _sds((n * dk, width)), _sds((s, GDN_HEADS * c))], (q, k, v, g, beta), exchange)


def delta_rule_bwd(q, k, v, g, beta, s_in, inv, do, name, exchange=None):
    s, width = q.shape
    c, dk = GDN_CHUNK, GDN_HEAD_DIM
    n = s // c
    heads, inv_at = _delta_heads()

    def walk(ins, outs, dstate):
        q_ref, k_ref, v_ref, g_ref, b_ref, s_ref, inv_ref, do_ref = ins
        xs = [[r[:, hd] for r in (q_ref, k_ref, v_ref, g_ref, b_ref, s_ref)] for hd in heads]
        known = [inv_ref[:, at] for at in inv_at]
        cts = [(do_ref[:, hd], dstate[:, hd]) for hd in heads]
        grads = []
        for x, t, ct in zip(xs, known, cts):
            _, vjp = jax.vjp(lambda *y, t=t: _delta_chunk(*y, known_inv=t)[:2], *x)
            grads.append(vjp(ct))
        for hd, (*d_ins, ds0) in zip(heads, grads):
            for r, d in zip(outs, d_ins):
                r[:, hd] = d
            dstate[:, hd] = ds0

    blk = pl.BlockSpec((c, width), lambda t: (n - 1 - t, 0))
    st = pl.BlockSpec((dk, width), lambda t: (n - 1 - t, 0))
    iv = pl.BlockSpec((c, GDN_HEADS * c), lambda t: (n - 1 - t, 0))
    return _delta_rule_call(name, walk, n, [blk] * 5 + [st, iv, blk], [blk] * 5, [_sds((s, width))] * 5,
                            (q, k, v, g, beta, s_in, inv, do), exchange)


def adamw(w, g, m, v, name):
    shape = w.shape
    cols = shape[-1]
    rows = w.size // cols
    tile = _pick(rows, (512, 256, 128, 64, 32, 16, 8))
    spec = pl.BlockSpec((tile, cols), lambda i: (i, 0))

    def body(w_ref, g_ref, m_ref, v_ref, d_ref, nm_ref, nv_ref):
        grad = g_ref[...]
        nm = ADAM_B1 * m_ref[...] + (1.0 - ADAM_B1) * grad
        nv = ADAM_B2 * v_ref[...] + (1.0 - ADAM_B2) * (grad * grad)
        m_hat = nm / (1.0 - ADAM_B1 ** ADAM_STEP)
        v_hat = nv / (1.0 - ADAM_B2 ** ADAM_STEP)
        d_ref[...] = -ADAM_LR * (m_hat / (jnp.sqrt(v_hat) + ADAM_EPS) + ADAM_WD * w_ref[...])
        nm_ref[...] = nm
        nv_ref[...] = nv

    outs = pl.pallas_call(
        body, name=name, grid=(rows // tile,), in_specs=[spec] * 4, out_specs=[spec] * 3,
        out_shape=[_sds((rows, cols))] * 3, compiler_params=_params(1),
    )(*[t.reshape(rows, cols) for t in (w, g, m, v)])
    return tuple(t.reshape(shape) for t in outs)


def _place():
    return lax.axis_index("x"), lax.axis_index("y"), lax.axis_index("c")


def _flip(p, bits):
    return tuple(1 - v if (bits >> s) & 1 else v for v, s in zip(p, (2, 1, 0)))


def _slot(p):
    return 4 * p[0] + 2 * p[1] + p[2]


def _chip_of(p):
    return 2 * p[0] + p[1]


ANY = pl.BlockSpec(memory_space=pl.ANY)


class Gather:
    scratch = (pltpu.SemaphoreType.DMA((7,)), pltpu.SemaphoreType.DMA((7,)), pltpu.SemaphoreType.DMA)

    def __init__(self, shard):
        self.operand = shard
        self.out_shape = jax.ShapeDtypeStruct((N_DEV,) + shard.shape, shard.dtype)

    def bind(self, x_ref, out_ref, send_sems, recv_sems, local_sem):
        me = _place()
        sibling = _flip(me, 1)
        chips = [_flip(me, 4), _flip(me, 2), _flip(me, 6)]

        def copy(k, block, to, src=None):
            return pltpu.make_async_remote_copy(
                src_ref=out_ref.at[_slot(block)] if src is None else src, dst_ref=out_ref.at[_slot(block)],
                send_sem=send_sems.at[k], recv_sem=recv_sems.at[k], device_id=to, device_id_type=MESH)

        mine = pltpu.make_async_copy(x_ref, out_ref.at[_slot(me)], local_sem)
        first = [copy(0, me, sibling, src=x_ref)] + [copy(1 + j, me, chip, src=x_ref) for j, chip in enumerate(chips)]
        passed = [copy(4 + j, chip, sibling) for j, chip in enumerate(chips)]

        def start():
            mine.start()
            for cp in first:
                cp.start()

        def finish():
            for j, chip in enumerate(chips):
                copy(1 + j, chip, me).wait_recv()
                passed[j].start()
            copy(0, sibling, me).wait_recv()
            for j, chip in enumerate(chips):
                copy(4 + j, _flip(chip, 1), me).wait_recv()
            for cp in first + passed:
                cp.wait_send()
            mine.wait()

        return start, finish


class ChipExchange:
    scratch = (pltpu.SemaphoreType.DMA((3,)), pltpu.SemaphoreType.DMA((3,)), pltpu.SemaphoreType.DMA)

    def __init__(self, blocks):
        self.operand = blocks
        self.out_shape = jax.ShapeDtypeStruct(blocks.shape, blocks.dtype)

    def bind(self, x_ref, out_ref, send_sems, recv_sems, local_sem):
        me = _place()
        peers = [_flip(me, 4), _flip(me, 2), _flip(me, 6)]
        mine = pltpu.make_async_copy(x_ref.at[_chip_of(me)], out_ref.at[_chip_of(me)], local_sem)

        def copy(j, src_chip, dst_chip):
            return pltpu.make_async_remote_copy(
                src_ref=x_ref.at[src_chip], dst_ref=out_ref.at[dst_chip], send_sem=send_sems.at[j],
                recv_sem=recv_sems.at[j], device_id=peers[j], device_id_type=MESH)

        sends = [copy(j, _chip_of(peer), _chip_of(me)) for j, peer in enumerate(peers)]

        def start():
            mine.start()
            for cp in sends:
                cp.start()

        def finish():
            for j, peer in enumerate(peers):
                copy(j, _chip_of(me), _chip_of(peer)).wait_recv()
            for cp in sends:
                cp.wait_send()
            mine.wait()

        return start, finish


class Together:
    def __init__(self, *parts):
        self.parts = parts
        self.operands = [p.operand for p in parts]
        self.out_shapes = [p.out_shape for p in parts]
        self.scratch = [s for p in parts for s in p.scratch]

    def bind(self, x_refs, out_refs, sems):
        bound, at = [], 0
        for p, x_ref, out_ref in zip(self.parts, x_refs, out_refs):
            bound.append(p.bind(x_ref, out_ref, *sems[at:at + len(p.scratch)]))
            at += len(p.scratch)

        def start():
            for s, _ in bound:
                s()

        def finish():
            for _, f in bound:
                f()

        return start, finish


def exchange_alone(exchange, name):
    n = len(exchange.operands)

    def body(*refs):
        start, finish = exchange.bind(refs[:n], refs[n:2 * n], refs[2 * n:])
        start()
        finish()

    return pl.pallas_call(body, name=name, out_shape=exchange.out_shapes, in_specs=[ANY] * n, out_specs=[ANY] * n,
                          scratch_shapes=exchange.scratch)(*exchange.operands)


def _row_tile(rows):
    return max(t for t in range(16, min(rows, 1024) + 1, 16) if rows % t == 0)


def pair_exchange(blocks, name):
    _, rows, width = blocks.shape

    def body(x_ref, theirs_ref, send_sems, recv_sems):
        me = _place()
        remote = [pltpu.make_async_remote_copy(
            src_ref=x_ref.at[2 * q + 1 - me[2]], dst_ref=theirs_ref.at[q], send_sem=send_sems.at[q], recv_sem=recv_sems.at[q],
            device_id=_flip(me, 1), device_id_type=MESH) for q in range(4)]
        for cp in remote:
            cp.start()
        for cp in remote:
            cp.wait()

    return pl.pallas_call(
        body, name=name, out_shape=jax.ShapeDtypeStruct((4, rows, width), blocks.dtype), in_specs=[ANY], out_specs=ANY,
        scratch_shapes=[pltpu.SemaphoreType.DMA((4,)), pltpu.SemaphoreType.DMA((4,))])(blocks)


def pair_add(blocks, theirs, name):
    n, rows, width = theirs.shape
    tile = _row_tile(rows)
    spec = pl.BlockSpec((None, tile, width), lambda q, i: (q, i, 0))
    south = pl.BlockSpec((None, None, tile, width), lambda q, i: (q, 0, i, 0))
    north = pl.BlockSpec((None, None, tile, width), lambda q, i: (q, 1, i, 0))

    def body(s_ref, n_ref, b_ref, o_ref):
        mine = jnp.where(lax.axis_index("c") == 0, s_ref[...], n_ref[...])
        o_ref[...] = (mine.astype(f32) + b_ref[...].astype(f32)).astype(o_ref.dtype)

    by_core = blocks.reshape(n, 2, rows, width)
    return pl.pallas_call(body, name=name, grid=(n, rows // tile), in_specs=[south, north, spec], out_specs=spec,
                          out_shape=jax.ShapeDtypeStruct(theirs.shape, theirs.dtype), compiler_params=_params(2))(by_core, by_core, theirs)


def sum_slots(blocks, name):
    n, rows, width = blocks.shape
    tile = _row_tile(rows)

    def body(x_ref, o_ref):
        total = x_ref[0].astype(f32)
        for s in range(1, n):
            total = total + x_ref[s].astype(f32)
        o_ref[...] = total

    return pl.pallas_call(
        body, name=name, grid=(rows // tile,), in_specs=[pl.BlockSpec((n, tile, width), lambda i: (0, i, 0))],
        out_specs=pl.BlockSpec((tile, width), lambda i: (i, 0)), out_shape=_sds((rows, width)), compiler_params=_params(1))(blocks)


def all_reduce_small(x, name):
    rows, width = x.shape

    def body(x_ref, o_ref, land, send_sems, recv_sems):
        me = _place()
        copies = []
        for k in range(1, N_DEV):
            peer = _flip(me, k)
            copies.append(pltpu.make_async_remote_copy(
                src_ref=x_ref, dst_ref=land.at[_slot(me)], send_sem=send_sems.at[k - 1], recv_sem=recv_sems.at[k - 1],
                device_id=peer, device_id_type=MESH))
        for cp in copies:
            cp.start()
        land[_slot(me)] = x_ref[...]
        for k in range(1, N_DEV):
            peer = _flip(me, k)
            pltpu.make_async_remote_copy(
                src_ref=x_ref, dst_ref=land.at[_slot(peer)], send_sem=send_sems.at[k - 1], recv_sem=recv_sems.at[k - 1],
                device_id=peer, device_id_type=MESH).wait_recv()
        total = land[0]
        for s in range(1, N_DEV):
            total = total + land[s]
        o_ref[...] = total
        for cp in copies:
            cp.wait_send()

    return pl.pallas_call(
        body, name=name, out_shape=_sds((rows, width)), in_specs=[pl.BlockSpec(memory_space=pltpu.VMEM)],
        out_specs=pl.BlockSpec(memory_space=pltpu.VMEM),
        scratch_shapes=[pltpu.VMEM((N_DEV, rows, width), f32), pltpu.SemaphoreType.DMA((7,)), pltpu.SemaphoreType.DMA((7,))],
    )(x)


def _slab(rows):
    return -(-rows // 16) * 16


def _pack_big(shards, group):
    parts = []
    for name, rows in group:
        part = shards[name].astype(bf16).reshape(DEPTH, rows, D_MODEL)
        parts.append(jnp.pad(part, ((0, 0), (0, _slab(rows) - rows), (0, 0))))
    return jnp.concatenate(parts, axis=1)


def _unpack_gathered(gathered, group):
    full, at = {}, 0
    for name, rows in group:
        part = gathered[:, at:at + rows, :]
        at += _slab(rows)
        if name in COL_SHARDED:
            full[name] = part.reshape(N_DEV, D_MODEL, rows).transpose(1, 0, 2).reshape(D_MODEL, N_DEV * rows)
        else:
            full[name] = part.reshape(N_DEV * rows, D_MODEL)
    if "w_gate_up" in full:
        full["w_gate_up"] = _interleave_gate_up(full["w_gate_up"])
    if "w_in" in full:
        w_in = full.pop("w_in")
        full["w_main"] = jnp.concatenate([w_in[:, :AB_AT], w_in[:, AB_AT + 2 * GDN_HEADS:]], axis=1)
        full["w_ab"] = jnp.pad(w_in[:, AB_AT:AB_AT + 2 * GDN_HEADS], ((0, 0), (0, LANES - 2 * GDN_HEADS)))
    return full


def _pack_grads(grads, group):
    grads = dict(grads)
    if "w_main" in grads:
        main, ab = grads.pop("w_main"), grads.pop("w_ab")
        grads["w_in"] = jnp.concatenate([main[:, :AB_AT], ab[:, :2 * GDN_HEADS], main[:, AB_AT:]], axis=1)
    if "w_gate_up" in grads:
        grads["w_gate_up"] = _interleave_gate_up(grads["w_gate_up"], undo=True)
    parts = []
    for name, rows in group:
        g = grads[name]
        if name in COL_SHARDED:
            g = g.reshape(D_MODEL, N_DEV, rows).transpose(1, 0, 2)
        parts.append(jnp.pad(g.reshape(N_DEV, rows, D_MODEL), ((0, 0), (0, _slab(rows) - rows), (0, 0))))
    return jnp.concatenate(parts, axis=1)


def _unpack_shard(layers, group):
    flat = jnp.stack(layers)
    out, at = {}, 0
    for name, rows in group:
        part = flat[:, at:at + rows, :]
        at += _slab(rows)
        out[name] = part.reshape(DEPTH, D_MODEL, rows) if name in COL_SHARDED else part
    return out


def _rows_of(flat_len):
    return -(-flat_len // (8 * D_MODEL)) * 8


def _pack_small(parts):
    flat = jnp.concatenate([p.reshape(-1) for p in parts])
    rows = _rows_of(flat.shape[0])
    flat = jnp.pad(flat, (0, rows * D_MODEL - flat.shape[0]))
    return flat.reshape(rows, D_MODEL)


def _unpack_small(packed, like):
    flat, out, at = packed.reshape(-1), [], 0
    for p in like:
        out.append(flat[at:at + p.size].reshape(p.shape))
        at += p.size
    return out


def _rope_tables(positions):
    inv_freq = jnp.float32(ROPE_THETA) ** (-jnp.arange(0, ROPE_DIM, 2, dtype=f32) / ROPE_DIM)
    ang = positions.astype(f32)[:, None] * inv_freq
    cos, sin = jnp.cos(ang), jnp.sin(ang)
    rest = ATTN_HEAD_DIM - ROPE_DIM
    cos_h = jnp.concatenate([cos, cos, jnp.ones((cos.shape[0], rest), f32)], axis=1)
    sin_h = jnp.concatenate([-sin, sin, jnp.zeros((sin.shape[0], rest), f32)], axis=1)
    return jnp.tile(cos_h, (1, ATTN_HEADS)), jnp.tile(sin_h, (1, ATTN_HEADS))


HEAD_SMALL = ("norm_mix_pre", "conv_short", "conv_gdn", "gdn_a_log", "gdn_dt_bias")


def _layer_head(h, p, cos_t, sin_t):
    hn = rms_norm(h, p["norm_mix_pre"][None], "norm_mix_pre")
    proj = _linear(hn, p["w_main"], "w_main")
    ab = _linear(hn, p["w_ab"], "w_ab")
    aw, cw, gw = ATTN_WIDTH, CONV_WIDTH, GDN_WIDTH
    aq, ak, av, cb, cc, cx, gqkv, gate = _split_cols(proj, (aw, aw, aw, cw, cw, cw, 3 * gw, gw))
    y_attn = dilated_attention(rope(aq, cos_t, sin_t, ATTN_HEAD_DIM ** -0.5, "rope_q"), rope(ak, cos_t, sin_t, 1.0, "rope_k"),
                               av, "attn")
    y_conv = short_conv(cb, cc, cx, p["conv_short"], "short_conv")
    qkv = gdn_pre(gqkv, p["conv_gdn"], "gdn_pre")
    pv = jnp.zeros((8, LANES), f32).at[0, :GDN_HEADS].set(p["gdn_a_log"]).at[1, :GDN_HEADS].set(p["gdn_dt_bias"])
    g, beta = gate_beta(ab, pv, "gate_beta")
    return (*_split_cols(qkv, (gw, gw, gw)), g, beta), (gate, y_attn, y_conv)


def _layer_tail(h, o, gate, y_attn, y_conv, p, mem):
    y_gdn = gdn_post(o, gate, p["gdn_norm"][None], "gdn_post")
    mix = _linear(jnp.concatenate([y_attn, y_conv, y_gdn], axis=1), p["w_out"], "w_out")
    h = add_norm(h, mix, p["norm_mix_post"][None], "norm_mix_post")

    hn = rms_norm(h, p["norm_xattn_pre"][None], "norm_xattn_pre")
    qx = _linear(hn, p["w_xq"], "w_xq")
    kv = _linear(rms_norm(mem, p["norm_mem"][None], "norm_mem"), p["w_xkv"], "w_xkv")
    xa = _linear(cross_attention(qx, kv, "xattn"), p["w_xo"], "w_xo")
    h = add_norm(h, xa, p["norm_xattn_post"][None], "norm_xattn_post")

    hn = rms_norm(h, p["norm_ffn_pre"][None], "norm_ffn_pre")
    return add_norm(h, swiglu_ffn(hn, p["w_gate_up"], p["w_down"], "ffn"), p["norm_ffn_post"][None], "norm_ffn_post")


def _pair_summed(grads, group, name):
    blocks = _pack_grads(grads, group)
    return pair_add(blocks, pair_exchange(blocks, name + "_pair_exchange"), name + "_pair_add")


def _forward_backward(x, head_rows, tail_rows, small, mem, cos_t, sin_t, target):
    h = x
    head_gathered, = exchange_alone(Together(Gather(head_rows[0])), "gather_first")
    saved = []
    for layer in range(DEPTH):
        at_layer = {n: t[layer] for n, t in small.items()}
        head_p = {**_unpack_gathered(head_gathered, HEAD_BIG), **{n: at_layer[n] for n in HEAD_SMALL}}
        (rule_in, rest), head_vjp = jax.vjp(lambda h, hp: _layer_head(h, hp, cos_t, sin_t), h, head_p)
        carried = [Gather(tail_rows[layer])] + ([Gather(head_rows[layer + 1])] if layer + 1 < DEPTH else [])
        results = delta_rule_fwd(*rule_in, "delta_rule_fwd", Together(*carried))
        (o, s_in, inv), landed = results[:3], results[3:]
        if layer + 1 < DEPTH:
            head_gathered = landed[1]
        tail_p = {**_unpack_gathered(landed[0], TAIL_BIG), **{n: t for n, t in at_layer.items() if n not in HEAD_SMALL}}
        h, tail_vjp = jax.vjp(lambda h, o, rest, tp: _layer_tail(h, o, *rest, tp, mem), h, o, rest, tail_p)
        saved.append((head_vjp, tail_vjp, rule_in, s_in, inv))

    loss, dh = jax.value_and_grad(lambda y: loss_rows(y, target, "loss"))(h)

    head_grads, tail_grads, small_grads, head_pending = [None] * DEPTH, [None] * DEPTH, [None] * DEPTH, None
    for layer in reversed(range(DEPTH)):
        head_vjp, tail_vjp, rule_in, s_in, inv = saved[layer]
        dh_tail, do, d_rest, d_tail_p = tail_vjp(dh)
        carried = [ChipExchange(_pair_summed(d_tail_p, TAIL_BIG, "tail"))]
        if head_pending is not None:
            carried.append(ChipExchange(head_pending))
        results = delta_rule_bwd(*rule_in, s_in, inv, do, "delta_rule_bwd", Together(*carried))
        d_rule_in, landed = results[:5], results[5:]
        tail_grads[layer] = sum_slots(landed[0], "sum_tail_grads")
        if head_pending is not None:
            head_grads[layer + 1] = sum_slots(landed[1], "sum_head_grads")
        dh_head, d_head_p = head_vjp((tuple(d_rule_in), d_rest))
        dh = dh_tail + dh_head
        small_grads[layer] = {n: t for n, t in {**d_head_p, **d_tail_p}.items() if n in small}
        head_pending = _pair_summed(d_head_p, HEAD_BIG, "head")
    head_landed, = exchange_alone(Together(ChipExchange(head_pending)), "exchange_last")
    head_grads[0] = sum_slots(head_landed, "sum_head_grads")
    return loss, dh, head_grads, tail_grads, small_grads


def kernel(x, mem, positions, norm_mix_pre, norm_mix_post, w_in, conv_short, conv_gdn, gdn_a_log, gdn_dt_bias, gdn_norm, w_out, norm_mem, norm_xattn_pre, norm_xattn_post, w_xq, w_xkv, w_xo, norm_ffn_pre, norm_ffn_post, w_gate_up, w_down, loss_target, m_norm_mix_pre, m_norm_mix_post, m_w_in, m_conv_short, m_conv_gdn, m_gdn_a_log, m_gdn_dt_bias, m_gdn_norm, m_w_out, m_norm_mem, m_norm_xattn_pre, m_norm_xattn_post, m_w_xq, m_w_xkv, m_w_xo, m_norm_ffn_pre, m_norm_ffn_post, m_w_gate_up, m_w_down, v_norm_mix_pre, v_norm_mix_post, v_w_in, v_conv_short, v_conv_gdn, v_gdn_a_log, v_gdn_dt_bias, v_gdn_norm, v_w_out, v_norm_mem, v_norm_xattn_pre, v_norm_xattn_post, v_w_xq, v_w_xkv, v_w_xo, v_norm_ffn_pre, v_norm_ffn_post, v_w_gate_up, v_w_down):
    given = dict(locals())
    weights = {n: given[n] for n in WEIGHTS}
    me = _slot(_place())

    def in_place(shard):
        full = jnp.zeros(shard.shape[:-1] + (shard.shape[-1] * N_DEV,), f32)
        return lax.dynamic_update_slice_in_dim(full, shard, me * shard.shape[-1], axis=shard.ndim - 1)

    placed = [in_place(conv_short), in_place(conv_gdn)]
    conv_short_full, conv_gdn_full = _unpack_small(all_reduce_small(_pack_small(placed), "gather_conv"), placed)
    small = {n: weights[n] for n in NORMS + ("gdn_a_log", "gdn_dt_bias", "gdn_norm")}
    small["conv_short"], small["conv_gdn"] = conv_short_full, conv_gdn_full

    cos_t, sin_t = _rope_tables(positions[0])
    loss, grad_x, head_grads, tail_grads, small_layers = _forward_backward(
        x[0], _pack_big(weights, HEAD_BIG), _pack_big(weights, TAIL_BIG), small, mem[0], cos_t, sin_t, loss_target[0])
    grads = {**_unpack_shard(head_grads, HEAD_BIG), **_unpack_shard(tail_grads, TAIL_BIG)}

    names = sorted(small)
    parts = [jnp.stack([layer[n] for layer in small_layers]) for n in names] + [loss.reshape(1)]
    reduced = _unpack_small(all_reduce_small(_pack_small(parts), "reduce_small"), parts)
    loss = reduced[-1][0]
    for n, g in zip(names, reduced[:-1]):
        if n in ("conv_short", "conv_gdn"):
            width = weights[n].shape[-1]
            g = lax.dynamic_slice_in_dim(g, me * width, width, axis=g.ndim - 1)
        grads[n] = g

    delta, new_m, new_v = {}, {}, {}
    for n in WEIGHTS:
        delta[n], new_m[n], new_v[n] = adamw(weights[n], grads[n], given["m_" + n], given["v_" + n], "adamw_" + n)
    return (loss, grad_x[None], *[grads[n] for n in WEIGHTS], *[delta[n] for n in WEIGHTS],
            *[new_m[n] for n in WEIGHTS], *[new_v[n] for n in WEIGHTS])
```

```python
import functools

import jax
import jax.numpy as jnp
from jax import lax
from jax.experimental import pallas as pl
from jax.experimental.pallas import tpu as pltpu

f32 = jnp.float32
bf16 = jnp.bfloat16
HIGHEST = lax.Precision.HIGHEST
MESH = pl.DeviceIdType.MESH

N_DEV = 8
DEPTH = 4
D_MODEL = 1024
EPS = 1e-6
ATTN_HEADS, ATTN_HEAD_DIM = 4, 64
ATTN_WIDTH = ATTN_HEADS * ATTN_HEAD_DIM
DILATIONS = (1, 4, 16)
QB = 128
ROPE_THETA = 500000.0
ROPE_DIM = ATTN_HEAD_DIM // 4
CONV_WIDTH = 256
GDN_HEADS, GDN_HEAD_DIM = 4, 128
GDN_WIDTH = GDN_HEADS * GDN_HEAD_DIM
GDN_CHUNK = 64
XATTN_HEADS, XATTN_HEAD_DIM = 4, 256
FFN_HIDDEN = 2816
IN_WIDTH = 3592
AB_AT = 3 * ATTN_WIDTH + 3 * CONV_WIDTH + 3 * GDN_WIDTH
MAIN_WIDTH = IN_WIDTH - 2 * GDN_HEADS
LANES = 128
ROW_TILE = 256
VMEM_LIMIT = 56 * 1024 * 1024

ADAM_LR, ADAM_B1, ADAM_B2, ADAM_EPS, ADAM_WD, ADAM_STEP = 0.001, 0.9, 0.999, 1e-08, 0.01, 10

COL_SHARDED = ("w_in", "w_xkv", "w_gate_up")
ROW_SHARDED = (("w_out", 128), ("w_xq", 128), ("w_xo", 128), ("w_down", 352))
HEAD_GROUP = ("w_in",)
TAIL_GROUP = ("w_gate_up", "w_xkv", "rows")
NORMS = ("norm_mix_pre", "norm_mix_post", "norm_mem", "norm_xattn_pre", "norm_xattn_post", "norm_ffn_pre", "norm_ffn_post")
WEIGHTS = ("norm_mix_pre", "norm_mix_post", "w_in", "conv_short", "conv_gdn", "gdn_a_log", "gdn_dt_bias", "gdn_norm", "w_out",
           "norm_mem", "norm_xattn_pre", "norm_xattn_post", "w_xq", "w_xkv", "w_xo", "norm_ffn_pre", "norm_ffn_post",
           "w_gate_up", "w_down")


def _params(n_grid):
    return pltpu.CompilerParams(dimension_semantics=("arbitrary",) * n_grid, vmem_limit_bytes=VMEM_LIMIT)


def _pick(n, cands):
    for c in cands:
        if n % c == 0:
            return c
    return n


MXU_FLOPS = 9.0e14
HBM_BYTES_PER_S = 2.5e12
VMEM_RMW_BYTES_PER_S = 7.0e12
STEP_S = 0.4e-6
MATMUL_VMEM = 44 * 1024 * 1024


def _tiles(m, n, k, sa, sb, so, tn=None):
    def divisors(d):
        return sorted({d // s for s in range(1, d // LANES + 1) if d % s == 0 and (d // s) % LANES == 0}, reverse=True)

    best = None
    for tk in divisors(k):
        nk = k // tk
        for tm in divisors(m):
            for tn_ in [tn] if tn else divisors(n):
                per_step = tm * tk * sa + tk * tn_ * sb + tm * tn_ * so
                vmem = 2 * per_step + (tm * tn_ * 4 if nk > 1 else 0)
                vmem += (tm * tk * 2 if sa == 4 else 0) + (tk * tn_ * 2 if sb == 4 else 0) + tm * tn_ * 4
                if vmem > MATMUL_VMEM:
                    continue
                moved = m * k * sa * (1 if nk == 1 else n // tn_) + k * n * sb * (1 if nk == 1 and n == tn_ else m // tm) + m * n * so
                busy = 2 * m * n * k / MXU_FLOPS + (m * n * 8 * nk / VMEM_RMW_BYTES_PER_S if nk > 1 else 0)
                cost = max(moved / HBM_BYTES_PER_S, busy) + per_step / HBM_BYTES_PER_S + (m // tm) * (n // tn_) * nk * STEP_S
                if best is None or cost < best[0]:
                    best = (cost, tm, tn_, tk)
    return best[1:]


def _mm(a, b, ta, tb, out_dtype, name, finish=None):
    m, k = (a.shape[1], a.shape[0]) if ta else a.shape
    n = b.shape[0] if tb else b.shape[1]
    tm, tn, tk = _tiles(m, n, k, a.dtype.itemsize, b.dtype.itemsize, jnp.dtype(out_dtype).itemsize, finish and finish[0])
    nk = k // tk
    if finish:
        assert nk == 1
        _, extra, results, function = finish
        dims = (((0 if ta else 1,), (1 if tb else 0,)), ((), ()))

        def finish_body(a_ref, b_ref, *refs):
            p = lax.dot_general(a_ref[...].astype(bf16), b_ref[...].astype(bf16), dims, preferred_element_type=f32)
            outs = function(p, *[r[...] for r in refs[:len(extra)]])
            for r, o in zip(refs[len(extra):], outs):
                r[...] = o.astype(r.dtype)

        return pl.pallas_call(
            finish_body, name=name, grid=(m // tm, n // tn),
            in_specs=[pl.BlockSpec((tk, tm), lambda i, j: (0, i)) if ta else pl.BlockSpec((tm, tk), lambda i, j: (i, 0)),
                      pl.BlockSpec((tn, tk), lambda i, j: (j, 0)) if tb else pl.BlockSpec((tk, tn), lambda i, j: (0, j))]
            + [pl.BlockSpec((tm, cols), lambda i, j: (i, j)) for _, cols in extra],
            out_specs=[pl.BlockSpec((tm, cols), lambda i, j: (i, j)) for _, cols in results],
            out_shape=[jax.ShapeDtypeStruct((m, n // tn * cols), dt) for dt, cols in results],
            compiler_params=_params(2))(a, b, *[x for x, _ in extra])
    a_spec = pl.BlockSpec((tk, tm), lambda i, j, kk: (kk, i)) if ta else pl.BlockSpec((tm, tk), lambda i, j, kk: (i, kk))
    b_spec = pl.BlockSpec((tn, tk), lambda i, j, kk: (j, kk)) if tb else pl.BlockSpec((tk, tn), lambda i, j, kk: (kk, j))
    dims = (((0 if ta else 1,), (1 if tb else 0,)), ((), ()))

    def body(a_ref, b_ref, o_ref, *acc):
        kk = pl.program_id(2)
        p = lax.dot_general(a_ref[...].astype(bf16), b_ref[...].astype(bf16), dims, preferred_element_type=f32)
        if nk == 1:
            o_ref[...] = p.astype(o_ref.dtype)
            return
        acc_ref, = acc

        @pl.when(kk == 0)
        def _():
            acc_ref[...] = p

        @pl.when(kk > 0)
        def _():
            acc_ref[...] += p

        @pl.when(kk == nk - 1)
        def _():
            o_ref[...] = acc_ref[...].astype(o_ref.dtype)

    return pl.pallas_call(
        body, name=name, grid=(m // tm, n // tn, nk), in_specs=[a_spec, b_spec],
        out_specs=pl.BlockSpec((tm, tn), lambda i, j, kk: (i, j)), out_shape=jax.ShapeDtypeStruct((m, n), out_dtype),
        scratch_shapes=[pltpu.VMEM((tm, tn), f32)] if nk > 1 else [], compiler_params=_params(3))(a, b)


def _linear(x, w, name):
    @jax.custom_vjp
    def lin(x, w):
        return _mm(x, w, False, False, f32, name + "_y")

    def lin_f(x, w):
        return _mm(x, w, False, False, f32, name + "_y"), (x, w)

    def lin_b(res, dy):
        x, w = res
        return _mm(dy, w, False, True, f32, name + "_dx"), _mm(x, dy, True, False, bf16, name + "_dw")

    lin.defvjp(lin_f, lin_b)
    return lin(x, w)


GATE_UP_TILE = 512


def _interleave_gate_up(w, undo=False):
    two_f = w.shape[1]
    half = GATE_UP_TILE // 2
    nb = two_f // GATE_UP_TILE
    if undo:
        order = [2 * j + side for side in range(2) for j in range(nb)]
    else:
        order = [side * nb + j for j in range(nb) for side in range(2)]
    return jnp.concatenate([w[:, b * half:(b + 1) * half] for b in order], axis=1)


def swiglu_ffn(hn, w_gate_up, w_down, name):
    half = GATE_UP_TILE // 2

    def act_of(p):
        return p, jax.nn.silu(p[:, :half]) * p[:, half:]

    def d_gate_up_of(d_act, gate_up):
        g, u = gate_up[:, :half].astype(f32), gate_up[:, half:].astype(f32)
        sig = jax.nn.sigmoid(g)
        return (jnp.concatenate([d_act * u * sig * (1.0 + g * (1.0 - sig)), d_act * g * sig], axis=1),)

    def forward(hn, w_gate_up, w_down):
        gate_up, act = _mm(hn, w_gate_up, False, False, bf16, name + "_act",
                           (GATE_UP_TILE, [], [(bf16, GATE_UP_TILE), (bf16, half)], act_of))
        return _mm(act, w_down, False, False, f32, name + "_y"), (hn, w_gate_up, w_down, gate_up, act)

    def backward(res, dy):
        hn, w_gate_up, w_down, gate_up, act = res
        d_gate_up, = _mm(dy, w_down, False, True, bf16, name + "_dact",
                         (half, [(gate_up, GATE_UP_TILE)], [(bf16, GATE_UP_TILE)], d_gate_up_of))
        return (_mm(d_gate_up, w_gate_up, False, True, f32, name + "_dx"), _mm(hn, d_gate_up, True, False, bf16, name + "_dw1"),
                _mm(act, dy, True, False, bf16, name + "_dw2"))

    @jax.custom_vjp
    def op(hn, w_gate_up, w_down):
        return forward(hn, w_gate_up, w_down)[0]

    op.defvjp(forward, backward)
    return op(hn, w_gate_up, w_down)


def _split_cols(x, widths):
    edges = [sum(widths[:i]) for i in range(len(widths) + 1)]

    def cut(x):
        return tuple(x[:, a:b] for a, b in zip(edges[:-1], edges[1:]))

    @jax.custom_vjp
    def split(x):
        return cut(x)

    split.defvjp(lambda x: (cut(x), None), lambda _, cts: (jnp.concatenate(cts, axis=1),))
    return split(x)


def _block_op(name, f, grid, in_specs, out_defs, arrays, diff, acc=None, gdefs=None):
    acc, gdefs = acc or {}, gdefs or {}
    n_in, n_out, n_grid = len(in_specs), len(out_defs), len(grid)

    def fwd_call(*xs):
        def body(*refs):
            outs = f(*[r[...] for r in refs[:n_in]])
            for r, o in zip(refs[n_in:], outs):
                r[...] = o.astype(r.dtype)

        return pl.pallas_call(
            body, name=name + "_fwd", grid=grid, in_specs=in_specs, out_specs=[d[1] for d in out_defs],
            out_shape=[d[0] for d in out_defs], compiler_params=_params(n_grid))(*xs)

    def bwd_call(*xs_and_cts):
        def body(*refs):
            xs = [r[...] for r in refs[:n_in]]
            cts = tuple(r[...] for r in refs[n_in:n_in + n_out])

            def of_diff(*dx):
                full = list(xs)
                for i, v in zip(diff, dx):
                    full[i] = v
                return tuple(f(*full))

            _, vjp = jax.vjp(of_diff, *[xs[i] for i in diff])
            grads = vjp(cts)
            for i, g, r in zip(diff, grads, refs[n_in + n_out:]):
                if i in acc:
                    first = functools.reduce(jnp.logical_and, [pl.program_id(a) == 0 for a in acc[i]])

                    @pl.when(first)
                    def _(r=r):
                        r[...] = jnp.zeros_like(r)

                    r[...] += g.astype(r.dtype)
                else:
                    r[...] = g.astype(r.dtype)

        g_defs = [gdefs.get(i, (jax.ShapeDtypeStruct(arrays[i].shape, f32), in_specs[i])) for i in diff]
        return pl.pallas_call(
            body, name=name + "_bwd", grid=grid, in_specs=list(in_specs) + [d[1] for d in out_defs],
            out_specs=[d[1] for d in g_defs], out_shape=[d[0] for d in g_defs], compiler_params=_params(n_grid))(*xs_and_cts)

    return fwd_call, bwd_call


def _simple_op(name, f, grid, in_specs, out_defs, arrays, diff, acc=None):
    fwd_call, bwd_call = _block_op(name, f, grid, in_specs, out_defs, arrays, diff, acc)

    @jax.custom_vjp
    def op(*xs):
        return tuple(fwd_call(*xs))

    def op_f(*xs):
        return tuple(fwd_call(*xs)), xs

    def op_b(xs, cts):
        grads = bwd_call(*xs, *cts)
        out = [jnp.zeros_like(x) for x in xs]
        for i, g in zip(diff, grads):
            out[i] = g
        return tuple(out)

    op.defvjp(op_f, op_b)
    return op(*arrays)


def _rows(width, tile=ROW_TILE):
    return pl.BlockSpec((tile, width), lambda i: (i, 0))


def _whole(shape):
    return pl.BlockSpec(shape, lambda *_: (0,) * len(shape))


def _sds(shape):
    return jax.ShapeDtypeStruct(shape, f32)


def _rms(x, w):
    return x * lax.rsqrt(jnp.mean(x * x, axis=-1, keepdims=True) + EPS) * w


def rms_norm(x, w, name):
    r, d = x.shape
    return _simple_op(name, lambda x, w: (_rms(x, w),), (r // ROW_TILE,), [_rows(d), _whole((1, d))],
                      [(_sds((r, d)), _rows(d))], (x, w), (0, 1), {1: (0,)})[0]


def add_norm(h, y, w, name):
    r, d = h.shape
    return _simple_op(name, lambda h, y, w: (h + _rms(y, w),), (r // ROW_TILE,), [_rows(d), _rows(d), _whole((1, d))],
                      [(_sds((r, d)), _rows(d))], (h, y, w), (0, 1, 2), {2: (0,)})[0]


def _swap8(x):
    def raw(x):
        lane = lax.broadcasted_iota(jnp.int32, x.shape, 1) % ATTN_HEAD_DIM
        half = ROPE_DIM // 2
        up = pltpu.roll(x, x.shape[1] - half, axis=1)
        down = pltpu.roll(x, half, axis=1)
        return jnp.where(lane < half, up, jnp.where(lane < ROPE_DIM, down, 0.0))

    @jax.custom_vjp
    def swap(x):
        return raw(x)

    swap.defvjp(lambda x: (raw(x), None), lambda _, g: (raw(g),))
    return swap(x)


def rope(x, cos_t, sin_t, scale, name):
    r, d = x.shape
    return _simple_op(name, lambda x, c, s: ((x * c + _swap8(x) * s) * scale,), (r // ROW_TILE,), [_rows(d)] * 3,
                      [(_sds((r, d)), _rows(d))], (x, cos_t, sin_t), (0,))[0]


def _shift_rows(x, k):
    n = x.shape[0]

    def down(x):
        row = lax.broadcasted_iota(jnp.int32, x.shape, 0)
        return jnp.where(row >= k, pltpu.roll(x, k, axis=0), 0.0)

    def up(x):
        row = lax.broadcasted_iota(jnp.int32, x.shape, 0)
        return jnp.where(row < n - k, pltpu.roll(x, n - k, axis=0), 0.0)

    @jax.custom_vjp
    def shift(x):
        return down(x)

    shift.defvjp(lambda x: (down(x), None), lambda _, g: (up(g),))
    return shift(x)


def _causal_conv(x, w):
    taps = w.shape[0]
    y = x * w[taps - 1:taps, :]
    for j in range(taps - 1):
        y = y + _shift_rows(x, taps - 1 - j) * w[j:j + 1, :]
    return y


def _cols(rows, at=0):
    return pl.BlockSpec((rows, LANES), lambda j: (0, at + j))


def short_conv(cb, cc, cx, w, name):
    s, c = cb.shape
    taps = w.shape[0]
    return _simple_op(name, lambda b, c_, x, w: (b * _causal_conv(c_ * x, w),), (c // LANES,),
                      [_cols(s)] * 3 + [_cols(taps)], [(_sds((s, c)), _cols(s))], (cb, cc, cx, w), (0, 1, 2, 3))[0]


def gdn_pre(qkv, w, name):
    s, c = qkv.shape
    taps = w.shape[0]

    def f(x, w):
        j = pl.program_id(0)
        y = jax.nn.silu(_causal_conv(x, w))
        normed = y * lax.rsqrt(jnp.sum(y * y, axis=-1, keepdims=True) + EPS)
        scale = jnp.where(j < GDN_HEADS, GDN_HEAD_DIM ** -0.5, 1.0).astype(f32)
        return (jnp.where(j < 2 * GDN_HEADS, normed * scale, y),)

    return _simple_op(name, f, (c // LANES,), [_cols(s), _cols(taps)], [(_sds((s, c)), _cols(s))], (qkv, w), (0, 1))[0]


def gate_beta(ab, pv, name):
    s = ab.shape[0]

    def f(ab, pv):
        row = lax.broadcasted_iota(jnp.int32, (LANES, GDN_WIDTH), 0)
        head = lax.broadcasted_iota(jnp.int32, (LANES, GDN_WIDTH), 1) // GDN_HEAD_DIM
        spread_a = (row == head).astype(f32)
        spread_b = (row == head + GDN_HEADS).astype(f32)
        a = jnp.dot(ab, spread_a, precision=HIGHEST, preferred_element_type=f32)
        b = jnp.dot(ab, spread_b, precision=HIGHEST, preferred_element_type=f32)
        p = jnp.dot(pv, spread_a, precision=HIGHEST, preferred_element_type=f32)
        g = -jnp.exp(p[0:1, :]) * jax.nn.softplus(a + p[1:2, :])
        return g, jax.nn.sigmoid(b)

    outs = [(_sds((s, GDN_WIDTH)), _rows(GDN_WIDTH))] * 2
    return _simple_op(name, f, (s // ROW_TILE,), [_rows(LANES), _whole((8, LANES))], outs, (ab, pv), (0, 1), {1: (0,)})


def gdn_post(o, gate, w, name):
    s, c = o.shape
    spec = pl.BlockSpec((ROW_TILE, LANES), lambda i, j: (i, j))
    return _simple_op(name, lambda o, g, w: (_rms(o, w) * jax.nn.silu(g),), (s // ROW_TILE, c // LANES),
                      [spec, spec, _whole((1, LANES))], [(_sds((s, c)), spec)], (o, gate, w), (0, 1, 2), {2: (0, 1)})[0]


def attn_merge(outs, lses, name):
    s, c = outs[0].shape

    def f(o1, o2, o3, l1, l2, l3):
        m = lax.stop_gradient(jnp.maximum(jnp.maximum(l1, l2), l3))
        e1, e2, e3 = jnp.exp(l1 - m), jnp.exp(l2 - m), jnp.exp(l3 - m)
        return ((e1 * o1 + e2 * o2 + e3 * o3) / (e1 + e2 + e3),)

    return _simple_op(name, f, (s // ROW_TILE,), [_rows(c)] * 6, [(_sds((s, c)), _rows(c))], (*outs, *lses), tuple(range(6)))[0]


def loss_rows(y, target, name):
    s, d = y.shape
    nt = s // ROW_TILE

    def f(y, t):
        e = y - t
        part = 0.5 * jnp.sum(jnp.mean(e * e, axis=-1, keepdims=True), axis=0, keepdims=True)
        return (jnp.broadcast_to(part * (1.0 / (8 * LANES)), (8, LANES)),)

    out = _simple_op(name, f, (nt,), [_rows(d)] * 2, [(_sds((nt * 8, LANES)), pl.BlockSpec((8, LANES), lambda i: (i, 0)))],
                     (y, target), (0,))[0]
    return jnp.sum(out)


def _mxu(a, b, form):
    dims = {"nn": ((1,), (0,)), "nt": ((1,), (1,)), "tn": ((0,), (0,))}

    def raw(a, b, form):
        return lax.dot_general(a.astype(bf16), b.astype(bf16), (dims[form], ((), ())), preferred_element_type=f32)

    @jax.custom_vjp
    def prod(a, b):
        return raw(a, b, form)

    def prod_b(res, ct):
        a, b = res
        if form == "nn":
            return raw(ct, b, "nt"), raw(a, ct, "tn")
        if form == "nt":
            return raw(ct, b, "nn"), raw(ct, a, "tn")
        return raw(b, ct, "nt"), raw(a, ct, "nn")

    prod.defvjp(lambda a, b: (raw(a, b, form), (a, b)), prod_b)
    return prod(a, b)


def band_attention(q, k, v, nb, name):
    r, qb, width = q.shape
    dh = ATTN_HEAD_DIM

    def f(q, kp, kc, vp, vc):
        has_prev = (pl.program_id(0) % nb) > 0
        i = lax.broadcasted_iota(jnp.int32, (qb, 2 * qb), 0)
        j = lax.broadcasted_iota(jnp.int32, (qb, 2 * qb), 1)
        seen = jnp.logical_or(jnp.logical_and(jnp.logical_and(j < qb, j >= i), has_prev), jnp.logical_and(j >= qb, j - qb <= i))
        keys, values = jnp.concatenate([kp, kc], axis=0), jnp.concatenate([vp, vc], axis=0)
        outs, lses = [], []
        for hd in range(width // dh):
            at = slice(hd * dh, (hd + 1) * dh)
            sc = jnp.where(seen, _mxu(q[:, at], keys[:, at], "nt"), -jnp.inf)
            m = lax.stop_gradient(jnp.max(sc, axis=-1, keepdims=True))
            p = jnp.exp(sc - m)
            l = jnp.sum(p, axis=-1, keepdims=True)
            outs.append(_mxu(p / l, values[:, at], "nn"))
            lses.append(jnp.broadcast_to(m + jnp.log(l), (qb, dh)))
        return jnp.concatenate(outs, axis=1), jnp.concatenate(lses, axis=1)

    blk = (None, qb, width)
    cur = pl.BlockSpec(blk, lambda b: (b, 0, 0))
    prev = pl.BlockSpec(blk, lambda b: (jnp.maximum(b - 1, 0), 0, 0))
    shape = _sds((r, qb, width))
    fwd_call, bwd_call = _block_op(name, f, (r,), [cur, prev, cur, prev, cur], [(shape, cur), (shape, cur)],
                                   (q, k, k, v, v), (0, 1, 2, 3, 4), gdefs={1: (shape, cur), 3: (shape, cur)})

    def to_prev(g):
        return jnp.concatenate([g[1:], jnp.zeros_like(g[:1])], axis=0)

    @jax.custom_vjp
    def op(q, k, v):
        return tuple(fwd_call(q, k, k, v, v))

    def op_b(res, cts):
        q, k, v = res
        dq, dkp, dkc, dvp, dvc = bwd_call(q, k, k, v, v, *cts)
        return dq, dkc + to_prev(dkp), dvc + to_prev(dvp)

    op.defvjp(lambda q, k, v: (tuple(fwd_call(q, k, k, v, v)), (q, k, v)), op_b)
    return op(q, k, v)


def dilated_attention(q, k, v, name):
    s = q.shape[0]
    outs, lses = [], []
    for d in DILATIONS:
        length = s // d
        nb = length // QB

        def to_residue(t):
            return t.reshape(length, d, ATTN_WIDTH).transpose(1, 0, 2).reshape(d * nb, QB, ATTN_WIDTH)

        def from_residue(t):
            return t.reshape(d, length, ATTN_WIDTH).transpose(1, 0, 2).reshape(s, ATTN_WIDTH)

        o, lse = band_attention(to_residue(q), to_residue(k), to_residue(v), nb, f"{name}_d{d}")
        outs.append(from_residue(o))
        lses.append(from_residue(lse))
    return attn_merge(outs, lses, name + "_merge")


def cross_attention(q, kv, name):
    s = q.shape[0]
    m = kv.shape[0]
    width = XATTN_HEADS * XATTN_HEAD_DIM
    tq = 512

    def f(q, k, v):
        sc = _mxu(q, k, "nt") * (XATTN_HEAD_DIM ** -0.5)
        mx = lax.stop_gradient(jnp.max(sc, axis=-1, keepdims=True))
        p = jnp.exp(sc - mx)
        return (_mxu(p / jnp.sum(p, axis=-1, keepdims=True), v, "nn"),)

    q_spec = pl.BlockSpec((tq, XATTN_HEAD_DIM), lambda a, i: (i, a))
    k_spec = pl.BlockSpec((m, XATTN_HEAD_DIM), lambda a, i: (0, a))
    v_spec = pl.BlockSpec((m, XATTN_HEAD_DIM), lambda a, i: (0, a + XATTN_HEADS))
    half = _sds((m, width))
    fwd_call, bwd_call = _block_op(name, f, (XATTN_HEADS, s // tq), [q_spec, k_spec, v_spec], [(_sds((s, width)), q_spec)],
                                   (q, kv, kv), (0, 1, 2), acc={1: (1,), 2: (1,)}, gdefs={1: (half, k_spec), 2: (half, k_spec)})

    @jax.custom_vjp
    def op(q, kv):
        return fwd_call(q, kv, kv)[0]

    def op_b(res, ct):
        q, kv = res
        dq, dk, dv = bwd_call(q, kv, kv, ct)
        return dq, jnp.concatenate([dk, dv], axis=1)

    op.defvjp(lambda q, kv: (fwd_call(q, kv, kv)[0], (q, kv)), op_b)
    return op(q, kv)


def _hi(a, b, form="nn"):
    dims = {"nn": ((1,), (0,)), "nt": ((1,), (1,)), "tn": ((0,), (0,))}[form]
    return lax.dot_general(a, b, (dims, ((), ())), precision=lax.Precision.HIGH, preferred_element_type=f32)


def _running_sum(g):
    def raw(x, form):
        c = x.shape[0]
        tri = (lax.broadcasted_iota(jnp.int32, (c, c), 0) >= lax.broadcasted_iota(jnp.int32, (c, c), 1)).astype(bf16)
        hi = x.astype(bf16)
        rest = x - hi.astype(f32)
        mid = rest.astype(bf16)
        low = (rest - mid.astype(f32)).astype(bf16)
        dims = (((1,) if form == "nn" else (0,), (0,)), ((), ()))
        return sum(lax.dot_general(tri, part, dims, preferred_element_type=f32) for part in (hi, mid, low))

    @jax.custom_vjp
    def run(x):
        return raw(x, "nn")

    run.defvjp(lambda x: (raw(x, "nn"), None), lambda _, ct: (raw(ct, "tn"),))
    return run(g)


def _unit_lower_inverse(a):
    c = a.shape[0]
    eye = (lax.broadcasted_iota(jnp.int32, (c, c), 0) == lax.broadcasted_iota(jnp.int32, (c, c), 1)).astype(f32)
    inv, power = eye - a, -a
    for _ in range(c.bit_length() - 2):
        power = _hi(power, power)
        inv = inv + _hi(inv, power)
    return inv


def _known_inverse(a, t):
    @jax.custom_vjp
    def inv(a, t):
        return t

    def inv_b(t, ct):
        return -_hi(_hi(t, ct, "tn"), t, "nt"), jnp.zeros_like(t)

    inv.defvjp(lambda a, t: (t, t), inv_b)
    return inv(a, t)


def _delta_chunk(q, k, v, g, beta, s0, known_inv=None):
    c = q.shape[0]
    i = lax.broadcasted_iota(jnp.int32, (c, c), 0)
    j = lax.broadcasted_iota(jnp.int32, (c, c), 1)
    causal, strict = i >= j, i > j
    dec = _running_sum(g)
    dec_i = dec[:, :c]
    rel = jnp.exp(jnp.where(causal, dec_i - dec_i.T, -jnp.inf))
    k_beta = k * beta
    a = jnp.where(strict, _mxu(k_beta, k, "nt") * rel, 0.0)
    inv = _unit_lower_inverse(a) if known_inv is None else _known_inverse(a, known_inv)
    e_dec = jnp.exp(dec)
    u = _hi(inv, v * beta)
    w = _hi(inv, k_beta * e_dec)
    attn = jnp.where(causal, _mxu(q, k, "nt") * rel, 0.0)
    total = jnp.sum(g, axis=0, keepdims=True)
    v_new = u - _mxu(w, s0, "nn")
    o = _mxu(q * e_dec, s0, "nn") + _mxu(attn, v_new, "nn")
    s1 = s0 * jnp.exp(total) + _mxu(k * jnp.exp(total - dec), v_new, "tn")
    return o, s1, inv


def _delta_rule_call(name, walk, n, in_specs, out_specs, out_shape, operands, exchange):
    n_in, n_out = len(in_specs), len(out_specs)
    carried = len(exchange.operands) if exchange else 0

    def body(*refs):
        ins, refs = refs[:n_in], refs[n_in:]
        x_refs, refs = refs[:carried], refs[carried:]
        outs, refs = refs[:n_out], refs[n_out:]
        land_refs, (state, *sems) = refs[:carried], refs[carried:]
        step = pl.program_id(0)
        if exchange:
            start, finish = exchange.bind(x_refs, land_refs, sems)
            pl.when(step == 0)(start)

        @pl.when(step == 0)
        def _():
            state[...] = jnp.zeros_like(state)

        walk(ins, outs, state)
        if exchange:
            pl.when(step == n - 1)(finish)

    return pl.pallas_call(
        body, name=name, grid=(n,), in_specs=list(in_specs) + [ANY] * carried, out_specs=list(out_specs) + [ANY] * carried,
        out_shape=list(out_shape) + (exchange.out_shapes if exchange else []),
        scratch_shapes=[pltpu.VMEM((GDN_HEAD_DIM, GDN_WIDTH), f32)] + (exchange.scratch if exchange else []),
        compiler_params=_params(1))(*operands, *(exchange.operands if exchange else []))


def _delta_heads():
    heads = [slice(hd * GDN_HEAD_DIM, (hd + 1) * GDN_HEAD_DIM) for hd in range(GDN_HEADS)]
    inv_at = [slice(hd * GDN_CHUNK, (hd + 1) * GDN_CHUNK) for hd in range(GDN_HEADS)]
    return heads, inv_at


def delta_rule_fwd(q, k, v, g, beta, name, exchange=None):
    s, width = q.shape
    c, dk = GDN_CHUNK, GDN_HEAD_DIM
    n = s // c
    heads, inv_at = _delta_heads()

    def walk(ins, outs, state):
        o_ref, s_in_ref, inv_ref = outs
        s_in_ref[...] = state[...]
        xs = [[r[:, hd] for r in (*ins, state)] for hd in heads]
        ys = [_delta_chunk(*x) for x in xs]
        for hd, at, (o, s1, inv) in zip(heads, inv_at, ys):
            o_ref[:, hd], state[:, hd], inv_ref[:, at] = o, s1, inv

    blk = pl.BlockSpec((c, width), lambda t: (t, 0))
    st = pl.BlockSpec((dk, width), lambda t: (t, 0))
    iv = pl.BlockSpec((c, GDN_HEADS * c), lambda t: (t, 0))
    return _delta_rule_call(name, walk, n, [blk] * 5, [blk, st, iv],
                            [_sds((s, width)), _sds((n * dk, width)), _sds((s, GDN_HEADS * c))], (q, k, v, g, beta), exchange)


def delta_rule_bwd(q, k, v, g, beta, s_in, inv, do, name, exchange=None):
    s, width = q.shape
    c, dk = GDN_CHUNK, GDN_HEAD_DIM
    n = s // c
    heads, inv_at = _delta_heads()

    def walk(ins, outs, dstate):
        q_ref, k_ref, v_ref, g_ref, b_ref, s_ref, inv_ref, do_ref = ins
        xs = [[r[:, hd] for r in (q_ref, k_ref, v_ref, g_ref, b_ref, s_ref)] for hd in heads]
        known = [inv_ref[:, at] for at in inv_at]
        cts = [(do_ref[:, hd], dstate[:, hd]) for hd in heads]
        grads = []
        for x, t, ct in zip(xs, known, cts):
            _, vjp = jax.vjp(lambda *y, t=t: _delta_chunk(*y, known_inv=t)[:2], *x)
            grads.append(vjp(ct))
        for hd, (*d_ins, ds0) in zip(heads, grads):
            for r, d in zip(outs, d_ins):
                r[:, hd] = d
            dstate[:, hd] = ds0

    blk = pl.BlockSpec((c, width), lambda t: (n - 1 - t, 0))
    st = pl.BlockSpec((dk, width), lambda t: (n - 1 - t, 0))
    iv = pl.BlockSpec((c, GDN_HEADS * c), lambda t: (n - 1 - t, 0))
    return _delta_rule_call(name, walk, n, [blk] * 5 + [st, iv, blk], [blk] * 5, [_sds((s, width))] * 5,
                            (q, k, v, g, beta, s_in, inv, do), exchange)


def adamw(w, g, m, v, name):
    shape = w.shape
    cols = shape[-1]
    rows = w.size // cols
    tile = _pick(rows, (512, 256, 128, 64, 32, 16, 8))
    spec = pl.BlockSpec((tile, cols), lambda i: (i, 0))

    def body(w_ref, g_ref, m_ref, v_ref, d_ref, nm_ref, nv_ref):
        grad = g_ref[...]
        nm = ADAM_B1 * m_ref[...] + (1.0 - ADAM_B1) * grad
        nv = ADAM_B2 * v_ref[...] + (1.0 - ADAM_B2) * (grad * grad)
        m_hat = nm / (1.0 - ADAM_B1 ** ADAM_STEP)
        v_hat = nv / (1.0 - ADAM_B2 ** ADAM_STEP)
        d_ref[...] = -ADAM_LR * (m_hat / (jnp.sqrt(v_hat) + ADAM_EPS) + ADAM_WD * w_ref[...])
        nm_ref[...] = nm
        nv_ref[...] = nv

    outs = pl.pallas_call(
        body, name=name, grid=(rows // tile,), in_specs=[spec] * 4, out_specs=[spec] * 3,
        out_shape=[_sds((rows, cols))] * 3, compiler_params=_params(1),
    )(*[t.reshape(rows, cols) for t in (w, g, m, v)])
    return tuple(t.reshape(shape) for t in outs)


def _place():
    return lax.axis_index("x"), lax.axis_index("y"), lax.axis_index("c")


def _flip(p, bits):
    return tuple(1 - v if (bits >> s) & 1 else v for v, s in zip(p, (2, 1, 0)))


def _slot(p):
    return 4 * p[0] + 2 * p[1] + p[2]


def _chip_of(p):
    return 2 * p[0] + p[1]


ANY = pl.BlockSpec(memory_space=pl.ANY)


class Gather:
    scratch = (pltpu.SemaphoreType.DMA((7,)), pltpu.SemaphoreType.DMA((7,)), pltpu.SemaphoreType.DMA)

    def __init__(self, shard):
        self.operand = shard
        self.out_shape = jax.ShapeDtypeStruct((N_DEV,) + shard.shape, shard.dtype)

    def bind(self, x_ref, out_ref, send_sems, recv_sems, local_sem):
        me = _place()
        sibling = _flip(me, 1)
        chips = [_flip(me, 4), _flip(me, 2), _flip(me, 6)]

        def copy(k, block, to, src=None):
            return pltpu.make_async_remote_copy(
                src_ref=out_ref.at[_slot(block)] if src is None else src, dst_ref=out_ref.at[_slot(block)],
                send_sem=send_sems.at[k], recv_sem=recv_sems.at[k], device_id=to, device_id_type=MESH)

        mine = pltpu.make_async_copy(x_ref, out_ref.at[_slot(me)], local_sem)
        first = [copy(0, me, sibling, src=x_ref)] + [copy(1 + j, me, chip, src=x_ref) for j, chip in enumerate(chips)]
        passed = [copy(4 + j, chip, sibling) for j, chip in enumerate(chips)]

        def start():
            mine.start()
            for cp in first:
                cp.start()

        def finish():
            for j, chip in enumerate(chips):
                copy(1 + j, chip, me).wait_recv()
                passed[j].start()
            copy(0, sibling, me).wait_recv()
            for j, chip in enumerate(chips):
                copy(4 + j, _flip(chip, 1), me).wait_recv()
            for cp in first + passed:
                cp.wait_send()
            mine.wait()

        return start, finish


class ChipExchange:
    scratch = (pltpu.SemaphoreType.DMA((3,)), pltpu.SemaphoreType.DMA((3,)), pltpu.SemaphoreType.DMA)

    def __init__(self, blocks):
        self.operand = blocks
        self.out_shape = jax.ShapeDtypeStruct(blocks.shape, blocks.dtype)

    def bind(self, x_ref, out_ref, send_sems, recv_sems, local_sem):
        me = _place()
        peers = [_flip(me, 4), _flip(me, 2), _flip(me, 6)]
        mine = pltpu.make_async_copy(x_ref.at[_chip_of(me)], out_ref.at[_chip_of(me)], local_sem)

        def copy(j, src_chip, dst_chip):
            return pltpu.make_async_remote_copy(
                src_ref=x_ref.at[src_chip], dst_ref=out_ref.at[dst_chip], send_sem=send_sems.at[j],
                recv_sem=recv_sems.at[j], device_id=peers[j], device_id_type=MESH)

        sends = [copy(j, _chip_of(peer), _chip_of(me)) for j, peer in enumerate(peers)]

        def start():
            mine.start()
            for cp in sends:
                cp.start()

        def finish():
            for j, peer in enumerate(peers):
                copy(j, _chip_of(me), _chip_of(peer)).wait_recv()
            for cp in sends:
                cp.wait_send()
            mine.wait()

        return start, finish


class Together:
    def __init__(self, *parts):
        self.parts = parts
        self.operands = [p.operand for p in parts]
        self.out_shapes = [p.out_shape for p in parts]
        self.scratch = [s for p in parts for s in p.scratch]

    def bind(self, x_refs, out_refs, sems):
        bound, at = [], 0
        for p, x_ref, out_ref in zip(self.parts, x_refs, out_refs):
            bound.append(p.bind(x_ref, out_ref, *sems[at:at + len(p.scratch)]))
            at += len(p.scratch)

        def start():
            for s, _ in bound:
                s()

        def finish():
            for _, f in bound:
                f()

        return start, finish


def exchange_alone(exchange, name):
    n = len(exchange.operands)

    def body(*refs):
        start, finish = exchange.bind(refs[:n], refs[n:2 * n], refs[2 * n:])
        start()
        finish()

    return pl.pallas_call(body, name=name, out_shape=exchange.out_shapes, in_specs=[ANY] * n, out_specs=[ANY] * n,
                          scratch_shapes=exchange.scratch)(*exchange.operands)


def _row_tile(rows):
    return max([t for t in range(16, min(rows, 1024) + 1, 16) if rows % t == 0] or [rows])


def pair_exchange(blocks, name):
    n = len(blocks)

    def body(*refs):
        x_refs, theirs_refs, (send_sems, recv_sems) = refs[:n], refs[n:2 * n], refs[2 * n:]
        me = _place()
        remote = [pltpu.make_async_remote_copy(
            src_ref=x_refs[t].at[2 * q + 1 - me[2]], dst_ref=theirs_refs[t].at[q], send_sem=send_sems.at[4 * t + q],
            recv_sem=recv_sems.at[4 * t + q], device_id=_flip(me, 1), device_id_type=MESH) for t in range(n) for q in range(4)]
        for cp in remote:
            cp.start()
        for cp in remote:
            cp.wait()

    return pl.pallas_call(
        body, name=name, out_shape=[jax.ShapeDtypeStruct((4,) + b.shape[1:], b.dtype) for b in blocks], in_specs=[ANY] * n,
        out_specs=[ANY] * n, scratch_shapes=[pltpu.SemaphoreType.DMA((4 * n,)), pltpu.SemaphoreType.DMA((4 * n,))])(*blocks)


def pair_add(blocks, theirs, name):
    n, rows, width = theirs.shape
    tile = _row_tile(rows)
    spec = pl.BlockSpec((None, tile, width), lambda q, i: (q, i, 0))
    south = pl.BlockSpec((None, None, tile, width), lambda q, i: (q, 0, i, 0))
    north = pl.BlockSpec((None, None, tile, width), lambda q, i: (q, 1, i, 0))

    def body(s_ref, n_ref, b_ref, o_ref):
        mine = jnp.where(lax.axis_index("c") == 0, s_ref[...], n_ref[...])
        o_ref[...] = (mine.astype(f32) + b_ref[...].astype(f32)).astype(o_ref.dtype)

    by_core = blocks.reshape(n, 2, rows, width)
    return pl.pallas_call(body, name=name, grid=(n, rows // tile), in_specs=[south, north, spec], out_specs=spec,
                          out_shape=jax.ShapeDtypeStruct(theirs.shape, theirs.dtype), compiler_params=_params(2))(by_core, by_core, theirs)


def sum_slots(blocks, name):
    n, rows, width = blocks.shape
    tile = _row_tile(rows)

    def body(x_ref, o_ref):
        total = x_ref[0].astype(f32)
        for s in range(1, n):
            total = total + x_ref[s].astype(f32)
        o_ref[...] = total

    return pl.pallas_call(
        body, name=name, grid=(rows // tile,), in_specs=[pl.BlockSpec((n, tile, width), lambda i: (0, i, 0))],
        out_specs=pl.BlockSpec((tile, width), lambda i: (i, 0)), out_shape=_sds((rows, width)), compiler_params=_params(1))(blocks)


def all_reduce_small(x, name):
    rows, width = x.shape

    def body(x_ref, o_ref, land, send_sems, recv_sems):
        me = _place()
        copies = []
        for k in range(1, N_DEV):
            peer = _flip(me, k)
            copies.append(pltpu.make_async_remote_copy(
                src_ref=x_ref, dst_ref=land.at[_slot(me)], send_sem=send_sems.at[k - 1], recv_sem=recv_sems.at[k - 1],
                device_id=peer, device_id_type=MESH))
        for cp in copies:
            cp.start()
        land[_slot(me)] = x_ref[...]
        for k in range(1, N_DEV):
            peer = _flip(me, k)
            pltpu.make_async_remote_copy(
                src_ref=x_ref, dst_ref=land.at[_slot(peer)], send_sem=send_sems.at[k - 1], recv_sem=recv_sems.at[k - 1],
                device_id=peer, device_id_type=MESH).wait_recv()
        total = land[0]
        for s in range(1, N_DEV):
            total = total + land[s]
        o_ref[...] = total
        for cp in copies:
            cp.wait_send()

    return pl.pallas_call(
        body, name=name, out_shape=_sds((rows, width)), in_specs=[pl.BlockSpec(memory_space=pltpu.VMEM)],
        out_specs=pl.BlockSpec(memory_space=pltpu.VMEM),
        scratch_shapes=[pltpu.VMEM((N_DEV, rows, width), f32), pltpu.SemaphoreType.DMA((7,)), pltpu.SemaphoreType.DMA((7,))],
    )(x)


def _pack_big(shards):
    packed = {name: shards[name].astype(bf16) for name in COL_SHARDED}
    packed["rows"] = jnp.concatenate([shards[name].astype(bf16) for name, _ in ROW_SHARDED], axis=1)
    return packed


def _unpack_gathered(gathered):
    full = {}
    for name, part in gathered.items():
        if name in COL_SHARDED:
            full[name] = part.transpose(1, 0, 2).reshape(D_MODEL, N_DEV * part.shape[2])
        else:
            at = 0
            for weight, rows in ROW_SHARDED:
                full[weight] = part[:, at:at + rows, :].reshape(N_DEV * rows, D_MODEL)
                at += rows
    if "w_gate_up" in full:
        full["w_gate_up"] = _interleave_gate_up(full["w_gate_up"])
    if "w_in" in full:
        w_in = full.pop("w_in")
        full["w_main"] = jnp.concatenate([w_in[:, :AB_AT], w_in[:, AB_AT + 2 * GDN_HEADS:]], axis=1)
        full["w_ab"] = jnp.pad(w_in[:, AB_AT:AB_AT + 2 * GDN_HEADS], ((0, 0), (0, LANES - 2 * GDN_HEADS)))
    return full


def _pack_grads(grads, group):
    packed = {}
    for name in group:
        if name == "w_in":
            main, ab = grads["w_main"], grads["w_ab"]
            g = jnp.concatenate([main[:, :AB_AT], ab[:, :2 * GDN_HEADS], main[:, AB_AT:]], axis=1)
        elif name == "w_gate_up":
            g = _interleave_gate_up(grads[name], undo=True)
        elif name == "rows":
            packed[name] = jnp.concatenate([grads[weight].reshape(N_DEV, rows, D_MODEL) for weight, rows in ROW_SHARDED], axis=1)
            continue
        else:
            g = grads[name]
        packed[name] = g.reshape(D_MODEL, N_DEV, g.shape[1] // N_DEV).transpose(1, 0, 2)
    return packed


def _unpack_shard(layers):
    out = {name: jnp.stack([layer[name] for layer in layers]) for name in COL_SHARDED}
    rows_pack, at = jnp.stack([layer["rows"] for layer in layers]), 0
    for weight, rows in ROW_SHARDED:
        out[weight] = rows_pack[:, at:at + rows, :]
        at += rows
    return out


def _rows_of(flat_len):
    return -(-flat_len // (8 * D_MODEL)) * 8


def _pack_small(parts):
    flat = jnp.concatenate([p.reshape(-1) for p in parts])
    rows = _rows_of(flat.shape[0])
    flat = jnp.pad(flat, (0, rows * D_MODEL - flat.shape[0]))
    return flat.reshape(rows, D_MODEL)


def _unpack_small(packed, like):
    flat, out, at = packed.reshape(-1), [], 0
    for p in like:
        out.append(flat[at:at + p.size].reshape(p.shape))
        at += p.size
    return out


def _rope_tables(positions):
    inv_freq = jnp.float32(ROPE_THETA) ** (-jnp.arange(0, ROPE_DIM, 2, dtype=f32) / ROPE_DIM)
    ang = positions.astype(f32)[:, None] * inv_freq
    cos, sin = jnp.cos(ang), jnp.sin(ang)
    rest = ATTN_HEAD_DIM - ROPE_DIM
    cos_h = jnp.concatenate([cos, cos, jnp.ones((cos.shape[0], rest), f32)], axis=1)
    sin_h = jnp.concatenate([-sin, sin, jnp.zeros((sin.shape[0], rest), f32)], axis=1)
    return jnp.tile(cos_h, (1, ATTN_HEADS)), jnp.tile(sin_h, (1, ATTN_HEADS))


HEAD_SMALL = ("norm_mix_pre", "conv_short", "conv_gdn", "gdn_a_log", "gdn_dt_bias")


def _layer_head(h, p, cos_t, sin_t):
    hn = rms_norm(h, p["norm_mix_pre"][None], "norm_mix_pre")
    proj = _linear(hn, p["w_main"], "w_main")
    ab = _linear(hn, p["w_ab"], "w_ab")
    aw, cw, gw = ATTN_WIDTH, CONV_WIDTH, GDN_WIDTH
    aq, ak, av, cb, cc, cx, gqkv, gate = _split_cols(proj, (aw, aw, aw, cw, cw, cw, 3 * gw, gw))
    y_attn = dilated_attention(rope(aq, cos_t, sin_t, ATTN_HEAD_DIM ** -0.5, "rope_q"), rope(ak, cos_t, sin_t, 1.0, "rope_k"),
                               av, "attn")
    y_conv = short_conv(cb, cc, cx, p["conv_short"], "short_conv")
    qkv = gdn_pre(gqkv, p["conv_gdn"], "gdn_pre")
    pv = jnp.zeros((8, LANES), f32).at[0, :GDN_HEADS].set(p["gdn_a_log"]).at[1, :GDN_HEADS].set(p["gdn_dt_bias"])
    g, beta = gate_beta(ab, pv, "gate_beta")
    return (*_split_cols(qkv, (gw, gw, gw)), g, beta), (gate, y_attn, y_conv)


def _layer_tail(h, o, gate, y_attn, y_conv, p, mem):
    y_gdn = gdn_post(o, gate, p["gdn_norm"][None], "gdn_post")
    mix = _linear(jnp.concatenate([y_attn, y_conv, y_gdn], axis=1), p["w_out"], "w_out")
    h = add_norm(h, mix, p["norm_mix_post"][None], "norm_mix_post")

    hn = rms_norm(h, p["norm_xattn_pre"][None], "norm_xattn_pre")
    qx = _linear(hn, p["w_xq"], "w_xq")
    kv = _linear(rms_norm(mem, p["norm_mem"][None], "norm_mem"), p["w_xkv"], "w_xkv")
    xa = _linear(cross_attention(qx, kv, "xattn"), p["w_xo"], "w_xo")
    h = add_norm(h, xa, p["norm_xattn_post"][None], "norm_xattn_post")

    hn = rms_norm(h, p["norm_ffn_pre"][None], "norm_ffn_pre")
    return add_norm(h, swiglu_ffn(hn, p["w_gate_up"], p["w_down"], "ffn"), p["norm_ffn_post"][None], "norm_ffn_post")


def _pair_summed(grads, group, name):
    blocks = _pack_grads(grads, group)
    theirs = pair_exchange([blocks[n] for n in group], name + "_pair_exchange")
    return [pair_add(blocks[n], t, f"{name}_pair_add_{n}") for n, t in zip(group, theirs)]


def _forward_backward(x, packed, small, mem, cos_t, sin_t, target):
    def gathers(group, layer):
        return [Gather(packed[n][layer]) for n in group]

    h = x
    head_gathered = exchange_alone(Together(*gathers(HEAD_GROUP, 0)), "gather_first")
    saved = []
    for layer in range(DEPTH):
        at_layer = {n: t[layer] for n, t in small.items()}
        head_p = {**_unpack_gathered(dict(zip(HEAD_GROUP, head_gathered))), **{n: at_layer[n] for n in HEAD_SMALL}}
        (rule_in, rest), head_vjp = jax.vjp(lambda h, hp: _layer_head(h, hp, cos_t, sin_t), h, head_p)
        carried = gathers(TAIL_GROUP, layer) + (gathers(HEAD_GROUP, layer + 1) if layer + 1 < DEPTH else [])
        results = delta_rule_fwd(*rule_in, "delta_rule_fwd", Together(*carried))
        (o, s_in, inv), landed = results[:3], results[3:]
        head_gathered = landed[len(TAIL_GROUP):]
        tail_p = {**_unpack_gathered(dict(zip(TAIL_GROUP, landed))), **{n: t for n, t in at_layer.items() if n not in HEAD_SMALL}}
        h, tail_vjp = jax.vjp(lambda h, o, rest, tp: _layer_tail(h, o, *rest, tp, mem), h, o, rest, tail_p)
        saved.append((head_vjp, tail_vjp, rule_in, s_in, inv))

    loss, dh = jax.value_and_grad(lambda y: loss_rows(y, target, "loss"))(h)

    def summed(group, landed):
        return {n: sum_slots(t, "sum_grads_" + n) for n, t in zip(group, landed)}

    big_grads, small_grads, head_pending = [{} for _ in range(DEPTH)], [None] * DEPTH, []
    for layer in reversed(range(DEPTH)):
        head_vjp, tail_vjp, rule_in, s_in, inv = saved[layer]
        dh_tail, do, d_rest, d_tail_p = tail_vjp(dh)
        carried = [ChipExchange(t) for t in _pair_summed(d_tail_p, TAIL_GROUP, "tail") + head_pending]
        results = delta_rule_bwd(*rule_in, s_in, inv, do, "delta_rule_bwd", Together(*carried))
        d_rule_in, landed = results[:5], results[5:]
        big_grads[layer].update(summed(TAIL_GROUP, landed))
        if head_pending:
            big_grads[layer + 1].update(summed(HEAD_GROUP, landed[len(TAIL_GROUP):]))
        dh_head, d_head_p = head_vjp((tuple(d_rule_in), d_rest))
        dh = dh_tail + dh_head
        small_grads[layer] = {n: t for n, t in {**d_head_p, **d_tail_p}.items() if n in small}
        head_pending = _pair_summed(d_head_p, HEAD_GROUP, "head")
    landed = exchange_alone(Together(*[ChipExchange(t) for t in head_pending]), "exchange_last")
    big_grads[0].update(summed(HEAD_GROUP, landed))
    return loss, dh, big_grads, small_grads


def kernel(x, mem, positions, norm_mix_pre, norm_mix_post, w_in, conv_short, conv_gdn, gdn_a_log, gdn_dt_bias, gdn_norm, w_out, norm_mem, norm_xattn_pre, norm_xattn_post, w_xq, w_xkv, w_xo, norm_ffn_pre, norm_ffn_post, w_gate_up, w_down, loss_target, m_norm_mix_pre, m_norm_mix_post, m_w_in, m_conv_short, m_conv_gdn, m_gdn_a_log, m_gdn_dt_bias, m_gdn_norm, m_w_out, m_norm_mem, m_norm_xattn_pre, m_norm_xattn_post, m_w_xq, m_w_xkv, m_w_xo, m_norm_ffn_pre, m_norm_ffn_post, m_w_gate_up, m_w_down, v_norm_mix_pre, v_norm_mix_post, v_w_in, v_conv_short, v_conv_gdn, v_gdn_a_log, v_gdn_dt_bias, v_gdn_norm, v_w_out, v_norm_mem, v_norm_xattn_pre, v_norm_xattn_post, v_w_xq, v_w_xkv, v_w_xo, v_norm_ffn_pre, v_norm_ffn_post, v_w_gate_up, v_w_down):
    given = dict(locals())
    weights = {n: given[n] for n in WEIGHTS}
    me = _slot(_place())

    def in_place(shard):
        full = jnp.zeros(shard.shape[:-1] + (shard.shape[-1] * N_DEV,), f32)
        return lax.dynamic_update_slice_in_dim(full, shard, me * shard.shape[-1], axis=shard.ndim - 1)

    placed = [in_place(conv_short), in_place(conv_gdn)]
    conv_short_full, conv_gdn_full = _unpack_small(all_reduce_small(_pack_small(placed), "gather_conv"), placed)
    small = {n: weights[n] for n in NORMS + ("gdn_a_log", "gdn_dt_bias", "gdn_norm")}
    small["conv_short"], small["conv_gdn"] = conv_short_full, conv_gdn_full

    cos_t, sin_t = _rope_tables(positions[0])
    loss, grad_x, big_layers, small_layers = _forward_backward(
        x[0], _pack_big(weights), small, mem[0], cos_t, sin_t, loss_target[0])
    grads = _unpack_shard(big_layers)

    names = sorted(small)
    parts = [jnp.stack([layer[n] for layer in small_layers]) for n in names] + [loss.reshape(1)]
    reduced = _unpack_small(all_reduce_small(_pack_small(parts), "reduce_small"), parts)
    loss = reduced[-1][0]
    for n, g in zip(names, reduced[:-1]):
        if n in ("conv_short", "conv_gdn"):
            width = weights[n].shape[-1]
            g = lax.dynamic_slice_in_dim(g, me * width, width, axis=g.ndim - 1)
        grads[n] = g

    delta, new_m, new_v = {}, {}, {}
    for n in WEIGHTS:
        delta[n], new_m[n], new_v[n] = adamw(weights[n], grads[n], given["m_" + n], given["v_" + n], "adamw_" + n)
    return (loss, grad_x[None], *[grads[n] for n in WEIGHTS], *[delta[n] for n in WEIGHTS],
            *[new_m[n] for n in WEIGHTS], *[new_v[n] for n in WEIGHTS])
```

```python
import functools

import jax
import jax.numpy as jnp
from jax import lax
from jax.experimental import pallas as pl
from jax.experimental.pallas import tpu as pltpu

f32 = jnp.float32
bf16 = jnp.bfloat16
HIGHEST = lax.Precision.HIGHEST
MESH = pl.DeviceIdType.MESH

N_DEV = 8
DEPTH = 4
D_MODEL = 1024
EPS = 1e-6
ATTN_HEADS, ATTN_HEAD_DIM = 4, 64
ATTN_WIDTH = ATTN_HEADS * ATTN_HEAD_DIM
DILATIONS = (1, 4, 16)
QB = 128
ROPE_THETA = 500000.0
ROPE_DIM = ATTN_HEAD_DIM // 4
CONV_WIDTH = 256
GDN_HEADS, GDN_HEAD_DIM = 4, 128
GDN_WIDTH = GDN_HEADS * GDN_HEAD_DIM
GDN_CHUNK = 64
XATTN_HEADS, XATTN_HEAD_DIM = 4, 256
FFN_HIDDEN = 2816
IN_WIDTH = 3592
AB_AT = 3 * ATTN_WIDTH + 3 * CONV_WIDTH + 3 * GDN_WIDTH
MAIN_WIDTH = IN_WIDTH - 2 * GDN_HEADS
LANES = 128
ROW_TILE = 256
VMEM_LIMIT = 56 * 1024 * 1024

ADAM_LR, ADAM_B1, ADAM_B2, ADAM_EPS, ADAM_WD, ADAM_STEP = 0.001, 0.9, 0.999, 1e-08, 0.01, 10

COL_SHARDED = ("w_in", "w_xkv", "w_gate_up")
ROW_SHARDED = (("w_out", 128), ("w_xq", 128), ("w_xo", 128), ("w_down", 352))
HEAD_GROUP = ("w_in",)
TAIL_GROUP = ("w_gate_up", "w_xkv", "rows")
NORMS = ("norm_mix_pre", "norm_mix_post", "norm_mem", "norm_xattn_pre", "norm_xattn_post", "norm_ffn_pre", "norm_ffn_post")
WEIGHTS = ("norm_mix_pre", "norm_mix_post", "w_in", "conv_short", "conv_gdn", "gdn_a_log", "gdn_dt_bias", "gdn_norm", "w_out",
           "norm_mem", "norm_xattn_pre", "norm_xattn_post", "w_xq", "w_xkv", "w_xo", "norm_ffn_pre", "norm_ffn_post",
           "w_gate_up", "w_down")


def _params(n_grid):
    return pltpu.CompilerParams(dimension_semantics=("arbitrary",) * n_grid, vmem_limit_bytes=VMEM_LIMIT)


def _pick(n, cands):
    for c in cands:
        if n % c == 0:
            return c
    return n


MXU_FLOPS = 9.0e14
HBM_BYTES_PER_S = 2.5e12
VMEM_RMW_BYTES_PER_S = 7.0e12
STEP_S = 0.4e-6
MATMUL_VMEM = 44 * 1024 * 1024


def _tiles(m, n, k, sa, sb, so, tn=None):
    def divisors(d):
        return sorted({d // s for s in range(1, d // LANES + 1) if d % s == 0 and (d // s) % LANES == 0}, reverse=True)

    best = None
    for tk in divisors(k):
        nk = k // tk
        for tm in divisors(m):
            for tn_ in [tn] if tn else divisors(n):
                per_step = tm * tk * sa + tk * tn_ * sb + tm * tn_ * so
                vmem = 2 * per_step + (tm * tn_ * 4 if nk > 1 else 0)
                vmem += (tm * tk * 2 if sa == 4 else 0) + (tk * tn_ * 2 if sb == 4 else 0) + tm * tn_ * 4
                if vmem > MATMUL_VMEM:
                    continue
                moved = m * k * sa * (1 if nk == 1 else n // tn_) + k * n * sb * (1 if nk == 1 and n == tn_ else m // tm) + m * n * so
                busy = 2 * m * n * k / MXU_FLOPS + (m * n * 8 * nk / VMEM_RMW_BYTES_PER_S if nk > 1 else 0)
                cost = max(moved / HBM_BYTES_PER_S, busy) + per_step / HBM_BYTES_PER_S + (m // tm) * (n // tn_) * nk * STEP_S
                if best is None or cost < best[0]:
                    best = (cost, tm, tn_, tk)
    return best[1:]


def _mm(a, b, ta, tb, out_dtype, name, finish=None):
    m, k = (a.shape[1], a.shape[0]) if ta else a.shape
    n = b.shape[0] if tb else b.shape[1]
    tm, tn, tk = _tiles(m, n, k, a.dtype.itemsize, b.dtype.itemsize, jnp.dtype(out_dtype).itemsize, finish and finish[0])
    nk = k // tk
    if finish:
        assert nk == 1
        _, extra, results, function = finish
        dims = (((0 if ta else 1,), (1 if tb else 0,)), ((), ()))

        def finish_body(a_ref, b_ref, *refs):
            p = lax.dot_general(a_ref[...].astype(bf16), b_ref[...].astype(bf16), dims, preferred_element_type=f32)
            outs = function(p, *[r[...] for r in refs[:len(extra)]])
            for r, o in zip(refs[len(extra):], outs):
                r[...] = o.astype(r.dtype)

        return pl.pallas_call(
            finish_body, name=name, grid=(m // tm, n // tn),
            in_specs=[pl.BlockSpec((tk, tm), lambda i, j: (0, i)) if ta else pl.BlockSpec((tm, tk), lambda i, j: (i, 0)),
                      pl.BlockSpec((tn, tk), lambda i, j: (j, 0)) if tb else pl.BlockSpec((tk, tn), lambda i, j: (0, j))]
            + [pl.BlockSpec((tm, cols), lambda i, j: (i, j)) for _, cols in extra],
            out_specs=[pl.BlockSpec((tm, cols), lambda i, j: (i, j)) for _, cols in results],
            out_shape=[jax.ShapeDtypeStruct((m, n // tn * cols), dt) for dt, cols in results],
            compiler_params=_params(2))(a, b, *[x for x, _ in extra])
    a_spec = pl.BlockSpec((tk, tm), lambda i, j, kk: (kk, i)) if ta else pl.BlockSpec((tm, tk), lambda i, j, kk: (i, kk))
    b_spec = pl.BlockSpec((tn, tk), lambda i, j, kk: (j, kk)) if tb else pl.BlockSpec((tk, tn), lambda i, j, kk: (kk, j))
    dims = (((0 if ta else 1,), (1 if tb else 0,)), ((), ()))

    def body(a_ref, b_ref, o_ref, *acc):
        kk = pl.program_id(2)
        p = lax.dot_general(a_ref[...].astype(bf16), b_ref[...].astype(bf16), dims, preferred_element_type=f32)
        if nk == 1:
            o_ref[...] = p.astype(o_ref.dtype)
            return
        acc_ref, = acc

        @pl.when(kk == 0)
        def _():
            acc_ref[...] = p

        @pl.when(kk > 0)
        def _():
            acc_ref[...] += p

        @pl.when(kk == nk - 1)
        def _():
            o_ref[...] = acc_ref[...].astype(o_ref.dtype)

    return pl.pallas_call(
        body, name=name, grid=(m // tm, n // tn, nk), in_specs=[a_spec, b_spec],
        out_specs=pl.BlockSpec((tm, tn), lambda i, j, kk: (i, j)), out_shape=jax.ShapeDtypeStruct((m, n), out_dtype),
        scratch_shapes=[pltpu.VMEM((tm, tn), f32)] if nk > 1 else [], compiler_params=_params(3))(a, b)


def _linear(x, w, name):
    @jax.custom_vjp
    def lin(x, w):
        return _mm(x, w, False, False, f32, name + "_y")

    def lin_f(x, w):
        return _mm(x, w, False, False, f32, name + "_y"), (x, w)

    def lin_b(res, dy):
        x, w = res
        return _mm(dy, w, False, True, f32, name + "_dx"), _mm(x, dy, True, False, bf16, name + "_dw")

    lin.defvjp(lin_f, lin_b)
    return lin(x, w)


GATE_UP_TILE = 512


def _interleave_gate_up(w, undo=False):
    two_f = w.shape[1]
    half = GATE_UP_TILE // 2
    nb = two_f // GATE_UP_TILE
    if undo:
        order = [2 * j + side for side in range(2) for j in range(nb)]
    else:
        order = [side * nb + j for j in range(nb) for side in range(2)]
    return jnp.concatenate([w[:, b * half:(b + 1) * half] for b in order], axis=1)


def swiglu_ffn(hn, w_gate_up, w_down, name):
    half = GATE_UP_TILE // 2

    def act_of(p):
        return p, jax.nn.silu(p[:, :half]) * p[:, half:]

    def d_gate_up_of(d_act, gate_up):
        g, u = gate_up[:, :half].astype(f32), gate_up[:, half:].astype(f32)
        sig = jax.nn.sigmoid(g)
        return (jnp.concatenate([d_act * u * sig * (1.0 + g * (1.0 - sig)), d_act * g * sig], axis=1),)

    def forward(hn, w_gate_up, w_down):
        gate_up, act = _mm(hn, w_gate_up, False, False, bf16, name + "_act",
                           (GATE_UP_TILE, [], [(bf16, GATE_UP_TILE), (bf16, half)], act_of))
        return _mm(act, w_down, False, False, f32, name + "_y"), (hn, w_gate_up, w_down, gate_up, act)

    def backward(res, dy):
        hn, w_gate_up, w_down, gate_up, act = res
        d_gate_up, = _mm(dy, w_down, False, True, bf16, name + "_dact",
                         (half, [(gate_up, GATE_UP_TILE)], [(bf16, GATE_UP_TILE)], d_gate_up_of))
        return (_mm(d_gate_up, w_gate_up, False, True, f32, name + "_dx"), _mm(hn, d_gate_up, True, False, bf16, name + "_dw1"),
                _mm(act, dy, True, False, bf16, name + "_dw2"))

    @jax.custom_vjp
    def op(hn, w_gate_up, w_down):
        return forward(hn, w_gate_up, w_down)[0]

    op.defvjp(forward, backward)
    return op(hn, w_gate_up, w_down)


def _split_cols(x, widths):
    edges = [sum(widths[:i]) for i in range(len(widths) + 1)]

    def cut(x):
        return tuple(x[:, a:b] for a, b in zip(edges[:-1], edges[1:]))

    @jax.custom_vjp
    def split(x):
        return cut(x)

    split.defvjp(lambda x: (cut(x), None), lambda _, cts: (jnp.concatenate(cts, axis=1),))
    return split(x)


def _block_op(name, f, grid, in_specs, out_defs, arrays, diff, acc=None, gdefs=None):
    acc, gdefs = acc or {}, gdefs or {}
    n_in, n_out, n_grid = len(in_specs), len(out_defs), len(grid)

    def fwd_call(*xs):
        def body(*refs):
            outs = f(*[r[...] for r in refs[:n_in]])
            for r, o in zip(refs[n_in:], outs):
                r[...] = o.astype(r.dtype)

        return pl.pallas_call(
            body, name=name + "_fwd", grid=grid, in_specs=in_specs, out_specs=[d[1] for d in out_defs],
            out_shape=[d[0] for d in out_defs], compiler_params=_params(n_grid))(*xs)

    def bwd_call(*xs_and_cts):
        def body(*refs):
            xs = [r[...] for r in refs[:n_in]]
            cts = tuple(r[...] for r in refs[n_in:n_in + n_out])

            def of_diff(*dx):
                full = list(xs)
                for i, v in zip(diff, dx):
                    full[i] = v
                return tuple(f(*full))

            _, vjp = jax.vjp(of_diff, *[xs[i] for i in diff])
            grads = vjp(cts)
            for i, g, r in zip(diff, grads, refs[n_in + n_out:]):
                if i in acc:
                    first = functools.reduce(jnp.logical_and, [pl.program_id(a) == 0 for a in acc[i]])

                    @pl.when(first)
                    def _(r=r):
                        r[...] = jnp.zeros_like(r)

                    r[...] += g.astype(r.dtype)
                else:
                    r[...] = g.astype(r.dtype)

        g_defs = [gdefs.get(i, (jax.ShapeDtypeStruct(arrays[i].shape, f32), in_specs[i])) for i in diff]
        return pl.pallas_call(
            body, name=name + "_bwd", grid=grid, in_specs=list(in_specs) + [d[1] for d in out_defs],
            out_specs=[d[1] for d in g_defs], out_shape=[d[0] for d in g_defs], compiler_params=_params(n_grid))(*xs_and_cts)

    return fwd_call, bwd_call


def _simple_op(name, f, grid, in_specs, out_defs, arrays, diff, acc=None):
    fwd_call, bwd_call = _block_op(name, f, grid, in_specs, out_defs, arrays, diff, acc)

    @jax.custom_vjp
    def op(*xs):
        return tuple(fwd_call(*xs))

    def op_f(*xs):
        return tuple(fwd_call(*xs)), xs

    def op_b(xs, cts):
        grads = bwd_call(*xs, *cts)
        out = [jnp.zeros_like(x) for x in xs]
        for i, g in zip(diff, grads):
            out[i] = g
        return tuple(out)

    op.defvjp(op_f, op_b)
    return op(*arrays)


def _rows(width, tile=ROW_TILE):
    return pl.BlockSpec((tile, width), lambda i: (i, 0))


def _whole(shape):
    return pl.BlockSpec(shape, lambda *_: (0,) * len(shape))


def _sds(shape):
    return jax.ShapeDtypeStruct(shape, f32)


def _rms(x, w):
    return x * lax.rsqrt(jnp.mean(x * x, axis=-1, keepdims=True) + EPS) * w


def rms_norm(x, w, name):
    r, d = x.shape
    return _simple_op(name, lambda x, w: (_rms(x, w),), (r // ROW_TILE,), [_rows(d), _whole((1, d))],
                      [(_sds((r, d)), _rows(d))], (x, w), (0, 1), {1: (0,)})[0]


def add_norm(h, y, w, name):
    r, d = h.shape
    return _simple_op(name, lambda h, y, w: (h + _rms(y, w),), (r // ROW_TILE,), [_rows(d), _rows(d), _whole((1, d))],
                      [(_sds((r, d)), _rows(d))], (h, y, w), (0, 1, 2), {2: (0,)})[0]


def add_norm_then_norm(h, y, w_post, w_pre, name):
    r, d = h.shape

    def f(h, y, w_post, w_pre):
        h_new = h + _rms(y, w_post)
        return h_new, _rms(h_new, w_pre)

    return _simple_op(name, f, (r // ROW_TILE,), [_rows(d), _rows(d), _whole((1, d)), _whole((1, d))],
                      [(_sds((r, d)), _rows(d))] * 2, (h, y, w_post, w_pre), (0, 1, 2, 3), {2: (0,), 3: (0,)})


def _swap8(x):
    def raw(x):
        lane = lax.broadcasted_iota(jnp.int32, x.shape, 1) % ATTN_HEAD_DIM
        half = ROPE_DIM // 2
        up = pltpu.roll(x, x.shape[1] - half, axis=1)
        down = pltpu.roll(x, half, axis=1)
        return jnp.where(lane < half, up, jnp.where(lane < ROPE_DIM, down, 0.0))

    @jax.custom_vjp
    def swap(x):
        return raw(x)

    swap.defvjp(lambda x: (raw(x), None), lambda _, g: (raw(g),))
    return swap(x)


def rope(x, cos_t, sin_t, scale, name):
    r, d = x.shape
    return _simple_op(name, lambda x, c, s: ((x * c + _swap8(x) * s) * scale,), (r // ROW_TILE,), [_rows(d)] * 3,
                      [(_sds((r, d)), _rows(d))], (x, cos_t, sin_t), (0,))[0]


def _shift_rows(x, k):
    n = x.shape[0]

    def down(x):
        row = lax.broadcasted_iota(jnp.int32, x.shape, 0)
        return jnp.where(row >= k, pltpu.roll(x, k, axis=0), 0.0)

    def up(x):
        row = lax.broadcasted_iota(jnp.int32, x.shape, 0)
        return jnp.where(row < n - k, pltpu.roll(x, n - k, axis=0), 0.0)

    @jax.custom_vjp
    def shift(x):
        return down(x)

    shift.defvjp(lambda x: (down(x), None), lambda _, g: (up(g),))
    return shift(x)


def _causal_conv(x, w):
    taps = w.shape[0]
    y = x * w[taps - 1:taps, :]
    for j in range(taps - 1):
        y = y + _shift_rows(x, taps - 1 - j) * w[j:j + 1, :]
    return y


def _cols(rows, at=0):
    return pl.BlockSpec((rows, LANES), lambda j: (0, at + j))


def short_conv(cb, cc, cx, w, name):
    s, c = cb.shape
    taps = w.shape[0]
    return _simple_op(name, lambda b, c_, x, w: (b * _causal_conv(c_ * x, w),), (c // LANES,),
                      [_cols(s)] * 3 + [_cols(taps)], [(_sds((s, c)), _cols(s))], (cb, cc, cx, w), (0, 1, 2, 3))[0]


def gdn_pre(qkv, w, name):
    s, c = qkv.shape
    taps = w.shape[0]

    def f(x, w):
        j = pl.program_id(0)
        y = jax.nn.silu(_causal_conv(x, w))
        normed = y * lax.rsqrt(jnp.sum(y * y, axis=-1, keepdims=True) + EPS)
        scale = jnp.where(j < GDN_HEADS, GDN_HEAD_DIM ** -0.5, 1.0).astype(f32)
        return (jnp.where(j < 2 * GDN_HEADS, normed * scale, y),)

    return _simple_op(name, f, (c // LANES,), [_cols(s), _cols(taps)], [(_sds((s, c)), _cols(s))], (qkv, w), (0, 1))[0]


def gate_beta(ab, pv, name):
    s = ab.shape[0]

    def f(ab, pv):
        lane = lax.broadcasted_iota(jnp.int32, ab.shape, 1)
        g = -jnp.exp(pv[0:1, :]) * jax.nn.softplus(ab + pv[1:2, :])
        return (jnp.where(lane < GDN_HEADS, g, jnp.where(lane < 2 * GDN_HEADS, jax.nn.sigmoid(ab), 0.0)),)

    return _simple_op(name, f, (s // ROW_TILE,), [_rows(LANES), _whole((8, LANES))], [(_sds((s, LANES)), _rows(LANES))],
                      (ab, pv), (0, 1), {1: (0,)})[0]


def gdn_post(o, gate, w, name):
    s, c = o.shape
    spec = pl.BlockSpec((ROW_TILE, LANES), lambda i, j: (i, j))
    return _simple_op(name, lambda o, g, w: (_rms(o, w) * jax.nn.silu(g),), (s // ROW_TILE, c // LANES),
                      [spec, spec, _whole((1, LANES))], [(_sds((s, c)), spec)], (o, gate, w), (0, 1, 2), {2: (0, 1)})[0]


def attn_merge(outs, lses, name):
    s, c = outs[0].shape

    def f(o1, o2, o3, l1, l2, l3):
        m = lax.stop_gradient(jnp.maximum(jnp.maximum(l1, l2), l3))
        e1, e2, e3 = jnp.exp(l1 - m), jnp.exp(l2 - m), jnp.exp(l3 - m)
        return ((e1 * o1 + e2 * o2 + e3 * o3) / (e1 + e2 + e3),)

    return _simple_op(name, f, (s // ROW_TILE,), [_rows(c)] * 6, [(_sds((s, c)), _rows(c))], (*outs, *lses), tuple(range(6)))[0]


def loss_rows(y, target, name):
    s, d = y.shape
    nt = s // ROW_TILE

    def f(y, t):
        e = y - t
        part = 0.5 * jnp.sum(jnp.mean(e * e, axis=-1, keepdims=True), axis=0, keepdims=True)
        return (jnp.broadcast_to(part * (1.0 / (8 * LANES)), (8, LANES)),)

    out = _simple_op(name, f, (nt,), [_rows(d)] * 2, [(_sds((nt * 8, LANES)), pl.BlockSpec((8, LANES), lambda i: (i, 0)))],
                     (y, target), (0,))[0]
    return jnp.sum(out)


def _mxu(a, b, form):
    dims = {"nn": ((1,), (0,)), "nt": ((1,), (1,)), "tn": ((0,), (0,))}

    def raw(a, b, form):
        return lax.dot_general(a.astype(bf16), b.astype(bf16), (dims[form], ((), ())), preferred_element_type=f32)

    @jax.custom_vjp
    def prod(a, b):
        return raw(a, b, form)

    def prod_b(res, ct):
        a, b = res
        if form == "nn":
            return raw(ct, b, "nt"), raw(a, ct, "tn")
        if form == "nt":
            return raw(ct, b, "nn"), raw(ct, a, "tn")
        return raw(b, ct, "nt"), raw(a, ct, "nn")

    prod.defvjp(lambda a, b: (raw(a, b, form), (a, b)), prod_b)
    return prod(a, b)


def band_attention(q, k, v, nb, name):
    r, qb, width = q.shape
    dh = ATTN_HEAD_DIM

    def f(q, kp, kc, vp, vc):
        has_prev = (pl.program_id(0) % nb) > 0
        i = lax.broadcasted_iota(jnp.int32, (qb, 2 * qb), 0)
        j = lax.broadcasted_iota(jnp.int32, (qb, 2 * qb), 1)
        seen = jnp.logical_or(jnp.logical_and(jnp.logical_and(j < qb, j >= i), has_prev), jnp.logical_and(j >= qb, j - qb <= i))
        keys, values = jnp.concatenate([kp, kc], axis=0), jnp.concatenate([vp, vc], axis=0)
        outs, lses = [], []
        for hd in range(width // dh):
            at = slice(hd * dh, (hd + 1) * dh)
            sc = jnp.where(seen, _mxu(q[:, at], keys[:, at], "nt"), -jnp.inf)
            m = lax.stop_gradient(jnp.max(sc, axis=-1, keepdims=True))
            p = jnp.exp(sc - m)
            l = jnp.sum(p, axis=-1, keepdims=True)
            outs.append(_mxu(p / l, values[:, at], "nn"))
            lses.append(jnp.broadcast_to(m + jnp.log(l), (qb, dh)))
        return jnp.concatenate(outs, axis=1), jnp.concatenate(lses, axis=1)

    blk = (None, qb, width)
    cur = pl.BlockSpec(blk, lambda b: (b, 0, 0))
    prev = pl.BlockSpec(blk, lambda b: (jnp.maximum(b - 1, 0), 0, 0))
    shape = _sds((r, qb, width))
    fwd_call, bwd_call = _block_op(name, f, (r,), [cur, prev, cur, prev, cur], [(shape, cur), (shape, cur)],
                                   (q, k, k, v, v), (0, 1, 2, 3, 4), gdefs={1: (shape, cur), 3: (shape, cur)})

    def to_prev(g):
        return jnp.concatenate([g[1:], jnp.zeros_like(g[:1])], axis=0)

    @jax.custom_vjp
    def op(q, k, v):
        return tuple(fwd_call(q, k, k, v, v))

    def op_b(res, cts):
        q, k, v = res
        dq, dkp, dkc, dvp, dvc = bwd_call(q, k, k, v, v, *cts)
        return dq, dkc + to_prev(dkp), dvc + to_prev(dvp)

    op.defvjp(lambda q, k, v: (tuple(fwd_call(q, k, k, v, v)), (q, k, v)), op_b)
    return op(q, k, v)


def dilated_attention(q, k, v, name):
    s = q.shape[0]
    outs, lses = [], []
    for d in DILATIONS:
        length = s // d
        nb = length // QB

        def to_residue(t):
            return t.reshape(length, d, ATTN_WIDTH).transpose(1, 0, 2).reshape(d * nb, QB, ATTN_WIDTH)

        def from_residue(t):
            return t.reshape(d, length, ATTN_WIDTH).transpose(1, 0, 2).reshape(s, ATTN_WIDTH)

        o, lse = band_attention(to_residue(q), to_residue(k), to_residue(v), nb, f"{name}_d{d}")
        outs.append(from_residue(o))
        lses.append(from_residue(lse))
    return attn_merge(outs, lses, name + "_merge")


def cross_attention(q, kv, name):
    s = q.shape[0]
    m = kv.shape[0]
    width = XATTN_HEADS * XATTN_HEAD_DIM
    tq = 512

    def f(q, k, v):
        sc = _mxu(q, k, "nt") * (XATTN_HEAD_DIM ** -0.5)
        mx = lax.stop_gradient(jnp.max(sc, axis=-1, keepdims=True))
        p = jnp.exp(sc - mx)
        return (_mxu(p / jnp.sum(p, axis=-1, keepdims=True), v, "nn"),)

    q_spec = pl.BlockSpec((tq, XATTN_HEAD_DIM), lambda a, i: (i, a))
    k_spec = pl.BlockSpec((m, XATTN_HEAD_DIM), lambda a, i: (0, a))
    v_spec = pl.BlockSpec((m, XATTN_HEAD_DIM), lambda a, i: (0, a + XATTN_HEADS))
    half = _sds((m, width))
    fwd_call, bwd_call = _block_op(name, f, (XATTN_HEADS, s // tq), [q_spec, k_spec, v_spec], [(_sds((s, width)), q_spec)],
                                   (q, kv, kv), (0, 1, 2), acc={1: (1,), 2: (1,)}, gdefs={1: (half, k_spec), 2: (half, k_spec)})

    @jax.custom_vjp
    def op(q, kv):
        return fwd_call(q, kv, kv)[0]

    def op_b(res, ct):
        q, kv = res
        dq, dk, dv = bwd_call(q, kv, kv, ct)
        return dq, jnp.concatenate([dk, dv], axis=1)

    op.defvjp(lambda q, kv: (fwd_call(q, kv, kv)[0], (q, kv)), op_b)
    return op(q, kv)


def _hi(a, b, form="nn"):
    dims = {"nn": ((1,), (0,)), "nt": ((1,), (1,)), "tn": ((0,), (0,))}[form]
    return lax.dot_general(a, b, (dims, ((), ())), precision=lax.Precision.HIGH, preferred_element_type=f32)


def _running_sum(g):
    def raw(x, form):
        c = x.shape[0]
        tri = (lax.broadcasted_iota(jnp.int32, (c, c), 0) >= lax.broadcasted_iota(jnp.int32, (c, c), 1)).astype(bf16)
        hi = x.astype(bf16)
        rest = x - hi.astype(f32)
        mid = rest.astype(bf16)
        low = (rest - mid.astype(f32)).astype(bf16)
        dims = (((1,) if form == "nn" else (0,), (0,)), ((), ()))
        return sum(lax.dot_general(tri, part, dims, preferred_element_type=f32) for part in (hi, mid, low))

    @jax.custom_vjp
    def run(x):
        return raw(x, "nn")

    run.defvjp(lambda x: (raw(x, "nn"), None), lambda _, ct: (raw(ct, "tn"),))
    return run(g)


def _unit_lower_inverse(a):
    c = a.shape[0]
    eye = (lax.broadcasted_iota(jnp.int32, (c, c), 0) == lax.broadcasted_iota(jnp.int32, (c, c), 1)).astype(f32)
    inv, power = eye - a, -a
    for _ in range(c.bit_length() - 2):
        power = _hi(power, power)
        inv = inv + _hi(inv, power)
    return inv


def _known_inverse(a, t):
    @jax.custom_vjp
    def inv(a, t):
        return t

    def inv_b(t, ct):
        return -_hi(_hi(t, ct, "tn"), t, "nt"), jnp.zeros_like(t)

    inv.defvjp(lambda a, t: (t, t), inv_b)
    return inv(a, t)


def _delta_chunk(q, k, v, g, beta, s0, known_inv=None):
    c = q.shape[0]
    i = lax.broadcasted_iota(jnp.int32, (c, c), 0)
    j = lax.broadcasted_iota(jnp.int32, (c, c), 1)
    causal, strict = i >= j, i > j
    dec = _running_sum(g)
    dec_i = dec[:, :c]
    rel = jnp.exp(jnp.where(causal, dec_i - dec_i.T, -jnp.inf))
    k_beta = k * beta
    a = jnp.where(strict, _mxu(k_beta, k, "nt") * rel, 0.0)
    inv = _unit_lower_inverse(a) if known_inv is None else _known_inverse(a, known_inv)
    e_dec = jnp.exp(dec)
    u = _hi(inv, v * beta)
    w = _hi(inv, k_beta * e_dec)
    attn = jnp.where(causal, _mxu(q, k, "nt") * rel, 0.0)
    total = jnp.sum(g, axis=0, keepdims=True)
    v_new = u - _mxu(w, s0, "nn")
    o = _mxu(q * e_dec, s0, "nn") + _mxu(attn, v_new, "nn")
    s1 = s0 * jnp.exp(total) + _mxu(k * jnp.exp(total - dec), v_new, "tn")
    return o, s1, inv


def _delta_rule_call(name, walk, n, in_specs, out_specs, out_shape, operands, exchange):
    n_in, n_out = len(in_specs), len(out_specs)
    carried = len(exchange.operands) if exchange else 0

    def body(*refs):
        ins, refs = refs[:n_in], refs[n_in:]
        x_refs, refs = refs[:carried], refs[carried:]
        outs, refs = refs[:n_out], refs[n_out:]
        land_refs, (state, *sems) = refs[:carried], refs[carried:]
        step = pl.program_id(0)
        if exchange:
            start, finish = exchange.bind(x_refs, land_refs, sems)
            pl.when(step == 0)(start)

        @pl.when(step == 0)
        def _():
            state[...] = jnp.zeros_like(state)

        walk(ins, outs, state)
        if exchange:
            pl.when(step == n - 1)(finish)

    return pl.pallas_call(
        body, name=name, grid=(n,), in_specs=list(in_specs) + [ANY] * carried, out_specs=list(out_specs) + [ANY] * carried,
        out_shape=list(out_shape) + (exchange.out_shapes if exchange else []),
        scratch_shapes=[pltpu.VMEM((GDN_HEAD_DIM, GDN_WIDTH), f32)] + (exchange.scratch if exchange else []),
        compiler_params=_params(1))(*operands, *(exchange.operands if exchange else []))


def _delta_heads():
    heads = [slice(hd * GDN_HEAD_DIM, (hd + 1) * GDN_HEAD_DIM) for hd in range(GDN_HEADS)]
    inv_at = [slice(hd * GDN_CHUNK, (hd + 1) * GDN_CHUNK) for hd in range(GDN_HEADS)]
    return heads, inv_at


def _head_chunk(q, k, v, gates, s0, head, known_inv=None):
    g = jnp.broadcast_to(gates[:, head:head + 1], q.shape)
    beta = jnp.broadcast_to(gates[:, GDN_HEADS + head:GDN_HEADS + head + 1], q.shape)
    return _delta_chunk(q, k, v, g, beta, s0, known_inv)


def delta_rule_fwd(q, k, v, gates, name, exchange=None):
    s, width = q.shape
    c, dk = GDN_CHUNK, GDN_HEAD_DIM
    n = s // c
    heads, inv_at = _delta_heads()

    def walk(ins, outs, state):
        q_ref, k_ref, v_ref, gates_ref = ins
        o_ref, s_in_ref, inv_ref = outs
        s_in_ref[...] = state[...]
        gates = gates_ref[...]
        xs = [[r[:, hd] for r in (q_ref, k_ref, v_ref)] + [gates, state[:, hd], i] for i, hd in enumerate(heads)]
        ys = [_head_chunk(*x) for x in xs]
        for hd, at, (o, s1, inv) in zip(heads, inv_at, ys):
            o_ref[:, hd], state[:, hd], inv_ref[:, at] = o, s1, inv

    blk = pl.BlockSpec((c, width), lambda t: (t, 0))
    gt = pl.BlockSpec((c, LANES), lambda t: (t, 0))
    st = pl.BlockSpec((dk, width), lambda t: (t, 0))
    iv = pl.BlockSpec((c, GDN_HEADS * c), lambda t: (t, 0))
    return _delta_rule_call(name, walk, n, [blk] * 3 + [gt], [blk, st, iv],
                            [_sds((s, width)), _sds((n * dk, width)), _sds((s, GDN_HEADS * c))], (q, k, v, gates), exchange)


def delta_rule_bwd(q, k, v, gates, s_in, inv, do, name, exchange=None):
    s, width = q.shape
    c, dk = GDN_CHUNK, GDN_HEAD_DIM
    n = s // c
    heads, inv_at = _delta_heads()

    def walk(ins, outs, dstate):
        q_ref, k_ref, v_ref, gates_ref, s_ref, inv_ref, do_ref = ins
        dq_ref, dk_ref, dv_ref, dgates_ref = outs
        gates = gates_ref[...]
        xs = [[r[:, hd] for r in (q_ref, k_ref, v_ref)] + [gates, s_ref[:, hd]] for hd in heads]
        known = [inv_ref[:, at] for at in inv_at]
        cts = [(do_ref[:, hd], dstate[:, hd]) for hd in heads]
        grads = []
        for i, (x, t, ct) in enumerate(zip(xs, known, cts)):
            _, vjp = jax.vjp(lambda *y, t=t, i=i: _head_chunk(*y, i, known_inv=t)[:2], *x)
            grads.append(vjp(ct))
        dgates = grads[0][3]
        for g in grads[1:]:
            dgates = dgates + g[3]
        dgates_ref[...] = dgates
        for hd, (dq, dk_, dv, _, ds0) in zip(heads, grads):
            dq_ref[:, hd], dk_ref[:, hd], dv_ref[:, hd], dstate[:, hd] = dq, dk_, dv, ds0

    blk = pl.BlockSpec((c, width), lambda t: (n - 1 - t, 0))
    gt = pl.BlockSpec((c, LANES), lambda t: (n - 1 - t, 0))
    st = pl.BlockSpec((dk, width), lambda t: (n - 1 - t, 0))
    iv = pl.BlockSpec((c, GDN_HEADS * c), lambda t: (n - 1 - t, 0))
    return _delta_rule_call(name, walk, n, [blk] * 3 + [gt, st, iv, blk], [blk] * 3 + [gt],
                            [_sds((s, width))] * 3 + [_sds((s, LANES))], (q, k, v, gates, s_in, inv, do), exchange)


def adamw(w, g, m, v, name):
    shape = w.shape
    if len(shape) == 2:
        grid, spec = (1,), pl.BlockSpec(shape, lambda i: (0, 0))
    else:
        tile = shape[1] if shape[1] <= 512 else _pick(shape[1], (512, 256, 128))
        grid, spec = (shape[0], shape[1] // tile), pl.BlockSpec((None, tile, shape[2]), lambda layer, i: (layer, i, 0))

    def body(w_ref, g_ref, m_ref, v_ref, d_ref, nm_ref, nv_ref):
        grad = g_ref[...]
        nm = ADAM_B1 * m_ref[...] + (1.0 - ADAM_B1) * grad
        nv = ADAM_B2 * v_ref[...] + (1.0 - ADAM_B2) * (grad * grad)
        m_hat = nm / (1.0 - ADAM_B1 ** ADAM_STEP)
        v_hat = nv / (1.0 - ADAM_B2 ** ADAM_STEP)
        d_ref[...] = -ADAM_LR * (m_hat / (jnp.sqrt(v_hat) + ADAM_EPS) + ADAM_WD * w_ref[...])
        nm_ref[...] = nm
        nv_ref[...] = nv

    return tuple(pl.pallas_call(body, name=name, grid=grid, in_specs=[spec] * 4, out_specs=[spec] * 3,
                                out_shape=[_sds(shape)] * 3, compiler_params=_params(len(grid)))(w, g, m, v))


def _place():
    return lax.axis_index("x"), lax.axis_index("y"), lax.axis_index("c")


def _flip(p, bits):
    return tuple(1 - v if (bits >> s) & 1 else v for v, s in zip(p, (2, 1, 0)))


def _slot(p):
    return 4 * p[0] + 2 * p[1] + p[2]


def _chip_of(p):
    return 2 * p[0] + p[1]


ANY = pl.BlockSpec(memory_space=pl.ANY)


class Gather:
    scratch = (pltpu.SemaphoreType.DMA((7,)), pltpu.SemaphoreType.DMA((7,)), pltpu.SemaphoreType.DMA)

    def __init__(self, shard):
        self.operand = shard
        self.out_shape = jax.ShapeDtypeStruct((N_DEV,) + shard.shape, shard.dtype)

    def bind(self, x_ref, out_ref, send_sems, recv_sems, local_sem):
        me = _place()
        sibling = _flip(me, 1)
        chips = [_flip(me, 4), _flip(me, 2), _flip(me, 6)]

        def copy(k, block, to, src=None):
            return pltpu.make_async_remote_copy(
                src_ref=out_ref.at[_slot(block)] if src is None else src, dst_ref=out_ref.at[_slot(block)],
                send_sem=send_sems.at[k], recv_sem=recv_sems.at[k], device_id=to, device_id_type=MESH)

        mine = pltpu.make_async_copy(x_ref, out_ref.at[_slot(me)], local_sem)
        first = [copy(0, me, sibling, src=x_ref)] + [copy(1 + j, me, chip, src=x_ref) for j, chip in enumerate(chips)]
        passed = [copy(4 + j, chip, sibling) for j, chip in enumerate(chips)]

        def start():
            mine.start()
            for cp in first:
                cp.start()

        def finish():
            for j, chip in enumerate(chips):
                copy(1 + j, chip, me).wait_recv()
                passed[j].start()
            copy(0, sibling, me).wait_recv()
            for j, chip in enumerate(chips):
                copy(4 + j, _flip(chip, 1), me).wait_recv()
            for cp in first + passed:
                cp.wait_send()
            mine.wait()

        return start, finish


class ChipExchange:
    scratch = (pltpu.SemaphoreType.DMA((3,)), pltpu.SemaphoreType.DMA((3,)), pltpu.SemaphoreType.DMA)

    def __init__(self, blocks):
        self.operand = blocks
        self.out_shape = jax.ShapeDtypeStruct(blocks.shape, blocks.dtype)

    def bind(self, x_ref, out_ref, send_sems, recv_sems, local_sem):
        me = _place()
        peers = [_flip(me, 4), _flip(me, 2), _flip(me, 6)]
        mine = pltpu.make_async_copy(x_ref.at[_chip_of(me)], out_ref.at[_chip_of(me)], local_sem)

        def copy(j, src_chip, dst_chip):
            return pltpu.make_async_remote_copy(
                src_ref=x_ref.at[src_chip], dst_ref=out_ref.at[dst_chip], send_sem=send_sems.at[j],
                recv_sem=recv_sems.at[j], device_id=peers[j], device_id_type=MESH)

        sends = [copy(j, _chip_of(peer), _chip_of(me)) for j, peer in enumerate(peers)]

        def start():
            mine.start()
            for cp in sends:
                cp.start()

        def finish():
            for j, peer in enumerate(peers):
                copy(j, _chip_of(me), _chip_of(peer)).wait_recv()
            for cp in sends:
                cp.wait_send()
            mine.wait()

        return start, finish


class Together:
    def __init__(self, *parts):
        self.parts = parts
        self.operands = [p.operand for p in parts]
        self.out_shapes = [p.out_shape for p in parts]
        self.scratch = [s for p in parts for s in p.scratch]

    def bind(self, x_refs, out_refs, sems):
        bound, at = [], 0
        for p, x_ref, out_ref in zip(self.parts, x_refs, out_refs):
            bound.append(p.bind(x_ref, out_ref, *sems[at:at + len(p.scratch)]))
            at += len(p.scratch)

        def start():
            for s, _ in bound:
                s()

        def finish():
            for _, f in bound:
                f()

        return start, finish


def exchange_alone(exchange, name):
    n = len(exchange.operands)

    def body(*refs):
        start, finish = exchange.bind(refs[:n], refs[n:2 * n], refs[2 * n:])
        start()
        finish()

    return pl.pallas_call(body, name=name, out_shape=exchange.out_shapes, in_specs=[ANY] * n, out_specs=[ANY] * n,
                          scratch_shapes=exchange.scratch)(*exchange.operands)


def _row_tile(rows):
    return max([t for t in range(16, min(rows, 1024) + 1, 16) if rows % t == 0] or [rows])


def pair_exchange(blocks, name):
    n = len(blocks)

    def body(*refs):
        x_refs, theirs_refs, (send_sems, recv_sems) = refs[:n], refs[n:2 * n], refs[2 * n:]
        me = _place()
        remote = [pltpu.make_async_remote_copy(
            src_ref=x_refs[t].at[2 * q + 1 - me[2]], dst_ref=theirs_refs[t].at[q], send_sem=send_sems.at[4 * t + q],
            recv_sem=recv_sems.at[4 * t + q], device_id=_flip(me, 1), device_id_type=MESH) for t in range(n) for q in range(4)]
        for cp in remote:
            cp.start()
        for cp in remote:
            cp.wait()

    return pl.pallas_call(
        body, name=name, out_shape=[jax.ShapeDtypeStruct((4,) + b.shape[1:], b.dtype) for b in blocks], in_specs=[ANY] * n,
        out_specs=[ANY] * n, scratch_shapes=[pltpu.SemaphoreType.DMA((4 * n,)), pltpu.SemaphoreType.DMA((4 * n,))])(*blocks)


def pair_add(blocks, theirs, name):
    n, rows, width = theirs.shape
    tile = _row_tile(rows)
    spec = pl.BlockSpec((None, tile, width), lambda q, i: (q, i, 0))
    south = pl.BlockSpec((None, None, tile, width), lambda q, i: (q, 0, i, 0))
    north = pl.BlockSpec((None, None, tile, width), lambda q, i: (q, 1, i, 0))

    def body(s_ref, n_ref, b_ref, o_ref):
        mine = jnp.where(lax.axis_index("c") == 0, s_ref[...], n_ref[...])
        o_ref[...] = (mine.astype(f32) + b_ref[...].astype(f32)).astype(o_ref.dtype)

    by_core = blocks.reshape(n, 2, rows, width)
    return pl.pallas_call(body, name=name, grid=(n, rows // tile), in_specs=[south, north, spec], out_specs=spec,
                          out_shape=jax.ShapeDtypeStruct(theirs.shape, theirs.dtype), compiler_params=_params(2))(by_core, by_core, theirs)


def sum_slots(blocks, name):
    n, rows, width = blocks.shape
    tile = _row_tile(rows)

    def body(x_ref, o_ref):
        total = x_ref[0].astype(f32)
        for s in range(1, n):
            total = total + x_ref[s].astype(f32)
        o_ref[...] = total

    return pl.pallas_call(
        body, name=name, grid=(rows // tile,), in_specs=[pl.BlockSpec((n, tile, width), lambda i: (0, i, 0))],
        out_specs=pl.BlockSpec((tile, width), lambda i: (i, 0)), out_shape=_sds((rows, width)), compiler_params=_params(1))(blocks)


def all_reduce_small(x, name):
    rows, width = x.shape

    def body(x_ref, o_ref, land, send_sems, recv_sems):
        me = _place()
        copies = []
        for k in range(1, N_DEV):
            peer = _flip(me, k)
            copies.append(pltpu.make_async_remote_copy(
                src_ref=x_ref, dst_ref=land.at[_slot(me)], send_sem=send_sems.at[k - 1], recv_sem=recv_sems.at[k - 1],
                device_id=peer, device_id_type=MESH))
        for cp in copies:
            cp.start()
        land[_slot(me)] = x_ref[...]
        for k in range(1, N_DEV):
            peer = _flip(me, k)
            pltpu.make_async_remote_copy(
                src_ref=x_ref, dst_ref=land.at[_slot(peer)], send_sem=send_sems.at[k - 1], recv_sem=recv_sems.at[k - 1],
                device_id=peer, device_id_type=MESH).wait_recv()
        total = land[0]
        for s in range(1, N_DEV):
            total = total + land[s]
        o_ref[...] = total
        for cp in copies:
            cp.wait_send()

    return pl.pallas_call(
        body, name=name, out_shape=_sds((rows, width)), in_specs=[pl.BlockSpec(memory_space=pltpu.VMEM)],
        out_specs=pl.BlockSpec(memory_space=pltpu.VMEM),
        scratch_shapes=[pltpu.VMEM((N_DEV, rows, width), f32), pltpu.SemaphoreType.DMA((7,)), pltpu.SemaphoreType.DMA((7,))],
    )(x)


def _pack_big(shards):
    packed = {name: shards[name].astype(bf16) for name in COL_SHARDED}
    packed["rows"] = jnp.concatenate([shards[name].astype(bf16) for name, _ in ROW_SHARDED], axis=1)
    return packed


def _unpack_gathered(gathered):
    full = {}
    for name, part in gathered.items():
        if name in COL_SHARDED:
            full[name] = part.transpose(1, 0, 2).reshape(D_MODEL, N_DEV * part.shape[2])
        else:
            at = 0
            for weight, rows in ROW_SHARDED:
                full[weight] = part[:, at:at + rows, :].reshape(N_DEV * rows, D_MODEL)
                at += rows
    if "w_gate_up" in full:
        full["w_gate_up"] = _interleave_gate_up(full["w_gate_up"])
    if "w_in" in full:
        w_in = full.pop("w_in")
        full["w_main"] = jnp.concatenate([w_in[:, :AB_AT], w_in[:, AB_AT + 2 * GDN_HEADS:]], axis=1)
        full["w_ab"] = jnp.pad(w_in[:, AB_AT:AB_AT + 2 * GDN_HEADS], ((0, 0), (0, LANES - 2 * GDN_HEADS)))
    return full


def _pack_grads(grads, group):
    packed = {}
    for name in group:
        if name == "w_in":
            main, ab = grads["w_main"], grads["w_ab"]
            g = jnp.concatenate([main[:, :AB_AT], ab[:, :2 * GDN_HEADS], main[:, AB_AT:]], axis=1)
        elif name == "w_gate_up":
            g = _interleave_gate_up(grads[name], undo=True)
        elif name == "rows":
            packed[name] = jnp.concatenate([grads[weight].reshape(N_DEV, rows, D_MODEL) for weight, rows in ROW_SHARDED], axis=1)
            continue
        else:
            g = grads[name]
        packed[name] = g.reshape(D_MODEL, N_DEV, g.shape[1] // N_DEV).transpose(1, 0, 2)
    return packed


def _unpack_shard(layers):
    out = {name: jnp.stack([layer[name] for layer in layers]) for name in COL_SHARDED}
    rows_pack, at = jnp.stack([layer["rows"] for layer in layers]), 0
    for weight, rows in ROW_SHARDED:
        out[weight] = rows_pack[:, at:at + rows, :]
        at += rows
    return out


def _rows_of(flat_len):
    return -(-flat_len // (8 * D_MODEL)) * 8


def _pack_small(parts):
    flat = jnp.concatenate([p.reshape(-1) for p in parts])
    rows = _rows_of(flat.shape[0])
    flat = jnp.pad(flat, (0, rows * D_MODEL - flat.shape[0]))
    return flat.reshape(rows, D_MODEL)


def _unpack_small(packed, like):
    flat, out, at = packed.reshape(-1), [], 0
    for p in like:
        out.append(flat[at:at + p.size].reshape(p.shape))
        at += p.size
    return out


def _rope_tables(positions):
    inv_freq = jnp.float32(ROPE_THETA) ** (-jnp.arange(0, ROPE_DIM, 2, dtype=f32) / ROPE_DIM)
    ang = positions.astype(f32)[:, None] * inv_freq
    cos, sin = jnp.cos(ang), jnp.sin(ang)
    rest = ATTN_HEAD_DIM - ROPE_DIM
    cos_h = jnp.concatenate([cos, cos, jnp.ones((cos.shape[0], rest), f32)], axis=1)
    sin_h = jnp.concatenate([-sin, sin, jnp.zeros((sin.shape[0], rest), f32)], axis=1)
    return jnp.tile(cos_h, (1, ATTN_HEADS)), jnp.tile(sin_h, (1, ATTN_HEADS))


HEAD_SMALL = ("norm_mix_pre", "conv_short", "conv_gdn", "gdn_a_log", "gdn_dt_bias")


def _layer_head(h, p, cos_t, sin_t):
    hn = rms_norm(h, p["norm_mix_pre"][None], "norm_mix_pre")
    proj = _linear(hn, p["w_main"], "w_main")
    ab = _linear(hn, p["w_ab"], "w_ab")
    aw, cw, gw = ATTN_WIDTH, CONV_WIDTH, GDN_WIDTH
    aq, ak, av, cb, cc, cx, gqkv, gate = _split_cols(proj, (aw, aw, aw, cw, cw, cw, 3 * gw, gw))
    y_attn = dilated_attention(rope(aq, cos_t, sin_t, ATTN_HEAD_DIM ** -0.5, "rope_q"), rope(ak, cos_t, sin_t, 1.0, "rope_k"),
                               av, "attn")
    y_conv = short_conv(cb, cc, cx, p["conv_short"], "short_conv")
    qkv = gdn_pre(gqkv, p["conv_gdn"], "gdn_pre")
    pv = jnp.zeros((8, LANES), f32).at[0, :GDN_HEADS].set(p["gdn_a_log"]).at[1, :GDN_HEADS].set(p["gdn_dt_bias"])
    return (*_split_cols(qkv, (gw, gw, gw)), gate_beta(ab, pv, "gate_beta")), (gate, y_attn, y_conv)


def _layer_tail(h, o, gate, y_attn, y_conv, p, mem):
    y_gdn = gdn_post(o, gate, p["gdn_norm"][None], "gdn_post")
    mix = _linear(jnp.concatenate([y_attn, y_conv, y_gdn], axis=1), p["w_out"], "w_out")
    h, hn = add_norm_then_norm(h, mix, p["norm_mix_post"][None], p["norm_xattn_pre"][None], "norm_mix_xattn")
    qx = _linear(hn, p["w_xq"], "w_xq")
    kv = _linear(rms_norm(mem, p["norm_mem"][None], "norm_mem"), p["w_xkv"], "w_xkv")
    xa = _linear(cross_attention(qx, kv, "xattn"), p["w_xo"], "w_xo")
    h, hn = add_norm_then_norm(h, xa, p["norm_xattn_post"][None], p["norm_ffn_pre"][None], "norm_xattn_ffn")
    return add_norm(h, swiglu_ffn(hn, p["w_gate_up"], p["w_down"], "ffn"), p["norm_ffn_post"][None], "norm_ffn_post")


def _pair_summed(grads, group, name):
    blocks = _pack_grads(grads, group)
    theirs = pair_exchange([blocks[n] for n in group], name + "_pair_exchange")
    return [pair_add(blocks[n], t, f"{name}_pair_add_{n}") for n, t in zip(group, theirs)]


def _forward_backward(x, packed, small, mem, cos_t, sin_t, target):
    def gathers(group, layer):
        return [Gather(packed[n][layer]) for n in group]

    h = x
    head_gathered = exchange_alone(Together(*gathers(HEAD_GROUP, 0)), "gather_first")
    saved = []
    for layer in range(DEPTH):
        at_layer = {n: t[layer] for n, t in small.items()}
        head_p = {**_unpack_gathered(dict(zip(HEAD_GROUP, head_gathered))), **{n: at_layer[n] for n in HEAD_SMALL}}
        (rule_in, rest), head_vjp = jax.vjp(lambda h, hp: _layer_head(h, hp, cos_t, sin_t), h, head_p)
        carried = gathers(TAIL_GROUP, layer) + (gathers(HEAD_GROUP, layer + 1) if layer + 1 < DEPTH else [])
        results = delta_rule_fwd(*rule_in, "delta_rule_fwd", Together(*carried))
        (o, s_in, inv), landed = results[:3], results[3:]
        head_gathered = landed[len(TAIL_GROUP):]
        tail_p = {**_unpack_gathered(dict(zip(TAIL_GROUP, landed))), **{n: t for n, t in at_layer.items() if n not in HEAD_SMALL}}
        h, tail_vjp = jax.vjp(lambda h, o, rest, tp: _layer_tail(h, o, *rest, tp, mem), h, o, rest, tail_p)
        saved.append((head_vjp, tail_vjp, rule_in, s_in, inv))

    loss, dh = jax.value_and_grad(lambda y: loss_rows(y, target, "loss"))(h)

    def summed(group, landed):
        return {n: sum_slots(t, "sum_grads_" + n) for n, t in zip(group, landed)}

    big_grads, small_grads, head_pending = [{} for _ in range(DEPTH)], [None] * DEPTH, []
    for layer in reversed(range(DEPTH)):
        head_vjp, tail_vjp, rule_in, s_in, inv = saved[layer]
        dh_tail, do, d_rest, d_tail_p = tail_vjp(dh)
        carried = [ChipExchange(t) for t in _pair_summed(d_tail_p, TAIL_GROUP, "tail") + head_pending]
        results = delta_rule_bwd(*rule_in, s_in, inv, do, "delta_rule_bwd", Together(*carried))
        d_rule_in, landed = results[:4], results[4:]
        big_grads[layer].update(summed(TAIL_GROUP, landed))
        if head_pending:
            big_grads[layer + 1].update(summed(HEAD_GROUP, landed[len(TAIL_GROUP):]))
        dh_head, d_head_p = head_vjp((tuple(d_rule_in), d_rest))
        dh = dh_tail + dh_head
        small_grads[layer] = {n: t for n, t in {**d_head_p, **d_tail_p}.items() if n in small}
        head_pending = _pair_summed(d_head_p, HEAD_GROUP, "head")
    landed = exchange_alone(Together(*[ChipExchange(t) for t in head_pending]), "exchange_last")
    big_grads[0].update(summed(HEAD_GROUP, landed))
    return loss, dh, big_grads, small_grads


def kernel(x, mem, positions, norm_mix_pre, norm_mix_post, w_in, conv_short, conv_gdn, gdn_a_log, gdn_dt_bias, gdn_norm, w_out, norm_mem, norm_xattn_pre, norm_xattn_post, w_xq, w_xkv, w_xo, norm_ffn_pre, norm_ffn_post, w_gate_up, w_down, loss_target, m_norm_mix_pre, m_norm_mix_post, m_w_in, m_conv_short, m_conv_gdn, m_gdn_a_log, m_gdn_dt_bias, m_gdn_norm, m_w_out, m_norm_mem, m_norm_xattn_pre, m_norm_xattn_post, m_w_xq, m_w_xkv, m_w_xo, m_norm_ffn_pre, m_norm_ffn_post, m_w_gate_up, m_w_down, v_norm_mix_pre, v_norm_mix_post, v_w_in, v_conv_short, v_conv_gdn, v_gdn_a_log, v_gdn_dt_bias, v_gdn_norm, v_w_out, v_norm_mem, v_norm_xattn_pre, v_norm_xattn_post, v_w_xq, v_w_xkv, v_w_xo, v_norm_ffn_pre, v_norm_ffn_post, v_w_gate_up, v_w_down):
    given = dict(locals())
    weights = {n: given[n] for n in WEIGHTS}
    me = _slot(_place())

    def in_place(shard):
        full = jnp.zeros(shard.shape[:-1] + (shard.shape[-1] * N_DEV,), f32)
        return lax.dynamic_update_slice_in_dim(full, shard, me * shard.shape[-1], axis=shard.ndim - 1)

    placed = [in_place(conv_short), in_place(conv_gdn)]
    conv_short_full, conv_gdn_full = _unpack_small(all_reduce_small(_pack_small(placed), "gather_conv"), placed)
    small = {n: weights[n] for n in NORMS + ("gdn_a_log", "gdn_dt_bias", "gdn_norm")}
    small["conv_short"], small["conv_gdn"] = conv_short_full, conv_gdn_full

    cos_t, sin_t = _rope_tables(positions[0])
    loss, grad_x, big_layers, small_layers = _forward_backward(
        x[0], _pack_big(weights), small, mem[0], cos_t, sin_t, loss_target[0])
    grads = _unpack_shard(big_layers)

    names = sorted(small)
    parts = [jnp.stack([layer[n] for layer in small_layers]) for n in names] + [loss.reshape(1)]
    reduced = _unpack_small(all_reduce_small(_pack_small(parts), "reduce_small"), parts)
    loss = reduced[-1][0]
    for n, g in zip(names, reduced[:-1]):
        if n in ("conv_short", "conv_gdn"):
            width = weights[n].shape[-1]
            g = lax.dynamic_slice_in_dim(g, me * width, width, axis=g.ndim - 1)
        grads[n] = g

    delta, new_m, new_v = {}, {}, {}
    for n in WEIGHTS:
        delta[n], new_m[n], new_v[n] = adamw(weights[n], grads[n], given["m_" + n], given["v_" + n], "adamw_" + n)
    return (loss, grad_x[None], *[grads[n] for n in WEIGHTS], *[delta[n] for n in WEIGHTS],
            *[new_m[n] for n in WEIGHTS], *[new_v[n] for n in WEIGHTS])
```

```python
import functools

import jax
import jax.numpy as jnp
from jax import lax
from jax.experimental import pallas as pl
from jax.experimental.pallas import tpu as pltpu

f32 = jnp.float32
bf16 = jnp.bfloat16
HIGHEST = lax.Precision.HIGHEST
MESH = pl.DeviceIdType.MESH

N_DEV = 8
DEPTH = 4
D_MODEL = 1024
EPS = 1e-6
ATTN_HEADS, ATTN_HEAD_DIM = 4, 64
ATTN_WIDTH = ATTN_HEADS * ATTN_HEAD_DIM
DILATIONS = (1, 4, 16)
QB = 128
ROPE_THETA = 500000.0
ROPE_DIM = ATTN_HEAD_DIM // 4
CONV_WIDTH = 256
GDN_HEADS, GDN_HEAD_DIM = 4, 128
GDN_WIDTH = GDN_HEADS * GDN_HEAD_DIM
GDN_CHUNK = 64
XATTN_HEADS, XATTN_HEAD_DIM = 4, 256
FFN_HIDDEN = 2816
IN_WIDTH = 3592
AB_AT = 3 * ATTN_WIDTH + 3 * CONV_WIDTH + 3 * GDN_WIDTH
MAIN_WIDTH = IN_WIDTH - 2 * GDN_HEADS
LANES = 128
ROW_TILE = 256
VMEM_LIMIT = 56 * 1024 * 1024

ADAM_LR, ADAM_B1, ADAM_B2, ADAM_EPS, ADAM_WD, ADAM_STEP = 0.001, 0.9, 0.999, 1e-08, 0.01, 10

COL_SHARDED = ("w_in", "w_xkv", "w_gate_up")
ROW_SHARDED = (("w_out", 128), ("w_xq", 128), ("w_xo", 128), ("w_down", 352))
HEAD_GROUP = ("w_in",)
TAIL_GROUP = ("w_gate_up", "w_xkv", "rows")
NORMS = ("norm_mix_pre", "norm_mix_post", "norm_mem", "norm_xattn_pre", "norm_xattn_post", "norm_ffn_pre", "norm_ffn_post")
WEIGHTS = ("norm_mix_pre", "norm_mix_post", "w_in", "conv_short", "conv_gdn", "gdn_a_log", "gdn_dt_bias", "gdn_norm", "w_out",
           "norm_mem", "norm_xattn_pre", "norm_xattn_post", "w_xq", "w_xkv", "w_xo", "norm_ffn_pre", "norm_ffn_post",
           "w_gate_up", "w_down")


def _params(n_grid):
    return pltpu.CompilerParams(dimension_semantics=("arbitrary",) * n_grid, vmem_limit_bytes=VMEM_LIMIT)


def _pick(n, cands):
    for c in cands:
        if n % c == 0:
            return c
    return n


MXU_FLOPS = 9.0e14
HBM_BYTES_PER_S = 2.5e12
VMEM_RMW_BYTES_PER_S = 7.0e12
STEP_S = 0.4e-6
MATMUL_VMEM = 44 * 1024 * 1024


def _tiles(m, n, k, sa, sb, so, tn=None):
    def divisors(d):
        return sorted({d // s for s in range(1, d // LANES + 1) if d % s == 0 and (d // s) % LANES == 0}, reverse=True)

    best = None
    for tk in divisors(k):
        nk = k // tk
        for tm in divisors(m):
            for tn_ in [tn] if tn else divisors(n):
                per_step = tm * tk * sa + tk * tn_ * sb + tm * tn_ * so
                vmem = 2 * per_step + (tm * tn_ * 4 if nk > 1 else 0)
                vmem += (tm * tk * 2 if sa == 4 else 0) + (tk * tn_ * 2 if sb == 4 else 0) + tm * tn_ * 4
                if vmem > MATMUL_VMEM:
                    continue
                moved = m * k * sa * (1 if nk == 1 else n // tn_) + k * n * sb * (1 if nk == 1 and n == tn_ else m // tm) + m * n * so
                busy = 2 * m * n * k / MXU_FLOPS + (m * n * 8 * nk / VMEM_RMW_BYTES_PER_S if nk > 1 else 0)
                cost = max(moved / HBM_BYTES_PER_S, busy) + per_step / HBM_BYTES_PER_S + (m // tm) * (n // tn_) * nk * STEP_S
                if best is None or cost < best[0]:
                    best = (cost, tm, tn_, tk)
    return best[1:]


def _mm(a, b, ta, tb, out_dtype, name, finish=None, exchange=None):
    m, k = (a.shape[1], a.shape[0]) if ta else a.shape
    n = b.shape[0] if tb else b.shape[1]
    tm, tn, tk = _tiles(m, n, k, a.dtype.itemsize, b.dtype.itemsize, jnp.dtype(out_dtype).itemsize, finish and finish[0])
    nk = k // tk
    if finish:
        assert nk == 1
        _, extra, results, function = finish
        dims = (((0 if ta else 1,), (1 if tb else 0,)), ((), ()))
        carried = len(exchange.operands) if exchange else 0
        steps = (m // tm, n // tn)

        def finish_body(a_ref, b_ref, *refs):
            extra_refs, refs = refs[:len(extra)], refs[len(extra):]
            x_refs, refs = refs[:carried], refs[carried:]
            out_refs, refs = refs[:len(results)], refs[len(results):]
            if exchange:
                at = pl.program_id(0) * steps[1] + pl.program_id(1)
                start, wait = exchange.bind(x_refs, refs[:carried], refs[carried:])
                pl.when(at == 0)(start)
            p = lax.dot_general(a_ref[...].astype(bf16), b_ref[...].astype(bf16), dims, preferred_element_type=f32)
            for r, o in zip(out_refs, function(p, *[r[...] for r in extra_refs])):
                r[...] = o.astype(r.dtype)
            if exchange:
                pl.when(at == steps[0] * steps[1] - 1)(wait)

        return pl.pallas_call(
            finish_body, name=name, grid=steps,
            in_specs=[pl.BlockSpec((tk, tm), lambda i, j: (0, i)) if ta else pl.BlockSpec((tm, tk), lambda i, j: (i, 0)),
                      pl.BlockSpec((tn, tk), lambda i, j: (j, 0)) if tb else pl.BlockSpec((tk, tn), lambda i, j: (0, j))]
            + [pl.BlockSpec((tm, cols), lambda i, j: (i, j)) for _, cols in extra] + [ANY] * carried,
            out_specs=[pl.BlockSpec((tm, cols), lambda i, j: (i, j)) for _, cols in results] + [ANY] * carried,
            out_shape=[jax.ShapeDtypeStruct((m, n // tn * cols), dt) for dt, cols in results] + (exchange.out_shapes if exchange else []),
            scratch_shapes=exchange.scratch if exchange else [],
            compiler_params=_params(2))(a, b, *[x for x, _ in extra], *(exchange.operands if exchange else []))
    a_spec = pl.BlockSpec((tk, tm), lambda i, j, kk: (kk, i)) if ta else pl.BlockSpec((tm, tk), lambda i, j, kk: (i, kk))
    b_spec = pl.BlockSpec((tn, tk), lambda i, j, kk: (j, kk)) if tb else pl.BlockSpec((tk, tn), lambda i, j, kk: (kk, j))
    dims = (((0 if ta else 1,), (1 if tb else 0,)), ((), ()))

    def body(a_ref, b_ref, o_ref, *acc):
        kk = pl.program_id(2)
        p = lax.dot_general(a_ref[...].astype(bf16), b_ref[...].astype(bf16), dims, preferred_element_type=f32)
        if nk == 1:
            o_ref[...] = p.astype(o_ref.dtype)
            return
        acc_ref, = acc

        @pl.when(kk == 0)
        def _():
            acc_ref[...] = p

        @pl.when(kk > 0)
        def _():
            acc_ref[...] += p

        @pl.when(kk == nk - 1)
        def _():
            o_ref[...] = acc_ref[...].astype(o_ref.dtype)

    return pl.pallas_call(
        body, name=name, grid=(m // tm, n // tn, nk), in_specs=[a_spec, b_spec],
        out_specs=pl.BlockSpec((tm, tn), lambda i, j, kk: (i, j)), out_shape=jax.ShapeDtypeStruct((m, n), out_dtype),
        scratch_shapes=[pltpu.VMEM((tm, tn), f32)] if nk > 1 else [], compiler_params=_params(3))(a, b)


def _linear(x, w, name):
    @jax.custom_vjp
    def lin(x, w):
        return _mm(x, w, False, False, f32, name + "_y")

    def lin_f(x, w):
        return _mm(x, w, False, False, f32, name + "_y"), (x, w)

    def lin_b(res, dy):
        x, w = res
        return _mm(dy, w, False, True, f32, name + "_dx"), _mm(x, dy, True, False, bf16, name + "_dw")

    lin.defvjp(lin_f, lin_b)
    return lin(x, w)


GATE_UP_TILE = 512


def _interleave_gate_up(w, undo=False):
    two_f = w.shape[1]
    half = GATE_UP_TILE // 2
    nb = two_f // GATE_UP_TILE
    if undo:
        order = [2 * j + side for side in range(2) for j in range(nb)]
    else:
        order = [side * nb + j for j in range(nb) for side in range(2)]
    return jnp.concatenate([w[:, b * half:(b + 1) * half] for b in order], axis=1)


def ffn_forward(hn, w_gate_up, w_down, name, exchange=None):
    half = GATE_UP_TILE // 2

    def act_of(p):
        return p, jax.nn.silu(p[:, :half]) * p[:, half:]

    gate_up, act, *landed = _mm(hn, w_gate_up, False, False, bf16, name + "_act",
                                (GATE_UP_TILE, [], [(bf16, GATE_UP_TILE), (bf16, half)], act_of), exchange)
    return _mm(act, w_down, False, False, f32, name + "_y"), (hn, w_gate_up, w_down, gate_up, act), landed


def ffn_backward(saved, dy, name, exchange=None):
    hn, w_gate_up, w_down, gate_up, act = saved
    half = GATE_UP_TILE // 2

    def d_gate_up_of(d_act, gate_up):
        g, u = gate_up[:, :half].astype(f32), gate_up[:, half:].astype(f32)
        sig = jax.nn.sigmoid(g)
        return (jnp.concatenate([d_act * u * sig * (1.0 + g * (1.0 - sig)), d_act * g * sig], axis=1),)

    d_gate_up, *landed = _mm(dy, w_down, False, True, bf16, name + "_dact",
                             (half, [(gate_up, GATE_UP_TILE)], [(bf16, GATE_UP_TILE)], d_gate_up_of), exchange)
    return (_mm(d_gate_up, w_gate_up, False, True, f32, name + "_dx"), _mm(hn, d_gate_up, True, False, bf16, name + "_dw1"),
            _mm(act, dy, True, False, bf16, name + "_dw2"), landed)


def _split_cols(x, widths):
    edges = [sum(widths[:i]) for i in range(len(widths) + 1)]

    def cut(x):
        return tuple(x[:, a:b] for a, b in zip(edges[:-1], edges[1:]))

    @jax.custom_vjp
    def split(x):
        return cut(x)

    split.defvjp(lambda x: (cut(x), None), lambda _, cts: (jnp.concatenate(cts, axis=1),))
    return split(x)


def _block_op(name, f, grid, in_specs, out_defs, arrays, diff, acc=None, gdefs=None):
    acc, gdefs = acc or {}, gdefs or {}
    n_in, n_out, n_grid = len(in_specs), len(out_defs), len(grid)

    def fwd_call(*xs):
        def body(*refs):
            outs = f(*[r[...] for r in refs[:n_in]])
            for r, o in zip(refs[n_in:], outs):
                r[...] = o.astype(r.dtype)

        return pl.pallas_call(
            body, name=name + "_fwd", grid=grid, in_specs=in_specs, out_specs=[d[1] for d in out_defs],
            out_shape=[d[0] for d in out_defs], compiler_params=_params(n_grid))(*xs)

    def bwd_call(*xs_and_cts):
        def body(*refs):
            xs = [r[...] for r in refs[:n_in]]
            cts = tuple(r[...] for r in refs[n_in:n_in + n_out])

            def of_diff(*dx):
                full = list(xs)
                for i, v in zip(diff, dx):
                    full[i] = v
                return tuple(f(*full))

            _, vjp = jax.vjp(of_diff, *[xs[i] for i in diff])
            grads = vjp(cts)
            for i, g, r in zip(diff, grads, refs[n_in + n_out:]):
                if i in acc:
                    first = functools.reduce(jnp.logical_and, [pl.program_id(a) == 0 for a in acc[i]])

                    @pl.when(first)
                    def _(r=r):
                        r[...] = jnp.zeros_like(r)

                    r[...] += g.astype(r.dtype)
                else:
                    r[...] = g.astype(r.dtype)

        g_defs = [gdefs.get(i, (jax.ShapeDtypeStruct(arrays[i].shape, f32), in_specs[i])) for i in diff]
        return pl.pallas_call(
            body, name=name + "_bwd", grid=grid, in_specs=list(in_specs) + [d[1] for d in out_defs],
            out_specs=[d[1] for d in g_defs], out_shape=[d[0] for d in g_defs], compiler_params=_params(n_grid))(*xs_and_cts)

    return fwd_call, bwd_call


def _simple_op(name, f, grid, in_specs, out_defs, arrays, diff, acc=None):
    fwd_call, bwd_call = _block_op(name, f, grid, in_specs, out_defs, arrays, diff, acc)

    @jax.custom_vjp
    def op(*xs):
        return tuple(fwd_call(*xs))

    def op_f(*xs):
        return tuple(fwd_call(*xs)), xs

    def op_b(xs, cts):
        grads = bwd_call(*xs, *cts)
        out = [jnp.zeros_like(x) for x in xs]
        for i, g in zip(diff, grads):
            out[i] = g
        return tuple(out)

    op.defvjp(op_f, op_b)
    return op(*arrays)


def _rows(width, tile=ROW_TILE):
    return pl.BlockSpec((tile, width), lambda i: (i, 0))


def _whole(shape):
    return pl.BlockSpec(shape, lambda *_: (0,) * len(shape))


def _sds(shape):
    return jax.ShapeDtypeStruct(shape, f32)


def _rms(x, w):
    return x * lax.rsqrt(jnp.mean(x * x, axis=-1, keepdims=True) + EPS) * w


def rms_norm(x, w, name):
    r, d = x.shape
    return _simple_op(name, lambda x, w: (_rms(x, w),), (r // ROW_TILE,), [_rows(d), _whole((1, d))],
                      [(_sds((r, d)), _rows(d))], (x, w), (0, 1), {1: (0,)})[0]


def add_norm(h, y, w, name):
    r, d = h.shape
    return _simple_op(name, lambda h, y, w: (h + _rms(y, w),), (r // ROW_TILE,), [_rows(d), _rows(d), _whole((1, d))],
                      [(_sds((r, d)), _rows(d))], (h, y, w), (0, 1, 2), {2: (0,)})[0]


def add_norm_then_norm(h, y, w_post, w_pre, name):
    r, d = h.shape

    def f(h, y, w_post, w_pre):
        h_new = h + _rms(y, w_post)
        return h_new, _rms(h_new, w_pre)

    return _simple_op(name, f, (r // ROW_TILE,), [_rows(d), _rows(d), _whole((1, d)), _whole((1, d))],
                      [(_sds((r, d)), _rows(d))] * 2, (h, y, w_post, w_pre), (0, 1, 2, 3), {2: (0,), 3: (0,)})


def _swap8(x):
    def raw(x):
        lane = lax.broadcasted_iota(jnp.int32, x.shape, 1) % ATTN_HEAD_DIM
        half = ROPE_DIM // 2
        up = pltpu.roll(x, x.shape[1] - half, axis=1)
        down = pltpu.roll(x, half, axis=1)
        return jnp.where(lane < half, up, jnp.where(lane < ROPE_DIM, down, 0.0))

    @jax.custom_vjp
    def swap(x):
        return raw(x)

    swap.defvjp(lambda x: (raw(x), None), lambda _, g: (raw(g),))
    return swap(x)


def rope(x, cos_t, sin_t, scale, name):
    r, d = x.shape
    return _simple_op(name, lambda x, c, s: ((x * c + _swap8(x) * s) * scale,), (r // ROW_TILE,), [_rows(d)] * 3,
                      [(_sds((r, d)), _rows(d))], (x, cos_t, sin_t), (0,))[0]


def _shift_rows(x, k):
    n = x.shape[0]

    def down(x):
        row = lax.broadcasted_iota(jnp.int32, x.shape, 0)
        return jnp.where(row >= k, pltpu.roll(x, k, axis=0), 0.0)

    def up(x):
        row = lax.broadcasted_iota(jnp.int32, x.shape, 0)
        return jnp.where(row < n - k, pltpu.roll(x, n - k, axis=0), 0.0)

    @jax.custom_vjp
    def shift(x):
        return down(x)

    shift.defvjp(lambda x: (down(x), None), lambda _, g: (up(g),))
    return shift(x)


def _causal_conv(x, w):
    taps = w.shape[0]
    y = x * w[taps - 1:taps, :]
    for j in range(taps - 1):
        y = y + _shift_rows(x, taps - 1 - j) * w[j:j + 1, :]
    return y


def _cols(rows, at=0):
    return pl.BlockSpec((rows, LANES), lambda j: (0, at + j))


def short_conv(cb, cc, cx, w, name):
    s, c = cb.shape
    taps = w.shape[0]
    return _simple_op(name, lambda b, c_, x, w: (b * _causal_conv(c_ * x, w),), (c // LANES,),
                      [_cols(s)] * 3 + [_cols(taps)], [(_sds((s, c)), _cols(s))], (cb, cc, cx, w), (0, 1, 2, 3))[0]


def gdn_pre(qkv, w, name):
    s, c = qkv.shape
    taps = w.shape[0]

    def f(x, w):
        j = pl.program_id(0)
        y = jax.nn.silu(_causal_conv(x, w))
        normed = y * lax.rsqrt(jnp.sum(y * y, axis=-1, keepdims=True) + EPS)
        scale = jnp.where(j < GDN_HEADS, GDN_HEAD_DIM ** -0.5, 1.0).astype(f32)
        return (jnp.where(j < 2 * GDN_HEADS, normed * scale, y),)

    return _simple_op(name, f, (c // LANES,), [_cols(s), _cols(taps)], [(_sds((s, c)), _cols(s))], (qkv, w), (0, 1))[0]


def gate_beta(ab, pv, name):
    s = ab.shape[0]

    def f(ab, pv):
        lane = lax.broadcasted_iota(jnp.int32, ab.shape, 1)
        g = -jnp.exp(pv[0:1, :]) * jax.nn.softplus(ab + pv[1:2, :])
        return (jnp.where(lane < GDN_HEADS, g, jnp.where(lane < 2 * GDN_HEADS, jax.nn.sigmoid(ab), 0.0)),)

    return _simple_op(name, f, (s // ROW_TILE,), [_rows(LANES), _whole((8, LANES))], [(_sds((s, LANES)), _rows(LANES))],
                      (ab, pv), (0, 1), {1: (0,)})[0]


def gdn_post(o, gate, w, name):
    s, c = o.shape
    spec = pl.BlockSpec((ROW_TILE, LANES), lambda i, j: (i, j))
    return _simple_op(name, lambda o, g, w: (_rms(o, w) * jax.nn.silu(g),), (s // ROW_TILE, c // LANES),
                      [spec, spec, _whole((1, LANES))], [(_sds((s, c)), spec)], (o, gate, w), (0, 1, 2), {2: (0, 1)})[0]


def attn_merge(outs, lses, name):
    s, c = outs[0].shape

    def f(o1, o2, o3, l1, l2, l3):
        m = lax.stop_gradient(jnp.maximum(jnp.maximum(l1, l2), l3))
        e1, e2, e3 = jnp.exp(l1 - m), jnp.exp(l2 - m), jnp.exp(l3 - m)
        return ((e1 * o1 + e2 * o2 + e3 * o3) / (e1 + e2 + e3),)

    return _simple_op(name, f, (s // ROW_TILE,), [_rows(c)] * 6, [(_sds((s, c)), _rows(c))], (*outs, *lses), tuple(range(6)))[0]


def loss_rows(y, target, name):
    s, d = y.shape
    nt = s // ROW_TILE

    def f(y, t):
        e = y - t
        part = 0.5 * jnp.sum(jnp.mean(e * e, axis=-1, keepdims=True), axis=0, keepdims=True)
        return (jnp.broadcast_to(part * (1.0 / (8 * LANES)), (8, LANES)),)

    out = _simple_op(name, f, (nt,), [_rows(d)] * 2, [(_sds((nt * 8, LANES)), pl.BlockSpec((8, LANES), lambda i: (i, 0)))],
                     (y, target), (0,))[0]
    return jnp.sum(out)


def _mxu(a, b, form):
    dims = {"nn": ((1,), (0,)), "nt": ((1,), (1,)), "tn": ((0,), (0,))}

    def raw(a, b, form):
        return lax.dot_general(a.astype(bf16), b.astype(bf16), (dims[form], ((), ())), preferred_element_type=f32)

    @jax.custom_vjp
    def prod(a, b):
        return raw(a, b, form)

    def prod_b(res, ct):
        a, b = res
        if form == "nn":
            return raw(ct, b, "nt"), raw(a, ct, "tn")
        if form == "nt":
            return raw(ct, b, "nn"), raw(ct, a, "tn")
        return raw(b, ct, "nt"), raw(a, ct, "nn")

    prod.defvjp(lambda a, b: (raw(a, b, form), (a, b)), prod_b)
    return prod(a, b)


def band_attention(q, k, v, nb, name):
    r, qb, width = q.shape
    dh = ATTN_HEAD_DIM

    def f(q, kp, kc, vp, vc):
        has_prev = (pl.program_id(0) % nb) > 0
        i = lax.broadcasted_iota(jnp.int32, (qb, 2 * qb), 0)
        j = lax.broadcasted_iota(jnp.int32, (qb, 2 * qb), 1)
        seen = jnp.logical_or(jnp.logical_and(jnp.logical_and(j < qb, j >= i), has_prev), jnp.logical_and(j >= qb, j - qb <= i))
        keys, values = jnp.concatenate([kp, kc], axis=0), jnp.concatenate([vp, vc], axis=0)
        outs, lses = [], []
        for hd in range(width // dh):
            at = slice(hd * dh, (hd + 1) * dh)
            sc = jnp.where(seen, _mxu(q[:, at], keys[:, at], "nt"), -jnp.inf)
            m = lax.stop_gradient(jnp.max(sc, axis=-1, keepdims=True))
            p = jnp.exp(sc - m)
            l = jnp.sum(p, axis=-1, keepdims=True)
            outs.append(_mxu(p / l, values[:, at], "nn"))
            lses.append(jnp.broadcast_to(m + jnp.log(l), (qb, dh)))
        return jnp.concatenate(outs, axis=1), jnp.concatenate(lses, axis=1)

    blk = (None, qb, width)
    cur = pl.BlockSpec(blk, lambda b: (b, 0, 0))
    prev = pl.BlockSpec(blk, lambda b: (jnp.maximum(b - 1, 0), 0, 0))
    shape = _sds((r, qb, width))
    fwd_call, bwd_call = _block_op(name, f, (r,), [cur, prev, cur, prev, cur], [(shape, cur), (shape, cur)],
                                   (q, k, k, v, v), (0, 1, 2, 3, 4), gdefs={1: (shape, cur), 3: (shape, cur)})

    def to_prev(g):
        return jnp.concatenate([g[1:], jnp.zeros_like(g[:1])], axis=0)

    @jax.custom_vjp
    def op(q, k, v):
        return tuple(fwd_call(q, k, k, v, v))

    def op_b(res, cts):
        q, k, v = res
        dq, dkp, dkc, dvp, dvc = bwd_call(q, k, k, v, v, *cts)
        return dq, dkc + to_prev(dkp), dvc + to_prev(dvp)

    op.defvjp(lambda q, k, v: (tuple(fwd_call(q, k, k, v, v)), (q, k, v)), op_b)
    return op(q, k, v)


def dilated_attention(q, k, v, name):
    s = q.shape[0]
    outs, lses = [], []
    for d in DILATIONS:
        length = s // d
        nb = length // QB

        def to_residue(t):
            return t.reshape(length, d, ATTN_WIDTH).transpose(1, 0, 2).reshape(d * nb, QB, ATTN_WIDTH)

        def from_residue(t):
            return t.reshape(d, length, ATTN_WIDTH).transpose(1, 0, 2).reshape(s, ATTN_WIDTH)

        o, lse = band_attention(to_residue(q), to_residue(k), to_residue(v), nb, f"{name}_d{d}")
        outs.append(from_residue(o))
        lses.append(from_residue(lse))
    return attn_merge(outs, lses, name + "_merge")


def cross_attention(q, kv, name):
    s = q.shape[0]
    m = kv.shape[0]
    width = XATTN_HEADS * XATTN_HEAD_DIM
    tq = 512

    def f(q, k, v):
        sc = _mxu(q, k, "nt") * (XATTN_HEAD_DIM ** -0.5)
        mx = lax.stop_gradient(jnp.max(sc, axis=-1, keepdims=True))
        p = jnp.exp(sc - mx)
        return (_mxu(p / jnp.sum(p, axis=-1, keepdims=True), v, "nn"),)

    q_spec = pl.BlockSpec((tq, XATTN_HEAD_DIM), lambda a, i: (i, a))
    k_spec = pl.BlockSpec((m, XATTN_HEAD_DIM), lambda a, i: (0, a))
    v_spec = pl.BlockSpec((m, XATTN_HEAD_DIM), lambda a, i: (0, a + XATTN_HEADS))
    half = _sds((m, width))
    fwd_call, bwd_call = _block_op(name, f, (XATTN_HEADS, s // tq), [q_spec, k_spec, v_spec], [(_sds((s, width)), q_spec)],
                                   (q, kv, kv), (0, 1, 2), acc={1: (1,), 2: (1,)}, gdefs={1: (half, k_spec), 2: (half, k_spec)})

    @jax.custom_vjp
    def op(q, kv):
        return fwd_call(q, kv, kv)[0]

    def op_b(res, ct):
        q, kv = res
        dq, dk, dv = bwd_call(q, kv, kv, ct)
        return dq, jnp.concatenate([dk, dv], axis=1)

    op.defvjp(lambda q, kv: (fwd_call(q, kv, kv)[0], (q, kv)), op_b)
    return op(q, kv)


def _hi(a, b, form="nn"):
    dims = {"nn": ((1,), (0,)), "nt": ((1,), (1,)), "tn": ((0,), (0,))}[form]
    return lax.dot_general(a, b, (dims, ((), ())), precision=lax.Precision.HIGH, preferred_element_type=f32)


def _running_sum(g):
    def raw(x, form):
        c = x.shape[0]
        tri = (lax.broadcasted_iota(jnp.int32, (c, c), 0) >= lax.broadcasted_iota(jnp.int32, (c, c), 1)).astype(bf16)
        hi = x.astype(bf16)
        rest = x - hi.astype(f32)
        mid = rest.astype(bf16)
        low = (rest - mid.astype(f32)).astype(bf16)
        dims = (((1,) if form == "nn" else (0,), (0,)), ((), ()))
        return sum(lax.dot_general(tri, part, dims, preferred_element_type=f32) for part in (hi, mid, low))

    @jax.custom_vjp
    def run(x):
        return raw(x, "nn")

    run.defvjp(lambda x: (raw(x, "nn"), None), lambda _, ct: (raw(ct, "tn"),))
    return run(g)


def _unit_lower_inverse(a):
    c = a.shape[0]
    eye = (lax.broadcasted_iota(jnp.int32, (c, c), 0) == lax.broadcasted_iota(jnp.int32, (c, c), 1)).astype(f32)
    inv, power = eye - a, _hi(a, a)
    for _ in range(c.bit_length() - 2):
        both = _hi(jnp.concatenate([inv, power], axis=0), power)
        inv, power = inv + both[:c], both[c:]
    return inv


def _known_inverse(a, t):
    @jax.custom_vjp
    def inv(a, t):
        return t

    def inv_b(t, ct):
        return -_hi(_hi(t, ct, "tn"), t, "nt"), jnp.zeros_like(t)

    inv.defvjp(lambda a, t: (t, t), inv_b)
    return inv(a, t)


def _delta_chunk(q, k, v, g, beta, s0, known_inv=None):
    c = q.shape[0]
    i = lax.broadcasted_iota(jnp.int32, (c, c), 0)
    j = lax.broadcasted_iota(jnp.int32, (c, c), 1)
    causal, strict = i >= j, i > j
    dec = _running_sum(g)
    dec_i = dec[:, :c]
    rel = jnp.exp(jnp.where(causal, dec_i - dec_i.T, -jnp.inf))
    k_beta = k * beta
    on_k = _mxu(jnp.concatenate([k_beta, q], axis=0), k, "nt")
    a = jnp.where(strict, on_k[:c] * rel, 0.0)
    attn = jnp.where(causal, on_k[c:] * rel, 0.0)
    inv = _unit_lower_inverse(a) if known_inv is None else _known_inverse(a, known_inv)
    e_dec = jnp.exp(dec)
    solved = _hi(inv, jnp.concatenate([v * beta, k_beta * e_dec], axis=1))
    u, w = solved[:, :v.shape[1]], solved[:, v.shape[1]:]
    total = jnp.sum(g, axis=0, keepdims=True)
    on_state = _mxu(jnp.concatenate([w, q * e_dec], axis=0), s0, "nn")
    v_new = u - on_state[:c]
    o = on_state[c:] + _mxu(attn, v_new, "nn")
    s1 = s0 * jnp.exp(total) + _mxu(k * jnp.exp(total - dec), v_new, "tn")
    return o, s1, inv


def _delta_rule_call(name, walk, n, in_specs, out_specs, out_shape, operands, exchange):
    n_in, n_out = len(in_specs), len(out_specs)
    carried = len(exchange.operands) if exchange else 0

    def body(*refs):
        ins, refs = refs[:n_in], refs[n_in:]
        x_refs, refs = refs[:carried], refs[carried:]
        outs, refs = refs[:n_out], refs[n_out:]
        land_refs, (state, *sems) = refs[:carried], refs[carried:]
        step = pl.program_id(0)
        if exchange:
            start, finish = exchange.bind(x_refs, land_refs, sems)
            pl.when(step == 0)(start)

        @pl.when(step == 0)
        def _():
            state[...] = jnp.zeros_like(state)

        walk(ins, outs, state)
        if exchange:
            pl.when(step == n - 1)(finish)

    return pl.pallas_call(
        body, name=name, grid=(n,), in_specs=list(in_specs) + [ANY] * carried, out_specs=list(out_specs) + [ANY] * carried,
        out_shape=list(out_shape) + (exchange.out_shapes if exchange else []),
        scratch_shapes=[pltpu.VMEM((GDN_HEAD_DIM, GDN_WIDTH), f32)] + (exchange.scratch if exchange else []),
        compiler_params=_params(1))(*operands, *(exchange.operands if exchange else []))


def _delta_heads():
    heads = [slice(hd * GDN_HEAD_DIM, (hd + 1) * GDN_HEAD_DIM) for hd in range(GDN_HEADS)]
    inv_at = [slice(hd * GDN_CHUNK, (hd + 1) * GDN_CHUNK) for hd in range(GDN_HEADS)]
    return heads, inv_at


def _head_chunk(q, k, v, gates, s0, head, known_inv=None):
    g = jnp.broadcast_to(gates[:, head:head + 1], q.shape)
    beta = jnp.broadcast_to(gates[:, GDN_HEADS + head:GDN_HEADS + head + 1], q.shape)
    return _delta_chunk(q, k, v, g, beta, s0, known_inv)


def delta_rule_fwd(q, k, v, gates, name, exchange=None):
    s, width = q.shape
    c, dk = GDN_CHUNK, GDN_HEAD_DIM
    n = s // c
    heads, inv_at = _delta_heads()

    def walk(ins, outs, state):
        q_ref, k_ref, v_ref, gates_ref = ins
        o_ref, s_in_ref, inv_ref = outs
        s_in_ref[...] = state[...]
        gates = gates_ref[...]
        xs = [[r[:, hd] for r in (q_ref, k_ref, v_ref)] + [gates, state[:, hd], i] for i, hd in enumerate(heads)]
        ys = [_head_chunk(*x) for x in xs]
        for hd, at, (o, s1, inv) in zip(heads, inv_at, ys):
            o_ref[:, hd], state[:, hd], inv_ref[:, at] = o, s1, inv

    blk = pl.BlockSpec((c, width), lambda t: (t, 0))
    gt = pl.BlockSpec((c, LANES), lambda t: (t, 0))
    st = pl.BlockSpec((dk, width), lambda t: (t, 0))
    iv = pl.BlockSpec((c, GDN_HEADS * c), lambda t: (t, 0))
    return _delta_rule_call(name, walk, n, [blk] * 3 + [gt], [blk, st, iv],
                            [_sds((s, width)), _sds((n * dk, width)), _sds((s, GDN_HEADS * c))], (q, k, v, gates), exchange)


def delta_rule_bwd(q, k, v, gates, s_in, inv, do, name, exchange=None):
    s, width = q.shape
    c, dk = GDN_CHUNK, GDN_HEAD_DIM
    n = s // c
    heads, inv_at = _delta_heads()

    def walk(ins, outs, dstate):
        q_ref, k_ref, v_ref, gates_ref, s_ref, inv_ref, do_ref = ins
        dq_ref, dk_ref, dv_ref, dgates_ref = outs
        gates = gates_ref[...]
        xs = [[r[:, hd] for r in (q_ref, k_ref, v_ref)] + [gates, s_ref[:, hd]] for hd in heads]
        known = [inv_ref[:, at] for at in inv_at]
        cts = [(do_ref[:, hd], dstate[:, hd]) for hd in heads]
        grads = []
        for i, (x, t, ct) in enumerate(zip(xs, known, cts)):
            _, vjp = jax.vjp(lambda *y, t=t, i=i: _head_chunk(*y, i, known_inv=t)[:2], *x)
            grads.append(vjp(ct))
        dgates = grads[0][3]
        for g in grads[1:]:
            dgates = dgates + g[3]
        dgates_ref[...] = dgates
        for hd, (dq, dk_, dv, _, ds0) in zip(heads, grads):
            dq_ref[:, hd], dk_ref[:, hd], dv_ref[:, hd], dstate[:, hd] = dq, dk_, dv, ds0

    blk = pl.BlockSpec((c, width), lambda t: (n - 1 - t, 0))
    gt = pl.BlockSpec((c, LANES), lambda t: (n - 1 - t, 0))
    st = pl.BlockSpec((dk, width), lambda t: (n - 1 - t, 0))
    iv = pl.BlockSpec((c, GDN_HEADS * c), lambda t: (n - 1 - t, 0))
    return _delta_rule_call(name, walk, n, [blk] * 3 + [gt, st, iv, blk], [blk] * 3 + [gt],
                            [_sds((s, width))] * 3 + [_sds((s, LANES))], (q, k, v, gates, s_in, inv, do), exchange)


def adamw(w, g, m, v, name):
    shape = w.shape
    if len(shape) == 2:
        grid, spec = (1,), pl.BlockSpec(shape, lambda i: (0, 0))
    else:
        tile = shape[1] if shape[1] <= 512 else _pick(shape[1], (512, 256, 128))
        grid, spec = (shape[0], shape[1] // tile), pl.BlockSpec((None, tile, shape[2]), lambda layer, i: (layer, i, 0))

    def body(w_ref, g_ref, m_ref, v_ref, d_ref, nm_ref, nv_ref):
        grad = g_ref[...]
        nm = ADAM_B1 * m_ref[...] + (1.0 - ADAM_B1) * grad
        nv = ADAM_B2 * v_ref[...] + (1.0 - ADAM_B2) * (grad * grad)
        m_hat = nm / (1.0 - ADAM_B1 ** ADAM_STEP)
        v_hat = nv / (1.0 - ADAM_B2 ** ADAM_STEP)
        d_ref[...] = -ADAM_LR * (m_hat / (jnp.sqrt(v_hat) + ADAM_EPS) + ADAM_WD * w_ref[...])
        nm_ref[...] = nm
        nv_ref[...] = nv

    return tuple(pl.pallas_call(body, name=name, grid=grid, in_specs=[spec] * 4, out_specs=[spec] * 3,
                                out_shape=[_sds(shape)] * 3, compiler_params=_params(len(grid)))(w, g, m, v))


def _place():
    return lax.axis_index("x"), lax.axis_index("y"), lax.axis_index("c")


def _flip(p, bits):
    return tuple(1 - v if (bits >> s) & 1 else v for v, s in zip(p, (2, 1, 0)))


def _slot(p):
    return 4 * p[0] + 2 * p[1] + p[2]


def _chip_of(p):
    return 2 * p[0] + p[1]


ANY = pl.BlockSpec(memory_space=pl.ANY)


class Gather:
    scratch = (pltpu.SemaphoreType.DMA((7,)), pltpu.SemaphoreType.DMA((7,)), pltpu.SemaphoreType.DMA)

    def __init__(self, shard):
        self.operand = shard
        self.out_shape = jax.ShapeDtypeStruct((N_DEV,) + shard.shape, shard.dtype)

    def bind(self, x_ref, out_ref, send_sems, recv_sems, local_sem):
        me = _place()
        sibling = _flip(me, 1)
        chips = [_flip(me, 4), _flip(me, 2), _flip(me, 6)]

        def copy(k, block, to, src=None):
            return pltpu.make_async_remote_copy(
                src_ref=out_ref.at[_slot(block)] if src is None else src, dst_ref=out_ref.at[_slot(block)],
                send_sem=send_sems.at[k], recv_sem=recv_sems.at[k], device_id=to, device_id_type=MESH)

        mine = pltpu.make_async_copy(x_ref, out_ref.at[_slot(me)], local_sem)
        first = [copy(0, me, sibling, src=x_ref)] + [copy(1 + j, me, chip, src=x_ref) for j, chip in enumerate(chips)]
        passed = [copy(4 + j, chip, sibling) for j, chip in enumerate(chips)]

        def start():
            mine.start()
            for cp in first:
                cp.start()

        def finish():
            for j, chip in enumerate(chips):
                copy(1 + j, chip, me).wait_recv()
                passed[j].start()
            copy(0, sibling, me).wait_recv()
            for j, chip in enumerate(chips):
                copy(4 + j, _flip(chip, 1), me).wait_recv()
            for cp in first + passed:
                cp.wait_send()
            mine.wait()

        return start, finish


class ChipExchange:
    scratch = (pltpu.SemaphoreType.DMA((3,)), pltpu.SemaphoreType.DMA((3,)), pltpu.SemaphoreType.DMA)

    def __init__(self, blocks):
        self.operand = blocks
        self.out_shape = jax.ShapeDtypeStruct(blocks.shape, blocks.dtype)

    def bind(self, x_ref, out_ref, send_sems, recv_sems, local_sem):
        me = _place()
        peers = [_flip(me, 4), _flip(me, 2), _flip(me, 6)]
        mine = pltpu.make_async_copy(x_ref.at[_chip_of(me)], out_ref.at[_chip_of(me)], local_sem)

        def copy(j, src_chip, dst_chip):
            return pltpu.make_async_remote_copy(
                src_ref=x_ref.at[src_chip], dst_ref=out_ref.at[dst_chip], send_sem=send_sems.at[j],
                recv_sem=recv_sems.at[j], device_id=peers[j], device_id_type=MESH)

        sends = [copy(j, _chip_of(peer), _chip_of(me)) for j, peer in enumerate(peers)]

        def start():
            mine.start()
            for cp in sends:
                cp.start()

        def finish():
            for j, peer in enumerate(peers):
                copy(j, _chip_of(me), _chip_of(peer)).wait_recv()
            for cp in sends:
                cp.wait_send()
            mine.wait()

        return start, finish


class Together:
    def __init__(self, *parts):
        self.parts = parts
        self.operands = [p.operand for p in parts]
        self.out_shapes = [p.out_shape for p in parts]
        self.scratch = [s for p in parts for s in p.scratch]

    def bind(self, x_refs, out_refs, sems):
        bound, at = [], 0
        for p, x_ref, out_ref in zip(self.parts, x_refs, out_refs):
            bound.append(p.bind(x_ref, out_ref, *sems[at:at + len(p.scratch)]))
            at += len(p.scratch)

        def start():
            for s, _ in bound:
                s()

        def finish():
            for _, f in bound:
                f()

        return start, finish


def exchange_alone(exchange, name):
    n = len(exchange.operands)

    def body(*refs):
        start, finish = exchange.bind(refs[:n], refs[n:2 * n], refs[2 * n:])
        start()
        finish()

    return pl.pallas_call(body, name=name, out_shape=exchange.out_shapes, in_specs=[ANY] * n, out_specs=[ANY] * n,
                          scratch_shapes=exchange.scratch)(*exchange.operands)


def _row_tile(rows):
    return max([t for t in range(16, min(rows, 1024) + 1, 16) if rows % t == 0] or [rows])


def pair_exchange(blocks, name):
    n = len(blocks)

    def body(*refs):
        x_refs, theirs_refs, (send_sems, recv_sems) = refs[:n], refs[n:2 * n], refs[2 * n:]
        me = _place()
        remote = [pltpu.make_async_remote_copy(
            src_ref=x_refs[t].at[2 * q + 1 - me[2]], dst_ref=theirs_refs[t].at[q], send_sem=send_sems.at[4 * t + q],
            recv_sem=recv_sems.at[4 * t + q], device_id=_flip(me, 1), device_id_type=MESH) for t in range(n) for q in range(4)]
        for cp in remote:
            cp.start()
        for cp in remote:
            cp.wait()

    return pl.pallas_call(
        body, name=name, out_shape=[jax.ShapeDtypeStruct((4,) + b.shape[1:], b.dtype) for b in blocks], in_specs=[ANY] * n,
        out_specs=[ANY] * n, scratch_shapes=[pltpu.SemaphoreType.DMA((4 * n,)), pltpu.SemaphoreType.DMA((4 * n,))])(*blocks)


def pair_add(blocks, theirs, name):
    n, rows, width = theirs.shape
    tile = _row_tile(rows)
    spec = pl.BlockSpec((None, tile, width), lambda q, i: (q, i, 0))
    south = pl.BlockSpec((None, None, tile, width), lambda q, i: (q, 0, i, 0))
    north = pl.BlockSpec((None, None, tile, width), lambda q, i: (q, 1, i, 0))

    def body(s_ref, n_ref, b_ref, o_ref):
        mine = jnp.where(lax.axis_index("c") == 0, s_ref[...], n_ref[...])
        o_ref[...] = (mine.astype(f32) + b_ref[...].astype(f32)).astype(o_ref.dtype)

    by_core = blocks.reshape(n, 2, rows, width)
    return pl.pallas_call(body, name=name, grid=(n, rows // tile), in_specs=[south, north, spec], out_specs=spec,
                          out_shape=jax.ShapeDtypeStruct(theirs.shape, theirs.dtype), compiler_params=_params(2))(by_core, by_core, theirs)


def sum_slots(blocks, name):
    n, rows, width = blocks.shape
    tile = _row_tile(rows)

    def body(x_ref, o_ref):
        total = x_ref[0].astype(f32)
        for s in range(1, n):
            total = total + x_ref[s].astype(f32)
        o_ref[...] = total

    return pl.pallas_call(
        body, name=name, grid=(rows // tile,), in_specs=[pl.BlockSpec((n, tile, width), lambda i: (0, i, 0))],
        out_specs=pl.BlockSpec((tile, width), lambda i: (i, 0)), out_shape=_sds((rows, width)), compiler_params=_params(1))(blocks)


def all_reduce_small(x, name):
    rows, width = x.shape

    def body(x_ref, o_ref, land, send_sems, recv_sems):
        me = _place()
        copies = []
        for k in range(1, N_DEV):
            peer = _flip(me, k)
            copies.append(pltpu.make_async_remote_copy(
                src_ref=x_ref, dst_ref=land.at[_slot(me)], send_sem=send_sems.at[k - 1], recv_sem=recv_sems.at[k - 1],
                device_id=peer, device_id_type=MESH))
        for cp in copies:
            cp.start()
        land[_slot(me)] = x_ref[...]
        for k in range(1, N_DEV):
            peer = _flip(me, k)
            pltpu.make_async_remote_copy(
                src_ref=x_ref, dst_ref=land.at[_slot(peer)], send_sem=send_sems.at[k - 1], recv_sem=recv_sems.at[k - 1],
                device_id=peer, device_id_type=MESH).wait_recv()
        total = land[0]
        for s in range(1, N_DEV):
            total = total + land[s]
        o_ref[...] = total
        for cp in copies:
            cp.wait_send()

    return pl.pallas_call(
        body, name=name, out_shape=_sds((rows, width)), in_specs=[pl.BlockSpec(memory_space=pltpu.VMEM)],
        out_specs=pl.BlockSpec(memory_space=pltpu.VMEM),
        scratch_shapes=[pltpu.VMEM((N_DEV, rows, width), f32), pltpu.SemaphoreType.DMA((7,)), pltpu.SemaphoreType.DMA((7,))],
    )(x)


def _pack_big(shards):
    packed = {name: shards[name].astype(bf16) for name in COL_SHARDED}
    packed["rows"] = jnp.concatenate([shards[name].astype(bf16) for name, _ in ROW_SHARDED], axis=1)
    return packed


def _unpack_gathered(gathered):
    full = {}
    for name, part in gathered.items():
        if name in COL_SHARDED:
            full[name] = part.transpose(1, 0, 2).reshape(D_MODEL, N_DEV * part.shape[2])
        else:
            at = 0
            for weight, rows in ROW_SHARDED:
                full[weight] = part[:, at:at + rows, :].reshape(N_DEV * rows, D_MODEL)
                at += rows
    if "w_gate_up" in full:
        full["w_gate_up"] = _interleave_gate_up(full["w_gate_up"])
    if "w_in" in full:
        w_in = full.pop("w_in")
        full["w_main"] = jnp.concatenate([w_in[:, :AB_AT], w_in[:, AB_AT + 2 * GDN_HEADS:]], axis=1)
        full["w_ab"] = jnp.pad(w_in[:, AB_AT:AB_AT + 2 * GDN_HEADS], ((0, 0), (0, LANES - 2 * GDN_HEADS)))
    return full


def _pack_grads(grads, group):
    packed = {}
    for name in group:
        if name == "w_in":
            main, ab = grads["w_main"], grads["w_ab"]
            g = jnp.concatenate([main[:, :AB_AT], ab[:, :2 * GDN_HEADS], main[:, AB_AT:]], axis=1)
        elif name == "w_gate_up":
            g = _interleave_gate_up(grads[name], undo=True)
        elif name == "rows":
            packed[name] = jnp.concatenate([grads[weight].reshape(N_DEV, rows, D_MODEL) for weight, rows in ROW_SHARDED], axis=1)
            continue
        else:
            g = grads[name]
        packed[name] = g.reshape(D_MODEL, N_DEV, g.shape[1] // N_DEV).transpose(1, 0, 2)
    return packed


def _unpack_shard(layers):
    out = {name: jnp.stack([layer[name] for layer in layers]) for name in COL_SHARDED}
    rows_pack, at = jnp.stack([layer["rows"] for layer in layers]), 0
    for weight, rows in ROW_SHARDED:
        out[weight] = rows_pack[:, at:at + rows, :]
        at += rows
    return out


def _rows_of(flat_len):
    return -(-flat_len // (8 * D_MODEL)) * 8


def _pack_small(parts):
    flat = jnp.concatenate([p.reshape(-1) for p in parts])
    rows = _rows_of(flat.shape[0])
    flat = jnp.pad(flat, (0, rows * D_MODEL - flat.shape[0]))
    return flat.reshape(rows, D_MODEL)


def _unpack_small(packed, like):
    flat, out, at = packed.reshape(-1), [], 0
    for p in like:
        out.append(flat[at:at + p.size].reshape(p.shape))
        at += p.size
    return out


def _rope_tables(positions):
    inv_freq = jnp.float32(ROPE_THETA) ** (-jnp.arange(0, ROPE_DIM, 2, dtype=f32) / ROPE_DIM)
    ang = positions.astype(f32)[:, None] * inv_freq
    cos, sin = jnp.cos(ang), jnp.sin(ang)
    rest = ATTN_HEAD_DIM - ROPE_DIM
    cos_h = jnp.concatenate([cos, cos, jnp.ones((cos.shape[0], rest), f32)], axis=1)
    sin_h = jnp.concatenate([-sin, sin, jnp.zeros((sin.shape[0], rest), f32)], axis=1)
    return jnp.tile(cos_h, (1, ATTN_HEADS)), jnp.tile(sin_h, (1, ATTN_HEADS))


HEAD_SMALL = ("norm_mix_pre", "conv_short", "conv_gdn", "gdn_a_log", "gdn_dt_bias")


def _layer_head(h, p, cos_t, sin_t):
    hn = rms_norm(h, p["norm_mix_pre"][None], "norm_mix_pre")
    proj = _linear(hn, p["w_main"], "w_main")
    ab = _linear(hn, p["w_ab"], "w_ab")
    aw, cw, gw = ATTN_WIDTH, CONV_WIDTH, GDN_WIDTH
    aq, ak, av, cb, cc, cx, gqkv, gate = _split_cols(proj, (aw, aw, aw, cw, cw, cw, 3 * gw, gw))
    y_attn = dilated_attention(rope(aq, cos_t, sin_t, ATTN_HEAD_DIM ** -0.5, "rope_q"), rope(ak, cos_t, sin_t, 1.0, "rope_k"),
                               av, "attn")
    y_conv = short_conv(cb, cc, cx, p["conv_short"], "short_conv")
    qkv = gdn_pre(gqkv, p["conv_gdn"], "gdn_pre")
    pv = jnp.zeros((8, LANES), f32).at[0, :GDN_HEADS].set(p["gdn_a_log"]).at[1, :GDN_HEADS].set(p["gdn_dt_bias"])
    return (*_split_cols(qkv, (gw, gw, gw)), gate_beta(ab, pv, "gate_beta")), (gate, y_attn, y_conv)


MID_PARAMS = ("gdn_norm", "w_out", "norm_mix_post", "norm_xattn_pre", "w_xq", "norm_mem", "w_xkv", "w_xo", "norm_xattn_post",
              "norm_ffn_pre")


def _layer_mid(h, o, gate, y_attn, y_conv, p, mem):
    y_gdn = gdn_post(o, gate, p["gdn_norm"][None], "gdn_post")
    mix = _linear(jnp.concatenate([y_attn, y_conv, y_gdn], axis=1), p["w_out"], "w_out")
    h, hn = add_norm_then_norm(h, mix, p["norm_mix_post"][None], p["norm_xattn_pre"][None], "norm_mix_xattn")
    qx = _linear(hn, p["w_xq"], "w_xq")
    kv = _linear(rms_norm(mem, p["norm_mem"][None], "norm_mem"), p["w_xkv"], "w_xkv")
    xa = _linear(cross_attention(qx, kv, "xattn"), p["w_xo"], "w_xo")
    return add_norm_then_norm(h, xa, p["norm_xattn_post"][None], p["norm_ffn_pre"][None], "norm_xattn_ffn")


def _pair_summed(grads, group, name):
    blocks = _pack_grads(grads, group)
    theirs = pair_exchange([blocks[n] for n in group], name + "_pair_exchange")
    return [pair_add(blocks[n], t, f"{name}_pair_add_{n}") for n, t in zip(group, theirs)]


def _forward_backward(x, packed, small, mem, cos_t, sin_t, target):
    def gathers(group, layer):
        return Together(*[Gather(packed[n][layer]) for n in group])

    h = x
    head_gathered = exchange_alone(gathers(HEAD_GROUP, 0), "gather_first")
    saved = []
    for layer in range(DEPTH):
        at_layer = {n: t[layer] for n, t in small.items()}
        head_p = {**_unpack_gathered(dict(zip(HEAD_GROUP, head_gathered))), **{n: at_layer[n] for n in HEAD_SMALL}}
        (rule_in, rest), head_vjp = jax.vjp(lambda h, hp: _layer_head(h, hp, cos_t, sin_t), h, head_p)
        o, s_in, inv, *landed = delta_rule_fwd(*rule_in, "delta_rule_fwd", gathers(TAIL_GROUP, layer))
        tail_p = {**_unpack_gathered(dict(zip(TAIL_GROUP, landed))), **at_layer}
        mid_p = {n: tail_p[n] for n in MID_PARAMS}
        (h, hn), mid_vjp = jax.vjp(lambda h, o, rest, mp: _layer_mid(h, o, *rest, mp, mem), h, o, rest, mid_p)
        y, ffn_saved, head_gathered = ffn_forward(hn, tail_p["w_gate_up"], tail_p["w_down"], "ffn",
                                                  gathers(HEAD_GROUP, layer + 1) if layer + 1 < DEPTH else None)
        h, last_vjp = jax.vjp(lambda h, y, w: add_norm(h, y, w[None], "norm_ffn_post"), h, y, tail_p["norm_ffn_post"])
        saved.append((head_vjp, mid_vjp, last_vjp, ffn_saved, rule_in, s_in, inv))

    loss, dh = jax.value_and_grad(lambda y: loss_rows(y, target, "loss"))(h)

    def summed(group, landed):
        return {n: sum_slots(t, "sum_grads_" + n) for n, t in zip(group, landed)}

    big_grads, small_grads, head_pending = [{} for _ in range(DEPTH)], [None] * DEPTH, []
    for layer in reversed(range(DEPTH)):
        head_vjp, mid_vjp, last_vjp, ffn_saved, rule_in, s_in, inv = saved[layer]
        dh, dy, d_norm_ffn_post = last_vjp(dh)
        dhn, d_gate_up, d_down, landed = ffn_backward(
            ffn_saved, dy, "ffn", Together(*[ChipExchange(t) for t in head_pending]) if head_pending else None)
        if head_pending:
            big_grads[layer + 1].update(summed(HEAD_GROUP, landed))
        dh_mid, do, d_rest, d_mid_p = mid_vjp((dh, dhn))
        d_tail_p = {**d_mid_p, "w_gate_up": d_gate_up, "w_down": d_down, "norm_ffn_post": d_norm_ffn_post}
        carried = Together(*[ChipExchange(t) for t in _pair_summed(d_tail_p, TAIL_GROUP, "tail")])
        *d_rule_in, = delta_rule_bwd(*rule_in, s_in, inv, do, "delta_rule_bwd", carried)
        big_grads[layer].update(summed(TAIL_GROUP, d_rule_in[4:]))
        dh_head, d_head_p = head_vjp((tuple(d_rule_in[:4]), d_rest))
        dh = dh_mid + dh_head
        small_grads[layer] = {n: t for n, t in {**d_head_p, **d_tail_p}.items() if n in small}
        head_pending = _pair_summed(d_head_p, HEAD_GROUP, "head")
    landed = exchange_alone(Together(*[ChipExchange(t) for t in head_pending]), "exchange_last")
    big_grads[0].update(summed(HEAD_GROUP, landed))
    return loss, dh, big_grads, small_grads


def kernel(x, mem, positions, norm_mix_pre, norm_mix_post, w_in, conv_short, conv_gdn, gdn_a_log, gdn_dt_bias, gdn_norm, w_out, norm_mem, norm_xattn_pre, norm_xattn_post, w_xq, w_xkv, w_xo, norm_ffn_pre, norm_ffn_post, w_gate_up, w_down, loss_target, m_norm_mix_pre, m_norm_mix_post, m_w_in, m_conv_short, m_conv_gdn, m_gdn_a_log, m_gdn_dt_bias, m_gdn_norm, m_w_out, m_norm_mem, m_norm_xattn_pre, m_norm_xattn_post, m_w_xq, m_w_xkv, m_w_xo, m_norm_ffn_pre, m_norm_ffn_post, m_w_gate_up, m_w_down, v_norm_mix_pre, v_norm_mix_post, v_w_in, v_conv_short, v_conv_gdn, v_gdn_a_log, v_gdn_dt_bias, v_gdn_norm, v_w_out, v_norm_mem, v_norm_xattn_pre, v_norm_xattn_post, v_w_xq, v_w_xkv, v_w_xo, v_norm_ffn_pre, v_norm_ffn_post, v_w_gate_up, v_w_down):
    given = dict(locals())
    weights = {n: given[n] for n in WEIGHTS}
    me = _slot(_place())

    def in_place(shard):
        full = jnp.zeros(shard.shape[:-1] + (shard.shape[-1] * N_DEV,), f32)
        return lax.dynamic_update_slice_in_dim(full, shard, me * shard.shape[-1], axis=shard.ndim - 1)

    placed = [in_place(conv_short), in_place(conv_gdn)]
    conv_short_full, conv_gdn_full = _unpack_small(all_reduce_small(_pack_small(placed), "gather_conv"), placed)
    small = {n: weights[n] for n in NORMS + ("gdn_a_log", "gdn_dt_bias", "gdn_norm")}
    small["conv_short"], small["conv_gdn"] = conv_short_full, conv_gdn_full

    cos_t, sin_t = _rope_tables(positions[0])
    loss, grad_x, big_layers, small_layers = _forward_backward(
        x[0], _pack_big(weights), small, mem[0], cos_t, sin_t, loss_target[0])
    grads = _unpack_shard(big_layers)

    names = sorted(small)
    parts = [jnp.stack([layer[n] for layer in small_layers]) for n in names] + [loss.reshape(1)]
    reduced = _unpack_small(all_reduce_small(_pack_small(parts), "reduce_small"), parts)
    loss = reduced[-1][0]
    for n, g in zip(names, reduced[:-1]):
        if n in ("conv_short", "conv_gdn"):
            width = weights[n].shape[-1]
            g = lax.dynamic_slice_in_dim(g, me * width, width, axis=g.ndim - 1)
        grads[n] = g

    delta, new_m, new_v = {}, {}, {}
    for n in WEIGHTS:
        delta[n], new_m[n], new_v[n] = adamw(weights[n], grads[n], given["m_" + n], given["v_" + n], "adamw_" + n)
    return (loss, grad_x[None], *[grads[n] for n in WEIGHTS], *[delta[n] for n in WEIGHTS],
            *[new_m[n] for n in WEIGHTS], *[new_v[n] for n in WEIGHTS])
```

```python
import functools

import jax
import jax.numpy as jnp
from jax import lax
from jax.experimental import pallas as pl
from jax.experimental.pallas import tpu as pltpu

f32 = jnp.float32
bf16 = jnp.bfloat16
HIGHEST = lax.Precision.HIGHEST
MESH = pl.DeviceIdType.MESH

N_DEV = 8
DEPTH = 4
D_MODEL = 1024
EPS = 1e-6
ATTN_HEADS, ATTN_HEAD_DIM = 4, 64
ATTN_WIDTH = ATTN_HEADS * ATTN_HEAD_DIM
DILATIONS = (1, 4, 16)
QB = 128
ROPE_THETA = 500000.0
ROPE_DIM = ATTN_HEAD_DIM // 4
CONV_WIDTH = 256
GDN_HEADS, GDN_HEAD_DIM = 4, 128
GDN_WIDTH = GDN_HEADS * GDN_HEAD_DIM
GDN_CHUNK = 64
XATTN_HEADS, XATTN_HEAD_DIM = 4, 256
FFN_HIDDEN = 2816
IN_WIDTH = 3592
AB_AT = 3 * ATTN_WIDTH + 3 * CONV_WIDTH + 3 * GDN_WIDTH
MAIN_WIDTH = IN_WIDTH - 2 * GDN_HEADS
LANES = 128
ROW_TILE = 256
VMEM_LIMIT = 56 * 1024 * 1024

ADAM_LR, ADAM_B1, ADAM_B2, ADAM_EPS, ADAM_WD, ADAM_STEP = 0.001, 0.9, 0.999, 1e-08, 0.01, 10

COL_SHARDED = ("w_in", "w_xkv", "w_gate_up")
ROW_SHARDED = (("w_out", 128), ("w_xq", 128), ("w_xo", 128), ("w_down", 352))
HEAD_GROUP = ("w_in",)
TAIL_GROUP = ("w_gate_up", "w_xkv", "rows")
NORMS = ("norm_mix_pre", "norm_mix_post", "norm_mem", "norm_xattn_pre", "norm_xattn_post", "norm_ffn_pre", "norm_ffn_post")
WEIGHTS = ("norm_mix_pre", "norm_mix_post", "w_in", "conv_short", "conv_gdn", "gdn_a_log", "gdn_dt_bias", "gdn_norm", "w_out",
           "norm_mem", "norm_xattn_pre", "norm_xattn_post", "w_xq", "w_xkv", "w_xo", "norm_ffn_pre", "norm_ffn_post",
           "w_gate_up", "w_down")


def _params(n_grid):
    return pltpu.CompilerParams(dimension_semantics=("arbitrary",) * n_grid, vmem_limit_bytes=VMEM_LIMIT)


def _pick(n, cands):
    for c in cands:
        if n % c == 0:
            return c
    return n


MXU_FLOPS = 9.0e14
HBM_BYTES_PER_S = 2.5e12
VMEM_RMW_BYTES_PER_S = 7.0e12
STEP_S = 0.4e-6
MATMUL_VMEM = 44 * 1024 * 1024


def _tiles(m, n, k, sa, sb, so, tn=None):
    def divisors(d):
        return sorted({d // s for s in range(1, d // LANES + 1) if d % s == 0 and (d // s) % LANES == 0}, reverse=True)

    best = None
    for tk in divisors(k):
        nk = k // tk
        for tm in divisors(m):
            for tn_ in [tn] if tn else divisors(n):
                per_step = tm * tk * sa + tk * tn_ * sb + tm * tn_ * so
                vmem = 2 * per_step + (tm * tn_ * 4 if nk > 1 else 0)
                vmem += (tm * tk * 2 if sa == 4 else 0) + (tk * tn_ * 2 if sb == 4 else 0) + tm * tn_ * 4
                if vmem > MATMUL_VMEM:
                    continue
                moved = m * k * sa * (1 if nk == 1 else n // tn_) + k * n * sb * (1 if nk == 1 and n == tn_ else m // tm) + m * n * so
                busy = 2 * m * n * k / MXU_FLOPS + (m * n * 8 * nk / VMEM_RMW_BYTES_PER_S if nk > 1 else 0)
                cost = max(moved / HBM_BYTES_PER_S, busy) + per_step / HBM_BYTES_PER_S + (m // tm) * (n // tn_) * nk * STEP_S
                if best is None or cost < best[0]:
                    best = (cost, tm, tn_, tk)
    return best[1:]


def _mm(a, b, ta, tb, out_dtype, name, finish=None, exchange=None):
    m, k = (a.shape[1], a.shape[0]) if ta else a.shape
    n = b.shape[0] if tb else b.shape[1]
    tm, tn, tk = _tiles(m, n, k, a.dtype.itemsize, b.dtype.itemsize, jnp.dtype(out_dtype).itemsize, finish and finish[0])
    nk = k // tk
    if finish:
        assert nk == 1
        _, extra, results, function = finish
        dims = (((0 if ta else 1,), (1 if tb else 0,)), ((), ()))
        carried = len(exchange.operands) if exchange else 0
        steps = (m // tm, n // tn)

        def finish_body(a_ref, b_ref, *refs):
            extra_refs, refs = refs[:len(extra)], refs[len(extra):]
            x_refs, refs = refs[:carried], refs[carried:]
            out_refs, refs = refs[:len(results)], refs[len(results):]
            if exchange:
                at = pl.program_id(0) * steps[1] + pl.program_id(1)
                start, wait = exchange.bind(x_refs, refs[:carried], refs[carried:])
                pl.when(at == 0)(start)
            p = lax.dot_general(a_ref[...].astype(bf16), b_ref[...].astype(bf16), dims, preferred_element_type=f32)
            for r, o in zip(out_refs, function(p, *[r[...] for r in extra_refs])):
                r[...] = o.astype(r.dtype)
            if exchange:
                pl.when(at == steps[0] * steps[1] - 1)(wait)

        return pl.pallas_call(
            finish_body, name=name, grid=steps,
            in_specs=[pl.BlockSpec((tk, tm), lambda i, j: (0, i)) if ta else pl.BlockSpec((tm, tk), lambda i, j: (i, 0)),
                      pl.BlockSpec((tn, tk), lambda i, j: (j, 0)) if tb else pl.BlockSpec((tk, tn), lambda i, j: (0, j))]
            + [pl.BlockSpec((tm, cols), lambda i, j: (i, j)) for _, cols in extra] + [ANY] * carried,
            out_specs=[pl.BlockSpec((tm, cols), lambda i, j: (i, j)) for _, cols in results] + [ANY] * carried,
            out_shape=[jax.ShapeDtypeStruct((m, n // tn * cols), dt) for dt, cols in results] + (exchange.out_shapes if exchange else []),
            scratch_shapes=exchange.scratch if exchange else [],
            compiler_params=_params(2))(a, b, *[x for x, _ in extra], *(exchange.operands if exchange else []))
    a_spec = pl.BlockSpec((tk, tm), lambda i, j, kk: (kk, i)) if ta else pl.BlockSpec((tm, tk), lambda i, j, kk: (i, kk))
    b_spec = pl.BlockSpec((tn, tk), lambda i, j, kk: (j, kk)) if tb else pl.BlockSpec((tk, tn), lambda i, j, kk: (kk, j))
    dims = (((0 if ta else 1,), (1 if tb else 0,)), ((), ()))

    def body(a_ref, b_ref, o_ref, *acc):
        kk = pl.program_id(2)
        p = lax.dot_general(a_ref[...].astype(bf16), b_ref[...].astype(bf16), dims, preferred_element_type=f32)
        if nk == 1:
            o_ref[...] = p.astype(o_ref.dtype)
            return
        acc_ref, = acc

        @pl.when(kk == 0)
        def _():
            acc_ref[...] = p

        @pl.when(kk > 0)
        def _():
            acc_ref[...] += p

        @pl.when(kk == nk - 1)
        def _():
            o_ref[...] = acc_ref[...].astype(o_ref.dtype)

    return pl.pallas_call(
        body, name=name, grid=(m // tm, n // tn, nk), in_specs=[a_spec, b_spec],
        out_specs=pl.BlockSpec((tm, tn), lambda i, j, kk: (i, j)), out_shape=jax.ShapeDtypeStruct((m, n), out_dtype),
        scratch_shapes=[pltpu.VMEM((tm, tn), f32)] if nk > 1 else [], compiler_params=_params(3))(a, b)


def _linear(x, w, name):
    @jax.custom_vjp
    def lin(x, w):
        return _mm(x, w, False, False, f32, name + "_y")

    def lin_f(x, w):
        return _mm(x, w, False, False, f32, name + "_y"), (x, w)

    def lin_b(res, dy):
        x, w = res
        return _mm(dy, w, False, True, f32, name + "_dx"), _mm(x, dy, True, False, bf16, name + "_dw")

    lin.defvjp(lin_f, lin_b)
    return lin(x, w)


GATE_UP_TILE = 512


def _interleave_gate_up(w, undo=False):
    two_f = w.shape[1]
    half = GATE_UP_TILE // 2
    nb = two_f // GATE_UP_TILE
    if undo:
        order = [2 * j + side for side in range(2) for j in range(nb)]
    else:
        order = [side * nb + j for j in range(nb) for side in range(2)]
    return jnp.concatenate([w[:, b * half:(b + 1) * half] for b in order], axis=1)


def ffn_forward(hn, w_gate_up, w_down, name):
    half = GATE_UP_TILE // 2

    def act_of(p):
        return p, jax.nn.silu(p[:, :half]) * p[:, half:]

    gate_up, act = _mm(hn, w_gate_up, False, False, bf16, name + "_act",
                       (GATE_UP_TILE, [], [(bf16, GATE_UP_TILE), (bf16, half)], act_of))
    return _mm(act, w_down, False, False, f32, name + "_y"), (hn, w_gate_up, w_down, gate_up, act)


def ffn_backward(saved, dy, name, exchange=None):
    hn, w_gate_up, w_down, gate_up, act = saved
    half = GATE_UP_TILE // 2

    def d_gate_up_of(d_act, gate_up):
        g, u = gate_up[:, :half].astype(f32), gate_up[:, half:].astype(f32)
        sig = jax.nn.sigmoid(g)
        return (jnp.concatenate([d_act * u * sig * (1.0 + g * (1.0 - sig)), d_act * g * sig], axis=1),)

    d_gate_up, *landed = _mm(dy, w_down, False, True, bf16, name + "_dact",
                             (half, [(gate_up, GATE_UP_TILE)], [(bf16, GATE_UP_TILE)], d_gate_up_of), exchange)
    return (_mm(d_gate_up, w_gate_up, False, True, f32, name + "_dx"), _mm(hn, d_gate_up, True, False, bf16, name + "_dw1"),
            _mm(act, dy, True, False, bf16, name + "_dw2"), landed)


def _split_cols(x, widths):
    edges = [sum(widths[:i]) for i in range(len(widths) + 1)]

    def cut(x):
        return tuple(x[:, a:b] for a, b in zip(edges[:-1], edges[1:]))

    @jax.custom_vjp
    def split(x):
        return cut(x)

    split.defvjp(lambda x: (cut(x), None), lambda _, cts: (jnp.concatenate(cts, axis=1),))
    return split(x)


def _block_op(name, f, grid, in_specs, out_defs, arrays, diff, acc=None, gdefs=None):
    acc, gdefs = acc or {}, gdefs or {}
    n_in, n_out, n_grid = len(in_specs), len(out_defs), len(grid)

    def fwd_call(*xs):
        def body(*refs):
            outs = f(*[r[...] for r in refs[:n_in]])
            for r, o in zip(refs[n_in:], outs):
                r[...] = o.astype(r.dtype)

        return pl.pallas_call(
            body, name=name + "_fwd", grid=grid, in_specs=in_specs, out_specs=[d[1] for d in out_defs],
            out_shape=[d[0] for d in out_defs], compiler_params=_params(n_grid))(*xs)

    def bwd_call(*xs_and_cts):
        def body(*refs):
            xs = [r[...] for r in refs[:n_in]]
            cts = tuple(r[...] for r in refs[n_in:n_in + n_out])

            def of_diff(*dx):
                full = list(xs)
                for i, v in zip(diff, dx):
                    full[i] = v
                return tuple(f(*full))

            _, vjp = jax.vjp(of_diff, *[xs[i] for i in diff])
            grads = vjp(cts)
            for i, g, r in zip(diff, grads, refs[n_in + n_out:]):
                if i in acc:
                    first = functools.reduce(jnp.logical_and, [pl.program_id(a) == 0 for a in acc[i]])

                    @pl.when(first)
                    def _(r=r):
                        r[...] = jnp.zeros_like(r)

                    r[...] += g.astype(r.dtype)
                else:
                    r[...] = g.astype(r.dtype)

        g_defs = [gdefs.get(i, (jax.ShapeDtypeStruct(arrays[i].shape, f32), in_specs[i])) for i in diff]
        return pl.pallas_call(
            body, name=name + "_bwd", grid=grid, in_specs=list(in_specs) + [d[1] for d in out_defs],
            out_specs=[d[1] for d in g_defs], out_shape=[d[0] for d in g_defs], compiler_params=_params(n_grid))(*xs_and_cts)

    return fwd_call, bwd_call


def _simple_op(name, f, grid, in_specs, out_defs, arrays, diff, acc=None):
    fwd_call, bwd_call = _block_op(name, f, grid, in_specs, out_defs, arrays, diff, acc)

    @jax.custom_vjp
    def op(*xs):
        return tuple(fwd_call(*xs))

    def op_f(*xs):
        return tuple(fwd_call(*xs)), xs

    def op_b(xs, cts):
        grads = bwd_call(*xs, *cts)
        out = [jnp.zeros_like(x) for x in xs]
        for i, g in zip(diff, grads):
            out[i] = g
        return tuple(out)

    op.defvjp(op_f, op_b)
    return op(*arrays)


def _rows(width, tile=ROW_TILE):
    return pl.BlockSpec((tile, width), lambda i: (i, 0))


def _whole(shape):
    return pl.BlockSpec(shape, lambda *_: (0,) * len(shape))


def _sds(shape):
    return jax.ShapeDtypeStruct(shape, f32)


def _rms(x, w):
    return x * lax.rsqrt(jnp.mean(x * x, axis=-1, keepdims=True) + EPS) * w


def rms_norm(x, w, name):
    r, d = x.shape
    return _simple_op(name, lambda x, w: (_rms(x, w),), (r // ROW_TILE,), [_rows(d), _whole((1, d))],
                      [(_sds((r, d)), _rows(d))], (x, w), (0, 1), {1: (0,)})[0]


def add_norm(h, y, w, name):
    r, d = h.shape
    return _simple_op(name, lambda h, y, w: (h + _rms(y, w),), (r // ROW_TILE,), [_rows(d), _rows(d), _whole((1, d))],
                      [(_sds((r, d)), _rows(d))], (h, y, w), (0, 1, 2), {2: (0,)})[0]


def add_norm_then_norm(h, y, w_post, w_pre, name):
    r, d = h.shape

    def f(h, y, w_post, w_pre):
        h_new = h + _rms(y, w_post)
        return h_new, _rms(h_new, w_pre)

    return _simple_op(name, f, (r // ROW_TILE,), [_rows(d), _rows(d), _whole((1, d)), _whole((1, d))],
                      [(_sds((r, d)), _rows(d))] * 2, (h, y, w_post, w_pre), (0, 1, 2, 3), {2: (0,), 3: (0,)})


def _swap8(x):
    def raw(x):
        lane = lax.broadcasted_iota(jnp.int32, x.shape, 1) % ATTN_HEAD_DIM
        half = ROPE_DIM // 2
        up = pltpu.roll(x, x.shape[1] - half, axis=1)
        down = pltpu.roll(x, half, axis=1)
        return jnp.where(lane < half, up, jnp.where(lane < ROPE_DIM, down, 0.0))

    @jax.custom_vjp
    def swap(x):
        return raw(x)

    swap.defvjp(lambda x: (raw(x), None), lambda _, g: (raw(g),))
    return swap(x)


def rope(x, cos_t, sin_t, scale, name):
    r, d = x.shape
    return _simple_op(name, lambda x, c, s: ((x * c + _swap8(x) * s) * scale,), (r // ROW_TILE,), [_rows(d)] * 3,
                      [(_sds((r, d)), _rows(d))], (x, cos_t, sin_t), (0,))[0]


def _shift_rows(x, k):
    n = x.shape[0]

    def down(x):
        row = lax.broadcasted_iota(jnp.int32, x.shape, 0)
        return jnp.where(row >= k, pltpu.roll(x, k, axis=0), 0.0)

    def up(x):
        row = lax.broadcasted_iota(jnp.int32, x.shape, 0)
        return jnp.where(row < n - k, pltpu.roll(x, n - k, axis=0), 0.0)

    @jax.custom_vjp
    def shift(x):
        return down(x)

    shift.defvjp(lambda x: (down(x), None), lambda _, g: (up(g),))
    return shift(x)


def _causal_conv(x, w):
    taps = w.shape[0]
    y = x * w[taps - 1:taps, :]
    for j in range(taps - 1):
        y = y + _shift_rows(x, taps - 1 - j) * w[j:j + 1, :]
    return y


def _cols(rows, at=0):
    return pl.BlockSpec((rows, LANES), lambda j: (0, at + j))


def short_conv(cb, cc, cx, w, name):
    s, c = cb.shape
    taps = w.shape[0]
    return _simple_op(name, lambda b, c_, x, w: (b * _causal_conv(c_ * x, w),), (c // LANES,),
                      [_cols(s)] * 3 + [_cols(taps)], [(_sds((s, c)), _cols(s))], (cb, cc, cx, w), (0, 1, 2, 3))[0]


def gdn_pre(qkv, w, name):
    s, c = qkv.shape
    taps = w.shape[0]

    def f(x, w):
        j = pl.program_id(0)
        y = jax.nn.silu(_causal_conv(x, w))
        normed = y * lax.rsqrt(jnp.sum(y * y, axis=-1, keepdims=True) + EPS)
        scale = jnp.where(j < GDN_HEADS, GDN_HEAD_DIM ** -0.5, 1.0).astype(f32)
        return (jnp.where(j < 2 * GDN_HEADS, normed * scale, y),)

    return _simple_op(name, f, (c // LANES,), [_cols(s), _cols(taps)], [(_sds((s, c)), _cols(s))], (qkv, w), (0, 1))[0]


def gate_beta(ab, pv, name):
    s = ab.shape[0]

    def f(ab, pv):
        lane = lax.broadcasted_iota(jnp.int32, ab.shape, 1)
        g = -jnp.exp(pv[0:1, :]) * jax.nn.softplus(ab + pv[1:2, :])
        return (jnp.where(lane < GDN_HEADS, g, jnp.where(lane < 2 * GDN_HEADS, jax.nn.sigmoid(ab), 0.0)),)

    return _simple_op(name, f, (s // ROW_TILE,), [_rows(LANES), _whole((8, LANES))], [(_sds((s, LANES)), _rows(LANES))],
                      (ab, pv), (0, 1), {1: (0,)})[0]


def gdn_post(o, gate, w, name):
    s, c = o.shape
    spec = pl.BlockSpec((ROW_TILE, LANES), lambda i, j: (i, j))
    return _simple_op(name, lambda o, g, w: (_rms(o, w) * jax.nn.silu(g),), (s // ROW_TILE, c // LANES),
                      [spec, spec, _whole((1, LANES))], [(_sds((s, c)), spec)], (o, gate, w), (0, 1, 2), {2: (0, 1)})[0]


def attn_merge(outs, lses, name):
    s, c = outs[0].shape

    def f(o1, o2, o3, l1, l2, l3):
        m = lax.stop_gradient(jnp.maximum(jnp.maximum(l1, l2), l3))
        e1, e2, e3 = jnp.exp(l1 - m), jnp.exp(l2 - m), jnp.exp(l3 - m)
        return ((e1 * o1 + e2 * o2 + e3 * o3) / (e1 + e2 + e3),)

    return _simple_op(name, f, (s // ROW_TILE,), [_rows(c)] * 6, [(_sds((s, c)), _rows(c))], (*outs, *lses), tuple(range(6)))[0]


def loss_rows(y, target, name):
    s, d = y.shape
    nt = s // ROW_TILE

    def f(y, t):
        e = y - t
        part = 0.5 * jnp.sum(jnp.mean(e * e, axis=-1, keepdims=True), axis=0, keepdims=True)
        return (jnp.broadcast_to(part * (1.0 / (8 * LANES)), (8, LANES)),)

    out = _simple_op(name, f, (nt,), [_rows(d)] * 2, [(_sds((nt * 8, LANES)), pl.BlockSpec((8, LANES), lambda i: (i, 0)))],
                     (y, target), (0,))[0]
    return jnp.sum(out)


def _mxu(a, b, form):
    dims = {"nn": ((1,), (0,)), "nt": ((1,), (1,)), "tn": ((0,), (0,))}

    def raw(a, b, form):
        return lax.dot_general(a.astype(bf16), b.astype(bf16), (dims[form], ((), ())), preferred_element_type=f32)

    @jax.custom_vjp
    def prod(a, b):
        return raw(a, b, form)

    def prod_b(res, ct):
        a, b = res
        if form == "nn":
            return raw(ct, b, "nt"), raw(a, ct, "tn")
        if form == "nt":
            return raw(ct, b, "nn"), raw(ct, a, "tn")
        return raw(b, ct, "nt"), raw(a, ct, "nn")

    prod.defvjp(lambda a, b: (raw(a, b, form), (a, b)), prod_b)
    return prod(a, b)


def band_attention(q, k, v, nb, name):
    r, qb, width = q.shape
    dh = ATTN_HEAD_DIM

    def f(q, kp, kc, vp, vc):
        has_prev = (pl.program_id(0) % nb) > 0
        i = lax.broadcasted_iota(jnp.int32, (qb, 2 * qb), 0)
        j = lax.broadcasted_iota(jnp.int32, (qb, 2 * qb), 1)
        seen = jnp.logical_or(jnp.logical_and(jnp.logical_and(j < qb, j >= i), has_prev), jnp.logical_and(j >= qb, j - qb <= i))
        keys, values = jnp.concatenate([kp, kc], axis=0), jnp.concatenate([vp, vc], axis=0)
        outs, lses = [], []
        for hd in range(width // dh):
            at = slice(hd * dh, (hd + 1) * dh)
            sc = jnp.where(seen, _mxu(q[:, at], keys[:, at], "nt"), -jnp.inf)
            m = lax.stop_gradient(jnp.max(sc, axis=-1, keepdims=True))
            p = jnp.exp(sc - m)
            l = jnp.sum(p, axis=-1, keepdims=True)
            outs.append(_mxu(p / l, values[:, at], "nn"))
            lses.append(jnp.broadcast_to(m + jnp.log(l), (qb, dh)))
        return jnp.concatenate(outs, axis=1), jnp.concatenate(lses, axis=1)

    blk = (None, qb, width)
    cur = pl.BlockSpec(blk, lambda b: (b, 0, 0))
    prev = pl.BlockSpec(blk, lambda b: (jnp.maximum(b - 1, 0), 0, 0))
    shape = _sds((r, qb, width))
    fwd_call, bwd_call = _block_op(name, f, (r,), [cur, prev, cur, prev, cur], [(shape, cur), (shape, cur)],
                                   (q, k, k, v, v), (0, 1, 2, 3, 4), gdefs={1: (shape, cur), 3: (shape, cur)})

    def to_prev(g):
        return jnp.concatenate([g[1:], jnp.zeros_like(g[:1])], axis=0)

    @jax.custom_vjp
    def op(q, k, v):
        return tuple(fwd_call(q, k, k, v, v))

    def op_b(res, cts):
        q, k, v = res
        dq, dkp, dkc, dvp, dvc = bwd_call(q, k, k, v, v, *cts)
        return dq, dkc + to_prev(dkp), dvc + to_prev(dvp)

    op.defvjp(lambda q, k, v: (tuple(fwd_call(q, k, k, v, v)), (q, k, v)), op_b)
    return op(q, k, v)


def dilated_attention(q, k, v, name):
    s = q.shape[0]
    outs, lses = [], []
    for d in DILATIONS:
        length = s // d
        nb = length // QB

        def to_residue(t):
            return t.reshape(length, d, ATTN_WIDTH).transpose(1, 0, 2).reshape(d * nb, QB, ATTN_WIDTH)

        def from_residue(t):
            return t.reshape(d, length, ATTN_WIDTH).transpose(1, 0, 2).reshape(s, ATTN_WIDTH)

        o, lse = band_attention(to_residue(q), to_residue(k), to_residue(v), nb, f"{name}_d{d}")
        outs.append(from_residue(o))
        lses.append(from_residue(lse))
    return attn_merge(outs, lses, name + "_merge")


def cross_attention(q, kv, name):
    s = q.shape[0]
    m = kv.shape[0]
    width = XATTN_HEADS * XATTN_HEAD_DIM
    tq = 512

    def f(q, k, v):
        sc = _mxu(q, k, "nt") * (XATTN_HEAD_DIM ** -0.5)
        mx = lax.stop_gradient(jnp.max(sc, axis=-1, keepdims=True))
        p = jnp.exp(sc - mx)
        return (_mxu(p / jnp.sum(p, axis=-1, keepdims=True), v, "nn"),)

    q_spec = pl.BlockSpec((tq, XATTN_HEAD_DIM), lambda a, i: (i, a))
    k_spec = pl.BlockSpec((m, XATTN_HEAD_DIM), lambda a, i: (0, a))
    v_spec = pl.BlockSpec((m, XATTN_HEAD_DIM), lambda a, i: (0, a + XATTN_HEADS))
    half = _sds((m, width))
    fwd_call, bwd_call = _block_op(name, f, (XATTN_HEADS, s // tq), [q_spec, k_spec, v_spec], [(_sds((s, width)), q_spec)],
                                   (q, kv, kv), (0, 1, 2), acc={1: (1,), 2: (1,)}, gdefs={1: (half, k_spec), 2: (half, k_spec)})

    @jax.custom_vjp
    def op(q, kv):
        return fwd_call(q, kv, kv)[0]

    def op_b(res, ct):
        q, kv = res
        dq, dk, dv = bwd_call(q, kv, kv, ct)
        return dq, jnp.concatenate([dk, dv], axis=1)

    op.defvjp(lambda q, kv: (fwd_call(q, kv, kv)[0], (q, kv)), op_b)
    return op(q, kv)


def _hi(a, b, form="nn"):
    dims = {"nn": ((1,), (0,)), "nt": ((1,), (1,)), "tn": ((0,), (0,))}[form]
    return lax.dot_general(a, b, (dims, ((), ())), precision=lax.Precision.HIGH, preferred_element_type=f32)


def _running_sum(g):
    def raw(x, form):
        c = x.shape[0]
        tri = (lax.broadcasted_iota(jnp.int32, (c, c), 0) >= lax.broadcasted_iota(jnp.int32, (c, c), 1)).astype(bf16)
        hi = x.astype(bf16)
        rest = x - hi.astype(f32)
        mid = rest.astype(bf16)
        low = (rest - mid.astype(f32)).astype(bf16)
        dims = (((1,) if form == "nn" else (0,), (0,)), ((), ()))
        return sum(lax.dot_general(tri, part, dims, preferred_element_type=f32) for part in (hi, mid, low))

    @jax.custom_vjp
    def run(x):
        return raw(x, "nn")

    run.defvjp(lambda x: (raw(x, "nn"), None), lambda _, ct: (raw(ct, "tn"),))
    return run(g)


def _unit_lower_inverse(a):
    c = a.shape[0]
    eye = (lax.broadcasted_iota(jnp.int32, (c, c), 0) == lax.broadcasted_iota(jnp.int32, (c, c), 1)).astype(f32)
    inv, power = eye - a, -a
    for _ in range(c.bit_length() - 2):
        power = _hi(power, power)
        inv = inv + _hi(inv, power)
    return inv


def _known_inverse(a, t):
    @jax.custom_vjp
    def inv(a, t):
        return t

    def inv_b(t, ct):
        return -_hi(_hi(t, ct, "tn"), t, "nt"), jnp.zeros_like(t)

    inv.defvjp(lambda a, t: (t, t), inv_b)
    return inv(a, t)


def _delta_chunk(q, k, v, g, beta, s0, known_inv=None):
    c = q.shape[0]
    i = lax.broadcasted_iota(jnp.int32, (c, c), 0)
    j = lax.broadcasted_iota(jnp.int32, (c, c), 1)
    causal, strict = i >= j, i > j
    dec = _running_sum(g)
    dec_i = dec[:, :c]
    rel = jnp.exp(jnp.where(causal, dec_i - dec_i.T, -jnp.inf))
    k_beta = k * beta
    on_k = _mxu(jnp.concatenate([k_beta, q], axis=0), k, "nt")
    a = jnp.where(strict, on_k[:c] * rel, 0.0)
    attn = jnp.where(causal, on_k[c:] * rel, 0.0)
    inv = _unit_lower_inverse(a) if known_inv is None else _known_inverse(a, known_inv)
    e_dec = jnp.exp(dec)
    solved = _hi(inv, jnp.concatenate([v * beta, k_beta * e_dec], axis=1))
    u, w = solved[:, :v.shape[1]], solved[:, v.shape[1]:]
    total = jnp.sum(g, axis=0, keepdims=True)
    on_state = _mxu(jnp.concatenate([w, q * e_dec], axis=0), s0, "nn")
    v_new = u - on_state[:c]
    o = on_state[c:] + _mxu(attn, v_new, "nn")
    s1 = s0 * jnp.exp(total) + _mxu(k * jnp.exp(total - dec), v_new, "tn")
    return o, s1, inv


def _delta_rule_call(name, walk, n, in_specs, out_specs, out_shape, operands, exchange):
    n_in, n_out = len(in_specs), len(out_specs)
    carried = len(exchange.operands) if exchange else 0

    def body(*refs):
        ins, refs = refs[:n_in], refs[n_in:]
        x_refs, refs = refs[:carried], refs[carried:]
        outs, refs = refs[:n_out], refs[n_out:]
        land_refs, (state, *sems) = refs[:carried], refs[carried:]
        step = pl.program_id(0)
        if exchange:
            start, finish = exchange.bind(x_refs, land_refs, sems)
            pl.when(step == 0)(start)

        @pl.when(step == 0)
        def _():
            state[...] = jnp.zeros_like(state)

        walk(ins, outs, state)
        if exchange:
            pl.when(step == n - 1)(finish)

    return pl.pallas_call(
        body, name=name, grid=(n,), in_specs=list(in_specs) + [ANY] * carried, out_specs=list(out_specs) + [ANY] * carried,
        out_shape=list(out_shape) + (exchange.out_shapes if exchange else []),
        scratch_shapes=[pltpu.VMEM((GDN_HEAD_DIM, GDN_WIDTH), f32)] + (exchange.scratch if exchange else []),
        compiler_params=_params(1))(*operands, *(exchange.operands if exchange else []))


def _delta_heads():
    heads = [slice(hd * GDN_HEAD_DIM, (hd + 1) * GDN_HEAD_DIM) for hd in range(GDN_HEADS)]
    inv_at = [slice(hd * GDN_CHUNK, (hd + 1) * GDN_CHUNK) for hd in range(GDN_HEADS)]
    return heads, inv_at


def _head_chunk(q, k, v, gates, s0, head, known_inv=None):
    g = jnp.broadcast_to(gates[:, head:head + 1], q.shape)
    beta = jnp.broadcast_to(gates[:, GDN_HEADS + head:GDN_HEADS + head + 1], q.shape)
    return _delta_chunk(q, k, v, g, beta, s0, known_inv)


def delta_rule_fwd(q, k, v, gates, name, exchange=None):
    s, width = q.shape
    c, dk = GDN_CHUNK, GDN_HEAD_DIM
    n = s // c
    heads, inv_at = _delta_heads()

    def walk(ins, outs, state):
        q_ref, k_ref, v_ref, gates_ref = ins
        o_ref, s_in_ref, inv_ref = outs
        s_in_ref[...] = state[...]
        gates = gates_ref[...]
        xs = [[r[:, hd] for r in (q_ref, k_ref, v_ref)] + [gates, state[:, hd], i] for i, hd in enumerate(heads)]
        ys = [_head_chunk(*x) for x in xs]
        for hd, at, (o, s1, inv) in zip(heads, inv_at, ys):
            o_ref[:, hd], state[:, hd], inv_ref[:, at] = o, s1, inv

    blk = pl.BlockSpec((c, width), lambda t: (t, 0))
    gt = pl.BlockSpec((c, LANES), lambda t: (t, 0))
    st = pl.BlockSpec((dk, width), lambda t: (t, 0))
    iv = pl.BlockSpec((c, GDN_HEADS * c), lambda t: (t, 0))
    return _delta_rule_call(name, walk, n, [blk] * 3 + [gt], [blk, st, iv],
                            [_sds((s, width)), _sds((n * dk, width)), _sds((s, GDN_HEADS * c))], (q, k, v, gates), exchange)


def delta_rule_bwd(q, k, v, gates, s_in, inv, do, name, exchange=None):
    s, width = q.shape
    c, dk = GDN_CHUNK, GDN_HEAD_DIM
    n = s // c
    heads, inv_at = _delta_heads()

    def walk(ins, outs, dstate):
        q_ref, k_ref, v_ref, gates_ref, s_ref, inv_ref, do_ref = ins
        dq_ref, dk_ref, dv_ref, dgates_ref = outs
        gates = gates_ref[...]
        xs = [[r[:, hd] for r in (q_ref, k_ref, v_ref)] + [gates, s_ref[:, hd]] for hd in heads]
        known = [inv_ref[:, at] for at in inv_at]
        cts = [(do_ref[:, hd], dstate[:, hd]) for hd in heads]
        grads = []
        for i, (x, t, ct) in enumerate(zip(xs, known, cts)):
            _, vjp = jax.vjp(lambda *y, t=t, i=i: _head_chunk(*y, i, known_inv=t)[:2], *x)
            grads.append(vjp(ct))
        dgates = grads[0][3]
        for g in grads[1:]:
            dgates = dgates + g[3]
        dgates_ref[...] = dgates
        for hd, (dq, dk_, dv, _, ds0) in zip(heads, grads):
            dq_ref[:, hd], dk_ref[:, hd], dv_ref[:, hd], dstate[:, hd] = dq, dk_, dv, ds0

    blk = pl.BlockSpec((c, width), lambda t: (n - 1 - t, 0))
    gt = pl.BlockSpec((c, LANES), lambda t: (n - 1 - t, 0))
    st = pl.BlockSpec((dk, width), lambda t: (n - 1 - t, 0))
    iv = pl.BlockSpec((c, GDN_HEADS * c), lambda t: (n - 1 - t, 0))
    return _delta_rule_call(name, walk, n, [blk] * 3 + [gt, st, iv, blk], [blk] * 3 + [gt],
                            [_sds((s, width))] * 3 + [_sds((s, LANES))], (q, k, v, gates, s_in, inv, do), exchange)


def adamw(w, g, m, v, name):
    shape = w.shape
    if len(shape) == 2:
        grid, spec = (1,), pl.BlockSpec(shape, lambda i: (0, 0))
    else:
        tile = shape[1] if shape[1] <= 512 else _pick(shape[1], (512, 256, 128))
        grid, spec = (shape[0], shape[1] // tile), pl.BlockSpec((None, tile, shape[2]), lambda layer, i: (layer, i, 0))

    def body(w_ref, g_ref, m_ref, v_ref, d_ref, nm_ref, nv_ref):
        grad = g_ref[...]
        nm = ADAM_B1 * m_ref[...] + (1.0 - ADAM_B1) * grad
        nv = ADAM_B2 * v_ref[...] + (1.0 - ADAM_B2) * (grad * grad)
        m_hat = nm / (1.0 - ADAM_B1 ** ADAM_STEP)
        v_hat = nv / (1.0 - ADAM_B2 ** ADAM_STEP)
        d_ref[...] = -ADAM_LR * (m_hat / (jnp.sqrt(v_hat) + ADAM_EPS) + ADAM_WD * w_ref[...])
        nm_ref[...] = nm
        nv_ref[...] = nv

    return tuple(pl.pallas_call(body, name=name, grid=grid, in_specs=[spec] * 4, out_specs=[spec] * 3,
                                out_shape=[_sds(shape)] * 3, compiler_params=_params(len(grid)))(w, g, m, v))


def _place():
    return lax.axis_index("x"), lax.axis_index("y"), lax.axis_index("c")


def _flip(p, bits):
    return tuple(1 - v if (bits >> s) & 1 else v for v, s in zip(p, (2, 1, 0)))


def _slot(p):
    return 4 * p[0] + 2 * p[1] + p[2]


def _chip_of(p):
    return 2 * p[0] + p[1]


ANY = pl.BlockSpec(memory_space=pl.ANY)


class Gather:
    scratch = (pltpu.SemaphoreType.DMA((7,)), pltpu.SemaphoreType.DMA((7,)), pltpu.SemaphoreType.DMA)

    def __init__(self, shard):
        self.operand = shard
        self.out_shape = jax.ShapeDtypeStruct((N_DEV,) + shard.shape, shard.dtype)

    def bind(self, x_ref, out_ref, send_sems, recv_sems, local_sem):
        me = _place()
        sibling = _flip(me, 1)
        chips = [_flip(me, 4), _flip(me, 2), _flip(me, 6)]

        def copy(k, block, to, src=None):
            return pltpu.make_async_remote_copy(
                src_ref=out_ref.at[_slot(block)] if src is None else src, dst_ref=out_ref.at[_slot(block)],
                send_sem=send_sems.at[k], recv_sem=recv_sems.at[k], device_id=to, device_id_type=MESH)

        mine = pltpu.make_async_copy(x_ref, out_ref.at[_slot(me)], local_sem)
        first = [copy(0, me, sibling, src=x_ref)] + [copy(1 + j, me, chip, src=x_ref) for j, chip in enumerate(chips)]
        passed = [copy(4 + j, chip, sibling) for j, chip in enumerate(chips)]

        def start():
            mine.start()
            for cp in first:
                cp.start()

        def finish():
            for j, chip in enumerate(chips):
                copy(1 + j, chip, me).wait_recv()
                passed[j].start()
            copy(0, sibling, me).wait_recv()
            for j, chip in enumerate(chips):
                copy(4 + j, _flip(chip, 1), me).wait_recv()
            for cp in first + passed:
                cp.wait_send()
            mine.wait()

        return start, finish


class ChipExchange:
    scratch = (pltpu.SemaphoreType.DMA((3,)), pltpu.SemaphoreType.DMA((3,)), pltpu.SemaphoreType.DMA)

    def __init__(self, blocks):
        self.operand = blocks
        self.out_shape = jax.ShapeDtypeStruct(blocks.shape, blocks.dtype)

    def bind(self, x_ref, out_ref, send_sems, recv_sems, local_sem):
        me = _place()
        peers = [_flip(me, 4), _flip(me, 2), _flip(me, 6)]
        mine = pltpu.make_async_copy(x_ref.at[_chip_of(me)], out_ref.at[_chip_of(me)], local_sem)

        def copy(j, src_chip, dst_chip):
            return pltpu.make_async_remote_copy(
                src_ref=x_ref.at[src_chip], dst_ref=out_ref.at[dst_chip], send_sem=send_sems.at[j],
                recv_sem=recv_sems.at[j], device_id=peers[j], device_id_type=MESH)

        sends = [copy(j, _chip_of(peer), _chip_of(me)) for j, peer in enumerate(peers)]

        def start():
            mine.start()
            for cp in sends:
                cp.start()

        def finish():
            for j, peer in enumerate(peers):
                copy(j, _chip_of(me), _chip_of(peer)).wait_recv()
            for cp in sends:
                cp.wait_send()
            mine.wait()

        return start, finish


class Together:
    def __init__(self, *parts):
        self.parts = parts
        self.operands = [p.operand for p in parts]
        self.out_shapes = [p.out_shape for p in parts]
        self.scratch = [s for p in parts for s in p.scratch]

    def bind(self, x_refs, out_refs, sems):
        bound, at = [], 0
        for p, x_ref, out_ref in zip(self.parts, x_refs, out_refs):
            bound.append(p.bind(x_ref, out_ref, *sems[at:at + len(p.scratch)]))
            at += len(p.scratch)

        def start():
            for s, _ in bound:
                s()

        def finish():
            for _, f in bound:
                f()

        return start, finish


def exchange_alone(exchange, name):
    n = len(exchange.operands)

    def body(*refs):
        start, finish = exchange.bind(refs[:n], refs[n:2 * n], refs[2 * n:])
        start()
        finish()

    return pl.pallas_call(body, name=name, out_shape=exchange.out_shapes, in_specs=[ANY] * n, out_specs=[ANY] * n,
                          scratch_shapes=exchange.scratch)(*exchange.operands)


def _row_tile(rows):
    return max([t for t in range(16, min(rows, 1024) + 1, 16) if rows % t == 0] or [rows])


def pair_exchange(blocks, name):
    n = len(blocks)

    def body(*refs):
        x_refs, theirs_refs, (send_sems, recv_sems) = refs[:n], refs[n:2 * n], refs[2 * n:]
        me = _place()
        remote = [pltpu.make_async_remote_copy(
            src_ref=x_refs[t].at[2 * q + 1 - me[2]], dst_ref=theirs_refs[t].at[q], send_sem=send_sems.at[4 * t + q],
            recv_sem=recv_sems.at[4 * t + q], device_id=_flip(me, 1), device_id_type=MESH) for t in range(n) for q in range(4)]
        for cp in remote:
            cp.start()
        for cp in remote:
            cp.wait()

    return pl.pallas_call(
        body, name=name, out_shape=[jax.ShapeDtypeStruct((4,) + b.shape[1:], b.dtype) for b in blocks], in_specs=[ANY] * n,
        out_specs=[ANY] * n, scratch_shapes=[pltpu.SemaphoreType.DMA((4 * n,)), pltpu.SemaphoreType.DMA((4 * n,))])(*blocks)


def pair_add(blocks, theirs, name):
    n, rows, width = theirs.shape
    tile = _row_tile(rows)
    spec = pl.BlockSpec((None, tile, width), lambda q, i: (q, i, 0))
    south = pl.BlockSpec((None, None, tile, width), lambda q, i: (q, 0, i, 0))
    north = pl.BlockSpec((None, None, tile, width), lambda q, i: (q, 1, i, 0))

    def body(s_ref, n_ref, b_ref, o_ref):
        mine = jnp.where(lax.axis_index("c") == 0, s_ref[...], n_ref[...])
        o_ref[...] = (mine.astype(f32) + b_ref[...].astype(f32)).astype(o_ref.dtype)

    by_core = blocks.reshape(n, 2, rows, width)
    return pl.pallas_call(body, name=name, grid=(n, rows // tile), in_specs=[south, north, spec], out_specs=spec,
                          out_shape=jax.ShapeDtypeStruct(theirs.shape, theirs.dtype), compiler_params=_params(2))(by_core, by_core, theirs)


def sum_slots(blocks, name):
    n, rows, width = blocks.shape
    tile = _row_tile(rows)

    def body(x_ref, o_ref):
        total = x_ref[0].astype(f32)
        for s in range(1, n):
            total = total + x_ref[s].astype(f32)
        o_ref[...] = total

    return pl.pallas_call(
        body, name=name, grid=(rows // tile,), in_specs=[pl.BlockSpec((n, tile, width), lambda i: (0, i, 0))],
        out_specs=pl.BlockSpec((tile, width), lambda i: (i, 0)), out_shape=_sds((rows, width)), compiler_params=_params(1))(blocks)


def all_reduce_small(x, name):
    rows, width = x.shape

    def body(x_ref, o_ref, land, send_sems, recv_sems):
        me = _place()
        copies = []
        for k in range(1, N_DEV):
            peer = _flip(me, k)
            copies.append(pltpu.make_async_remote_copy(
                src_ref=x_ref, dst_ref=land.at[_slot(me)], send_sem=send_sems.at[k - 1], recv_sem=recv_sems.at[k - 1],
                device_id=peer, device_id_type=MESH))
        for cp in copies:
            cp.start()
        land[_slot(me)] = x_ref[...]
        for k in range(1, N_DEV):
            peer = _flip(me, k)
            pltpu.make_async_remote_copy(
                src_ref=x_ref, dst_ref=land.at[_slot(peer)], send_sem=send_sems.at[k - 1], recv_sem=recv_sems.at[k - 1],
                device_id=peer, device_id_type=MESH).wait_recv()
        total = land[0]
        for s in range(1, N_DEV):
            total = total + land[s]
        o_ref[...] = total
        for cp in copies:
            cp.wait_send()

    return pl.pallas_call(
        body, name=name, out_shape=_sds((rows, width)), in_specs=[pl.BlockSpec(memory_space=pltpu.VMEM)],
        out_specs=pl.BlockSpec(memory_space=pltpu.VMEM),
        scratch_shapes=[pltpu.VMEM((N_DEV, rows, width), f32), pltpu.SemaphoreType.DMA((7,)), pltpu.SemaphoreType.DMA((7,))],
    )(x)


def _pack_big(shards):
    packed = {name: shards[name].astype(bf16) for name in COL_SHARDED}
    packed["rows"] = jnp.concatenate([shards[name].astype(bf16) for name, _ in ROW_SHARDED], axis=1)
    return packed


def _unpack_gathered(gathered):
    full = {}
    for name, part in gathered.items():
        if name in COL_SHARDED:
            full[name] = part.transpose(1, 0, 2).reshape(D_MODEL, N_DEV * part.shape[2])
        else:
            at = 0
            for weight, rows in ROW_SHARDED:
                full[weight] = part[:, at:at + rows, :].reshape(N_DEV * rows, D_MODEL)
                at += rows
    if "w_gate_up" in full:
        full["w_gate_up"] = _interleave_gate_up(full["w_gate_up"])
    if "w_in" in full:
        w_in = full.pop("w_in")
        full["w_main"] = jnp.concatenate([w_in[:, :AB_AT], w_in[:, AB_AT + 2 * GDN_HEADS:]], axis=1)
        full["w_ab"] = jnp.pad(w_in[:, AB_AT:AB_AT + 2 * GDN_HEADS], ((0, 0), (0, LANES - 2 * GDN_HEADS)))
    return full


def _pack_grads(grads, group):
    packed = {}
    for name in group:
        if name == "w_in":
            main, ab = grads["w_main"], grads["w_ab"]
            g = jnp.concatenate([main[:, :AB_AT], ab[:, :2 * GDN_HEADS], main[:, AB_AT:]], axis=1)
        elif name == "w_gate_up":
            g = _interleave_gate_up(grads[name], undo=True)
        elif name == "rows":
            packed[name] = jnp.concatenate([grads[weight].reshape(N_DEV, rows, D_MODEL) for weight, rows in ROW_SHARDED], axis=1)
            continue
        else:
            g = grads[name]
        packed[name] = g.reshape(D_MODEL, N_DEV, g.shape[1] // N_DEV).transpose(1, 0, 2)
    return packed


def _unpack_shard(layers):
    out = {name: jnp.stack([layer[name] for layer in layers]) for name in COL_SHARDED}
    rows_pack, at = jnp.stack([layer["rows"] for layer in layers]), 0
    for weight, rows in ROW_SHARDED:
        out[weight] = rows_pack[:, at:at + rows, :]
        at += rows
    return out


def _rows_of(flat_len):
    return -(-flat_len // (8 * D_MODEL)) * 8


def _pack_small(parts):
    flat = jnp.concatenate([p.reshape(-1) for p in parts])
    rows = _rows_of(flat.shape[0])
    flat = jnp.pad(flat, (0, rows * D_MODEL - flat.shape[0]))
    return flat.reshape(rows, D_MODEL)


def _unpack_small(packed, like):
    flat, out, at = packed.reshape(-1), [], 0
    for p in like:
        out.append(flat[at:at + p.size].reshape(p.shape))
        at += p.size
    return out


def _rope_tables(positions):
    inv_freq = jnp.float32(ROPE_THETA) ** (-jnp.arange(0, ROPE_DIM, 2, dtype=f32) / ROPE_DIM)
    ang = positions.astype(f32)[:, None] * inv_freq
    cos, sin = jnp.cos(ang), jnp.sin(ang)
    rest = ATTN_HEAD_DIM - ROPE_DIM
    cos_h = jnp.concatenate([cos, cos, jnp.ones((cos.shape[0], rest), f32)], axis=1)
    sin_h = jnp.concatenate([-sin, sin, jnp.zeros((sin.shape[0], rest), f32)], axis=1)
    return jnp.tile(cos_h, (1, ATTN_HEADS)), jnp.tile(sin_h, (1, ATTN_HEADS))


HEAD_SMALL = ("norm_mix_pre", "conv_short", "conv_gdn", "gdn_a_log", "gdn_dt_bias")


def _layer_head(h, p, cos_t, sin_t):
    hn = rms_norm(h, p["norm_mix_pre"][None], "norm_mix_pre")
    proj = _linear(hn, p["w_main"], "w_main")
    ab = _linear(hn, p["w_ab"], "w_ab")
    aw, cw, gw = ATTN_WIDTH, CONV_WIDTH, GDN_WIDTH
    aq, ak, av, cb, cc, cx, gqkv, gate = _split_cols(proj, (aw, aw, aw, cw, cw, cw, 3 * gw, gw))
    y_attn = dilated_attention(rope(aq, cos_t, sin_t, ATTN_HEAD_DIM ** -0.5, "rope_q"), rope(ak, cos_t, sin_t, 1.0, "rope_k"),
                               av, "attn")
    y_conv = short_conv(cb, cc, cx, p["conv_short"], "short_conv")
    qkv = gdn_pre(gqkv, p["conv_gdn"], "gdn_pre")
    pv = jnp.zeros((8, LANES), f32).at[0, :GDN_HEADS].set(p["gdn_a_log"]).at[1, :GDN_HEADS].set(p["gdn_dt_bias"])
    return (*_split_cols(qkv, (gw, gw, gw)), gate_beta(ab, pv, "gate_beta")), (gate, y_attn, y_conv)


MID_PARAMS = ("gdn_norm", "w_out", "norm_mix_post", "norm_xattn_pre", "w_xq", "norm_mem", "w_xkv", "w_xo", "norm_xattn_post",
              "norm_ffn_pre")


def _layer_mid(h, o, gate, y_attn, y_conv, p, mem):
    y_gdn = gdn_post(o, gate, p["gdn_norm"][None], "gdn_post")
    mix = _linear(jnp.concatenate([y_attn, y_conv, y_gdn], axis=1), p["w_out"], "w_out")
    h, hn = add_norm_then_norm(h, mix, p["norm_mix_post"][None], p["norm_xattn_pre"][None], "norm_mix_xattn")
    qx = _linear(hn, p["w_xq"], "w_xq")
    kv = _linear(rms_norm(mem, p["norm_mem"][None], "norm_mem"), p["w_xkv"], "w_xkv")
    xa = _linear(cross_attention(qx, kv, "xattn"), p["w_xo"], "w_xo")
    return add_norm_then_norm(h, xa, p["norm_xattn_post"][None], p["norm_ffn_pre"][None], "norm_xattn_ffn")


def _pair_summed(grads, group, name):
    blocks = _pack_grads(grads, group)
    theirs = pair_exchange([blocks[n] for n in group], name + "_pair_exchange")
    return [pair_add(blocks[n], t, f"{name}_pair_add_{n}") for n, t in zip(group, theirs)]


def _forward_backward(x, packed, small, mem, cos_t, sin_t, target):
    def gathers(group, layer):
        return [Gather(packed[n][layer]) for n in group]

    h = x
    head_gathered = exchange_alone(Together(*gathers(HEAD_GROUP, 0)), "gather_first")
    saved = []
    for layer in range(DEPTH):
        at_layer = {n: t[layer] for n, t in small.items()}
        head_p = {**_unpack_gathered(dict(zip(HEAD_GROUP, head_gathered))), **{n: at_layer[n] for n in HEAD_SMALL}}
        (rule_in, rest), head_vjp = jax.vjp(lambda h, hp: _layer_head(h, hp, cos_t, sin_t), h, head_p)
        carried = gathers(TAIL_GROUP, layer) + (gathers(HEAD_GROUP, layer + 1) if layer + 1 < DEPTH else [])
        o, s_in, inv, *landed = delta_rule_fwd(*rule_in, "delta_rule_fwd", Together(*carried))
        head_gathered = landed[len(TAIL_GROUP):]
        tail_p = {**_unpack_gathered(dict(zip(TAIL_GROUP, landed))), **at_layer}
        mid_p = {n: tail_p[n] for n in MID_PARAMS}
        (h, hn), mid_vjp = jax.vjp(lambda h, o, rest, mp: _layer_mid(h, o, *rest, mp, mem), h, o, rest, mid_p)
        y, ffn_saved = ffn_forward(hn, tail_p["w_gate_up"], tail_p["w_down"], "ffn")
        h, last_vjp = jax.vjp(lambda h, y, w: add_norm(h, y, w[None], "norm_ffn_post"), h, y, tail_p["norm_ffn_post"])
        saved.append((head_vjp, mid_vjp, last_vjp, ffn_saved, rule_in, s_in, inv))

    loss, dh = jax.value_and_grad(lambda y: loss_rows(y, target, "loss"))(h)

    def summed(group, landed):
        return {n: sum_slots(t, "sum_grads_" + n) for n, t in zip(group, landed)}

    big_grads, small_grads, head_pending = [{} for _ in range(DEPTH)], [None] * DEPTH, []
    for layer in reversed(range(DEPTH)):
        head_vjp, mid_vjp, last_vjp, ffn_saved, rule_in, s_in, inv = saved[layer]
        dh, dy, d_norm_ffn_post = last_vjp(dh)
        dhn, d_gate_up, d_down, landed = ffn_backward(
            ffn_saved, dy, "ffn", Together(*[ChipExchange(t) for t in head_pending]) if head_pending else None)
        if head_pending:
            big_grads[layer + 1].update(summed(HEAD_GROUP, landed))
        dh_mid, do, d_rest, d_mid_p = mid_vjp((dh, dhn))
        d_tail_p = {**d_mid_p, "w_gate_up": d_gate_up, "w_down": d_down, "norm_ffn_post": d_norm_ffn_post}
        carried = Together(*[ChipExchange(t) for t in _pair_summed(d_tail_p, TAIL_GROUP, "tail")])
        *d_rule_in, = delta_rule_bwd(*rule_in, s_in, inv, do, "delta_rule_bwd", carried)
        big_grads[layer].update(summed(TAIL_GROUP, d_rule_in[4:]))
        dh_head, d_head_p = head_vjp((tuple(d_rule_in[:4]), d_rest))
        dh = dh_mid + dh_head
        small_grads[layer] = {n: t for n, t in {**d_head_p, **d_tail_p}.items() if n in small}
        head_pending = _pair_summed(d_head_p, HEAD_GROUP, "head")
    landed = exchange_alone(Together(*[ChipExchange(t) for t in head_pending]), "exchange_last")
    big_grads[0].update(summed(HEAD_GROUP, landed))
    return loss, dh, big_grads, small_grads


def kernel(x, mem, positions, norm_mix_pre, norm_mix_post, w_in, conv_short, conv_gdn, gdn_a_log, gdn_dt_bias, gdn_norm, w_out, norm_mem, norm_xattn_pre, norm_xattn_post, w_xq, w_xkv, w_xo, norm_ffn_pre, norm_ffn_post, w_gate_up, w_down, loss_target, m_norm_mix_pre, m_norm_mix_post, m_w_in, m_conv_short, m_conv_gdn, m_gdn_a_log, m_gdn_dt_bias, m_gdn_norm, m_w_out, m_norm_mem, m_norm_xattn_pre, m_norm_xattn_post, m_w_xq, m_w_xkv, m_w_xo, m_norm_ffn_pre, m_norm_ffn_post, m_w_gate_up, m_w_down, v_norm_mix_pre, v_norm_mix_post, v_w_in, v_conv_short, v_conv_gdn, v_gdn_a_log, v_gdn_dt_bias, v_gdn_norm, v_w_out, v_norm_mem, v_norm_xattn_pre, v_norm_xattn_post, v_w_xq, v_w_xkv, v_w_xo, v_norm_ffn_pre, v_norm_ffn_post, v_w_gate_up, v_w_down):
    given = dict(locals())
    weights = {n: given[n] for n in WEIGHTS}
    me = _slot(_place())

    def in_place(shard):
        full = jnp.zeros(shard.shape[:-1] + (shard.shape[-1] * N_DEV,), f32)
        return lax.dynamic_update_slice_in_dim(full, shard, me * shard.shape[-1], axis=shard.ndim - 1)

    placed = [in_place(conv_short), in_place(conv_gdn)]
    conv_short_full, conv_gdn_full = _unpack_small(all_reduce_small(_pack_small(placed), "gather_conv"), placed)
    small = {n: weights[n] for n in NORMS + ("gdn_a_log", "gdn_dt_bias", "gdn_norm")}
    small["conv_short"], small["conv_gdn"] = conv_short_full, conv_gdn_full

    cos_t, sin_t = _rope_tables(positions[0])
    loss, grad_x, big_layers, small_layers = _forward_backward(
        x[0], _pack_big(weights), small, mem[0], cos_t, sin_t, loss_target[0])
    grads = _unpack_shard(big_layers)

    names = sorted(small)
    parts = [jnp.stack([layer[n] for layer in small_layers]) for n in names] + [loss.reshape(1)]
    reduced = _unpack_small(all_reduce_small(_pack_small(parts), "reduce_small"), parts)
    loss = reduced[-1][0]
    for n, g in zip(names, reduced[:-1]):
        if n in ("conv_short", "conv_gdn"):
            width = weights[n].shape[-1]
            g = lax.dynamic_slice_in_dim(g, me * width, width, axis=g.ndim - 1)
        grads[n] = g

    delta, new_m, new_v = {}, {}, {}
    for n in WEIGHTS:
        delta[n], new_m[n], new_v[n] = adamw(weights[n], grads[n], given["m_" + n], given["v_" + n], "adamw_" + n)
    return (loss, grad_x[None], *[grads[n] for n in WEIGHTS], *[delta[n] for n in WEIGHTS],
            *[new_m[n] for n in WEIGHTS], *[new_v[n] for n in WEIGHTS])
```

```python
import functools

import jax
import jax.numpy as jnp
from jax import lax
from jax.experimental import pallas as pl
from jax.experimental.pallas import tpu as pltpu

f32 = jnp.float32
bf16 = jnp.bfloat16
HIGHEST = lax.Precision.HIGHEST
MESH = pl.DeviceIdType.MESH

N_DEV = 8
DEPTH = 4
D_MODEL = 1024
EPS = 1e-6
ATTN_HEADS, ATTN_HEAD_DIM = 4, 64
ATTN_WIDTH = ATTN_HEADS * ATTN_HEAD_DIM
DILATIONS = (1, 4, 16)
QB = 128
ROPE_THETA = 500000.0
ROPE_DIM = ATTN_HEAD_DIM // 4
CONV_WIDTH = 256
GDN_HEADS, GDN_HEAD_DIM = 4, 128
GDN_WIDTH = GDN_HEADS * GDN_HEAD_DIM
GDN_CHUNK = 64
XATTN_HEADS, XATTN_HEAD_DIM = 4, 256
FFN_HIDDEN = 2816
IN_WIDTH = 3592
AB_AT = 3 * ATTN_WIDTH + 3 * CONV_WIDTH + 3 * GDN_WIDTH
MAIN_WIDTH = IN_WIDTH - 2 * GDN_HEADS
LANES = 128
ROW_TILE = 256
VMEM_LIMIT = 56 * 1024 * 1024

ADAM_LR, ADAM_B1, ADAM_B2, ADAM_EPS, ADAM_WD, ADAM_STEP = 0.001, 0.9, 0.999, 1e-08, 0.01, 10

COL_SHARDED = ("w_in", "w_xkv", "w_gate_up")
ROW_SHARDED = (("w_out", 128), ("w_xq", 128), ("w_xo", 128), ("w_down", 352))
HEAD_GROUP = ("w_in",)
TAIL_GROUP = ("w_gate_up", "w_xkv", "rows")
NORMS = ("norm_mix_pre", "norm_mix_post", "norm_mem", "norm_xattn_pre", "norm_xattn_post", "norm_ffn_pre", "norm_ffn_post")
WEIGHTS = ("norm_mix_pre", "norm_mix_post", "w_in", "conv_short", "conv_gdn", "gdn_a_log", "gdn_dt_bias", "gdn_norm", "w_out",
           "norm_mem", "norm_xattn_pre", "norm_xattn_post", "w_xq", "w_xkv", "w_xo", "norm_ffn_pre", "norm_ffn_post",
           "w_gate_up", "w_down")


def _params(n_grid):
    return pltpu.CompilerParams(dimension_semantics=("arbitrary",) * n_grid, vmem_limit_bytes=VMEM_LIMIT)


def _pick(n, cands):
    for c in cands:
        if n % c == 0:
            return c
    return n


MXU_FLOPS = 9.0e14
HBM_BYTES_PER_S = 2.5e12
VMEM_RMW_BYTES_PER_S = 7.0e12
STEP_S = 0.4e-6
MATMUL_VMEM = 44 * 1024 * 1024


def _tiles(m, n, k, sa, sb, so, tn=None):
    def divisors(d):
        return sorted({d // s for s in range(1, d // LANES + 1) if d % s == 0 and (d // s) % LANES == 0}, reverse=True)

    best = None
    for tk in divisors(k):
        nk = k // tk
        for tm in divisors(m):
            for tn_ in [tn] if tn else divisors(n):
                per_step = tm * tk * sa + tk * tn_ * sb + tm * tn_ * so
                vmem = 2 * per_step + (tm * tn_ * 4 if nk > 1 else 0)
                vmem += (tm * tk * 2 if sa == 4 else 0) + (tk * tn_ * 2 if sb == 4 else 0) + tm * tn_ * 4
                if vmem > MATMUL_VMEM:
                    continue
                moved = m * k * sa * (1 if nk == 1 else n // tn_) + k * n * sb * (1 if nk == 1 and n == tn_ else m // tm) + m * n * so
                busy = 2 * m * n * k / MXU_FLOPS + (m * n * 8 * nk / VMEM_RMW_BYTES_PER_S if nk > 1 else 0)
                cost = max(moved / HBM_BYTES_PER_S, busy) + per_step / HBM_BYTES_PER_S + (m // tm) * (n // tn_) * nk * STEP_S
                if best is None or cost < best[0]:
                    best = (cost, tm, tn_, tk)
    return best[1:]


def _mm(a, b, ta, tb, out_dtype, name, finish=None, exchange=None):
    m, k = (a.shape[1], a.shape[0]) if ta else a.shape
    n = b.shape[0] if tb else b.shape[1]
    tm, tn, tk = _tiles(m, n, k, a.dtype.itemsize, b.dtype.itemsize, jnp.dtype(out_dtype).itemsize, finish and finish[0])
    nk = k // tk
    if finish:
        assert nk == 1
        _, extra, results, function = finish
        dims = (((0 if ta else 1,), (1 if tb else 0,)), ((), ()))
        carried = len(exchange.operands) if exchange else 0
        steps = (m // tm, n // tn)

        def finish_body(a_ref, b_ref, *refs):
            extra_refs, refs = refs[:len(extra)], refs[len(extra):]
            x_refs, refs = refs[:carried], refs[carried:]
            out_refs, refs = refs[:len(results)], refs[len(results):]
            if exchange:
                at = pl.program_id(0) * steps[1] + pl.program_id(1)
                start, wait = exchange.bind(x_refs, refs[:carried], refs[carried:])
                pl.when(at == 0)(start)
            p = lax.dot_general(a_ref[...].astype(bf16), b_ref[...].astype(bf16), dims, preferred_element_type=f32)
            for r, o in zip(out_refs, function(p, *[r[...] for r in extra_refs])):
                r[...] = o.astype(r.dtype)
            if exchange:
                pl.when(at == steps[0] * steps[1] - 1)(wait)

        return pl.pallas_call(
            finish_body, name=name, grid=steps,
            in_specs=[pl.BlockSpec((tk, tm), lambda i, j: (0, i)) if ta else pl.BlockSpec((tm, tk), lambda i, j: (i, 0)),
                      pl.BlockSpec((tn, tk), lambda i, j: (j, 0)) if tb else pl.BlockSpec((tk, tn), lambda i, j: (0, j))]
            + [pl.BlockSpec((tm, cols), lambda i, j: (i, j)) for _, cols in extra] + [ANY] * carried,
            out_specs=[pl.BlockSpec((tm, cols), lambda i, j: (i, j)) for _, cols in results] + [ANY] * carried,
            out_shape=[jax.ShapeDtypeStruct((m, n // tn * cols), dt) for dt, cols in results] + (exchange.out_shapes if exchange else []),
            scratch_shapes=exchange.scratch if exchange else [],
            compiler_params=_params(2))(a, b, *[x for x, _ in extra], *(exchange.operands if exchange else []))
    a_spec = pl.BlockSpec((tk, tm), lambda i, j, kk: (kk, i)) if ta else pl.BlockSpec((tm, tk), lambda i, j, kk: (i, kk))
    b_spec = pl.BlockSpec((tn, tk), lambda i, j, kk: (j, kk)) if tb else pl.BlockSpec((tk, tn), lambda i, j, kk: (kk, j))
    dims = (((0 if ta else 1,), (1 if tb else 0,)), ((), ()))

    def body(a_ref, b_ref, o_ref, *acc):
        kk = pl.program_id(2)
        p = lax.dot_general(a_ref[...].astype(bf16), b_ref[...].astype(bf16), dims, preferred_element_type=f32)
        if nk == 1:
            o_ref[...] = p.astype(o_ref.dtype)
            return
        acc_ref, = acc

        @pl.when(kk == 0)
        def _():
            acc_ref[...] = p

        @pl.when(kk > 0)
        def _():
            acc_ref[...] += p

        @pl.when(kk == nk - 1)
        def _():
            o_ref[...] = acc_ref[...].astype(o_ref.dtype)

    return pl.pallas_call(
        body, name=name, grid=(m // tm, n // tn, nk), in_specs=[a_spec, b_spec],
        out_specs=pl.BlockSpec((tm, tn), lambda i, j, kk: (i, j)), out_shape=jax.ShapeDtypeStruct((m, n), out_dtype),
        scratch_shapes=[pltpu.VMEM((tm, tn), f32)] if nk > 1 else [], compiler_params=_params(3))(a, b)


def _linear(x, w, name):
    @jax.custom_vjp
    def lin(x, w):
        return _mm(x, w, False, False, f32, name + "_y")

    def lin_f(x, w):
        return _mm(x, w, False, False, f32, name + "_y"), (x, w)

    def lin_b(res, dy):
        x, w = res
        return _mm(dy, w, False, True, f32, name + "_dx"), _mm(x, dy, True, False, bf16, name + "_dw")

    lin.defvjp(lin_f, lin_b)
    return lin(x, w)


GATE_UP_TILE = 512


def _interleave_gate_up(w, undo=False):
    two_f = w.shape[1]
    half = GATE_UP_TILE // 2
    nb = two_f // GATE_UP_TILE
    if undo:
        order = [2 * j + side for side in range(2) for j in range(nb)]
    else:
        order = [side * nb + j for j in range(nb) for side in range(2)]
    return jnp.concatenate([w[:, b * half:(b + 1) * half] for b in order], axis=1)


def ffn_forward(hn, w_gate_up, w_down, name):
    half = GATE_UP_TILE // 2

    def act_of(p):
        return p, jax.nn.silu(p[:, :half]) * p[:, half:]

    gate_up, act = _mm(hn, w_gate_up, False, False, bf16, name + "_act",
                       (GATE_UP_TILE, [], [(bf16, GATE_UP_TILE), (bf16, half)], act_of))
    return _mm(act, w_down, False, False, f32, name + "_y"), (hn, w_gate_up, w_down, gate_up, act)


def ffn_backward(saved, dy, name, exchange=None):
    hn, w_gate_up, w_down, gate_up, act = saved
    half = GATE_UP_TILE // 2

    def d_gate_up_of(d_act, gate_up):
        g, u = gate_up[:, :half].astype(f32), gate_up[:, half:].astype(f32)
        sig = jax.nn.sigmoid(g)
        return (jnp.concatenate([d_act * u * sig * (1.0 + g * (1.0 - sig)), d_act * g * sig], axis=1),)

    d_gate_up, *landed = _mm(dy, w_down, False, True, bf16, name + "_dact",
                             (half, [(gate_up, GATE_UP_TILE)], [(bf16, GATE_UP_TILE)], d_gate_up_of), exchange)
    return (_mm(d_gate_up, w_gate_up, False, True, f32, name + "_dx"), _mm(hn, d_gate_up, True, False, bf16, name + "_dw1"),
            _mm(act, dy, True, False, bf16, name + "_dw2"), landed)


def _split_cols(x, widths):
    edges = [sum(widths[:i]) for i in range(len(widths) + 1)]

    def cut(x):
        return tuple(x[:, a:b] for a, b in zip(edges[:-1], edges[1:]))

    @jax.custom_vjp
    def split(x):
        return cut(x)

    split.defvjp(lambda x: (cut(x), None), lambda _, cts: (jnp.concatenate(cts, axis=1),))
    return split(x)


def _block_op(name, f, grid, in_specs, out_defs, arrays, diff, acc=None, gdefs=None):
    acc, gdefs = acc or {}, gdefs or {}
    n_in, n_out, n_grid = len(in_specs), len(out_defs), len(grid)

    def fwd_call(*xs):
        def body(*refs):
            outs = f(*[r[...] for r in refs[:n_in]])
            for r, o in zip(refs[n_in:], outs):
                r[...] = o.astype(r.dtype)

        return pl.pallas_call(
            body, name=name + "_fwd", grid=grid, in_specs=in_specs, out_specs=[d[1] for d in out_defs],
            out_shape=[d[0] for d in out_defs], compiler_params=_params(n_grid))(*xs)

    def bwd_call(*xs_and_cts):
        def body(*refs):
            xs = [r[...] for r in refs[:n_in]]
            cts = tuple(r[...] for r in refs[n_in:n_in + n_out])

            def of_diff(*dx):
                full = list(xs)
                for i, v in zip(diff, dx):
                    full[i] = v
                return tuple(f(*full))

            _, vjp = jax.vjp(of_diff, *[xs[i] for i in diff])
            grads = vjp(cts)
            for i, g, r in zip(diff, grads, refs[n_in + n_out:]):
                if i in acc:
                    first = functools.reduce(jnp.logical_and, [pl.program_id(a) == 0 for a in acc[i]])

                    @pl.when(first)
                    def _(r=r):
                        r[...] = jnp.zeros_like(r)

                    r[...] += g.astype(r.dtype)
                else:
                    r[...] = g.astype(r.dtype)

        g_defs = [gdefs.get(i, (jax.ShapeDtypeStruct(arrays[i].shape, f32), in_specs[i])) for i in diff]
        return pl.pallas_call(
            body, name=name + "_bwd", grid=grid, in_specs=list(in_specs) + [d[1] for d in out_defs],
            out_specs=[d[1] for d in g_defs], out_shape=[d[0] for d in g_defs], compiler_params=_params(n_grid))(*xs_and_cts)

    return fwd_call, bwd_call


def _simple_op(name, f, grid, in_specs, out_defs, arrays, diff, acc=None):
    fwd_call, bwd_call = _block_op(name, f, grid, in_specs, out_defs, arrays, diff, acc)

    @jax.custom_vjp
    def op(*xs):
        return tuple(fwd_call(*xs))

    def op_f(*xs):
        return tuple(fwd_call(*xs)), xs

    def op_b(xs, cts):
        grads = bwd_call(*xs, *cts)
        out = [jnp.zeros_like(x) for x in xs]
        for i, g in zip(diff, grads):
            out[i] = g
        return tuple(out)

    op.defvjp(op_f, op_b)
    return op(*arrays)


def _rows(width, tile=ROW_TILE):
    return pl.BlockSpec((tile, width), lambda i: (i, 0))


def _whole(shape):
    return pl.BlockSpec(shape, lambda *_: (0,) * len(shape))


def _sds(shape):
    return jax.ShapeDtypeStruct(shape, f32)


def _rms(x, w):
    return x * lax.rsqrt(jnp.mean(x * x, axis=-1, keepdims=True) + EPS) * w


def rms_norm(x, w, name):
    r, d = x.shape
    return _simple_op(name, lambda x, w: (_rms(x, w),), (r // ROW_TILE,), [_rows(d), _whole((1, d))],
                      [(_sds((r, d)), _rows(d))], (x, w), (0, 1), {1: (0,)})[0]


def add_norm(h, y, w, name):
    r, d = h.shape
    return _simple_op(name, lambda h, y, w: (h + _rms(y, w),), (r // ROW_TILE,), [_rows(d), _rows(d), _whole((1, d))],
                      [(_sds((r, d)), _rows(d))], (h, y, w), (0, 1, 2), {2: (0,)})[0]


def add_norm_then_norm(h, y, w_post, w_pre, name):
    r, d = h.shape

    def f(h, y, w_post, w_pre):
        h_new = h + _rms(y, w_post)
        return h_new, _rms(h_new, w_pre)

    return _simple_op(name, f, (r // ROW_TILE,), [_rows(d), _rows(d), _whole((1, d)), _whole((1, d))],
                      [(_sds((r, d)), _rows(d))] * 2, (h, y, w_post, w_pre), (0, 1, 2, 3), {2: (0,), 3: (0,)})


def _swap8(x):
    def raw(x):
        lane = lax.broadcasted_iota(jnp.int32, x.shape, 1) % ATTN_HEAD_DIM
        half = ROPE_DIM // 2
        up = pltpu.roll(x, x.shape[1] - half, axis=1)
        down = pltpu.roll(x, half, axis=1)
        return jnp.where(lane < half, up, jnp.where(lane < ROPE_DIM, down, 0.0))

    @jax.custom_vjp
    def swap(x):
        return raw(x)

    swap.defvjp(lambda x: (raw(x), None), lambda _, g: (raw(g),))
    return swap(x)


def rope(x, cos_t, sin_t, scale, name):
    r, d = x.shape
    return _simple_op(name, lambda x, c, s: ((x * c + _swap8(x) * s) * scale,), (r // ROW_TILE,), [_rows(d)] * 3,
                      [(_sds((r, d)), _rows(d))], (x, cos_t, sin_t), (0,))[0]


def _shift_rows(x, k):
    n = x.shape[0]

    def down(x):
        row = lax.broadcasted_iota(jnp.int32, x.shape, 0)
        return jnp.where(row >= k, pltpu.roll(x, k, axis=0), 0.0)

    def up(x):
        row = lax.broadcasted_iota(jnp.int32, x.shape, 0)
        return jnp.where(row < n - k, pltpu.roll(x, n - k, axis=0), 0.0)

    @jax.custom_vjp
    def shift(x):
        return down(x)

    shift.defvjp(lambda x: (down(x), None), lambda _, g: (up(g),))
    return shift(x)


def _causal_conv(x, w):
    taps = w.shape[0]
    y = x * w[taps - 1:taps, :]
    for j in range(taps - 1):
        y = y + _shift_rows(x, taps - 1 - j) * w[j:j + 1, :]
    return y


def _cols(rows, at=0):
    return pl.BlockSpec((rows, LANES), lambda j: (0, at + j))


def short_conv(cb, cc, cx, w, name):
    s, c = cb.shape
    taps = w.shape[0]
    return _simple_op(name, lambda b, c_, x, w: (b * _causal_conv(c_ * x, w),), (c // LANES,),
                      [_cols(s)] * 3 + [_cols(taps)], [(_sds((s, c)), _cols(s))], (cb, cc, cx, w), (0, 1, 2, 3))[0]


def gdn_pre(qkv, w, name):
    s, c = qkv.shape
    taps = w.shape[0]

    def f(x, w):
        j = pl.program_id(0)
        y = jax.nn.silu(_causal_conv(x, w))
        normed = y * lax.rsqrt(jnp.sum(y * y, axis=-1, keepdims=True) + EPS)
        scale = jnp.where(j < GDN_HEADS, GDN_HEAD_DIM ** -0.5, 1.0).astype(f32)
        return (jnp.where(j < 2 * GDN_HEADS, normed * scale, y),)

    return _simple_op(name, f, (c // LANES,), [_cols(s), _cols(taps)], [(_sds((s, c)), _cols(s))], (qkv, w), (0, 1))[0]


def gate_beta(ab, pv, name):
    s = ab.shape[0]

    def f(ab, pv):
        lane = lax.broadcasted_iota(jnp.int32, ab.shape, 1)
        g = -jnp.exp(pv[0:1, :]) * jax.nn.softplus(ab + pv[1:2, :])
        return (jnp.where(lane < GDN_HEADS, g, jnp.where(lane < 2 * GDN_HEADS, jax.nn.sigmoid(ab), 0.0)),)

    return _simple_op(name, f, (s // ROW_TILE,), [_rows(LANES), _whole((8, LANES))], [(_sds((s, LANES)), _rows(LANES))],
                      (ab, pv), (0, 1), {1: (0,)})[0]


def gdn_post(o, gate, w, name):
    s, c = o.shape

    def f(o, g, w):
        heads = [slice(hd * LANES, (hd + 1) * LANES) for hd in range(c // LANES)]
        return (jnp.concatenate([_rms(o[:, hd], w) * jax.nn.silu(g[:, hd]) for hd in heads], axis=1),)

    return _simple_op(name, f, (s // ROW_TILE,), [_rows(c), _rows(c), _whole((1, LANES))], [(_sds((s, c)), _rows(c))],
                      (o, gate, w), (0, 1, 2), {2: (0,)})[0]


def attn_merge(outs, lses, name):
    s, c = outs[0].shape

    def f(o1, o2, o3, l1, l2, l3):
        m = lax.stop_gradient(jnp.maximum(jnp.maximum(l1, l2), l3))
        e1, e2, e3 = jnp.exp(l1 - m), jnp.exp(l2 - m), jnp.exp(l3 - m)
        return ((e1 * o1 + e2 * o2 + e3 * o3) / (e1 + e2 + e3),)

    return _simple_op(name, f, (s // ROW_TILE,), [_rows(c)] * 6, [(_sds((s, c)), _rows(c))], (*outs, *lses), tuple(range(6)))[0]


def loss_rows(y, target, name):
    s, d = y.shape
    nt = s // ROW_TILE

    def f(y, t):
        e = y - t
        part = 0.5 * jnp.sum(jnp.mean(e * e, axis=-1, keepdims=True), axis=0, keepdims=True)
        return (jnp.broadcast_to(part * (1.0 / (8 * LANES)), (8, LANES)),)

    out = _simple_op(name, f, (nt,), [_rows(d)] * 2, [(_sds((nt * 8, LANES)), pl.BlockSpec((8, LANES), lambda i: (i, 0)))],
                     (y, target), (0,))[0]
    return jnp.sum(out)


def _mxu(a, b, form):
    dims = {"nn": ((1,), (0,)), "nt": ((1,), (1,)), "tn": ((0,), (0,))}

    def raw(a, b, form):
        return lax.dot_general(a.astype(bf16), b.astype(bf16), (dims[form], ((), ())), preferred_element_type=f32)

    @jax.custom_vjp
    def prod(a, b):
        return raw(a, b, form)

    def prod_b(res, ct):
        a, b = res
        if form == "nn":
            return raw(ct, b, "nt"), raw(a, ct, "tn")
        if form == "nt":
            return raw(ct, b, "nn"), raw(ct, a, "tn")
        return raw(b, ct, "nt"), raw(a, ct, "nn")

    prod.defvjp(lambda a, b: (raw(a, b, form), (a, b)), prod_b)
    return prod(a, b)


def _masked_heads_attention(q, keys, values, seen):
    dh = ATTN_HEAD_DIM
    outs, lses = [], []
    for hd in range(q.shape[1] // dh):
        at = slice(hd * dh, (hd + 1) * dh)
        sc = jnp.where(seen, _mxu(q[:, at], keys[:, at], "nt"), -jnp.inf)
        m = lax.stop_gradient(jnp.max(sc, axis=-1, keepdims=True))
        p = jnp.exp(sc - m)
        l = jnp.sum(p, axis=-1, keepdims=True)
        outs.append(_mxu(p / l, values[:, at], "nn"))
        lses.append(jnp.broadcast_to(m + jnp.log(l), (q.shape[0], dh)))
    return jnp.concatenate(outs, axis=1), jnp.concatenate(lses, axis=1)


def residue_attention(q, k, v, name):
    r, qb, width = q.shape

    def f(q, k, v):
        seen = lax.broadcasted_iota(jnp.int32, (qb, qb), 1) <= lax.broadcasted_iota(jnp.int32, (qb, qb), 0)
        return _masked_heads_attention(q, k, v, seen)

    spec = pl.BlockSpec((None, qb, width), lambda b: (b, 0, 0))
    return _simple_op(name, f, (r,), [spec] * 3, [(_sds((r, qb, width)), spec)] * 2, (q, k, v), (0, 1, 2))


def band_attention(q, k, v, nb, name):
    r, qb, width = q.shape

    def f(q, kp, kc, vp, vc):
        has_prev = (pl.program_id(0) % nb) > 0
        i = lax.broadcasted_iota(jnp.int32, (qb, 2 * qb), 0)
        j = lax.broadcasted_iota(jnp.int32, (qb, 2 * qb), 1)
        seen = jnp.logical_or(jnp.logical_and(jnp.logical_and(j < qb, j >= i), has_prev), jnp.logical_and(j >= qb, j - qb <= i))
        return _masked_heads_attention(q, jnp.concatenate([kp, kc], axis=0), jnp.concatenate([vp, vc], axis=0), seen)

    blk = (None, qb, width)
    cur = pl.BlockSpec(blk, lambda b: (b, 0, 0))
    prev = pl.BlockSpec(blk, lambda b: (jnp.maximum(b - 1, 0), 0, 0))
    shape = _sds((r, qb, width))
    fwd_call, bwd_call = _block_op(name, f, (r,), [cur, prev, cur, prev, cur], [(shape, cur), (shape, cur)],
                                   (q, k, k, v, v), (0, 1, 2, 3, 4), gdefs={1: (shape, cur), 3: (shape, cur)})

    def to_prev(g):
        return jnp.concatenate([g[1:], jnp.zeros_like(g[:1])], axis=0)

    @jax.custom_vjp
    def op(q, k, v):
        return tuple(fwd_call(q, k, k, v, v))

    def op_b(res, cts):
        q, k, v = res
        dq, dkp, dkc, dvp, dvc = bwd_call(q, k, k, v, v, *cts)
        return dq, dkc + to_prev(dkp), dvc + to_prev(dvp)

    op.defvjp(lambda q, k, v: (tuple(fwd_call(q, k, k, v, v)), (q, k, v)), op_b)
    return op(q, k, v)


def dilated_attention(q, k, v, name):
    s = q.shape[0]
    outs, lses = [], []
    for d in DILATIONS:
        length = s // d
        nb = length // QB
        def to_residue(t):
            return t.reshape(length, d, ATTN_WIDTH).transpose(1, 0, 2).reshape(d * nb, QB, ATTN_WIDTH)

        def from_residue(t):
            return t.reshape(d, length, ATTN_WIDTH).transpose(1, 0, 2).reshape(s, ATTN_WIDTH)

        if nb == 1:
            o, lse = residue_attention(to_residue(q), to_residue(k), to_residue(v), f"{name}_d{d}")
        else:
            o, lse = band_attention(to_residue(q), to_residue(k), to_residue(v), nb, f"{name}_d{d}")
        outs.append(from_residue(o))
        lses.append(from_residue(lse))
    return attn_merge(outs, lses, name + "_merge")


def cross_attention(q, kv, name):
    s = q.shape[0]
    m = kv.shape[0]
    width = XATTN_HEADS * XATTN_HEAD_DIM
    tq = 512

    def f(q, k, v):
        sc = _mxu(q, k, "nt") * (XATTN_HEAD_DIM ** -0.5)
        mx = lax.stop_gradient(jnp.max(sc, axis=-1, keepdims=True))
        p = jnp.exp(sc - mx)
        return (_mxu(p / jnp.sum(p, axis=-1, keepdims=True), v, "nn"),)

    q_spec = pl.BlockSpec((tq, XATTN_HEAD_DIM), lambda a, i: (i, a))
    k_spec = pl.BlockSpec((m, XATTN_HEAD_DIM), lambda a, i: (0, a))
    v_spec = pl.BlockSpec((m, XATTN_HEAD_DIM), lambda a, i: (0, a + XATTN_HEADS))
    half = _sds((m, width))
    fwd_call, bwd_call = _block_op(name, f, (XATTN_HEADS, s // tq), [q_spec, k_spec, v_spec], [(_sds((s, width)), q_spec)],
                                   (q, kv, kv), (0, 1, 2), acc={1: (1,), 2: (1,)}, gdefs={1: (half, k_spec), 2: (half, k_spec)})

    @jax.custom_vjp
    def op(q, kv):
        return fwd_call(q, kv, kv)[0]

    def op_b(res, ct):
        q, kv = res
        dq, dk, dv = bwd_call(q, kv, kv, ct)
        return dq, jnp.concatenate([dk, dv], axis=1)

    op.defvjp(lambda q, kv: (fwd_call(q, kv, kv)[0], (q, kv)), op_b)
    return op(q, kv)


def _hi(a, b, form="nn"):
    dims = {"nn": ((1,), (0,)), "nt": ((1,), (1,)), "tn": ((0,), (0,))}[form]
    return lax.dot_general(a, b, (dims, ((), ())), precision=lax.Precision.HIGH, preferred_element_type=f32)


def _running_sum(g):
    def raw(x, form):
        c = x.shape[0]
        tri = (lax.broadcasted_iota(jnp.int32, (c, c), 0) >= lax.broadcasted_iota(jnp.int32, (c, c), 1)).astype(bf16)
        hi = x.astype(bf16)
        rest = x - hi.astype(f32)
        mid = rest.astype(bf16)
        low = (rest - mid.astype(f32)).astype(bf16)
        dims = (((1,) if form == "nn" else (0,), (0,)), ((), ()))
        return sum(lax.dot_general(tri, part, dims, preferred_element_type=f32) for part in (hi, mid, low))

    @jax.custom_vjp
    def run(x):
        return raw(x, "nn")

    run.defvjp(lambda x: (raw(x, "nn"), None), lambda _, ct: (raw(ct, "tn"),))
    return run(g)


def _unit_lower_inverse(a):
    c = a.shape[0]
    eye = (lax.broadcasted_iota(jnp.int32, (c, c), 0) == lax.broadcasted_iota(jnp.int32, (c, c), 1)).astype(f32)
    inv, power = eye - a, -a
    for _ in range(c.bit_length() - 2):
        power = _hi(power, power)
        inv = inv + _hi(inv, power)
    return inv


def _known_inverse(a, t):
    @jax.custom_vjp
    def inv(a, t):
        return t

    def inv_b(t, ct):
        return -_hi(_hi(t, ct, "tn"), t, "nt"), jnp.zeros_like(t)

    inv.defvjp(lambda a, t: (t, t), inv_b)
    return inv(a, t)


def _delta_chunk(q, k, v, g, beta, s0, known_inv=None):
    c = q.shape[0]
    i = lax.broadcasted_iota(jnp.int32, (c, c), 0)
    j = lax.broadcasted_iota(jnp.int32, (c, c), 1)
    causal, strict = i >= j, i > j
    dec = _running_sum(g)
    dec_i = dec[:, :c]
    rel = jnp.exp(jnp.where(causal, dec_i - dec_i.T, -jnp.inf))
    k_beta = k * beta
    on_k = _mxu(jnp.concatenate([k_beta, q], axis=0), k, "nt")
    a = jnp.where(strict, on_k[:c] * rel, 0.0)
    attn = jnp.where(causal, on_k[c:] * rel, 0.0)
    inv = _unit_lower_inverse(a) if known_inv is None else _known_inverse(a, known_inv)
    e_dec = jnp.exp(dec)
    solved = _hi(inv, jnp.concatenate([v * beta, k_beta * e_dec], axis=1))
    u, w = solved[:, :v.shape[1]], solved[:, v.shape[1]:]
    total = jnp.sum(g, axis=0, keepdims=True)
    on_state = _mxu(jnp.concatenate([w, q * e_dec], axis=0), s0, "nn")
    v_new = u - on_state[:c]
    o = on_state[c:] + _mxu(attn, v_new, "nn")
    s1 = s0 * jnp.exp(total) + _mxu(k * jnp.exp(total - dec), v_new, "tn")
    return o, s1, inv


def _delta_rule_call(name, walk, n, in_specs, out_specs, out_shape, operands, exchange):
    n_in, n_out = len(in_specs), len(out_specs)
    carried = len(exchange.operands) if exchange else 0

    def body(*refs):
        ins, refs = refs[:n_in], refs[n_in:]
        x_refs, refs = refs[:carried], refs[carried:]
        outs, refs = refs[:n_out], refs[n_out:]
        land_refs, (state, *sems) = refs[:carried], refs[carried:]
        step = pl.program_id(0)
        if exchange:
            start, finish = exchange.bind(x_refs, land_refs, sems)
            pl.when(step == 0)(start)

        @pl.when(step == 0)
        def _():
            state[...] = jnp.zeros_like(state)

        walk(ins, outs, state)
        if exchange:
            pl.when(step == n - 1)(finish)

    return pl.pallas_call(
        body, name=name, grid=(n,), in_specs=list(in_specs) + [ANY] * carried, out_specs=list(out_specs) + [ANY] * carried,
        out_shape=list(out_shape) + (exchange.out_shapes if exchange else []),
        scratch_shapes=[pltpu.VMEM((GDN_HEAD_DIM, GDN_WIDTH), f32)] + (exchange.scratch if exchange else []),
        compiler_params=_params(1))(*operands, *(exchange.operands if exchange else []))


def _delta_heads():
    heads = [slice(hd * GDN_HEAD_DIM, (hd + 1) * GDN_HEAD_DIM) for hd in range(GDN_HEADS)]
    inv_at = [slice(hd * GDN_CHUNK, (hd + 1) * GDN_CHUNK) for hd in range(GDN_HEADS)]
    return heads, inv_at


def _head_chunk(q, k, v, gates, s0, head, known_inv=None):
    g = jnp.broadcast_to(gates[:, head:head + 1], q.shape)
    beta = jnp.broadcast_to(gates[:, GDN_HEADS + head:GDN_HEADS + head + 1], q.shape)
    return _delta_chunk(q, k, v, g, beta, s0, known_inv)


def delta_rule_fwd(q, k, v, gates, name, exchange=None):
    s, width = q.shape
    c, dk = GDN_CHUNK, GDN_HEAD_DIM
    n = s // c
    heads, inv_at = _delta_heads()

    def walk(ins, outs, state):
        q_ref, k_ref, v_ref, gates_ref = ins
        o_ref, s_in_ref, inv_ref = outs
        s_in_ref[...] = state[...]
        gates = gates_ref[...]
        xs = [[r[:, hd] for r in (q_ref, k_ref, v_ref)] + [gates, state[:, hd], i] for i, hd in enumerate(heads)]
        ys = [_head_chunk(*x) for x in xs]
        for hd, at, (o, s1, inv) in zip(heads, inv_at, ys):
            o_ref[:, hd], state[:, hd], inv_ref[:, at] = o, s1, inv

    blk = pl.BlockSpec((c, width), lambda t: (t, 0))
    gt = pl.BlockSpec((c, LANES), lambda t: (t, 0))
    st = pl.BlockSpec((dk, width), lambda t: (t, 0))
    iv = pl.BlockSpec((c, GDN_HEADS * c), lambda t: (t, 0))
    return _delta_rule_call(name, walk, n, [blk] * 3 + [gt], [blk, st, iv],
                            [_sds((s, width)), _sds((n * dk, width)), _sds((s, GDN_HEADS * c))], (q, k, v, gates), exchange)


def delta_rule_bwd(q, k, v, gates, s_in, inv, do, name, exchange=None):
    s, width = q.shape
    c, dk = GDN_CHUNK, GDN_HEAD_DIM
    n = s // c
    heads, inv_at = _delta_heads()

    def walk(ins, outs, dstate):
        q_ref, k_ref, v_ref, gates_ref, s_ref, inv_ref, do_ref = ins
        dq_ref, dk_ref, dv_ref, dgates_ref = outs
        gates = gates_ref[...]
        xs = [[r[:, hd] for r in (q_ref, k_ref, v_ref)] + [gates, s_ref[:, hd]] for hd in heads]
        known = [inv_ref[:, at] for at in inv_at]
        cts = [(do_ref[:, hd], dstate[:, hd]) for hd in heads]
        grads = []
        for i, (x, t, ct) in enumerate(zip(xs, known, cts)):
            _, vjp = jax.vjp(lambda *y, t=t, i=i: _head_chunk(*y, i, known_inv=t)[:2], *x)
            grads.append(vjp(ct))
        dgates = grads[0][3]
        for g in grads[1:]:
            dgates = dgates + g[3]
        dgates_ref[...] = dgates
        for hd, (dq, dk_, dv, _, ds0) in zip(heads, grads):
            dq_ref[:, hd], dk_ref[:, hd], dv_ref[:, hd], dstate[:, hd] = dq, dk_, dv, ds0

    blk = pl.BlockSpec((c, width), lambda t: (n - 1 - t, 0))
    gt = pl.BlockSpec((c, LANES), lambda t: (n - 1 - t, 0))
    st = pl.BlockSpec((dk, width), lambda t: (n - 1 - t, 0))
    iv = pl.BlockSpec((c, GDN_HEADS * c), lambda t: (n - 1 - t, 0))
    return _delta_rule_call(name, walk, n, [blk] * 3 + [gt, st, iv, blk], [blk] * 3 + [gt],
                            [_sds((s, width))] * 3 + [_sds((s, LANES))], (q, k, v, gates, s_in, inv, do), exchange)


def adamw(w, g, m, v, name):
    shape = w.shape
    if len(shape) == 2:
        grid, spec = (1,), pl.BlockSpec(shape, lambda i: (0, 0))
    else:
        tile = shape[1] if shape[1] <= 512 else _pick(shape[1], (512, 256, 128))
        grid, spec = (shape[0], shape[1] // tile), pl.BlockSpec((None, tile, shape[2]), lambda layer, i: (layer, i, 0))

    def body(w_ref, g_ref, m_ref, v_ref, d_ref, nm_ref, nv_ref):
        grad = g_ref[...]
        nm = ADAM_B1 * m_ref[...] + (1.0 - ADAM_B1) * grad
        nv = ADAM_B2 * v_ref[...] + (1.0 - ADAM_B2) * (grad * grad)
        m_hat = nm / (1.0 - ADAM_B1 ** ADAM_STEP)
        v_hat = nv / (1.0 - ADAM_B2 ** ADAM_STEP)
        d_ref[...] = -ADAM_LR * (m_hat / (jnp.sqrt(v_hat) + ADAM_EPS) + ADAM_WD * w_ref[...])
        nm_ref[...] = nm
        nv_ref[...] = nv

    return tuple(pl.pallas_call(body, name=name, grid=grid, in_specs=[spec] * 4, out_specs=[spec] * 3,
                                out_shape=[_sds(shape)] * 3, compiler_params=_params(len(grid)))(w, g, m, v))


def _place():
    return lax.axis_index("x"), lax.axis_index("y"), lax.axis_index("c")


def _flip(p, bits):
    return tuple(1 - v if (bits >> s) & 1 else v for v, s in zip(p, (2, 1, 0)))


def _slot(p):
    return 4 * p[0] + 2 * p[1] + p[2]


def _chip_of(p):
    return 2 * p[0] + p[1]


ANY = pl.BlockSpec(memory_space=pl.ANY)


class Gather:
    scratch = (pltpu.SemaphoreType.DMA((7,)), pltpu.SemaphoreType.DMA((7,)), pltpu.SemaphoreType.DMA)

    def __init__(self, shard):
        self.operand = shard
        self.out_shape = jax.ShapeDtypeStruct((N_DEV,) + shard.shape, shard.dtype)

    def bind(self, x_ref, out_ref, send_sems, recv_sems, local_sem):
        me = _place()
        sibling = _flip(me, 1)
        chips = [_flip(me, 4), _flip(me, 2), _flip(me, 6)]

        def copy(k, block, to, src=None):
            return pltpu.make_async_remote_copy(
                src_ref=out_ref.at[_slot(block)] if src is None else src, dst_ref=out_ref.at[_slot(block)],
                send_sem=send_sems.at[k], recv_sem=recv_sems.at[k], device_id=to, device_id_type=MESH)

        mine = pltpu.make_async_copy(x_ref, out_ref.at[_slot(me)], local_sem)
        first = [copy(0, me, sibling, src=x_ref)] + [copy(1 + j, me, chip, src=x_ref) for j, chip in enumerate(chips)]
        passed = [copy(4 + j, chip, sibling) for j, chip in enumerate(chips)]

        def start():
            mine.start()
            for cp in first:
                cp.start()

        def finish():
            for j, chip in enumerate(chips):
                copy(1 + j, chip, me).wait_recv()
                passed[j].start()
            copy(0, sibling, me).wait_recv()
            for j, chip in enumerate(chips):
                copy(4 + j, _flip(chip, 1), me).wait_recv()
            for cp in first + passed:
                cp.wait_send()
            mine.wait()

        return start, finish


class ChipExchange:
    scratch = (pltpu.SemaphoreType.DMA((3,)), pltpu.SemaphoreType.DMA((3,)), pltpu.SemaphoreType.DMA)

    def __init__(self, blocks):
        self.operand = blocks
        self.out_shape = jax.ShapeDtypeStruct(blocks.shape, blocks.dtype)

    def bind(self, x_ref, out_ref, send_sems, recv_sems, local_sem):
        me = _place()
        peers = [_flip(me, 4), _flip(me, 2), _flip(me, 6)]
        mine = pltpu.make_async_copy(x_ref.at[_chip_of(me)], out_ref.at[_chip_of(me)], local_sem)

        def copy(j, src_chip, dst_chip):
            return pltpu.make_async_remote_copy(
                src_ref=x_ref.at[src_chip], dst_ref=out_ref.at[dst_chip], send_sem=send_sems.at[j],
                recv_sem=recv_sems.at[j], device_id=peers[j], device_id_type=MESH)

        sends = [copy(j, _chip_of(peer), _chip_of(me)) for j, peer in enumerate(peers)]

        def start():
            mine.start()
            for cp in sends:
                cp.start()

        def finish():
            for j, peer in enumerate(peers):
                copy(j, _chip_of(me), _chip_of(peer)).wait_recv()
            for cp in sends:
                cp.wait_send()
            mine.wait()

        return start, finish


class Together:
    def __init__(self, *parts):
        self.parts = parts
        self.operands = [p.operand for p in parts]
        self.out_shapes = [p.out_shape for p in parts]
        self.scratch = [s for p in parts for s in p.scratch]

    def bind(self, x_refs, out_refs, sems):
        bound, at = [], 0
        for p, x_ref, out_ref in zip(self.parts, x_refs, out_refs):
            bound.append(p.bind(x_ref, out_ref, *sems[at:at + len(p.scratch)]))
            at += len(p.scratch)

        def start():
            for s, _ in bound:
                s()

        def finish():
            for _, f in bound:
                f()

        return start, finish


def exchange_alone(exchange, name):
    n = len(exchange.operands)

    def body(*refs):
        start, finish = exchange.bind(refs[:n], refs[n:2 * n], refs[2 * n:])
        start()
        finish()

    return pl.pallas_call(body, name=name, out_shape=exchange.out_shapes, in_specs=[ANY] * n, out_specs=[ANY] * n,
                          scratch_shapes=exchange.scratch)(*exchange.operands)


def _row_tile(rows):
    return max([t for t in range(16, min(rows, 1024) + 1, 16) if rows % t == 0] or [rows])


def pair_exchange(blocks, name):
    n = len(blocks)

    def body(*refs):
        x_refs, theirs_refs, (send_sems, recv_sems) = refs[:n], refs[n:2 * n], refs[2 * n:]
        me = _place()
        remote = [pltpu.make_async_remote_copy(
            src_ref=x_refs[t].at[2 * q + 1 - me[2]], dst_ref=theirs_refs[t].at[q], send_sem=send_sems.at[4 * t + q],
            recv_sem=recv_sems.at[4 * t + q], device_id=_flip(me, 1), device_id_type=MESH) for t in range(n) for q in range(4)]
        for cp in remote:
            cp.start()
        for cp in remote:
            cp.wait()

    return pl.pallas_call(
        body, name=name, out_shape=[jax.ShapeDtypeStruct((4,) + b.shape[1:], b.dtype) for b in blocks], in_specs=[ANY] * n,
        out_specs=[ANY] * n, scratch_shapes=[pltpu.SemaphoreType.DMA((4 * n,)), pltpu.SemaphoreType.DMA((4 * n,))])(*blocks)


def pair_add(blocks, theirs, name):
    n, rows, width = theirs.shape
    tile = _row_tile(rows)
    spec = pl.BlockSpec((None, tile, width), lambda q, i: (q, i, 0))
    south = pl.BlockSpec((None, None, tile, width), lambda q, i: (q, 0, i, 0))
    north = pl.BlockSpec((None, None, tile, width), lambda q, i: (q, 1, i, 0))

    def body(s_ref, n_ref, b_ref, o_ref):
        mine = jnp.where(lax.axis_index("c") == 0, s_ref[...], n_ref[...])
        o_ref[...] = (mine.astype(f32) + b_ref[...].astype(f32)).astype(o_ref.dtype)

    by_core = blocks.reshape(n, 2, rows, width)
    return pl.pallas_call(body, name=name, grid=(n, rows // tile), in_specs=[south, north, spec], out_specs=spec,
                          out_shape=jax.ShapeDtypeStruct(theirs.shape, theirs.dtype), compiler_params=_params(2))(by_core, by_core, theirs)


def sum_slots(blocks, name):
    n, rows, width = blocks.shape
    tile = _row_tile(rows)

    def body(x_ref, o_ref):
        total = x_ref[0].astype(f32)
        for s in range(1, n):
            total = total + x_ref[s].astype(f32)
        o_ref[...] = total

    return pl.pallas_call(
        body, name=name, grid=(rows // tile,), in_specs=[pl.BlockSpec((n, tile, width), lambda i: (0, i, 0))],
        out_specs=pl.BlockSpec((tile, width), lambda i: (i, 0)), out_shape=_sds((rows, width)), compiler_params=_params(1))(blocks)


def all_reduce_small(x, name):
    rows, width = x.shape

    def body(x_ref, o_ref, land, send_sems, recv_sems):
        me = _place()
        copies = []
        for k in range(1, N_DEV):
            peer = _flip(me, k)
            copies.append(pltpu.make_async_remote_copy(
                src_ref=x_ref, dst_ref=land.at[_slot(me)], send_sem=send_sems.at[k - 1], recv_sem=recv_sems.at[k - 1],
                device_id=peer, device_id_type=MESH))
        for cp in copies:
            cp.start()
        land[_slot(me)] = x_ref[...]
        for k in range(1, N_DEV):
            peer = _flip(me, k)
            pltpu.make_async_remote_copy(
                src_ref=x_ref, dst_ref=land.at[_slot(peer)], send_sem=send_sems.at[k - 1], recv_sem=recv_sems.at[k - 1],
                device_id=peer, device_id_type=MESH).wait_recv()
        total = land[0]
        for s in range(1, N_DEV):
            total = total + land[s]
        o_ref[...] = total
        for cp in copies:
            cp.wait_send()

    return pl.pallas_call(
        body, name=name, out_shape=_sds((rows, width)), in_specs=[pl.BlockSpec(memory_space=pltpu.VMEM)],
        out_specs=pl.BlockSpec(memory_space=pltpu.VMEM),
        scratch_shapes=[pltpu.VMEM((N_DEV, rows, width), f32), pltpu.SemaphoreType.DMA((7,)), pltpu.SemaphoreType.DMA((7,))],
    )(x)


def _pack_big(shards):
    packed = {name: shards[name].astype(bf16) for name in COL_SHARDED}
    packed["rows"] = jnp.concatenate([shards[name].astype(bf16) for name, _ in ROW_SHARDED], axis=1)
    return packed


def _unpack_gathered(gathered):
    full = {}
    for name, part in gathered.items():
        if name in COL_SHARDED:
            full[name] = part.transpose(1, 0, 2).reshape(D_MODEL, N_DEV * part.shape[2])
        else:
            at = 0
            for weight, rows in ROW_SHARDED:
                full[weight] = part[:, at:at + rows, :].reshape(N_DEV * rows, D_MODEL)
                at += rows
    if "w_gate_up" in full:
        full["w_gate_up"] = _interleave_gate_up(full["w_gate_up"])
    if "w_in" in full:
        w_in = full.pop("w_in")
        full["w_main"] = jnp.concatenate([w_in[:, :AB_AT], w_in[:, AB_AT + 2 * GDN_HEADS:]], axis=1)
        full["w_ab"] = jnp.pad(w_in[:, AB_AT:AB_AT + 2 * GDN_HEADS], ((0, 0), (0, LANES - 2 * GDN_HEADS)))
    return full


def _pack_grads(grads, group):
    packed = {}
    for name in group:
        if name == "w_in":
            main, ab = grads["w_main"], grads["w_ab"]
            g = jnp.concatenate([main[:, :AB_AT], ab[:, :2 * GDN_HEADS], main[:, AB_AT:]], axis=1)
        elif name == "w_gate_up":
            g = _interleave_gate_up(grads[name], undo=True)
        elif name == "rows":
            packed[name] = jnp.concatenate([grads[weight].reshape(N_DEV, rows, D_MODEL) for weight, rows in ROW_SHARDED], axis=1)
            continue
        else:
            g = grads[name]
        packed[name] = g.reshape(D_MODEL, N_DEV, g.shape[1] // N_DEV).transpose(1, 0, 2)
    return packed


def _unpack_shard(layers):
    out = {name: jnp.stack([layer[name] for layer in layers]) for name in COL_SHARDED}
    rows_pack, at = jnp.stack([layer["rows"] for layer in layers]), 0
    for weight, rows in ROW_SHARDED:
        out[weight] = rows_pack[:, at:at + rows, :]
        at += rows
    return out


def _rows_of(flat_len):
    return -(-flat_len // (8 * D_MODEL)) * 8


def _pack_small(parts):
    flat = jnp.concatenate([p.reshape(-1) for p in parts])
    rows = _rows_of(flat.shape[0])
    flat = jnp.pad(flat, (0, rows * D_MODEL - flat.shape[0]))
    return flat.reshape(rows, D_MODEL)


def _unpack_small(packed, like):
    flat, out, at = packed.reshape(-1), [], 0
    for p in like:
        out.append(flat[at:at + p.size].reshape(p.shape))
        at += p.size
    return out


def _rope_tables(positions):
    inv_freq = jnp.float32(ROPE_THETA) ** (-jnp.arange(0, ROPE_DIM, 2, dtype=f32) / ROPE_DIM)
    ang = positions.astype(f32)[:, None] * inv_freq
    cos, sin = jnp.cos(ang), jnp.sin(ang)
    rest = ATTN_HEAD_DIM - ROPE_DIM
    cos_h = jnp.concatenate([cos, cos, jnp.ones((cos.shape[0], rest), f32)], axis=1)
    sin_h = jnp.concatenate([-sin, sin, jnp.zeros((sin.shape[0], rest), f32)], axis=1)
    return jnp.tile(cos_h, (1, ATTN_HEADS)), jnp.tile(sin_h, (1, ATTN_HEADS))


HEAD_SMALL = ("norm_mix_pre", "conv_short", "conv_gdn", "gdn_a_log", "gdn_dt_bias")


def _layer_head(h, p, cos_t, sin_t):
    hn = rms_norm(h, p["norm_mix_pre"][None], "norm_mix_pre")
    proj = _linear(hn, p["w_main"], "w_main")
    ab = _linear(hn, p["w_ab"], "w_ab")
    aw, cw, gw = ATTN_WIDTH, CONV_WIDTH, GDN_WIDTH
    aq, ak, av, cb, cc, cx, gqkv, gate = _split_cols(proj, (aw, aw, aw, cw, cw, cw, 3 * gw, gw))
    y_attn = dilated_attention(rope(aq, cos_t, sin_t, ATTN_HEAD_DIM ** -0.5, "rope_q"), rope(ak, cos_t, sin_t, 1.0, "rope_k"),
                               av, "attn")
    y_conv = short_conv(cb, cc, cx, p["conv_short"], "short_conv")
    qkv = gdn_pre(gqkv, p["conv_gdn"], "gdn_pre")
    pv = jnp.zeros((8, LANES), f32).at[0, :GDN_HEADS].set(p["gdn_a_log"]).at[1, :GDN_HEADS].set(p["gdn_dt_bias"])
    return (*_split_cols(qkv, (gw, gw, gw)), gate_beta(ab, pv, "gate_beta")), (gate, y_attn, y_conv)


MID_PARAMS = ("gdn_norm", "w_out", "norm_mix_post", "norm_xattn_pre", "w_xq", "norm_mem", "w_xkv", "w_xo", "norm_xattn_post",
              "norm_ffn_pre")


def _layer_mid(h, o, gate, y_attn, y_conv, p, mem):
    y_gdn = gdn_post(o, gate, p["gdn_norm"][None], "gdn_post")
    mix = _linear(jnp.concatenate([y_attn, y_conv, y_gdn], axis=1), p["w_out"], "w_out")
    h, hn = add_norm_then_norm(h, mix, p["norm_mix_post"][None], p["norm_xattn_pre"][None], "norm_mix_xattn")
    qx = _linear(hn, p["w_xq"], "w_xq")
    kv = _linear(rms_norm(mem, p["norm_mem"][None], "norm_mem"), p["w_xkv"], "w_xkv")
    xa = _linear(cross_attention(qx, kv, "xattn"), p["w_xo"], "w_xo")
    return add_norm_then_norm(h, xa, p["norm_xattn_post"][None], p["norm_ffn_pre"][None], "norm_xattn_ffn")


def _pair_summed(grads, group, name):
    blocks = _pack_grads(grads, group)
    theirs = pair_exchange([blocks[n] for n in group], name + "_pair_exchange")
    return [pair_add(blocks[n], t, f"{name}_pair_add_{n}") for n, t in zip(group, theirs)]


def _forward_backward(x, packed, small, mem, cos_t, sin_t, target):
    def gathers(group, layer):
        return [Gather(packed[n][layer]) for n in group]

    h = x
    head_gathered = exchange_alone(Together(*gathers(HEAD_GROUP, 0)), "gather_first")
    saved = []
    for layer in range(DEPTH):
        at_layer = {n: t[layer] for n, t in small.items()}
        head_p = {**_unpack_gathered(dict(zip(HEAD_GROUP, head_gathered))), **{n: at_layer[n] for n in HEAD_SMALL}}
        (rule_in, rest), head_vjp = jax.vjp(lambda h, hp: _layer_head(h, hp, cos_t, sin_t), h, head_p)
        carried = gathers(TAIL_GROUP, layer) + (gathers(HEAD_GROUP, layer + 1) if layer + 1 < DEPTH else [])
        o, s_in, inv, *landed = delta_rule_fwd(*rule_in, "delta_rule_fwd", Together(*carried))
        head_gathered = landed[len(TAIL_GROUP):]
        tail_p = {**_unpack_gathered(dict(zip(TAIL_GROUP, landed))), **at_layer}
        mid_p = {n: tail_p[n] for n in MID_PARAMS}
        (h, hn), mid_vjp = jax.vjp(lambda h, o, rest, mp: _layer_mid(h, o, *rest, mp, mem), h, o, rest, mid_p)
        y, ffn_saved = ffn_forward(hn, tail_p["w_gate_up"], tail_p["w_down"], "ffn")
        h, last_vjp = jax.vjp(lambda h, y, w: add_norm(h, y, w[None], "norm_ffn_post"), h, y, tail_p["norm_ffn_post"])
        saved.append((head_vjp, mid_vjp, last_vjp, ffn_saved, rule_in, s_in, inv))

    loss, dh = jax.value_and_grad(lambda y: loss_rows(y, target, "loss"))(h)

    def summed(group, landed):
        return {n: sum_slots(t, "sum_grads_" + n) for n, t in zip(group, landed)}

    big_grads, small_grads, head_pending = [{} for _ in range(DEPTH)], [None] * DEPTH, []
    for layer in reversed(range(DEPTH)):
        head_vjp, mid_vjp, last_vjp, ffn_saved, rule_in, s_in, inv = saved[layer]
        dh, dy, d_norm_ffn_post = last_vjp(dh)
        dhn, d_gate_up, d_down, landed = ffn_backward(
            ffn_saved, dy, "ffn", Together(*[ChipExchange(t) for t in head_pending]) if head_pending else None)
        if head_pending:
            big_grads[layer + 1].update(summed(HEAD_GROUP, landed))
        dh_mid, do, d_rest, d_mid_p = mid_vjp((dh, dhn))
        d_tail_p = {**d_mid_p, "w_gate_up": d_gate_up, "w_down": d_down, "norm_ffn_post": d_norm_ffn_post}
        carried = Together(*[ChipExchange(t) for t in _pair_summed(d_tail_p, TAIL_GROUP, "tail")])
        *d_rule_in, = delta_rule_bwd(*rule_in, s_in, inv, do, "delta_rule_bwd", carried)
        big_grads[layer].update(summed(TAIL_GROUP, d_rule_in[4:]))
        dh_head, d_head_p = head_vjp((tuple(d_rule_in[:4]), d_rest))
        dh = dh_mid + dh_head
        small_grads[layer] = {n: t for n, t in {**d_head_p, **d_tail_p}.items() if n in small}
        head_pending = _pair_summed(d_head_p, HEAD_GROUP, "head")
    landed = exchange_alone(Together(*[ChipExchange(t) for t in head_pending]), "exchange_last")
    big_grads[0].update(summed(HEAD_GROUP, landed))
    return loss, dh, big_grads, small_grads


def kernel(x, mem, positions, norm_mix_pre, norm_mix_post, w_in, conv_short, conv_gdn, gdn_a_log, gdn_dt_bias, gdn_norm, w_out, norm_mem, norm_xattn_pre, norm_xattn_post, w_xq, w_xkv, w_xo, norm_ffn_pre, norm_ffn_post, w_gate_up, w_down, loss_target, m_norm_mix_pre, m_norm_mix_post, m_w_in, m_conv_short, m_conv_gdn, m_gdn_a_log, m_gdn_dt_bias, m_gdn_norm, m_w_out, m_norm_mem, m_norm_xattn_pre, m_norm_xattn_post, m_w_xq, m_w_xkv, m_w_xo, m_norm_ffn_pre, m_norm_ffn_post, m_w_gate_up, m_w_down, v_norm_mix_pre, v_norm_mix_post, v_w_in, v_conv_short, v_conv_gdn, v_gdn_a_log, v_gdn_dt_bias, v_gdn_norm, v_w_out, v_norm_mem, v_norm_xattn_pre, v_norm_xattn_post, v_w_xq, v_w_xkv, v_w_xo, v_norm_ffn_pre, v_norm_ffn_post, v_w_gate_up, v_w_down):
    given = dict(locals())
    weights = {n: given[n] for n in WEIGHTS}
    me = _slot(_place())

    def in_place(shard):
        full = jnp.zeros(shard.shape[:-1] + (shard.shape[-1] * N_DEV,), f32)
        return lax.dynamic_update_slice_in_dim(full, shard, me * shard.shape[-1], axis=shard.ndim - 1)

    placed = [in_place(conv_short), in_place(conv_gdn)]
    conv_short_full, conv_gdn_full = _unpack_small(all_reduce_small(_pack_small(placed), "gather_conv"), placed)
    small = {n: weights[n] for n in NORMS + ("gdn_a_log", "gdn_dt_bias", "gdn_norm")}
    small["conv_short"], small["conv_gdn"] = conv_short_full, conv_gdn_full

    cos_t, sin_t = _rope_tables(positions[0])
    loss, grad_x, big_layers, small_layers = _forward_backward(
        x[0], _pack_big(weights), small, mem[0], cos_t, sin_t, loss_target[0])
    grads = _unpack_shard(big_layers)

    names = sorted(small)
    parts = [jnp.stack([layer[n] for layer in small_layers]) for n in names] + [loss.reshape(1)]
    reduced = _unpack_small(all_reduce_small(_pack_small(parts), "reduce_small"), parts)
    loss = reduced[-1][0]
    for n, g in zip(names, reduced[:-1]):
        if n in ("conv_short", "conv_gdn"):
            width = weights[n].shape[-1]
            g = lax.dynamic_slice_in_dim(g, me * width, width, axis=g.ndim - 1)
        grads[n] = g

    delta, new_m, new_v = {}, {}, {}
    for n in WEIGHTS:
        delta[n], new_m[n], new_v[n] = adamw(weights[n], grads[n], given["m_" + n], given["v_" + n], "adamw_" + n)
    return (loss, grad_x[None], *[grads[n] for n in WEIGHTS], *[delta[n] for n in WEIGHTS],
            *[new_m[n] for n in WEIGHTS], *[new_v[n] for n in WEIGHTS])
```

```python
import functools

import jax
import jax.numpy as jnp
from jax import lax
from jax.experimental import pallas as pl
from jax.experimental.pallas import tpu as pltpu

f32 = jnp.float32
bf16 = jnp.bfloat16
HIGHEST = lax.Precision.HIGHEST
MESH = pl.DeviceIdType.MESH

N_DEV = 8
DEPTH = 4
D_MODEL = 1024
EPS = 1e-6
ATTN_HEADS, ATTN_HEAD_DIM = 4, 64
ATTN_WIDTH = ATTN_HEADS * ATTN_HEAD_DIM
DILATIONS = (1, 4, 16)
QB = 128
ROPE_THETA = 500000.0
ROPE_DIM = ATTN_HEAD_DIM // 4
CONV_WIDTH = 256
GDN_HEADS, GDN_HEAD_DIM = 4, 128
GDN_WIDTH = GDN_HEADS * GDN_HEAD_DIM
GDN_CHUNK = 64
XATTN_HEADS, XATTN_HEAD_DIM = 4, 256
FFN_HIDDEN = 2816
IN_WIDTH = 3592
AB_AT = 3 * ATTN_WIDTH + 3 * CONV_WIDTH + 3 * GDN_WIDTH
MAIN_WIDTH = IN_WIDTH - 2 * GDN_HEADS
LANES = 128
ROW_TILE = 512
VMEM_LIMIT = 56 * 1024 * 1024

ADAM_LR, ADAM_B1, ADAM_B2, ADAM_EPS, ADAM_WD, ADAM_STEP = 0.001, 0.9, 0.999, 1e-08, 0.01, 10

COL_SHARDED = ("w_in", "w_xkv", "w_gate_up")
ROW_SHARDED = (("w_out", 128), ("w_xq", 128), ("w_xo", 128), ("w_down", 352))
HEAD_GROUP = ("w_in",)
TAIL_GROUP = ("w_gate_up", "w_xkv", "rows")
NORMS = ("norm_mix_pre", "norm_mix_post", "norm_mem", "norm_xattn_pre", "norm_xattn_post", "norm_ffn_pre", "norm_ffn_post")
WEIGHTS = ("norm_mix_pre", "norm_mix_post", "w_in", "conv_short", "conv_gdn", "gdn_a_log", "gdn_dt_bias", "gdn_norm", "w_out",
           "norm_mem", "norm_xattn_pre", "norm_xattn_post", "w_xq", "w_xkv", "w_xo", "norm_ffn_pre", "norm_ffn_post",
           "w_gate_up", "w_down")


def _params(n_grid):
    return pltpu.CompilerParams(dimension_semantics=("arbitrary",) * n_grid, vmem_limit_bytes=VMEM_LIMIT)


def _pick(n, cands):
    for c in cands:
        if n % c == 0:
            return c
    return n


MXU_FLOPS = 9.0e14
HBM_BYTES_PER_S = 2.5e12
VMEM_RMW_BYTES_PER_S = 7.0e12
STEP_S = 0.4e-6
MATMUL_VMEM = 44 * 1024 * 1024


def _tiles(m, n, k, sa, sb, so, tn=None):
    def divisors(d):
        return sorted({d // s for s in range(1, d // LANES + 1) if d % s == 0 and (d // s) % LANES == 0}, reverse=True)

    best = None
    for tk in divisors(k):
        nk = k // tk
        for tm in divisors(m):
            for tn_ in [tn] if tn else divisors(n):
                per_step = tm * tk * sa + tk * tn_ * sb + tm * tn_ * so
                vmem = 2 * per_step + (tm * tn_ * 4 if nk > 1 else 0)
                vmem += (tm * tk * 2 if sa == 4 else 0) + (tk * tn_ * 2 if sb == 4 else 0) + tm * tn_ * 4
                if vmem > MATMUL_VMEM:
                    continue
                moved = m * k * sa * (1 if nk == 1 else n // tn_) + k * n * sb * (1 if nk == 1 and n == tn_ else m // tm) + m * n * so
                busy = 2 * m * n * k / MXU_FLOPS + (m * n * 8 * nk / VMEM_RMW_BYTES_PER_S if nk > 1 else 0)
                cost = max(moved / HBM_BYTES_PER_S, busy) + per_step / HBM_BYTES_PER_S + (m // tm) * (n // tn_) * nk * STEP_S
                if best is None or cost < best[0]:
                    best = (cost, tm, tn_, tk)
    return best[1:]


def _mm(a, b, ta, tb, out_dtype, name, finish=None, exchange=None):
    m, k = (a.shape[1], a.shape[0]) if ta else a.shape
    n = b.shape[0] if tb else b.shape[1]
    tm, tn, tk = _tiles(m, n, k, a.dtype.itemsize, b.dtype.itemsize, jnp.dtype(out_dtype).itemsize, finish and finish[0])
    nk = k // tk
    if finish:
        assert nk == 1
        _, extra, results, function = finish
        dims = (((0 if ta else 1,), (1 if tb else 0,)), ((), ()))
        carried = len(exchange.operands) if exchange else 0
        steps = (m // tm, n // tn)

        def finish_body(a_ref, b_ref, *refs):
            extra_refs, refs = refs[:len(extra)], refs[len(extra):]
            x_refs, refs = refs[:carried], refs[carried:]
            out_refs, refs = refs[:len(results)], refs[len(results):]
            if exchange:
                at = pl.program_id(0) * steps[1] + pl.program_id(1)
                start, wait = exchange.bind(x_refs, refs[:carried], refs[carried:])
                pl.when(at == 0)(start)
            p = lax.dot_general(a_ref[...].astype(bf16), b_ref[...].astype(bf16), dims, preferred_element_type=f32)
            for r, o in zip(out_refs, function(p, *[r[...] for r in extra_refs])):
                r[...] = o.astype(r.dtype)
            if exchange:
                pl.when(at == steps[0] * steps[1] - 1)(wait)

        return pl.pallas_call(
            finish_body, name=name, grid=steps,
            in_specs=[pl.BlockSpec((tk, tm), lambda i, j: (0, i)) if ta else pl.BlockSpec((tm, tk), lambda i, j: (i, 0)),
                      pl.BlockSpec((tn, tk), lambda i, j: (j, 0)) if tb else pl.BlockSpec((tk, tn), lambda i, j: (0, j))]
            + [pl.BlockSpec((tm, cols), lambda i, j: (i, j)) for _, cols in extra] + [ANY] * carried,
            out_specs=[pl.BlockSpec((tm, cols), lambda i, j: (i, j)) for _, cols in results] + [ANY] * carried,
            out_shape=[jax.ShapeDtypeStruct((m, n // tn * cols), dt) for dt, cols in results] + (exchange.out_shapes if exchange else []),
            scratch_shapes=exchange.scratch if exchange else [],
            compiler_params=_params(2))(a, b, *[x for x, _ in extra], *(exchange.operands if exchange else []))
    a_spec = pl.BlockSpec((tk, tm), lambda i, j, kk: (kk, i)) if ta else pl.BlockSpec((tm, tk), lambda i, j, kk: (i, kk))
    b_spec = pl.BlockSpec((tn, tk), lambda i, j, kk: (j, kk)) if tb else pl.BlockSpec((tk, tn), lambda i, j, kk: (kk, j))
    dims = (((0 if ta else 1,), (1 if tb else 0,)), ((), ()))

    def body(a_ref, b_ref, o_ref, *acc):
        kk = pl.program_id(2)
        p = lax.dot_general(a_ref[...].astype(bf16), b_ref[...].astype(bf16), dims, preferred_element_type=f32)
        if nk == 1:
            o_ref[...] = p.astype(o_ref.dtype)
            return
        acc_ref, = acc

        @pl.when(kk == 0)
        def _():
            acc_ref[...] = p

        @pl.when(kk > 0)
        def _():
            acc_ref[...] += p

        @pl.when(kk == nk - 1)
        def _():
            o_ref[...] = acc_ref[...].astype(o_ref.dtype)

    return pl.pallas_call(
        body, name=name, grid=(m // tm, n // tn, nk), in_specs=[a_spec, b_spec],
        out_specs=pl.BlockSpec((tm, tn), lambda i, j, kk: (i, j)), out_shape=jax.ShapeDtypeStruct((m, n), out_dtype),
        scratch_shapes=[pltpu.VMEM((tm, tn), f32)] if nk > 1 else [], compiler_params=_params(3))(a, b)


def _linear(x, w, name):
    @jax.custom_vjp
    def lin(x, w):
        return _mm(x, w, False, False, f32, name + "_y")

    def lin_f(x, w):
        return _mm(x, w, False, False, f32, name + "_y"), (x, w)

    def lin_b(res, dy):
        x, w = res
        return _mm(dy, w, False, True, f32, name + "_dx"), _mm(x, dy, True, False, bf16, name + "_dw")

    lin.defvjp(lin_f, lin_b)
    return lin(x, w)


GATE_UP_TILE = 512


def _interleave_gate_up(w, undo=False):
    two_f = w.shape[1]
    half = GATE_UP_TILE // 2
    nb = two_f // GATE_UP_TILE
    if undo:
        order = [2 * j + side for side in range(2) for j in range(nb)]
    else:
        order = [side * nb + j for j in range(nb) for side in range(2)]
    return jnp.concatenate([w[:, b * half:(b + 1) * half] for b in order], axis=1)


def ffn_forward(hn, w_gate_up, w_down, name):
    half = GATE_UP_TILE // 2

    def act_of(p):
        return p, jax.nn.silu(p[:, :half]) * p[:, half:]

    gate_up, act = _mm(hn, w_gate_up, False, False, bf16, name + "_act",
                       (GATE_UP_TILE, [], [(bf16, GATE_UP_TILE), (bf16, half)], act_of))
    return _mm(act, w_down, False, False, f32, name + "_y"), (hn, w_gate_up, w_down, gate_up, act)


def ffn_backward(saved, dy, name, exchange=None):
    hn, w_gate_up, w_down, gate_up, act = saved
    half = GATE_UP_TILE // 2

    def d_gate_up_of(d_act, gate_up):
        g, u = gate_up[:, :half].astype(f32), gate_up[:, half:].astype(f32)
        sig = jax.nn.sigmoid(g)
        return (jnp.concatenate([d_act * u * sig * (1.0 + g * (1.0 - sig)), d_act * g * sig], axis=1),)

    d_gate_up, *landed = _mm(dy, w_down, False, True, bf16, name + "_dact",
                             (half, [(gate_up, GATE_UP_TILE)], [(bf16, GATE_UP_TILE)], d_gate_up_of), exchange)
    return (_mm(d_gate_up, w_gate_up, False, True, f32, name + "_dx"), _mm(hn, d_gate_up, True, False, bf16, name + "_dw1"),
            _mm(act, dy, True, False, bf16, name + "_dw2"), landed)


def _split_cols(x, widths):
    edges = [sum(widths[:i]) for i in range(len(widths) + 1)]

    def cut(x):
        return tuple(x[:, a:b] for a, b in zip(edges[:-1], edges[1:]))

    @jax.custom_vjp
    def split(x):
        return cut(x)

    split.defvjp(lambda x: (cut(x), None), lambda _, cts: (jnp.concatenate(cts, axis=1),))
    return split(x)


def _block_op(name, f, grid, in_specs, out_defs, arrays, diff, acc=None, gdefs=None):
    acc, gdefs = acc or {}, gdefs or {}
    n_in, n_out, n_grid = len(in_specs), len(out_defs), len(grid)

    def fwd_call(*xs):
        def body(*refs):
            outs = f(*[r[...] for r in refs[:n_in]])
            for r, o in zip(refs[n_in:], outs):
                r[...] = o.astype(r.dtype)

        return pl.pallas_call(
            body, name=name + "_fwd", grid=grid, in_specs=in_specs, out_specs=[d[1] for d in out_defs],
            out_shape=[d[0] for d in out_defs], compiler_params=_params(n_grid))(*xs)

    def bwd_call(*xs_and_cts):
        def body(*refs):
            xs = [r[...] for r in refs[:n_in]]
            cts = tuple(r[...] for r in refs[n_in:n_in + n_out])

            def of_diff(*dx):
                full = list(xs)
                for i, v in zip(diff, dx):
                    full[i] = v
                return tuple(f(*full))

            _, vjp = jax.vjp(of_diff, *[xs[i] for i in diff])
            grads = vjp(cts)
            for i, g, r in zip(diff, grads, refs[n_in + n_out:]):
                if i in acc:
                    first = functools.reduce(jnp.logical_and, [pl.program_id(a) == 0 for a in acc[i]])

                    @pl.when(first)
                    def _(r=r):
                        r[...] = jnp.zeros_like(r)

                    r[...] += g.astype(r.dtype)
                else:
                    r[...] = g.astype(r.dtype)

        g_defs = [gdefs.get(i, (jax.ShapeDtypeStruct(arrays[i].shape, f32), in_specs[i])) for i in diff]
        return pl.pallas_call(
            body, name=name + "_bwd", grid=grid, in_specs=list(in_specs) + [d[1] for d in out_defs],
            out_specs=[d[1] for d in g_defs], out_shape=[d[0] for d in g_defs], compiler_params=_params(n_grid))(*xs_and_cts)

    return fwd_call, bwd_call


def _simple_op(name, f, grid, in_specs, out_defs, arrays, diff, acc=None):
    fwd_call, bwd_call = _block_op(name, f, grid, in_specs, out_defs, arrays, diff, acc)

    @jax.custom_vjp
    def op(*xs):
        return tuple(fwd_call(*xs))

    def op_f(*xs):
        return tuple(fwd_call(*xs)), xs

    def op_b(xs, cts):
        grads = bwd_call(*xs, *cts)
        out = [jnp.zeros_like(x) for x in xs]
        for i, g in zip(diff, grads):
            out[i] = g
        return tuple(out)

    op.defvjp(op_f, op_b)
    return op(*arrays)


def _rows(width, tile=ROW_TILE):
    return pl.BlockSpec((tile, width), lambda i: (i, 0))


def _whole(shape):
    return pl.BlockSpec(shape, lambda *_: (0,) * len(shape))


def _sds(shape):
    return jax.ShapeDtypeStruct(shape, f32)


def _rms(x, w):
    return x * lax.rsqrt(jnp.mean(x * x, axis=-1, keepdims=True) + EPS) * w


def rms_norm(x, w, name):
    r, d = x.shape
    tile = min(ROW_TILE, r)
    return _simple_op(name, lambda x, w: (_rms(x, w),), (r // tile,), [_rows(d, tile), _whole((1, d))],
                      [(_sds((r, d)), _rows(d, tile))], (x, w), (0, 1), {1: (0,)})[0]


def add_norm(h, y, w, name):
    r, d = h.shape
    return _simple_op(name, lambda h, y, w: (h + _rms(y, w),), (r // ROW_TILE,), [_rows(d), _rows(d), _whole((1, d))],
                      [(_sds((r, d)), _rows(d))], (h, y, w), (0, 1, 2), {2: (0,)})[0]


def add_norm_then_norm(h, y, w_post, w_pre, name):
    r, d = h.shape

    def f(h, y, w_post, w_pre):
        h_new = h + _rms(y, w_post)
        return h_new, _rms(h_new, w_pre)

    return _simple_op(name, f, (r // ROW_TILE,), [_rows(d), _rows(d), _whole((1, d)), _whole((1, d))],
                      [(_sds((r, d)), _rows(d))] * 2, (h, y, w_post, w_pre), (0, 1, 2, 3), {2: (0,), 3: (0,)})


def _swap8(x):
    def raw(x):
        lane = lax.broadcasted_iota(jnp.int32, x.shape, 1) % ATTN_HEAD_DIM
        half = ROPE_DIM // 2
        up = pltpu.roll(x, x.shape[1] - half, axis=1)
        down = pltpu.roll(x, half, axis=1)
        return jnp.where(lane < half, up, jnp.where(lane < ROPE_DIM, down, 0.0))

    @jax.custom_vjp
    def swap(x):
        return raw(x)

    swap.defvjp(lambda x: (raw(x), None), lambda _, g: (raw(g),))
    return swap(x)


def rope(x, cos_t, sin_t, scale, name):
    r, d = x.shape
    return _simple_op(name, lambda x, c, s: ((x * c + _swap8(x) * s) * scale,), (r // ROW_TILE,), [_rows(d)] * 3,
                      [(_sds((r, d)), _rows(d))], (x, cos_t, sin_t), (0,))[0]


def _shift_rows(x, k):
    n = x.shape[0]

    def down(x):
        row = lax.broadcasted_iota(jnp.int32, x.shape, 0)
        return jnp.where(row >= k, pltpu.roll(x, k, axis=0), 0.0)

    def up(x):
        row = lax.broadcasted_iota(jnp.int32, x.shape, 0)
        return jnp.where(row < n - k, pltpu.roll(x, n - k, axis=0), 0.0)

    @jax.custom_vjp
    def shift(x):
        return down(x)

    shift.defvjp(lambda x: (down(x), None), lambda _, g: (up(g),))
    return shift(x)


def _causal_conv(x, w):
    taps = w.shape[0]
    y = x * w[taps - 1:taps, :]
    for j in range(taps - 1):
        y = y + _shift_rows(x, taps - 1 - j) * w[j:j + 1, :]
    return y


def _cols(rows, at=0):
    return pl.BlockSpec((rows, LANES), lambda j: (0, at + j))


def short_conv(cb, cc, cx, w, name):
    s, c = cb.shape
    taps = w.shape[0]
    return _simple_op(name, lambda b, c_, x, w: (b * _causal_conv(c_ * x, w),), (c // LANES,),
                      [_cols(s)] * 3 + [_cols(taps)], [(_sds((s, c)), _cols(s))], (cb, cc, cx, w), (0, 1, 2, 3))[0]


def gdn_pre(qkv, w, name):
    s, c = qkv.shape
    taps = w.shape[0]

    def f(x, w):
        j = pl.program_id(0)
        y = jax.nn.silu(_causal_conv(x, w))
        normed = y * lax.rsqrt(jnp.sum(y * y, axis=-1, keepdims=True) + EPS)
        scale = jnp.where(j < GDN_HEADS, GDN_HEAD_DIM ** -0.5, 1.0).astype(f32)
        return (jnp.where(j < 2 * GDN_HEADS, normed * scale, y),)

    return _simple_op(name, f, (c // LANES,), [_cols(s), _cols(taps)], [(_sds((s, c)), _cols(s))], (qkv, w), (0, 1))[0]


def gate_beta(ab, pv, name):
    s = ab.shape[0]

    def f(ab, pv):
        lane = lax.broadcasted_iota(jnp.int32, ab.shape, 1)
        g = -jnp.exp(pv[0:1, :]) * jax.nn.softplus(ab + pv[1:2, :])
        return (jnp.where(lane < GDN_HEADS, g, jnp.where(lane < 2 * GDN_HEADS, jax.nn.sigmoid(ab), 0.0)),)

    return _simple_op(name, f, (s // ROW_TILE,), [_rows(LANES), _whole((8, LANES))], [(_sds((s, LANES)), _rows(LANES))],
                      (ab, pv), (0, 1), {1: (0,)})[0]


def gdn_post(o, gate, w, name):
    s, c = o.shape

    def f(o, g, w):
        heads = [slice(hd * LANES, (hd + 1) * LANES) for hd in range(c // LANES)]
        return (jnp.concatenate([_rms(o[:, hd], w) * jax.nn.silu(g[:, hd]) for hd in heads], axis=1),)

    return _simple_op(name, f, (s // ROW_TILE,), [_rows(c), _rows(c), _whole((1, LANES))], [(_sds((s, c)), _rows(c))],
                      (o, gate, w), (0, 1, 2), {2: (0,)})[0]


def attn_merge(outs, lses, name):
    s, c = outs[0].shape

    def f(o1, o2, o3, l1, l2, l3):
        m = lax.stop_gradient(jnp.maximum(jnp.maximum(l1, l2), l3))
        e1, e2, e3 = jnp.exp(l1 - m), jnp.exp(l2 - m), jnp.exp(l3 - m)
        return ((e1 * o1 + e2 * o2 + e3 * o3) / (e1 + e2 + e3),)

    return _simple_op(name, f, (s // ROW_TILE,), [_rows(c)] * 6, [(_sds((s, c)), _rows(c))], (*outs, *lses), tuple(range(6)))[0]


def loss_rows(y, target, name):
    s, d = y.shape
    nt = s // ROW_TILE

    def f(y, t):
        e = y - t
        part = 0.5 * jnp.sum(jnp.mean(e * e, axis=-1, keepdims=True), axis=0, keepdims=True)
        return (jnp.broadcast_to(part * (1.0 / (8 * LANES)), (8, LANES)),)

    out = _simple_op(name, f, (nt,), [_rows(d)] * 2, [(_sds((nt * 8, LANES)), pl.BlockSpec((8, LANES), lambda i: (i, 0)))],
                     (y, target), (0,))[0]
    return jnp.sum(out)


def _mxu(a, b, form):
    dims = {"nn": ((1,), (0,)), "nt": ((1,), (1,)), "tn": ((0,), (0,))}

    def raw(a, b, form):
        return lax.dot_general(a.astype(bf16), b.astype(bf16), (dims[form], ((), ())), preferred_element_type=f32)

    @jax.custom_vjp
    def prod(a, b):
        return raw(a, b, form)

    def prod_b(res, ct):
        a, b = res
        if form == "nn":
            return raw(ct, b, "nt"), raw(a, ct, "tn")
        if form == "nt":
            return raw(ct, b, "nn"), raw(ct, a, "tn")
        return raw(b, ct, "nt"), raw(a, ct, "nn")

    prod.defvjp(lambda a, b: (raw(a, b, form), (a, b)), prod_b)
    return prod(a, b)


def _masked_heads_attention(q, keys, values, seen):
    dh = ATTN_HEAD_DIM
    outs, lses = [], []
    for hd in range(q.shape[1] // dh):
        at = slice(hd * dh, (hd + 1) * dh)
        sc = jnp.where(seen, _mxu(q[:, at], keys[:, at], "nt"), -jnp.inf)
        m = lax.stop_gradient(jnp.max(sc, axis=-1, keepdims=True))
        p = jnp.exp(sc - m)
        l = jnp.sum(p, axis=-1, keepdims=True)
        outs.append(_mxu(p / l, values[:, at], "nn"))
        lses.append(jnp.broadcast_to(m + jnp.log(l), (q.shape[0], dh)))
    return jnp.concatenate(outs, axis=1), jnp.concatenate(lses, axis=1)


def band_attention(q, k, v, nb, name):
    r, qb, width = q.shape

    def f(q, kp, kc, vp, vc):
        has_prev = (pl.program_id(0) % nb) > 0
        i = lax.broadcasted_iota(jnp.int32, (qb, 2 * qb), 0)
        j = lax.broadcasted_iota(jnp.int32, (qb, 2 * qb), 1)
        seen = jnp.logical_or(jnp.logical_and(jnp.logical_and(j < qb, j >= i), has_prev), jnp.logical_and(j >= qb, j - qb <= i))
        return _masked_heads_attention(q, jnp.concatenate([kp, kc], axis=0), jnp.concatenate([vp, vc], axis=0), seen)

    blk = (None, qb, width)
    cur = pl.BlockSpec(blk, lambda b: (b, 0, 0))
    prev = pl.BlockSpec(blk, lambda b: (jnp.maximum(b - 1, 0), 0, 0))
    shape = _sds((r, qb, width))
    fwd_call, bwd_call = _block_op(name, f, (r,), [cur, prev, cur, prev, cur], [(shape, cur), (shape, cur)],
                                   (q, k, k, v, v), (0, 1, 2, 3, 4), gdefs={1: (shape, cur), 3: (shape, cur)})

    def to_prev(g):
        return jnp.concatenate([g[1:], jnp.zeros_like(g[:1])], axis=0)

    @jax.custom_vjp
    def op(q, k, v):
        return tuple(fwd_call(q, k, k, v, v))

    def op_b(res, cts):
        q, k, v = res
        dq, dkp, dkc, dvp, dvc = bwd_call(q, k, k, v, v, *cts)
        return dq, dkc + to_prev(dkp), dvc + to_prev(dvp)

    op.defvjp(lambda q, k, v: (tuple(fwd_call(q, k, k, v, v)), (q, k, v)), op_b)
    return op(q, k, v)


def dilated_attention(q, k, v, name):
    s = q.shape[0]
    outs, lses = [], []
    for d in DILATIONS:
        length = s // d
        nb = length // QB
        def to_residue(t):
            return t.reshape(length, d, ATTN_WIDTH).transpose(1, 0, 2).reshape(d * nb, QB, ATTN_WIDTH)

        def from_residue(t):
            return t.reshape(d, length, ATTN_WIDTH).transpose(1, 0, 2).reshape(s, ATTN_WIDTH)

        o, lse = band_attention(to_residue(q), to_residue(k), to_residue(v), nb, f"{name}_d{d}")
        outs.append(from_residue(o))
        lses.append(from_residue(lse))
    return attn_merge(outs, lses, name + "_merge")


def cross_attention(q, kv, name):
    s = q.shape[0]
    m = kv.shape[0]
    width = XATTN_HEADS * XATTN_HEAD_DIM
    tq = 512

    def f(q, k, v):
        sc = _mxu(q, k, "nt") * (XATTN_HEAD_DIM ** -0.5)
        mx = lax.stop_gradient(jnp.max(sc, axis=-1, keepdims=True))
        p = jnp.exp(sc - mx)
        return (_mxu(p / jnp.sum(p, axis=-1, keepdims=True), v, "nn"),)

    q_spec = pl.BlockSpec((tq, XATTN_HEAD_DIM), lambda a, i: (i, a))
    k_spec = pl.BlockSpec((m, XATTN_HEAD_DIM), lambda a, i: (0, a))
    v_spec = pl.BlockSpec((m, XATTN_HEAD_DIM), lambda a, i: (0, a + XATTN_HEADS))
    half = _sds((m, width))
    fwd_call, bwd_call = _block_op(name, f, (XATTN_HEADS, s // tq), [q_spec, k_spec, v_spec], [(_sds((s, width)), q_spec)],
                                   (q, kv, kv), (0, 1, 2), acc={1: (1,), 2: (1,)}, gdefs={1: (half, k_spec), 2: (half, k_spec)})

    @jax.custom_vjp
    def op(q, kv):
        return fwd_call(q, kv, kv)[0]

    def op_b(res, ct):
        q, kv = res
        dq, dk, dv = bwd_call(q, kv, kv, ct)
        return dq, jnp.concatenate([dk, dv], axis=1)

    op.defvjp(lambda q, kv: (fwd_call(q, kv, kv)[0], (q, kv)), op_b)
    return op(q, kv)


def _hi(a, b, form="nn"):
    dims = {"nn": ((1,), (0,)), "nt": ((1,), (1,)), "tn": ((0,), (0,))}[form]
    return lax.dot_general(a, b, (dims, ((), ())), precision=lax.Precision.HIGH, preferred_element_type=f32)


def _running_sum(g):
    def raw(x, form):
        c = x.shape[0]
        tri = (lax.broadcasted_iota(jnp.int32, (c, c), 0) >= lax.broadcasted_iota(jnp.int32, (c, c), 1)).astype(bf16)
        hi = x.astype(bf16)
        rest = x - hi.astype(f32)
        mid = rest.astype(bf16)
        low = (rest - mid.astype(f32)).astype(bf16)
        dims = (((1,) if form == "nn" else (0,), (0,)), ((), ()))
        return sum(lax.dot_general(tri, part, dims, preferred_element_type=f32) for part in (hi, mid, low))

    @jax.custom_vjp
    def run(x):
        return raw(x, "nn")

    run.defvjp(lambda x: (raw(x, "nn"), None), lambda _, ct: (raw(ct, "tn"),))
    return run(g)


def _unit_lower_inverse(a):
    c = a.shape[0]
    eye = (lax.broadcasted_iota(jnp.int32, (c, c), 0) == lax.broadcasted_iota(jnp.int32, (c, c), 1)).astype(f32)
    inv, power = eye - a, -a
    for _ in range(c.bit_length() - 2):
        power = _hi(power, power)
        inv = inv + _hi(inv, power)
    return inv


def _known_inverse(a, t):
    @jax.custom_vjp
    def inv(a, t):
        return t

    def inv_b(t, ct):
        return -_hi(_hi(t, ct, "tn"), t, "nt"), jnp.zeros_like(t)

    inv.defvjp(lambda a, t: (t, t), inv_b)
    return inv(a, t)


def _delta_chunk(q, k, v, g, beta, s0, known_inv=None):
    c = q.shape[0]
    i = lax.broadcasted_iota(jnp.int32, (c, c), 0)
    j = lax.broadcasted_iota(jnp.int32, (c, c), 1)
    causal, strict = i >= j, i > j
    dec = _running_sum(g)
    dec_i = dec[:, :c]
    rel = jnp.exp(jnp.where(causal, dec_i - dec_i.T, -jnp.inf))
    k_beta = k * beta
    on_k = _mxu(jnp.concatenate([k_beta, q], axis=0), k, "nt")
    a = jnp.where(strict, on_k[:c] * rel, 0.0)
    attn = jnp.where(causal, on_k[c:] * rel, 0.0)
    inv = _unit_lower_inverse(a) if known_inv is None else _known_inverse(a, known_inv)
    e_dec = jnp.exp(dec)
    solved = _hi(inv, jnp.concatenate([v * beta, k_beta * e_dec], axis=1))
    u, w = solved[:, :v.shape[1]], solved[:, v.shape[1]:]
    total = jnp.sum(g, axis=0, keepdims=True)
    on_state = _mxu(jnp.concatenate([w, q * e_dec], axis=0), s0, "nn")
    v_new = u - on_state[:c]
    o = on_state[c:] + _mxu(attn, v_new, "nn")
    s1 = s0 * jnp.exp(total) + _mxu(k * jnp.exp(total - dec), v_new, "tn")
    return o, s1, inv


def _delta_rule_call(name, walk, n, in_specs, out_specs, out_shape, operands, exchange):
    n_in, n_out = len(in_specs), len(out_specs)
    carried = len(exchange.operands) if exchange else 0

    def body(*refs):
        ins, refs = refs[:n_in], refs[n_in:]
        x_refs, refs = refs[:carried], refs[carried:]
        outs, refs = refs[:n_out], refs[n_out:]
        land_refs, (state, *sems) = refs[:carried], refs[carried:]
        step = pl.program_id(0)
        if exchange:
            start, finish = exchange.bind(x_refs, land_refs, sems)
            pl.when(step == 0)(start)

        @pl.when(step == 0)
        def _():
            state[...] = jnp.zeros_like(state)

        walk(ins, outs, state)
        if exchange:
            pl.when(step == n - 1)(finish)

    return pl.pallas_call(
        body, name=name, grid=(n,), in_specs=list(in_specs) + [ANY] * carried, out_specs=list(out_specs) + [ANY] * carried,
        out_shape=list(out_shape) + (exchange.out_shapes if exchange else []),
        scratch_shapes=[pltpu.VMEM((GDN_HEAD_DIM, GDN_WIDTH), f32)] + (exchange.scratch if exchange else []),
        compiler_params=_params(1))(*operands, *(exchange.operands if exchange else []))


def _delta_heads():
    heads = [slice(hd * GDN_HEAD_DIM, (hd + 1) * GDN_HEAD_DIM) for hd in range(GDN_HEADS)]
    inv_at = [slice(hd * GDN_CHUNK, (hd + 1) * GDN_CHUNK) for hd in range(GDN_HEADS)]
    return heads, inv_at


def _head_chunk(q, k, v, gates, s0, head, known_inv=None):
    g = jnp.broadcast_to(gates[:, head:head + 1], q.shape)
    beta = jnp.broadcast_to(gates[:, GDN_HEADS + head:GDN_HEADS + head + 1], q.shape)
    return _delta_chunk(q, k, v, g, beta, s0, known_inv)


def delta_rule_fwd(q, k, v, gates, name, exchange=None):
    s, width = q.shape
    c, dk = GDN_CHUNK, GDN_HEAD_DIM
    n = s // c
    heads, inv_at = _delta_heads()

    def walk(ins, outs, state):
        q_ref, k_ref, v_ref, gates_ref = ins
        o_ref, s_in_ref, inv_ref = outs
        s_in_ref[...] = state[...]
        gates = gates_ref[...]
        xs = [[r[:, hd] for r in (q_ref, k_ref, v_ref)] + [gates, state[:, hd], i] for i, hd in enumerate(heads)]
        ys = [_head_chunk(*x) for x in xs]
        for hd, at, (o, s1, inv) in zip(heads, inv_at, ys):
            o_ref[:, hd], state[:, hd], inv_ref[:, at] = o, s1, inv

    blk = pl.BlockSpec((c, width), lambda t: (t, 0))
    gt = pl.BlockSpec((c, LANES), lambda t: (t, 0))
    st = pl.BlockSpec((dk, width), lambda t: (t, 0))
    iv = pl.BlockSpec((c, GDN_HEADS * c), lambda t: (t, 0))
    return _delta_rule_call(name, walk, n, [blk] * 3 + [gt], [blk, st, iv],
                            [_sds((s, width)), _sds((n * dk, width)), _sds((s, GDN_HEADS * c))], (q, k, v, gates), exchange)


def delta_rule_bwd(q, k, v, gates, s_in, inv, do, name, exchange=None):
    s, width = q.shape
    c, dk = GDN_CHUNK, GDN_HEAD_DIM
    n = s // c
    heads, inv_at = _delta_heads()

    def walk(ins, outs, dstate):
        q_ref, k_ref, v_ref, gates_ref, s_ref, inv_ref, do_ref = ins
        dq_ref, dk_ref, dv_ref, dgates_ref = outs
        gates = gates_ref[...]
        xs = [[r[:, hd] for r in (q_ref, k_ref, v_ref)] + [gates, s_ref[:, hd]] for hd in heads]
        known = [inv_ref[:, at] for at in inv_at]
        cts = [(do_ref[:, hd], dstate[:, hd]) for hd in heads]
        grads = []
        for i, (x, t, ct) in enumerate(zip(xs, known, cts)):
            _, vjp = jax.vjp(lambda *y, t=t, i=i: _head_chunk(*y, i, known_inv=t)[:2], *x)
            grads.append(vjp(ct))
        dgates = grads[0][3]
        for g in grads[1:]:
            dgates = dgates + g[3]
        dgates_ref[...] = dgates
        for hd, (dq, dk_, dv, _, ds0) in zip(heads, grads):
            dq_ref[:, hd], dk_ref[:, hd], dv_ref[:, hd], dstate[:, hd] = dq, dk_, dv, ds0

    blk = pl.BlockSpec((c, width), lambda t: (n - 1 - t, 0))
    gt = pl.BlockSpec((c, LANES), lambda t: (n - 1 - t, 0))
    st = pl.BlockSpec((dk, width), lambda t: (n - 1 - t, 0))
    iv = pl.BlockSpec((c, GDN_HEADS * c), lambda t: (n - 1 - t, 0))
    return _delta_rule_call(name, walk, n, [blk] * 3 + [gt, st, iv, blk], [blk] * 3 + [gt],
                            [_sds((s, width))] * 3 + [_sds((s, LANES))], (q, k, v, gates, s_in, inv, do), exchange)


def adamw(w, g, m, v, name):
    shape = w.shape
    if len(shape) == 2:
        grid, spec = (1,), pl.BlockSpec(shape, lambda i: (0, 0))
    else:
        tile = shape[1] if shape[1] <= 512 else _pick(shape[1], (512, 256, 128))
        grid, spec = (shape[0], shape[1] // tile), pl.BlockSpec((None, tile, shape[2]), lambda layer, i: (layer, i, 0))

    def body(w_ref, g_ref, m_ref, v_ref, d_ref, nm_ref, nv_ref):
        grad = g_ref[...]
        nm = ADAM_B1 * m_ref[...] + (1.0 - ADAM_B1) * grad
        nv = ADAM_B2 * v_ref[...] + (1.0 - ADAM_B2) * (grad * grad)
        m_hat = nm / (1.0 - ADAM_B1 ** ADAM_STEP)
        v_hat = nv / (1.0 - ADAM_B2 ** ADAM_STEP)
        d_ref[...] = -ADAM_LR * (m_hat / (jnp.sqrt(v_hat) + ADAM_EPS) + ADAM_WD * w_ref[...])
        nm_ref[...] = nm
        nv_ref[...] = nv

    return tuple(pl.pallas_call(body, name=name, grid=grid, in_specs=[spec] * 4, out_specs=[spec] * 3,
                                out_shape=[_sds(shape)] * 3, compiler_params=_params(len(grid)))(w, g, m, v))


def _place():
    return lax.axis_index("x"), lax.axis_index("y"), lax.axis_index("c")


def _flip(p, bits):
    return tuple(1 - v if (bits >> s) & 1 else v for v, s in zip(p, (2, 1, 0)))


def _slot(p):
    return 4 * p[0] + 2 * p[1] + p[2]


def _chip_of(p):
    return 2 * p[0] + p[1]


ANY = pl.BlockSpec(memory_space=pl.ANY)


class Gather:
    scratch = (pltpu.SemaphoreType.DMA((7,)), pltpu.SemaphoreType.DMA((7,)), pltpu.SemaphoreType.DMA)

    def __init__(self, shard):
        self.operand = shard
        self.out_shape = jax.ShapeDtypeStruct((N_DEV,) + shard.shape, shard.dtype)

    def bind(self, x_ref, out_ref, send_sems, recv_sems, local_sem):
        me = _place()
        sibling = _flip(me, 1)
        chips = [_flip(me, 4), _flip(me, 2), _flip(me, 6)]

        def copy(k, block, to, src=None):
            return pltpu.make_async_remote_copy(
                src_ref=out_ref.at[_slot(block)] if src is None else src, dst_ref=out_ref.at[_slot(block)],
                send_sem=send_sems.at[k], recv_sem=recv_sems.at[k], device_id=to, device_id_type=MESH)

        mine = pltpu.make_async_copy(x_ref, out_ref.at[_slot(me)], local_sem)
        first = [copy(0, me, sibling, src=x_ref)] + [copy(1 + j, me, chip, src=x_ref) for j, chip in enumerate(chips)]
        passed = [copy(4 + j, chip, sibling) for j, chip in enumerate(chips)]

        def start():
            mine.start()
            for cp in first:
                cp.start()

        def finish():
            for j, chip in enumerate(chips):
                copy(1 + j, chip, me).wait_recv()
                passed[j].start()
            copy(0, sibling, me).wait_recv()
            for j, chip in enumerate(chips):
                copy(4 + j, _flip(chip, 1), me).wait_recv()
            for cp in first + passed:
                cp.wait_send()
            mine.wait()

        return start, finish


class ChipExchange:
    scratch = (pltpu.SemaphoreType.DMA((3,)), pltpu.SemaphoreType.DMA((3,)), pltpu.SemaphoreType.DMA)

    def __init__(self, blocks):
        self.operand = blocks
        self.out_shape = jax.ShapeDtypeStruct(blocks.shape, blocks.dtype)

    def bind(self, x_ref, out_ref, send_sems, recv_sems, local_sem):
        me = _place()
        peers = [_flip(me, 4), _flip(me, 2), _flip(me, 6)]
        mine = pltpu.make_async_copy(x_ref.at[_chip_of(me)], out_ref.at[_chip_of(me)], local_sem)

        def copy(j, src_chip, dst_chip):
            return pltpu.make_async_remote_copy(
                src_ref=x_ref.at[src_chip], dst_ref=out_ref.at[dst_chip], send_sem=send_sems.at[j],
                recv_sem=recv_sems.at[j], device_id=peers[j], device_id_type=MESH)

        sends = [copy(j, _chip_of(peer), _chip_of(me)) for j, peer in enumerate(peers)]

        def start():
            mine.start()
            for cp in sends:
                cp.start()

        def finish():
            for j, peer in enumerate(peers):
                copy(j, _chip_of(me), _chip_of(peer)).wait_recv()
            for cp in sends:
                cp.wait_send()
            mine.wait()

        return start, finish


class Together:
    def __init__(self, *parts):
        self.parts = parts
        self.operands = [p.operand for p in parts]
        self.out_shapes = [p.out_shape for p in parts]
        self.scratch = [s for p in parts for s in p.scratch]

    def bind(self, x_refs, out_refs, sems):
        bound, at = [], 0
        for p, x_ref, out_ref in zip(self.parts, x_refs, out_refs):
            bound.append(p.bind(x_ref, out_ref, *sems[at:at + len(p.scratch)]))
            at += len(p.scratch)

        def start():
            for s, _ in bound:
                s()

        def finish():
            for _, f in bound:
                f()

        return start, finish


def exchange_alone(exchange, name):
    n = len(exchange.operands)

    def body(*refs):
        start, finish = exchange.bind(refs[:n], refs[n:2 * n], refs[2 * n:])
        start()
        finish()

    return pl.pallas_call(body, name=name, out_shape=exchange.out_shapes, in_specs=[ANY] * n, out_specs=[ANY] * n,
                          scratch_shapes=exchange.scratch)(*exchange.operands)


def _row_tile(rows):
    return max([t for t in range(16, min(rows, 1024) + 1, 16) if rows % t == 0] or [rows])


def pair_exchange(blocks, name):
    n = len(blocks)

    def body(*refs):
        x_refs, theirs_refs, (send_sems, recv_sems) = refs[:n], refs[n:2 * n], refs[2 * n:]
        me = _place()
        remote = [pltpu.make_async_remote_copy(
            src_ref=x_refs[t].at[2 * q + 1 - me[2]], dst_ref=theirs_refs[t].at[q], send_sem=send_sems.at[4 * t + q],
            recv_sem=recv_sems.at[4 * t + q], device_id=_flip(me, 1), device_id_type=MESH) for t in range(n) for q in range(4)]
        for cp in remote:
            cp.start()
        for cp in remote:
            cp.wait()

    return pl.pallas_call(
        body, name=name, out_shape=[jax.ShapeDtypeStruct((4,) + b.shape[1:], b.dtype) for b in blocks], in_specs=[ANY] * n,
        out_specs=[ANY] * n, scratch_shapes=[pltpu.SemaphoreType.DMA((4 * n,)), pltpu.SemaphoreType.DMA((4 * n,))])(*blocks)


def pair_add(blocks, theirs, name):
    n, rows, width = theirs.shape
    tile = _row_tile(rows)
    spec = pl.BlockSpec((None, tile, width), lambda q, i: (q, i, 0))
    south = pl.BlockSpec((None, None, tile, width), lambda q, i: (q, 0, i, 0))
    north = pl.BlockSpec((None, None, tile, width), lambda q, i: (q, 1, i, 0))

    def body(s_ref, n_ref, b_ref, o_ref):
        mine = jnp.where(lax.axis_index("c") == 0, s_ref[...], n_ref[...])
        o_ref[...] = (mine.astype(f32) + b_ref[...].astype(f32)).astype(o_ref.dtype)

    by_core = blocks.reshape(n, 2, rows, width)
    return pl.pallas_call(body, name=name, grid=(n, rows // tile), in_specs=[south, north, spec], out_specs=spec,
                          out_shape=jax.ShapeDtypeStruct(theirs.shape, theirs.dtype), compiler_params=_params(2))(by_core, by_core, theirs)


def sum_slots(blocks, name):
    n, rows, width = blocks.shape
    tile = _row_tile(rows)

    def body(x_ref, o_ref):
        total = x_ref[0].astype(f32)
        for s in range(1, n):
            total = total + x_ref[s].astype(f32)
        o_ref[...] = total

    return pl.pallas_call(
        body, name=name, grid=(rows // tile,), in_specs=[pl.BlockSpec((n, tile, width), lambda i: (0, i, 0))],
        out_specs=pl.BlockSpec((tile, width), lambda i: (i, 0)), out_shape=_sds((rows, width)), compiler_params=_params(1))(blocks)


def all_reduce_small(x, name):
    rows, width = x.shape

    def body(x_ref, o_ref, land, send_sems, recv_sems):
        me = _place()
        copies = []
        for k in range(1, N_DEV):
            peer = _flip(me, k)
            copies.append(pltpu.make_async_remote_copy(
                src_ref=x_ref, dst_ref=land.at[_slot(me)], send_sem=send_sems.at[k - 1], recv_sem=recv_sems.at[k - 1],
                device_id=peer, device_id_type=MESH))
        for cp in copies:
            cp.start()
        land[_slot(me)] = x_ref[...]
        for k in range(1, N_DEV):
            peer = _flip(me, k)
            pltpu.make_async_remote_copy(
                src_ref=x_ref, dst_ref=land.at[_slot(peer)], send_sem=send_sems.at[k - 1], recv_sem=recv_sems.at[k - 1],
                device_id=peer, device_id_type=MESH).wait_recv()
        total = land[0]
        for s in range(1, N_DEV):
            total = total + land[s]
        o_ref[...] = total
        for cp in copies:
            cp.wait_send()

    return pl.pallas_call(
        body, name=name, out_shape=_sds((rows, width)), in_specs=[pl.BlockSpec(memory_space=pltpu.VMEM)],
        out_specs=pl.BlockSpec(memory_space=pltpu.VMEM),
        scratch_shapes=[pltpu.VMEM((N_DEV, rows, width), f32), pltpu.SemaphoreType.DMA((7,)), pltpu.SemaphoreType.DMA((7,))],
    )(x)


def _pack_big(shards):
    packed = {name: shards[name].astype(bf16) for name in COL_SHARDED}
    packed["rows"] = jnp.concatenate([shards[name].astype(bf16) for name, _ in ROW_SHARDED], axis=1)
    return packed


def _unpack_gathered(gathered):
    full = {}
    for name, part in gathered.items():
        if name in COL_SHARDED:
            full[name] = part.transpose(1, 0, 2).reshape(D_MODEL, N_DEV * part.shape[2])
        else:
            at = 0
            for weight, rows in ROW_SHARDED:
                full[weight] = part[:, at:at + rows, :].reshape(N_DEV * rows, D_MODEL)
                at += rows
    if "w_gate_up" in full:
        full["w_gate_up"] = _interleave_gate_up(full["w_gate_up"])
    if "w_in" in full:
        w_in = full.pop("w_in")
        full["w_main"] = jnp.concatenate([w_in[:, :AB_AT], w_in[:, AB_AT + 2 * GDN_HEADS:]], axis=1)
        full["w_ab"] = jnp.pad(w_in[:, AB_AT:AB_AT + 2 * GDN_HEADS], ((0, 0), (0, LANES - 2 * GDN_HEADS)))
    return full


def _pack_grads(grads, group):
    packed = {}
    for name in group:
        if name == "w_in":
            main, ab = grads["w_main"], grads["w_ab"]
            g = jnp.concatenate([main[:, :AB_AT], ab[:, :2 * GDN_HEADS], main[:, AB_AT:]], axis=1)
        elif name == "w_gate_up":
            g = _interleave_gate_up(grads[name], undo=True)
        elif name == "rows":
            packed[name] = jnp.concatenate([grads[weight].reshape(N_DEV, rows, D_MODEL) for weight, rows in ROW_SHARDED], axis=1)
            continue
        else:
            g = grads[name]
        packed[name] = g.reshape(D_MODEL, N_DEV, g.shape[1] // N_DEV).transpose(1, 0, 2)
    return packed


def _unpack_shard(layers):
    out = {name: jnp.stack([layer[name] for layer in layers]) for name in COL_SHARDED}
    rows_pack, at = jnp.stack([layer["rows"] for layer in layers]), 0
    for weight, rows in ROW_SHARDED:
        out[weight] = rows_pack[:, at:at + rows, :]
        at += rows
    return out


def _rows_of(flat_len):
    return -(-flat_len // (8 * D_MODEL)) * 8


def _pack_small(parts):
    flat = jnp.concatenate([p.reshape(-1) for p in parts])
    rows = _rows_of(flat.shape[0])
    flat = jnp.pad(flat, (0, rows * D_MODEL - flat.shape[0]))
    return flat.reshape(rows, D_MODEL)


def _unpack_small(packed, like):
    flat, out, at = packed.reshape(-1), [], 0
    for p in like:
        out.append(flat[at:at + p.size].reshape(p.shape))
        at += p.size
    return out


def _rope_tables(positions):
    inv_freq = jnp.float32(ROPE_THETA) ** (-jnp.arange(0, ROPE_DIM, 2, dtype=f32) / ROPE_DIM)
    ang = positions.astype(f32)[:, None] * inv_freq
    cos, sin = jnp.cos(ang), jnp.sin(ang)
    rest = ATTN_HEAD_DIM - ROPE_DIM
    cos_h = jnp.concatenate([cos, cos, jnp.ones((cos.shape[0], rest), f32)], axis=1)
    sin_h = jnp.concatenate([-sin, sin, jnp.zeros((sin.shape[0], rest), f32)], axis=1)
    return jnp.tile(cos_h, (1, ATTN_HEADS)), jnp.tile(sin_h, (1, ATTN_HEADS))


HEAD_SMALL = ("norm_mix_pre", "conv_short", "conv_gdn", "gdn_a_log", "gdn_dt_bias")


def _layer_head(h, p, cos_t, sin_t):
    hn = rms_norm(h, p["norm_mix_pre"][None], "norm_mix_pre")
    proj = _linear(hn, p["w_main"], "w_main")
    ab = _linear(hn, p["w_ab"], "w_ab")
    aw, cw, gw = ATTN_WIDTH, CONV_WIDTH, GDN_WIDTH
    aq, ak, av, cb, cc, cx, gqkv, gate = _split_cols(proj, (aw, aw, aw, cw, cw, cw, 3 * gw, gw))
    y_attn = dilated_attention(rope(aq, cos_t, sin_t, ATTN_HEAD_DIM ** -0.5, "rope_q"), rope(ak, cos_t, sin_t, 1.0, "rope_k"),
                               av, "attn")
    y_conv = short_conv(cb, cc, cx, p["conv_short"], "short_conv")
    qkv = gdn_pre(gqkv, p["conv_gdn"], "gdn_pre")
    pv = jnp.zeros((8, LANES), f32).at[0, :GDN_HEADS].set(p["gdn_a_log"]).at[1, :GDN_HEADS].set(p["gdn_dt_bias"])
    return (*_split_cols(qkv, (gw, gw, gw)), gate_beta(ab, pv, "gate_beta")), (gate, y_attn, y_conv)


MID_PARAMS = ("gdn_norm", "w_out", "norm_mix_post", "norm_xattn_pre", "w_xq", "norm_mem", "w_xkv", "w_xo", "norm_xattn_post",
              "norm_ffn_pre")


def _layer_mid(h, o, gate, y_attn, y_conv, p, mem):
    y_gdn = gdn_post(o, gate, p["gdn_norm"][None], "gdn_post")
    mix = _linear(jnp.concatenate([y_attn, y_conv, y_gdn], axis=1), p["w_out"], "w_out")
    h, hn = add_norm_then_norm(h, mix, p["norm_mix_post"][None], p["norm_xattn_pre"][None], "norm_mix_xattn")
    qx = _linear(hn, p["w_xq"], "w_xq")
    kv = _linear(rms_norm(mem, p["norm_mem"][None], "norm_mem"), p["w_xkv"], "w_xkv")
    xa = _linear(cross_attention(qx, kv, "xattn"), p["w_xo"], "w_xo")
    return add_norm_then_norm(h, xa, p["norm_xattn_post"][None], p["norm_ffn_pre"][None], "norm_xattn_ffn")


def _pair_summed(grads, group, name):
    blocks = _pack_grads(grads, group)
    theirs = pair_exchange([blocks[n] for n in group], name + "_pair_exchange")
    return [pair_add(blocks[n], t, f"{name}_pair_add_{n}") for n, t in zip(group, theirs)]


def _forward_backward(x, packed, small, mem, cos_t, sin_t, target):
    def gathers(group, layer):
        return [Gather(packed[n][layer]) for n in group]

    h = x
    head_gathered = exchange_alone(Together(*gathers(HEAD_GROUP, 0)), "gather_first")
    saved = []
    for layer in range(DEPTH):
        at_layer = {n: t[layer] for n, t in small.items()}
        head_p = {**_unpack_gathered(dict(zip(HEAD_GROUP, head_gathered))), **{n: at_layer[n] for n in HEAD_SMALL}}
        (rule_in, rest), head_vjp = jax.vjp(lambda h, hp: _layer_head(h, hp, cos_t, sin_t), h, head_p)
        carried = gathers(TAIL_GROUP, layer) + (gathers(HEAD_GROUP, layer + 1) if layer + 1 < DEPTH else [])
        o, s_in, inv, *landed = delta_rule_fwd(*rule_in, "delta_rule_fwd", Together(*carried))
        head_gathered = landed[len(TAIL_GROUP):]
        tail_p = {**_unpack_gathered(dict(zip(TAIL_GROUP, landed))), **at_layer}
        mid_p = {n: tail_p[n] for n in MID_PARAMS}
        (h, hn), mid_vjp = jax.vjp(lambda h, o, rest, mp: _layer_mid(h, o, *rest, mp, mem), h, o, rest, mid_p)
        y, ffn_saved = ffn_forward(hn, tail_p["w_gate_up"], tail_p["w_down"], "ffn")
        h, last_vjp = jax.vjp(lambda h, y, w: add_norm(h, y, w[None], "norm_ffn_post"), h, y, tail_p["norm_ffn_post"])
        saved.append((head_vjp, mid_vjp, last_vjp, ffn_saved, rule_in, s_in, inv))

    loss, dh = jax.value_and_grad(lambda y: loss_rows(y, target, "loss"))(h)

    def summed(group, landed):
        return {n: sum_slots(t, "sum_grads_" + n) for n, t in zip(group, landed)}

    big_grads, small_grads, head_pending = [{} for _ in range(DEPTH)], [None] * DEPTH, []
    for layer in reversed(range(DEPTH)):
        head_vjp, mid_vjp, last_vjp, ffn_saved, rule_in, s_in, inv = saved[layer]
        dh, dy, d_norm_ffn_post = last_vjp(dh)
        dhn, d_gate_up, d_down, landed = ffn_backward(
            ffn_saved, dy, "ffn", Together(*[ChipExchange(t) for t in head_pending]) if head_pending else None)
        if head_pending:
            big_grads[layer + 1].update(summed(HEAD_GROUP, landed))
        dh_mid, do, d_rest, d_mid_p = mid_vjp((dh, dhn))
        d_tail_p = {**d_mid_p, "w_gate_up": d_gate_up, "w_down": d_down, "norm_ffn_post": d_norm_ffn_post}
        carried = Together(*[ChipExchange(t) for t in _pair_summed(d_tail_p, TAIL_GROUP, "tail")])
        *d_rule_in, = delta_rule_bwd(*rule_in, s_in, inv, do, "delta_rule_bwd", carried)
        big_grads[layer].update(summed(TAIL_GROUP, d_rule_in[4:]))
        dh_head, d_head_p = head_vjp((tuple(d_rule_in[:4]), d_rest))
        dh = dh_mid + dh_head
        small_grads[layer] = {n: t for n, t in {**d_head_p, **d_tail_p}.items() if n in small}
        head_pending = _pair_summed(d_head_p, HEAD_GROUP, "head")
    landed = exchange_alone(Together(*[ChipExchange(t) for t in head_pending]), "exchange_last")
    big_grads[0].update(summed(HEAD_GROUP, landed))
    return loss, dh, big_grads, small_grads


def kernel(x, mem, positions, norm_mix_pre, norm_mix_post, w_in, conv_short, conv_gdn, gdn_a_log, gdn_dt_bias, gdn_norm, w_out, norm_mem, norm_xattn_pre, norm_xattn_post, w_xq, w_xkv, w_xo, norm_ffn_pre, norm_ffn_post, w_gate_up, w_down, loss_target, m_norm_mix_pre, m_norm_mix_post, m_w_in, m_conv_short, m_conv_gdn, m_gdn_a_log, m_gdn_dt_bias, m_gdn_norm, m_w_out, m_norm_mem, m_norm_xattn_pre, m_norm_xattn_post, m_w_xq, m_w_xkv, m_w_xo, m_norm_ffn_pre, m_norm_ffn_post, m_w_gate_up, m_w_down, v_norm_mix_pre, v_norm_mix_post, v_w_in, v_conv_short, v_conv_gdn, v_gdn_a_log, v_gdn_dt_bias, v_gdn_norm, v_w_out, v_norm_mem, v_norm_xattn_pre, v_norm_xattn_post, v_w_xq, v_w_xkv, v_w_xo, v_norm_ffn_pre, v_norm_ffn_post, v_w_gate_up, v_w_down):
    given = dict(locals())
    weights = {n: given[n] for n in WEIGHTS}
    me = _slot(_place())

    def in_place(shard):
        full = jnp.zeros(shard.shape[:-1] + (shard.shape[-1] * N_DEV,), f32)
        return lax.dynamic_update_slice_in_dim(full, shard, me * shard.shape[-1], axis=shard.ndim - 1)

    placed = [in_place(conv_short), in_place(conv_gdn)]
    conv_short_full, conv_gdn_full = _unpack_small(all_reduce_small(_pack_small(placed), "gather_conv"), placed)
    small = {n: weights[n] for n in NORMS + ("gdn_a_log", "gdn_dt_bias", "gdn_norm")}
    small["conv_short"], small["conv_gdn"] = conv_short_full, conv_gdn_full

    cos_t, sin_t = _rope_tables(positions[0])
    loss, grad_x, big_layers, small_layers = _forward_backward(
        x[0], _pack_big(weights), small, mem[0], cos_t, sin_t, loss_target[0])
    grads = _unpack_shard(big_layers)

    names = sorted(small)
    parts = [jnp.stack([layer[n] for layer in small_layers]) for n in names] + [loss.reshape(1)]
    reduced = _unpack_small(all_reduce_small(_pack_small(parts), "reduce_small"), parts)
    loss = reduced[-1][0]
    for n, g in zip(names, reduced[:-1]):
        if n in ("conv_short", "conv_gdn"):
            width = weights[n].shape[-1]
            g = lax.dynamic_slice_in_dim(g, me * width, width, axis=g.ndim - 1)
        grads[n] = g

    delta, new_m, new_v = {}, {}, {}
    for n in WEIGHTS:
        delta[n], new_m[n], new_v[n] = adamw(weights[n], grads[n], given["m_" + n], given["v_" + n], "adamw_" + n)
    return (loss, grad_x[None], *[grads[n] for n in WEIGHTS], *[delta[n] for n in WEIGHTS],
            *[new_m[n] for n in WEIGHTS], *[new_v[n] for n in WEIGHTS])
```

```python
import functools

import jax
import jax.numpy as jnp
from jax import lax
from jax.experimental import pallas as pl
from jax.experimental.pallas import tpu as pltpu

f32 = jnp.float32
bf16 = jnp.bfloat16
HIGHEST = lax.Precision.HIGHEST
MESH = pl.DeviceIdType.MESH

N_DEV = 8
DEPTH = 4
D_MODEL = 1024
EPS = 1e-6
ATTN_HEADS, ATTN_HEAD_DIM = 4, 64
ATTN_WIDTH = ATTN_HEADS * ATTN_HEAD_DIM
DILATIONS = (1, 4, 16)
QB = 128
ROPE_THETA = 500000.0
ROPE_DIM = ATTN_HEAD_DIM // 4
CONV_WIDTH = 256
GDN_HEADS, GDN_HEAD_DIM = 4, 128
GDN_WIDTH = GDN_HEADS * GDN_HEAD_DIM
GDN_CHUNK = 64
XATTN_HEADS, XATTN_HEAD_DIM = 4, 256
FFN_HIDDEN = 2816
IN_WIDTH = 3592
AB_AT = 3 * ATTN_WIDTH + 3 * CONV_WIDTH + 3 * GDN_WIDTH
MAIN_WIDTH = IN_WIDTH - 2 * GDN_HEADS
LANES = 128
ROW_TILE = 512
VMEM_LIMIT = 56 * 1024 * 1024

ADAM_LR, ADAM_B1, ADAM_B2, ADAM_EPS, ADAM_WD, ADAM_STEP = 0.001, 0.9, 0.999, 1e-08, 0.01, 10

COL_SHARDED = ("w_in", "w_xkv", "w_gate_up")
ROW_SHARDED = (("w_out", 128), ("w_xq", 128), ("w_xo", 128), ("w_down", 352))
HEAD_GROUP = ("w_in",)
TAIL_GROUP = ("w_gate_up", "w_xkv", "rows")
NORMS = ("norm_mix_pre", "norm_mix_post", "norm_mem", "norm_xattn_pre", "norm_xattn_post", "norm_ffn_pre", "norm_ffn_post")
WEIGHTS = ("norm_mix_pre", "norm_mix_post", "w_in", "conv_short", "conv_gdn", "gdn_a_log", "gdn_dt_bias", "gdn_norm", "w_out",
           "norm_mem", "norm_xattn_pre", "norm_xattn_post", "w_xq", "w_xkv", "w_xo", "norm_ffn_pre", "norm_ffn_post",
           "w_gate_up", "w_down")


def _params(n_grid):
    return pltpu.CompilerParams(dimension_semantics=("arbitrary",) * n_grid, vmem_limit_bytes=VMEM_LIMIT)


def _pick(n, cands):
    for c in cands:
        if n % c == 0:
            return c
    return n


MXU_FLOPS = 9.0e14
HBM_BYTES_PER_S = 2.5e12
VMEM_RMW_BYTES_PER_S = 7.0e12
STEP_S = 0.4e-6
MATMUL_VMEM = 44 * 1024 * 1024


def _tiles(m, n, k, sa, sb, so):
    def divisors(d):
        return sorted({d // s for s in range(1, d // LANES + 1) if d % s == 0 and (d // s) % LANES == 0}, reverse=True)

    best = None
    for tk in divisors(k):
        nk = k // tk
        for tm in divisors(m):
            for tn_ in divisors(n):
                per_step = tm * tk * sa + tk * tn_ * sb + tm * tn_ * so
                vmem = 2 * per_step + (tm * tn_ * 4 if nk > 1 else 0)
                vmem += (tm * tk * 2 if sa == 4 else 0) + (tk * tn_ * 2 if sb == 4 else 0) + tm * tn_ * 4
                if vmem > MATMUL_VMEM:
                    continue
                moved = m * k * sa * (1 if nk == 1 else n // tn_) + k * n * sb * (1 if nk == 1 and n == tn_ else m // tm) + m * n * so
                busy = 2 * m * n * k / MXU_FLOPS + (m * n * 8 * nk / VMEM_RMW_BYTES_PER_S if nk > 1 else 0)
                cost = max(moved / HBM_BYTES_PER_S, busy) + per_step / HBM_BYTES_PER_S + (m // tm) * (n // tn_) * nk * STEP_S
                if best is None or cost < best[0]:
                    best = (cost, tm, tn_, tk)
    return best[1:]


def _mm(a, b, ta, tb, out_dtype, name):
    m, k = (a.shape[1], a.shape[0]) if ta else a.shape
    n = b.shape[0] if tb else b.shape[1]
    tm, tn, tk = _tiles(m, n, k, a.dtype.itemsize, b.dtype.itemsize, jnp.dtype(out_dtype).itemsize)
    nk = k // tk
    a_spec = pl.BlockSpec((tk, tm), lambda i, j, kk: (kk, i)) if ta else pl.BlockSpec((tm, tk), lambda i, j, kk: (i, kk))
    b_spec = pl.BlockSpec((tn, tk), lambda i, j, kk: (j, kk)) if tb else pl.BlockSpec((tk, tn), lambda i, j, kk: (kk, j))
    dims = (((0 if ta else 1,), (1 if tb else 0,)), ((), ()))

    def body(a_ref, b_ref, o_ref, *acc):
        kk = pl.program_id(2)
        p = lax.dot_general(a_ref[...].astype(bf16), b_ref[...].astype(bf16), dims, preferred_element_type=f32)
        if nk == 1:
            o_ref[...] = p.astype(o_ref.dtype)
            return
        acc_ref, = acc

        @pl.when(kk == 0)
        def _():
            acc_ref[...] = p

        @pl.when(kk > 0)
        def _():
            acc_ref[...] += p

        @pl.when(kk == nk - 1)
        def _():
            o_ref[...] = acc_ref[...].astype(o_ref.dtype)

    return pl.pallas_call(
        body, name=name, grid=(m // tm, n // tn, nk), in_specs=[a_spec, b_spec],
        out_specs=pl.BlockSpec((tm, tn), lambda i, j, kk: (i, j)), out_shape=jax.ShapeDtypeStruct((m, n), out_dtype),
        scratch_shapes=[pltpu.VMEM((tm, tn), f32)] if nk > 1 else [], compiler_params=_params(3))(a, b)


def _linear(x, w, name):
    @jax.custom_vjp
    def lin(x, w):
        return _mm(x, w, False, False, f32, name + "_y")

    def lin_f(x, w):
        return _mm(x, w, False, False, f32, name + "_y"), (x, w)

    def lin_b(res, dy):
        x, w = res
        return _mm(dy, w, False, True, f32, name + "_dx"), _mm(x, dy, True, False, bf16, name + "_dw")

    lin.defvjp(lin_f, lin_b)
    return lin(x, w)


def _bdot(a, b, form):
    dims = {"nn": ((1,), (0,)), "nt": ((1,), (1,)), "tn": ((0,), (0,))}[form]
    return lax.dot_general(a.astype(bf16), b.astype(bf16), (dims, ((), ())), preferred_element_type=f32)


def ffn_forward(hn, w_gate_up, w_down, name):
    s, k = hn.shape
    n_blocks, _, width = w_gate_up.shape
    half = n_blocks // 2
    tm = 512
    blocked = jax.ShapeDtypeStruct((half, s, width), bf16)

    def act_body(x_ref, wg_ref, wu_ref, gate_ref, up_ref, act_ref):
        x = x_ref[...]
        gate, up = _bdot(x, wg_ref[...], "nn"), _bdot(x, wu_ref[...], "nn")
        gate_ref[...], up_ref[...] = gate.astype(bf16), up.astype(bf16)
        act_ref[...] = (jax.nn.silu(gate) * up).astype(bf16)

    tile = pl.BlockSpec((None, tm, width), lambda i, d: (d, i, 0))
    gate, up, act = pl.pallas_call(
        act_body, name=name + "_act", grid=(s // tm, half),
        in_specs=[pl.BlockSpec((tm, k), lambda i, d: (i, 0)), pl.BlockSpec((None, k, width), lambda i, d: (d, 0, 0)),
                  pl.BlockSpec((None, k, width), lambda i, d: (d + half, 0, 0))],
        out_specs=[tile] * 3, out_shape=[blocked] * 3, compiler_params=_params(2))(hn, w_gate_up, w_gate_up)

    n = w_down.shape[1]
    tn = 512

    def y_body(act_ref, w_ref, y_ref):
        y_ref[...] = sum(_bdot(act_ref[d], w_ref[d * width:(d + 1) * width, :], "nn") for d in range(half))

    y = pl.pallas_call(
        y_body, name=name + "_y", grid=(s // tm, n // tn),
        in_specs=[pl.BlockSpec((half, tm, width), lambda i, j: (0, i, 0)), pl.BlockSpec((half * width, tn), lambda i, j: (0, j))],
        out_specs=pl.BlockSpec((tm, tn), lambda i, j: (i, j)), out_shape=_sds((s, n)), compiler_params=_params(2))(act, w_down)
    return y, (hn, w_gate_up, w_down, gate, up, act)


def ffn_backward(saved, dy, name, exchange=None):
    hn, w_gate_up, w_down, gate, up, act = saved
    s, k = hn.shape
    n_blocks, _, width = w_gate_up.shape
    half = n_blocks // 2
    n = w_down.shape[1]
    tm = 512
    blocked = jax.ShapeDtypeStruct((half, s, width), bf16)
    carried = len(exchange.operands) if exchange else 0
    steps = (s // tm, half)

    def dact_body(dy_ref, w_ref, gate_ref, up_ref, *refs):
        x_refs, refs = refs[:carried], refs[carried:]
        (dgate_ref, dup_ref), refs = refs[:2], refs[2:]
        if exchange:
            at = pl.program_id(0) * steps[1] + pl.program_id(1)
            start, wait = exchange.bind(x_refs, refs[:carried], refs[carried:])
            pl.when(at == 0)(start)
        d_act = _bdot(dy_ref[...], w_ref[...], "nt")
        g, u = gate_ref[...].astype(f32), up_ref[...].astype(f32)
        sig = jax.nn.sigmoid(g)
        dgate_ref[...] = (d_act * u * sig * (1.0 + g * (1.0 - sig))).astype(bf16)
        dup_ref[...] = (d_act * g * sig).astype(bf16)
        if exchange:
            pl.when(at == steps[0] * steps[1] - 1)(wait)

    tile = pl.BlockSpec((None, tm, width), lambda i, d: (d, i, 0))
    d_gate, d_up, *landed = pl.pallas_call(
        dact_body, name=name + "_dact", grid=steps,
        in_specs=[pl.BlockSpec((tm, n), lambda i, d: (i, 0)), pl.BlockSpec((width, n), lambda i, d: (d, 0)), tile, tile] + [ANY] * carried,
        out_specs=[tile, tile] + [ANY] * carried, out_shape=[blocked, blocked] + (exchange.out_shapes if exchange else []),
        scratch_shapes=exchange.scratch if exchange else [],
        compiler_params=_params(2))(dy, w_down, gate, up, *(exchange.operands if exchange else []))

    def dx_body(dg_ref, du_ref, w_ref, dx_ref):
        dx_ref[...] = sum(_bdot(dg_ref[d], w_ref[d], "nt") + _bdot(du_ref[d], w_ref[d + half], "nt") for d in range(half))

    tx = 256
    rows = pl.BlockSpec((half, tx, width), lambda i: (0, i, 0))
    dx = pl.pallas_call(
        dx_body, name=name + "_dx", grid=(s // tx,), in_specs=[rows, rows, _whole(w_gate_up.shape)],
        out_specs=pl.BlockSpec((tx, k), lambda i: (i, 0)), out_shape=_sds((s, k)), compiler_params=_params(1))(d_gate, d_up, w_gate_up)

    def dw1_body(x_ref, dg_ref, du_ref, dw_ref):
        d_block = jnp.where(pl.program_id(0) < half, dg_ref[...], du_ref[...])
        dw_ref[...] = _bdot(x_ref[...], d_block, "tn").astype(bf16)

    d_w_gate_up = pl.pallas_call(
        dw1_body, name=name + "_dw1", grid=(n_blocks,),
        in_specs=[_whole((s, k)), pl.BlockSpec((None, s, width), lambda b: (jnp.minimum(b, half - 1), 0, 0)),
                  pl.BlockSpec((None, s, width), lambda b: (jnp.maximum(b - half, 0), 0, 0))],
        out_specs=pl.BlockSpec((None, k, width), lambda b: (b, 0, 0)), out_shape=jax.ShapeDtypeStruct(w_gate_up.shape, bf16),
        compiler_params=_params(1))(hn, d_gate, d_up)

    tn = 512

    def dw2_body(act_ref, dy_ref, dw_ref):
        dw_ref[...] = _bdot(act_ref[...], dy_ref[...], "tn").astype(bf16)

    d_w_down = pl.pallas_call(
        dw2_body, name=name + "_dw2", grid=(half, n // tn),
        in_specs=[pl.BlockSpec((None, s, width), lambda d, j: (d, 0, 0)), pl.BlockSpec((s, tn), lambda d, j: (0, j))],
        out_specs=pl.BlockSpec((width, tn), lambda d, j: (d, j)), out_shape=jax.ShapeDtypeStruct(w_down.shape, bf16),
        compiler_params=_params(2))(act, dy)
    return dx, d_w_gate_up, d_w_down, landed


def _split_cols(x, widths):
    edges = [sum(widths[:i]) for i in range(len(widths) + 1)]

    def cut(x):
        return tuple(x[:, a:b] for a, b in zip(edges[:-1], edges[1:]))

    @jax.custom_vjp
    def split(x):
        return cut(x)

    split.defvjp(lambda x: (cut(x), None), lambda _, cts: (jnp.concatenate(cts, axis=1),))
    return split(x)


def _block_op(name, f, grid, in_specs, out_defs, arrays, diff, acc=None, gdefs=None):
    acc, gdefs = acc or {}, gdefs or {}
    n_in, n_out, n_grid = len(in_specs), len(out_defs), len(grid)

    def fwd_call(*xs):
        def body(*refs):
            outs = f(*[r[...] for r in refs[:n_in]])
            for r, o in zip(refs[n_in:], outs):
                r[...] = o.astype(r.dtype)

        return pl.pallas_call(
            body, name=name + "_fwd", grid=grid, in_specs=in_specs, out_specs=[d[1] for d in out_defs],
            out_shape=[d[0] for d in out_defs], compiler_params=_params(n_grid))(*xs)

    def bwd_call(*xs_and_cts):
        def body(*refs):
            xs = [r[...] for r in refs[:n_in]]
            cts = tuple(r[...] for r in refs[n_in:n_in + n_out])

            def of_diff(*dx):
                full = list(xs)
                for i, v in zip(diff, dx):
                    full[i] = v
                return tuple(f(*full))

            _, vjp = jax.vjp(of_diff, *[xs[i] for i in diff])
            grads = vjp(cts)
            for i, g, r in zip(diff, grads, refs[n_in + n_out:]):
                if i in acc:
                    first = functools.reduce(jnp.logical_and, [pl.program_id(a) == 0 for a in acc[i]])

                    @pl.when(first)
                    def _(r=r):
                        r[...] = jnp.zeros_like(r)

                    r[...] += g.astype(r.dtype)
                else:
                    r[...] = g.astype(r.dtype)

        g_defs = [gdefs.get(i, (jax.ShapeDtypeStruct(arrays[i].shape, f32), in_specs[i])) for i in diff]
        return pl.pallas_call(
            body, name=name + "_bwd", grid=grid, in_specs=list(in_specs) + [d[1] for d in out_defs],
            out_specs=[d[1] for d in g_defs], out_shape=[d[0] for d in g_defs], compiler_params=_params(n_grid))(*xs_and_cts)

    return fwd_call, bwd_call


def _simple_op(name, f, grid, in_specs, out_defs, arrays, diff, acc=None):
    fwd_call, bwd_call = _block_op(name, f, grid, in_specs, out_defs, arrays, diff, acc)

    @jax.custom_vjp
    def op(*xs):
        return tuple(fwd_call(*xs))

    def op_f(*xs):
        return tuple(fwd_call(*xs)), xs

    def op_b(xs, cts):
        grads = bwd_call(*xs, *cts)
        out = [jnp.zeros_like(x) for x in xs]
        for i, g in zip(diff, grads):
            out[i] = g
        return tuple(out)

    op.defvjp(op_f, op_b)
    return op(*arrays)


def _rows(width, tile=ROW_TILE):
    return pl.BlockSpec((tile, width), lambda i: (i, 0))


def _whole(shape):
    return pl.BlockSpec(shape, lambda *_: (0,) * len(shape))


def _sds(shape):
    return jax.ShapeDtypeStruct(shape, f32)


def _rms(x, w):
    return x * lax.rsqrt(jnp.mean(x * x, axis=-1, keepdims=True) + EPS) * w


def rms_norm(x, w, name):
    r, d = x.shape
    tile = min(ROW_TILE, r)
    return _simple_op(name, lambda x, w: (_rms(x, w),), (r // tile,), [_rows(d, tile), _whole((1, d))],
                      [(_sds((r, d)), _rows(d, tile))], (x, w), (0, 1), {1: (0,)})[0]


def add_norm(h, y, w, name):
    r, d = h.shape
    return _simple_op(name, lambda h, y, w: (h + _rms(y, w),), (r // ROW_TILE,), [_rows(d), _rows(d), _whole((1, d))],
                      [(_sds((r, d)), _rows(d))], (h, y, w), (0, 1, 2), {2: (0,)})[0]


def add_norm_then_norm(h, y, w_post, w_pre, name):
    r, d = h.shape

    def f(h, y, w_post, w_pre):
        h_new = h + _rms(y, w_post)
        return h_new, _rms(h_new, w_pre)

    return _simple_op(name, f, (r // ROW_TILE,), [_rows(d), _rows(d), _whole((1, d)), _whole((1, d))],
                      [(_sds((r, d)), _rows(d))] * 2, (h, y, w_post, w_pre), (0, 1, 2, 3), {2: (0,), 3: (0,)})


def _swap8(x):
    def raw(x):
        lane = lax.broadcasted_iota(jnp.int32, x.shape, 1) % ATTN_HEAD_DIM
        half = ROPE_DIM // 2
        up = pltpu.roll(x, x.shape[1] - half, axis=1)
        down = pltpu.roll(x, half, axis=1)
        return jnp.where(lane < half, up, jnp.where(lane < ROPE_DIM, down, 0.0))

    @jax.custom_vjp
    def swap(x):
        return raw(x)

    swap.defvjp(lambda x: (raw(x), None), lambda _, g: (raw(g),))
    return swap(x)


def rope(x, cos_t, sin_t, scale, name):
    r, d = x.shape
    return _simple_op(name, lambda x, c, s: ((x * c + _swap8(x) * s) * scale,), (r // ROW_TILE,), [_rows(d)] * 3,
                      [(_sds((r, d)), _rows(d))], (x, cos_t, sin_t), (0,))[0]


def _shift_rows(x, k):
    n = x.shape[0]

    def down(x):
        row = lax.broadcasted_iota(jnp.int32, x.shape, 0)
        return jnp.where(row >= k, pltpu.roll(x, k, axis=0), 0.0)

    def up(x):
        row = lax.broadcasted_iota(jnp.int32, x.shape, 0)
        return jnp.where(row < n - k, pltpu.roll(x, n - k, axis=0), 0.0)

    @jax.custom_vjp
    def shift(x):
        return down(x)

    shift.defvjp(lambda x: (down(x), None), lambda _, g: (up(g),))
    return shift(x)


def _causal_conv(x, w):
    taps = w.shape[0]
    y = x * w[taps - 1:taps, :]
    for j in range(taps - 1):
        y = y + _shift_rows(x, taps - 1 - j) * w[j:j + 1, :]
    return y


def _cols(rows, at=0):
    return pl.BlockSpec((rows, LANES), lambda j: (0, at + j))


def short_conv(cb, cc, cx, w, name):
    s, c = cb.shape
    taps = w.shape[0]
    return _simple_op(name, lambda b, c_, x, w: (b * _causal_conv(c_ * x, w),), (c // LANES,),
                      [_cols(s)] * 3 + [_cols(taps)], [(_sds((s, c)), _cols(s))], (cb, cc, cx, w), (0, 1, 2, 3))[0]


def gdn_pre(qkv, w, name):
    s, c = qkv.shape
    taps = w.shape[0]

    def f(x, w):
        j = pl.program_id(0)
        y = jax.nn.silu(_causal_conv(x, w))
        normed = y * lax.rsqrt(jnp.sum(y * y, axis=-1, keepdims=True) + EPS)
        scale = jnp.where(j < GDN_HEADS, GDN_HEAD_DIM ** -0.5, 1.0).astype(f32)
        return (jnp.where(j < 2 * GDN_HEADS, normed * scale, y),)

    return _simple_op(name, f, (c // LANES,), [_cols(s), _cols(taps)], [(_sds((s, c)), _cols(s))], (qkv, w), (0, 1))[0]


def gate_beta(ab, pv, name):
    s = ab.shape[0]

    def f(ab, pv):
        lane = lax.broadcasted_iota(jnp.int32, ab.shape, 1)
        g = -jnp.exp(pv[0:1, :]) * jax.nn.softplus(ab + pv[1:2, :])
        return (jnp.where(lane < GDN_HEADS, g, jnp.where(lane < 2 * GDN_HEADS, jax.nn.sigmoid(ab), 0.0)),)

    return _simple_op(name, f, (s // ROW_TILE,), [_rows(LANES), _whole((8, LANES))], [(_sds((s, LANES)), _rows(LANES))],
                      (ab, pv), (0, 1), {1: (0,)})[0]


def gdn_post(o, gate, w, name):
    s, c = o.shape

    def f(o, g, w):
        heads = [slice(hd * LANES, (hd + 1) * LANES) for hd in range(c // LANES)]
        return (jnp.concatenate([_rms(o[:, hd], w) * jax.nn.silu(g[:, hd]) for hd in heads], axis=1),)

    return _simple_op(name, f, (s // ROW_TILE,), [_rows(c), _rows(c), _whole((1, LANES))], [(_sds((s, c)), _rows(c))],
                      (o, gate, w), (0, 1, 2), {2: (0,)})[0]


def attn_merge(outs, lses, name):
    s, c = outs[0].shape

    def f(o1, o2, o3, l1, l2, l3):
        m = lax.stop_gradient(jnp.maximum(jnp.maximum(l1, l2), l3))
        e1, e2, e3 = jnp.exp(l1 - m), jnp.exp(l2 - m), jnp.exp(l3 - m)
        return ((e1 * o1 + e2 * o2 + e3 * o3) / (e1 + e2 + e3),)

    return _simple_op(name, f, (s // ROW_TILE,), [_rows(c)] * 6, [(_sds((s, c)), _rows(c))], (*outs, *lses), tuple(range(6)))[0]


def loss_rows(y, target, name):
    s, d = y.shape
    nt = s // ROW_TILE

    def f(y, t):
        e = y - t
        part = 0.5 * jnp.sum(jnp.mean(e * e, axis=-1, keepdims=True), axis=0, keepdims=True)
        return (jnp.broadcast_to(part * (1.0 / (8 * LANES)), (8, LANES)),)

    out = _simple_op(name, f, (nt,), [_rows(d)] * 2, [(_sds((nt * 8, LANES)), pl.BlockSpec((8, LANES), lambda i: (i, 0)))],
                     (y, target), (0,))[0]
    return jnp.sum(out)


def _mxu(a, b, form):
    dims = {"nn": ((1,), (0,)), "nt": ((1,), (1,)), "tn": ((0,), (0,))}

    def raw(a, b, form):
        return lax.dot_general(a.astype(bf16), b.astype(bf16), (dims[form], ((), ())), preferred_element_type=f32)

    @jax.custom_vjp
    def prod(a, b):
        return raw(a, b, form)

    def prod_b(res, ct):
        a, b = res
        if form == "nn":
            return raw(ct, b, "nt"), raw(a, ct, "tn")
        if form == "nt":
            return raw(ct, b, "nn"), raw(ct, a, "tn")
        return raw(b, ct, "nt"), raw(a, ct, "nn")

    prod.defvjp(lambda a, b: (raw(a, b, form), (a, b)), prod_b)
    return prod(a, b)


def _masked_heads_attention(q, keys, values, seen):
    dh = ATTN_HEAD_DIM
    outs, lses = [], []
    for hd in range(q.shape[1] // dh):
        at = slice(hd * dh, (hd + 1) * dh)
        sc = jnp.where(seen, _mxu(q[:, at], keys[:, at], "nt"), -jnp.inf)
        m = lax.stop_gradient(jnp.max(sc, axis=-1, keepdims=True))
        p = jnp.exp(sc - m)
        l = jnp.sum(p, axis=-1, keepdims=True)
        outs.append(_mxu(p / l, values[:, at], "nn"))
        lses.append(jnp.broadcast_to(m + jnp.log(l), (q.shape[0], dh)))
    return jnp.concatenate(outs, axis=1), jnp.concatenate(lses, axis=1)


def band_attention(q, k, v, nb, name):
    r, qb, width = q.shape

    def f(q, kp, kc, vp, vc):
        has_prev = (pl.program_id(0) % nb) > 0
        i = lax.broadcasted_iota(jnp.int32, (qb, 2 * qb), 0)
        j = lax.broadcasted_iota(jnp.int32, (qb, 2 * qb), 1)
        seen = jnp.logical_or(jnp.logical_and(jnp.logical_and(j < qb, j >= i), has_prev), jnp.logical_and(j >= qb, j - qb <= i))
        return _masked_heads_attention(q, jnp.concatenate([kp, kc], axis=0), jnp.concatenate([vp, vc], axis=0), seen)

    blk = (None, qb, width)
    cur = pl.BlockSpec(blk, lambda b: (b, 0, 0))
    prev = pl.BlockSpec(blk, lambda b: (jnp.maximum(b - 1, 0), 0, 0))
    shape = _sds((r, qb, width))
    fwd_call, bwd_call = _block_op(name, f, (r,), [cur, prev, cur, prev, cur], [(shape, cur), (shape, cur)],
                                   (q, k, k, v, v), (0, 1, 2, 3, 4), gdefs={1: (shape, cur), 3: (shape, cur)})

    def to_prev(g):
        return jnp.concatenate([g[1:], jnp.zeros_like(g[:1])], axis=0)

    @jax.custom_vjp
    def op(q, k, v):
        return tuple(fwd_call(q, k, k, v, v))

    def op_b(res, cts):
        q, k, v = res
        dq, dkp, dkc, dvp, dvc = bwd_call(q, k, k, v, v, *cts)
        return dq, dkc + to_prev(dkp), dvc + to_prev(dvp)

    op.defvjp(lambda q, k, v: (tuple(fwd_call(q, k, k, v, v)), (q, k, v)), op_b)
    return op(q, k, v)


def dilated_attention(q, k, v, name):
    s = q.shape[0]
    outs, lses = [], []
    for d in DILATIONS:
        length = s // d
        nb = length // QB
        def to_residue(t):
            return t.reshape(length, d, ATTN_WIDTH).transpose(1, 0, 2).reshape(d * nb, QB, ATTN_WIDTH)

        def from_residue(t):
            return t.reshape(d, length, ATTN_WIDTH).transpose(1, 0, 2).reshape(s, ATTN_WIDTH)

        o, lse = band_attention(to_residue(q), to_residue(k), to_residue(v), nb, f"{name}_d{d}")
        outs.append(from_residue(o))
        lses.append(from_residue(lse))
    return attn_merge(outs, lses, name + "_merge")


def cross_attention(q, kv, name):
    s = q.shape[0]
    m = kv.shape[0]
    width = XATTN_HEADS * XATTN_HEAD_DIM
    tq = 512

    def f(q, k, v):
        sc = _mxu(q, k, "nt") * (XATTN_HEAD_DIM ** -0.5)
        mx = lax.stop_gradient(jnp.max(sc, axis=-1, keepdims=True))
        p = jnp.exp(sc - mx)
        return (_mxu(p / jnp.sum(p, axis=-1, keepdims=True), v, "nn"),)

    q_spec = pl.BlockSpec((tq, XATTN_HEAD_DIM), lambda a, i: (i, a))
    k_spec = pl.BlockSpec((m, XATTN_HEAD_DIM), lambda a, i: (0, a))
    v_spec = pl.BlockSpec((m, XATTN_HEAD_DIM), lambda a, i: (0, a + XATTN_HEADS))
    half = _sds((m, width))
    fwd_call, bwd_call = _block_op(name, f, (XATTN_HEADS, s // tq), [q_spec, k_spec, v_spec], [(_sds((s, width)), q_spec)],
                                   (q, kv, kv), (0, 1, 2), acc={1: (1,), 2: (1,)}, gdefs={1: (half, k_spec), 2: (half, k_spec)})

    @jax.custom_vjp
    def op(q, kv):
        return fwd_call(q, kv, kv)[0]

    def op_b(res, ct):
        q, kv = res
        dq, dk, dv = bwd_call(q, kv, kv, ct)
        return dq, jnp.concatenate([dk, dv], axis=1)

    op.defvjp(lambda q, kv: (fwd_call(q, kv, kv)[0], (q, kv)), op_b)
    return op(q, kv)


def _hi(a, b, form="nn"):
    dims = {"nn": ((1,), (0,)), "nt": ((1,), (1,)), "tn": ((0,), (0,))}[form]
    return lax.dot_general(a, b, (dims, ((), ())), precision=lax.Precision.HIGH, preferred_element_type=f32)


def _running_sum(g):
    def raw(x, form):
        c = x.shape[0]
        tri = (lax.broadcasted_iota(jnp.int32, (c, c), 0) >= lax.broadcasted_iota(jnp.int32, (c, c), 1)).astype(bf16)
        hi = x.astype(bf16)
        rest = x - hi.astype(f32)
        mid = rest.astype(bf16)
        low = (rest - mid.astype(f32)).astype(bf16)
        dims = (((1,) if form == "nn" else (0,), (0,)), ((), ()))
        return sum(lax.dot_general(tri, part, dims, preferred_element_type=f32) for part in (hi, mid, low))

    @jax.custom_vjp
    def run(x):
        return raw(x, "nn")

    run.defvjp(lambda x: (raw(x, "nn"), None), lambda _, ct: (raw(ct, "tn"),))
    return run(g)


def _unit_lower_inverse(a):
    c = a.shape[0]
    eye = (lax.broadcasted_iota(jnp.int32, (c, c), 0) == lax.broadcasted_iota(jnp.int32, (c, c), 1)).astype(f32)
    inv, power = eye - a, -a
    for _ in range(c.bit_length() - 2):
        power = _hi(power, power)
        inv = inv + _hi(inv, power)
    return inv


def _known_inverse(a, t):
    @jax.custom_vjp
    def inv(a, t):
        return t

    def inv_b(t, ct):
        return -_hi(_hi(t, ct, "tn"), t, "nt"), jnp.zeros_like(t)

    inv.defvjp(lambda a, t: (t, t), inv_b)
    return inv(a, t)


def _delta_chunk(q, k, v, g, beta, s0, known_inv=None):
    c = q.shape[0]
    i = lax.broadcasted_iota(jnp.int32, (c, c), 0)
    j = lax.broadcasted_iota(jnp.int32, (c, c), 1)
    causal, strict = i >= j, i > j
    dec = _running_sum(g)
    dec_i = dec[:, :c]
    rel = jnp.exp(jnp.where(causal, dec_i - dec_i.T, -jnp.inf))
    k_beta = k * beta
    on_k = _mxu(jnp.concatenate([k_beta, q], axis=0), k, "nt")
    a = jnp.where(strict, on_k[:c] * rel, 0.0)
    attn = jnp.where(causal, on_k[c:] * rel, 0.0)
    inv = _unit_lower_inverse(a) if known_inv is None else _known_inverse(a, known_inv)
    e_dec = jnp.exp(dec)
    solved = _hi(inv, jnp.concatenate([v * beta, k_beta * e_dec], axis=1))
    u, w = solved[:, :v.shape[1]], solved[:, v.shape[1]:]
    total = jnp.sum(g, axis=0, keepdims=True)
    on_state = _mxu(jnp.concatenate([w, q * e_dec], axis=0), s0, "nn")
    v_new = u - on_state[:c]
    o = on_state[c:] + _mxu(attn, v_new, "nn")
    s1 = s0 * jnp.exp(total) + _mxu(k * jnp.exp(total - dec), v_new, "tn")
    return o, s1, inv


def _delta_rule_call(name, walk, n, in_specs, out_specs, out_shape, operands, exchange):
    n_in, n_out = len(in_specs), len(out_specs)
    carried = len(exchange.operands) if exchange else 0

    def body(*refs):
        ins, refs = refs[:n_in], refs[n_in:]
        x_refs, refs = refs[:carried], refs[carried:]
        outs, refs = refs[:n_out], refs[n_out:]
        land_refs, (state, *sems) = refs[:carried], refs[carried:]
        step = pl.program_id(0)
        if exchange:
            start, finish = exchange.bind(x_refs, land_refs, sems)
            pl.when(step == 0)(start)

        @pl.when(step == 0)
        def _():
            state[...] = jnp.zeros_like(state)

        walk(ins, outs, state)
        if exchange:
            pl.when(step == n - 1)(finish)

    return pl.pallas_call(
        body, name=name, grid=(n,), in_specs=list(in_specs) + [ANY] * carried, out_specs=list(out_specs) + [ANY] * carried,
        out_shape=list(out_shape) + (exchange.out_shapes if exchange else []),
        scratch_shapes=[pltpu.VMEM((GDN_HEAD_DIM, GDN_WIDTH), f32)] + (exchange.scratch if exchange else []),
        compiler_params=_params(1))(*operands, *(exchange.operands if exchange else []))


def _delta_heads():
    heads = [slice(hd * GDN_HEAD_DIM, (hd + 1) * GDN_HEAD_DIM) for hd in range(GDN_HEADS)]
    inv_at = [slice(hd * GDN_CHUNK, (hd + 1) * GDN_CHUNK) for hd in range(GDN_HEADS)]
    return heads, inv_at


def _head_chunk(q, k, v, gates, s0, head, known_inv=None):
    g = jnp.broadcast_to(gates[:, head:head + 1], q.shape)
    beta = jnp.broadcast_to(gates[:, GDN_HEADS + head:GDN_HEADS + head + 1], q.shape)
    return _delta_chunk(q, k, v, g, beta, s0, known_inv)


def delta_rule_fwd(q, k, v, gates, name, exchange=None):
    s, width = q.shape
    c, dk = GDN_CHUNK, GDN_HEAD_DIM
    n = s // c
    heads, inv_at = _delta_heads()

    def walk(ins, outs, state):
        q_ref, k_ref, v_ref, gates_ref = ins
        o_ref, s_in_ref, inv_ref = outs
        s_in_ref[...] = state[...]
        gates = gates_ref[...]
        xs = [[r[:, hd] for r in (q_ref, k_ref, v_ref)] + [gates, state[:, hd], i] for i, hd in enumerate(heads)]
        ys = [_head_chunk(*x) for x in xs]
        for hd, at, (o, s1, inv) in zip(heads, inv_at, ys):
            o_ref[:, hd], state[:, hd], inv_ref[:, at] = o, s1, inv

    blk = pl.BlockSpec((c, width), lambda t: (t, 0))
    gt = pl.BlockSpec((c, LANES), lambda t: (t, 0))
    st = pl.BlockSpec((dk, width), lambda t: (t, 0))
    iv = pl.BlockSpec((c, GDN_HEADS * c), lambda t: (t, 0))
    return _delta_rule_call(name, walk, n, [blk] * 3 + [gt], [blk, st, iv],
                            [_sds((s, width)), _sds((n * dk, width)), _sds((s, GDN_HEADS * c))], (q, k, v, gates), exchange)


def delta_rule_bwd(q, k, v, gates, s_in, inv, do, name, exchange=None):
    s, width = q.shape
    c, dk = GDN_CHUNK, GDN_HEAD_DIM
    n = s // c
    heads, inv_at = _delta_heads()

    def walk(ins, outs, dstate):
        q_ref, k_ref, v_ref, gates_ref, s_ref, inv_ref, do_ref = ins
        dq_ref, dk_ref, dv_ref, dgates_ref = outs
        gates = gates_ref[...]
        xs = [[r[:, hd] for r in (q_ref, k_ref, v_ref)] + [gates, s_ref[:, hd]] for hd in heads]
        known = [inv_ref[:, at] for at in inv_at]
        cts = [(do_ref[:, hd], dstate[:, hd]) for hd in heads]
        grads = []
        for i, (x, t, ct) in enumerate(zip(xs, known, cts)):
            _, vjp = jax.vjp(lambda *y, t=t, i=i: _head_chunk(*y, i, known_inv=t)[:2], *x)
            grads.append(vjp(ct))
        dgates = grads[0][3]
        for g in grads[1:]:
            dgates = dgates + g[3]
        dgates_ref[...] = dgates
        for hd, (dq, dk_, dv, _, ds0) in zip(heads, grads):
            dq_ref[:, hd], dk_ref[:, hd], dv_ref[:, hd], dstate[:, hd] = dq, dk_, dv, ds0

    blk = pl.BlockSpec((c, width), lambda t: (n - 1 - t, 0))
    gt = pl.BlockSpec((c, LANES), lambda t: (n - 1 - t, 0))
    st = pl.BlockSpec((dk, width), lambda t: (n - 1 - t, 0))
    iv = pl.BlockSpec((c, GDN_HEADS * c), lambda t: (n - 1 - t, 0))
    return _delta_rule_call(name, walk, n, [blk] * 3 + [gt, st, iv, blk], [blk] * 3 + [gt],
                            [_sds((s, width))] * 3 + [_sds((s, LANES))], (q, k, v, gates, s_in, inv, do), exchange)


def adamw(w, g, m, v, name):
    shape = w.shape
    if len(shape) == 2:
        grid, spec = (1,), pl.BlockSpec(shape, lambda i: (0, 0))
    else:
        tile = shape[1] if shape[1] <= 512 else _pick(shape[1], (512, 256, 128))
        grid, spec = (shape[0], shape[1] // tile), pl.BlockSpec((None, tile, shape[2]), lambda layer, i: (layer, i, 0))

    def body(w_ref, g_ref, m_ref, v_ref, d_ref, nm_ref, nv_ref):
        grad = g_ref[...]
        nm = ADAM_B1 * m_ref[...] + (1.0 - ADAM_B1) * grad
        nv = ADAM_B2 * v_ref[...] + (1.0 - ADAM_B2) * (grad * grad)
        m_hat = nm / (1.0 - ADAM_B1 ** ADAM_STEP)
        v_hat = nv / (1.0 - ADAM_B2 ** ADAM_STEP)
        d_ref[...] = -ADAM_LR * (m_hat / (jnp.sqrt(v_hat) + ADAM_EPS) + ADAM_WD * w_ref[...])
        nm_ref[...] = nm
        nv_ref[...] = nv

    return tuple(pl.pallas_call(body, name=name, grid=grid, in_specs=[spec] * 4, out_specs=[spec] * 3,
                                out_shape=[_sds(shape)] * 3, compiler_params=_params(len(grid)))(w, g, m, v))


def _place():
    return lax.axis_index("x"), lax.axis_index("y"), lax.axis_index("c")


def _flip(p, bits):
    return tuple(1 - v if (bits >> s) & 1 else v for v, s in zip(p, (2, 1, 0)))


def _slot(p):
    return 4 * p[0] + 2 * p[1] + p[2]


def _chip_of(p):
    return 2 * p[0] + p[1]


ANY = pl.BlockSpec(memory_space=pl.ANY)


class Gather:
    scratch = (pltpu.SemaphoreType.DMA((7,)), pltpu.SemaphoreType.DMA((7,)), pltpu.SemaphoreType.DMA)

    def __init__(self, shard):
        self.operand = shard
        self.out_shape = jax.ShapeDtypeStruct((N_DEV,) + shard.shape, shard.dtype)

    def bind(self, x_ref, out_ref, send_sems, recv_sems, local_sem):
        me = _place()
        sibling = _flip(me, 1)
        chips = [_flip(me, 4), _flip(me, 2), _flip(me, 6)]

        def copy(k, block, to, src=None):
            return pltpu.make_async_remote_copy(
                src_ref=out_ref.at[_slot(block)] if src is None else src, dst_ref=out_ref.at[_slot(block)],
                send_sem=send_sems.at[k], recv_sem=recv_sems.at[k], device_id=to, device_id_type=MESH)

        mine = pltpu.make_async_copy(x_ref, out_ref.at[_slot(me)], local_sem)
        first = [copy(0, me, sibling, src=x_ref)] + [copy(1 + j, me, chip, src=x_ref) for j, chip in enumerate(chips)]
        passed = [copy(4 + j, chip, sibling) for j, chip in enumerate(chips)]

        def start():
            mine.start()
            for cp in first:
                cp.start()

        def finish():
            for j, chip in enumerate(chips):
                copy(1 + j, chip, me).wait_recv()
                passed[j].start()
            copy(0, sibling, me).wait_recv()
            for j, chip in enumerate(chips):
                copy(4 + j, _flip(chip, 1), me).wait_recv()
            for cp in first + passed:
                cp.wait_send()
            mine.wait()

        return start, finish


class ChipExchange:
    scratch = (pltpu.SemaphoreType.DMA((3,)), pltpu.SemaphoreType.DMA((3,)), pltpu.SemaphoreType.DMA)

    def __init__(self, blocks):
        self.operand = blocks
        self.out_shape = jax.ShapeDtypeStruct(blocks.shape, blocks.dtype)

    def bind(self, x_ref, out_ref, send_sems, recv_sems, local_sem):
        me = _place()
        peers = [_flip(me, 4), _flip(me, 2), _flip(me, 6)]
        mine = pltpu.make_async_copy(x_ref.at[_chip_of(me)], out_ref.at[_chip_of(me)], local_sem)

        def copy(j, src_chip, dst_chip):
            return pltpu.make_async_remote_copy(
                src_ref=x_ref.at[src_chip], dst_ref=out_ref.at[dst_chip], send_sem=send_sems.at[j],
                recv_sem=recv_sems.at[j], device_id=peers[j], device_id_type=MESH)

        sends = [copy(j, _chip_of(peer), _chip_of(me)) for j, peer in enumerate(peers)]

        def start():
            mine.start()
            for cp in sends:
                cp.start()

        def finish():
            for j, peer in enumerate(peers):
                copy(j, _chip_of(me), _chip_of(peer)).wait_recv()
            for cp in sends:
                cp.wait_send()
            mine.wait()

        return start, finish


class Together:
    def __init__(self, *parts):
        self.parts = parts
        self.operands = [p.operand for p in parts]
        self.out_shapes = [p.out_shape for p in parts]
        self.scratch = [s for p in parts for s in p.scratch]

    def bind(self, x_refs, out_refs, sems):
        bound, at = [], 0
        for p, x_ref, out_ref in zip(self.parts, x_refs, out_refs):
            bound.append(p.bind(x_ref, out_ref, *sems[at:at + len(p.scratch)]))
            at += len(p.scratch)

        def start():
            for s, _ in bound:
                s()

        def finish():
            for _, f in bound:
                f()

        return start, finish


def exchange_alone(exchange, name):
    n = len(exchange.operands)

    def body(*refs):
        start, finish = exchange.bind(refs[:n], refs[n:2 * n], refs[2 * n:])
        start()
        finish()

    return pl.pallas_call(body, name=name, out_shape=exchange.out_shapes, in_specs=[ANY] * n, out_specs=[ANY] * n,
                          scratch_shapes=exchange.scratch)(*exchange.operands)


def _row_tile(rows):
    return max([t for t in range(16, min(rows, 1024) + 1, 16) if rows % t == 0] or [rows])


def pair_exchange(blocks, name):
    n = len(blocks)

    def body(*refs):
        x_refs, theirs_refs, (send_sems, recv_sems) = refs[:n], refs[n:2 * n], refs[2 * n:]
        me = _place()
        remote = [pltpu.make_async_remote_copy(
            src_ref=x_refs[t].at[2 * q + 1 - me[2]], dst_ref=theirs_refs[t].at[q], send_sem=send_sems.at[4 * t + q],
            recv_sem=recv_sems.at[4 * t + q], device_id=_flip(me, 1), device_id_type=MESH) for t in range(n) for q in range(4)]
        for cp in remote:
            cp.start()
        for cp in remote:
            cp.wait()

    return pl.pallas_call(
        body, name=name, out_shape=[jax.ShapeDtypeStruct((4,) + b.shape[1:], b.dtype) for b in blocks], in_specs=[ANY] * n,
        out_specs=[ANY] * n, scratch_shapes=[pltpu.SemaphoreType.DMA((4 * n,)), pltpu.SemaphoreType.DMA((4 * n,))])(*blocks)


def pair_add(blocks, theirs, name):
    n, rows, width = theirs.shape
    tile = _row_tile(rows)
    spec = pl.BlockSpec((None, tile, width), lambda q, i: (q, i, 0))
    south = pl.BlockSpec((None, None, tile, width), lambda q, i: (q, 0, i, 0))
    north = pl.BlockSpec((None, None, tile, width), lambda q, i: (q, 1, i, 0))

    def body(s_ref, n_ref, b_ref, o_ref):
        mine = jnp.where(lax.axis_index("c") == 0, s_ref[...], n_ref[...])
        o_ref[...] = (mine.astype(f32) + b_ref[...].astype(f32)).astype(o_ref.dtype)

    by_core = blocks.reshape(n, 2, rows, width)
    return pl.pallas_call(body, name=name, grid=(n, rows // tile), in_specs=[south, north, spec], out_specs=spec,
                          out_shape=jax.ShapeDtypeStruct(theirs.shape, theirs.dtype), compiler_params=_params(2))(by_core, by_core, theirs)


def sum_slots(blocks, name):
    n, rows, width = blocks.shape
    tile = _row_tile(rows)

    def body(x_ref, o_ref):
        total = x_ref[0].astype(f32)
        for s in range(1, n):
            total = total + x_ref[s].astype(f32)
        o_ref[...] = total

    return pl.pallas_call(
        body, name=name, grid=(rows // tile,), in_specs=[pl.BlockSpec((n, tile, width), lambda i: (0, i, 0))],
        out_specs=pl.BlockSpec((tile, width), lambda i: (i, 0)), out_shape=_sds((rows, width)), compiler_params=_params(1))(blocks)


def all_reduce_small(x, name):
    rows, width = x.shape

    def body(x_ref, o_ref, land, send_sems, recv_sems):
        me = _place()
        copies = []
        for k in range(1, N_DEV):
            peer = _flip(me, k)
            copies.append(pltpu.make_async_remote_copy(
                src_ref=x_ref, dst_ref=land.at[_slot(me)], send_sem=send_sems.at[k - 1], recv_sem=recv_sems.at[k - 1],
                device_id=peer, device_id_type=MESH))
        for cp in copies:
            cp.start()
        land[_slot(me)] = x_ref[...]
        for k in range(1, N_DEV):
            peer = _flip(me, k)
            pltpu.make_async_remote_copy(
                src_ref=x_ref, dst_ref=land.at[_slot(peer)], send_sem=send_sems.at[k - 1], recv_sem=recv_sems.at[k - 1],
                device_id=peer, device_id_type=MESH).wait_recv()
        total = land[0]
        for s in range(1, N_DEV):
            total = total + land[s]
        o_ref[...] = total
        for cp in copies:
            cp.wait_send()

    return pl.pallas_call(
        body, name=name, out_shape=_sds((rows, width)), in_specs=[pl.BlockSpec(memory_space=pltpu.VMEM)],
        out_specs=pl.BlockSpec(memory_space=pltpu.VMEM),
        scratch_shapes=[pltpu.VMEM((N_DEV, rows, width), f32), pltpu.SemaphoreType.DMA((7,)), pltpu.SemaphoreType.DMA((7,))],
    )(x)


def _pack_big(shards):
    packed = {name: shards[name].astype(bf16) for name in COL_SHARDED}
    packed["rows"] = jnp.concatenate([shards[name].astype(bf16) for name, _ in ROW_SHARDED], axis=1)
    return packed


def _unpack_gathered(gathered):
    full = {}
    for name, part in gathered.items():
        if name == "w_gate_up":
            full[name] = part
        elif name in COL_SHARDED:
            full[name] = part.transpose(1, 0, 2).reshape(D_MODEL, N_DEV * part.shape[2])
        else:
            at = 0
            for weight, rows in ROW_SHARDED:
                full[weight] = part[:, at:at + rows, :].reshape(N_DEV * rows, D_MODEL)
                at += rows
    if "w_in" in full:
        w_in = full.pop("w_in")
        full["w_main"] = jnp.concatenate([w_in[:, :AB_AT], w_in[:, AB_AT + 2 * GDN_HEADS:]], axis=1)
        full["w_ab"] = jnp.pad(w_in[:, AB_AT:AB_AT + 2 * GDN_HEADS], ((0, 0), (0, LANES - 2 * GDN_HEADS)))
    return full


def _pack_grads(grads, group):
    packed = {}
    for name in group:
        if name == "w_in":
            main, ab = grads["w_main"], grads["w_ab"]
            g = jnp.concatenate([main[:, :AB_AT], ab[:, :2 * GDN_HEADS], main[:, AB_AT:]], axis=1)
        elif name == "w_gate_up":
            packed[name] = grads[name]
            continue
        elif name == "rows":
            packed[name] = jnp.concatenate([grads[weight].reshape(N_DEV, rows, D_MODEL) for weight, rows in ROW_SHARDED], axis=1)
            continue
        else:
            g = grads[name]
        packed[name] = g.reshape(D_MODEL, N_DEV, g.shape[1] // N_DEV).transpose(1, 0, 2)
    return packed


def _unpack_shard(layers):
    out = {name: jnp.stack([layer[name] for layer in layers]) for name in COL_SHARDED}
    rows_pack, at = jnp.stack([layer["rows"] for layer in layers]), 0
    for weight, rows in ROW_SHARDED:
        out[weight] = rows_pack[:, at:at + rows, :]
        at += rows
    return out


def _rows_of(flat_len):
    return -(-flat_len // (8 * D_MODEL)) * 8


def _pack_small(parts):
    flat = jnp.concatenate([p.reshape(-1) for p in parts])
    rows = _rows_of(flat.shape[0])
    flat = jnp.pad(flat, (0, rows * D_MODEL - flat.shape[0]))
    return flat.reshape(rows, D_MODEL)


def _unpack_small(packed, like):
    flat, out, at = packed.reshape(-1), [], 0
    for p in like:
        out.append(flat[at:at + p.size].reshape(p.shape))
        at += p.size
    return out


def _rope_tables(positions):
    inv_freq = jnp.float32(ROPE_THETA) ** (-jnp.arange(0, ROPE_DIM, 2, dtype=f32) / ROPE_DIM)
    ang = positions.astype(f32)[:, None] * inv_freq
    cos, sin = jnp.cos(ang), jnp.sin(ang)
    rest = ATTN_HEAD_DIM - ROPE_DIM
    cos_h = jnp.concatenate([cos, cos, jnp.ones((cos.shape[0], rest), f32)], axis=1)
    sin_h = jnp.concatenate([-sin, sin, jnp.zeros((sin.shape[0], rest), f32)], axis=1)
    return jnp.tile(cos_h, (1, ATTN_HEADS)), jnp.tile(sin_h, (1, ATTN_HEADS))


HEAD_SMALL = ("norm_mix_pre", "conv_short", "conv_gdn", "gdn_a_log", "gdn_dt_bias")


def _layer_head(h, p, cos_t, sin_t):
    hn = rms_norm(h, p["norm_mix_pre"][None], "norm_mix_pre")
    proj = _linear(hn, p["w_main"], "w_main")
    ab = _linear(hn, p["w_ab"], "w_ab")
    aw, cw, gw = ATTN_WIDTH, CONV_WIDTH, GDN_WIDTH
    aq, ak, av, cb, cc, cx, gqkv, gate = _split_cols(proj, (aw, aw, aw, cw, cw, cw, 3 * gw, gw))
    y_attn = dilated_attention(rope(aq, cos_t, sin_t, ATTN_HEAD_DIM ** -0.5, "rope_q"), rope(ak, cos_t, sin_t, 1.0, "rope_k"),
                               av, "attn")
    y_conv = short_conv(cb, cc, cx, p["conv_short"], "short_conv")
    qkv = gdn_pre(gqkv, p["conv_gdn"], "gdn_pre")
    pv = jnp.zeros((8, LANES), f32).at[0, :GDN_HEADS].set(p["gdn_a_log"]).at[1, :GDN_HEADS].set(p["gdn_dt_bias"])
    return (*_split_cols(qkv, (gw, gw, gw)), gate_beta(ab, pv, "gate_beta")), (gate, y_attn, y_conv)


MID_PARAMS = ("gdn_norm", "w_out", "norm_mix_post", "norm_xattn_pre", "w_xq", "norm_mem", "w_xkv", "w_xo", "norm_xattn_post",
              "norm_ffn_pre")


def _layer_mid(h, o, gate, y_attn, y_conv, p, mem):
    y_gdn = gdn_post(o, gate, p["gdn_norm"][None], "gdn_post")
    mix = _linear(jnp.concatenate([y_attn, y_conv, y_gdn], axis=1), p["w_out"], "w_out")
    h, hn = add_norm_then_norm(h, mix, p["norm_mix_post"][None], p["norm_xattn_pre"][None], "norm_mix_xattn")
    qx = _linear(hn, p["w_xq"], "w_xq")
    kv = _linear(rms_norm(mem, p["norm_mem"][None], "norm_mem"), p["w_xkv"], "w_xkv")
    xa = _linear(cross_attention(qx, kv, "xattn"), p["w_xo"], "w_xo")
    return add_norm_then_norm(h, xa, p["norm_xattn_post"][None], p["norm_ffn_pre"][None], "norm_xattn_ffn")


def _pair_summed(grads, group, name):
    blocks = _pack_grads(grads, group)
    theirs = pair_exchange([blocks[n] for n in group], name + "_pair_exchange")
    return [pair_add(blocks[n], t, f"{name}_pair_add_{n}") for n, t in zip(group, theirs)]


def _forward_backward(x, packed, small, mem, cos_t, sin_t, target):
    def gathers(group, layer):
        return [Gather(packed[n][layer]) for n in group]

    h = x
    head_gathered = exchange_alone(Together(*gathers(HEAD_GROUP, 0)), "gather_first")
    saved = []
    for layer in range(DEPTH):
        at_layer = {n: t[layer] for n, t in small.items()}
        head_p = {**_unpack_gathered(dict(zip(HEAD_GROUP, head_gathered))), **{n: at_layer[n] for n in HEAD_SMALL}}
        (rule_in, rest), head_vjp = jax.vjp(lambda h, hp: _layer_head(h, hp, cos_t, sin_t), h, head_p)
        carried = gathers(TAIL_GROUP, layer) + (gathers(HEAD_GROUP, layer + 1) if layer + 1 < DEPTH else [])
        o, s_in, inv, *landed = delta_rule_fwd(*rule_in, "delta_rule_fwd", Together(*carried))
        head_gathered = landed[len(TAIL_GROUP):]
        tail_p = {**_unpack_gathered(dict(zip(TAIL_GROUP, landed))), **at_layer}
        mid_p = {n: tail_p[n] for n in MID_PARAMS}
        (h, hn), mid_vjp = jax.vjp(lambda h, o, rest, mp: _layer_mid(h, o, *rest, mp, mem), h, o, rest, mid_p)
        y, ffn_saved = ffn_forward(hn, tail_p["w_gate_up"], tail_p["w_down"], "ffn")
        h, last_vjp = jax.vjp(lambda h, y, w: add_norm(h, y, w[None], "norm_ffn_post"), h, y, tail_p["norm_ffn_post"])
        saved.append((head_vjp, mid_vjp, last_vjp, ffn_saved, rule_in, s_in, inv))

    loss, dh = jax.value_and_grad(lambda y: loss_rows(y, target, "loss"))(h)

    def summed(group, landed):
        return {n: sum_slots(t, "sum_grads_" + n) for n, t in zip(group, landed)}

    big_grads, small_grads, head_pending = [{} for _ in range(DEPTH)], [None] * DEPTH, []
    for layer in reversed(range(DEPTH)):
        head_vjp, mid_vjp, last_vjp, ffn_saved, rule_in, s_in, inv = saved[layer]
        dh, dy, d_norm_ffn_post = last_vjp(dh)
        dhn, d_gate_up, d_down, landed = ffn_backward(
            ffn_saved, dy, "ffn", Together(*[ChipExchange(t) for t in head_pending]) if head_pending else None)
        if head_pending:
            big_grads[layer + 1].update(summed(HEAD_GROUP, landed))
        dh_mid, do, d_rest, d_mid_p = mid_vjp((dh, dhn))
        d_tail_p = {**d_mid_p, "w_gate_up": d_gate_up, "w_down": d_down, "norm_ffn_post": d_norm_ffn_post}
        carried = Together(*[ChipExchange(t) for t in _pair_summed(d_tail_p, TAIL_GROUP, "tail")])
        *d_rule_in, = delta_rule_bwd(*rule_in, s_in, inv, do, "delta_rule_bwd", carried)
        big_grads[layer].update(summed(TAIL_GROUP, d_rule_in[4:]))
        dh_head, d_head_p = head_vjp((tuple(d_rule_in[:4]), d_rest))
        dh = dh_mid + dh_head
        small_grads[layer] = {n: t for n, t in {**d_head_p, **d_tail_p}.items() if n in small}
        head_pending = _pair_summed(d_head_p, HEAD_GROUP, "head")
    landed = exchange_alone(Together(*[ChipExchange(t) for t in head_pending]), "exchange_last")
    big_grads[0].update(summed(HEAD_GROUP, landed))
    return loss, dh, big_grads, small_grads


def kernel(x, mem, positions, norm_mix_pre, norm_mix_post, w_in, conv_short, conv_gdn, gdn_a_log, gdn_dt_bias, gdn_norm, w_out, norm_mem, norm_xattn_pre, norm_xattn_post, w_xq, w_xkv, w_xo, norm_ffn_pre, norm_ffn_post, w_gate_up, w_down, loss_target, m_norm_mix_pre, m_norm_mix_post, m_w_in, m_conv_short, m_conv_gdn, m_gdn_a_log, m_gdn_dt_bias, m_gdn_norm, m_w_out, m_norm_mem, m_norm_xattn_pre, m_norm_xattn_post, m_w_xq, m_w_xkv, m_w_xo, m_norm_ffn_pre, m_norm_ffn_post, m_w_gate_up, m_w_down, v_norm_mix_pre, v_norm_mix_post, v_w_in, v_conv_short, v_conv_gdn, v_gdn_a_log, v_gdn_dt_bias, v_gdn_norm, v_w_out, v_norm_mem, v_norm_xattn_pre, v_norm_xattn_post, v_w_xq, v_w_xkv, v_w_xo, v_norm_ffn_pre, v_norm_ffn_post, v_w_gate_up, v_w_down):
    given = dict(locals())
    weights = {n: given[n] for n in WEIGHTS}
    me = _slot(_place())

    def in_place(shard):
        full = jnp.zeros(shard.shape[:-1] + (shard.shape[-1] * N_DEV,), f32)
        return lax.dynamic_update_slice_in_dim(full, shard, me * shard.shape[-1], axis=shard.ndim - 1)

    placed = [in_place(conv_short), in_place(conv_gdn)]
    conv_short_full, conv_gdn_full = _unpack_small(all_reduce_small(_pack_small(placed), "gather_conv"), placed)
    small = {n: weights[n] for n in NORMS + ("gdn_a_log", "gdn_dt_bias", "gdn_norm")}
    small["conv_short"], small["conv_gdn"] = conv_short_full, conv_gdn_full

    cos_t, sin_t = _rope_tables(positions[0])
    loss, grad_x, big_layers, small_layers = _forward_backward(
        x[0], _pack_big(weights), small, mem[0], cos_t, sin_t, loss_target[0])
    grads = _unpack_shard(big_layers)

    names = sorted(small)
    parts = [jnp.stack([layer[n] for layer in small_layers]) for n in names] + [loss.reshape(1)]
    reduced = _unpack_small(all_reduce_small(_pack_small(parts), "reduce_small"), parts)
    loss = reduced[-1][0]
    for n, g in zip(names, reduced[:-1]):
        if n in ("conv_short", "conv_gdn"):
            width = weights[n].shape[-1]
            g = lax.dynamic_slice_in_dim(g, me * width, width, axis=g.ndim - 1)
        grads[n] = g

    delta, new_m, new_v = {}, {}, {}
    for n in WEIGHTS:
        delta[n], new_m[n], new_v[n] = adamw(weights[n], grads[n], given["m_" + n], given["v_" + n], "adamw_" + n)
    return (loss, grad_x[None], *[grads[n] for n in WEIGHTS], *[delta[n] for n in WEIGHTS],
            *[new_m[n] for n in WEIGHTS], *[new_v[n] for n in WEIGHTS])
```

```python
import functools

import jax
import jax.numpy as jnp
from jax import lax
from jax.experimental import pallas as pl
from jax.experimental.pallas import tpu as pltpu

f32 = jnp.float32
bf16 = jnp.bfloat16
MESH = pl.DeviceIdType.MESH

N_DEV = 8
DEPTH = 4
D_MODEL = 1024
EPS = 1e-6
ATTN_HEADS, ATTN_HEAD_DIM = 4, 64
ATTN_WIDTH = ATTN_HEADS * ATTN_HEAD_DIM
DILATIONS = (1, 4, 16)
QB = 128
ROPE_THETA = 500000.0
ROPE_DIM = ATTN_HEAD_DIM // 4
CONV_WIDTH = 256
GDN_HEADS, GDN_HEAD_DIM = 4, 128
GDN_WIDTH = GDN_HEADS * GDN_HEAD_DIM
GDN_CHUNK = 64
XATTN_HEADS, XATTN_HEAD_DIM = 4, 256
LANES = 128
ROW_TILE = 512
VMEM_LIMIT = 56 * 1024 * 1024

ADAM_LR, ADAM_B1, ADAM_B2, ADAM_EPS, ADAM_WD, ADAM_STEP = 0.001, 0.9, 0.999, 1e-08, 0.01, 10

COL_SHARDED = ("w_in", "w_xkv", "w_gate_up")
ROW_SHARDED = (("w_out", 128), ("w_xq", 128), ("w_xo", 128), ("w_down", 352))
HEAD_GROUP = ("w_in",)
TAIL_GROUP = ("w_gate_up", "w_xkv", "rows")
NORMS = ("norm_mix_pre", "norm_mix_post", "norm_mem", "norm_xattn_pre", "norm_xattn_post", "norm_ffn_pre", "norm_ffn_post")
WEIGHTS = ("norm_mix_pre", "norm_mix_post", "w_in", "conv_short", "conv_gdn", "gdn_a_log", "gdn_dt_bias", "gdn_norm", "w_out",
           "norm_mem", "norm_xattn_pre", "norm_xattn_post", "w_xq", "w_xkv", "w_xo", "norm_ffn_pre", "norm_ffn_post",
           "w_gate_up", "w_down")


def _params(n_grid):
    return pltpu.CompilerParams(dimension_semantics=("arbitrary",) * n_grid, vmem_limit_bytes=VMEM_LIMIT)


def _pick(n, cands):
    for c in cands:
        if n % c == 0:
            return c
    return n


MXU_FLOPS = 9.0e14
HBM_BYTES_PER_S = 2.5e12
VMEM_RMW_BYTES_PER_S = 7.0e12
STEP_S = 0.4e-6
MATMUL_VMEM = 44 * 1024 * 1024


def _tiles(m, n, k, sa, sb, so):
    def divisors(d):
        return sorted({d} | {d // s for s in range(1, d // LANES + 1) if d % s == 0 and (d // s) % LANES == 0}, reverse=True)

    best = None
    for tk in divisors(k):
        nk = k // tk
        for tm in divisors(m):
            for tn_ in divisors(n):
                per_step = tm * tk * sa + tk * tn_ * sb + tm * tn_ * so
                vmem = 2 * per_step + (tm * tn_ * 4 if nk > 1 else 0)
                vmem += (tm * tk * 2 if sa == 4 else 0) + (tk * tn_ * 2 if sb == 4 else 0) + tm * tn_ * 4
                if vmem > MATMUL_VMEM:
                    continue
                moved = m * k * sa * (1 if nk == 1 else n // tn_) + k * n * sb * (1 if nk == 1 and n == tn_ else m // tm) + m * n * so
                busy = 2 * m * n * k / MXU_FLOPS + (m * n * 8 * nk / VMEM_RMW_BYTES_PER_S if nk > 1 else 0)
                cost = max(moved / HBM_BYTES_PER_S, busy) + per_step / HBM_BYTES_PER_S + (m // tm) * (n // tn_) * nk * STEP_S
                if best is None or cost < best[0]:
                    best = (cost, tm, tn_, tk)
    return best[1:]


def _mm(a, b, ta, tb, out_dtype, name):
    m, k = (a.shape[1], a.shape[0]) if ta else a.shape
    n = b.shape[0] if tb else b.shape[1]
    tm, tn, tk = _tiles(m, n, k, a.dtype.itemsize, b.dtype.itemsize, jnp.dtype(out_dtype).itemsize)
    nk = k // tk
    a_spec = pl.BlockSpec((tk, tm), lambda i, j, kk: (kk, i)) if ta else pl.BlockSpec((tm, tk), lambda i, j, kk: (i, kk))
    b_spec = pl.BlockSpec((tn, tk), lambda i, j, kk: (j, kk)) if tb else pl.BlockSpec((tk, tn), lambda i, j, kk: (kk, j))
    dims = (((0 if ta else 1,), (1 if tb else 0,)), ((), ()))

    def body(a_ref, b_ref, o_ref, *acc):
        kk = pl.program_id(2)
        p = lax.dot_general(a_ref[...].astype(bf16), b_ref[...].astype(bf16), dims, preferred_element_type=f32)
        if nk == 1:
            o_ref[...] = p.astype(o_ref.dtype)
            return
        acc_ref, = acc

        @pl.when(kk == 0)
        def _():
            acc_ref[...] = p

        @pl.when(kk > 0)
        def _():
            acc_ref[...] += p

        @pl.when(kk == nk - 1)
        def _():
            o_ref[...] = acc_ref[...].astype(o_ref.dtype)

    return pl.pallas_call(
        body, name=name, grid=(m // tm, n // tn, nk), in_specs=[a_spec, b_spec],
        out_specs=pl.BlockSpec((tm, tn), lambda i, j, kk: (i, j)), out_shape=jax.ShapeDtypeStruct((m, n), out_dtype),
        scratch_shapes=[pltpu.VMEM((tm, tn), f32)] if nk > 1 else [], compiler_params=_params(3))(a, b)


def _linear(x, w, name):
    @jax.custom_vjp
    def lin(x, w):
        return _mm(x, w, False, False, f32, name + "_y")

    def lin_f(x, w):
        return _mm(x, w, False, False, f32, name + "_y"), (x, w)

    def lin_b(res, dy):
        x, w = res
        return _mm(dy, w, False, True, f32, name + "_dx"), _mm(x, dy, True, False, bf16, name + "_dw")

    lin.defvjp(lin_f, lin_b)
    return lin(x, w)


def _bdot(a, b, form):
    dims = {"nn": ((1,), (0,)), "nt": ((1,), (1,)), "tn": ((0,), (0,))}[form]
    return lax.dot_general(a.astype(bf16), b.astype(bf16), (dims, ((), ())), preferred_element_type=f32)


def ffn_forward(hn, w_gate_up, w_down, name):
    s, k = hn.shape
    n_blocks, _, width = w_gate_up.shape
    half = n_blocks // 2
    tm = 512
    blocked = jax.ShapeDtypeStruct((half, s, width), bf16)

    def act_body(x_ref, wg_ref, wu_ref, gate_ref, up_ref, act_ref):
        x = x_ref[...]
        gate, up = _bdot(x, wg_ref[...], "nn"), _bdot(x, wu_ref[...], "nn")
        gate_ref[...], up_ref[...] = gate.astype(bf16), up.astype(bf16)
        act_ref[...] = (jax.nn.silu(gate) * up).astype(bf16)

    tile = pl.BlockSpec((None, tm, width), lambda i, d: (d, i, 0))
    gate, up, act = pl.pallas_call(
        act_body, name=name + "_act", grid=(s // tm, half),
        in_specs=[pl.BlockSpec((tm, k), lambda i, d: (i, 0)), pl.BlockSpec((None, k, width), lambda i, d: (d, 0, 0)),
                  pl.BlockSpec((None, k, width), lambda i, d: (d + half, 0, 0))],
        out_specs=[tile] * 3, out_shape=[blocked] * 3, compiler_params=_params(2))(hn, w_gate_up, w_gate_up)

    n = w_down.shape[1]
    tn = 512

    def y_body(act_ref, w_ref, y_ref):
        y_ref[...] = sum(_bdot(act_ref[d], w_ref[d * width:(d + 1) * width, :], "nn") for d in range(half))

    y = pl.pallas_call(
        y_body, name=name + "_y", grid=(s // tm, n // tn),
        in_specs=[pl.BlockSpec((half, tm, width), lambda i, j: (0, i, 0)), pl.BlockSpec((half * width, tn), lambda i, j: (0, j))],
        out_specs=pl.BlockSpec((tm, tn), lambda i, j: (i, j)), out_shape=_sds((s, n)), compiler_params=_params(2))(act, w_down)
    return y, (hn, w_gate_up, w_down, gate, up, act)


def ffn_backward(saved, dy, name, exchange=None):
    hn, w_gate_up, w_down, gate, up, act = saved
    s, k = hn.shape
    n_blocks, _, width = w_gate_up.shape
    half = n_blocks // 2
    n = w_down.shape[1]
    tm = 512
    blocked = jax.ShapeDtypeStruct((half, s, width), bf16)
    carried = len(exchange.operands) if exchange else 0
    steps = (s // tm, half)

    def dact_body(dy_ref, w_ref, gate_ref, up_ref, *refs):
        x_refs, refs = refs[:carried], refs[carried:]
        (dgate_ref, dup_ref), refs = refs[:2], refs[2:]
        if exchange:
            at = pl.program_id(0) * steps[1] + pl.program_id(1)
            start, wait = exchange.bind(x_refs, refs[:carried], refs[carried:])
            pl.when(at == 0)(start)
        d_act = _bdot(dy_ref[...], w_ref[...], "nt")
        g, u = gate_ref[...].astype(f32), up_ref[...].astype(f32)
        sig = jax.nn.sigmoid(g)
        dgate_ref[...] = (d_act * u * sig * (1.0 + g * (1.0 - sig))).astype(bf16)
        dup_ref[...] = (d_act * g * sig).astype(bf16)
        if exchange:
            pl.when(at == steps[0] * steps[1] - 1)(wait)

    tile = pl.BlockSpec((None, tm, width), lambda i, d: (d, i, 0))
    d_gate, d_up, *landed = pl.pallas_call(
        dact_body, name=name + "_dact", grid=steps,
        in_specs=[pl.BlockSpec((tm, n), lambda i, d: (i, 0)), pl.BlockSpec((width, n), lambda i, d: (d, 0)), tile, tile] + [ANY] * carried,
        out_specs=[tile, tile] + [ANY] * carried, out_shape=[blocked, blocked] + (exchange.out_shapes if exchange else []),
        scratch_shapes=exchange.scratch if exchange else [],
        compiler_params=_params(2))(dy, w_down, gate, up, *(exchange.operands if exchange else []))

    def dx_body(dg_ref, du_ref, w_ref, dx_ref):
        dx_ref[...] = sum(_bdot(dg_ref[d], w_ref[d], "nt") + _bdot(du_ref[d], w_ref[d + half], "nt") for d in range(half))

    tx = 256
    rows = pl.BlockSpec((half, tx, width), lambda i: (0, i, 0))
    dx = pl.pallas_call(
        dx_body, name=name + "_dx", grid=(s // tx,), in_specs=[rows, rows, _whole(w_gate_up.shape)],
        out_specs=pl.BlockSpec((tx, k), lambda i: (i, 0)), out_shape=_sds((s, k)), compiler_params=_params(1))(d_gate, d_up, w_gate_up)

    def dw1_body(x_ref, dg_ref, du_ref, dw_ref):
        d_block = jnp.where(pl.program_id(0) < half, dg_ref[...], du_ref[...])
        dw_ref[...] = _bdot(x_ref[...], d_block, "tn").astype(bf16)

    d_w_gate_up = pl.pallas_call(
        dw1_body, name=name + "_dw1", grid=(n_blocks,),
        in_specs=[_whole((s, k)), pl.BlockSpec((None, s, width), lambda b: (jnp.minimum(b, half - 1), 0, 0)),
                  pl.BlockSpec((None, s, width), lambda b: (jnp.maximum(b - half, 0), 0, 0))],
        out_specs=pl.BlockSpec((None, k, width), lambda b: (b, 0, 0)), out_shape=jax.ShapeDtypeStruct(w_gate_up.shape, bf16),
        compiler_params=_params(1))(hn, d_gate, d_up)

    tn = 512

    def dw2_body(act_ref, dy_ref, dw_ref):
        dw_ref[...] = _bdot(act_ref[...], dy_ref[...], "tn").astype(bf16)

    d_w_down = pl.pallas_call(
        dw2_body, name=name + "_dw2", grid=(half, n // tn),
        in_specs=[pl.BlockSpec((None, s, width), lambda d, j: (d, 0, 0)), pl.BlockSpec((s, tn), lambda d, j: (0, j))],
        out_specs=pl.BlockSpec((width, tn), lambda d, j: (d, j)), out_shape=jax.ShapeDtypeStruct(w_down.shape, bf16),
        compiler_params=_params(2))(act, dy)
    return dx, d_w_gate_up, d_w_down, landed


def _split_cols(x, widths):
    edges = [sum(widths[:i]) for i in range(len(widths) + 1)]

    def cut(x):
        return tuple(x[:, a:b] for a, b in zip(edges[:-1], edges[1:]))

    @jax.custom_vjp
    def split(x):
        return cut(x)

    split.defvjp(lambda x: (cut(x), None), lambda _, cts: (jnp.concatenate(cts, axis=1),))
    return split(x)


def _block_op(name, f, grid, in_specs, out_defs, arrays, diff, acc=None, gdefs=None):
    acc, gdefs = acc or {}, gdefs or {}
    n_in, n_out, n_grid = len(in_specs), len(out_defs), len(grid)

    def fwd_call(*xs):
        def body(*refs):
            outs = f(*[r[...] for r in refs[:n_in]])
            for r, o in zip(refs[n_in:], outs):
                r[...] = o.astype(r.dtype)

        return pl.pallas_call(
            body, name=name + "_fwd", grid=grid, in_specs=in_specs, out_specs=[d[1] for d in out_defs],
            out_shape=[d[0] for d in out_defs], compiler_params=_params(n_grid))(*xs)

    def bwd_call(*xs_and_cts):
        def body(*refs):
            xs = [r[...] for r in refs[:n_in]]
            cts = tuple(r[...] for r in refs[n_in:n_in + n_out])

            def of_diff(*dx):
                full = list(xs)
                for i, v in zip(diff, dx):
                    full[i] = v
                return tuple(f(*full))

            _, vjp = jax.vjp(of_diff, *[xs[i] for i in diff])
            grads = vjp(cts)
            for i, g, r in zip(diff, grads, refs[n_in + n_out:]):
                if i in acc:
                    first = functools.reduce(jnp.logical_and, [pl.program_id(a) == 0 for a in acc[i]])

                    @pl.when(first)
                    def _(r=r):
                        r[...] = jnp.zeros_like(r)

                    r[...] += g.astype(r.dtype)
                else:
                    r[...] = g.astype(r.dtype)

        g_defs = [gdefs.get(i, (jax.ShapeDtypeStruct(arrays[i].shape, f32), in_specs[i])) for i in diff]
        return pl.pallas_call(
            body, name=name + "_bwd", grid=grid, in_specs=list(in_specs) + [d[1] for d in out_defs],
            out_specs=[d[1] for d in g_defs], out_shape=[d[0] for d in g_defs], compiler_params=_params(n_grid))(*xs_and_cts)

    return fwd_call, bwd_call


def _simple_op(name, f, grid, in_specs, out_defs, arrays, diff, acc=None):
    fwd_call, bwd_call = _block_op(name, f, grid, in_specs, out_defs, arrays, diff, acc)

    @jax.custom_vjp
    def op(*xs):
        return tuple(fwd_call(*xs))

    def op_f(*xs):
        return tuple(fwd_call(*xs)), xs

    def op_b(xs, cts):
        grads = bwd_call(*xs, *cts)
        out = [jnp.zeros_like(x) for x in xs]
        for i, g in zip(diff, grads):
            out[i] = g
        return tuple(out)

    op.defvjp(op_f, op_b)
    return op(*arrays)


def _rows(width, tile=ROW_TILE):
    return pl.BlockSpec((tile, width), lambda i: (i, 0))


def _whole(shape):
    return pl.BlockSpec(shape, lambda *_: (0,) * len(shape))


def _sds(shape):
    return jax.ShapeDtypeStruct(shape, f32)


def _rms(x, w):
    return x * lax.rsqrt(jnp.mean(x * x, axis=-1, keepdims=True) + EPS) * w


def rms_norm(x, w, name):
    r, d = x.shape
    tile = min(ROW_TILE, r)
    return _simple_op(name, lambda x, w: (_rms(x, w),), (r // tile,), [_rows(d, tile), _whole((1, d))],
                      [(_sds((r, d)), _rows(d, tile))], (x, w), (0, 1), {1: (0,)})[0]


def add_norm(h, y, w, name):
    r, d = h.shape
    return _simple_op(name, lambda h, y, w: (h + _rms(y, w),), (r // ROW_TILE,), [_rows(d), _rows(d), _whole((1, d))],
                      [(_sds((r, d)), _rows(d))], (h, y, w), (0, 1, 2), {2: (0,)})[0]


def add_norm_then_norm(h, y, w_post, w_pre, name):
    r, d = h.shape

    def f(h, y, w_post, w_pre):
        h_new = h + _rms(y, w_post)
        return h_new, _rms(h_new, w_pre)

    return _simple_op(name, f, (r // ROW_TILE,), [_rows(d), _rows(d), _whole((1, d)), _whole((1, d))],
                      [(_sds((r, d)), _rows(d))] * 2, (h, y, w_post, w_pre), (0, 1, 2, 3), {2: (0,), 3: (0,)})


def _swap8(x):
    def raw(x):
        lane = lax.broadcasted_iota(jnp.int32, x.shape, 1) % ATTN_HEAD_DIM
        half = ROPE_DIM // 2
        up = pltpu.roll(x, x.shape[1] - half, axis=1)
        down = pltpu.roll(x, half, axis=1)
        return jnp.where(lane < half, up, jnp.where(lane < ROPE_DIM, down, 0.0))

    @jax.custom_vjp
    def swap(x):
        return raw(x)

    swap.defvjp(lambda x: (raw(x), None), lambda _, g: (raw(g),))
    return swap(x)


def rope(x, cos_t, sin_t, scale, name):
    r, d = x.shape
    return _simple_op(name, lambda x, c, s: ((x * c + _swap8(x) * s) * scale,), (r // ROW_TILE,), [_rows(d)] * 3,
                      [(_sds((r, d)), _rows(d))], (x, cos_t, sin_t), (0,))[0]


def _shift_rows(x, k):
    n = x.shape[0]

    def down(x):
        row = lax.broadcasted_iota(jnp.int32, x.shape, 0)
        return jnp.where(row >= k, pltpu.roll(x, k, axis=0), 0.0)

    def up(x):
        row = lax.broadcasted_iota(jnp.int32, x.shape, 0)
        return jnp.where(row < n - k, pltpu.roll(x, n - k, axis=0), 0.0)

    @jax.custom_vjp
    def shift(x):
        return down(x)

    shift.defvjp(lambda x: (down(x), None), lambda _, g: (up(g),))
    return shift(x)


def _causal_conv(x, w):
    taps = w.shape[0]
    y = x * w[taps - 1:taps, :]
    for j in range(taps - 1):
        y = y + _shift_rows(x, taps - 1 - j) * w[j:j + 1, :]
    return y


def _cols(rows, at=0):
    return pl.BlockSpec((rows, LANES), lambda j: (0, at + j))


def short_conv(cb, cc, cx, w, name):
    s, c = cb.shape
    taps = w.shape[0]
    return _simple_op(name, lambda b, c_, x, w: (b * _causal_conv(c_ * x, w),), (c // LANES,),
                      [_cols(s)] * 3 + [_cols(taps)], [(_sds((s, c)), _cols(s))], (cb, cc, cx, w), (0, 1, 2, 3))[0]


def gdn_pre(qkv, w, name):
    s, c = qkv.shape
    taps = w.shape[0]

    def f(x, w):
        j = pl.program_id(0)
        y = jax.nn.silu(_causal_conv(x, w))
        normed = y * lax.rsqrt(jnp.sum(y * y, axis=-1, keepdims=True) + EPS)
        scale = jnp.where(j < GDN_HEADS, GDN_HEAD_DIM ** -0.5, 1.0).astype(f32)
        return (jnp.where(j < 2 * GDN_HEADS, normed * scale, y),)

    return _simple_op(name, f, (c // LANES,), [_cols(s), _cols(taps)], [(_sds((s, c)), _cols(s))], (qkv, w), (0, 1))[0]


def gate_beta(ab, pv, name):
    s = ab.shape[0]

    def f(ab, pv):
        lane = lax.broadcasted_iota(jnp.int32, ab.shape, 1)
        g = -jnp.exp(pv[0:1, :]) * jax.nn.softplus(ab + pv[1:2, :])
        return (jnp.where(lane < GDN_HEADS, g, jnp.where(lane < 2 * GDN_HEADS, jax.nn.sigmoid(ab), 0.0)),)

    return _simple_op(name, f, (s // ROW_TILE,), [_rows(LANES), _whole((8, LANES))], [(_sds((s, LANES)), _rows(LANES))],
                      (ab, pv), (0, 1), {1: (0,)})[0]


def gdn_post(o, gate, w, name):
    s, c = o.shape

    def f(o, g, w):
        heads = [slice(hd * LANES, (hd + 1) * LANES) for hd in range(c // LANES)]
        return (jnp.concatenate([_rms(o[:, hd], w) * jax.nn.silu(g[:, hd]) for hd in heads], axis=1),)

    return _simple_op(name, f, (s // ROW_TILE,), [_rows(c), _rows(c), _whole((1, LANES))], [(_sds((s, c)), _rows(c))],
                      (o, gate, w), (0, 1, 2), {2: (0,)})[0]


def attn_merge(outs, lses, name):
    s, c = outs[0].shape

    def f(o1, o2, o3, l1, l2, l3):
        m = lax.stop_gradient(jnp.maximum(jnp.maximum(l1, l2), l3))
        e1, e2, e3 = jnp.exp(l1 - m), jnp.exp(l2 - m), jnp.exp(l3 - m)
        return ((e1 * o1 + e2 * o2 + e3 * o3) / (e1 + e2 + e3),)

    return _simple_op(name, f, (s // ROW_TILE,), [_rows(c)] * 6, [(_sds((s, c)), _rows(c))], (*outs, *lses), tuple(range(6)))[0]


def loss_rows(y, target, name):
    s, d = y.shape
    nt = s // ROW_TILE

    def f(y, t):
        e = y - t
        part = 0.5 * jnp.sum(jnp.mean(e * e, axis=-1, keepdims=True), axis=0, keepdims=True)
        return (jnp.broadcast_to(part * (1.0 / (8 * LANES)), (8, LANES)),)

    out = _simple_op(name, f, (nt,), [_rows(d)] * 2, [(_sds((nt * 8, LANES)), pl.BlockSpec((8, LANES), lambda i: (i, 0)))],
                     (y, target), (0,))[0]
    return jnp.sum(out)


def _mxu(a, b, form):
    dims = {"nn": ((1,), (0,)), "nt": ((1,), (1,)), "tn": ((0,), (0,))}

    def raw(a, b, form):
        return lax.dot_general(a.astype(bf16), b.astype(bf16), (dims[form], ((), ())), preferred_element_type=f32)

    @jax.custom_vjp
    def prod(a, b):
        return raw(a, b, form)

    def prod_b(res, ct):
        a, b = res
        if form == "nn":
            return raw(ct, b, "nt"), raw(a, ct, "tn")
        if form == "nt":
            return raw(ct, b, "nn"), raw(ct, a, "tn")
        return raw(b, ct, "nt"), raw(a, ct, "nn")

    prod.defvjp(lambda a, b: (raw(a, b, form), (a, b)), prod_b)
    return prod(a, b)


def _masked_heads_attention(q, keys, values, seen):
    dh = ATTN_HEAD_DIM
    outs, lses = [], []
    for hd in range(q.shape[1] // dh):
        at = slice(hd * dh, (hd + 1) * dh)
        sc = jnp.where(seen, _mxu(q[:, at], keys[:, at], "nt"), -jnp.inf)
        m = lax.stop_gradient(jnp.max(sc, axis=-1, keepdims=True))
        p = jnp.exp(sc - m)
        l = jnp.sum(p, axis=-1, keepdims=True)
        outs.append(_mxu(p / l, values[:, at], "nn"))
        lses.append(jnp.broadcast_to(m + jnp.log(l), (q.shape[0], dh)))
    return jnp.concatenate(outs, axis=1), jnp.concatenate(lses, axis=1)


def band_attention(q, k, v, nb, name):
    r, qb, width = q.shape

    def f(q, kp, kc, vp, vc):
        has_prev = (pl.program_id(0) % nb) > 0
        i = lax.broadcasted_iota(jnp.int32, (qb, 2 * qb), 0)
        j = lax.broadcasted_iota(jnp.int32, (qb, 2 * qb), 1)
        seen = jnp.logical_or(jnp.logical_and(jnp.logical_and(j < qb, j >= i), has_prev), jnp.logical_and(j >= qb, j - qb <= i))
        return _masked_heads_attention(q, jnp.concatenate([kp, kc], axis=0), jnp.concatenate([vp, vc], axis=0), seen)

    blk = (None, qb, width)
    cur = pl.BlockSpec(blk, lambda b: (b, 0, 0))
    prev = pl.BlockSpec(blk, lambda b: (jnp.maximum(b - 1, 0), 0, 0))
    shape = _sds((r, qb, width))
    fwd_call, bwd_call = _block_op(name, f, (r,), [cur, prev, cur, prev, cur], [(shape, cur), (shape, cur)],
                                   (q, k, k, v, v), (0, 1, 2, 3, 4), gdefs={1: (shape, cur), 3: (shape, cur)})

    def to_prev(g):
        return jnp.concatenate([g[1:], jnp.zeros_like(g[:1])], axis=0)

    @jax.custom_vjp
    def op(q, k, v):
        return tuple(fwd_call(q, k, k, v, v))

    def op_b(res, cts):
        q, k, v = res
        dq, dkp, dkc, dvp, dvc = bwd_call(q, k, k, v, v, *cts)
        return dq, dkc + to_prev(dkp), dvc + to_prev(dvp)

    op.defvjp(lambda q, k, v: (tuple(fwd_call(q, k, k, v, v)), (q, k, v)), op_b)
    return op(q, k, v)


def dilated_attention(q, k, v, name):
    s = q.shape[0]
    outs, lses = [], []
    for d in DILATIONS:
        length = s // d
        nb = length // QB
        def to_residue(t):
            return t.reshape(length, d, ATTN_WIDTH).transpose(1, 0, 2).reshape(d * nb, QB, ATTN_WIDTH)

        def from_residue(t):
            return t.reshape(d, length, ATTN_WIDTH).transpose(1, 0, 2).reshape(s, ATTN_WIDTH)

        o, lse = band_attention(to_residue(q), to_residue(k), to_residue(v), nb, f"{name}_d{d}")
        outs.append(from_residue(o))
        lses.append(from_residue(lse))
    return attn_merge(outs, lses, name + "_merge")


def cross_attention(q, kv, name):
    s = q.shape[0]
    m = kv.shape[0]
    width = XATTN_HEADS * XATTN_HEAD_DIM
    tq = 512

    def f(q, k, v):
        sc = _mxu(q, k, "nt") * (XATTN_HEAD_DIM ** -0.5)
        mx = lax.stop_gradient(jnp.max(sc, axis=-1, keepdims=True))
        p = jnp.exp(sc - mx)
        return (_mxu(p / jnp.sum(p, axis=-1, keepdims=True), v, "nn"),)

    q_spec = pl.BlockSpec((tq, XATTN_HEAD_DIM), lambda a, i: (i, a))
    k_spec = pl.BlockSpec((m, XATTN_HEAD_DIM), lambda a, i: (0, a))
    v_spec = pl.BlockSpec((m, XATTN_HEAD_DIM), lambda a, i: (0, a + XATTN_HEADS))
    half = _sds((m, width))
    fwd_call, bwd_call = _block_op(name, f, (XATTN_HEADS, s // tq), [q_spec, k_spec, v_spec], [(_sds((s, width)), q_spec)],
                                   (q, kv, kv), (0, 1, 2), acc={1: (1,), 2: (1,)}, gdefs={1: (half, k_spec), 2: (half, k_spec)})

    @jax.custom_vjp
    def op(q, kv):
        return fwd_call(q, kv, kv)[0]

    def op_b(res, ct):
        q, kv = res
        dq, dk, dv = bwd_call(q, kv, kv, ct)
        return dq, jnp.concatenate([dk, dv], axis=1)

    op.defvjp(lambda q, kv: (fwd_call(q, kv, kv)[0], (q, kv)), op_b)
    return op(q, kv)


def _hi(a, b, form="nn"):
    dims = {"nn": ((1,), (0,)), "nt": ((1,), (1,)), "tn": ((0,), (0,))}[form]
    return lax.dot_general(a, b, (dims, ((), ())), precision=lax.Precision.HIGH, preferred_element_type=f32)


def _running_sum(g):
    def raw(x, form):
        c = x.shape[0]
        tri = (lax.broadcasted_iota(jnp.int32, (c, c), 0) >= lax.broadcasted_iota(jnp.int32, (c, c), 1)).astype(bf16)
        hi = x.astype(bf16)
        rest = x - hi.astype(f32)
        mid = rest.astype(bf16)
        low = (rest - mid.astype(f32)).astype(bf16)
        dims = (((1,) if form == "nn" else (0,), (0,)), ((), ()))
        return sum(lax.dot_general(tri, part, dims, preferred_element_type=f32) for part in (hi, mid, low))

    @jax.custom_vjp
    def run(x):
        return raw(x, "nn")

    run.defvjp(lambda x: (raw(x, "nn"), None), lambda _, ct: (raw(ct, "tn"),))
    return run(g)


def _unit_lower_inverse(a):
    c = a.shape[0]
    eye = (lax.broadcasted_iota(jnp.int32, (c, c), 0) == lax.broadcasted_iota(jnp.int32, (c, c), 1)).astype(f32)
    inv, power = eye - a, -a
    for _ in range(c.bit_length() - 2):
        power = _hi(power, power)
        inv = inv + _hi(inv, power)
    return inv


def _known_inverse(a, t):
    @jax.custom_vjp
    def inv(a, t):
        return t

    def inv_b(t, ct):
        return -_hi(_hi(t, ct, "tn"), t, "nt"), jnp.zeros_like(t)

    inv.defvjp(lambda a, t: (t, t), inv_b)
    return inv(a, t)


def _delta_chunk(q, k, v, g, beta, s0, known_inv=None):
    c = q.shape[0]
    i = lax.broadcasted_iota(jnp.int32, (c, c), 0)
    j = lax.broadcasted_iota(jnp.int32, (c, c), 1)
    causal, strict = i >= j, i > j
    dec = _running_sum(g)
    dec_i = dec[:, :c]
    rel = jnp.exp(jnp.where(causal, dec_i - dec_i.T, -jnp.inf))
    k_beta = k * beta
    on_k = _mxu(jnp.concatenate([k_beta, q], axis=0), k, "nt")
    a = jnp.where(strict, on_k[:c] * rel, 0.0)
    attn = jnp.where(causal, on_k[c:] * rel, 0.0)
    inv = _unit_lower_inverse(a) if known_inv is None else _known_inverse(a, known_inv)
    e_dec = jnp.exp(dec)
    solved = _hi(inv, jnp.concatenate([v * beta, k_beta * e_dec], axis=1))
    u, w = solved[:, :v.shape[1]], solved[:, v.shape[1]:]
    total = jnp.sum(g, axis=0, keepdims=True)
    on_state = _mxu(jnp.concatenate([w, q * e_dec], axis=0), s0, "nn")
    v_new = u - on_state[:c]
    o = on_state[c:] + _mxu(attn, v_new, "nn")
    s1 = s0 * jnp.exp(total) + _mxu(k * jnp.exp(total - dec), v_new, "tn")
    return o, s1, inv


def _delta_rule_call(name, walk, n, in_specs, out_specs, out_shape, operands, exchange):
    n_in, n_out = len(in_specs), len(out_specs)
    carried = len(exchange.operands) if exchange else 0

    def body(*refs):
        ins, refs = refs[:n_in], refs[n_in:]
        x_refs, refs = refs[:carried], refs[carried:]
        outs, refs = refs[:n_out], refs[n_out:]
        land_refs, (state, *sems) = refs[:carried], refs[carried:]
        step = pl.program_id(0)
        if exchange:
            start, finish = exchange.bind(x_refs, land_refs, sems)
            pl.when(step == 0)(start)

        @pl.when(step == 0)
        def _():
            state[...] = jnp.zeros_like(state)

        walk(ins, outs, state)
        if exchange:
            pl.when(step == n - 1)(finish)

    return pl.pallas_call(
        body, name=name, grid=(n,), in_specs=list(in_specs) + [ANY] * carried, out_specs=list(out_specs) + [ANY] * carried,
        out_shape=list(out_shape) + (exchange.out_shapes if exchange else []),
        scratch_shapes=[pltpu.VMEM((GDN_HEAD_DIM, GDN_WIDTH), f32)] + (exchange.scratch if exchange else []),
        compiler_params=_params(1))(*operands, *(exchange.operands if exchange else []))


def _delta_heads():
    heads = [slice(hd * GDN_HEAD_DIM, (hd + 1) * GDN_HEAD_DIM) for hd in range(GDN_HEADS)]
    inv_at = [slice(hd * GDN_CHUNK, (hd + 1) * GDN_CHUNK) for hd in range(GDN_HEADS)]
    return heads, inv_at


def _head_chunk(q, k, v, gates, s0, head, known_inv=None):
    g = jnp.broadcast_to(gates[:, head:head + 1], q.shape)
    beta = jnp.broadcast_to(gates[:, GDN_HEADS + head:GDN_HEADS + head + 1], q.shape)
    return _delta_chunk(q, k, v, g, beta, s0, known_inv)


def delta_rule_fwd(q, k, v, gates, name, exchange=None):
    s, width = q.shape
    c, dk = GDN_CHUNK, GDN_HEAD_DIM
    n = s // c
    heads, inv_at = _delta_heads()

    def walk(ins, outs, state):
        q_ref, k_ref, v_ref, gates_ref = ins
        o_ref, s_in_ref, inv_ref = outs
        s_in_ref[...] = state[...]
        gates = gates_ref[...]
        xs = [[r[:, hd] for r in (q_ref, k_ref, v_ref)] + [gates, state[:, hd], i] for i, hd in enumerate(heads)]
        ys = [_head_chunk(*x) for x in xs]
        for hd, at, (o, s1, inv) in zip(heads, inv_at, ys):
            o_ref[:, hd], state[:, hd], inv_ref[:, at] = o, s1, inv

    blk = pl.BlockSpec((c, width), lambda t: (t, 0))
    gt = pl.BlockSpec((c, LANES), lambda t: (t, 0))
    st = pl.BlockSpec((dk, width), lambda t: (t, 0))
    iv = pl.BlockSpec((c, GDN_HEADS * c), lambda t: (t, 0))
    return _delta_rule_call(name, walk, n, [blk] * 3 + [gt], [blk, st, iv],
                            [_sds((s, width)), _sds((n * dk, width)), _sds((s, GDN_HEADS * c))], (q, k, v, gates), exchange)


def delta_rule_bwd(q, k, v, gates, s_in, inv, do, name, exchange=None):
    s, width = q.shape
    c, dk = GDN_CHUNK, GDN_HEAD_DIM
    n = s // c
    heads, inv_at = _delta_heads()

    def walk(ins, outs, dstate):
        q_ref, k_ref, v_ref, gates_ref, s_ref, inv_ref, do_ref = ins
        dq_ref, dk_ref, dv_ref, dgates_ref = outs
        gates = gates_ref[...]
        xs = [[r[:, hd] for r in (q_ref, k_ref, v_ref)] + [gates, s_ref[:, hd]] for hd in heads]
        known = [inv_ref[:, at] for at in inv_at]
        cts = [(do_ref[:, hd], dstate[:, hd]) for hd in heads]
        grads = []
        for i, (x, t, ct) in enumerate(zip(xs, known, cts)):
            _, vjp = jax.vjp(lambda *y, t=t, i=i: _head_chunk(*y, i, known_inv=t)[:2], *x)
            grads.append(vjp(ct))
        dgates = grads[0][3]
        for g in grads[1:]:
            dgates = dgates + g[3]
        dgates_ref[...] = dgates
        for hd, (dq, dk_, dv, _, ds0) in zip(heads, grads):
            dq_ref[:, hd], dk_ref[:, hd], dv_ref[:, hd], dstate[:, hd] = dq, dk_, dv, ds0

    blk = pl.BlockSpec((c, width), lambda t: (n - 1 - t, 0))
    gt = pl.BlockSpec((c, LANES), lambda t: (n - 1 - t, 0))
    st = pl.BlockSpec((dk, width), lambda t: (n - 1 - t, 0))
    iv = pl.BlockSpec((c, GDN_HEADS * c), lambda t: (n - 1 - t, 0))
    return _delta_rule_call(name, walk, n, [blk] * 3 + [gt, st, iv, blk], [blk] * 3 + [gt],
                            [_sds((s, width))] * 3 + [_sds((s, LANES))], (q, k, v, gates, s_in, inv, do), exchange)


def adamw(w, g, m, v, name):
    shape = w.shape
    if len(shape) == 2:
        grid, spec = (1,), pl.BlockSpec(shape, lambda i: (0, 0))
    else:
        tile = shape[1] if shape[1] <= 512 else _pick(shape[1], (512, 256, 128))
        grid, spec = (shape[0], shape[1] // tile), pl.BlockSpec((None, tile, shape[2]), lambda layer, i: (layer, i, 0))

    def body(w_ref, g_ref, m_ref, v_ref, d_ref, nm_ref, nv_ref):
        grad = g_ref[...]
        nm = ADAM_B1 * m_ref[...] + (1.0 - ADAM_B1) * grad
        nv = ADAM_B2 * v_ref[...] + (1.0 - ADAM_B2) * (grad * grad)
        m_hat = nm / (1.0 - ADAM_B1 ** ADAM_STEP)
        v_hat = nv / (1.0 - ADAM_B2 ** ADAM_STEP)
        d_ref[...] = -ADAM_LR * (m_hat / (jnp.sqrt(v_hat) + ADAM_EPS) + ADAM_WD * w_ref[...])
        nm_ref[...] = nm
        nv_ref[...] = nv

    return tuple(pl.pallas_call(body, name=name, grid=grid, in_specs=[spec] * 4, out_specs=[spec] * 3,
                                out_shape=[_sds(shape)] * 3, compiler_params=_params(len(grid)))(w, g, m, v))


def _place():
    return lax.axis_index("x"), lax.axis_index("y"), lax.axis_index("c")


def _flip(p, bits):
    return tuple(1 - v if (bits >> s) & 1 else v for v, s in zip(p, (2, 1, 0)))


def _slot(p):
    return 4 * p[0] + 2 * p[1] + p[2]


def _chip_of(p):
    return 2 * p[0] + p[1]


ANY = pl.BlockSpec(memory_space=pl.ANY)


class Gather:
    scratch = (pltpu.SemaphoreType.DMA((7,)), pltpu.SemaphoreType.DMA((7,)), pltpu.SemaphoreType.DMA)

    def __init__(self, shard):
        self.operand = shard
        self.out_shape = jax.ShapeDtypeStruct((N_DEV,) + shard.shape, shard.dtype)

    def bind(self, x_ref, out_ref, send_sems, recv_sems, local_sem):
        me = _place()
        sibling = _flip(me, 1)
        chips = [_flip(me, 4), _flip(me, 2), _flip(me, 6)]

        def copy(k, block, to, src=None):
            return pltpu.make_async_remote_copy(
                src_ref=out_ref.at[_slot(block)] if src is None else src, dst_ref=out_ref.at[_slot(block)],
                send_sem=send_sems.at[k], recv_sem=recv_sems.at[k], device_id=to, device_id_type=MESH)

        mine = pltpu.make_async_copy(x_ref, out_ref.at[_slot(me)], local_sem)
        first = [copy(0, me, sibling, src=x_ref)] + [copy(1 + j, me, chip, src=x_ref) for j, chip in enumerate(chips)]
        passed = [copy(4 + j, chip, sibling) for j, chip in enumerate(chips)]

        def start():
            mine.start()
            for cp in first:
                cp.start()

        def finish():
            for j, chip in enumerate(chips):
                copy(1 + j, chip, me).wait_recv()
                passed[j].start()
            copy(0, sibling, me).wait_recv()
            for j, chip in enumerate(chips):
                copy(4 + j, _flip(chip, 1), me).wait_recv()
            for cp in first + passed:
                cp.wait_send()
            mine.wait()

        return start, finish


class ChipExchange:
    scratch = (pltpu.SemaphoreType.DMA((3,)), pltpu.SemaphoreType.DMA((3,)), pltpu.SemaphoreType.DMA)

    def __init__(self, blocks):
        self.operand = blocks
        self.out_shape = jax.ShapeDtypeStruct(blocks.shape, blocks.dtype)

    def bind(self, x_ref, out_ref, send_sems, recv_sems, local_sem):
        me = _place()
        peers = [_flip(me, 4), _flip(me, 2), _flip(me, 6)]
        mine = pltpu.make_async_copy(x_ref.at[_chip_of(me)], out_ref.at[_chip_of(me)], local_sem)

        def copy(j, src_chip, dst_chip):
            return pltpu.make_async_remote_copy(
                src_ref=x_ref.at[src_chip], dst_ref=out_ref.at[dst_chip], send_sem=send_sems.at[j],
                recv_sem=recv_sems.at[j], device_id=peers[j], device_id_type=MESH)

        sends = [copy(j, _chip_of(peer), _chip_of(me)) for j, peer in enumerate(peers)]

        def start():
            mine.start()
            for cp in sends:
                cp.start()

        def finish():
            for j, peer in enumerate(peers):
                copy(j, _chip_of(me), _chip_of(peer)).wait_recv()
            for cp in sends:
                cp.wait_send()
            mine.wait()

        return start, finish


class Together:
    def __init__(self, *parts):
        self.parts = parts
        self.operands = [p.operand for p in parts]
        self.out_shapes = [p.out_shape for p in parts]
        self.scratch = [s for p in parts for s in p.scratch]

    def bind(self, x_refs, out_refs, sems):
        bound, at = [], 0
        for p, x_ref, out_ref in zip(self.parts, x_refs, out_refs):
            bound.append(p.bind(x_ref, out_ref, *sems[at:at + len(p.scratch)]))
            at += len(p.scratch)

        def start():
            for s, _ in bound:
                s()

        def finish():
            for _, f in bound:
                f()

        return start, finish


def exchange_alone(exchange, name):
    n = len(exchange.operands)

    def body(*refs):
        start, finish = exchange.bind(refs[:n], refs[n:2 * n], refs[2 * n:])
        start()
        finish()

    return pl.pallas_call(body, name=name, out_shape=exchange.out_shapes, in_specs=[ANY] * n, out_specs=[ANY] * n,
                          scratch_shapes=exchange.scratch)(*exchange.operands)


def _row_tile(rows):
    return max([t for t in range(16, min(rows, 1024) + 1, 16) if rows % t == 0] or [rows])


def pair_exchange(blocks, name):
    n = len(blocks)

    def body(*refs):
        x_refs, theirs_refs, (send_sems, recv_sems) = refs[:n], refs[n:2 * n], refs[2 * n:]
        me = _place()
        remote = [pltpu.make_async_remote_copy(
            src_ref=x_refs[t].at[2 * q + 1 - me[2]], dst_ref=theirs_refs[t].at[q], send_sem=send_sems.at[4 * t + q],
            recv_sem=recv_sems.at[4 * t + q], device_id=_flip(me, 1), device_id_type=MESH) for t in range(n) for q in range(4)]
        for cp in remote:
            cp.start()
        for cp in remote:
            cp.wait()

    return pl.pallas_call(
        body, name=name, out_shape=[jax.ShapeDtypeStruct((4,) + b.shape[1:], b.dtype) for b in blocks], in_specs=[ANY] * n,
        out_specs=[ANY] * n, scratch_shapes=[pltpu.SemaphoreType.DMA((4 * n,)), pltpu.SemaphoreType.DMA((4 * n,))])(*blocks)


def pair_add(blocks, theirs, name):
    n, rows, width = theirs.shape
    tile = _row_tile(rows)
    spec = pl.BlockSpec((None, tile, width), lambda q, i: (q, i, 0))
    south = pl.BlockSpec((None, None, tile, width), lambda q, i: (q, 0, i, 0))
    north = pl.BlockSpec((None, None, tile, width), lambda q, i: (q, 1, i, 0))

    def body(s_ref, n_ref, b_ref, o_ref):
        mine = jnp.where(lax.axis_index("c") == 0, s_ref[...], n_ref[...])
        o_ref[...] = (mine.astype(f32) + b_ref[...].astype(f32)).astype(o_ref.dtype)

    by_core = blocks.reshape(n, 2, rows, width)
    return pl.pallas_call(body, name=name, grid=(n, rows // tile), in_specs=[south, north, spec], out_specs=spec,
                          out_shape=jax.ShapeDtypeStruct(theirs.shape, theirs.dtype), compiler_params=_params(2))(by_core, by_core, theirs)


def sum_slots(blocks, name):
    n, rows, width = blocks.shape
    tile = _row_tile(rows)

    def body(x_ref, o_ref):
        total = x_ref[0].astype(f32)
        for s in range(1, n):
            total = total + x_ref[s].astype(f32)
        o_ref[...] = total

    return pl.pallas_call(
        body, name=name, grid=(rows // tile,), in_specs=[pl.BlockSpec((n, tile, width), lambda i: (0, i, 0))],
        out_specs=pl.BlockSpec((tile, width), lambda i: (i, 0)), out_shape=_sds((rows, width)), compiler_params=_params(1))(blocks)


def all_reduce_small(x, name):
    rows, width = x.shape

    def body(x_ref, o_ref, land, send_sems, recv_sems):
        me = _place()
        copies = []
        for k in range(1, N_DEV):
            peer = _flip(me, k)
            copies.append(pltpu.make_async_remote_copy(
                src_ref=x_ref, dst_ref=land.at[_slot(me)], send_sem=send_sems.at[k - 1], recv_sem=recv_sems.at[k - 1],
                device_id=peer, device_id_type=MESH))
        for cp in copies:
            cp.start()
        land[_slot(me)] = x_ref[...]
        for k in range(1, N_DEV):
            peer = _flip(me, k)
            pltpu.make_async_remote_copy(
                src_ref=x_ref, dst_ref=land.at[_slot(peer)], send_sem=send_sems.at[k - 1], recv_sem=recv_sems.at[k - 1],
                device_id=peer, device_id_type=MESH).wait_recv()
        total = land[0]
        for s in range(1, N_DEV):
            total = total + land[s]
        o_ref[...] = total
        for cp in copies:
            cp.wait_send()

    return pl.pallas_call(
        body, name=name, out_shape=_sds((rows, width)), in_specs=[pl.BlockSpec(memory_space=pltpu.VMEM)],
        out_specs=pl.BlockSpec(memory_space=pltpu.VMEM),
        scratch_shapes=[pltpu.VMEM((N_DEV, rows, width), f32), pltpu.SemaphoreType.DMA((7,)), pltpu.SemaphoreType.DMA((7,))],
    )(x)


def _pack_big(shards):
    packed = {name: shards[name].astype(bf16) for name in COL_SHARDED}
    packed["rows"] = jnp.concatenate([shards[name].astype(bf16) for name, _ in ROW_SHARDED], axis=1)
    return packed


def _unpack_gathered(gathered):
    full = {}
    for name, part in gathered.items():
        if name == "w_gate_up":
            full[name] = part
        elif name in COL_SHARDED:
            full[name] = part.transpose(1, 0, 2).reshape(D_MODEL, N_DEV * part.shape[2])
        else:
            at = 0
            for weight, rows in ROW_SHARDED:
                full[weight] = part[:, at:at + rows, :].reshape(N_DEV * rows, D_MODEL)
                at += rows
    return full


def _pack_grads(grads, group):
    packed = {}
    for name in group:
        if name == "w_gate_up":
            packed[name] = grads[name]
        elif name == "rows":
            packed[name] = jnp.concatenate([grads[weight].reshape(N_DEV, rows, D_MODEL) for weight, rows in ROW_SHARDED], axis=1)
        else:
            packed[name] = grads[name].reshape(D_MODEL, N_DEV, grads[name].shape[1] // N_DEV).transpose(1, 0, 2)
    return packed


def _unpack_shard(layers):
    out = {name: jnp.stack([layer[name] for layer in layers]) for name in COL_SHARDED}
    rows_pack, at = jnp.stack([layer["rows"] for layer in layers]), 0
    for weight, rows in ROW_SHARDED:
        out[weight] = rows_pack[:, at:at + rows, :]
        at += rows
    return out


def _rows_of(flat_len):
    return -(-flat_len // (8 * D_MODEL)) * 8


def _pack_small(parts):
    flat = jnp.concatenate([p.reshape(-1) for p in parts])
    rows = _rows_of(flat.shape[0])
    flat = jnp.pad(flat, (0, rows * D_MODEL - flat.shape[0]))
    return flat.reshape(rows, D_MODEL)


def _unpack_small(packed, like):
    flat, out, at = packed.reshape(-1), [], 0
    for p in like:
        out.append(flat[at:at + p.size].reshape(p.shape))
        at += p.size
    return out


def _rope_tables(positions):
    inv_freq = jnp.float32(ROPE_THETA) ** (-jnp.arange(0, ROPE_DIM, 2, dtype=f32) / ROPE_DIM)
    ang = positions.astype(f32)[:, None] * inv_freq
    cos, sin = jnp.cos(ang), jnp.sin(ang)
    rest = ATTN_HEAD_DIM - ROPE_DIM
    cos_h = jnp.concatenate([cos, cos, jnp.ones((cos.shape[0], rest), f32)], axis=1)
    sin_h = jnp.concatenate([-sin, sin, jnp.zeros((sin.shape[0], rest), f32)], axis=1)
    return jnp.tile(cos_h, (1, ATTN_HEADS)), jnp.tile(sin_h, (1, ATTN_HEADS))


HEAD_SMALL = ("norm_mix_pre", "conv_short", "conv_gdn", "gdn_a_log", "gdn_dt_bias")


def _layer_head(h, p, cos_t, sin_t):
    hn = rms_norm(h, p["norm_mix_pre"][None], "norm_mix_pre")
    aw, cw, gw = ATTN_WIDTH, CONV_WIDTH, GDN_WIDTH
    aq, ak, av, cb, cc, cx, gqkv, ab, gate = _split_cols(_linear(hn, p["w_in"], "w_in"),
                                                         (aw, aw, aw, cw, cw, cw, 3 * gw, 2 * GDN_HEADS, gw))
    ab = jnp.pad(ab, ((0, 0), (0, LANES - 2 * GDN_HEADS)))
    y_attn = dilated_attention(rope(aq, cos_t, sin_t, ATTN_HEAD_DIM ** -0.5, "rope_q"), rope(ak, cos_t, sin_t, 1.0, "rope_k"),
                               av, "attn")
    y_conv = short_conv(cb, cc, cx, p["conv_short"], "short_conv")
    qkv = gdn_pre(gqkv, p["conv_gdn"], "gdn_pre")
    pv = jnp.zeros((8, LANES), f32).at[0, :GDN_HEADS].set(p["gdn_a_log"]).at[1, :GDN_HEADS].set(p["gdn_dt_bias"])
    return (*_split_cols(qkv, (gw, gw, gw)), gate_beta(ab, pv, "gate_beta")), (gate, y_attn, y_conv)


MID_PARAMS = ("gdn_norm", "w_out", "norm_mix_post", "norm_xattn_pre", "w_xq", "norm_mem", "w_xkv", "w_xo", "norm_xattn_post",
              "norm_ffn_pre")


def _layer_mid(h, o, gate, y_attn, y_conv, p, mem):
    y_gdn = gdn_post(o, gate, p["gdn_norm"][None], "gdn_post")
    mix = _linear(jnp.concatenate([y_attn, y_conv, y_gdn], axis=1), p["w_out"], "w_out")
    h, hn = add_norm_then_norm(h, mix, p["norm_mix_post"][None], p["norm_xattn_pre"][None], "norm_mix_xattn")
    qx = _linear(hn, p["w_xq"], "w_xq")
    kv = _linear(rms_norm(mem, p["norm_mem"][None], "norm_mem"), p["w_xkv"], "w_xkv")
    xa = _linear(cross_attention(qx, kv, "xattn"), p["w_xo"], "w_xo")
    return add_norm_then_norm(h, xa, p["norm_xattn_post"][None], p["norm_ffn_pre"][None], "norm_xattn_ffn")


def _pair_summed(grads, group, name):
    blocks = _pack_grads(grads, group)
    theirs = pair_exchange([blocks[n] for n in group], name + "_pair_exchange")
    return [pair_add(blocks[n], t, f"{name}_pair_add_{n}") for n, t in zip(group, theirs)]


def _forward_backward(x, packed, small, mem, cos_t, sin_t, target):
    def gathers(group, layer):
        return [Gather(packed[n][layer]) for n in group]

    h = x
    head_gathered = exchange_alone(Together(*gathers(HEAD_GROUP, 0)), "gather_first")
    saved = []
    for layer in range(DEPTH):
        at_layer = {n: t[layer] for n, t in small.items()}
        head_p = {**_unpack_gathered(dict(zip(HEAD_GROUP, head_gathered))), **{n: at_layer[n] for n in HEAD_SMALL}}
        (rule_in, rest), head_vjp = jax.vjp(lambda h, hp: _layer_head(h, hp, cos_t, sin_t), h, head_p)
        carried = gathers(TAIL_GROUP, layer) + (gathers(HEAD_GROUP, layer + 1) if layer + 1 < DEPTH else [])
        o, s_in, inv, *landed = delta_rule_fwd(*rule_in, "delta_rule_fwd", Together(*carried))
        head_gathered = landed[len(TAIL_GROUP):]
        tail_p = {**_unpack_gathered(dict(zip(TAIL_GROUP, landed))), **at_layer}
        mid_p = {n: tail_p[n] for n in MID_PARAMS}
        (h, hn), mid_vjp = jax.vjp(lambda h, o, rest, mp: _layer_mid(h, o, *rest, mp, mem), h, o, rest, mid_p)
        y, ffn_saved = ffn_forward(hn, tail_p["w_gate_up"], tail_p["w_down"], "ffn")
        h, last_vjp = jax.vjp(lambda h, y, w: add_norm(h, y, w[None], "norm_ffn_post"), h, y, tail_p["norm_ffn_post"])
        saved.append((head_vjp, mid_vjp, last_vjp, ffn_saved, rule_in, s_in, inv))

    loss, dh = jax.value_and_grad(lambda y: loss_rows(y, target, "loss"))(h)

    def summed(group, landed):
        return {n: sum_slots(t, "sum_grads_" + n) for n, t in zip(group, landed)}

    big_grads, small_grads, head_pending = [{} for _ in range(DEPTH)], [None] * DEPTH, []
    for layer in reversed(range(DEPTH)):
        head_vjp, mid_vjp, last_vjp, ffn_saved, rule_in, s_in, inv = saved[layer]
        dh, dy, d_norm_ffn_post = last_vjp(dh)
        dhn, d_gate_up, d_down, landed = ffn_backward(
            ffn_saved, dy, "ffn", Together(*[ChipExchange(t) for t in head_pending]) if head_pending else None)
        if head_pending:
            big_grads[layer + 1].update(summed(HEAD_GROUP, landed))
        dh_mid, do, d_rest, d_mid_p = mid_vjp((dh, dhn))
        d_tail_p = {**d_mid_p, "w_gate_up": d_gate_up, "w_down": d_down, "norm_ffn_post": d_norm_ffn_post}
        carried = Together(*[ChipExchange(t) for t in _pair_summed(d_tail_p, TAIL_GROUP, "tail")])
        *d_rule_in, = delta_rule_bwd(*rule_in, s_in, inv, do, "delta_rule_bwd", carried)
        big_grads[layer].update(summed(TAIL_GROUP, d_rule_in[4:]))
        dh_head, d_head_p = head_vjp((tuple(d_rule_in[:4]), d_rest))
        dh = dh_mid + dh_head
        small_grads[layer] = {n: t for n, t in {**d_head_p, **d_tail_p}.items() if n in small}
        head_pending = _pair_summed(d_head_p, HEAD_GROUP, "head")
    landed = exchange_alone(Together(*[ChipExchange(t) for t in head_pending]), "exchange_last")
    big_grads[0].update(summed(HEAD_GROUP, landed))
    return loss, dh, big_grads, small_grads


def kernel(x, mem, positions, norm_mix_pre, norm_mix_post, w_in, conv_short, conv_gdn, gdn_a_log, gdn_dt_bias, gdn_norm, w_out, norm_mem, norm_xattn_pre, norm_xattn_post, w_xq, w_xkv, w_xo, norm_ffn_pre, norm_ffn_post, w_gate_up, w_down, loss_target, m_norm_mix_pre, m_norm_mix_post, m_w_in, m_conv_short, m_conv_gdn, m_gdn_a_log, m_gdn_dt_bias, m_gdn_norm, m_w_out, m_norm_mem, m_norm_xattn_pre, m_norm_xattn_post, m_w_xq, m_w_xkv, m_w_xo, m_norm_ffn_pre, m_norm_ffn_post, m_w_gate_up, m_w_down, v_norm_mix_pre, v_norm_mix_post, v_w_in, v_conv_short, v_conv_gdn, v_gdn_a_log, v_gdn_dt_bias, v_gdn_norm, v_w_out, v_norm_mem, v_norm_xattn_pre, v_norm_xattn_post, v_w_xq, v_w_xkv, v_w_xo, v_norm_ffn_pre, v_norm_ffn_post, v_w_gate_up, v_w_down):
    given = dict(locals())
    weights = {n: given[n] for n in WEIGHTS}
    me = _slot(_place())

    def in_place(shard):
        full = jnp.zeros(shard.shape[:-1] + (shard.shape[-1] * N_DEV,), f32)
        return lax.dynamic_update_slice_in_dim(full, shard, me * shard.shape[-1], axis=shard.ndim - 1)

    placed = [in_place(conv_short), in_place(conv_gdn)]
    conv_short_full, conv_gdn_full = _unpack_small(all_reduce_small(_pack_small(placed), "gather_conv"), placed)
    small = {n: weights[n] for n in NORMS + ("gdn_a_log", "gdn_dt_bias", "gdn_norm")}
    small["conv_short"], small["conv_gdn"] = conv_short_full, conv_gdn_full

    cos_t, sin_t = _rope_tables(positions[0])
    loss, grad_x, big_layers, small_layers = _forward_backward(
        x[0], _pack_big(weights), small, mem[0], cos_t, sin_t, loss_target[0])
    grads = _unpack_shard(big_layers)

    names = sorted(small)
    parts = [jnp.stack([layer[n] for layer in small_layers]) for n in names] + [loss.reshape(1)]
    reduced = _unpack_small(all_reduce_small(_pack_small(parts), "reduce_small"), parts)
    loss = reduced[-1][0]
    for n, g in zip(names, reduced[:-1]):
        if n in ("conv_short", "conv_gdn"):
            width = weights[n].shape[-1]
            g = lax.dynamic_slice_in_dim(g, me * width, width, axis=g.ndim - 1)
        grads[n] = g

    delta, new_m, new_v = {}, {}, {}
    for n in WEIGHTS:
        delta[n], new_m[n], new_v[n] = adamw(weights[n], grads[n], given["m_" + n], given["v_" + n], "adamw_" + n)
    return (loss, grad_x[None], *[grads[n] for n in WEIGHTS], *[delta[n] for n in WEIGHTS],
            *[new_m[n] for n in WEIGHTS], *[new_v[n] for n in WEIGHTS])
```

```python
import functools

import jax
import jax.numpy as jnp
from jax import lax
from jax.experimental import pallas as pl
from jax.experimental.pallas import tpu as pltpu

f32 = jnp.float32
bf16 = jnp.bfloat16
MESH = pl.DeviceIdType.MESH

N_DEV = 8
DEPTH = 4
D_MODEL = 1024
EPS = 1e-6
ATTN_HEADS, ATTN_HEAD_DIM = 4, 64
ATTN_WIDTH = ATTN_HEADS * ATTN_HEAD_DIM
DILATIONS = (1, 4, 16)
QB = 128
ROPE_THETA = 500000.0
ROPE_DIM = ATTN_HEAD_DIM // 4
CONV_WIDTH = 256
GDN_HEADS, GDN_HEAD_DIM = 4, 128
GDN_WIDTH = GDN_HEADS * GDN_HEAD_DIM
GDN_CHUNK = 64
XATTN_HEADS, XATTN_HEAD_DIM = 4, 256
LANES = 128
ROW_TILE = 512
VMEM_LIMIT = 56 * 1024 * 1024

ADAM_LR, ADAM_B1, ADAM_B2, ADAM_EPS, ADAM_WD, ADAM_STEP = 0.001, 0.9, 0.999, 1e-08, 0.01, 10

COL_SHARDED = ("w_in", "w_xkv", "w_gate_up")
ROW_SHARDED = (("w_out", 128), ("w_xq", 128), ("w_xo", 128), ("w_down", 352))
HEAD_GROUP = ("w_in",)
TAIL_GROUP = ("w_gate_up", "w_xkv", "rows")
NORMS = ("norm_mix_pre", "norm_mix_post", "norm_mem", "norm_xattn_pre", "norm_xattn_post", "norm_ffn_pre", "norm_ffn_post")
WEIGHTS = ("norm_mix_pre", "norm_mix_post", "w_in", "conv_short", "conv_gdn", "gdn_a_log", "gdn_dt_bias", "gdn_norm", "w_out",
           "norm_mem", "norm_xattn_pre", "norm_xattn_post", "w_xq", "w_xkv", "w_xo", "norm_ffn_pre", "norm_ffn_post",
           "w_gate_up", "w_down")


def _params(n_grid):
    return pltpu.CompilerParams(dimension_semantics=("arbitrary",) * n_grid, vmem_limit_bytes=VMEM_LIMIT)


def _pick(n, cands):
    for c in cands:
        if n % c == 0:
            return c
    return n


MXU_FLOPS = 9.0e14
HBM_BYTES_PER_S = 2.5e12
VMEM_RMW_BYTES_PER_S = 7.0e12
STEP_S = 0.4e-6
MATMUL_VMEM = 44 * 1024 * 1024


def _tiles(m, n, k, sa, sb, so):
    def divisors(d):
        return sorted({d} | {d // s for s in range(1, d // LANES + 1) if d % s == 0 and (d // s) % LANES == 0}, reverse=True)

    best = None
    for tk in divisors(k):
        nk = k // tk
        for tm in divisors(m):
            for tn_ in divisors(n):
                per_step = tm * tk * sa + tk * tn_ * sb + tm * tn_ * so
                vmem = 2 * per_step + (tm * tn_ * 4 if nk > 1 else 0)
                vmem += (tm * tk * 2 if sa == 4 else 0) + (tk * tn_ * 2 if sb == 4 else 0) + tm * tn_ * 4
                if vmem > MATMUL_VMEM:
                    continue
                moved = m * k * sa * (1 if nk == 1 else n // tn_) + k * n * sb * (1 if nk == 1 and n == tn_ else m // tm) + m * n * so
                busy = 2 * m * n * k / MXU_FLOPS + (m * n * 8 * nk / VMEM_RMW_BYTES_PER_S if nk > 1 else 0)
                cost = max(moved / HBM_BYTES_PER_S, busy) + per_step / HBM_BYTES_PER_S + (m // tm) * (n // tn_) * nk * STEP_S
                if best is None or cost < best[0]:
                    best = (cost, tm, tn_, tk)
    return best[1:]


def _mm(a, b, ta, tb, out_dtype, name):
    m, k = (a.shape[1], a.shape[0]) if ta else a.shape
    n = b.shape[0] if tb else b.shape[1]
    tm, tn, tk = _tiles(m, n, k, a.dtype.itemsize, b.dtype.itemsize, jnp.dtype(out_dtype).itemsize)
    nk = k // tk
    a_spec = pl.BlockSpec((tk, tm), lambda i, j, kk: (kk, i)) if ta else pl.BlockSpec((tm, tk), lambda i, j, kk: (i, kk))
    b_spec = pl.BlockSpec((tn, tk), lambda i, j, kk: (j, kk)) if tb else pl.BlockSpec((tk, tn), lambda i, j, kk: (kk, j))
    dims = (((0 if ta else 1,), (1 if tb else 0,)), ((), ()))

    def body(a_ref, b_ref, o_ref, *acc):
        kk = pl.program_id(2)
        p = lax.dot_general(a_ref[...].astype(bf16), b_ref[...].astype(bf16), dims, preferred_element_type=f32)
        if nk == 1:
            o_ref[...] = p.astype(o_ref.dtype)
            return
        acc_ref, = acc

        @pl.when(kk == 0)
        def _():
            acc_ref[...] = p

        @pl.when(kk > 0)
        def _():
            acc_ref[...] += p

        @pl.when(kk == nk - 1)
        def _():
            o_ref[...] = acc_ref[...].astype(o_ref.dtype)

    return pl.pallas_call(
        body, name=name, grid=(m // tm, n // tn, nk), in_specs=[a_spec, b_spec],
        out_specs=pl.BlockSpec((tm, tn), lambda i, j, kk: (i, j)), out_shape=jax.ShapeDtypeStruct((m, n), out_dtype),
        scratch_shapes=[pltpu.VMEM((tm, tn), f32)] if nk > 1 else [], compiler_params=_params(3))(a, b)


def _linear(x, w, name):
    @jax.custom_vjp
    def lin(x, w):
        return _mm(x, w, False, False, f32, name + "_y")

    def lin_f(x, w):
        return _mm(x, w, False, False, f32, name + "_y"), (x, w)

    def lin_b(res, dy):
        x, w = res
        return _mm(dy, w, False, True, f32, name + "_dx"), _mm(x, dy, True, False, bf16, name + "_dw")

    lin.defvjp(lin_f, lin_b)
    return lin(x, w)


def _bdot(a, b, form):
    dims = {"nn": ((1,), (0,)), "nt": ((1,), (1,)), "tn": ((0,), (0,))}[form]
    return lax.dot_general(a.astype(bf16), b.astype(bf16), (dims, ((), ())), preferred_element_type=f32)


def ffn_forward(hn, w_gate_up, w_down, name):
    s, k = hn.shape
    n_blocks, _, width = w_gate_up.shape
    half = n_blocks // 2
    tm = 512
    blocked = jax.ShapeDtypeStruct((half, s, width), bf16)

    def act_body(x_ref, wg_ref, wu_ref, gate_ref, up_ref, act_ref):
        x = x_ref[...]
        gate, up = _bdot(x, wg_ref[...], "nn"), _bdot(x, wu_ref[...], "nn")
        gate_ref[...], up_ref[...] = gate.astype(bf16), up.astype(bf16)
        act_ref[...] = (jax.nn.silu(gate) * up).astype(bf16)

    tile = pl.BlockSpec((None, tm, width), lambda i, d: (d, i, 0))
    gate, up, act = pl.pallas_call(
        act_body, name=name + "_act", grid=(s // tm, half),
        in_specs=[pl.BlockSpec((tm, k), lambda i, d: (i, 0)), pl.BlockSpec((None, k, width), lambda i, d: (d, 0, 0)),
                  pl.BlockSpec((None, k, width), lambda i, d: (d + half, 0, 0))],
        out_specs=[tile] * 3, out_shape=[blocked] * 3, compiler_params=_params(2))(hn, w_gate_up, w_gate_up)

    n = w_down.shape[1]
    tn = 512

    def y_body(act_ref, w_ref, y_ref):
        y_ref[...] = sum(_bdot(act_ref[d], w_ref[d * width:(d + 1) * width, :], "nn") for d in range(half))

    y = pl.pallas_call(
        y_body, name=name + "_y", grid=(s // tm, n // tn),
        in_specs=[pl.BlockSpec((half, tm, width), lambda i, j: (0, i, 0)), pl.BlockSpec((half * width, tn), lambda i, j: (0, j))],
        out_specs=pl.BlockSpec((tm, tn), lambda i, j: (i, j)), out_shape=_sds((s, n)), compiler_params=_params(2))(act, w_down)
    return y, (hn, w_gate_up, w_down, gate, up, act)


def ffn_backward(saved, dy, name, exchange=None):
    hn, w_gate_up, w_down, gate, up, act = saved
    s, k = hn.shape
    n_blocks, _, width = w_gate_up.shape
    half = n_blocks // 2
    n = w_down.shape[1]
    tm = 512
    blocked = jax.ShapeDtypeStruct((half, s, width), bf16)
    carried = len(exchange.operands) if exchange else 0
    steps = (s // tm, half)

    def dact_body(dy_ref, w_ref, gate_ref, up_ref, *refs):
        x_refs, refs = refs[:carried], refs[carried:]
        (dgate_ref, dup_ref), refs = refs[:2], refs[2:]
        if exchange:
            at = pl.program_id(0) * steps[1] + pl.program_id(1)
            start, wait = exchange.bind(x_refs, refs[:carried], refs[carried:])
            pl.when(at == 0)(start)
        d_act = _bdot(dy_ref[...], w_ref[...], "nt")
        g, u = gate_ref[...].astype(f32), up_ref[...].astype(f32)
        sig = jax.nn.sigmoid(g)
        dgate_ref[...] = (d_act * u * sig * (1.0 + g * (1.0 - sig))).astype(bf16)
        dup_ref[...] = (d_act * g * sig).astype(bf16)
        if exchange:
            pl.when(at == steps[0] * steps[1] - 1)(wait)

    tile = pl.BlockSpec((None, tm, width), lambda i, d: (d, i, 0))
    d_gate, d_up, *landed = pl.pallas_call(
        dact_body, name=name + "_dact", grid=steps,
        in_specs=[pl.BlockSpec((tm, n), lambda i, d: (i, 0)), pl.BlockSpec((width, n), lambda i, d: (d, 0)), tile, tile] + [ANY] * carried,
        out_specs=[tile, tile] + [ANY] * carried, out_shape=[blocked, blocked] + (exchange.out_shapes if exchange else []),
        scratch_shapes=exchange.scratch if exchange else [],
        compiler_params=_params(2))(dy, w_down, gate, up, *(exchange.operands if exchange else []))

    def dx_body(dg_ref, du_ref, w_ref, dx_ref):
        dx_ref[...] = sum(_bdot(dg_ref[d], w_ref[d], "nt") + _bdot(du_ref[d], w_ref[d + half], "nt") for d in range(half))

    tx = 256
    rows = pl.BlockSpec((half, tx, width), lambda i: (0, i, 0))
    dx = pl.pallas_call(
        dx_body, name=name + "_dx", grid=(s // tx,), in_specs=[rows, rows, _whole(w_gate_up.shape)],
        out_specs=pl.BlockSpec((tx, k), lambda i: (i, 0)), out_shape=_sds((s, k)), compiler_params=_params(1))(d_gate, d_up, w_gate_up)

    def dw1_body(x_ref, dg_ref, du_ref, dw_ref):
        d_block = jnp.where(pl.program_id(0) < half, dg_ref[...], du_ref[...])
        dw_ref[...] = _bdot(x_ref[...], d_block, "tn").astype(bf16)

    d_w_gate_up = pl.pallas_call(
        dw1_body, name=name + "_dw1", grid=(n_blocks,),
        in_specs=[_whole((s, k)), pl.BlockSpec((None, s, width), lambda b: (jnp.minimum(b, half - 1), 0, 0)),
                  pl.BlockSpec((None, s, width), lambda b: (jnp.maximum(b - half, 0), 0, 0))],
        out_specs=pl.BlockSpec((None, k, width), lambda b: (b, 0, 0)), out_shape=jax.ShapeDtypeStruct(w_gate_up.shape, bf16),
        compiler_params=_params(1))(hn, d_gate, d_up)

    tn = 512

    def dw2_body(act_ref, dy_ref, dw_ref):
        dw_ref[...] = _bdot(act_ref[...], dy_ref[...], "tn").astype(bf16)

    d_w_down = pl.pallas_call(
        dw2_body, name=name + "_dw2", grid=(half, n // tn),
        in_specs=[pl.BlockSpec((None, s, width), lambda d, j: (d, 0, 0)), pl.BlockSpec((s, tn), lambda d, j: (0, j))],
        out_specs=pl.BlockSpec((width, tn), lambda d, j: (d, j)), out_shape=jax.ShapeDtypeStruct(w_down.shape, bf16),
        compiler_params=_params(2))(act, dy)
    return dx, d_w_gate_up, d_w_down, landed


def _linear_split(x, w, widths, name):
    edges = [sum(widths[:i]) for i in range(len(widths) + 1)]

    def forward(x, w):
        y = _mm(x, w, False, False, f32, name + "_y")
        return tuple(y[:, a:b] for a, b in zip(edges[:-1], edges[1:])), (x, w)

    def backward(res, cts):
        x, w = res
        dy = jnp.concatenate([ct.astype(bf16) for ct in cts], axis=1)
        return _mm(dy, w, False, True, f32, name + "_dx"), _mm(x, dy, True, False, bf16, name + "_dw")

    @jax.custom_vjp
    def op(x, w):
        return forward(x, w)[0]

    op.defvjp(forward, backward)
    return op(x, w)


def _split_cols(x, widths):
    edges = [sum(widths[:i]) for i in range(len(widths) + 1)]

    def cut(x):
        return tuple(x[:, a:b] for a, b in zip(edges[:-1], edges[1:]))

    @jax.custom_vjp
    def split(x):
        return cut(x)

    split.defvjp(lambda x: (cut(x), None), lambda _, cts: (jnp.concatenate(cts, axis=1),))
    return split(x)


def _block_op(name, f, grid, in_specs, out_defs, arrays, diff, acc=None, gdefs=None):
    acc, gdefs = acc or {}, gdefs or {}
    n_in, n_out, n_grid = len(in_specs), len(out_defs), len(grid)

    def fwd_call(*xs):
        def body(*refs):
            outs = f(*[r[...] for r in refs[:n_in]])
            for r, o in zip(refs[n_in:], outs):
                r[...] = o.astype(r.dtype)

        return pl.pallas_call(
            body, name=name + "_fwd", grid=grid, in_specs=in_specs, out_specs=[d[1] for d in out_defs],
            out_shape=[d[0] for d in out_defs], compiler_params=_params(n_grid))(*xs)

    def bwd_call(*xs_and_cts):
        def body(*refs):
            xs = [r[...] for r in refs[:n_in]]
            cts = tuple(r[...] for r in refs[n_in:n_in + n_out])

            def of_diff(*dx):
                full = list(xs)
                for i, v in zip(diff, dx):
                    full[i] = v
                return tuple(f(*full))

            _, vjp = jax.vjp(of_diff, *[xs[i] for i in diff])
            grads = vjp(cts)
            for i, g, r in zip(diff, grads, refs[n_in + n_out:]):
                if i in acc:
                    first = functools.reduce(jnp.logical_and, [pl.program_id(a) == 0 for a in acc[i]])

                    @pl.when(first)
                    def _(r=r):
                        r[...] = jnp.zeros_like(r)

                    r[...] += g.astype(r.dtype)
                else:
                    r[...] = g.astype(r.dtype)

        g_defs = [gdefs.get(i, (jax.ShapeDtypeStruct(arrays[i].shape, f32), in_specs[i])) for i in diff]
        return pl.pallas_call(
            body, name=name + "_bwd", grid=grid, in_specs=list(in_specs) + [d[1] for d in out_defs],
            out_specs=[d[1] for d in g_defs], out_shape=[d[0] for d in g_defs], compiler_params=_params(n_grid))(*xs_and_cts)

    return fwd_call, bwd_call


def _simple_op(name, f, grid, in_specs, out_defs, arrays, diff, acc=None):
    fwd_call, bwd_call = _block_op(name, f, grid, in_specs, out_defs, arrays, diff, acc)

    @jax.custom_vjp
    def op(*xs):
        return tuple(fwd_call(*xs))

    def op_f(*xs):
        return tuple(fwd_call(*xs)), xs

    def op_b(xs, cts):
        grads = bwd_call(*xs, *cts)
        out = [jnp.zeros_like(x) for x in xs]
        for i, g in zip(diff, grads):
            out[i] = g
        return tuple(out)

    op.defvjp(op_f, op_b)
    return op(*arrays)


def _rows(width, tile=ROW_TILE):
    return pl.BlockSpec((tile, width), lambda i: (i, 0))


def _whole(shape):
    return pl.BlockSpec(shape, lambda *_: (0,) * len(shape))


def _sds(shape):
    return jax.ShapeDtypeStruct(shape, f32)


def _rms(x, w):
    return x * lax.rsqrt(jnp.mean(x * x, axis=-1, keepdims=True) + EPS) * w


def rms_norm(x, w, name):
    r, d = x.shape
    tile = min(ROW_TILE, r)
    return _simple_op(name, lambda x, w: (_rms(x, w),), (r // tile,), [_rows(d, tile), _whole((1, d))],
                      [(_sds((r, d)), _rows(d, tile))], (x, w), (0, 1), {1: (0,)})[0]


def add_norm(h, y, w, name):
    r, d = h.shape
    return _simple_op(name, lambda h, y, w: (h + _rms(y, w),), (r // ROW_TILE,), [_rows(d), _rows(d), _whole((1, d))],
                      [(_sds((r, d)), _rows(d))], (h, y, w), (0, 1, 2), {2: (0,)})[0]


def add_norm_then_norm(h, y, w_post, w_pre, name):
    r, d = h.shape

    def f(h, y, w_post, w_pre):
        h_new = h + _rms(y, w_post)
        return h_new, _rms(h_new, w_pre)

    return _simple_op(name, f, (r // ROW_TILE,), [_rows(d), _rows(d), _whole((1, d)), _whole((1, d))],
                      [(_sds((r, d)), _rows(d))] * 2, (h, y, w_post, w_pre), (0, 1, 2, 3), {2: (0,), 3: (0,)})


def _swap8(x):
    def raw(x):
        lane = lax.broadcasted_iota(jnp.int32, x.shape, 1) % ATTN_HEAD_DIM
        half = ROPE_DIM // 2
        up = pltpu.roll(x, x.shape[1] - half, axis=1)
        down = pltpu.roll(x, half, axis=1)
        return jnp.where(lane < half, up, jnp.where(lane < ROPE_DIM, down, 0.0))

    @jax.custom_vjp
    def swap(x):
        return raw(x)

    swap.defvjp(lambda x: (raw(x), None), lambda _, g: (raw(g),))
    return swap(x)


def rope(x, cos_t, sin_t, scale, name):
    r, d = x.shape
    return _simple_op(name, lambda x, c, s: ((x * c + _swap8(x) * s) * scale,), (r // ROW_TILE,), [_rows(d)] * 3,
                      [(_sds((r, d)), _rows(d))], (x, cos_t, sin_t), (0,))[0]


def _shift_rows(x, k):
    n = x.shape[0]

    def down(x):
        row = lax.broadcasted_iota(jnp.int32, x.shape, 0)
        return jnp.where(row >= k, pltpu.roll(x, k, axis=0), 0.0)

    def up(x):
        row = lax.broadcasted_iota(jnp.int32, x.shape, 0)
        return jnp.where(row < n - k, pltpu.roll(x, n - k, axis=0), 0.0)

    @jax.custom_vjp
    def shift(x):
        return down(x)

    shift.defvjp(lambda x: (down(x), None), lambda _, g: (up(g),))
    return shift(x)


def _causal_conv(x, w):
    taps = w.shape[0]
    y = x * w[taps - 1:taps, :]
    for j in range(taps - 1):
        y = y + _shift_rows(x, taps - 1 - j) * w[j:j + 1, :]
    return y


def _cols(rows, at=0):
    return pl.BlockSpec((rows, LANES), lambda j: (0, at + j))


def short_conv(cb, cc, cx, w, name):
    s, c = cb.shape
    taps = w.shape[0]
    return _simple_op(name, lambda b, c_, x, w: (b * _causal_conv(c_ * x, w),), (c // LANES,),
                      [_cols(s)] * 3 + [_cols(taps)], [(_sds((s, c)), _cols(s))], (cb, cc, cx, w), (0, 1, 2, 3))[0]


def gdn_pre(qkv, w, name):
    s, c = qkv.shape
    taps = w.shape[0]

    def f(x, w):
        j = pl.program_id(0)
        y = jax.nn.silu(_causal_conv(x, w))
        normed = y * lax.rsqrt(jnp.sum(y * y, axis=-1, keepdims=True) + EPS)
        scale = jnp.where(j < GDN_HEADS, GDN_HEAD_DIM ** -0.5, 1.0).astype(f32)
        return (jnp.where(j < 2 * GDN_HEADS, normed * scale, y),)

    return _simple_op(name, f, (c // LANES,), [_cols(s), _cols(taps)], [(_sds((s, c)), _cols(s))], (qkv, w), (0, 1))[0]


def gate_beta(ab, pv, name):
    s = ab.shape[0]

    def f(ab, pv):
        lane = lax.broadcasted_iota(jnp.int32, ab.shape, 1)
        g = -jnp.exp(pv[0:1, :]) * jax.nn.softplus(ab + pv[1:2, :])
        return (jnp.where(lane < GDN_HEADS, g, jnp.where(lane < 2 * GDN_HEADS, jax.nn.sigmoid(ab), 0.0)),)

    return _simple_op(name, f, (s // ROW_TILE,), [_rows(LANES), _whole((8, LANES))], [(_sds((s, LANES)), _rows(LANES))],
                      (ab, pv), (0, 1), {1: (0,)})[0]


def gdn_post(o, gate, w, name):
    s, c = o.shape

    def f(o, g, w):
        heads = [slice(hd * LANES, (hd + 1) * LANES) for hd in range(c // LANES)]
        return (jnp.concatenate([_rms(o[:, hd], w) * jax.nn.silu(g[:, hd]) for hd in heads], axis=1),)

    return _simple_op(name, f, (s // ROW_TILE,), [_rows(c), _rows(c), _whole((1, LANES))], [(_sds((s, c)), _rows(c))],
                      (o, gate, w), (0, 1, 2), {2: (0,)})[0]


def attn_merge(outs, lses, name):
    s, c = outs[0].shape

    def f(o1, o2, o3, l1, l2, l3):
        m = lax.stop_gradient(jnp.maximum(jnp.maximum(l1, l2), l3))
        e1, e2, e3 = jnp.exp(l1 - m), jnp.exp(l2 - m), jnp.exp(l3 - m)
        return ((e1 * o1 + e2 * o2 + e3 * o3) / (e1 + e2 + e3),)

    return _simple_op(name, f, (s // ROW_TILE,), [_rows(c)] * 6, [(_sds((s, c)), _rows(c))], (*outs, *lses), tuple(range(6)))[0]


def loss_rows(y, target, name):
    s, d = y.shape
    nt = s // ROW_TILE

    def f(y, t):
        e = y - t
        part = 0.5 * jnp.sum(jnp.mean(e * e, axis=-1, keepdims=True), axis=0, keepdims=True)
        return (jnp.broadcast_to(part * (1.0 / (8 * LANES)), (8, LANES)),)

    out = _simple_op(name, f, (nt,), [_rows(d)] * 2, [(_sds((nt * 8, LANES)), pl.BlockSpec((8, LANES), lambda i: (i, 0)))],
                     (y, target), (0,))[0]
    return jnp.sum(out)


def _mxu(a, b, form):
    dims = {"nn": ((1,), (0,)), "nt": ((1,), (1,)), "tn": ((0,), (0,))}

    def raw(a, b, form):
        return lax.dot_general(a.astype(bf16), b.astype(bf16), (dims[form], ((), ())), preferred_element_type=f32)

    @jax.custom_vjp
    def prod(a, b):
        return raw(a, b, form)

    def prod_b(res, ct):
        a, b = res
        if form == "nn":
            return raw(ct, b, "nt"), raw(a, ct, "tn")
        if form == "nt":
            return raw(ct, b, "nn"), raw(ct, a, "tn")
        return raw(b, ct, "nt"), raw(a, ct, "nn")

    prod.defvjp(lambda a, b: (raw(a, b, form), (a, b)), prod_b)
    return prod(a, b)


def _masked_heads_attention(q, keys, values, seen):
    dh = ATTN_HEAD_DIM
    outs, lses = [], []
    for hd in range(q.shape[1] // dh):
        at = slice(hd * dh, (hd + 1) * dh)
        sc = jnp.where(seen, _mxu(q[:, at], keys[:, at], "nt"), -jnp.inf)
        m = lax.stop_gradient(jnp.max(sc, axis=-1, keepdims=True))
        p = jnp.exp(sc - m)
        l = jnp.sum(p, axis=-1, keepdims=True)
        outs.append(_mxu(p / l, values[:, at], "nn"))
        lses.append(jnp.broadcast_to(m + jnp.log(l), (q.shape[0], dh)))
    return jnp.concatenate(outs, axis=1), jnp.concatenate(lses, axis=1)


def band_attention(q, k, v, nb, name):
    r, qb, width = q.shape

    def f(q, kp, kc, vp, vc):
        has_prev = (pl.program_id(0) % nb) > 0
        i = lax.broadcasted_iota(jnp.int32, (qb, 2 * qb), 0)
        j = lax.broadcasted_iota(jnp.int32, (qb, 2 * qb), 1)
        seen = jnp.logical_or(jnp.logical_and(jnp.logical_and(j < qb, j >= i), has_prev), jnp.logical_and(j >= qb, j - qb <= i))
        return _masked_heads_attention(q, jnp.concatenate([kp, kc], axis=0), jnp.concatenate([vp, vc], axis=0), seen)

    blk = (None, qb, width)
    cur = pl.BlockSpec(blk, lambda b: (b, 0, 0))
    prev = pl.BlockSpec(blk, lambda b: (jnp.maximum(b - 1, 0), 0, 0))
    shape = _sds((r, qb, width))
    fwd_call, bwd_call = _block_op(name, f, (r,), [cur, prev, cur, prev, cur], [(shape, cur), (shape, cur)],
                                   (q, k, k, v, v), (0, 1, 2, 3, 4), gdefs={1: (shape, cur), 3: (shape, cur)})

    def to_prev(g):
        return jnp.concatenate([g[1:], jnp.zeros_like(g[:1])], axis=0)

    @jax.custom_vjp
    def op(q, k, v):
        return tuple(fwd_call(q, k, k, v, v))

    def op_b(res, cts):
        q, k, v = res
        dq, dkp, dkc, dvp, dvc = bwd_call(q, k, k, v, v, *cts)
        return dq, dkc + to_prev(dkp), dvc + to_prev(dvp)

    op.defvjp(lambda q, k, v: (tuple(fwd_call(q, k, k, v, v)), (q, k, v)), op_b)
    return op(q, k, v)


def dilated_attention(q, k, v, name):
    s = q.shape[0]
    outs, lses = [], []
    for d in DILATIONS:
        length = s // d
        nb = length // QB
        def to_residue(t):
            return t.reshape(length, d, ATTN_WIDTH).transpose(1, 0, 2).reshape(d * nb, QB, ATTN_WIDTH)

        def from_residue(t):
            return t.reshape(d, length, ATTN_WIDTH).transpose(1, 0, 2).reshape(s, ATTN_WIDTH)

        o, lse = band_attention(to_residue(q), to_residue(k), to_residue(v), nb, f"{name}_d{d}")
        outs.append(from_residue(o))
        lses.append(from_residue(lse))
    return attn_merge(outs, lses, name + "_merge")


def cross_attention(q, kv, name):
    s = q.shape[0]
    m = kv.shape[0]
    width = XATTN_HEADS * XATTN_HEAD_DIM
    tq = 512

    def f(q, k, v):
        sc = _mxu(q, k, "nt") * (XATTN_HEAD_DIM ** -0.5)
        mx = lax.stop_gradient(jnp.max(sc, axis=-1, keepdims=True))
        p = jnp.exp(sc - mx)
        return (_mxu(p / jnp.sum(p, axis=-1, keepdims=True), v, "nn"),)

    q_spec = pl.BlockSpec((tq, XATTN_HEAD_DIM), lambda a, i: (i, a))
    k_spec = pl.BlockSpec((m, XATTN_HEAD_DIM), lambda a, i: (0, a))
    v_spec = pl.BlockSpec((m, XATTN_HEAD_DIM), lambda a, i: (0, a + XATTN_HEADS))
    half = _sds((m, width))
    fwd_call, bwd_call = _block_op(name, f, (XATTN_HEADS, s // tq), [q_spec, k_spec, v_spec], [(_sds((s, width)), q_spec)],
                                   (q, kv, kv), (0, 1, 2), acc={1: (1,), 2: (1,)}, gdefs={1: (half, k_spec), 2: (half, k_spec)})

    @jax.custom_vjp
    def op(q, kv):
        return fwd_call(q, kv, kv)[0]

    def op_b(res, ct):
        q, kv = res
        dq, dk, dv = bwd_call(q, kv, kv, ct)
        return dq, jnp.concatenate([dk, dv], axis=1)

    op.defvjp(lambda q, kv: (fwd_call(q, kv, kv)[0], (q, kv)), op_b)
    return op(q, kv)


def _hi(a, b, form="nn"):
    dims = {"nn": ((1,), (0,)), "nt": ((1,), (1,)), "tn": ((0,), (0,))}[form]
    return lax.dot_general(a, b, (dims, ((), ())), precision=lax.Precision.HIGH, preferred_element_type=f32)


def _running_sum(g):
    def raw(x, form):
        c = x.shape[0]
        tri = (lax.broadcasted_iota(jnp.int32, (c, c), 0) >= lax.broadcasted_iota(jnp.int32, (c, c), 1)).astype(bf16)
        hi = x.astype(bf16)
        rest = x - hi.astype(f32)
        mid = rest.astype(bf16)
        low = (rest - mid.astype(f32)).astype(bf16)
        dims = (((1,) if form == "nn" else (0,), (0,)), ((), ()))
        return sum(lax.dot_general(tri, part, dims, preferred_element_type=f32) for part in (hi, mid, low))

    @jax.custom_vjp
    def run(x):
        return raw(x, "nn")

    run.defvjp(lambda x: (raw(x, "nn"), None), lambda _, ct: (raw(ct, "tn"),))
    return run(g)


def _unit_lower_inverse(a):
    c = a.shape[0]
    eye = (lax.broadcasted_iota(jnp.int32, (c, c), 0) == lax.broadcasted_iota(jnp.int32, (c, c), 1)).astype(f32)
    inv, power = eye - a, -a
    for _ in range(c.bit_length() - 2):
        power = _hi(power, power)
        inv = inv + _hi(inv, power)
    return inv


def _known_inverse(a, t):
    @jax.custom_vjp
    def inv(a, t):
        return t

    def inv_b(t, ct):
        return -_hi(_hi(t, ct, "tn"), t, "nt"), jnp.zeros_like(t)

    inv.defvjp(lambda a, t: (t, t), inv_b)
    return inv(a, t)


def _delta_chunk(q, k, v, g, beta, s0, known_inv=None):
    c = q.shape[0]
    i = lax.broadcasted_iota(jnp.int32, (c, c), 0)
    j = lax.broadcasted_iota(jnp.int32, (c, c), 1)
    causal, strict = i >= j, i > j
    dec = _running_sum(g)
    dec_i = dec[:, :c]
    rel = jnp.exp(jnp.where(causal, dec_i - dec_i.T, -jnp.inf))
    k_beta = k * beta
    on_k = _mxu(jnp.concatenate([k_beta, q], axis=0), k, "nt")
    a = jnp.where(strict, on_k[:c] * rel, 0.0)
    attn = jnp.where(causal, on_k[c:] * rel, 0.0)
    inv = _unit_lower_inverse(a) if known_inv is None else _known_inverse(a, known_inv)
    e_dec = jnp.exp(dec)
    solved = _hi(inv, jnp.concatenate([v * beta, k_beta * e_dec], axis=1))
    u, w = solved[:, :v.shape[1]], solved[:, v.shape[1]:]
    total = jnp.sum(g, axis=0, keepdims=True)
    on_state = _mxu(jnp.concatenate([w, q * e_dec], axis=0), s0, "nn")
    v_new = u - on_state[:c]
    o = on_state[c:] + _mxu(attn, v_new, "nn")
    s1 = s0 * jnp.exp(total) + _mxu(k * jnp.exp(total - dec), v_new, "tn")
    return o, s1, inv


def _delta_rule_call(name, walk, n, in_specs, out_specs, out_shape, operands, exchange):
    n_in, n_out = len(in_specs), len(out_specs)
    carried = len(exchange.operands) if exchange else 0

    def body(*refs):
        ins, refs = refs[:n_in], refs[n_in:]
        x_refs, refs = refs[:carried], refs[carried:]
        outs, refs = refs[:n_out], refs[n_out:]
        land_refs, (state, *sems) = refs[:carried], refs[carried:]
        step = pl.program_id(0)
        if exchange:
            start, finish = exchange.bind(x_refs, land_refs, sems)
            pl.when(step == 0)(start)

        @pl.when(step == 0)
        def _():
            state[...] = jnp.zeros_like(state)

        walk(ins, outs, state)
        if exchange:
            pl.when(step == n - 1)(finish)

    return pl.pallas_call(
        body, name=name, grid=(n,), in_specs=list(in_specs) + [ANY] * carried, out_specs=list(out_specs) + [ANY] * carried,
        out_shape=list(out_shape) + (exchange.out_shapes if exchange else []),
        scratch_shapes=[pltpu.VMEM((GDN_HEAD_DIM, GDN_WIDTH), f32)] + (exchange.scratch if exchange else []),
        compiler_params=_params(1))(*operands, *(exchange.operands if exchange else []))


def _delta_heads():
    heads = [slice(hd * GDN_HEAD_DIM, (hd + 1) * GDN_HEAD_DIM) for hd in range(GDN_HEADS)]
    inv_at = [slice(hd * GDN_CHUNK, (hd + 1) * GDN_CHUNK) for hd in range(GDN_HEADS)]
    return heads, inv_at


def _head_chunk(q, k, v, gates, s0, head, known_inv=None):
    g = jnp.broadcast_to(gates[:, head:head + 1], q.shape)
    beta = jnp.broadcast_to(gates[:, GDN_HEADS + head:GDN_HEADS + head + 1], q.shape)
    return _delta_chunk(q, k, v, g, beta, s0, known_inv)


def delta_rule_fwd(q, k, v, gates, name, exchange=None):
    s, width = q.shape
    c, dk = GDN_CHUNK, GDN_HEAD_DIM
    n = s // c
    heads, inv_at = _delta_heads()

    def walk(ins, outs, state):
        q_ref, k_ref, v_ref, gates_ref = ins
        o_ref, s_in_ref, inv_ref = outs
        s_in_ref[...] = state[...]
        gates = gates_ref[...]
        xs = [[r[:, hd] for r in (q_ref, k_ref, v_ref)] + [gates, state[:, hd], i] for i, hd in enumerate(heads)]
        ys = [_head_chunk(*x) for x in xs]
        for hd, at, (o, s1, inv) in zip(heads, inv_at, ys):
            o_ref[:, hd], state[:, hd], inv_ref[:, at] = o, s1, inv

    blk = pl.BlockSpec((c, width), lambda t: (t, 0))
    gt = pl.BlockSpec((c, LANES), lambda t: (t, 0))
    st = pl.BlockSpec((dk, width), lambda t: (t, 0))
    iv = pl.BlockSpec((c, GDN_HEADS * c), lambda t: (t, 0))
    return _delta_rule_call(name, walk, n, [blk] * 3 + [gt], [blk, st, iv],
                            [_sds((s, width)), _sds((n * dk, width)), _sds((s, GDN_HEADS * c))], (q, k, v, gates), exchange)


def delta_rule_bwd(q, k, v, gates, s_in, inv, do, name, exchange=None):
    s, width = q.shape
    c, dk = GDN_CHUNK, GDN_HEAD_DIM
    n = s // c
    heads, inv_at = _delta_heads()

    def walk(ins, outs, dstate):
        q_ref, k_ref, v_ref, gates_ref, s_ref, inv_ref, do_ref = ins
        dq_ref, dk_ref, dv_ref, dgates_ref = outs
        gates = gates_ref[...]
        xs = [[r[:, hd] for r in (q_ref, k_ref, v_ref)] + [gates, s_ref[:, hd]] for hd in heads]
        known = [inv_ref[:, at] for at in inv_at]
        cts = [(do_ref[:, hd], dstate[:, hd]) for hd in heads]
        grads = []
        for i, (x, t, ct) in enumerate(zip(xs, known, cts)):
            _, vjp = jax.vjp(lambda *y, t=t, i=i: _head_chunk(*y, i, known_inv=t)[:2], *x)
            grads.append(vjp(ct))
        dgates = grads[0][3]
        for g in grads[1:]:
            dgates = dgates + g[3]
        dgates_ref[...] = dgates
        for hd, (dq, dk_, dv, _, ds0) in zip(heads, grads):
            dq_ref[:, hd], dk_ref[:, hd], dv_ref[:, hd], dstate[:, hd] = dq, dk_, dv, ds0

    blk = pl.BlockSpec((c, width), lambda t: (n - 1 - t, 0))
    gt = pl.BlockSpec((c, LANES), lambda t: (n - 1 - t, 0))
    st = pl.BlockSpec((dk, width), lambda t: (n - 1 - t, 0))
    iv = pl.BlockSpec((c, GDN_HEADS * c), lambda t: (n - 1 - t, 0))
    return _delta_rule_call(name, walk, n, [blk] * 3 + [gt, st, iv, blk], [blk] * 3 + [gt],
                            [_sds((s, width))] * 3 + [_sds((s, LANES))], (q, k, v, gates, s_in, inv, do), exchange)


def adamw(w, g, m, v, name):
    shape = w.shape
    if len(shape) == 2:
        grid, spec = (1,), pl.BlockSpec(shape, lambda i: (0, 0))
    else:
        tile = shape[1] if shape[1] <= 512 else _pick(shape[1], (512, 256, 128))
        grid, spec = (shape[0], shape[1] // tile), pl.BlockSpec((None, tile, shape[2]), lambda layer, i: (layer, i, 0))

    def body(w_ref, g_ref, m_ref, v_ref, d_ref, nm_ref, nv_ref):
        grad = g_ref[...]
        nm = ADAM_B1 * m_ref[...] + (1.0 - ADAM_B1) * grad
        nv = ADAM_B2 * v_ref[...] + (1.0 - ADAM_B2) * (grad * grad)
        m_hat = nm / (1.0 - ADAM_B1 ** ADAM_STEP)
        v_hat = nv / (1.0 - ADAM_B2 ** ADAM_STEP)
        d_ref[...] = -ADAM_LR * (m_hat / (jnp.sqrt(v_hat) + ADAM_EPS) + ADAM_WD * w_ref[...])
        nm_ref[...] = nm
        nv_ref[...] = nv

    return tuple(pl.pallas_call(body, name=name, grid=grid, in_specs=[spec] * 4, out_specs=[spec] * 3,
                                out_shape=[_sds(shape)] * 3, compiler_params=_params(len(grid)))(w, g, m, v))


def _place():
    return lax.axis_index("x"), lax.axis_index("y"), lax.axis_index("c")


def _flip(p, bits):
    return tuple(1 - v if (bits >> s) & 1 else v for v, s in zip(p, (2, 1, 0)))


def _slot(p):
    return 4 * p[0] + 2 * p[1] + p[2]


def _chip_of(p):
    return 2 * p[0] + p[1]


ANY = pl.BlockSpec(memory_space=pl.ANY)


class Gather:
    scratch = (pltpu.SemaphoreType.DMA((7,)), pltpu.SemaphoreType.DMA((7,)), pltpu.SemaphoreType.DMA)

    def __init__(self, shard):
        self.operand = shard
        self.out_shape = jax.ShapeDtypeStruct((N_DEV,) + shard.shape, shard.dtype)

    def bind(self, x_ref, out_ref, send_sems, recv_sems, local_sem):
        me = _place()
        sibling = _flip(me, 1)
        chips = [_flip(me, 4), _flip(me, 2), _flip(me, 6)]

        def copy(k, block, to, src=None):
            return pltpu.make_async_remote_copy(
                src_ref=out_ref.at[_slot(block)] if src is None else src, dst_ref=out_ref.at[_slot(block)],
                send_sem=send_sems.at[k], recv_sem=recv_sems.at[k], device_id=to, device_id_type=MESH)

        mine = pltpu.make_async_copy(x_ref, out_ref.at[_slot(me)], local_sem)
        first = [copy(0, me, sibling, src=x_ref)] + [copy(1 + j, me, chip, src=x_ref) for j, chip in enumerate(chips)]
        passed = [copy(4 + j, chip, sibling) for j, chip in enumerate(chips)]

        def start():
            mine.start()
            for cp in first:
                cp.start()

        def finish():
            for j, chip in enumerate(chips):
                copy(1 + j, chip, me).wait_recv()
                passed[j].start()
            copy(0, sibling, me).wait_recv()
            for j, chip in enumerate(chips):
                copy(4 + j, _flip(chip, 1), me).wait_recv()
            for cp in first + passed:
                cp.wait_send()
            mine.wait()

        return start, finish


class ChipExchange:
    scratch = (pltpu.SemaphoreType.DMA((3,)), pltpu.SemaphoreType.DMA((3,)), pltpu.SemaphoreType.DMA)

    def __init__(self, blocks):
        self.operand = blocks
        self.out_shape = jax.ShapeDtypeStruct(blocks.shape, blocks.dtype)

    def bind(self, x_ref, out_ref, send_sems, recv_sems, local_sem):
        me = _place()
        peers = [_flip(me, 4), _flip(me, 2), _flip(me, 6)]
        mine = pltpu.make_async_copy(x_ref.at[_chip_of(me)], out_ref.at[_chip_of(me)], local_sem)

        def copy(j, src_chip, dst_chip):
            return pltpu.make_async_remote_copy(
                src_ref=x_ref.at[src_chip], dst_ref=out_ref.at[dst_chip], send_sem=send_sems.at[j],
                recv_sem=recv_sems.at[j], device_id=peers[j], device_id_type=MESH)

        sends = [copy(j, _chip_of(peer), _chip_of(me)) for j, peer in enumerate(peers)]

        def start():
            mine.start()
            for cp in sends:
                cp.start()

        def finish():
            for j, peer in enumerate(peers):
                copy(j, _chip_of(me), _chip_of(peer)).wait_recv()
            for cp in sends:
                cp.wait_send()
            mine.wait()

        return start, finish


class Together:
    def __init__(self, *parts):
        self.parts = parts
        self.operands = [p.operand for p in parts]
        self.out_shapes = [p.out_shape for p in parts]
        self.scratch = [s for p in parts for s in p.scratch]

    def bind(self, x_refs, out_refs, sems):
        bound, at = [], 0
        for p, x_ref, out_ref in zip(self.parts, x_refs, out_refs):
            bound.append(p.bind(x_ref, out_ref, *sems[at:at + len(p.scratch)]))
            at += len(p.scratch)

        def start():
            for s, _ in bound:
                s()

        def finish():
            for _, f in bound:
                f()

        return start, finish


def exchange_alone(exchange, name):
    n = len(exchange.operands)

    def body(*refs):
        start, finish = exchange.bind(refs[:n], refs[n:2 * n], refs[2 * n:])
        start()
        finish()

    return pl.pallas_call(body, name=name, out_shape=exchange.out_shapes, in_specs=[ANY] * n, out_specs=[ANY] * n,
                          scratch_shapes=exchange.scratch)(*exchange.operands)


def _row_tile(rows):
    return max([t for t in range(16, min(rows, 1024) + 1, 16) if rows % t == 0] or [rows])


def pair_exchange(blocks, name):
    n = len(blocks)

    def body(*refs):
        x_refs, theirs_refs, (send_sems, recv_sems) = refs[:n], refs[n:2 * n], refs[2 * n:]
        me = _place()
        remote = [pltpu.make_async_remote_copy(
            src_ref=x_refs[t].at[2 * q + 1 - me[2]], dst_ref=theirs_refs[t].at[q], send_sem=send_sems.at[4 * t + q],
            recv_sem=recv_sems.at[4 * t + q], device_id=_flip(me, 1), device_id_type=MESH) for t in range(n) for q in range(4)]
        for cp in remote:
            cp.start()
        for cp in remote:
            cp.wait()

    return pl.pallas_call(
        body, name=name, out_shape=[jax.ShapeDtypeStruct((4,) + b.shape[1:], b.dtype) for b in blocks], in_specs=[ANY] * n,
        out_specs=[ANY] * n, scratch_shapes=[pltpu.SemaphoreType.DMA((4 * n,)), pltpu.SemaphoreType.DMA((4 * n,))])(*blocks)


def pair_add(blocks, theirs, name):
    n, rows, width = theirs.shape
    tile = _row_tile(rows)
    spec = pl.BlockSpec((None, tile, width), lambda q, i: (q, i, 0))
    south = pl.BlockSpec((None, None, tile, width), lambda q, i: (q, 0, i, 0))
    north = pl.BlockSpec((None, None, tile, width), lambda q, i: (q, 1, i, 0))

    def body(s_ref, n_ref, b_ref, o_ref):
        mine = jnp.where(lax.axis_index("c") == 0, s_ref[...], n_ref[...])
        o_ref[...] = (mine.astype(f32) + b_ref[...].astype(f32)).astype(o_ref.dtype)

    by_core = blocks.reshape(n, 2, rows, width)
    return pl.pallas_call(body, name=name, grid=(n, rows // tile), in_specs=[south, north, spec], out_specs=spec,
                          out_shape=jax.ShapeDtypeStruct(theirs.shape, theirs.dtype), compiler_params=_params(2))(by_core, by_core, theirs)


def sum_slots(blocks, name):
    n, rows, width = blocks.shape
    tile = _row_tile(rows)

    def body(x_ref, o_ref):
        total = x_ref[0].astype(f32)
        for s in range(1, n):
            total = total + x_ref[s].astype(f32)
        o_ref[...] = total

    return pl.pallas_call(
        body, name=name, grid=(rows // tile,), in_specs=[pl.BlockSpec((n, tile, width), lambda i: (0, i, 0))],
        out_specs=pl.BlockSpec((tile, width), lambda i: (i, 0)), out_shape=_sds((rows, width)), compiler_params=_params(1))(blocks)


def all_reduce_small(x, name):
    rows, width = x.shape

    def body(x_ref, o_ref, land, send_sems, recv_sems):
        me = _place()
        copies = []
        for k in range(1, N_DEV):
            peer = _flip(me, k)
            copies.append(pltpu.make_async_remote_copy(
                src_ref=x_ref, dst_ref=land.at[_slot(me)], send_sem=send_sems.at[k - 1], recv_sem=recv_sems.at[k - 1],
                device_id=peer, device_id_type=MESH))
        for cp in copies:
            cp.start()
        land[_slot(me)] = x_ref[...]
        for k in range(1, N_DEV):
            peer = _flip(me, k)
            pltpu.make_async_remote_copy(
                src_ref=x_ref, dst_ref=land.at[_slot(peer)], send_sem=send_sems.at[k - 1], recv_sem=recv_sems.at[k - 1],
                device_id=peer, device_id_type=MESH).wait_recv()
        total = land[0]
        for s in range(1, N_DEV):
            total = total + land[s]
        o_ref[...] = total
        for cp in copies:
            cp.wait_send()

    return pl.pallas_call(
        body, name=name, out_shape=_sds((rows, width)), in_specs=[pl.BlockSpec(memory_space=pltpu.VMEM)],
        out_specs=pl.BlockSpec(memory_space=pltpu.VMEM),
        scratch_shapes=[pltpu.VMEM((N_DEV, rows, width), f32), pltpu.SemaphoreType.DMA((7,)), pltpu.SemaphoreType.DMA((7,))],
    )(x)


def _pack_big(shards):
    packed = {name: shards[name].astype(bf16) for name in COL_SHARDED}
    packed["rows"] = jnp.concatenate([shards[name].astype(bf16) for name, _ in ROW_SHARDED], axis=1)
    return packed


def _unpack_gathered(gathered):
    full = {}
    for name, part in gathered.items():
        if name == "w_gate_up":
            full[name] = part
        elif name in COL_SHARDED:
            full[name] = part.transpose(1, 0, 2).reshape(D_MODEL, N_DEV * part.shape[2])
        else:
            at = 0
            for weight, rows in ROW_SHARDED:
                full[weight] = part[:, at:at + rows, :].reshape(N_DEV * rows, D_MODEL)
                at += rows
    return full


def _pack_grads(grads, group):
    packed = {}
    for name in group:
        if name == "w_gate_up":
            packed[name] = grads[name]
        elif name == "rows":
            packed[name] = jnp.concatenate([grads[weight].reshape(N_DEV, rows, D_MODEL) for weight, rows in ROW_SHARDED], axis=1)
        else:
            packed[name] = grads[name].reshape(D_MODEL, N_DEV, grads[name].shape[1] // N_DEV).transpose(1, 0, 2)
    return packed


def _unpack_shard(layers):
    out = {name: jnp.stack([layer[name] for layer in layers]) for name in COL_SHARDED}
    rows_pack, at = jnp.stack([layer["rows"] for layer in layers]), 0
    for weight, rows in ROW_SHARDED:
        out[weight] = rows_pack[:, at:at + rows, :]
        at += rows
    return out


def _rows_of(flat_len):
    return -(-flat_len // (8 * D_MODEL)) * 8


def _pack_small(parts):
    flat = jnp.concatenate([p.reshape(-1) for p in parts])
    rows = _rows_of(flat.shape[0])
    flat = jnp.pad(flat, (0, rows * D_MODEL - flat.shape[0]))
    return flat.reshape(rows, D_MODEL)


def _unpack_small(packed, like):
    flat, out, at = packed.reshape(-1), [], 0
    for p in like:
        out.append(flat[at:at + p.size].reshape(p.shape))
        at += p.size
    return out


def _rope_tables(positions):
    inv_freq = jnp.float32(ROPE_THETA) ** (-jnp.arange(0, ROPE_DIM, 2, dtype=f32) / ROPE_DIM)
    ang = positions.astype(f32)[:, None] * inv_freq
    cos, sin = jnp.cos(ang), jnp.sin(ang)
    rest = ATTN_HEAD_DIM - ROPE_DIM
    cos_h = jnp.concatenate([cos, cos, jnp.ones((cos.shape[0], rest), f32)], axis=1)
    sin_h = jnp.concatenate([-sin, sin, jnp.zeros((sin.shape[0], rest), f32)], axis=1)
    return jnp.tile(cos_h, (1, ATTN_HEADS)), jnp.tile(sin_h, (1, ATTN_HEADS))


HEAD_SMALL = ("norm_mix_pre", "conv_short", "conv_gdn", "gdn_a_log", "gdn_dt_bias")


def _layer_head(h, p, cos_t, sin_t):
    hn = rms_norm(h, p["norm_mix_pre"][None], "norm_mix_pre")
    aw, cw, gw = ATTN_WIDTH, CONV_WIDTH, GDN_WIDTH
    aq, ak, av, cb, cc, cx, gqkv, ab, gate = _linear_split(hn, p["w_in"], (aw, aw, aw, cw, cw, cw, 3 * gw, 2 * GDN_HEADS, gw), "w_in")
    ab = jnp.pad(ab, ((0, 0), (0, LANES - 2 * GDN_HEADS)))
    y_attn = dilated_attention(rope(aq, cos_t, sin_t, ATTN_HEAD_DIM ** -0.5, "rope_q"), rope(ak, cos_t, sin_t, 1.0, "rope_k"),
                               av, "attn")
    y_conv = short_conv(cb, cc, cx, p["conv_short"], "short_conv")
    qkv = gdn_pre(gqkv, p["conv_gdn"], "gdn_pre")
    pv = jnp.zeros((8, LANES), f32).at[0, :GDN_HEADS].set(p["gdn_a_log"]).at[1, :GDN_HEADS].set(p["gdn_dt_bias"])
    return (*_split_cols(qkv, (gw, gw, gw)), gate_beta(ab, pv, "gate_beta")), (gate, y_attn, y_conv)


MID_PARAMS = ("gdn_norm", "w_out", "norm_mix_post", "norm_xattn_pre", "w_xq", "norm_mem", "w_xkv", "w_xo", "norm_xattn_post",
              "norm_ffn_pre")


def _layer_mid(h, o, gate, y_attn, y_conv, p, mem):
    y_gdn = gdn_post(o, gate, p["gdn_norm"][None], "gdn_post")
    mix = _linear(jnp.concatenate([y_attn, y_conv, y_gdn], axis=1), p["w_out"], "w_out")
    h, hn = add_norm_then_norm(h, mix, p["norm_mix_post"][None], p["norm_xattn_pre"][None], "norm_mix_xattn")
    qx = _linear(hn, p["w_xq"], "w_xq")
    kv = _linear(rms_norm(mem, p["norm_mem"][None], "norm_mem"), p["w_xkv"], "w_xkv")
    xa = _linear(cross_attention(qx, kv, "xattn"), p["w_xo"], "w_xo")
    return add_norm_then_norm(h, xa, p["norm_xattn_post"][None], p["norm_ffn_pre"][None], "norm_xattn_ffn")


def _pair_summed(grads, group, name):
    blocks = _pack_grads(grads, group)
    theirs = pair_exchange([blocks[n] for n in group], name + "_pair_exchange")
    return [pair_add(blocks[n], t, f"{name}_pair_add_{n}") for n, t in zip(group, theirs)]


def _forward_backward(x, packed, small, mem, cos_t, sin_t, target):
    def gathers(group, layer):
        return [Gather(packed[n][layer]) for n in group]

    h = x
    head_gathered = exchange_alone(Together(*gathers(HEAD_GROUP, 0)), "gather_first")
    saved = []
    for layer in range(DEPTH):
        at_layer = {n: t[layer] for n, t in small.items()}
        head_p = {**_unpack_gathered(dict(zip(HEAD_GROUP, head_gathered))), **{n: at_layer[n] for n in HEAD_SMALL}}
        (rule_in, rest), head_vjp = jax.vjp(lambda h, hp: _layer_head(h, hp, cos_t, sin_t), h, head_p)
        carried = gathers(TAIL_GROUP, layer) + (gathers(HEAD_GROUP, layer + 1) if layer + 1 < DEPTH else [])
        o, s_in, inv, *landed = delta_rule_fwd(*rule_in, "delta_rule_fwd", Together(*carried))
        head_gathered = landed[len(TAIL_GROUP):]
        tail_p = {**_unpack_gathered(dict(zip(TAIL_GROUP, landed))), **at_layer}
        mid_p = {n: tail_p[n] for n in MID_PARAMS}
        (h, hn), mid_vjp = jax.vjp(lambda h, o, rest, mp: _layer_mid(h, o, *rest, mp, mem), h, o, rest, mid_p)
        y, ffn_saved = ffn_forward(hn, tail_p["w_gate_up"], tail_p["w_down"], "ffn")
        h, last_vjp = jax.vjp(lambda h, y, w: add_norm(h, y, w[None], "norm_ffn_post"), h, y, tail_p["norm_ffn_post"])
        saved.append((head_vjp, mid_vjp, last_vjp, ffn_saved, rule_in, s_in, inv))

    loss, dh = jax.value_and_grad(lambda y: loss_rows(y, target, "loss"))(h)

    def summed(group, landed):
        return {n: sum_slots(t, "sum_grads_" + n) for n, t in zip(group, landed)}

    big_grads, small_grads, head_pending = [{} for _ in range(DEPTH)], [None] * DEPTH, []
    for layer in reversed(range(DEPTH)):
        head_vjp, mid_vjp, last_vjp, ffn_saved, rule_in, s_in, inv = saved[layer]
        dh, dy, d_norm_ffn_post = last_vjp(dh)
        dhn, d_gate_up, d_down, landed = ffn_backward(
            ffn_saved, dy, "ffn", Together(*[ChipExchange(t) for t in head_pending]) if head_pending else None)
        if head_pending:
            big_grads[layer + 1].update(summed(HEAD_GROUP, landed))
        dh_mid, do, d_rest, d_mid_p = mid_vjp((dh, dhn))
        d_tail_p = {**d_mid_p, "w_gate_up": d_gate_up, "w_down": d_down, "norm_ffn_post": d_norm_ffn_post}
        carried = Together(*[ChipExchange(t) for t in _pair_summed(d_tail_p, TAIL_GROUP, "tail")])
        *d_rule_in, = delta_rule_bwd(*rule_in, s_in, inv, do, "delta_rule_bwd", carried)
        big_grads[layer].update(summed(TAIL_GROUP, d_rule_in[4:]))
        dh_head, d_head_p = head_vjp((tuple(d_rule_in[:4]), d_rest))
        dh = dh_mid + dh_head
        small_grads[layer] = {n: t for n, t in {**d_head_p, **d_tail_p}.items() if n in small}
        head_pending = _pair_summed(d_head_p, HEAD_GROUP, "head")
    landed = exchange_alone(Together(*[ChipExchange(t) for t in head_pending]), "exchange_last")
    big_grads[0].update(summed(HEAD_GROUP, landed))
    return loss, dh, big_grads, small_grads


def kernel(x, mem, positions, norm_mix_pre, norm_mix_post, w_in, conv_short, conv_gdn, gdn_a_log, gdn_dt_bias, gdn_norm, w_out, norm_mem, norm_xattn_pre, norm_xattn_post, w_xq, w_xkv, w_xo, norm_ffn_pre, norm_ffn_post, w_gate_up, w_down, loss_target, m_norm_mix_pre, m_norm_mix_post, m_w_in, m_conv_short, m_conv_gdn, m_gdn_a_log, m_gdn_dt_bias, m_gdn_norm, m_w_out, m_norm_mem, m_norm_xattn_pre, m_norm_xattn_post, m_w_xq, m_w_xkv, m_w_xo, m_norm_ffn_pre, m_norm_ffn_post, m_w_gate_up, m_w_down, v_norm_mix_pre, v_norm_mix_post, v_w_in, v_conv_short, v_conv_gdn, v_gdn_a_log, v_gdn_dt_bias, v_gdn_norm, v_w_out, v_norm_mem, v_norm_xattn_pre, v_norm_xattn_post, v_w_xq, v_w_xkv, v_w_xo, v_norm_ffn_pre, v_norm_ffn_post, v_w_gate_up, v_w_down):
    given = dict(locals())
    weights = {n: given[n] for n in WEIGHTS}
    me = _slot(_place())

    def in_place(shard):
        full = jnp.zeros(shard.shape[:-1] + (shard.shape[-1] * N_DEV,), f32)
        return lax.dynamic_update_slice_in_dim(full, shard, me * shard.shape[-1], axis=shard.ndim - 1)

    placed = [in_place(conv_short), in_place(conv_gdn)]
    conv_short_full, conv_gdn_full = _unpack_small(all_reduce_small(_pack_small(placed), "gather_conv"), placed)
    small = {n: weights[n] for n in NORMS + ("gdn_a_log", "gdn_dt_bias", "gdn_norm")}
    small["conv_short"], small["conv_gdn"] = conv_short_full, conv_gdn_full

    cos_t, sin_t = _rope_tables(positions[0])
    loss, grad_x, big_layers, small_layers = _forward_backward(
        x[0], _pack_big(weights), small, mem[0], cos_t, sin_t, loss_target[0])
    grads = _unpack_shard(big_layers)

    names = sorted(small)
    parts = [jnp.stack([layer[n] for layer in small_layers]) for n in names] + [loss.reshape(1)]
    reduced = _unpack_small(all_reduce_small(_pack_small(parts), "reduce_small"), parts)
    loss = reduced[-1][0]
    for n, g in zip(names, reduced[:-1]):
        if n in ("conv_short", "conv_gdn"):
            width = weights[n].shape[-1]
            g = lax.dynamic_slice_in_dim(g, me * width, width, axis=g.ndim - 1)
        grads[n] = g

    delta, new_m, new_v = {}, {}, {}
    for n in WEIGHTS:
        delta[n], new_m[n], new_v[n] = adamw(weights[n], grads[n], given["m_" + n], given["v_" + n], "adamw_" + n)
    return (loss, grad_x[None], *[grads[n] for n in WEIGHTS], *[delta[n] for n in WEIGHTS],
            *[new_m[n] for n in WEIGHTS], *[new_v[n] for n in WEIGHTS])
```

```python
import functools

import jax
import jax.numpy as jnp
from jax import lax
from jax.experimental import pallas as pl
from jax.experimental.pallas import tpu as pltpu

f32 = jnp.float32
bf16 = jnp.bfloat16
MESH = pl.DeviceIdType.MESH

N_DEV = 8
DEPTH = 4
D_MODEL = 1024
EPS = 1e-6
ATTN_HEADS, ATTN_HEAD_DIM = 4, 64
ATTN_WIDTH = ATTN_HEADS * ATTN_HEAD_DIM
DILATIONS = (1, 4, 16)
QB = 128
ROPE_THETA = 500000.0
ROPE_DIM = ATTN_HEAD_DIM // 4
CONV_WIDTH = 256
GDN_HEADS, GDN_HEAD_DIM = 4, 128
GDN_WIDTH = GDN_HEADS * GDN_HEAD_DIM
GDN_CHUNK = 64
XATTN_HEADS, XATTN_HEAD_DIM = 4, 256
LANES = 128
ROW_TILE = 512
VMEM_LIMIT = 56 * 1024 * 1024

ADAM_LR, ADAM_B1, ADAM_B2, ADAM_EPS, ADAM_WD, ADAM_STEP = 0.001, 0.9, 0.999, 1e-08, 0.01, 10

COL_SHARDED = ("w_in", "w_xkv", "w_gate_up")
ROW_SHARDED = (("w_out", 128), ("w_xq", 128), ("w_xo", 128), ("w_down", 352))
HEAD_GROUP = ("w_in",)
TAIL_GROUP = ("w_gate_up", "w_xkv", "rows")
NORMS = ("norm_mix_pre", "norm_mix_post", "norm_mem", "norm_xattn_pre", "norm_xattn_post", "norm_ffn_pre", "norm_ffn_post")
WEIGHTS = ("norm_mix_pre", "norm_mix_post", "w_in", "conv_short", "conv_gdn", "gdn_a_log", "gdn_dt_bias", "gdn_norm", "w_out",
           "norm_mem", "norm_xattn_pre", "norm_xattn_post", "w_xq", "w_xkv", "w_xo", "norm_ffn_pre", "norm_ffn_post",
           "w_gate_up", "w_down")


def _params(n_grid):
    return pltpu.CompilerParams(dimension_semantics=("arbitrary",) * n_grid, vmem_limit_bytes=VMEM_LIMIT)


def _pick(n, cands):
    for c in cands:
        if n % c == 0:
            return c
    return n


MXU_FLOPS = 9.0e14
HBM_BYTES_PER_S = 2.5e12
VMEM_RMW_BYTES_PER_S = 7.0e12
STEP_S = 0.4e-6
MATMUL_VMEM = 44 * 1024 * 1024


def _tiles(m, n, k, sa, sb, so):
    def divisors(d):
        return sorted({d} | {d // s for s in range(1, d // LANES + 1) if d % s == 0 and (d // s) % LANES == 0}, reverse=True)

    best = None
    for tk in divisors(k):
        nk = k // tk
        for tm in divisors(m):
            for tn_ in divisors(n):
                per_step = tm * tk * sa + tk * tn_ * sb + tm * tn_ * so
                vmem = 2 * per_step + (tm * tn_ * 4 if nk > 1 else 0)
                vmem += (tm * tk * 2 if sa == 4 else 0) + (tk * tn_ * 2 if sb == 4 else 0) + tm * tn_ * 4
                if vmem > MATMUL_VMEM:
                    continue
                moved = m * k * sa * (1 if nk == 1 else n // tn_) + k * n * sb * (1 if nk == 1 and n == tn_ else m // tm) + m * n * so
                busy = 2 * m * n * k / MXU_FLOPS + (m * n * 8 * nk / VMEM_RMW_BYTES_PER_S if nk > 1 else 0)
                cost = max(moved / HBM_BYTES_PER_S, busy) + per_step / HBM_BYTES_PER_S + (m // tm) * (n // tn_) * nk * STEP_S
                if best is None or cost < best[0]:
                    best = (cost, tm, tn_, tk)
    return best[1:]


def _mm(a, b, ta, tb, out_dtype, name):
    m, k = (a.shape[1], a.shape[0]) if ta else a.shape
    n = b.shape[0] if tb else b.shape[1]
    tm, tn, tk = _tiles(m, n, k, a.dtype.itemsize, b.dtype.itemsize, jnp.dtype(out_dtype).itemsize)
    nk = k // tk
    a_spec = pl.BlockSpec((tk, tm), lambda i, j, kk: (kk, i)) if ta else pl.BlockSpec((tm, tk), lambda i, j, kk: (i, kk))
    b_spec = pl.BlockSpec((tn, tk), lambda i, j, kk: (j, kk)) if tb else pl.BlockSpec((tk, tn), lambda i, j, kk: (kk, j))
    dims = (((0 if ta else 1,), (1 if tb else 0,)), ((), ()))

    def body(a_ref, b_ref, o_ref, *acc):
        kk = pl.program_id(2)
        p = lax.dot_general(a_ref[...].astype(bf16), b_ref[...].astype(bf16), dims, preferred_element_type=f32)
        if nk == 1:
            o_ref[...] = p.astype(o_ref.dtype)
            return
        acc_ref, = acc

        @pl.when(kk == 0)
        def _():
            acc_ref[...] = p

        @pl.when(kk > 0)
        def _():
            acc_ref[...] += p

        @pl.when(kk == nk - 1)
        def _():
            o_ref[...] = acc_ref[...].astype(o_ref.dtype)

    return pl.pallas_call(
        body, name=name, grid=(m // tm, n // tn, nk), in_specs=[a_spec, b_spec],
        out_specs=pl.BlockSpec((tm, tn), lambda i, j, kk: (i, j)), out_shape=jax.ShapeDtypeStruct((m, n), out_dtype),
        scratch_shapes=[pltpu.VMEM((tm, tn), f32)] if nk > 1 else [], compiler_params=_params(3))(a, b)


def _linear(x, w, name):
    @jax.custom_vjp
    def lin(x, w):
        return _mm(x, w, False, False, f32, name + "_y")

    def lin_f(x, w):
        return _mm(x, w, False, False, f32, name + "_y"), (x, w)

    def lin_b(res, dy):
        x, w = res
        return _mm(dy, w, False, True, f32, name + "_dx"), _mm(x, dy, True, False, bf16, name + "_dw")

    lin.defvjp(lin_f, lin_b)
    return lin(x, w)


def _bdot(a, b, form):
    dims = {"nn": ((1,), (0,)), "nt": ((1,), (1,)), "tn": ((0,), (0,))}[form]
    return lax.dot_general(a.astype(bf16), b.astype(bf16), (dims, ((), ())), preferred_element_type=f32)


def ffn_forward(hn, w_gate_up, w_down, name):
    s, k = hn.shape
    n_blocks, _, width = w_gate_up.shape
    half = n_blocks // 2
    tm = 512
    blocked = jax.ShapeDtypeStruct((half, s, width), bf16)

    def act_body(x_ref, wg_ref, wu_ref, gate_ref, up_ref, act_ref):
        x = x_ref[...]
        gate, up = _bdot(x, wg_ref[...], "nn"), _bdot(x, wu_ref[...], "nn")
        gate_ref[...], up_ref[...] = gate.astype(bf16), up.astype(bf16)
        act_ref[...] = (jax.nn.silu(gate) * up).astype(bf16)

    tile = pl.BlockSpec((None, tm, width), lambda i, d: (d, i, 0))
    gate, up, act = pl.pallas_call(
        act_body, name=name + "_act", grid=(s // tm, half),
        in_specs=[pl.BlockSpec((tm, k), lambda i, d: (i, 0)), pl.BlockSpec((None, k, width), lambda i, d: (d, 0, 0)),
                  pl.BlockSpec((None, k, width), lambda i, d: (d + half, 0, 0))],
        out_specs=[tile] * 3, out_shape=[blocked] * 3, compiler_params=_params(2))(hn, w_gate_up, w_gate_up)

    n = w_down.shape[1]
    tn = 512

    def y_body(act_ref, w_ref, y_ref):
        y_ref[...] = sum(_bdot(act_ref[d], w_ref[d * width:(d + 1) * width, :], "nn") for d in range(half))

    y = pl.pallas_call(
        y_body, name=name + "_y", grid=(s // tm, n // tn),
        in_specs=[pl.BlockSpec((half, tm, width), lambda i, j: (0, i, 0)), pl.BlockSpec((half * width, tn), lambda i, j: (0, j))],
        out_specs=pl.BlockSpec((tm, tn), lambda i, j: (i, j)), out_shape=_sds((s, n)), compiler_params=_params(2))(act, w_down)
    return y, (hn, w_gate_up, w_down, gate, up, act)


def ffn_backward(saved, dy, name, exchange=None):
    hn, w_gate_up, w_down, gate, up, act = saved
    s, k = hn.shape
    n_blocks, _, width = w_gate_up.shape
    half = n_blocks // 2
    n = w_down.shape[1]
    tm = 512
    blocked = jax.ShapeDtypeStruct((half, s, width), bf16)
    carried = len(exchange.operands) if exchange else 0
    steps = (s // tm, half)

    def dact_body(dy_ref, w_ref, gate_ref, up_ref, *refs):
        x_refs, refs = refs[:carried], refs[carried:]
        (dgate_ref, dup_ref), refs = refs[:2], refs[2:]
        if exchange:
            at = pl.program_id(0) * steps[1] + pl.program_id(1)
            start, wait = exchange.bind(x_refs, refs[:carried], refs[carried:])
            pl.when(at == 0)(start)
        d_act = _bdot(dy_ref[...], w_ref[...], "nt")
        g, u = gate_ref[...].astype(f32), up_ref[...].astype(f32)
        sig = jax.nn.sigmoid(g)
        dgate_ref[...] = (d_act * u * sig * (1.0 + g * (1.0 - sig))).astype(bf16)
        dup_ref[...] = (d_act * g * sig).astype(bf16)
        if exchange:
            pl.when(at == steps[0] * steps[1] - 1)(wait)

    tile = pl.BlockSpec((None, tm, width), lambda i, d: (d, i, 0))
    d_gate, d_up, *landed = pl.pallas_call(
        dact_body, name=name + "_dact", grid=steps,
        in_specs=[pl.BlockSpec((tm, n), lambda i, d: (i, 0)), pl.BlockSpec((width, n), lambda i, d: (d, 0)), tile, tile] + [ANY] * carried,
        out_specs=[tile, tile] + [ANY] * carried, out_shape=[blocked, blocked] + (exchange.out_shapes if exchange else []),
        scratch_shapes=exchange.scratch if exchange else [],
        compiler_params=_params(2))(dy, w_down, gate, up, *(exchange.operands if exchange else []))

    def dx_body(dg_ref, du_ref, w_ref, dx_ref):
        dx_ref[...] = sum(_bdot(dg_ref[d], w_ref[d], "nt") + _bdot(du_ref[d], w_ref[d + half], "nt") for d in range(half))

    tx = 256
    rows = pl.BlockSpec((half, tx, width), lambda i: (0, i, 0))
    dx = pl.pallas_call(
        dx_body, name=name + "_dx", grid=(s // tx,), in_specs=[rows, rows, _whole(w_gate_up.shape)],
        out_specs=pl.BlockSpec((tx, k), lambda i: (i, 0)), out_shape=_sds((s, k)), compiler_params=_params(1))(d_gate, d_up, w_gate_up)

    def dw1_body(x_ref, dg_ref, du_ref, dw_ref):
        d_block = jnp.where(pl.program_id(0) < half, dg_ref[...], du_ref[...])
        dw_ref[...] = _bdot(x_ref[...], d_block, "tn").astype(bf16)

    d_w_gate_up = pl.pallas_call(
        dw1_body, name=name + "_dw1", grid=(n_blocks,),
        in_specs=[_whole((s, k)), pl.BlockSpec((None, s, width), lambda b: (jnp.minimum(b, half - 1), 0, 0)),
                  pl.BlockSpec((None, s, width), lambda b: (jnp.maximum(b - half, 0), 0, 0))],
        out_specs=pl.BlockSpec((None, k, width), lambda b: (b, 0, 0)), out_shape=jax.ShapeDtypeStruct(w_gate_up.shape, bf16),
        compiler_params=_params(1))(hn, d_gate, d_up)

    tn = 512

    def dw2_body(act_ref, dy_ref, dw_ref):
        dw_ref[...] = _bdot(act_ref[...], dy_ref[...], "tn").astype(bf16)

    d_w_down = pl.pallas_call(
        dw2_body, name=name + "_dw2", grid=(half, n // tn),
        in_specs=[pl.BlockSpec((None, s, width), lambda d, j: (d, 0, 0)), pl.BlockSpec((s, tn), lambda d, j: (0, j))],
        out_specs=pl.BlockSpec((width, tn), lambda d, j: (d, j)), out_shape=jax.ShapeDtypeStruct(w_down.shape, bf16),
        compiler_params=_params(2))(act, dy)
    return dx, d_w_gate_up, d_w_down, landed


def _blocked_linear_split(x, w, widths, name):
    nb, k, c = w.shape
    s = x.shape[0]
    edges = [sum(widths[:i]) for i in range(len(widths) + 1)]
    ranges = list(zip(edges[:-1], edges[1:]))
    assert edges[-1] == nb * c
    tm, tx = 512, 256

    def forward(x, w):
        def y_body(x_ref, w_ref, y_ref):
            y_ref[...] = _bdot(x_ref[...], w_ref[...], "nn")

        y = pl.pallas_call(
            y_body, name=name + "_y", grid=(s // tm, nb),
            in_specs=[pl.BlockSpec((tm, k), lambda i, d: (i, 0)), pl.BlockSpec((None, k, c), lambda i, d: (d, 0, 0))],
            out_specs=pl.BlockSpec((None, tm, c), lambda i, d: (d, i, 0)), out_shape=_sds((nb, s, c)), compiler_params=_params(2))(x, w)
        outs = []
        for c0, c1 in ranges:
            blocks = [d for d in range(nb) if c0 < (d + 1) * c and c1 > d * c]
            outs.append(jnp.concatenate([y[d][:, max(c0, d * c) - d * c:min(c1, (d + 1) * c) - d * c] for d in blocks], axis=1))
        return tuple(outs), (x, w)

    def backward(res, cts):
        x, w = res
        dy = []
        for d in range(nb):
            parts = [ct[:, max(c0, d * c) - c0:min(c1, (d + 1) * c) - c0].astype(bf16)
                     for (c0, c1), ct in zip(ranges, cts) if c0 < (d + 1) * c and c1 > d * c]
            dy.append(jnp.concatenate(parts, axis=1))
        dy = jnp.stack(dy)

        def dx_body(dy_ref, w_ref, dx_ref):
            dx_ref[...] = sum(_bdot(dy_ref[d], w_ref[d], "nt") for d in range(nb))

        dx = pl.pallas_call(
            dx_body, name=name + "_dx", grid=(s // tx,), in_specs=[pl.BlockSpec((nb, tx, c), lambda i: (0, i, 0)), _whole(w.shape)],
            out_specs=pl.BlockSpec((tx, k), lambda i: (i, 0)), out_shape=_sds((s, k)), compiler_params=_params(1))(dy, w)

        def dw_body(x_ref, dy_ref, dw_ref):
            dw_ref[...] = _bdot(x_ref[...], dy_ref[...], "tn").astype(bf16)

        dw = pl.pallas_call(
            dw_body, name=name + "_dw", grid=(nb,), in_specs=[_whole((s, k)), pl.BlockSpec((None, s, c), lambda d: (d, 0, 0))],
            out_specs=pl.BlockSpec((None, k, c), lambda d: (d, 0, 0)), out_shape=jax.ShapeDtypeStruct(w.shape, bf16),
            compiler_params=_params(1))(x, dy)
        return dx, dw

    @jax.custom_vjp
    def op(x, w):
        return forward(x, w)[0]

    op.defvjp(forward, backward)
    return op(x, w)


def _split_cols(x, widths):
    edges = [sum(widths[:i]) for i in range(len(widths) + 1)]

    def cut(x):
        return tuple(x[:, a:b] for a, b in zip(edges[:-1], edges[1:]))

    @jax.custom_vjp
    def split(x):
        return cut(x)

    split.defvjp(lambda x: (cut(x), None), lambda _, cts: (jnp.concatenate(cts, axis=1),))
    return split(x)


def _block_op(name, f, grid, in_specs, out_defs, arrays, diff, acc=None, gdefs=None):
    acc, gdefs = acc or {}, gdefs or {}
    n_in, n_out, n_grid = len(in_specs), len(out_defs), len(grid)

    def fwd_call(*xs):
        def body(*refs):
            outs = f(*[r[...] for r in refs[:n_in]])
            for r, o in zip(refs[n_in:], outs):
                r[...] = o.astype(r.dtype)

        return pl.pallas_call(
            body, name=name + "_fwd", grid=grid, in_specs=in_specs, out_specs=[d[1] for d in out_defs],
            out_shape=[d[0] for d in out_defs], compiler_params=_params(n_grid))(*xs)

    def bwd_call(*xs_and_cts):
        def body(*refs):
            xs = [r[...] for r in refs[:n_in]]
            cts = tuple(r[...] for r in refs[n_in:n_in + n_out])

            def of_diff(*dx):
                full = list(xs)
                for i, v in zip(diff, dx):
                    full[i] = v
                return tuple(f(*full))

            _, vjp = jax.vjp(of_diff, *[xs[i] for i in diff])
            grads = vjp(cts)
            for i, g, r in zip(diff, grads, refs[n_in + n_out:]):
                if i in acc:
                    first = functools.reduce(jnp.logical_and, [pl.program_id(a) == 0 for a in acc[i]])

                    @pl.when(first)
                    def _(r=r):
                        r[...] = jnp.zeros_like(r)

                    r[...] += g.astype(r.dtype)
                else:
                    r[...] = g.astype(r.dtype)

        g_defs = [gdefs.get(i, (jax.ShapeDtypeStruct(arrays[i].shape, f32), in_specs[i])) for i in diff]
        return pl.pallas_call(
            body, name=name + "_bwd", grid=grid, in_specs=list(in_specs) + [d[1] for d in out_defs],
            out_specs=[d[1] for d in g_defs], out_shape=[d[0] for d in g_defs], compiler_params=_params(n_grid))(*xs_and_cts)

    return fwd_call, bwd_call


def _simple_op(name, f, grid, in_specs, out_defs, arrays, diff, acc=None):
    fwd_call, bwd_call = _block_op(name, f, grid, in_specs, out_defs, arrays, diff, acc)

    @jax.custom_vjp
    def op(*xs):
        return tuple(fwd_call(*xs))

    def op_f(*xs):
        return tuple(fwd_call(*xs)), xs

    def op_b(xs, cts):
        grads = bwd_call(*xs, *cts)
        out = [jnp.zeros_like(x) for x in xs]
        for i, g in zip(diff, grads):
            out[i] = g
        return tuple(out)

    op.defvjp(op_f, op_b)
    return op(*arrays)


def _rows(width, tile=ROW_TILE):
    return pl.BlockSpec((tile, width), lambda i: (i, 0))


def _whole(shape):
    return pl.BlockSpec(shape, lambda *_: (0,) * len(shape))


def _sds(shape):
    return jax.ShapeDtypeStruct(shape, f32)


def _rms(x, w):
    return x * lax.rsqrt(jnp.mean(x * x, axis=-1, keepdims=True) + EPS) * w


def rms_norm(x, w, name):
    r, d = x.shape
    tile = min(ROW_TILE, r)
    return _simple_op(name, lambda x, w: (_rms(x, w),), (r // tile,), [_rows(d, tile), _whole((1, d))],
                      [(_sds((r, d)), _rows(d, tile))], (x, w), (0, 1), {1: (0,)})[0]


def add_norm(h, y, w, name):
    r, d = h.shape
    return _simple_op(name, lambda h, y, w: (h + _rms(y, w),), (r // ROW_TILE,), [_rows(d), _rows(d), _whole((1, d))],
                      [(_sds((r, d)), _rows(d))], (h, y, w), (0, 1, 2), {2: (0,)})[0]


def add_norm_then_norm(h, y, w_post, w_pre, name):
    r, d = h.shape

    def f(h, y, w_post, w_pre):
        h_new = h + _rms(y, w_post)
        return h_new, _rms(h_new, w_pre)

    return _simple_op(name, f, (r // ROW_TILE,), [_rows(d), _rows(d), _whole((1, d)), _whole((1, d))],
                      [(_sds((r, d)), _rows(d))] * 2, (h, y, w_post, w_pre), (0, 1, 2, 3), {2: (0,), 3: (0,)})


def _swap8(x):
    def raw(x):
        lane = lax.broadcasted_iota(jnp.int32, x.shape, 1) % ATTN_HEAD_DIM
        half = ROPE_DIM // 2
        up = pltpu.roll(x, x.shape[1] - half, axis=1)
        down = pltpu.roll(x, half, axis=1)
        return jnp.where(lane < half, up, jnp.where(lane < ROPE_DIM, down, 0.0))

    @jax.custom_vjp
    def swap(x):
        return raw(x)

    swap.defvjp(lambda x: (raw(x), None), lambda _, g: (raw(g),))
    return swap(x)


def rope(x, cos_t, sin_t, scale, name):
    r, d = x.shape
    return _simple_op(name, lambda x, c, s: ((x * c + _swap8(x) * s) * scale,), (r // ROW_TILE,), [_rows(d)] * 3,
                      [(_sds((r, d)), _rows(d))], (x, cos_t, sin_t), (0,))[0]


def _shift_rows(x, k):
    n = x.shape[0]

    def down(x):
        row = lax.broadcasted_iota(jnp.int32, x.shape, 0)
        return jnp.where(row >= k, pltpu.roll(x, k, axis=0), 0.0)

    def up(x):
        row = lax.broadcasted_iota(jnp.int32, x.shape, 0)
        return jnp.where(row < n - k, pltpu.roll(x, n - k, axis=0), 0.0)

    @jax.custom_vjp
    def shift(x):
        return down(x)

    shift.defvjp(lambda x: (down(x), None), lambda _, g: (up(g),))
    return shift(x)


def _causal_conv(x, w):
    taps = w.shape[0]
    y = x * w[taps - 1:taps, :]
    for j in range(taps - 1):
        y = y + _shift_rows(x, taps - 1 - j) * w[j:j + 1, :]
    return y


def _cols(rows, at=0):
    return pl.BlockSpec((rows, LANES), lambda j: (0, at + j))


def short_conv(cb, cc, cx, w, name):
    s, c = cb.shape
    taps = w.shape[0]
    return _simple_op(name, lambda b, c_, x, w: (b * _causal_conv(c_ * x, w),), (c // LANES,),
                      [_cols(s)] * 3 + [_cols(taps)], [(_sds((s, c)), _cols(s))], (cb, cc, cx, w), (0, 1, 2, 3))[0]


def gdn_pre(qkv, w, name):
    s, c = qkv.shape
    taps = w.shape[0]

    def f(x, w):
        j = pl.program_id(0)
        y = jax.nn.silu(_causal_conv(x, w))
        normed = y * lax.rsqrt(jnp.sum(y * y, axis=-1, keepdims=True) + EPS)
        scale = jnp.where(j < GDN_HEADS, GDN_HEAD_DIM ** -0.5, 1.0).astype(f32)
        return (jnp.where(j < 2 * GDN_HEADS, normed * scale, y),)

    return _simple_op(name, f, (c // LANES,), [_cols(s), _cols(taps)], [(_sds((s, c)), _cols(s))], (qkv, w), (0, 1))[0]


def gate_beta(ab, pv, name):
    s = ab.shape[0]

    def f(ab, pv):
        lane = lax.broadcasted_iota(jnp.int32, ab.shape, 1)
        g = -jnp.exp(pv[0:1, :]) * jax.nn.softplus(ab + pv[1:2, :])
        return (jnp.where(lane < GDN_HEADS, g, jnp.where(lane < 2 * GDN_HEADS, jax.nn.sigmoid(ab), 0.0)),)

    return _simple_op(name, f, (s // ROW_TILE,), [_rows(LANES), _whole((8, LANES))], [(_sds((s, LANES)), _rows(LANES))],
                      (ab, pv), (0, 1), {1: (0,)})[0]


def gdn_post(o, gate, w, name):
    s, c = o.shape

    def f(o, g, w):
        heads = [slice(hd * LANES, (hd + 1) * LANES) for hd in range(c // LANES)]
        return (jnp.concatenate([_rms(o[:, hd], w) * jax.nn.silu(g[:, hd]) for hd in heads], axis=1),)

    return _simple_op(name, f, (s // ROW_TILE,), [_rows(c), _rows(c), _whole((1, LANES))], [(_sds((s, c)), _rows(c))],
                      (o, gate, w), (0, 1, 2), {2: (0,)})[0]


def attn_merge(outs, lses, name):
    s, c = outs[0].shape

    def f(o1, o2, o3, l1, l2, l3):
        m = lax.stop_gradient(jnp.maximum(jnp.maximum(l1, l2), l3))
        e1, e2, e3 = jnp.exp(l1 - m), jnp.exp(l2 - m), jnp.exp(l3 - m)
        return ((e1 * o1 + e2 * o2 + e3 * o3) / (e1 + e2 + e3),)

    return _simple_op(name, f, (s // ROW_TILE,), [_rows(c)] * 6, [(_sds((s, c)), _rows(c))], (*outs, *lses), tuple(range(6)))[0]


def loss_rows(y, target, name):
    s, d = y.shape
    nt = s // ROW_TILE

    def f(y, t):
        e = y - t
        part = 0.5 * jnp.sum(jnp.mean(e * e, axis=-1, keepdims=True), axis=0, keepdims=True)
        return (jnp.broadcast_to(part * (1.0 / (8 * LANES)), (8, LANES)),)

    out = _simple_op(name, f, (nt,), [_rows(d)] * 2, [(_sds((nt * 8, LANES)), pl.BlockSpec((8, LANES), lambda i: (i, 0)))],
                     (y, target), (0,))[0]
    return jnp.sum(out)


def _mxu(a, b, form):
    dims = {"nn": ((1,), (0,)), "nt": ((1,), (1,)), "tn": ((0,), (0,))}

    def raw(a, b, form):
        return lax.dot_general(a.astype(bf16), b.astype(bf16), (dims[form], ((), ())), preferred_element_type=f32)

    @jax.custom_vjp
    def prod(a, b):
        return raw(a, b, form)

    def prod_b(res, ct):
        a, b = res
        if form == "nn":
            return raw(ct, b, "nt"), raw(a, ct, "tn")
        if form == "nt":
            return raw(ct, b, "nn"), raw(ct, a, "tn")
        return raw(b, ct, "nt"), raw(a, ct, "nn")

    prod.defvjp(lambda a, b: (raw(a, b, form), (a, b)), prod_b)
    return prod(a, b)


def _masked_heads_attention(q, keys, values, seen):
    dh = ATTN_HEAD_DIM
    outs, lses = [], []
    for hd in range(q.shape[1] // dh):
        at = slice(hd * dh, (hd + 1) * dh)
        sc = jnp.where(seen, _mxu(q[:, at], keys[:, at], "nt"), -jnp.inf)
        m = lax.stop_gradient(jnp.max(sc, axis=-1, keepdims=True))
        p = jnp.exp(sc - m)
        l = jnp.sum(p, axis=-1, keepdims=True)
        outs.append(_mxu(p / l, values[:, at], "nn"))
        lses.append(jnp.broadcast_to(m + jnp.log(l), (q.shape[0], dh)))
    return jnp.concatenate(outs, axis=1), jnp.concatenate(lses, axis=1)


def band_attention(q, k, v, nb, name):
    r, qb, width = q.shape

    def f(q, kp, kc, vp, vc):
        has_prev = (pl.program_id(0) % nb) > 0
        i = lax.broadcasted_iota(jnp.int32, (qb, 2 * qb), 0)
        j = lax.broadcasted_iota(jnp.int32, (qb, 2 * qb), 1)
        seen = jnp.logical_or(jnp.logical_and(jnp.logical_and(j < qb, j >= i), has_prev), jnp.logical_and(j >= qb, j - qb <= i))
        return _masked_heads_attention(q, jnp.concatenate([kp, kc], axis=0), jnp.concatenate([vp, vc], axis=0), seen)

    blk = (None, qb, width)
    cur = pl.BlockSpec(blk, lambda b: (b, 0, 0))
    prev = pl.BlockSpec(blk, lambda b: (jnp.maximum(b - 1, 0), 0, 0))
    shape = _sds((r, qb, width))
    fwd_call, bwd_call = _block_op(name, f, (r,), [cur, prev, cur, prev, cur], [(shape, cur), (shape, cur)],
                                   (q, k, k, v, v), (0, 1, 2, 3, 4), gdefs={1: (shape, cur), 3: (shape, cur)})

    def to_prev(g):
        return jnp.concatenate([g[1:], jnp.zeros_like(g[:1])], axis=0)

    @jax.custom_vjp
    def op(q, k, v):
        return tuple(fwd_call(q, k, k, v, v))

    def op_b(res, cts):
        q, k, v = res
        dq, dkp, dkc, dvp, dvc = bwd_call(q, k, k, v, v, *cts)
        return dq, dkc + to_prev(dkp), dvc + to_prev(dvp)

    op.defvjp(lambda q, k, v: (tuple(fwd_call(q, k, k, v, v)), (q, k, v)), op_b)
    return op(q, k, v)


def dilated_attention(q, k, v, name):
    s = q.shape[0]
    outs, lses = [], []
    for d in DILATIONS:
        length = s // d
        nb = length // QB
        def to_residue(t):
            return t.reshape(length, d, ATTN_WIDTH).transpose(1, 0, 2).reshape(d * nb, QB, ATTN_WIDTH)

        def from_residue(t):
            return t.reshape(d, length, ATTN_WIDTH).transpose(1, 0, 2).reshape(s, ATTN_WIDTH)

        o, lse = band_attention(to_residue(q), to_residue(k), to_residue(v), nb, f"{name}_d{d}")
        outs.append(from_residue(o))
        lses.append(from_residue(lse))
    return attn_merge(outs, lses, name + "_merge")


def cross_attention(q, kv, name):
    s = q.shape[0]
    m = kv.shape[0]
    width = XATTN_HEADS * XATTN_HEAD_DIM
    tq = 512

    def f(q, k, v):
        sc = _mxu(q, k, "nt") * (XATTN_HEAD_DIM ** -0.5)
        mx = lax.stop_gradient(jnp.max(sc, axis=-1, keepdims=True))
        p = jnp.exp(sc - mx)
        return (_mxu(p / jnp.sum(p, axis=-1, keepdims=True), v, "nn"),)

    q_spec = pl.BlockSpec((tq, XATTN_HEAD_DIM), lambda a, i: (i, a))
    k_spec = pl.BlockSpec((m, XATTN_HEAD_DIM), lambda a, i: (0, a))
    v_spec = pl.BlockSpec((m, XATTN_HEAD_DIM), lambda a, i: (0, a + XATTN_HEADS))
    half = _sds((m, width))
    fwd_call, bwd_call = _block_op(name, f, (XATTN_HEADS, s // tq), [q_spec, k_spec, v_spec], [(_sds((s, width)), q_spec)],
                                   (q, kv, kv), (0, 1, 2), acc={1: (1,), 2: (1,)}, gdefs={1: (half, k_spec), 2: (half, k_spec)})

    @jax.custom_vjp
    def op(q, kv):
        return fwd_call(q, kv, kv)[0]

    def op_b(res, ct):
        q, kv = res
        dq, dk, dv = bwd_call(q, kv, kv, ct)
        return dq, jnp.concatenate([dk, dv], axis=1)

    op.defvjp(lambda q, kv: (fwd_call(q, kv, kv)[0], (q, kv)), op_b)
    return op(q, kv)


def _hi(a, b, form="nn"):
    dims = {"nn": ((1,), (0,)), "nt": ((1,), (1,)), "tn": ((0,), (0,))}[form]
    return lax.dot_general(a, b, (dims, ((), ())), precision=lax.Precision.HIGH, preferred_element_type=f32)


def _running_sum(g):
    def raw(x, form):
        c = x.shape[0]
        tri = (lax.broadcasted_iota(jnp.int32, (c, c), 0) >= lax.broadcasted_iota(jnp.int32, (c, c), 1)).astype(bf16)
        hi = x.astype(bf16)
        rest = x - hi.astype(f32)
        mid = rest.astype(bf16)
        low = (rest - mid.astype(f32)).astype(bf16)
        dims = (((1,) if form == "nn" else (0,), (0,)), ((), ()))
        return sum(lax.dot_general(tri, part, dims, preferred_element_type=f32) for part in (hi, mid, low))

    @jax.custom_vjp
    def run(x):
        return raw(x, "nn")

    run.defvjp(lambda x: (raw(x, "nn"), None), lambda _, ct: (raw(ct, "tn"),))
    return run(g)


def _unit_lower_inverse(a):
    c = a.shape[0]
    eye = (lax.broadcasted_iota(jnp.int32, (c, c), 0) == lax.broadcasted_iota(jnp.int32, (c, c), 1)).astype(f32)
    inv, power = eye - a, -a
    for _ in range(c.bit_length() - 2):
        power = _hi(power, power)
        inv = inv + _hi(inv, power)
    return inv


def _known_inverse(a, t):
    @jax.custom_vjp
    def inv(a, t):
        return t

    def inv_b(t, ct):
        return -_hi(_hi(t, ct, "tn"), t, "nt"), jnp.zeros_like(t)

    inv.defvjp(lambda a, t: (t, t), inv_b)
    return inv(a, t)


def _delta_chunk(q, k, v, g, beta, s0, known_inv=None):
    c = q.shape[0]
    i = lax.broadcasted_iota(jnp.int32, (c, c), 0)
    j = lax.broadcasted_iota(jnp.int32, (c, c), 1)
    causal, strict = i >= j, i > j
    dec = _running_sum(g)
    dec_i = dec[:, :c]
    rel = jnp.exp(jnp.where(causal, dec_i - dec_i.T, -jnp.inf))
    k_beta = k * beta
    on_k = _mxu(jnp.concatenate([k_beta, q], axis=0), k, "nt")
    a = jnp.where(strict, on_k[:c] * rel, 0.0)
    attn = jnp.where(causal, on_k[c:] * rel, 0.0)
    inv = _unit_lower_inverse(a) if known_inv is None else _known_inverse(a, known_inv)
    e_dec = jnp.exp(dec)
    solved = _hi(inv, jnp.concatenate([v * beta, k_beta * e_dec], axis=1))
    u, w = solved[:, :v.shape[1]], solved[:, v.shape[1]:]
    total = jnp.sum(g, axis=0, keepdims=True)
    on_state = _mxu(jnp.concatenate([w, q * e_dec], axis=0), s0, "nn")
    v_new = u - on_state[:c]
    o = on_state[c:] + _mxu(attn, v_new, "nn")
    s1 = s0 * jnp.exp(total) + _mxu(k * jnp.exp(total - dec), v_new, "tn")
    return o, s1, inv


def _delta_rule_call(name, walk, n, in_specs, out_specs, out_shape, operands, exchange):
    n_in, n_out = len(in_specs), len(out_specs)
    carried = len(exchange.operands) if exchange else 0

    def body(*refs):
        ins, refs = refs[:n_in], refs[n_in:]
        x_refs, refs = refs[:carried], refs[carried:]
        outs, refs = refs[:n_out], refs[n_out:]
        land_refs, (state, *sems) = refs[:carried], refs[carried:]
        step = pl.program_id(0)
        if exchange:
            start, finish = exchange.bind(x_refs, land_refs, sems)
            pl.when(step == 0)(start)

        @pl.when(step == 0)
        def _():
            state[...] = jnp.zeros_like(state)

        walk(ins, outs, state)
        if exchange:
            pl.when(step == n - 1)(finish)

    return pl.pallas_call(
        body, name=name, grid=(n,), in_specs=list(in_specs) + [ANY] * carried, out_specs=list(out_specs) + [ANY] * carried,
        out_shape=list(out_shape) + (exchange.out_shapes if exchange else []),
        scratch_shapes=[pltpu.VMEM((GDN_HEAD_DIM, GDN_WIDTH), f32)] + (exchange.scratch if exchange else []),
        compiler_params=_params(1))(*operands, *(exchange.operands if exchange else []))


def _delta_heads():
    heads = [slice(hd * GDN_HEAD_DIM, (hd + 1) * GDN_HEAD_DIM) for hd in range(GDN_HEADS)]
    inv_at = [slice(hd * GDN_CHUNK, (hd + 1) * GDN_CHUNK) for hd in range(GDN_HEADS)]
    return heads, inv_at


def _head_chunk(q, k, v, gates, s0, head, known_inv=None):
    g = jnp.broadcast_to(gates[:, head:head + 1], q.shape)
    beta = jnp.broadcast_to(gates[:, GDN_HEADS + head:GDN_HEADS + head + 1], q.shape)
    return _delta_chunk(q, k, v, g, beta, s0, known_inv)


def delta_rule_fwd(q, k, v, gates, name, exchange=None):
    s, width = q.shape
    c, dk = GDN_CHUNK, GDN_HEAD_DIM
    n = s // c
    heads, inv_at = _delta_heads()

    def walk(ins, outs, state):
        q_ref, k_ref, v_ref, gates_ref = ins
        o_ref, s_in_ref, inv_ref = outs
        s_in_ref[...] = state[...]
        gates = gates_ref[...]
        xs = [[r[:, hd] for r in (q_ref, k_ref, v_ref)] + [gates, state[:, hd], i] for i, hd in enumerate(heads)]
        ys = [_head_chunk(*x) for x in xs]
        for hd, at, (o, s1, inv) in zip(heads, inv_at, ys):
            o_ref[:, hd], state[:, hd], inv_ref[:, at] = o, s1, inv

    blk = pl.BlockSpec((c, width), lambda t: (t, 0))
    gt = pl.BlockSpec((c, LANES), lambda t: (t, 0))
    st = pl.BlockSpec((dk, width), lambda t: (t, 0))
    iv = pl.BlockSpec((c, GDN_HEADS * c), lambda t: (t, 0))
    return _delta_rule_call(name, walk, n, [blk] * 3 + [gt], [blk, st, iv],
                            [_sds((s, width)), _sds((n * dk, width)), _sds((s, GDN_HEADS * c))], (q, k, v, gates), exchange)


def delta_rule_bwd(q, k, v, gates, s_in, inv, do, name, exchange=None):
    s, width = q.shape
    c, dk = GDN_CHUNK, GDN_HEAD_DIM
    n = s // c
    heads, inv_at = _delta_heads()

    def walk(ins, outs, dstate):
        q_ref, k_ref, v_ref, gates_ref, s_ref, inv_ref, do_ref = ins
        dq_ref, dk_ref, dv_ref, dgates_ref = outs
        gates = gates_ref[...]
        xs = [[r[:, hd] for r in (q_ref, k_ref, v_ref)] + [gates, s_ref[:, hd]] for hd in heads]
        known = [inv_ref[:, at] for at in inv_at]
        cts = [(do_ref[:, hd], dstate[:, hd]) for hd in heads]
        grads = []
        for i, (x, t, ct) in enumerate(zip(xs, known, cts)):
            _, vjp = jax.vjp(lambda *y, t=t, i=i: _head_chunk(*y, i, known_inv=t)[:2], *x)
            grads.append(vjp(ct))
        dgates = grads[0][3]
        for g in grads[1:]:
            dgates = dgates + g[3]
        dgates_ref[...] = dgates
        for hd, (dq, dk_, dv, _, ds0) in zip(heads, grads):
            dq_ref[:, hd], dk_ref[:, hd], dv_ref[:, hd], dstate[:, hd] = dq, dk_, dv, ds0

    blk = pl.BlockSpec((c, width), lambda t: (n - 1 - t, 0))
    gt = pl.BlockSpec((c, LANES), lambda t: (n - 1 - t, 0))
    st = pl.BlockSpec((dk, width), lambda t: (n - 1 - t, 0))
    iv = pl.BlockSpec((c, GDN_HEADS * c), lambda t: (n - 1 - t, 0))
    return _delta_rule_call(name, walk, n, [blk] * 3 + [gt, st, iv, blk], [blk] * 3 + [gt],
                            [_sds((s, width))] * 3 + [_sds((s, LANES))], (q, k, v, gates, s_in, inv, do), exchange)


def adamw(w, g, m, v, name):
    shape = w.shape
    if len(shape) == 2:
        grid, spec = (1,), pl.BlockSpec(shape, lambda i: (0, 0))
    else:
        tile = shape[1] if shape[1] <= 512 else _pick(shape[1], (512, 256, 128))
        grid, spec = (shape[0], shape[1] // tile), pl.BlockSpec((None, tile, shape[2]), lambda layer, i: (layer, i, 0))

    def body(w_ref, g_ref, m_ref, v_ref, d_ref, nm_ref, nv_ref):
        grad = g_ref[...]
        nm = ADAM_B1 * m_ref[...] + (1.0 - ADAM_B1) * grad
        nv = ADAM_B2 * v_ref[...] + (1.0 - ADAM_B2) * (grad * grad)
        m_hat = nm / (1.0 - ADAM_B1 ** ADAM_STEP)
        v_hat = nv / (1.0 - ADAM_B2 ** ADAM_STEP)
        d_ref[...] = -ADAM_LR * (m_hat / (jnp.sqrt(v_hat) + ADAM_EPS) + ADAM_WD * w_ref[...])
        nm_ref[...] = nm
        nv_ref[...] = nv

    return tuple(pl.pallas_call(body, name=name, grid=grid, in_specs=[spec] * 4, out_specs=[spec] * 3,
                                out_shape=[_sds(shape)] * 3, compiler_params=_params(len(grid)))(w, g, m, v))


def _place():
    return lax.axis_index("x"), lax.axis_index("y"), lax.axis_index("c")


def _flip(p, bits):
    return tuple(1 - v if (bits >> s) & 1 else v for v, s in zip(p, (2, 1, 0)))


def _slot(p):
    return 4 * p[0] + 2 * p[1] + p[2]


def _chip_of(p):
    return 2 * p[0] + p[1]


ANY = pl.BlockSpec(memory_space=pl.ANY)


class Gather:
    scratch = (pltpu.SemaphoreType.DMA((7,)), pltpu.SemaphoreType.DMA((7,)), pltpu.SemaphoreType.DMA)

    def __init__(self, shard):
        self.operand = shard
        self.out_shape = jax.ShapeDtypeStruct((N_DEV,) + shard.shape, shard.dtype)

    def bind(self, x_ref, out_ref, send_sems, recv_sems, local_sem):
        me = _place()
        sibling = _flip(me, 1)
        chips = [_flip(me, 4), _flip(me, 2), _flip(me, 6)]

        def copy(k, block, to, src=None):
            return pltpu.make_async_remote_copy(
                src_ref=out_ref.at[_slot(block)] if src is None else src, dst_ref=out_ref.at[_slot(block)],
                send_sem=send_sems.at[k], recv_sem=recv_sems.at[k], device_id=to, device_id_type=MESH)

        mine = pltpu.make_async_copy(x_ref, out_ref.at[_slot(me)], local_sem)
        first = [copy(0, me, sibling, src=x_ref)] + [copy(1 + j, me, chip, src=x_ref) for j, chip in enumerate(chips)]
        passed = [copy(4 + j, chip, sibling) for j, chip in enumerate(chips)]

        def start():
            mine.start()
            for cp in first:
                cp.start()

        def finish():
            for j, chip in enumerate(chips):
                copy(1 + j, chip, me).wait_recv()
                passed[j].start()
            copy(0, sibling, me).wait_recv()
            for j, chip in enumerate(chips):
                copy(4 + j, _flip(chip, 1), me).wait_recv()
            for cp in first + passed:
                cp.wait_send()
            mine.wait()

        return start, finish


class ChipExchange:
    scratch = (pltpu.SemaphoreType.DMA((3,)), pltpu.SemaphoreType.DMA((3,)), pltpu.SemaphoreType.DMA)

    def __init__(self, blocks):
        self.operand = blocks
        self.out_shape = jax.ShapeDtypeStruct(blocks.shape, blocks.dtype)

    def bind(self, x_ref, out_ref, send_sems, recv_sems, local_sem):
        me = _place()
        peers = [_flip(me, 4), _flip(me, 2), _flip(me, 6)]
        mine = pltpu.make_async_copy(x_ref.at[_chip_of(me)], out_ref.at[_chip_of(me)], local_sem)

        def copy(j, src_chip, dst_chip):
            return pltpu.make_async_remote_copy(
                src_ref=x_ref.at[src_chip], dst_ref=out_ref.at[dst_chip], send_sem=send_sems.at[j],
                recv_sem=recv_sems.at[j], device_id=peers[j], device_id_type=MESH)

        sends = [copy(j, _chip_of(peer), _chip_of(me)) for j, peer in enumerate(peers)]

        def start():
            mine.start()
            for cp in sends:
                cp.start()

        def finish():
            for j, peer in enumerate(peers):
                copy(j, _chip_of(me), _chip_of(peer)).wait_recv()
            for cp in sends:
                cp.wait_send()
            mine.wait()

        return start, finish


class Together:
    def __init__(self, *parts):
        self.parts = parts
        self.operands = [p.operand for p in parts]
        self.out_shapes = [p.out_shape for p in parts]
        self.scratch = [s for p in parts for s in p.scratch]

    def bind(self, x_refs, out_refs, sems):
        bound, at = [], 0
        for p, x_ref, out_ref in zip(self.parts, x_refs, out_refs):
            bound.append(p.bind(x_ref, out_ref, *sems[at:at + len(p.scratch)]))
            at += len(p.scratch)

        def start():
            for s, _ in bound:
                s()

        def finish():
            for _, f in bound:
                f()

        return start, finish


def exchange_alone(exchange, name):
    n = len(exchange.operands)

    def body(*refs):
        start, finish = exchange.bind(refs[:n], refs[n:2 * n], refs[2 * n:])
        start()
        finish()

    return pl.pallas_call(body, name=name, out_shape=exchange.out_shapes, in_specs=[ANY] * n, out_specs=[ANY] * n,
                          scratch_shapes=exchange.scratch)(*exchange.operands)


def _row_tile(rows):
    return max([t for t in range(16, min(rows, 1024) + 1, 16) if rows % t == 0] or [rows])


def pair_exchange(blocks, name):
    n = len(blocks)

    def body(*refs):
        x_refs, theirs_refs, (send_sems, recv_sems) = refs[:n], refs[n:2 * n], refs[2 * n:]
        me = _place()
        remote = [pltpu.make_async_remote_copy(
            src_ref=x_refs[t].at[2 * q + 1 - me[2]], dst_ref=theirs_refs[t].at[q], send_sem=send_sems.at[4 * t + q],
            recv_sem=recv_sems.at[4 * t + q], device_id=_flip(me, 1), device_id_type=MESH) for t in range(n) for q in range(4)]
        for cp in remote:
            cp.start()
        for cp in remote:
            cp.wait()

    return pl.pallas_call(
        body, name=name, out_shape=[jax.ShapeDtypeStruct((4,) + b.shape[1:], b.dtype) for b in blocks], in_specs=[ANY] * n,
        out_specs=[ANY] * n, scratch_shapes=[pltpu.SemaphoreType.DMA((4 * n,)), pltpu.SemaphoreType.DMA((4 * n,))])(*blocks)


def pair_add(blocks, theirs, name):
    n, rows, width = theirs.shape
    tile = _row_tile(rows)
    spec = pl.BlockSpec((None, tile, width), lambda q, i: (q, i, 0))
    south = pl.BlockSpec((None, None, tile, width), lambda q, i: (q, 0, i, 0))
    north = pl.BlockSpec((None, None, tile, width), lambda q, i: (q, 1, i, 0))

    def body(s_ref, n_ref, b_ref, o_ref):
        mine = jnp.where(lax.axis_index("c") == 0, s_ref[...], n_ref[...])
        o_ref[...] = (mine.astype(f32) + b_ref[...].astype(f32)).astype(o_ref.dtype)

    by_core = blocks.reshape(n, 2, rows, width)
    return pl.pallas_call(body, name=name, grid=(n, rows // tile), in_specs=[south, north, spec], out_specs=spec,
                          out_shape=jax.ShapeDtypeStruct(theirs.shape, theirs.dtype), compiler_params=_params(2))(by_core, by_core, theirs)


def sum_slots(blocks, name):
    n, rows, width = blocks.shape
    tile = _row_tile(rows)

    def body(x_ref, o_ref):
        total = x_ref[0].astype(f32)
        for s in range(1, n):
            total = total + x_ref[s].astype(f32)
        o_ref[...] = total

    return pl.pallas_call(
        body, name=name, grid=(rows // tile,), in_specs=[pl.BlockSpec((n, tile, width), lambda i: (0, i, 0))],
        out_specs=pl.BlockSpec((tile, width), lambda i: (i, 0)), out_shape=_sds((rows, width)), compiler_params=_params(1))(blocks)


def all_reduce_small(x, name):
    rows, width = x.shape

    def body(x_ref, o_ref, land, send_sems, recv_sems):
        me = _place()
        copies = []
        for k in range(1, N_DEV):
            peer = _flip(me, k)
            copies.append(pltpu.make_async_remote_copy(
                src_ref=x_ref, dst_ref=land.at[_slot(me)], send_sem=send_sems.at[k - 1], recv_sem=recv_sems.at[k - 1],
                device_id=peer, device_id_type=MESH))
        for cp in copies:
            cp.start()
        land[_slot(me)] = x_ref[...]
        for k in range(1, N_DEV):
            peer = _flip(me, k)
            pltpu.make_async_remote_copy(
                src_ref=x_ref, dst_ref=land.at[_slot(peer)], send_sem=send_sems.at[k - 1], recv_sem=recv_sems.at[k - 1],
                device_id=peer, device_id_type=MESH).wait_recv()
        total = land[0]
        for s in range(1, N_DEV):
            total = total + land[s]
        o_ref[...] = total
        for cp in copies:
            cp.wait_send()

    return pl.pallas_call(
        body, name=name, out_shape=_sds((rows, width)), in_specs=[pl.BlockSpec(memory_space=pltpu.VMEM)],
        out_specs=pl.BlockSpec(memory_space=pltpu.VMEM),
        scratch_shapes=[pltpu.VMEM((N_DEV, rows, width), f32), pltpu.SemaphoreType.DMA((7,)), pltpu.SemaphoreType.DMA((7,))],
    )(x)


def _pack_big(shards):
    packed = {name: shards[name].astype(bf16) for name in COL_SHARDED}
    packed["rows"] = jnp.concatenate([shards[name].astype(bf16) for name, _ in ROW_SHARDED], axis=1)
    return packed


def _unpack_gathered(gathered):
    full = {}
    for name, part in gathered.items():
        if name in ("w_in", "w_gate_up"):
            full[name] = part
        elif name in COL_SHARDED:
            full[name] = part.transpose(1, 0, 2).reshape(D_MODEL, N_DEV * part.shape[2])
        else:
            at = 0
            for weight, rows in ROW_SHARDED:
                full[weight] = part[:, at:at + rows, :].reshape(N_DEV * rows, D_MODEL)
                at += rows
    return full


def _pack_grads(grads, group):
    packed = {}
    for name in group:
        if name in ("w_in", "w_gate_up"):
            packed[name] = grads[name]
        elif name == "rows":
            packed[name] = jnp.concatenate([grads[weight].reshape(N_DEV, rows, D_MODEL) for weight, rows in ROW_SHARDED], axis=1)
        else:
            packed[name] = grads[name].reshape(D_MODEL, N_DEV, grads[name].shape[1] // N_DEV).transpose(1, 0, 2)
    return packed


def _unpack_shard(layers):
    out = {name: jnp.stack([layer[name] for layer in layers]) for name in COL_SHARDED}
    rows_pack, at = jnp.stack([layer["rows"] for layer in layers]), 0
    for weight, rows in ROW_SHARDED:
        out[weight] = rows_pack[:, at:at + rows, :]
        at += rows
    return out


def _rows_of(flat_len):
    return -(-flat_len // (8 * D_MODEL)) * 8


def _pack_small(parts):
    flat = jnp.concatenate([p.reshape(-1) for p in parts])
    rows = _rows_of(flat.shape[0])
    flat = jnp.pad(flat, (0, rows * D_MODEL - flat.shape[0]))
    return flat.reshape(rows, D_MODEL)


def _unpack_small(packed, like):
    flat, out, at = packed.reshape(-1), [], 0
    for p in like:
        out.append(flat[at:at + p.size].reshape(p.shape))
        at += p.size
    return out


def _rope_tables(positions):
    inv_freq = jnp.float32(ROPE_THETA) ** (-jnp.arange(0, ROPE_DIM, 2, dtype=f32) / ROPE_DIM)
    ang = positions.astype(f32)[:, None] * inv_freq
    cos, sin = jnp.cos(ang), jnp.sin(ang)
    rest = ATTN_HEAD_DIM - ROPE_DIM
    cos_h = jnp.concatenate([cos, cos, jnp.ones((cos.shape[0], rest), f32)], axis=1)
    sin_h = jnp.concatenate([-sin, sin, jnp.zeros((sin.shape[0], rest), f32)], axis=1)
    return jnp.tile(cos_h, (1, ATTN_HEADS)), jnp.tile(sin_h, (1, ATTN_HEADS))


HEAD_SMALL = ("norm_mix_pre", "conv_short", "conv_gdn", "gdn_a_log", "gdn_dt_bias")


def _layer_head(h, p, cos_t, sin_t):
    hn = rms_norm(h, p["norm_mix_pre"][None], "norm_mix_pre")
    aw, cw, gw = ATTN_WIDTH, CONV_WIDTH, GDN_WIDTH
    aq, ak, av, cb, cc, cx, gqkv, ab, gate = _blocked_linear_split(
        hn, p["w_in"], (aw, aw, aw, cw, cw, cw, 3 * gw, 2 * GDN_HEADS, gw), "w_in")
    ab = jnp.pad(ab, ((0, 0), (0, LANES - 2 * GDN_HEADS)))
    y_attn = dilated_attention(rope(aq, cos_t, sin_t, ATTN_HEAD_DIM ** -0.5, "rope_q"), rope(ak, cos_t, sin_t, 1.0, "rope_k"),
                               av, "attn")
    y_conv = short_conv(cb, cc, cx, p["conv_short"], "short_conv")
    qkv = gdn_pre(gqkv, p["conv_gdn"], "gdn_pre")
    pv = jnp.zeros((8, LANES), f32).at[0, :GDN_HEADS].set(p["gdn_a_log"]).at[1, :GDN_HEADS].set(p["gdn_dt_bias"])
    return (*_split_cols(qkv, (gw, gw, gw)), gate_beta(ab, pv, "gate_beta")), (gate, y_attn, y_conv)


MID_PARAMS = ("gdn_norm", "w_out", "norm_mix_post", "norm_xattn_pre", "w_xq", "norm_mem", "w_xkv", "w_xo", "norm_xattn_post",
              "norm_ffn_pre")


def _layer_mid(h, o, gate, y_attn, y_conv, p, mem):
    y_gdn = gdn_post(o, gate, p["gdn_norm"][None], "gdn_post")
    mix = _linear(jnp.concatenate([y_attn, y_conv, y_gdn], axis=1), p["w_out"], "w_out")
    h, hn = add_norm_then_norm(h, mix, p["norm_mix_post"][None], p["norm_xattn_pre"][None], "norm_mix_xattn")
    qx = _linear(hn, p["w_xq"], "w_xq")
    kv = _linear(rms_norm(mem, p["norm_mem"][None], "norm_mem"), p["w_xkv"], "w_xkv")
    xa = _linear(cross_attention(qx, kv, "xattn"), p["w_xo"], "w_xo")
    return add_norm_then_norm(h, xa, p["norm_xattn_post"][None], p["norm_ffn_pre"][None], "norm_xattn_ffn")


def _pair_summed(grads, group, name):
    blocks = _pack_grads(grads, group)
    theirs = pair_exchange([blocks[n] for n in group], name + "_pair_exchange")
    return [pair_add(blocks[n], t, f"{name}_pair_add_{n}") for n, t in zip(group, theirs)]


def _forward_backward(x, packed, small, mem, cos_t, sin_t, target):
    def gathers(group, layer):
        return [Gather(packed[n][layer]) for n in group]

    h = x
    head_gathered = exchange_alone(Together(*gathers(HEAD_GROUP, 0)), "gather_first")
    saved = []
    for layer in range(DEPTH):
        at_layer = {n: t[layer] for n, t in small.items()}
        head_p = {**_unpack_gathered(dict(zip(HEAD_GROUP, head_gathered))), **{n: at_layer[n] for n in HEAD_SMALL}}
        (rule_in, rest), head_vjp = jax.vjp(lambda h, hp: _layer_head(h, hp, cos_t, sin_t), h, head_p)
        carried = gathers(TAIL_GROUP, layer) + (gathers(HEAD_GROUP, layer + 1) if layer + 1 < DEPTH else [])
        o, s_in, inv, *landed = delta_rule_fwd(*rule_in, "delta_rule_fwd", Together(*carried))
        head_gathered = landed[len(TAIL_GROUP):]
        tail_p = {**_unpack_gathered(dict(zip(TAIL_GROUP, landed))), **at_layer}
        mid_p = {n: tail_p[n] for n in MID_PARAMS}
        (h, hn), mid_vjp = jax.vjp(lambda h, o, rest, mp: _layer_mid(h, o, *rest, mp, mem), h, o, rest, mid_p)
        y, ffn_saved = ffn_forward(hn, tail_p["w_gate_up"], tail_p["w_down"], "ffn")
        h, last_vjp = jax.vjp(lambda h, y, w: add_norm(h, y, w[None], "norm_ffn_post"), h, y, tail_p["norm_ffn_post"])
        saved.append((head_vjp, mid_vjp, last_vjp, ffn_saved, rule_in, s_in, inv))

    loss, dh = jax.value_and_grad(lambda y: loss_rows(y, target, "loss"))(h)

    def summed(group, landed):
        return {n: sum_slots(t, "sum_grads_" + n) for n, t in zip(group, landed)}

    big_grads, small_grads, head_pending = [{} for _ in range(DEPTH)], [None] * DEPTH, []
    for layer in reversed(range(DEPTH)):
        head_vjp, mid_vjp, last_vjp, ffn_saved, rule_in, s_in, inv = saved[layer]
        dh, dy, d_norm_ffn_post = last_vjp(dh)
        dhn, d_gate_up, d_down, landed = ffn_backward(
            ffn_saved, dy, "ffn", Together(*[ChipExchange(t) for t in head_pending]) if head_pending else None)
        if head_pending:
            big_grads[layer + 1].update(summed(HEAD_GROUP, landed))
        dh_mid, do, d_rest, d_mid_p = mid_vjp((dh, dhn))
        d_tail_p = {**d_mid_p, "w_gate_up": d_gate_up, "w_down": d_down, "norm_ffn_post": d_norm_ffn_post}
        carried = Together(*[ChipExchange(t) for t in _pair_summed(d_tail_p, TAIL_GROUP, "tail")])
        *d_rule_in, = delta_rule_bwd(*rule_in, s_in, inv, do, "delta_rule_bwd", carried)
        big_grads[layer].update(summed(TAIL_GROUP, d_rule_in[4:]))
        dh_head, d_head_p = head_vjp((tuple(d_rule_in[:4]), d_rest))
        dh = dh_mid + dh_head
        small_grads[layer] = {n: t for n, t in {**d_head_p, **d_tail_p}.items() if n in small}
        head_pending = _pair_summed(d_head_p, HEAD_GROUP, "head")
    landed = exchange_alone(Together(*[ChipExchange(t) for t in head_pending]), "exchange_last")
    big_grads[0].update(summed(HEAD_GROUP, landed))
    return loss, dh, big_grads, small_grads


def kernel(x, mem, positions, norm_mix_pre, norm_mix_post, w_in, conv_short, conv_gdn, gdn_a_log, gdn_dt_bias, gdn_norm, w_out, norm_mem, norm_xattn_pre, norm_xattn_post, w_xq, w_xkv, w_xo, norm_ffn_pre, norm_ffn_post, w_gate_up, w_down, loss_target, m_norm_mix_pre, m_norm_mix_post, m_w_in, m_conv_short, m_conv_gdn, m_gdn_a_log, m_gdn_dt_bias, m_gdn_norm, m_w_out, m_norm_mem, m_norm_xattn_pre, m_norm_xattn_post, m_w_xq, m_w_xkv, m_w_xo, m_norm_ffn_pre, m_norm_ffn_post, m_w_gate_up, m_w_down, v_norm_mix_pre, v_norm_mix_post, v_w_in, v_conv_short, v_conv_gdn, v_gdn_a_log, v_gdn_dt_bias, v_gdn_norm, v_w_out, v_norm_mem, v_norm_xattn_pre, v_norm_xattn_post, v_w_xq, v_w_xkv, v_w_xo, v_norm_ffn_pre, v_norm_ffn_post, v_w_gate_up, v_w_down):
    given = dict(locals())
    weights = {n: given[n] for n in WEIGHTS}
    me = _slot(_place())

    def in_place(shard):
        full = jnp.zeros(shard.shape[:-1] + (shard.shape[-1] * N_DEV,), f32)
        return lax.dynamic_update_slice_in_dim(full, shard, me * shard.shape[-1], axis=shard.ndim - 1)

    placed = [in_place(conv_short), in_place(conv_gdn)]
    conv_short_full, conv_gdn_full = _unpack_small(all_reduce_small(_pack_small(placed), "gather_conv"), placed)
    small = {n: weights[n] for n in NORMS + ("gdn_a_log", "gdn_dt_bias", "gdn_norm")}
    small["conv_short"], small["conv_gdn"] = conv_short_full, conv_gdn_full

    cos_t, sin_t = _rope_tables(positions[0])
    loss, grad_x, big_layers, small_layers = _forward_backward(
        x[0], _pack_big(weights), small, mem[0], cos_t, sin_t, loss_target[0])
    grads = _unpack_shard(big_layers)

    names = sorted(small)
    parts = [jnp.stack([layer[n] for layer in small_layers]) for n in names] + [loss.reshape(1)]
    reduced = _unpack_small(all_reduce_small(_pack_small(parts), "reduce_small"), parts)
    loss = reduced[-1][0]
    for n, g in zip(names, reduced[:-1]):
        if n in ("conv_short", "conv_gdn"):
            width = weights[n].shape[-1]
            g = lax.dynamic_slice_in_dim(g, me * width, width, axis=g.ndim - 1)
        grads[n] = g

    delta, new_m, new_v = {}, {}, {}
    for n in WEIGHTS:
        delta[n], new_m[n], new_v[n] = adamw(weights[n], grads[n], given["m_" + n], given["v_" + n], "adamw_" + n)
    return (loss, grad_x[None], *[grads[n] for n in WEIGHTS], *[delta[n] for n in WEIGHTS],
            *[new_m[n] for n in WEIGHTS], *[new_v[n] for n in WEIGHTS])
```

```python
import functools

import jax
import jax.numpy as jnp
from jax import lax
from jax.experimental import pallas as pl
from jax.experimental.pallas import tpu as pltpu

f32 = jnp.float32
bf16 = jnp.bfloat16
MESH = pl.DeviceIdType.MESH

N_DEV = 8
DEPTH = 4
D_MODEL = 1024
EPS = 1e-6
ATTN_HEADS, ATTN_HEAD_DIM = 4, 64
ATTN_WIDTH = ATTN_HEADS * ATTN_HEAD_DIM
DILATIONS = (1, 4, 16)
QB = 128
ROPE_THETA = 500000.0
ROPE_DIM = ATTN_HEAD_DIM // 4
CONV_WIDTH = 256
GDN_HEADS, GDN_HEAD_DIM = 4, 128
GDN_WIDTH = GDN_HEADS * GDN_HEAD_DIM
GDN_CHUNK = 64
XATTN_HEADS, XATTN_HEAD_DIM = 4, 256
LANES = 128
ROW_TILE = 512
VMEM_LIMIT = 56 * 1024 * 1024

ADAM_LR, ADAM_B1, ADAM_B2, ADAM_EPS, ADAM_WD, ADAM_STEP = 0.001, 0.9, 0.999, 1e-08, 0.01, 10

COL_SHARDED = ("w_in", "w_xkv", "w_gate_up")
ROW_SHARDED = (("w_out", 128), ("w_xq", 128), ("w_xo", 128), ("w_down", 352))
HEAD_GROUP = ("w_in",)
TAIL_GROUP = ("w_gate_up", "w_xkv", "rows")
NORMS = ("norm_mix_pre", "norm_mix_post", "norm_mem", "norm_xattn_pre", "norm_xattn_post", "norm_ffn_pre", "norm_ffn_post")
WEIGHTS = ("norm_mix_pre", "norm_mix_post", "w_in", "conv_short", "conv_gdn", "gdn_a_log", "gdn_dt_bias", "gdn_norm", "w_out",
           "norm_mem", "norm_xattn_pre", "norm_xattn_post", "w_xq", "w_xkv", "w_xo", "norm_ffn_pre", "norm_ffn_post",
           "w_gate_up", "w_down")


def _params(n_grid):
    return pltpu.CompilerParams(dimension_semantics=("arbitrary",) * n_grid, vmem_limit_bytes=VMEM_LIMIT)


def _pick(n, cands):
    for c in cands:
        if n % c == 0:
            return c
    return n


MXU_FLOPS = 9.0e14
HBM_BYTES_PER_S = 2.5e12
VMEM_RMW_BYTES_PER_S = 7.0e12
STEP_S = 0.4e-6
MATMUL_VMEM = 44 * 1024 * 1024


def _tiles(m, n, k, sa, sb, so):
    def divisors(d):
        return sorted({d} | {d // s for s in range(1, d // LANES + 1) if d % s == 0 and (d // s) % LANES == 0}, reverse=True)

    best = None
    for tk in divisors(k):
        nk = k // tk
        for tm in divisors(m):
            for tn_ in divisors(n):
                per_step = tm * tk * sa + tk * tn_ * sb + tm * tn_ * so
                vmem = 2 * per_step + (tm * tn_ * 4 if nk > 1 else 0)
                vmem += (tm * tk * 2 if sa == 4 else 0) + (tk * tn_ * 2 if sb == 4 else 0) + tm * tn_ * 4
                if vmem > MATMUL_VMEM:
                    continue
                moved = m * k * sa * (1 if nk == 1 else n // tn_) + k * n * sb * (1 if nk == 1 and n == tn_ else m // tm) + m * n * so
                busy = 2 * m * n * k / MXU_FLOPS + (m * n * 8 * nk / VMEM_RMW_BYTES_PER_S if nk > 1 else 0)
                cost = max(moved / HBM_BYTES_PER_S, busy) + per_step / HBM_BYTES_PER_S + (m // tm) * (n // tn_) * nk * STEP_S
                if best is None or cost < best[0]:
                    best = (cost, tm, tn_, tk)
    return best[1:]


def _mm(a, b, ta, tb, out_dtype, name):
    m, k = (a.shape[1], a.shape[0]) if ta else a.shape
    n = b.shape[0] if tb else b.shape[1]
    tm, tn, tk = _tiles(m, n, k, a.dtype.itemsize, b.dtype.itemsize, jnp.dtype(out_dtype).itemsize)
    nk = k // tk
    a_spec = pl.BlockSpec((tk, tm), lambda i, j, kk: (kk, i)) if ta else pl.BlockSpec((tm, tk), lambda i, j, kk: (i, kk))
    b_spec = pl.BlockSpec((tn, tk), lambda i, j, kk: (j, kk)) if tb else pl.BlockSpec((tk, tn), lambda i, j, kk: (kk, j))
    dims = (((0 if ta else 1,), (1 if tb else 0,)), ((), ()))

    def body(a_ref, b_ref, o_ref, *acc):
        kk = pl.program_id(2)
        p = lax.dot_general(a_ref[...].astype(bf16), b_ref[...].astype(bf16), dims, preferred_element_type=f32)
        if nk == 1:
            o_ref[...] = p.astype(o_ref.dtype)
            return
        acc_ref, = acc

        @pl.when(kk == 0)
        def _():
            acc_ref[...] = p

        @pl.when(kk > 0)
        def _():
            acc_ref[...] += p

        @pl.when(kk == nk - 1)
        def _():
            o_ref[...] = acc_ref[...].astype(o_ref.dtype)

    return pl.pallas_call(
        body, name=name, grid=(m // tm, n // tn, nk), in_specs=[a_spec, b_spec],
        out_specs=pl.BlockSpec((tm, tn), lambda i, j, kk: (i, j)), out_shape=jax.ShapeDtypeStruct((m, n), out_dtype),
        scratch_shapes=[pltpu.VMEM((tm, tn), f32)] if nk > 1 else [], compiler_params=_params(3))(a, b)


def _linear(x, w, name):
    @jax.custom_vjp
    def lin(x, w):
        return _mm(x, w, False, False, f32, name + "_y")

    def lin_f(x, w):
        return _mm(x, w, False, False, f32, name + "_y"), (x, w)

    def lin_b(res, dy):
        x, w = res
        return _mm(dy, w, False, True, f32, name + "_dx"), _mm(x, dy, True, False, bf16, name + "_dw")

    lin.defvjp(lin_f, lin_b)
    return lin(x, w)


def _bdot(a, b, form):
    dims = {"nn": ((1,), (0,)), "nt": ((1,), (1,)), "tn": ((0,), (0,))}[form]
    return lax.dot_general(a.astype(bf16), b.astype(bf16), (dims, ((), ())), preferred_element_type=f32)


def ffn_forward(hn, w_gate_up, w_down, name):
    s, k = hn.shape
    n_blocks, _, width = w_gate_up.shape
    half = n_blocks // 2
    tm = 512
    blocked = jax.ShapeDtypeStruct((half, s, width), bf16)

    def act_body(x_ref, wg_ref, wu_ref, gate_ref, up_ref, act_ref):
        x = x_ref[...]
        gate, up = _bdot(x, wg_ref[...], "nn"), _bdot(x, wu_ref[...], "nn")
        gate_ref[...], up_ref[...] = gate.astype(bf16), up.astype(bf16)
        act_ref[...] = (jax.nn.silu(gate) * up).astype(bf16)

    tile = pl.BlockSpec((None, tm, width), lambda i, d: (d, i, 0))
    gate, up, act = pl.pallas_call(
        act_body, name=name + "_act", grid=(s // tm, half),
        in_specs=[pl.BlockSpec((tm, k), lambda i, d: (i, 0)), pl.BlockSpec((None, k, width), lambda i, d: (d, 0, 0)),
                  pl.BlockSpec((None, k, width), lambda i, d: (d + half, 0, 0))],
        out_specs=[tile] * 3, out_shape=[blocked] * 3, compiler_params=_params(2))(hn, w_gate_up, w_gate_up)

    n = w_down.shape[1]
    tn = 512

    def y_body(act_ref, w_ref, y_ref):
        y_ref[...] = sum(_bdot(act_ref[d], w_ref[d * width:(d + 1) * width, :], "nn") for d in range(half))

    y = pl.pallas_call(
        y_body, name=name + "_y", grid=(s // tm, n // tn),
        in_specs=[pl.BlockSpec((half, tm, width), lambda i, j: (0, i, 0)), pl.BlockSpec((half * width, tn), lambda i, j: (0, j))],
        out_specs=pl.BlockSpec((tm, tn), lambda i, j: (i, j)), out_shape=_sds((s, n)), compiler_params=_params(2))(act, w_down)
    return y, (hn, w_gate_up, w_down, gate, up, act)


def ffn_backward(saved, dy, name, exchange=None):
    hn, w_gate_up, w_down, gate, up, act = saved
    s, k = hn.shape
    n_blocks, _, width = w_gate_up.shape
    half = n_blocks // 2
    n = w_down.shape[1]
    tm = 512
    blocked = jax.ShapeDtypeStruct((half, s, width), bf16)
    carried = len(exchange.operands) if exchange else 0
    steps = (s // tm, half)

    def dact_body(dy_ref, w_ref, gate_ref, up_ref, *refs):
        x_refs, refs = refs[:carried], refs[carried:]
        (dgate_ref, dup_ref), refs = refs[:2], refs[2:]
        if exchange:
            at = pl.program_id(0) * steps[1] + pl.program_id(1)
            start, wait = exchange.bind(x_refs, refs[:carried], refs[carried:])
            pl.when(at == 0)(start)
        d_act = _bdot(dy_ref[...], w_ref[...], "nt")
        g, u = gate_ref[...].astype(f32), up_ref[...].astype(f32)
        sig = jax.nn.sigmoid(g)
        dgate_ref[...] = (d_act * u * sig * (1.0 + g * (1.0 - sig))).astype(bf16)
        dup_ref[...] = (d_act * g * sig).astype(bf16)
        if exchange:
            pl.when(at == steps[0] * steps[1] - 1)(wait)

    tile = pl.BlockSpec((None, tm, width), lambda i, d: (d, i, 0))
    d_gate, d_up, *landed = pl.pallas_call(
        dact_body, name=name + "_dact", grid=steps,
        in_specs=[pl.BlockSpec((tm, n), lambda i, d: (i, 0)), pl.BlockSpec((width, n), lambda i, d: (d, 0)), tile, tile] + [ANY] * carried,
        out_specs=[tile, tile] + [ANY] * carried, out_shape=[blocked, blocked] + (exchange.out_shapes if exchange else []),
        scratch_shapes=exchange.scratch if exchange else [],
        compiler_params=_params(2))(dy, w_down, gate, up, *(exchange.operands if exchange else []))

    def dx_body(dg_ref, du_ref, w_ref, dx_ref):
        dx_ref[...] = sum(_bdot(dg_ref[d], w_ref[d], "nt") + _bdot(du_ref[d], w_ref[d + half], "nt") for d in range(half))

    tx = 256
    rows = pl.BlockSpec((half, tx, width), lambda i: (0, i, 0))
    dx = pl.pallas_call(
        dx_body, name=name + "_dx", grid=(s // tx,), in_specs=[rows, rows, _whole(w_gate_up.shape)],
        out_specs=pl.BlockSpec((tx, k), lambda i: (i, 0)), out_shape=_sds((s, k)), compiler_params=_params(1))(d_gate, d_up, w_gate_up)

    def dw1_body(x_ref, dg_ref, du_ref, dw_ref):
        d_block = jnp.where(pl.program_id(0) < half, dg_ref[...], du_ref[...])
        dw_ref[...] = _bdot(x_ref[...], d_block, "tn").astype(bf16)

    d_w_gate_up = pl.pallas_call(
        dw1_body, name=name + "_dw1", grid=(n_blocks,),
        in_specs=[_whole((s, k)), pl.BlockSpec((None, s, width), lambda b: (jnp.minimum(b, half - 1), 0, 0)),
                  pl.BlockSpec((None, s, width), lambda b: (jnp.maximum(b - half, 0), 0, 0))],
        out_specs=pl.BlockSpec((None, k, width), lambda b: (b, 0, 0)), out_shape=jax.ShapeDtypeStruct(w_gate_up.shape, bf16),
        compiler_params=_params(1))(hn, d_gate, d_up)

    tn = 512

    def dw2_body(act_ref, dy_ref, dw_ref):
        dw_ref[...] = _bdot(act_ref[...], dy_ref[...], "tn").astype(bf16)

    d_w_down = pl.pallas_call(
        dw2_body, name=name + "_dw2", grid=(half, n // tn),
        in_specs=[pl.BlockSpec((None, s, width), lambda d, j: (d, 0, 0)), pl.BlockSpec((s, tn), lambda d, j: (0, j))],
        out_specs=pl.BlockSpec((width, tn), lambda d, j: (d, j)), out_shape=jax.ShapeDtypeStruct(w_down.shape, bf16),
        compiler_params=_params(2))(act, dy)
    return dx, d_w_gate_up, d_w_down, landed


def _joined_linear(parts, w, name):
    s, n = parts[0].shape[0], w.shape[1]
    widths = [p.shape[1] for p in parts]
    edges = [sum(widths[:i]) for i in range(len(widths) + 1)]
    spans = list(zip(edges[:-1], edges[1:]))
    tm, tn = 512, 512

    def forward(*args):
        *xs, w = args

        def y_body(*refs):
            *x_refs, w_ref, y_ref = refs
            y_ref[...] = sum(_bdot(x_ref[...], w_ref[a:b, :], "nn") for x_ref, (a, b) in zip(x_refs, spans))

        y = pl.pallas_call(
            y_body, name=name + "_y", grid=(s // tm, n // tn),
            in_specs=[pl.BlockSpec((tm, k), lambda i, j: (i, 0)) for k in widths] + [pl.BlockSpec((edges[-1], tn), lambda i, j: (0, j))],
            out_specs=pl.BlockSpec((tm, tn), lambda i, j: (i, j)), out_shape=_sds((s, n)), compiler_params=_params(2))(*xs, w)
        return y, args

    def backward(args, dy):
        *xs, w = args

        def dx_body(dy_ref, w_ref, *dx_refs):
            d_all = _bdot(dy_ref[...], w_ref[...], "nt")
            for dx_ref, (a, b) in zip(dx_refs, spans):
                dx_ref[...] = d_all[:, a:b]

        dxs = pl.pallas_call(
            dx_body, name=name + "_dx", grid=(s // tm,), in_specs=[pl.BlockSpec((tm, n), lambda i: (i, 0)), _whole(w.shape)],
            out_specs=[pl.BlockSpec((tm, k), lambda i: (i, 0)) for k in widths], out_shape=[_sds((s, k)) for k in widths],
            compiler_params=_params(1))(dy, w)

        def dw_body(*refs):
            *x_refs, dy_ref, dw_ref = refs
            dy_tile = dy_ref[...]
            dw_ref[...] = jnp.concatenate([_bdot(x_ref[...], dy_tile, "tn") for x_ref in x_refs], axis=0).astype(bf16)

        dw = pl.pallas_call(
            dw_body, name=name + "_dw", grid=(n // tn,),
            in_specs=[_whole((s, k)) for k in widths] + [pl.BlockSpec((s, tn), lambda j: (0, j))],
            out_specs=pl.BlockSpec((edges[-1], tn), lambda j: (0, j)), out_shape=jax.ShapeDtypeStruct(w.shape, bf16),
            compiler_params=_params(1))(*xs, dy)
        return (*dxs, dw)

    @jax.custom_vjp
    def op(*args):
        return forward(*args)[0]

    op.defvjp(forward, backward)
    return op(*parts, w)


def _split_cols(x, widths):
    edges = [sum(widths[:i]) for i in range(len(widths) + 1)]

    def cut(x):
        return tuple(x[:, a:b] for a, b in zip(edges[:-1], edges[1:]))

    @jax.custom_vjp
    def split(x):
        return cut(x)

    split.defvjp(lambda x: (cut(x), None), lambda _, cts: (jnp.concatenate(cts, axis=1),))
    return split(x)


def _block_op(name, f, grid, in_specs, out_defs, arrays, diff, acc=None, gdefs=None):
    acc, gdefs = acc or {}, gdefs or {}
    n_in, n_out, n_grid = len(in_specs), len(out_defs), len(grid)

    def fwd_call(*xs):
        def body(*refs):
            outs = f(*[r[...] for r in refs[:n_in]])
            for r, o in zip(refs[n_in:], outs):
                r[...] = o.astype(r.dtype)

        return pl.pallas_call(
            body, name=name + "_fwd", grid=grid, in_specs=in_specs, out_specs=[d[1] for d in out_defs],
            out_shape=[d[0] for d in out_defs], compiler_params=_params(n_grid))(*xs)

    def bwd_call(*xs_and_cts):
        def body(*refs):
            xs = [r[...] for r in refs[:n_in]]
            cts = tuple(r[...] for r in refs[n_in:n_in + n_out])

            def of_diff(*dx):
                full = list(xs)
                for i, v in zip(diff, dx):
                    full[i] = v
                return tuple(f(*full))

            _, vjp = jax.vjp(of_diff, *[xs[i] for i in diff])
            grads = vjp(cts)
            for i, g, r in zip(diff, grads, refs[n_in + n_out:]):
                if i in acc:
                    first = functools.reduce(jnp.logical_and, [pl.program_id(a) == 0 for a in acc[i]])

                    @pl.when(first)
                    def _(r=r):
                        r[...] = jnp.zeros_like(r)

                    r[...] += g.astype(r.dtype)
                else:
                    r[...] = g.astype(r.dtype)

        g_defs = [gdefs.get(i, (jax.ShapeDtypeStruct(arrays[i].shape, f32), in_specs[i])) for i in diff]
        return pl.pallas_call(
            body, name=name + "_bwd", grid=grid, in_specs=list(in_specs) + [d[1] for d in out_defs],
            out_specs=[d[1] for d in g_defs], out_shape=[d[0] for d in g_defs], compiler_params=_params(n_grid))(*xs_and_cts)

    return fwd_call, bwd_call


def _simple_op(name, f, grid, in_specs, out_defs, arrays, diff, acc=None):
    fwd_call, bwd_call = _block_op(name, f, grid, in_specs, out_defs, arrays, diff, acc)

    @jax.custom_vjp
    def op(*xs):
        return tuple(fwd_call(*xs))

    def op_f(*xs):
        return tuple(fwd_call(*xs)), xs

    def op_b(xs, cts):
        grads = bwd_call(*xs, *cts)
        out = [jnp.zeros_like(x) for x in xs]
        for i, g in zip(diff, grads):
            out[i] = g
        return tuple(out)

    op.defvjp(op_f, op_b)
    return op(*arrays)


def _rows(width, tile=ROW_TILE):
    return pl.BlockSpec((tile, width), lambda i: (i, 0))


def _whole(shape):
    return pl.BlockSpec(shape, lambda *_: (0,) * len(shape))


def _sds(shape):
    return jax.ShapeDtypeStruct(shape, f32)


def _rms(x, w):
    return x * lax.rsqrt(jnp.mean(x * x, axis=-1, keepdims=True) + EPS) * w


def rms_norm(x, w, name):
    r, d = x.shape
    tile = min(ROW_TILE, r)
    return _simple_op(name, lambda x, w: (_rms(x, w),), (r // tile,), [_rows(d, tile), _whole((1, d))],
                      [(_sds((r, d)), _rows(d, tile))], (x, w), (0, 1), {1: (0,)})[0]


def add_norm(h, y, w, name):
    r, d = h.shape
    return _simple_op(name, lambda h, y, w: (h + _rms(y, w),), (r // ROW_TILE,), [_rows(d), _rows(d), _whole((1, d))],
                      [(_sds((r, d)), _rows(d))], (h, y, w), (0, 1, 2), {2: (0,)})[0]


def add_norm_then_norm(h, y, w_post, w_pre, name):
    r, d = h.shape

    def f(h, y, w_post, w_pre):
        h_new = h + _rms(y, w_post)
        return h_new, _rms(h_new, w_pre)

    return _simple_op(name, f, (r // ROW_TILE,), [_rows(d), _rows(d), _whole((1, d)), _whole((1, d))],
                      [(_sds((r, d)), _rows(d))] * 2, (h, y, w_post, w_pre), (0, 1, 2, 3), {2: (0,), 3: (0,)})


def _swap8(x):
    def raw(x):
        lane = lax.broadcasted_iota(jnp.int32, x.shape, 1) % ATTN_HEAD_DIM
        half = ROPE_DIM // 2
        up = pltpu.roll(x, x.shape[1] - half, axis=1)
        down = pltpu.roll(x, half, axis=1)
        return jnp.where(lane < half, up, jnp.where(lane < ROPE_DIM, down, 0.0))

    @jax.custom_vjp
    def swap(x):
        return raw(x)

    swap.defvjp(lambda x: (raw(x), None), lambda _, g: (raw(g),))
    return swap(x)


def rope(x, cos_t, sin_t, scale, name):
    r, d = x.shape
    return _simple_op(name, lambda x, c, s: ((x * c + _swap8(x) * s) * scale,), (r // ROW_TILE,), [_rows(d)] * 3,
                      [(_sds((r, d)), _rows(d))], (x, cos_t, sin_t), (0,))[0]


def _shift_rows(x, k):
    n = x.shape[0]

    def down(x):
        row = lax.broadcasted_iota(jnp.int32, x.shape, 0)
        return jnp.where(row >= k, pltpu.roll(x, k, axis=0), 0.0)

    def up(x):
        row = lax.broadcasted_iota(jnp.int32, x.shape, 0)
        return jnp.where(row < n - k, pltpu.roll(x, n - k, axis=0), 0.0)

    @jax.custom_vjp
    def shift(x):
        return down(x)

    shift.defvjp(lambda x: (down(x), None), lambda _, g: (up(g),))
    return shift(x)


def _causal_conv(x, w):
    taps = w.shape[0]
    y = x * w[taps - 1:taps, :]
    for j in range(taps - 1):
        y = y + _shift_rows(x, taps - 1 - j) * w[j:j + 1, :]
    return y


def _cols(rows, at=0):
    return pl.BlockSpec((rows, LANES), lambda j: (0, at + j))


def short_conv(cb, cc, cx, w, name):
    s, c = cb.shape
    taps = w.shape[0]
    return _simple_op(name, lambda b, c_, x, w: (b * _causal_conv(c_ * x, w),), (c // LANES,),
                      [_cols(s)] * 3 + [_cols(taps)], [(_sds((s, c)), _cols(s))], (cb, cc, cx, w), (0, 1, 2, 3))[0]


def gdn_pre(qkv, w, name):
    s, c = qkv.shape
    taps = w.shape[0]

    def f(x, w):
        j = pl.program_id(0)
        y = jax.nn.silu(_causal_conv(x, w))
        normed = y * lax.rsqrt(jnp.sum(y * y, axis=-1, keepdims=True) + EPS)
        scale = jnp.where(j < GDN_HEADS, GDN_HEAD_DIM ** -0.5, 1.0).astype(f32)
        return (jnp.where(j < 2 * GDN_HEADS, normed * scale, y),)

    return _simple_op(name, f, (c // LANES,), [_cols(s), _cols(taps)], [(_sds((s, c)), _cols(s))], (qkv, w), (0, 1))[0]


def gate_beta(ab, pv, name):
    s = ab.shape[0]

    def f(ab, pv):
        lane = lax.broadcasted_iota(jnp.int32, ab.shape, 1)
        g = -jnp.exp(pv[0:1, :]) * jax.nn.softplus(ab + pv[1:2, :])
        return (jnp.where(lane < GDN_HEADS, g, jnp.where(lane < 2 * GDN_HEADS, jax.nn.sigmoid(ab), 0.0)),)

    return _simple_op(name, f, (s // ROW_TILE,), [_rows(LANES), _whole((8, LANES))], [(_sds((s, LANES)), _rows(LANES))],
                      (ab, pv), (0, 1), {1: (0,)})[0]


def gdn_post(o, gate, w, name):
    s, c = o.shape

    def f(o, g, w):
        heads = [slice(hd * LANES, (hd + 1) * LANES) for hd in range(c // LANES)]
        return (jnp.concatenate([_rms(o[:, hd], w) * jax.nn.silu(g[:, hd]) for hd in heads], axis=1),)

    return _simple_op(name, f, (s // ROW_TILE,), [_rows(c), _rows(c), _whole((1, LANES))], [(_sds((s, c)), _rows(c))],
                      (o, gate, w), (0, 1, 2), {2: (0,)})[0]


def attn_merge(outs, lses, name):
    s, c = outs[0].shape

    def f(o1, o2, o3, l1, l2, l3):
        m = lax.stop_gradient(jnp.maximum(jnp.maximum(l1, l2), l3))
        e1, e2, e3 = jnp.exp(l1 - m), jnp.exp(l2 - m), jnp.exp(l3 - m)
        return ((e1 * o1 + e2 * o2 + e3 * o3) / (e1 + e2 + e3),)

    return _simple_op(name, f, (s // ROW_TILE,), [_rows(c)] * 6, [(_sds((s, c)), _rows(c))], (*outs, *lses), tuple(range(6)))[0]


def loss_rows(y, target, name):
    s, d = y.shape
    nt = s // ROW_TILE

    def f(y, t):
        e = y - t
        part = 0.5 * jnp.sum(jnp.mean(e * e, axis=-1, keepdims=True), axis=0, keepdims=True)
        return (jnp.broadcast_to(part * (1.0 / (8 * LANES)), (8, LANES)),)

    out = _simple_op(name, f, (nt,), [_rows(d)] * 2, [(_sds((nt * 8, LANES)), pl.BlockSpec((8, LANES), lambda i: (i, 0)))],
                     (y, target), (0,))[0]
    return jnp.sum(out)


def _mxu(a, b, form):
    dims = {"nn": ((1,), (0,)), "nt": ((1,), (1,)), "tn": ((0,), (0,))}

    def raw(a, b, form):
        return lax.dot_general(a.astype(bf16), b.astype(bf16), (dims[form], ((), ())), preferred_element_type=f32)

    @jax.custom_vjp
    def prod(a, b):
        return raw(a, b, form)

    def prod_b(res, ct):
        a, b = res
        if form == "nn":
            return raw(ct, b, "nt"), raw(a, ct, "tn")
        if form == "nt":
            return raw(ct, b, "nn"), raw(ct, a, "tn")
        return raw(b, ct, "nt"), raw(a, ct, "nn")

    prod.defvjp(lambda a, b: (raw(a, b, form), (a, b)), prod_b)
    return prod(a, b)


def _masked_heads_attention(q, keys, values, seen):
    dh = ATTN_HEAD_DIM
    outs, lses = [], []
    for hd in range(q.shape[1] // dh):
        at = slice(hd * dh, (hd + 1) * dh)
        sc = jnp.where(seen, _mxu(q[:, at], keys[:, at], "nt"), -jnp.inf)
        m = lax.stop_gradient(jnp.max(sc, axis=-1, keepdims=True))
        p = jnp.exp(sc - m)
        l = jnp.sum(p, axis=-1, keepdims=True)
        outs.append(_mxu(p / l, values[:, at], "nn"))
        lses.append(jnp.broadcast_to(m + jnp.log(l), (q.shape[0], dh)))
    return jnp.concatenate(outs, axis=1), jnp.concatenate(lses, axis=1)


def band_attention(q, k, v, nb, name):
    r, qb, width = q.shape

    def f(q, kp, kc, vp, vc):
        has_prev = (pl.program_id(0) % nb) > 0
        i = lax.broadcasted_iota(jnp.int32, (qb, 2 * qb), 0)
        j = lax.broadcasted_iota(jnp.int32, (qb, 2 * qb), 1)
        seen = jnp.logical_or(jnp.logical_and(jnp.logical_and(j < qb, j >= i), has_prev), jnp.logical_and(j >= qb, j - qb <= i))
        return _masked_heads_attention(q, jnp.concatenate([kp, kc], axis=0), jnp.concatenate([vp, vc], axis=0), seen)

    blk = (None, qb, width)
    cur = pl.BlockSpec(blk, lambda b: (b, 0, 0))
    prev = pl.BlockSpec(blk, lambda b: (jnp.maximum(b - 1, 0), 0, 0))
    shape = _sds((r, qb, width))
    fwd_call, bwd_call = _block_op(name, f, (r,), [cur, prev, cur, prev, cur], [(shape, cur), (shape, cur)],
                                   (q, k, k, v, v), (0, 1, 2, 3, 4), gdefs={1: (shape, cur), 3: (shape, cur)})

    def to_prev(g):
        return jnp.concatenate([g[1:], jnp.zeros_like(g[:1])], axis=0)

    @jax.custom_vjp
    def op(q, k, v):
        return tuple(fwd_call(q, k, k, v, v))

    def op_b(res, cts):
        q, k, v = res
        dq, dkp, dkc, dvp, dvc = bwd_call(q, k, k, v, v, *cts)
        return dq, dkc + to_prev(dkp), dvc + to_prev(dvp)

    op.defvjp(lambda q, k, v: (tuple(fwd_call(q, k, k, v, v)), (q, k, v)), op_b)
    return op(q, k, v)


def dilated_attention(q, k, v, name):
    s = q.shape[0]
    outs, lses = [], []
    for d in DILATIONS:
        length = s // d
        nb = length // QB
        def to_residue(t):
            return t.reshape(length, d, ATTN_WIDTH).transpose(1, 0, 2).reshape(d * nb, QB, ATTN_WIDTH)

        def from_residue(t):
            return t.reshape(d, length, ATTN_WIDTH).transpose(1, 0, 2).reshape(s, ATTN_WIDTH)

        o, lse = band_attention(to_residue(q), to_residue(k), to_residue(v), nb, f"{name}_d{d}")
        outs.append(from_residue(o))
        lses.append(from_residue(lse))
    return attn_merge(outs, lses, name + "_merge")


def cross_attention(q, kv, name):
    s = q.shape[0]
    m = kv.shape[0]
    width = XATTN_HEADS * XATTN_HEAD_DIM
    tq = 512

    def f(q, k, v):
        sc = _mxu(q, k, "nt") * (XATTN_HEAD_DIM ** -0.5)
        mx = lax.stop_gradient(jnp.max(sc, axis=-1, keepdims=True))
        p = jnp.exp(sc - mx)
        return (_mxu(p / jnp.sum(p, axis=-1, keepdims=True), v, "nn"),)

    q_spec = pl.BlockSpec((tq, XATTN_HEAD_DIM), lambda a, i: (i, a))
    k_spec = pl.BlockSpec((m, XATTN_HEAD_DIM), lambda a, i: (0, a))
    v_spec = pl.BlockSpec((m, XATTN_HEAD_DIM), lambda a, i: (0, a + XATTN_HEADS))
    half = _sds((m, width))
    fwd_call, bwd_call = _block_op(name, f, (XATTN_HEADS, s // tq), [q_spec, k_spec, v_spec], [(_sds((s, width)), q_spec)],
                                   (q, kv, kv), (0, 1, 2), acc={1: (1,), 2: (1,)}, gdefs={1: (half, k_spec), 2: (half, k_spec)})

    @jax.custom_vjp
    def op(q, kv):
        return fwd_call(q, kv, kv)[0]

    def op_b(res, ct):
        q, kv = res
        dq, dk, dv = bwd_call(q, kv, kv, ct)
        return dq, jnp.concatenate([dk, dv], axis=1)

    op.defvjp(lambda q, kv: (fwd_call(q, kv, kv)[0], (q, kv)), op_b)
    return op(q, kv)


def _hi(a, b, form="nn"):
    dims = {"nn": ((1,), (0,)), "nt": ((1,), (1,)), "tn": ((0,), (0,))}[form]
    return lax.dot_general(a, b, (dims, ((), ())), precision=lax.Precision.HIGH, preferred_element_type=f32)


def _running_sum(g):
    def raw(x, form):
        c = x.shape[0]
        tri = (lax.broadcasted_iota(jnp.int32, (c, c), 0) >= lax.broadcasted_iota(jnp.int32, (c, c), 1)).astype(bf16)
        hi = x.astype(bf16)
        rest = x - hi.astype(f32)
        mid = rest.astype(bf16)
        low = (rest - mid.astype(f32)).astype(bf16)
        dims = (((1,) if form == "nn" else (0,), (0,)), ((), ()))
        return sum(lax.dot_general(tri, part, dims, preferred_element_type=f32) for part in (hi, mid, low))

    @jax.custom_vjp
    def run(x):
        return raw(x, "nn")

    run.defvjp(lambda x: (raw(x, "nn"), None), lambda _, ct: (raw(ct, "tn"),))
    return run(g)


def _unit_lower_inverse(a):
    c = a.shape[0]
    eye = (lax.broadcasted_iota(jnp.int32, (c, c), 0) == lax.broadcasted_iota(jnp.int32, (c, c), 1)).astype(f32)
    inv, power = eye - a, -a
    for _ in range(c.bit_length() - 2):
        power = _hi(power, power)
        inv = inv + _hi(inv, power)
    return inv


def _known_inverse(a, t):
    @jax.custom_vjp
    def inv(a, t):
        return t

    def inv_b(t, ct):
        return -_hi(_hi(t, ct, "tn"), t, "nt"), jnp.zeros_like(t)

    inv.defvjp(lambda a, t: (t, t), inv_b)
    return inv(a, t)


def _delta_chunk(q, k, v, g, beta, s0, known_inv=None):
    c = q.shape[0]
    i = lax.broadcasted_iota(jnp.int32, (c, c), 0)
    j = lax.broadcasted_iota(jnp.int32, (c, c), 1)
    causal, strict = i >= j, i > j
    dec = _running_sum(g)
    dec_i = dec[:, :c]
    rel = jnp.exp(jnp.where(causal, dec_i - dec_i.T, -jnp.inf))
    k_beta = k * beta
    on_k = _mxu(jnp.concatenate([k_beta, q], axis=0), k, "nt")
    a = jnp.where(strict, on_k[:c] * rel, 0.0)
    attn = jnp.where(causal, on_k[c:] * rel, 0.0)
    inv = _unit_lower_inverse(a) if known_inv is None else _known_inverse(a, known_inv)
    e_dec = jnp.exp(dec)
    solved = _hi(inv, jnp.concatenate([v * beta, k_beta * e_dec], axis=1))
    u, w = solved[:, :v.shape[1]], solved[:, v.shape[1]:]
    total = jnp.sum(g, axis=0, keepdims=True)
    on_state = _mxu(jnp.concatenate([w, q * e_dec], axis=0), s0, "nn")
    v_new = u - on_state[:c]
    o = on_state[c:] + _mxu(attn, v_new, "nn")
    s1 = s0 * jnp.exp(total) + _mxu(k * jnp.exp(total - dec), v_new, "tn")
    return o, s1, inv


def _delta_rule_call(name, walk, n, in_specs, out_specs, out_shape, operands, exchange):
    n_in, n_out = len(in_specs), len(out_specs)
    carried = len(exchange.operands) if exchange else 0

    def body(*refs):
        ins, refs = refs[:n_in], refs[n_in:]
        x_refs, refs = refs[:carried], refs[carried:]
        outs, refs = refs[:n_out], refs[n_out:]
        land_refs, (state, *sems) = refs[:carried], refs[carried:]
        step = pl.program_id(0)
        if exchange:
            start, finish = exchange.bind(x_refs, land_refs, sems)
            pl.when(step == 0)(start)

        @pl.when(step == 0)
        def _():
            state[...] = jnp.zeros_like(state)

        walk(ins, outs, state)
        if exchange:
            pl.when(step == n - 1)(finish)

    return pl.pallas_call(
        body, name=name, grid=(n,), in_specs=list(in_specs) + [ANY] * carried, out_specs=list(out_specs) + [ANY] * carried,
        out_shape=list(out_shape) + (exchange.out_shapes if exchange else []),
        scratch_shapes=[pltpu.VMEM((GDN_HEAD_DIM, GDN_WIDTH), f32)] + (exchange.scratch if exchange else []),
        compiler_params=_params(1))(*operands, *(exchange.operands if exchange else []))


def _delta_heads():
    heads = [slice(hd * GDN_HEAD_DIM, (hd + 1) * GDN_HEAD_DIM) for hd in range(GDN_HEADS)]
    inv_at = [slice(hd * GDN_CHUNK, (hd + 1) * GDN_CHUNK) for hd in range(GDN_HEADS)]
    return heads, inv_at


def _head_chunk(q, k, v, gates, s0, head, known_inv=None):
    g = jnp.broadcast_to(gates[:, head:head + 1], q.shape)
    beta = jnp.broadcast_to(gates[:, GDN_HEADS + head:GDN_HEADS + head + 1], q.shape)
    return _delta_chunk(q, k, v, g, beta, s0, known_inv)


def delta_rule_fwd(q, k, v, gates, name, exchange=None):
    s, width = q.shape
    c, dk = GDN_CHUNK, GDN_HEAD_DIM
    n = s // c
    heads, inv_at = _delta_heads()

    def walk(ins, outs, state):
        q_ref, k_ref, v_ref, gates_ref = ins
        o_ref, s_in_ref, inv_ref = outs
        s_in_ref[...] = state[...]
        gates = gates_ref[...]
        xs = [[r[:, hd] for r in (q_ref, k_ref, v_ref)] + [gates, state[:, hd], i] for i, hd in enumerate(heads)]
        ys = [_head_chunk(*x) for x in xs]
        for hd, at, (o, s1, inv) in zip(heads, inv_at, ys):
            o_ref[:, hd], state[:, hd], inv_ref[:, at] = o, s1, inv

    blk = pl.BlockSpec((c, width), lambda t: (t, 0))
    gt = pl.BlockSpec((c, LANES), lambda t: (t, 0))
    st = pl.BlockSpec((dk, width), lambda t: (t, 0))
    iv = pl.BlockSpec((c, GDN_HEADS * c), lambda t: (t, 0))
    return _delta_rule_call(name, walk, n, [blk] * 3 + [gt], [blk, st, iv],
                            [_sds((s, width)), _sds((n * dk, width)), _sds((s, GDN_HEADS * c))], (q, k, v, gates), exchange)


def delta_rule_bwd(q, k, v, gates, s_in, inv, do, name, exchange=None):
    s, width = q.shape
    c, dk = GDN_CHUNK, GDN_HEAD_DIM
    n = s // c
    heads, inv_at = _delta_heads()

    def walk(ins, outs, dstate):
        q_ref, k_ref, v_ref, gates_ref, s_ref, inv_ref, do_ref = ins
        dq_ref, dk_ref, dv_ref, dgates_ref = outs
        gates = gates_ref[...]
        xs = [[r[:, hd] for r in (q_ref, k_ref, v_ref)] + [gates, s_ref[:, hd]] for hd in heads]
        known = [inv_ref[:, at] for at in inv_at]
        cts = [(do_ref[:, hd], dstate[:, hd]) for hd in heads]
        grads = []
        for i, (x, t, ct) in enumerate(zip(xs, known, cts)):
            _, vjp = jax.vjp(lambda *y, t=t, i=i: _head_chunk(*y, i, known_inv=t)[:2], *x)
            grads.append(vjp(ct))
        dgates = grads[0][3]
        for g in grads[1:]:
            dgates = dgates + g[3]
        dgates_ref[...] = dgates
        for hd, (dq, dk_, dv, _, ds0) in zip(heads, grads):
            dq_ref[:, hd], dk_ref[:, hd], dv_ref[:, hd], dstate[:, hd] = dq, dk_, dv, ds0

    blk = pl.BlockSpec((c, width), lambda t: (n - 1 - t, 0))
    gt = pl.BlockSpec((c, LANES), lambda t: (n - 1 - t, 0))
    st = pl.BlockSpec((dk, width), lambda t: (n - 1 - t, 0))
    iv = pl.BlockSpec((c, GDN_HEADS * c), lambda t: (n - 1 - t, 0))
    return _delta_rule_call(name, walk, n, [blk] * 3 + [gt, st, iv, blk], [blk] * 3 + [gt],
                            [_sds((s, width))] * 3 + [_sds((s, LANES))], (q, k, v, gates, s_in, inv, do), exchange)


def adamw(w, g, m, v, name):
    shape = w.shape
    if len(shape) == 2:
        grid, spec = (1,), pl.BlockSpec(shape, lambda i: (0, 0))
    else:
        tile = shape[1] if shape[1] <= 512 else _pick(shape[1], (512, 256, 128))
        grid, spec = (shape[0], shape[1] // tile), pl.BlockSpec((None, tile, shape[2]), lambda layer, i: (layer, i, 0))

    def body(w_ref, g_ref, m_ref, v_ref, d_ref, nm_ref, nv_ref):
        grad = g_ref[...]
        nm = ADAM_B1 * m_ref[...] + (1.0 - ADAM_B1) * grad
        nv = ADAM_B2 * v_ref[...] + (1.0 - ADAM_B2) * (grad * grad)
        m_hat = nm / (1.0 - ADAM_B1 ** ADAM_STEP)
        v_hat = nv / (1.0 - ADAM_B2 ** ADAM_STEP)
        d_ref[...] = -ADAM_LR * (m_hat / (jnp.sqrt(v_hat) + ADAM_EPS) + ADAM_WD * w_ref[...])
        nm_ref[...] = nm
        nv_ref[...] = nv

    return tuple(pl.pallas_call(body, name=name, grid=grid, in_specs=[spec] * 4, out_specs=[spec] * 3,
                                out_shape=[_sds(shape)] * 3, compiler_params=_params(len(grid)))(w, g, m, v))


def _place():
    return lax.axis_index("x"), lax.axis_index("y"), lax.axis_index("c")


def _flip(p, bits):
    return tuple(1 - v if (bits >> s) & 1 else v for v, s in zip(p, (2, 1, 0)))


def _slot(p):
    return 4 * p[0] + 2 * p[1] + p[2]


def _chip_of(p):
    return 2 * p[0] + p[1]


ANY = pl.BlockSpec(memory_space=pl.ANY)


class Gather:
    scratch = (pltpu.SemaphoreType.DMA((7,)), pltpu.SemaphoreType.DMA((7,)), pltpu.SemaphoreType.DMA)

    def __init__(self, shard):
        self.operand = shard
        self.out_shape = jax.ShapeDtypeStruct((N_DEV,) + shard.shape, shard.dtype)

    def bind(self, x_ref, out_ref, send_sems, recv_sems, local_sem):
        me = _place()
        sibling = _flip(me, 1)
        chips = [_flip(me, 4), _flip(me, 2), _flip(me, 6)]

        def copy(k, block, to, src=None):
            return pltpu.make_async_remote_copy(
                src_ref=out_ref.at[_slot(block)] if src is None else src, dst_ref=out_ref.at[_slot(block)],
                send_sem=send_sems.at[k], recv_sem=recv_sems.at[k], device_id=to, device_id_type=MESH)

        mine = pltpu.make_async_copy(x_ref, out_ref.at[_slot(me)], local_sem)
        first = [copy(0, me, sibling, src=x_ref)] + [copy(1 + j, me, chip, src=x_ref) for j, chip in enumerate(chips)]
        passed = [copy(4 + j, chip, sibling) for j, chip in enumerate(chips)]

        def start():
            mine.start()
            for cp in first:
                cp.start()

        def finish():
            for j, chip in enumerate(chips):
                copy(1 + j, chip, me).wait_recv()
                passed[j].start()
            copy(0, sibling, me).wait_recv()
            for j, chip in enumerate(chips):
                copy(4 + j, _flip(chip, 1), me).wait_recv()
            for cp in first + passed:
                cp.wait_send()
            mine.wait()

        return start, finish


class ChipExchange:
    scratch = (pltpu.SemaphoreType.DMA((3,)), pltpu.SemaphoreType.DMA((3,)), pltpu.SemaphoreType.DMA)

    def __init__(self, blocks):
        self.operand = blocks
        self.out_shape = jax.ShapeDtypeStruct(blocks.shape, blocks.dtype)

    def bind(self, x_ref, out_ref, send_sems, recv_sems, local_sem):
        me = _place()
        peers = [_flip(me, 4), _flip(me, 2), _flip(me, 6)]
        mine = pltpu.make_async_copy(x_ref.at[_chip_of(me)], out_ref.at[_chip_of(me)], local_sem)

        def copy(j, src_chip, dst_chip):
            return pltpu.make_async_remote_copy(
                src_ref=x_ref.at[src_chip], dst_ref=out_ref.at[dst_chip], send_sem=send_sems.at[j],
                recv_sem=recv_sems.at[j], device_id=peers[j], device_id_type=MESH)

        sends = [copy(j, _chip_of(peer), _chip_of(me)) for j, peer in enumerate(peers)]

        def start():
            mine.start()
            for cp in sends:
                cp.start()

        def finish():
            for j, peer in enumerate(peers):
                copy(j, _chip_of(me), _chip_of(peer)).wait_recv()
            for cp in sends:
                cp.wait_send()
            mine.wait()

        return start, finish


class Together:
    def __init__(self, *parts):
        self.parts = parts
        self.operands = [p.operand for p in parts]
        self.out_shapes = [p.out_shape for p in parts]
        self.scratch = [s for p in parts for s in p.scratch]

    def bind(self, x_refs, out_refs, sems):
        bound, at = [], 0
        for p, x_ref, out_ref in zip(self.parts, x_refs, out_refs):
            bound.append(p.bind(x_ref, out_ref, *sems[at:at + len(p.scratch)]))
            at += len(p.scratch)

        def start():
            for s, _ in bound:
                s()

        def finish():
            for _, f in bound:
                f()

        return start, finish


def exchange_alone(exchange, name):
    n = len(exchange.operands)

    def body(*refs):
        start, finish = exchange.bind(refs[:n], refs[n:2 * n], refs[2 * n:])
        start()
        finish()

    return pl.pallas_call(body, name=name, out_shape=exchange.out_shapes, in_specs=[ANY] * n, out_specs=[ANY] * n,
                          scratch_shapes=exchange.scratch)(*exchange.operands)


def _row_tile(rows):
    return max([t for t in range(16, min(rows, 1024) + 1, 16) if rows % t == 0] or [rows])


def pair_exchange(blocks, name):
    n = len(blocks)

    def body(*refs):
        x_refs, theirs_refs, (send_sems, recv_sems) = refs[:n], refs[n:2 * n], refs[2 * n:]
        me = _place()
        remote = [pltpu.make_async_remote_copy(
            src_ref=x_refs[t].at[2 * q + 1 - me[2]], dst_ref=theirs_refs[t].at[q], send_sem=send_sems.at[4 * t + q],
            recv_sem=recv_sems.at[4 * t + q], device_id=_flip(me, 1), device_id_type=MESH) for t in range(n) for q in range(4)]
        for cp in remote:
            cp.start()
        for cp in remote:
            cp.wait()

    return pl.pallas_call(
        body, name=name, out_shape=[jax.ShapeDtypeStruct((4,) + b.shape[1:], b.dtype) for b in blocks], in_specs=[ANY] * n,
        out_specs=[ANY] * n, scratch_shapes=[pltpu.SemaphoreType.DMA((4 * n,)), pltpu.SemaphoreType.DMA((4 * n,))])(*blocks)


def pair_add(blocks, theirs, name):
    n, rows, width = theirs.shape
    tile = _row_tile(rows)
    spec = pl.BlockSpec((None, tile, width), lambda q, i: (q, i, 0))
    south = pl.BlockSpec((None, None, tile, width), lambda q, i: (q, 0, i, 0))
    north = pl.BlockSpec((None, None, tile, width), lambda q, i: (q, 1, i, 0))

    def body(s_ref, n_ref, b_ref, o_ref):
        mine = jnp.where(lax.axis_index("c") == 0, s_ref[...], n_ref[...])
        o_ref[...] = (mine.astype(f32) + b_ref[...].astype(f32)).astype(o_ref.dtype)

    by_core = blocks.reshape(n, 2, rows, width)
    return pl.pallas_call(body, name=name, grid=(n, rows // tile), in_specs=[south, north, spec], out_specs=spec,
                          out_shape=jax.ShapeDtypeStruct(theirs.shape, theirs.dtype), compiler_params=_params(2))(by_core, by_core, theirs)


def sum_slots(blocks, name):
    n, rows, width = blocks.shape
    tile = _row_tile(rows)

    def body(x_ref, o_ref):
        total = x_ref[0].astype(f32)
        for s in range(1, n):
            total = total + x_ref[s].astype(f32)
        o_ref[...] = total

    return pl.pallas_call(
        body, name=name, grid=(rows // tile,), in_specs=[pl.BlockSpec((n, tile, width), lambda i: (0, i, 0))],
        out_specs=pl.BlockSpec((tile, width), lambda i: (i, 0)), out_shape=_sds((rows, width)), compiler_params=_params(1))(blocks)


def all_reduce_small(x, name):
    rows, width = x.shape

    def body(x_ref, o_ref, land, send_sems, recv_sems):
        me = _place()
        copies = []
        for k in range(1, N_DEV):
            peer = _flip(me, k)
            copies.append(pltpu.make_async_remote_copy(
                src_ref=x_ref, dst_ref=land.at[_slot(me)], send_sem=send_sems.at[k - 1], recv_sem=recv_sems.at[k - 1],
                device_id=peer, device_id_type=MESH))
        for cp in copies:
            cp.start()
        land[_slot(me)] = x_ref[...]
        for k in range(1, N_DEV):
            peer = _flip(me, k)
            pltpu.make_async_remote_copy(
                src_ref=x_ref, dst_ref=land.at[_slot(peer)], send_sem=send_sems.at[k - 1], recv_sem=recv_sems.at[k - 1],
                device_id=peer, device_id_type=MESH).wait_recv()
        total = land[0]
        for s in range(1, N_DEV):
            total = total + land[s]
        o_ref[...] = total
        for cp in copies:
            cp.wait_send()

    return pl.pallas_call(
        body, name=name, out_shape=_sds((rows, width)), in_specs=[pl.BlockSpec(memory_space=pltpu.VMEM)],
        out_specs=pl.BlockSpec(memory_space=pltpu.VMEM),
        scratch_shapes=[pltpu.VMEM((N_DEV, rows, width), f32), pltpu.SemaphoreType.DMA((7,)), pltpu.SemaphoreType.DMA((7,))],
    )(x)


def _pack_big(shards):
    packed = {name: shards[name].astype(bf16) for name in COL_SHARDED}
    packed["rows"] = jnp.concatenate([shards[name].astype(bf16) for name, _ in ROW_SHARDED], axis=1)
    return packed


def _unpack_gathered(gathered):
    full = {}
    for name, part in gathered.items():
        if name == "w_gate_up":
            full[name] = part
        elif name in COL_SHARDED:
            full[name] = part.transpose(1, 0, 2).reshape(D_MODEL, N_DEV * part.shape[2])
        else:
            at = 0
            for weight, rows in ROW_SHARDED:
                full[weight] = part[:, at:at + rows, :].reshape(N_DEV * rows, D_MODEL)
                at += rows
    return full


def _pack_grads(grads, group):
    packed = {}
    for name in group:
        if name == "w_gate_up":
            packed[name] = grads[name]
        elif name == "rows":
            packed[name] = jnp.concatenate([grads[weight].reshape(N_DEV, rows, D_MODEL) for weight, rows in ROW_SHARDED], axis=1)
        else:
            packed[name] = grads[name].reshape(D_MODEL, N_DEV, grads[name].shape[1] // N_DEV).transpose(1, 0, 2)
    return packed


def _unpack_shard(layers):
    out = {name: jnp.stack([layer[name] for layer in layers]) for name in COL_SHARDED}
    rows_pack, at = jnp.stack([layer["rows"] for layer in layers]), 0
    for weight, rows in ROW_SHARDED:
        out[weight] = rows_pack[:, at:at + rows, :]
        at += rows
    return out


def _rows_of(flat_len):
    return -(-flat_len // (8 * D_MODEL)) * 8


def _pack_small(parts):
    flat = jnp.concatenate([p.reshape(-1) for p in parts])
    rows = _rows_of(flat.shape[0])
    flat = jnp.pad(flat, (0, rows * D_MODEL - flat.shape[0]))
    return flat.reshape(rows, D_MODEL)


def _unpack_small(packed, like):
    flat, out, at = packed.reshape(-1), [], 0
    for p in like:
        out.append(flat[at:at + p.size].reshape(p.shape))
        at += p.size
    return out


def _rope_tables(positions):
    inv_freq = jnp.float32(ROPE_THETA) ** (-jnp.arange(0, ROPE_DIM, 2, dtype=f32) / ROPE_DIM)
    ang = positions.astype(f32)[:, None] * inv_freq
    cos, sin = jnp.cos(ang), jnp.sin(ang)
    rest = ATTN_HEAD_DIM - ROPE_DIM
    cos_h = jnp.concatenate([cos, cos, jnp.ones((cos.shape[0], rest), f32)], axis=1)
    sin_h = jnp.concatenate([-sin, sin, jnp.zeros((sin.shape[0], rest), f32)], axis=1)
    return jnp.tile(cos_h, (1, ATTN_HEADS)), jnp.tile(sin_h, (1, ATTN_HEADS))


HEAD_SMALL = ("norm_mix_pre", "conv_short", "conv_gdn", "gdn_a_log", "gdn_dt_bias")


def _layer_head(h, p, cos_t, sin_t):
    hn = rms_norm(h, p["norm_mix_pre"][None], "norm_mix_pre")
    aw, cw, gw = ATTN_WIDTH, CONV_WIDTH, GDN_WIDTH
    aq, ak, av, cb, cc, cx, gqkv, ab, gate = _split_cols(_linear(hn, p["w_in"], "w_in"),
                                                         (aw, aw, aw, cw, cw, cw, 3 * gw, 2 * GDN_HEADS, gw))
    ab = jnp.pad(ab, ((0, 0), (0, LANES - 2 * GDN_HEADS)))
    y_attn = dilated_attention(rope(aq, cos_t, sin_t, ATTN_HEAD_DIM ** -0.5, "rope_q"), rope(ak, cos_t, sin_t, 1.0, "rope_k"),
                               av, "attn")
    y_conv = short_conv(cb, cc, cx, p["conv_short"], "short_conv")
    qkv = gdn_pre(gqkv, p["conv_gdn"], "gdn_pre")
    pv = jnp.zeros((8, LANES), f32).at[0, :GDN_HEADS].set(p["gdn_a_log"]).at[1, :GDN_HEADS].set(p["gdn_dt_bias"])
    return (*_split_cols(qkv, (gw, gw, gw)), gate_beta(ab, pv, "gate_beta")), (gate, y_attn, y_conv)


MID_PARAMS = ("gdn_norm", "w_out", "norm_mix_post", "norm_xattn_pre", "w_xq", "norm_mem", "w_xkv", "w_xo", "norm_xattn_post",
              "norm_ffn_pre")


def _layer_mid(h, o, gate, y_attn, y_conv, p, mem):
    y_gdn = gdn_post(o, gate, p["gdn_norm"][None], "gdn_post")
    mix = _joined_linear((y_attn, y_conv, y_gdn), p["w_out"], "w_out")
    h, hn = add_norm_then_norm(h, mix, p["norm_mix_post"][None], p["norm_xattn_pre"][None], "norm_mix_xattn")
    qx = _linear(hn, p["w_xq"], "w_xq")
    kv = _linear(rms_norm(mem, p["norm_mem"][None], "norm_mem"), p["w_xkv"], "w_xkv")
    xa = _linear(cross_attention(qx, kv, "xattn"), p["w_xo"], "w_xo")
    return add_norm_then_norm(h, xa, p["norm_xattn_post"][None], p["norm_ffn_pre"][None], "norm_xattn_ffn")


def _pair_summed(grads, group, name):
    blocks = _pack_grads(grads, group)
    theirs = pair_exchange([blocks[n] for n in group], name + "_pair_exchange")
    return [pair_add(blocks[n], t, f"{name}_pair_add_{n}") for n, t in zip(group, theirs)]


def _forward_backward(x, packed, small, mem, cos_t, sin_t, target):
    def gathers(group, layer):
        return [Gather(packed[n][layer]) for n in group]

    h = x
    head_gathered = exchange_alone(Together(*gathers(HEAD_GROUP, 0)), "gather_first")
    saved = []
    for layer in range(DEPTH):
        at_layer = {n: t[layer] for n, t in small.items()}
        head_p = {**_unpack_gathered(dict(zip(HEAD_GROUP, head_gathered))), **{n: at_layer[n] for n in HEAD_SMALL}}
        (rule_in, rest), head_vjp = jax.vjp(lambda h, hp: _layer_head(h, hp, cos_t, sin_t), h, head_p)
        carried = gathers(TAIL_GROUP, layer) + (gathers(HEAD_GROUP, layer + 1) if layer + 1 < DEPTH else [])
        o, s_in, inv, *landed = delta_rule_fwd(*rule_in, "delta_rule_fwd", Together(*carried))
        head_gathered = landed[len(TAIL_GROUP):]
        tail_p = {**_unpack_gathered(dict(zip(TAIL_GROUP, landed))), **at_layer}
        mid_p = {n: tail_p[n] for n in MID_PARAMS}
        (h, hn), mid_vjp = jax.vjp(lambda h, o, rest, mp: _layer_mid(h, o, *rest, mp, mem), h, o, rest, mid_p)
        y, ffn_saved = ffn_forward(hn, tail_p["w_gate_up"], tail_p["w_down"], "ffn")
        h, last_vjp = jax.vjp(lambda h, y, w: add_norm(h, y, w[None], "norm_ffn_post"), h, y, tail_p["norm_ffn_post"])
        saved.append((head_vjp, mid_vjp, last_vjp, ffn_saved, rule_in, s_in, inv))

    loss, dh = jax.value_and_grad(lambda y: loss_rows(y, target, "loss"))(h)

    def summed(group, landed):
        return {n: sum_slots(t, "sum_grads_" + n) for n, t in zip(group, landed)}

    big_grads, small_grads, head_pending = [{} for _ in range(DEPTH)], [None] * DEPTH, []
    for layer in reversed(range(DEPTH)):
        head_vjp, mid_vjp, last_vjp, ffn_saved, rule_in, s_in, inv = saved[layer]
        dh, dy, d_norm_ffn_post = last_vjp(dh)
        dhn, d_gate_up, d_down, landed = ffn_backward(
            ffn_saved, dy, "ffn", Together(*[ChipExchange(t) for t in head_pending]) if head_pending else None)
        if head_pending:
            big_grads[layer + 1].update(summed(HEAD_GROUP, landed))
        dh_mid, do, d_rest, d_mid_p = mid_vjp((dh, dhn))
        d_tail_p = {**d_mid_p, "w_gate_up": d_gate_up, "w_down": d_down, "norm_ffn_post": d_norm_ffn_post}
        carried = Together(*[ChipExchange(t) for t in _pair_summed(d_tail_p, TAIL_GROUP, "tail")])
        *d_rule_in, = delta_rule_bwd(*rule_in, s_in, inv, do, "delta_rule_bwd", carried)
        big_grads[layer].update(summed(TAIL_GROUP, d_rule_in[4:]))
        dh_head, d_head_p = head_vjp((tuple(d_rule_in[:4]), d_rest))
        dh = dh_mid + dh_head
        small_grads[layer] = {n: t for n, t in {**d_head_p, **d_tail_p}.items() if n in small}
        head_pending = _pair_summed(d_head_p, HEAD_GROUP, "head")
    landed = exchange_alone(Together(*[ChipExchange(t) for t in head_pending]), "exchange_last")
    big_grads[0].update(summed(HEAD_GROUP, landed))
    return loss, dh, big_grads, small_grads


def kernel(x, mem, positions, norm_mix_pre, norm_mix_post, w_in, conv_short, conv_gdn, gdn_a_log, gdn_dt_bias, gdn_norm, w_out, norm_mem, norm_xattn_pre, norm_xattn_post, w_xq, w_xkv, w_xo, norm_ffn_pre, norm_ffn_post, w_gate_up, w_down, loss_target, m_norm_mix_pre, m_norm_mix_post, m_w_in, m_conv_short, m_conv_gdn, m_gdn_a_log, m_gdn_dt_bias, m_gdn_norm, m_w_out, m_norm_mem, m_norm_xattn_pre, m_norm_xattn_post, m_w_xq, m_w_xkv, m_w_xo, m_norm_ffn_pre, m_norm_ffn_post, m_w_gate_up, m_w_down, v_norm_mix_pre, v_norm_mix_post, v_w_in, v_conv_short, v_conv_gdn, v_gdn_a_log, v_gdn_dt_bias, v_gdn_norm, v_w_out, v_norm_mem, v_norm_xattn_pre, v_norm_xattn_post, v_w_xq, v_w_xkv, v_w_xo, v_norm_ffn_pre, v_norm_ffn_post, v_w_gate_up, v_w_down):
    given = dict(locals())
    weights = {n: given[n] for n in WEIGHTS}
    me = _slot(_place())

    def in_place(shard):
        full = jnp.zeros(shard.shape[:-1] + (shard.shape[-1] * N_DEV,), f32)
        return lax.dynamic_update_slice_in_dim(full, shard, me * shard.shape[-1], axis=shard.ndim - 1)

    placed = [in_place(conv_short), in_place(conv_gdn)]
    conv_short_full, conv_gdn_full = _unpack_small(all_reduce_small(_pack_small(placed), "gather_conv"), placed)
    small = {n: weights[n] for n in NORMS + ("gdn_a_log", "gdn_dt_bias", "gdn_norm")}
    small["conv_short"], small["conv_gdn"] = conv_short_full, conv_gdn_full

    cos_t, sin_t = _rope_tables(positions[0])
    loss, grad_x, big_layers, small_layers = _forward_backward(
        x[0], _pack_big(weights), small, mem[0], cos_t, sin_t, loss_target[0])
    grads = _unpack_shard(big_layers)

    names = sorted(small)
    parts = [jnp.stack([layer[n] for layer in small_layers]) for n in names] + [loss.reshape(1)]
    reduced = _unpack_small(all_reduce_small(_pack_small(parts), "reduce_small"), parts)
    loss = reduced[-1][0]
    for n, g in zip(names, reduced[:-1]):
        if n in ("conv_short", "conv_gdn"):
            width = weights[n].shape[-1]
            g = lax.dynamic_slice_in_dim(g, me * width, width, axis=g.ndim - 1)
        grads[n] = g

    delta, new_m, new_v = {}, {}, {}
    for n in WEIGHTS:
        delta[n], new_m[n], new_v[n] = adamw(weights[n], grads[n], given["m_" + n], given["v_" + n], "adamw_" + n)
    return (loss, grad_x[None], *[grads[n] for n in WEIGHTS], *[delta[n] for n in WEIGHTS],
            *[new_m[n] for n in WEIGHTS], *[new_v[n] for n in WEIGHTS])
```

```python
import functools

import jax
import jax.numpy as jnp
from jax import lax
from jax.experimental import pallas as pl
from jax.experimental.pallas import tpu as pltpu

f32 = jnp.float32
bf16 = jnp.bfloat16
MESH = pl.DeviceIdType.MESH

N_DEV = 8
DEPTH = 4
D_MODEL = 1024
EPS = 1e-6
ATTN_HEADS, ATTN_HEAD_DIM = 4, 64
ATTN_WIDTH = ATTN_HEADS * ATTN_HEAD_DIM
DILATIONS = (1, 4, 16)
QB = 128
ROPE_THETA = 500000.0
ROPE_DIM = ATTN_HEAD_DIM // 4
CONV_WIDTH = 256
GDN_HEADS, GDN_HEAD_DIM = 4, 128
GDN_WIDTH = GDN_HEADS * GDN_HEAD_DIM
GDN_CHUNK = 64
XATTN_HEADS, XATTN_HEAD_DIM = 4, 256
LANES = 128
ROW_TILE = 512
VMEM_LIMIT = 56 * 1024 * 1024

ADAM_LR, ADAM_B1, ADAM_B2, ADAM_EPS, ADAM_WD, ADAM_STEP = 0.001, 0.9, 0.999, 1e-08, 0.01, 10

COL_SHARDED = ("w_in", "w_xkv", "w_gate_up")
ROW_SHARDED = (("w_out", 128), ("w_xq", 128), ("w_xo", 128), ("w_down", 352))
HEAD_GROUP = ("w_in",)
TAIL_GROUP = ("w_gate_up", "w_xkv", "rows")
NORMS = ("norm_mix_pre", "norm_mix_post", "norm_mem", "norm_xattn_pre", "norm_xattn_post", "norm_ffn_pre", "norm_ffn_post")
WEIGHTS = ("norm_mix_pre", "norm_mix_post", "w_in", "conv_short", "conv_gdn", "gdn_a_log", "gdn_dt_bias", "gdn_norm", "w_out",
           "norm_mem", "norm_xattn_pre", "norm_xattn_post", "w_xq", "w_xkv", "w_xo", "norm_ffn_pre", "norm_ffn_post",
           "w_gate_up", "w_down")


def _params(n_grid):
    return pltpu.CompilerParams(dimension_semantics=("arbitrary",) * n_grid, vmem_limit_bytes=VMEM_LIMIT)


def _pick(n, cands):
    for c in cands:
        if n % c == 0:
            return c
    return n


MXU_FLOPS = 9.0e14
HBM_BYTES_PER_S = 2.5e12
VMEM_RMW_BYTES_PER_S = 7.0e12
STEP_S = 0.4e-6
MATMUL_VMEM = 44 * 1024 * 1024


def _tiles(m, n, k, sa, sb, so):
    def divisors(d):
        return sorted({d} | {d // s for s in range(1, d // LANES + 1) if d % s == 0 and (d // s) % LANES == 0}, reverse=True)

    best = None
    for tk in divisors(k):
        nk = k // tk
        for tm in divisors(m):
            for tn_ in divisors(n):
                per_step = tm * tk * sa + tk * tn_ * sb + tm * tn_ * so
                vmem = 2 * per_step + (tm * tn_ * 4 if nk > 1 else 0)
                vmem += (tm * tk * 2 if sa == 4 else 0) + (tk * tn_ * 2 if sb == 4 else 0) + tm * tn_ * 4
                if vmem > MATMUL_VMEM:
                    continue
                moved = m * k * sa * (1 if nk == 1 else n // tn_) + k * n * sb * (1 if nk == 1 and n == tn_ else m // tm) + m * n * so
                busy = 2 * m * n * k / MXU_FLOPS + (m * n * 8 * nk / VMEM_RMW_BYTES_PER_S if nk > 1 else 0)
                cost = max(moved / HBM_BYTES_PER_S, busy) + per_step / HBM_BYTES_PER_S + (m // tm) * (n // tn_) * nk * STEP_S
                if best is None or cost < best[0]:
                    best = (cost, tm, tn_, tk)
    return best[1:]


def _mm(a, b, ta, tb, out_dtype, name):
    m, k = (a.shape[1], a.shape[0]) if ta else a.shape
    n = b.shape[0] if tb else b.shape[1]
    tm, tn, tk = _tiles(m, n, k, a.dtype.itemsize, b.dtype.itemsize, jnp.dtype(out_dtype).itemsize)
    nk = k // tk
    a_spec = pl.BlockSpec((tk, tm), lambda i, j, kk: (kk, i)) if ta else pl.BlockSpec((tm, tk), lambda i, j, kk: (i, kk))
    b_spec = pl.BlockSpec((tn, tk), lambda i, j, kk: (j, kk)) if tb else pl.BlockSpec((tk, tn), lambda i, j, kk: (kk, j))
    dims = (((0 if ta else 1,), (1 if tb else 0,)), ((), ()))

    def body(a_ref, b_ref, o_ref, *acc):
        kk = pl.program_id(2)
        p = lax.dot_general(a_ref[...].astype(bf16), b_ref[...].astype(bf16), dims, preferred_element_type=f32)
        if nk == 1:
            o_ref[...] = p.astype(o_ref.dtype)
            return
        acc_ref, = acc

        @pl.when(kk == 0)
        def _():
            acc_ref[...] = p

        @pl.when(kk > 0)
        def _():
            acc_ref[...] += p

        @pl.when(kk == nk - 1)
        def _():
            o_ref[...] = acc_ref[...].astype(o_ref.dtype)

    return pl.pallas_call(
        body, name=name, grid=(m // tm, n // tn, nk), in_specs=[a_spec, b_spec],
        out_specs=pl.BlockSpec((tm, tn), lambda i, j, kk: (i, j)), out_shape=jax.ShapeDtypeStruct((m, n), out_dtype),
        scratch_shapes=[pltpu.VMEM((tm, tn), f32)] if nk > 1 else [], compiler_params=_params(3))(a, b)


def _linear(x, w, name):
    @jax.custom_vjp
    def lin(x, w):
        return _mm(x, w, False, False, f32, name + "_y")

    def lin_f(x, w):
        return _mm(x, w, False, False, f32, name + "_y"), (x, w)

    def lin_b(res, dy):
        x, w = res
        return _mm(dy, w, False, True, f32, name + "_dx"), _mm(x, dy, True, False, bf16, name + "_dw")

    lin.defvjp(lin_f, lin_b)
    return lin(x, w)


def _bdot(a, b, form):
    dims = {"nn": ((1,), (0,)), "nt": ((1,), (1,)), "tn": ((0,), (0,))}[form]
    return lax.dot_general(a.astype(bf16), b.astype(bf16), (dims, ((), ())), preferred_element_type=f32)


def ffn_forward(hn, w_gate_up, w_down, name):
    s, k = hn.shape
    n_blocks, _, width = w_gate_up.shape
    half = n_blocks // 2
    tm = 512
    blocked = jax.ShapeDtypeStruct((half, s, width), bf16)

    def act_body(x_ref, wg_ref, wu_ref, gate_ref, up_ref, act_ref):
        x = x_ref[...]
        gate, up = _bdot(x, wg_ref[...], "nn"), _bdot(x, wu_ref[...], "nn")
        gate_ref[...], up_ref[...] = gate.astype(bf16), up.astype(bf16)
        act_ref[...] = (jax.nn.silu(gate) * up).astype(bf16)

    tile = pl.BlockSpec((None, tm, width), lambda i, d: (d, i, 0))
    gate, up, act = pl.pallas_call(
        act_body, name=name + "_act", grid=(s // tm, half),
        in_specs=[pl.BlockSpec((tm, k), lambda i, d: (i, 0)), pl.BlockSpec((None, k, width), lambda i, d: (d, 0, 0)),
                  pl.BlockSpec((None, k, width), lambda i, d: (d + half, 0, 0))],
        out_specs=[tile] * 3, out_shape=[blocked] * 3, compiler_params=_params(2))(hn, w_gate_up, w_gate_up)

    n = w_down.shape[1]
    tn = 512

    def y_body(act_ref, w_ref, y_ref):
        y_ref[...] = sum(_bdot(act_ref[d], w_ref[d * width:(d + 1) * width, :], "nn") for d in range(half))

    y = pl.pallas_call(
        y_body, name=name + "_y", grid=(s // tm, n // tn),
        in_specs=[pl.BlockSpec((half, tm, width), lambda i, j: (0, i, 0)), pl.BlockSpec((half * width, tn), lambda i, j: (0, j))],
        out_specs=pl.BlockSpec((tm, tn), lambda i, j: (i, j)), out_shape=_sds((s, n)), compiler_params=_params(2))(act, w_down)
    return y, (hn, w_gate_up, w_down, gate, up, act)


def ffn_backward(saved, dy, name, exchange=None):
    hn, w_gate_up, w_down, gate, up, act = saved
    s, k = hn.shape
    n_blocks, _, width = w_gate_up.shape
    half = n_blocks // 2
    n = w_down.shape[1]
    tm = 512
    blocked = jax.ShapeDtypeStruct((half, s, width), bf16)
    carried = len(exchange.operands) if exchange else 0
    steps = (s // tm, half)

    def dact_body(dy_ref, w_ref, gate_ref, up_ref, *refs):
        x_refs, refs = refs[:carried], refs[carried:]
        (dgate_ref, dup_ref), refs = refs[:2], refs[2:]
        if exchange:
            at = pl.program_id(0) * steps[1] + pl.program_id(1)
            start, wait = exchange.bind(x_refs, refs[:carried], refs[carried:])
            pl.when(at == 0)(start)
        d_act = _bdot(dy_ref[...], w_ref[...], "nt")
        g, u = gate_ref[...].astype(f32), up_ref[...].astype(f32)
        sig = jax.nn.sigmoid(g)
        dgate_ref[...] = (d_act * u * sig * (1.0 + g * (1.0 - sig))).astype(bf16)
        dup_ref[...] = (d_act * g * sig).astype(bf16)
        if exchange:
            pl.when(at == steps[0] * steps[1] - 1)(wait)

    tile = pl.BlockSpec((None, tm, width), lambda i, d: (d, i, 0))
    d_gate, d_up, *landed = pl.pallas_call(
        dact_body, name=name + "_dact", grid=steps,
        in_specs=[pl.BlockSpec((tm, n), lambda i, d: (i, 0)), pl.BlockSpec((width, n), lambda i, d: (d, 0)), tile, tile] + [ANY] * carried,
        out_specs=[tile, tile] + [ANY] * carried, out_shape=[blocked, blocked] + (exchange.out_shapes if exchange else []),
        scratch_shapes=exchange.scratch if exchange else [],
        compiler_params=_params(2))(dy, w_down, gate, up, *(exchange.operands if exchange else []))

    def dx_body(dg_ref, du_ref, w_ref, dx_ref):
        dx_ref[...] = sum(_bdot(dg_ref[d], w_ref[d], "nt") + _bdot(du_ref[d], w_ref[d + half], "nt") for d in range(half))

    tx = 256
    rows = pl.BlockSpec((half, tx, width), lambda i: (0, i, 0))
    dx = pl.pallas_call(
        dx_body, name=name + "_dx", grid=(s // tx,), in_specs=[rows, rows, _whole(w_gate_up.shape)],
        out_specs=pl.BlockSpec((tx, k), lambda i: (i, 0)), out_shape=_sds((s, k)), compiler_params=_params(1))(d_gate, d_up, w_gate_up)

    def dw1_body(x_ref, dg_ref, du_ref, dw_ref):
        d_block = jnp.where(pl.program_id(0) < half, dg_ref[...], du_ref[...])
        dw_ref[...] = _bdot(x_ref[...], d_block, "tn").astype(bf16)

    d_w_gate_up = pl.pallas_call(
        dw1_body, name=name + "_dw1", grid=(n_blocks,),
        in_specs=[_whole((s, k)), pl.BlockSpec((None, s, width), lambda b: (jnp.minimum(b, half - 1), 0, 0)),
                  pl.BlockSpec((None, s, width), lambda b: (jnp.maximum(b - half, 0), 0, 0))],
        out_specs=pl.BlockSpec((None, k, width), lambda b: (b, 0, 0)), out_shape=jax.ShapeDtypeStruct(w_gate_up.shape, bf16),
        compiler_params=_params(1))(hn, d_gate, d_up)

    tn = 512

    def dw2_body(act_ref, dy_ref, dw_ref):
        dw_ref[...] = _bdot(act_ref[...], dy_ref[...], "tn").astype(bf16)

    d_w_down = pl.pallas_call(
        dw2_body, name=name + "_dw2", grid=(half, n // tn),
        in_specs=[pl.BlockSpec((None, s, width), lambda d, j: (d, 0, 0)), pl.BlockSpec((s, tn), lambda d, j: (0, j))],
        out_specs=pl.BlockSpec((width, tn), lambda d, j: (d, j)), out_shape=jax.ShapeDtypeStruct(w_down.shape, bf16),
        compiler_params=_params(2))(act, dy)
    return dx, d_w_gate_up, d_w_down, landed


def _joined_linear(parts, w, name):
    s, n = parts[0].shape[0], w.shape[1]
    widths = [p.shape[1] for p in parts]
    edges = [sum(widths[:i]) for i in range(len(widths) + 1)]
    spans = list(zip(edges[:-1], edges[1:]))
    tm, tn = 512, 512

    def forward(*args):
        *xs, w = args

        def y_body(*refs):
            *x_refs, w_ref, y_ref = refs
            y_ref[...] = sum(_bdot(x_ref[...], w_ref[a:b, :], "nn") for x_ref, (a, b) in zip(x_refs, spans))

        y = pl.pallas_call(
            y_body, name=name + "_y", grid=(s // tm, n // tn),
            in_specs=[pl.BlockSpec((tm, k), lambda i, j: (i, 0)) for k in widths] + [pl.BlockSpec((edges[-1], tn), lambda i, j: (0, j))],
            out_specs=pl.BlockSpec((tm, tn), lambda i, j: (i, j)), out_shape=_sds((s, n)), compiler_params=_params(2))(*xs, w)
        return y, args

    def backward(args, dy):
        *xs, w = args

        def dx_body(dy_ref, w_ref, *dx_refs):
            d_all = _bdot(dy_ref[...], w_ref[...], "nt")
            for dx_ref, (a, b) in zip(dx_refs, spans):
                dx_ref[...] = d_all[:, a:b]

        dxs = pl.pallas_call(
            dx_body, name=name + "_dx", grid=(s // tm,), in_specs=[pl.BlockSpec((tm, n), lambda i: (i, 0)), _whole(w.shape)],
            out_specs=[pl.BlockSpec((tm, k), lambda i: (i, 0)) for k in widths], out_shape=[_sds((s, k)) for k in widths],
            compiler_params=_params(1))(dy, w)

        def dw_body(*refs):
            *x_refs, dy_ref, dw_ref = refs
            dy_tile = dy_ref[...]
            dw_ref[...] = jnp.concatenate([_bdot(x_ref[...], dy_tile, "tn") for x_ref in x_refs], axis=0).astype(bf16)

        dw = pl.pallas_call(
            dw_body, name=name + "_dw", grid=(n // tn,),
            in_specs=[_whole((s, k)) for k in widths] + [pl.BlockSpec((s, tn), lambda j: (0, j))],
            out_specs=pl.BlockSpec((edges[-1], tn), lambda j: (0, j)), out_shape=jax.ShapeDtypeStruct(w.shape, bf16),
            compiler_params=_params(1))(*xs, dy)
        return (*dxs, dw)

    @jax.custom_vjp
    def op(*args):
        return forward(*args)[0]

    op.defvjp(forward, backward)
    return op(*parts, w)


def _split_cols(x, widths):
    edges = [sum(widths[:i]) for i in range(len(widths) + 1)]

    def cut(x):
        return tuple(x[:, a:b] for a, b in zip(edges[:-1], edges[1:]))

    @jax.custom_vjp
    def split(x):
        return cut(x)

    split.defvjp(lambda x: (cut(x), None), lambda _, cts: (jnp.concatenate(cts, axis=1),))
    return split(x)


def _block_op(name, f, grid, in_specs, out_defs, arrays, diff, acc=None, gdefs=None):
    acc, gdefs = acc or {}, gdefs or {}
    n_in, n_out, n_grid = len(in_specs), len(out_defs), len(grid)

    def fwd_call(*xs):
        def body(*refs):
            outs = f(*[r[...] for r in refs[:n_in]])
            for r, o in zip(refs[n_in:], outs):
                r[...] = o.astype(r.dtype)

        return pl.pallas_call(
            body, name=name + "_fwd", grid=grid, in_specs=in_specs, out_specs=[d[1] for d in out_defs],
            out_shape=[d[0] for d in out_defs], compiler_params=_params(n_grid))(*xs)

    def bwd_call(*xs_and_cts):
        def body(*refs):
            xs = [r[...] for r in refs[:n_in]]
            cts = tuple(r[...] for r in refs[n_in:n_in + n_out])

            def of_diff(*dx):
                full = list(xs)
                for i, v in zip(diff, dx):
                    full[i] = v
                return tuple(f(*full))

            _, vjp = jax.vjp(of_diff, *[xs[i] for i in diff])
            grads = vjp(cts)
            for i, g, r in zip(diff, grads, refs[n_in + n_out:]):
                if i in acc:
                    first = functools.reduce(jnp.logical_and, [pl.program_id(a) == 0 for a in acc[i]])

                    @pl.when(first)
                    def _(r=r):
                        r[...] = jnp.zeros_like(r)

                    r[...] += g.astype(r.dtype)
                else:
                    r[...] = g.astype(r.dtype)

        g_defs = [gdefs.get(i, (jax.ShapeDtypeStruct(arrays[i].shape, f32), in_specs[i])) for i in diff]
        return pl.pallas_call(
            body, name=name + "_bwd", grid=grid, in_specs=list(in_specs) + [d[1] for d in out_defs],
            out_specs=[d[1] for d in g_defs], out_shape=[d[0] for d in g_defs], compiler_params=_params(n_grid))(*xs_and_cts)

    return fwd_call, bwd_call


def _simple_op(name, f, grid, in_specs, out_defs, arrays, diff, acc=None):
    fwd_call, bwd_call = _block_op(name, f, grid, in_specs, out_defs, arrays, diff, acc)

    @jax.custom_vjp
    def op(*xs):
        return tuple(fwd_call(*xs))

    def op_f(*xs):
        return tuple(fwd_call(*xs)), xs

    def op_b(xs, cts):
        grads = bwd_call(*xs, *cts)
        out = [jnp.zeros_like(x) for x in xs]
        for i, g in zip(diff, grads):
            out[i] = g
        return tuple(out)

    op.defvjp(op_f, op_b)
    return op(*arrays)


def _rows(width, tile=ROW_TILE):
    return pl.BlockSpec((tile, width), lambda i: (i, 0))


def _whole(shape):
    return pl.BlockSpec(shape, lambda *_: (0,) * len(shape))


def _sds(shape):
    return jax.ShapeDtypeStruct(shape, f32)


def _rms(x, w):
    return x * lax.rsqrt(jnp.mean(x * x, axis=-1, keepdims=True) + EPS) * w


def rms_norm(x, w, name):
    r, d = x.shape
    tile = min(ROW_TILE, r)
    return _simple_op(name, lambda x, w: (_rms(x, w),), (r // tile,), [_rows(d, tile), _whole((1, d))],
                      [(_sds((r, d)), _rows(d, tile))], (x, w), (0, 1), {1: (0,)})[0]


def add_norm(h, y, w, name):
    r, d = h.shape
    return _simple_op(name, lambda h, y, w: (h + _rms(y, w),), (r // ROW_TILE,), [_rows(d), _rows(d), _whole((1, d))],
                      [(_sds((r, d)), _rows(d))], (h, y, w), (0, 1, 2), {2: (0,)})[0]


def add_norm_then_norm(h, y, w_post, w_pre, name):
    r, d = h.shape

    def f(h, y, w_post, w_pre):
        h_new = h + _rms(y, w_post)
        return h_new, _rms(h_new, w_pre)

    return _simple_op(name, f, (r // ROW_TILE,), [_rows(d), _rows(d), _whole((1, d)), _whole((1, d))],
                      [(_sds((r, d)), _rows(d))] * 2, (h, y, w_post, w_pre), (0, 1, 2, 3), {2: (0,), 3: (0,)})


def _swap8(x):
    def raw(x):
        lane = lax.broadcasted_iota(jnp.int32, x.shape, 1) % ATTN_HEAD_DIM
        half = ROPE_DIM // 2
        up = pltpu.roll(x, x.shape[1] - half, axis=1)
        down = pltpu.roll(x, half, axis=1)
        return jnp.where(lane < half, up, jnp.where(lane < ROPE_DIM, down, 0.0))

    @jax.custom_vjp
    def swap(x):
        return raw(x)

    swap.defvjp(lambda x: (raw(x), None), lambda _, g: (raw(g),))
    return swap(x)


def rope(x, cos_t, sin_t, scale, name):
    r, d = x.shape
    return _simple_op(name, lambda x, c, s: ((x * c + _swap8(x) * s) * scale,), (r // ROW_TILE,), [_rows(d)] * 3,
                      [(_sds((r, d)), _rows(d))], (x, cos_t, sin_t), (0,))[0]


def _shift_rows(x, k):
    n = x.shape[0]

    def down(x):
        row = lax.broadcasted_iota(jnp.int32, x.shape, 0)
        return jnp.where(row >= k, pltpu.roll(x, k, axis=0), 0.0)

    def up(x):
        row = lax.broadcasted_iota(jnp.int32, x.shape, 0)
        return jnp.where(row < n - k, pltpu.roll(x, n - k, axis=0), 0.0)

    @jax.custom_vjp
    def shift(x):
        return down(x)

    shift.defvjp(lambda x: (down(x), None), lambda _, g: (up(g),))
    return shift(x)


def _causal_conv(x, w):
    taps = w.shape[0]
    y = x * w[taps - 1:taps, :]
    for j in range(taps - 1):
        y = y + _shift_rows(x, taps - 1 - j) * w[j:j + 1, :]
    return y


def _cols(rows, at=0):
    return pl.BlockSpec((rows, LANES), lambda j: (0, at + j))


def short_conv(cb, cc, cx, w, name):
    s, c = cb.shape
    taps = w.shape[0]
    return _simple_op(name, lambda b, c_, x, w: (b * _causal_conv(c_ * x, w),), (c // LANES,),
                      [_cols(s)] * 3 + [_cols(taps)], [(_sds((s, c)), _cols(s))], (cb, cc, cx, w), (0, 1, 2, 3))[0]


def gdn_pre(qkv, w, name):
    s, c = qkv.shape
    taps = w.shape[0]

    def f(x, w):
        j = pl.program_id(0)
        y = jax.nn.silu(_causal_conv(x, w))
        normed = y * lax.rsqrt(jnp.sum(y * y, axis=-1, keepdims=True) + EPS)
        scale = jnp.where(j < GDN_HEADS, GDN_HEAD_DIM ** -0.5, 1.0).astype(f32)
        return (jnp.where(j < 2 * GDN_HEADS, normed * scale, y),)

    return _simple_op(name, f, (c // LANES,), [_cols(s), _cols(taps)], [(_sds((s, c)), _cols(s))], (qkv, w), (0, 1))[0]


def gate_beta(ab, pv, name):
    s = ab.shape[0]

    def f(ab, pv):
        lane = lax.broadcasted_iota(jnp.int32, ab.shape, 1)
        g = -jnp.exp(pv[0:1, :]) * jax.nn.softplus(ab + pv[1:2, :])
        return (jnp.where(lane < GDN_HEADS, g, jnp.where(lane < 2 * GDN_HEADS, jax.nn.sigmoid(ab), 0.0)),)

    return _simple_op(name, f, (s // ROW_TILE,), [_rows(LANES), _whole((8, LANES))], [(_sds((s, LANES)), _rows(LANES))],
                      (ab, pv), (0, 1), {1: (0,)})[0]


def gdn_post(o, gate, w, name):
    s, c = o.shape

    def f(o, g, w):
        heads = [slice(hd * LANES, (hd + 1) * LANES) for hd in range(c // LANES)]
        return (jnp.concatenate([_rms(o[:, hd], w) * jax.nn.silu(g[:, hd]) for hd in heads], axis=1),)

    return _simple_op(name, f, (s // ROW_TILE,), [_rows(c), _rows(c), _whole((1, LANES))], [(_sds((s, c)), _rows(c))],
                      (o, gate, w), (0, 1, 2), {2: (0,)})[0]


def attn_merge(outs, lses, name):
    s, c = outs[0].shape

    def f(o1, o2, o3, l1, l2, l3):
        m = lax.stop_gradient(jnp.maximum(jnp.maximum(l1, l2), l3))
        e1, e2, e3 = jnp.exp(l1 - m), jnp.exp(l2 - m), jnp.exp(l3 - m)
        return ((e1 * o1 + e2 * o2 + e3 * o3) / (e1 + e2 + e3),)

    return _simple_op(name, f, (s // ROW_TILE,), [_rows(c)] * 6, [(_sds((s, c)), _rows(c))], (*outs, *lses), tuple(range(6)))[0]


def loss_rows(y, target, name):
    s, d = y.shape
    nt = s // ROW_TILE

    def f(y, t):
        e = y - t
        part = 0.5 * jnp.sum(jnp.mean(e * e, axis=-1, keepdims=True), axis=0, keepdims=True)
        return (jnp.broadcast_to(part * (1.0 / (8 * LANES)), (8, LANES)),)

    out = _simple_op(name, f, (nt,), [_rows(d)] * 2, [(_sds((nt * 8, LANES)), pl.BlockSpec((8, LANES), lambda i: (i, 0)))],
                     (y, target), (0,))[0]
    return jnp.sum(out)


def _mxu(a, b, form):
    dims = {"nn": ((1,), (0,)), "nt": ((1,), (1,)), "tn": ((0,), (0,))}

    def raw(a, b, form):
        return lax.dot_general(a.astype(bf16), b.astype(bf16), (dims[form], ((), ())), preferred_element_type=f32)

    @jax.custom_vjp
    def prod(a, b):
        return raw(a, b, form)

    def prod_b(res, ct):
        a, b = res
        if form == "nn":
            return raw(ct, b, "nt"), raw(a, ct, "tn")
        if form == "nt":
            return raw(ct, b, "nn"), raw(ct, a, "tn")
        return raw(b, ct, "nt"), raw(a, ct, "nn")

    prod.defvjp(lambda a, b: (raw(a, b, form), (a, b)), prod_b)
    return prod(a, b)


def _masked_heads_attention(q, keys, values, seen):
    dh = ATTN_HEAD_DIM
    outs, lses = [], []
    for hd in range(q.shape[1] // dh):
        at = slice(hd * dh, (hd + 1) * dh)
        sc = jnp.where(seen, _mxu(q[:, at], keys[:, at], "nt"), -jnp.inf)
        m = lax.stop_gradient(jnp.max(sc, axis=-1, keepdims=True))
        p = jnp.exp(sc - m)
        l = jnp.sum(p, axis=-1, keepdims=True)
        outs.append(_mxu(p / l, values[:, at], "nn"))
        lses.append(jnp.broadcast_to(m + jnp.log(l), (q.shape[0], dh)))
    return jnp.concatenate(outs, axis=1), jnp.concatenate(lses, axis=1)


def band_attention(q, k, v, nb, name):
    r, qb, width = q.shape

    def f(q, kp, kc, vp, vc):
        has_prev = (pl.program_id(0) % nb) > 0
        i = lax.broadcasted_iota(jnp.int32, (qb, 2 * qb), 0)
        j = lax.broadcasted_iota(jnp.int32, (qb, 2 * qb), 1)
        seen = jnp.logical_or(jnp.logical_and(jnp.logical_and(j < qb, j >= i), has_prev), jnp.logical_and(j >= qb, j - qb <= i))
        return _masked_heads_attention(q, jnp.concatenate([kp, kc], axis=0), jnp.concatenate([vp, vc], axis=0), seen)

    blk = (None, qb, width)
    cur = pl.BlockSpec(blk, lambda b: (b, 0, 0))
    prev = pl.BlockSpec(blk, lambda b: (jnp.maximum(b - 1, 0), 0, 0))
    shape = _sds((r, qb, width))
    fwd_call, bwd_call = _block_op(name, f, (r,), [cur, prev, cur, prev, cur], [(shape, cur), (shape, cur)],
                                   (q, k, k, v, v), (0, 1, 2, 3, 4), gdefs={1: (shape, cur), 3: (shape, cur)})

    def to_prev(g):
        return jnp.concatenate([g[1:], jnp.zeros_like(g[:1])], axis=0)

    @jax.custom_vjp
    def op(q, k, v):
        return tuple(fwd_call(q, k, k, v, v))

    def op_b(res, cts):
        q, k, v = res
        dq, dkp, dkc, dvp, dvc = bwd_call(q, k, k, v, v, *cts)
        return dq, dkc + to_prev(dkp), dvc + to_prev(dvp)

    op.defvjp(lambda q, k, v: (tuple(fwd_call(q, k, k, v, v)), (q, k, v)), op_b)
    return op(q, k, v)


def dilated_attention(q, k, v, name):
    s = q.shape[0]
    outs, lses = [], []
    for d in DILATIONS:
        length = s // d
        nb = length // QB
        def to_residue(t):
            return t.reshape(length, d, ATTN_WIDTH).transpose(1, 0, 2).reshape(d * nb, QB, ATTN_WIDTH)

        def from_residue(t):
            return t.reshape(d, length, ATTN_WIDTH).transpose(1, 0, 2).reshape(s, ATTN_WIDTH)

        o, lse = band_attention(to_residue(q), to_residue(k), to_residue(v), nb, f"{name}_d{d}")
        outs.append(from_residue(o))
        lses.append(from_residue(lse))
    return attn_merge(outs, lses, name + "_merge")


def cross_attention(q, kv, name):
    s = q.shape[0]
    m = kv.shape[0]
    width = XATTN_HEADS * XATTN_HEAD_DIM
    tq = 512

    def f(q, k, v):
        sc = _mxu(q, k, "nt") * (XATTN_HEAD_DIM ** -0.5)
        mx = lax.stop_gradient(jnp.max(sc, axis=-1, keepdims=True))
        p = jnp.exp(sc - mx)
        return (_mxu(p / jnp.sum(p, axis=-1, keepdims=True), v, "nn"),)

    q_spec = pl.BlockSpec((tq, XATTN_HEAD_DIM), lambda a, i: (i, a))
    k_spec = pl.BlockSpec((m, XATTN_HEAD_DIM), lambda a, i: (0, a))
    v_spec = pl.BlockSpec((m, XATTN_HEAD_DIM), lambda a, i: (0, a + XATTN_HEADS))
    half = _sds((m, width))
    fwd_call, bwd_call = _block_op(name, f, (XATTN_HEADS, s // tq), [q_spec, k_spec, v_spec], [(_sds((s, width)), q_spec)],
                                   (q, kv, kv), (0, 1, 2), acc={1: (1,), 2: (1,)}, gdefs={1: (half, k_spec), 2: (half, k_spec)})

    @jax.custom_vjp
    def op(q, kv):
        return fwd_call(q, kv, kv)[0]

    def op_b(res, ct):
        q, kv = res
        dq, dk, dv = bwd_call(q, kv, kv, ct)
        return dq, jnp.concatenate([dk, dv], axis=1)

    op.defvjp(lambda q, kv: (fwd_call(q, kv, kv)[0], (q, kv)), op_b)
    return op(q, kv)


def _hi(a, b, form="nn"):
    dims = {"nn": ((1,), (0,)), "nt": ((1,), (1,)), "tn": ((0,), (0,))}[form]
    return lax.dot_general(a, b, (dims, ((), ())), precision=lax.Precision.HIGH, preferred_element_type=f32)


def _running_sum(g):
    def raw(x, form):
        c = x.shape[0]
        tri = (lax.broadcasted_iota(jnp.int32, (c, c), 0) >= lax.broadcasted_iota(jnp.int32, (c, c), 1)).astype(bf16)
        hi = x.astype(bf16)
        rest = x - hi.astype(f32)
        mid = rest.astype(bf16)
        low = (rest - mid.astype(f32)).astype(bf16)
        dims = (((1,) if form == "nn" else (0,), (0,)), ((), ()))
        return sum(lax.dot_general(tri, part, dims, preferred_element_type=f32) for part in (hi, mid, low))

    @jax.custom_vjp
    def run(x):
        return raw(x, "nn")

    run.defvjp(lambda x: (raw(x, "nn"), None), lambda _, ct: (raw(ct, "tn"),))
    return run(g)


def _unit_lower_inverse(a):
    c = a.shape[0]
    eye = (lax.broadcasted_iota(jnp.int32, (c, c), 0) == lax.broadcasted_iota(jnp.int32, (c, c), 1)).astype(f32)
    inv, power = eye - a, -a
    for _ in range(c.bit_length() - 2):
        power = _hi(power, power)
        inv = inv + _hi(inv, power)
    return inv


def _known_inverse(a, t):
    @jax.custom_vjp
    def inv(a, t):
        return t

    def inv_b(t, ct):
        return -_hi(_hi(t, ct, "tn"), t, "nt"), jnp.zeros_like(t)

    inv.defvjp(lambda a, t: (t, t), inv_b)
    return inv(a, t)


def _delta_chunk(q, k, v, g, beta, s0, known_inv=None):
    c = q.shape[0]
    i = lax.broadcasted_iota(jnp.int32, (c, c), 0)
    j = lax.broadcasted_iota(jnp.int32, (c, c), 1)
    causal, strict = i >= j, i > j
    dec = _running_sum(g)
    dec_i = dec[:, :c]
    rel = jnp.exp(jnp.where(causal, dec_i - dec_i.T, -jnp.inf))
    k_beta = k * beta
    on_k = _mxu(jnp.concatenate([k_beta, q], axis=0), k, "nt")
    a = jnp.where(strict, on_k[:c] * rel, 0.0)
    attn = jnp.where(causal, on_k[c:] * rel, 0.0)
    inv = _unit_lower_inverse(a) if known_inv is None else _known_inverse(a, known_inv)
    e_dec = jnp.exp(dec)
    solved = _hi(inv, jnp.concatenate([v * beta, k_beta * e_dec], axis=1))
    u, w = solved[:, :v.shape[1]], solved[:, v.shape[1]:]
    total = jnp.sum(g, axis=0, keepdims=True)
    on_state = _mxu(jnp.concatenate([w, q * e_dec], axis=0), s0, "nn")
    v_new = u - on_state[:c]
    o = on_state[c:] + _mxu(attn, v_new, "nn")
    s1 = s0 * jnp.exp(total) + _mxu(k * jnp.exp(total - dec), v_new, "tn")
    return o, s1, inv


def _delta_rule_call(name, walk, n, in_specs, out_specs, out_shape, operands, exchange):
    n_in, n_out = len(in_specs), len(out_specs)
    carried = len(exchange.operands) if exchange else 0

    def body(*refs):
        ins, refs = refs[:n_in], refs[n_in:]
        x_refs, refs = refs[:carried], refs[carried:]
        outs, refs = refs[:n_out], refs[n_out:]
        land_refs, (state, *sems) = refs[:carried], refs[carried:]
        step = pl.program_id(0)
        if exchange:
            start, finish = exchange.bind(x_refs, land_refs, sems)
            pl.when(step == 0)(start)

        @pl.when(step == 0)
        def _():
            state[...] = jnp.zeros_like(state)

        walk(ins, outs, state)
        if exchange:
            pl.when(step == n - 1)(finish)

    return pl.pallas_call(
        body, name=name, grid=(n,), in_specs=list(in_specs) + [ANY] * carried, out_specs=list(out_specs) + [ANY] * carried,
        out_shape=list(out_shape) + (exchange.out_shapes if exchange else []),
        scratch_shapes=[pltpu.VMEM((GDN_HEAD_DIM, GDN_WIDTH), f32)] + (exchange.scratch if exchange else []),
        compiler_params=_params(1))(*operands, *(exchange.operands if exchange else []))


def _delta_heads():
    heads = [slice(hd * GDN_HEAD_DIM, (hd + 1) * GDN_HEAD_DIM) for hd in range(GDN_HEADS)]
    inv_at = [slice(hd * GDN_CHUNK, (hd + 1) * GDN_CHUNK) for hd in range(GDN_HEADS)]
    return heads, inv_at


def _head_chunk(q, k, v, gates, s0, head, known_inv=None):
    g = jnp.broadcast_to(gates[:, head:head + 1], q.shape)
    beta = jnp.broadcast_to(gates[:, GDN_HEADS + head:GDN_HEADS + head + 1], q.shape)
    return _delta_chunk(q, k, v, g, beta, s0, known_inv)


def delta_rule_fwd(q, k, v, gates, name, exchange=None):
    s, width = q.shape
    c, dk = GDN_CHUNK, GDN_HEAD_DIM
    n = s // c
    heads, inv_at = _delta_heads()

    def walk(ins, outs, state):
        q_ref, k_ref, v_ref, gates_ref = ins
        o_ref, s_in_ref, inv_ref = outs
        s_in_ref[...] = state[...]
        gates = gates_ref[...]
        xs = [[r[:, hd] for r in (q_ref, k_ref, v_ref)] + [gates, state[:, hd], i] for i, hd in enumerate(heads)]
        ys = [_head_chunk(*x) for x in xs]
        for hd, at, (o, s1, inv) in zip(heads, inv_at, ys):
            o_ref[:, hd], state[:, hd], inv_ref[:, at] = o, s1, inv

    blk = pl.BlockSpec((c, width), lambda t: (t, 0))
    gt = pl.BlockSpec((c, LANES), lambda t: (t, 0))
    st = pl.BlockSpec((dk, width), lambda t: (t, 0))
    iv = pl.BlockSpec((c, GDN_HEADS * c), lambda t: (t, 0))
    return _delta_rule_call(name, walk, n, [blk] * 3 + [gt], [blk, st, iv],
                            [_sds((s, width)), _sds((n * dk, width)), _sds((s, GDN_HEADS * c))], (q, k, v, gates), exchange)


def delta_rule_bwd(q, k, v, gates, s_in, inv, do, name, exchange=None):
    s, width = q.shape
    c, dk = GDN_CHUNK, GDN_HEAD_DIM
    n = s // c
    heads, inv_at = _delta_heads()

    def walk(ins, outs, dstate):
        q_ref, k_ref, v_ref, gates_ref, s_ref, inv_ref, do_ref = ins
        dq_ref, dk_ref, dv_ref, dgates_ref = outs
        gates = gates_ref[...]
        xs = [[r[:, hd] for r in (q_ref, k_ref, v_ref)] + [gates, s_ref[:, hd]] for hd in heads]
        known = [inv_ref[:, at] for at in inv_at]
        cts = [(do_ref[:, hd], dstate[:, hd]) for hd in heads]
        grads = []
        for i, (x, t, ct) in enumerate(zip(xs, known, cts)):
            _, vjp = jax.vjp(lambda *y, t=t, i=i: _head_chunk(*y, i, known_inv=t)[:2], *x)
            grads.append(vjp(ct))
        dgates = grads[0][3]
        for g in grads[1:]:
            dgates = dgates + g[3]
        dgates_ref[...] = dgates
        for hd, (dq, dk_, dv, _, ds0) in zip(heads, grads):
            dq_ref[:, hd], dk_ref[:, hd], dv_ref[:, hd], dstate[:, hd] = dq, dk_, dv, ds0

    blk = pl.BlockSpec((c, width), lambda t: (n - 1 - t, 0))
    gt = pl.BlockSpec((c, LANES), lambda t: (n - 1 - t, 0))
    st = pl.BlockSpec((dk, width), lambda t: (n - 1 - t, 0))
    iv = pl.BlockSpec((c, GDN_HEADS * c), lambda t: (n - 1 - t, 0))
    return _delta_rule_call(name, walk, n, [blk] * 3 + [gt, st, iv, blk], [blk] * 3 + [gt],
                            [_sds((s, width))] * 3 + [_sds((s, LANES))], (q, k, v, gates, s_in, inv, do), exchange)


def adamw(w, g, m, v, name):
    shape = w.shape
    if len(shape) == 2:
        grid, spec = (1,), pl.BlockSpec(shape, lambda i: (0, 0))
    else:
        tile = shape[1] if shape[1] <= 512 else _pick(shape[1], (512, 256, 128))
        grid, spec = (shape[0], shape[1] // tile), pl.BlockSpec((None, tile, shape[2]), lambda layer, i: (layer, i, 0))

    def body(w_ref, g_ref, m_ref, v_ref, d_ref, nm_ref, nv_ref):
        grad = g_ref[...]
        nm = ADAM_B1 * m_ref[...] + (1.0 - ADAM_B1) * grad
        nv = ADAM_B2 * v_ref[...] + (1.0 - ADAM_B2) * (grad * grad)
        m_hat = nm / (1.0 - ADAM_B1 ** ADAM_STEP)
        v_hat = nv / (1.0 - ADAM_B2 ** ADAM_STEP)
        d_ref[...] = -ADAM_LR * (m_hat / (jnp.sqrt(v_hat) + ADAM_EPS) + ADAM_WD * w_ref[...])
        nm_ref[...] = nm
        nv_ref[...] = nv

    return tuple(pl.pallas_call(body, name=name, grid=grid, in_specs=[spec] * 4, out_specs=[spec] * 3,
                                out_shape=[_sds(shape)] * 3, compiler_params=_params(len(grid)))(w, g, m, v))


def _place():
    return lax.axis_index("x"), lax.axis_index("y"), lax.axis_index("c")


def _flip(p, bits):
    return tuple(1 - v if (bits >> s) & 1 else v for v, s in zip(p, (2, 1, 0)))


def _slot(p):
    return 4 * p[0] + 2 * p[1] + p[2]


def _chip_of(p):
    return 2 * p[0] + p[1]


ANY = pl.BlockSpec(memory_space=pl.ANY)


class Gather:
    scratch = (pltpu.SemaphoreType.DMA((7,)), pltpu.SemaphoreType.DMA((7,)), pltpu.SemaphoreType.DMA)

    def __init__(self, shard):
        self.operand = shard
        self.out_shape = jax.ShapeDtypeStruct((N_DEV,) + shard.shape, shard.dtype)

    def bind(self, x_ref, out_ref, send_sems, recv_sems, local_sem):
        me = _place()
        sibling = _flip(me, 1)
        chips = [_flip(me, 4), _flip(me, 2), _flip(me, 6)]

        def copy(k, block, to, src=None):
            return pltpu.make_async_remote_copy(
                src_ref=out_ref.at[_slot(block)] if src is None else src, dst_ref=out_ref.at[_slot(block)],
                send_sem=send_sems.at[k], recv_sem=recv_sems.at[k], device_id=to, device_id_type=MESH)

        mine = pltpu.make_async_copy(x_ref, out_ref.at[_slot(me)], local_sem)
        first = [copy(0, me, sibling, src=x_ref)] + [copy(1 + j, me, chip, src=x_ref) for j, chip in enumerate(chips)]
        passed = [copy(4 + j, chip, sibling) for j, chip in enumerate(chips)]

        def start():
            mine.start()
            for cp in first:
                cp.start()

        def finish():
            for j, chip in enumerate(chips):
                copy(1 + j, chip, me).wait_recv()
                passed[j].start()
            copy(0, sibling, me).wait_recv()
            for j, chip in enumerate(chips):
                copy(4 + j, _flip(chip, 1), me).wait_recv()
            for cp in first + passed:
                cp.wait_send()
            mine.wait()

        return start, finish


class ChipExchange:
    scratch = (pltpu.SemaphoreType.DMA((3,)), pltpu.SemaphoreType.DMA((3,)), pltpu.SemaphoreType.DMA)

    def __init__(self, blocks):
        self.operand = blocks
        self.out_shape = jax.ShapeDtypeStruct(blocks.shape, blocks.dtype)

    def bind(self, x_ref, out_ref, send_sems, recv_sems, local_sem):
        me = _place()
        peers = [_flip(me, 4), _flip(me, 2), _flip(me, 6)]
        mine = pltpu.make_async_copy(x_ref.at[_chip_of(me)], out_ref.at[_chip_of(me)], local_sem)

        def copy(j, src_chip, dst_chip):
            return pltpu.make_async_remote_copy(
                src_ref=x_ref.at[src_chip], dst_ref=out_ref.at[dst_chip], send_sem=send_sems.at[j],
                recv_sem=recv_sems.at[j], device_id=peers[j], device_id_type=MESH)

        sends = [copy(j, _chip_of(peer), _chip_of(me)) for j, peer in enumerate(peers)]

        def start():
            mine.start()
            for cp in sends:
                cp.start()

        def finish():
            for j, peer in enumerate(peers):
                copy(j, _chip_of(me), _chip_of(peer)).wait_recv()
            for cp in sends:
                cp.wait_send()
            mine.wait()

        return start, finish


class Together:
    def __init__(self, *parts):
        self.parts = parts
        self.operands = [p.operand for p in parts]
        self.out_shapes = [p.out_shape for p in parts]
        self.scratch = [s for p in parts for s in p.scratch]

    def bind(self, x_refs, out_refs, sems):
        bound, at = [], 0
        for p, x_ref, out_ref in zip(self.parts, x_refs, out_refs):
            bound.append(p.bind(x_ref, out_ref, *sems[at:at + len(p.scratch)]))
            at += len(p.scratch)

        def start():
            for s, _ in bound:
                s()

        def finish():
            for _, f in bound:
                f()

        return start, finish


def _row_tile(rows):
    return max([t for t in range(16, min(rows, 1024) + 1, 16) if rows % t == 0] or [rows])


def pair_exchange(blocks, name):
    n = len(blocks)

    def body(*refs):
        x_refs, theirs_refs, (send_sems, recv_sems) = refs[:n], refs[n:2 * n], refs[2 * n:]
        me = _place()
        remote = [pltpu.make_async_remote_copy(
            src_ref=x_refs[t].at[2 * q + 1 - me[2]], dst_ref=theirs_refs[t].at[q], send_sem=send_sems.at[4 * t + q],
            recv_sem=recv_sems.at[4 * t + q], device_id=_flip(me, 1), device_id_type=MESH) for t in range(n) for q in range(4)]
        for cp in remote:
            cp.start()
        for cp in remote:
            cp.wait()

    return pl.pallas_call(
        body, name=name, out_shape=[jax.ShapeDtypeStruct((4,) + b.shape[1:], b.dtype) for b in blocks], in_specs=[ANY] * n,
        out_specs=[ANY] * n, scratch_shapes=[pltpu.SemaphoreType.DMA((4 * n,)), pltpu.SemaphoreType.DMA((4 * n,))])(*blocks)


def pair_add(blocks, theirs, name):
    n, rows, width = theirs.shape
    tile = _row_tile(rows)
    spec = pl.BlockSpec((None, tile, width), lambda q, i: (q, i, 0))
    south = pl.BlockSpec((None, None, tile, width), lambda q, i: (q, 0, i, 0))
    north = pl.BlockSpec((None, None, tile, width), lambda q, i: (q, 1, i, 0))

    def body(s_ref, n_ref, b_ref, o_ref):
        mine = jnp.where(lax.axis_index("c") == 0, s_ref[...], n_ref[...])
        o_ref[...] = (mine.astype(f32) + b_ref[...].astype(f32)).astype(o_ref.dtype)

    by_core = blocks.reshape(n, 2, rows, width)
    return pl.pallas_call(body, name=name, grid=(n, rows // tile), in_specs=[south, north, spec], out_specs=spec,
                          out_shape=jax.ShapeDtypeStruct(theirs.shape, theirs.dtype), compiler_params=_params(2))(by_core, by_core, theirs)


def sum_slots(blocks, name):
    n, rows, width = blocks.shape
    tile = _row_tile(rows)

    def body(x_ref, o_ref):
        total = x_ref[0].astype(f32)
        for s in range(1, n):
            total = total + x_ref[s].astype(f32)
        o_ref[...] = total

    return pl.pallas_call(
        body, name=name, grid=(rows // tile,), in_specs=[pl.BlockSpec((n, tile, width), lambda i: (0, i, 0))],
        out_specs=pl.BlockSpec((tile, width), lambda i: (i, 0)), out_shape=_sds((rows, width)), compiler_params=_params(1))(blocks)


def all_reduce_small(x, name, exchange=None):
    rows, width = x.shape
    carried = len(exchange.operands) if exchange else 0

    def body(x_ref, *refs):
        x_refs, refs = refs[:carried], refs[carried:]
        o_ref, refs = refs[0], refs[1:]
        land_refs, (land, send_sems, recv_sems, *sems) = refs[:carried], refs[carried:]
        if exchange:
            start, finish = exchange.bind(x_refs, land_refs, sems)
            start()
        me = _place()
        copies = []
        for k in range(1, N_DEV):
            peer = _flip(me, k)
            copies.append(pltpu.make_async_remote_copy(
                src_ref=x_ref, dst_ref=land.at[_slot(me)], send_sem=send_sems.at[k - 1], recv_sem=recv_sems.at[k - 1],
                device_id=peer, device_id_type=MESH))
        for cp in copies:
            cp.start()
        land[_slot(me)] = x_ref[...]
        for k in range(1, N_DEV):
            peer = _flip(me, k)
            pltpu.make_async_remote_copy(
                src_ref=x_ref, dst_ref=land.at[_slot(peer)], send_sem=send_sems.at[k - 1], recv_sem=recv_sems.at[k - 1],
                device_id=peer, device_id_type=MESH).wait_recv()
        total = land[0]
        for s in range(1, N_DEV):
            total = total + land[s]
        o_ref[...] = total
        for cp in copies:
            cp.wait_send()
        if exchange:
            finish()

    in_vmem = pl.BlockSpec(memory_space=pltpu.VMEM)
    total, *landed = pl.pallas_call(
        body, name=name, out_shape=[_sds((rows, width))] + (exchange.out_shapes if exchange else []),
        in_specs=[in_vmem] + [ANY] * carried, out_specs=[in_vmem] + [ANY] * carried,
        scratch_shapes=[pltpu.VMEM((N_DEV, rows, width), f32), pltpu.SemaphoreType.DMA((7,)), pltpu.SemaphoreType.DMA((7,))]
        + (exchange.scratch if exchange else []),
    )(x, *(exchange.operands if exchange else []))
    return total, landed


def _pack_big(shards):
    packed = {name: shards[name].astype(bf16) for name in COL_SHARDED}
    packed["rows"] = jnp.concatenate([shards[name].astype(bf16) for name, _ in ROW_SHARDED], axis=1)
    return packed


def _unpack_gathered(gathered):
    full = {}
    for name, part in gathered.items():
        if name == "w_gate_up":
            full[name] = part
        elif name in COL_SHARDED:
            full[name] = part.transpose(1, 0, 2).reshape(D_MODEL, N_DEV * part.shape[2])
        else:
            at = 0
            for weight, rows in ROW_SHARDED:
                full[weight] = part[:, at:at + rows, :].reshape(N_DEV * rows, D_MODEL)
                at += rows
    return full


def _pack_grads(grads, group):
    packed = {}
    for name in group:
        if name == "w_gate_up":
            packed[name] = grads[name]
        elif name == "rows":
            packed[name] = jnp.concatenate([grads[weight].reshape(N_DEV, rows, D_MODEL) for weight, rows in ROW_SHARDED], axis=1)
        else:
            packed[name] = grads[name].reshape(D_MODEL, N_DEV, grads[name].shape[1] // N_DEV).transpose(1, 0, 2)
    return packed


def _unpack_shard(layers):
    out = {name: jnp.stack([layer[name] for layer in layers]) for name in COL_SHARDED}
    rows_pack, at = jnp.stack([layer["rows"] for layer in layers]), 0
    for weight, rows in ROW_SHARDED:
        out[weight] = rows_pack[:, at:at + rows, :]
        at += rows
    return out


def _rows_of(flat_len):
    return -(-flat_len // (8 * D_MODEL)) * 8


def _pack_small(parts):
    flat = jnp.concatenate([p.reshape(-1) for p in parts])
    rows = _rows_of(flat.shape[0])
    flat = jnp.pad(flat, (0, rows * D_MODEL - flat.shape[0]))
    return flat.reshape(rows, D_MODEL)


def _unpack_small(packed, like):
    flat, out, at = packed.reshape(-1), [], 0
    for p in like:
        out.append(flat[at:at + p.size].reshape(p.shape))
        at += p.size
    return out


def _rope_tables(positions):
    inv_freq = jnp.float32(ROPE_THETA) ** (-jnp.arange(0, ROPE_DIM, 2, dtype=f32) / ROPE_DIM)
    ang = positions.astype(f32)[:, None] * inv_freq
    cos, sin = jnp.cos(ang), jnp.sin(ang)
    rest = ATTN_HEAD_DIM - ROPE_DIM
    cos_h = jnp.concatenate([cos, cos, jnp.ones((cos.shape[0], rest), f32)], axis=1)
    sin_h = jnp.concatenate([-sin, sin, jnp.zeros((sin.shape[0], rest), f32)], axis=1)
    return jnp.tile(cos_h, (1, ATTN_HEADS)), jnp.tile(sin_h, (1, ATTN_HEADS))


HEAD_SMALL = ("norm_mix_pre", "conv_short", "conv_gdn", "gdn_a_log", "gdn_dt_bias")


def _layer_head(h, p, cos_t, sin_t):
    hn = rms_norm(h, p["norm_mix_pre"][None], "norm_mix_pre")
    aw, cw, gw = ATTN_WIDTH, CONV_WIDTH, GDN_WIDTH
    aq, ak, av, cb, cc, cx, gqkv, ab, gate = _split_cols(_linear(hn, p["w_in"], "w_in"),
                                                         (aw, aw, aw, cw, cw, cw, 3 * gw, 2 * GDN_HEADS, gw))
    ab = jnp.pad(ab, ((0, 0), (0, LANES - 2 * GDN_HEADS)))
    y_attn = dilated_attention(rope(aq, cos_t, sin_t, ATTN_HEAD_DIM ** -0.5, "rope_q"), rope(ak, cos_t, sin_t, 1.0, "rope_k"),
                               av, "attn")
    y_conv = short_conv(cb, cc, cx, p["conv_short"], "short_conv")
    qkv = gdn_pre(gqkv, p["conv_gdn"], "gdn_pre")
    pv = jnp.zeros((8, LANES), f32).at[0, :GDN_HEADS].set(p["gdn_a_log"]).at[1, :GDN_HEADS].set(p["gdn_dt_bias"])
    return (*_split_cols(qkv, (gw, gw, gw)), gate_beta(ab, pv, "gate_beta")), (gate, y_attn, y_conv)


MID_PARAMS = ("gdn_norm", "w_out", "norm_mix_post", "norm_xattn_pre", "w_xq", "norm_mem", "w_xkv", "w_xo", "norm_xattn_post",
              "norm_ffn_pre")


def _layer_mid(h, o, gate, y_attn, y_conv, p, mem):
    y_gdn = gdn_post(o, gate, p["gdn_norm"][None], "gdn_post")
    mix = _joined_linear((y_attn, y_conv, y_gdn), p["w_out"], "w_out")
    h, hn = add_norm_then_norm(h, mix, p["norm_mix_post"][None], p["norm_xattn_pre"][None], "norm_mix_xattn")
    qx = _linear(hn, p["w_xq"], "w_xq")
    kv = _linear(rms_norm(mem, p["norm_mem"][None], "norm_mem"), p["w_xkv"], "w_xkv")
    xa = _linear(cross_attention(qx, kv, "xattn"), p["w_xo"], "w_xo")
    return add_norm_then_norm(h, xa, p["norm_xattn_post"][None], p["norm_ffn_pre"][None], "norm_xattn_ffn")


def _pair_summed(grads, group, name):
    blocks = _pack_grads(grads, group)
    theirs = pair_exchange([blocks[n] for n in group], name + "_pair_exchange")
    return [pair_add(blocks[n], t, f"{name}_pair_add_{n}") for n, t in zip(group, theirs)]


def _forward_backward(x, packed, head_gathered, small, mem, cos_t, sin_t, target):
    def gathers(group, layer):
        return [Gather(packed[n][layer]) for n in group]

    h = x
    saved = []
    for layer in range(DEPTH):
        at_layer = {n: t[layer] for n, t in small.items()}
        head_p = {**_unpack_gathered(dict(zip(HEAD_GROUP, head_gathered))), **{n: at_layer[n] for n in HEAD_SMALL}}
        (rule_in, rest), head_vjp = jax.vjp(lambda h, hp: _layer_head(h, hp, cos_t, sin_t), h, head_p)
        carried = gathers(TAIL_GROUP, layer) + (gathers(HEAD_GROUP, layer + 1) if layer + 1 < DEPTH else [])
        o, s_in, inv, *landed = delta_rule_fwd(*rule_in, "delta_rule_fwd", Together(*carried))
        head_gathered = landed[len(TAIL_GROUP):]
        tail_p = {**_unpack_gathered(dict(zip(TAIL_GROUP, landed))), **at_layer}
        mid_p = {n: tail_p[n] for n in MID_PARAMS}
        (h, hn), mid_vjp = jax.vjp(lambda h, o, rest, mp: _layer_mid(h, o, *rest, mp, mem), h, o, rest, mid_p)
        y, ffn_saved = ffn_forward(hn, tail_p["w_gate_up"], tail_p["w_down"], "ffn")
        h, last_vjp = jax.vjp(lambda h, y, w: add_norm(h, y, w[None], "norm_ffn_post"), h, y, tail_p["norm_ffn_post"])
        saved.append((head_vjp, mid_vjp, last_vjp, ffn_saved, rule_in, s_in, inv))

    loss, dh = jax.value_and_grad(lambda y: loss_rows(y, target, "loss"))(h)

    def summed(group, landed):
        return {n: sum_slots(t, "sum_grads_" + n) for n, t in zip(group, landed)}

    big_grads, small_grads, head_pending = [{} for _ in range(DEPTH)], [None] * DEPTH, []
    for layer in reversed(range(DEPTH)):
        head_vjp, mid_vjp, last_vjp, ffn_saved, rule_in, s_in, inv = saved[layer]
        dh, dy, d_norm_ffn_post = last_vjp(dh)
        dhn, d_gate_up, d_down, landed = ffn_backward(
            ffn_saved, dy, "ffn", Together(*[ChipExchange(t) for t in head_pending]) if head_pending else None)
        if head_pending:
            big_grads[layer + 1].update(summed(HEAD_GROUP, landed))
        dh_mid, do, d_rest, d_mid_p = mid_vjp((dh, dhn))
        d_tail_p = {**d_mid_p, "w_gate_up": d_gate_up, "w_down": d_down, "norm_ffn_post": d_norm_ffn_post}
        carried = Together(*[ChipExchange(t) for t in _pair_summed(d_tail_p, TAIL_GROUP, "tail")])
        *d_rule_in, = delta_rule_bwd(*rule_in, s_in, inv, do, "delta_rule_bwd", carried)
        big_grads[layer].update(summed(TAIL_GROUP, d_rule_in[4:]))
        dh_head, d_head_p = head_vjp((tuple(d_rule_in[:4]), d_rest))
        dh = dh_mid + dh_head
        small_grads[layer] = {n: t for n, t in {**d_head_p, **d_tail_p}.items() if n in small}
        head_pending = _pair_summed(d_head_p, HEAD_GROUP, "head")
    return loss, dh, big_grads, small_grads, head_pending


def kernel(x, mem, positions, norm_mix_pre, norm_mix_post, w_in, conv_short, conv_gdn, gdn_a_log, gdn_dt_bias, gdn_norm, w_out, norm_mem, norm_xattn_pre, norm_xattn_post, w_xq, w_xkv, w_xo, norm_ffn_pre, norm_ffn_post, w_gate_up, w_down, loss_target, m_norm_mix_pre, m_norm_mix_post, m_w_in, m_conv_short, m_conv_gdn, m_gdn_a_log, m_gdn_dt_bias, m_gdn_norm, m_w_out, m_norm_mem, m_norm_xattn_pre, m_norm_xattn_post, m_w_xq, m_w_xkv, m_w_xo, m_norm_ffn_pre, m_norm_ffn_post, m_w_gate_up, m_w_down, v_norm_mix_pre, v_norm_mix_post, v_w_in, v_conv_short, v_conv_gdn, v_gdn_a_log, v_gdn_dt_bias, v_gdn_norm, v_w_out, v_norm_mem, v_norm_xattn_pre, v_norm_xattn_post, v_w_xq, v_w_xkv, v_w_xo, v_norm_ffn_pre, v_norm_ffn_post, v_w_gate_up, v_w_down):
    given = dict(locals())
    weights = {n: given[n] for n in WEIGHTS}
    me = _slot(_place())

    def in_place(shard):
        full = jnp.zeros(shard.shape[:-1] + (shard.shape[-1] * N_DEV,), f32)
        return lax.dynamic_update_slice_in_dim(full, shard, me * shard.shape[-1], axis=shard.ndim - 1)

    placed = [in_place(conv_short), in_place(conv_gdn)]
    packed = _pack_big(weights)
    gathered_conv, head_gathered = all_reduce_small(
        _pack_small(placed), "gather_conv", Together(*[Gather(packed[n][0]) for n in HEAD_GROUP]))
    conv_short_full, conv_gdn_full = _unpack_small(gathered_conv, placed)
    small = {n: weights[n] for n in NORMS + ("gdn_a_log", "gdn_dt_bias", "gdn_norm")}
    small["conv_short"], small["conv_gdn"] = conv_short_full, conv_gdn_full

    cos_t, sin_t = _rope_tables(positions[0])
    loss, grad_x, big_layers, small_layers, head_pending = _forward_backward(
        x[0], packed, head_gathered, small, mem[0], cos_t, sin_t, loss_target[0])

    names = sorted(small)
    parts = [jnp.stack([layer[n] for layer in small_layers]) for n in names] + [loss.reshape(1)]
    reduced, landed = all_reduce_small(_pack_small(parts), "reduce_small", Together(*[ChipExchange(t) for t in head_pending]))
    big_layers[0].update({n: sum_slots(t, "sum_grads_" + n) for n, t in zip(HEAD_GROUP, landed)})
    grads = _unpack_shard(big_layers)
    reduced = _unpack_small(reduced, parts)
    loss = reduced[-1][0]
    for n, g in zip(names, reduced[:-1]):
        if n in ("conv_short", "conv_gdn"):
            width = weights[n].shape[-1]
            g = lax.dynamic_slice_in_dim(g, me * width, width, axis=g.ndim - 1)
        grads[n] = g

    delta, new_m, new_v = {}, {}, {}
    for n in WEIGHTS:
        delta[n], new_m[n], new_v[n] = adamw(weights[n], grads[n], given["m_" + n], given["v_" + n], "adamw_" + n)
    return (loss, grad_x[None], *[grads[n] for n in WEIGHTS], *[delta[n] for n in WEIGHTS],
            *[new_m[n] for n in WEIGHTS], *[new_v[n] for n in WEIGHTS])
```

```python
import functools

import jax
import jax.numpy as jnp
from jax import lax
from jax.experimental import pallas as pl
from jax.experimental.pallas import tpu as pltpu

f32 = jnp.float32
bf16 = jnp.bfloat16
MESH = pl.DeviceIdType.MESH

N_DEV = 8
DEPTH = 4
D_MODEL = 1024
EPS = 1e-6
ATTN_HEADS, ATTN_HEAD_DIM = 4, 64
ATTN_WIDTH = ATTN_HEADS * ATTN_HEAD_DIM
DILATIONS = (1, 4, 16)
QB = 128
ROPE_THETA = 500000.0
ROPE_DIM = ATTN_HEAD_DIM // 4
CONV_WIDTH = 256
GDN_HEADS, GDN_HEAD_DIM = 4, 128
GDN_WIDTH = GDN_HEADS * GDN_HEAD_DIM
GDN_CHUNK = 64
XATTN_HEADS, XATTN_HEAD_DIM = 4, 256
LANES = 128
ROW_TILE = 512
VMEM_LIMIT = 56 * 1024 * 1024

ADAM_LR, ADAM_B1, ADAM_B2, ADAM_EPS, ADAM_WD, ADAM_STEP = 0.001, 0.9, 0.999, 1e-08, 0.01, 10

COL_SHARDED = ("w_in", "w_xkv", "w_gate_up")
ROW_SHARDED = (("w_out", 128), ("w_xq", 128), ("w_xo", 128), ("w_down", 352))
HEAD_GROUP = ("w_in",)
TAIL_GROUP = ("w_gate_up", "w_xkv", "rows")
NORMS = ("norm_mix_pre", "norm_mix_post", "norm_mem", "norm_xattn_pre", "norm_xattn_post", "norm_ffn_pre", "norm_ffn_post")
WEIGHTS = ("norm_mix_pre", "norm_mix_post", "w_in", "conv_short", "conv_gdn", "gdn_a_log", "gdn_dt_bias", "gdn_norm", "w_out",
           "norm_mem", "norm_xattn_pre", "norm_xattn_post", "w_xq", "w_xkv", "w_xo", "norm_ffn_pre", "norm_ffn_post",
           "w_gate_up", "w_down")


def _params(n_grid):
    return pltpu.CompilerParams(dimension_semantics=("arbitrary",) * n_grid, vmem_limit_bytes=VMEM_LIMIT)


def _pick(n, cands):
    for c in cands:
        if n % c == 0:
            return c
    return n


MXU_FLOPS = 9.0e14
HBM_BYTES_PER_S = 2.5e12
VMEM_RMW_BYTES_PER_S = 7.0e12
STEP_S = 0.4e-6
MATMUL_VMEM = 44 * 1024 * 1024


def _tiles(m, n, k, sa, sb, so):
    def divisors(d):
        return sorted({d} | {d // s for s in range(1, d // LANES + 1) if d % s == 0 and (d // s) % LANES == 0}, reverse=True)

    best = None
    for tk in divisors(k):
        nk = k // tk
        for tm in divisors(m):
            for tn_ in divisors(n):
                per_step = tm * tk * sa + tk * tn_ * sb + tm * tn_ * so
                vmem = 2 * per_step + (tm * tn_ * 4 if nk > 1 else 0)
                vmem += (tm * tk * 2 if sa == 4 else 0) + (tk * tn_ * 2 if sb == 4 else 0) + tm * tn_ * 4
                if vmem > MATMUL_VMEM:
                    continue
                moved = m * k * sa * (1 if nk == 1 else n // tn_) + k * n * sb * (1 if nk == 1 and n == tn_ else m // tm) + m * n * so
                busy = 2 * m * n * k / MXU_FLOPS + (m * n * 8 * nk / VMEM_RMW_BYTES_PER_S if nk > 1 else 0)
                cost = max(moved / HBM_BYTES_PER_S, busy) + per_step / HBM_BYTES_PER_S + (m // tm) * (n // tn_) * nk * STEP_S
                if best is None or cost < best[0]:
                    best = (cost, tm, tn_, tk)
    return best[1:]


def _mm(a, b, ta, tb, out_dtype, name):
    m, k = (a.shape[1], a.shape[0]) if ta else a.shape
    n = b.shape[0] if tb else b.shape[1]
    tm, tn, tk = _tiles(m, n, k, a.dtype.itemsize, b.dtype.itemsize, jnp.dtype(out_dtype).itemsize)
    nk = k // tk
    a_spec = pl.BlockSpec((tk, tm), lambda i, j, kk: (kk, i)) if ta else pl.BlockSpec((tm, tk), lambda i, j, kk: (i, kk))
    b_spec = pl.BlockSpec((tn, tk), lambda i, j, kk: (j, kk)) if tb else pl.BlockSpec((tk, tn), lambda i, j, kk: (kk, j))
    dims = (((0 if ta else 1,), (1 if tb else 0,)), ((), ()))

    def body(a_ref, b_ref, o_ref, *acc):
        kk = pl.program_id(2)
        p = lax.dot_general(a_ref[...].astype(bf16), b_ref[...].astype(bf16), dims, preferred_element_type=f32)
        if nk == 1:
            o_ref[...] = p.astype(o_ref.dtype)
            return
        acc_ref, = acc

        @pl.when(kk == 0)
        def _():
            acc_ref[...] = p

        @pl.when(kk > 0)
        def _():
            acc_ref[...] += p

        @pl.when(kk == nk - 1)
        def _():
            o_ref[...] = acc_ref[...].astype(o_ref.dtype)

    return pl.pallas_call(
        body, name=name, grid=(m // tm, n // tn, nk), in_specs=[a_spec, b_spec],
        out_specs=pl.BlockSpec((tm, tn), lambda i, j, kk: (i, j)), out_shape=jax.ShapeDtypeStruct((m, n), out_dtype),
        scratch_shapes=[pltpu.VMEM((tm, tn), f32)] if nk > 1 else [], compiler_params=_params(3))(a, b)


def _linear(x, w, name):
    @jax.custom_vjp
    def lin(x, w):
        return _mm(x, w, False, False, f32, name + "_y")

    def lin_f(x, w):
        return _mm(x, w, False, False, f32, name + "_y"), (x, w)

    def lin_b(res, dy):
        x, w = res
        return _mm(dy, w, False, True, f32, name + "_dx"), _mm(x, dy, True, False, bf16, name + "_dw")

    lin.defvjp(lin_f, lin_b)
    return lin(x, w)


def _bdot(a, b, form):
    dims = {"nn": ((1,), (0,)), "nt": ((1,), (1,)), "tn": ((0,), (0,))}[form]
    return lax.dot_general(a.astype(bf16), b.astype(bf16), (dims, ((), ())), preferred_element_type=f32)


def ffn_forward(hn, w_gate_up, w_down, name):
    s, k = hn.shape
    n_blocks, _, width = w_gate_up.shape
    half = n_blocks // 2
    tm = 512
    blocked = jax.ShapeDtypeStruct((half, s, width), bf16)

    def act_body(x_ref, wg_ref, wu_ref, gate_ref, up_ref, act_ref):
        x = x_ref[...]
        gate, up = _bdot(x, wg_ref[...], "nn"), _bdot(x, wu_ref[...], "nn")
        gate_ref[...], up_ref[...] = gate.astype(bf16), up.astype(bf16)
        act_ref[...] = (jax.nn.silu(gate) * up).astype(bf16)

    tile = pl.BlockSpec((None, tm, width), lambda i, d: (d, i, 0))
    gate, up, act = pl.pallas_call(
        act_body, name=name + "_act", grid=(s // tm, half),
        in_specs=[pl.BlockSpec((tm, k), lambda i, d: (i, 0)), pl.BlockSpec((None, k, width), lambda i, d: (d, 0, 0)),
                  pl.BlockSpec((None, k, width), lambda i, d: (d + half, 0, 0))],
        out_specs=[tile] * 3, out_shape=[blocked] * 3, compiler_params=_params(2))(hn, w_gate_up, w_gate_up)

    n = w_down.shape[1]
    tn = 512

    def y_body(act_ref, w_ref, y_ref):
        y_ref[...] = sum(_bdot(act_ref[d], w_ref[d * width:(d + 1) * width, :], "nn") for d in range(half))

    y = pl.pallas_call(
        y_body, name=name + "_y", grid=(s // tm, n // tn),
        in_specs=[pl.BlockSpec((half, tm, width), lambda i, j: (0, i, 0)), pl.BlockSpec((half * width, tn), lambda i, j: (0, j))],
        out_specs=pl.BlockSpec((tm, tn), lambda i, j: (i, j)), out_shape=_sds((s, n)), compiler_params=_params(2))(act, w_down)
    return y, (hn, w_gate_up, w_down, gate, up, act)


def ffn_backward(saved, dy, name, exchange=None):
    hn, w_gate_up, w_down, gate, up, act = saved
    s, k = hn.shape
    n_blocks, _, width = w_gate_up.shape
    half = n_blocks // 2
    n = w_down.shape[1]
    tm = 512
    blocked = jax.ShapeDtypeStruct((half, s, width), bf16)
    carried = len(exchange.operands) if exchange else 0
    steps = (s // tm, half)

    def dact_body(dy_ref, w_ref, gate_ref, up_ref, *refs):
        x_refs, refs = refs[:carried], refs[carried:]
        (dgate_ref, dup_ref), refs = refs[:2], refs[2:]
        if exchange:
            at = pl.program_id(0) * steps[1] + pl.program_id(1)
            start, wait = exchange.bind(x_refs, refs[:carried], refs[carried:])
            pl.when(at == 0)(start)
        d_act = _bdot(dy_ref[...], w_ref[...], "nt")
        g, u = gate_ref[...].astype(f32), up_ref[...].astype(f32)
        sig = jax.nn.sigmoid(g)
        dgate_ref[...] = (d_act * u * sig * (1.0 + g * (1.0 - sig))).astype(bf16)
        dup_ref[...] = (d_act * g * sig).astype(bf16)
        if exchange:
            pl.when(at == steps[0] * steps[1] - 1)(wait)

    tile = pl.BlockSpec((None, tm, width), lambda i, d: (d, i, 0))
    d_gate, d_up, *landed = pl.pallas_call(
        dact_body, name=name + "_dact", grid=steps,
        in_specs=[pl.BlockSpec((tm, n), lambda i, d: (i, 0)), pl.BlockSpec((width, n), lambda i, d: (d, 0)), tile, tile] + [ANY] * carried,
        out_specs=[tile, tile] + [ANY] * carried, out_shape=[blocked, blocked] + (exchange.out_shapes if exchange else []),
        scratch_shapes=exchange.scratch if exchange else [],
        compiler_params=_params(2))(dy, w_down, gate, up, *(exchange.operands if exchange else []))

    def dx_body(dg_ref, du_ref, w_ref, dx_ref):
        dx_ref[...] = sum(_bdot(dg_ref[d], w_ref[d], "nt") + _bdot(du_ref[d], w_ref[d + half], "nt") for d in range(half))

    tx = 256
    rows = pl.BlockSpec((half, tx, width), lambda i: (0, i, 0))
    dx = pl.pallas_call(
        dx_body, name=name + "_dx", grid=(s // tx,), in_specs=[rows, rows, _whole(w_gate_up.shape)],
        out_specs=pl.BlockSpec((tx, k), lambda i: (i, 0)), out_shape=_sds((s, k)), compiler_params=_params(1))(d_gate, d_up, w_gate_up)

    def dw1_body(x_ref, dg_ref, du_ref, dw_ref):
        d_block = jnp.where(pl.program_id(0) < half, dg_ref[...], du_ref[...])
        dw_ref[...] = _bdot(x_ref[...], d_block, "tn").astype(bf16)

    d_w_gate_up = pl.pallas_call(
        dw1_body, name=name + "_dw1", grid=(n_blocks,),
        in_specs=[_whole((s, k)), pl.BlockSpec((None, s, width), lambda b: (jnp.minimum(b, half - 1), 0, 0)),
                  pl.BlockSpec((None, s, width), lambda b: (jnp.maximum(b - half, 0), 0, 0))],
        out_specs=pl.BlockSpec((None, k, width), lambda b: (b, 0, 0)), out_shape=jax.ShapeDtypeStruct(w_gate_up.shape, bf16),
        compiler_params=_params(1))(hn, d_gate, d_up)

    tn = 512

    def dw2_body(act_ref, dy_ref, dw_ref):
        dw_ref[...] = _bdot(act_ref[...], dy_ref[...], "tn").astype(bf16)

    d_w_down = pl.pallas_call(
        dw2_body, name=name + "_dw2", grid=(half, n // tn),
        in_specs=[pl.BlockSpec((None, s, width), lambda d, j: (d, 0, 0)), pl.BlockSpec((s, tn), lambda d, j: (0, j))],
        out_specs=pl.BlockSpec((width, tn), lambda d, j: (d, j)), out_shape=jax.ShapeDtypeStruct(w_down.shape, bf16),
        compiler_params=_params(2))(act, dy)
    return dx, d_w_gate_up, d_w_down, landed


def _joined_linear(parts, w, name):
    s, n = parts[0].shape[0], w.shape[1]
    widths = [p.shape[1] for p in parts]
    edges = [sum(widths[:i]) for i in range(len(widths) + 1)]
    spans = list(zip(edges[:-1], edges[1:]))
    tm, tn = 512, 512

    def forward(*args):
        *xs, w = args

        def y_body(*refs):
            *x_refs, w_ref, y_ref = refs
            y_ref[...] = sum(_bdot(x_ref[...], w_ref[a:b, :], "nn") for x_ref, (a, b) in zip(x_refs, spans))

        y = pl.pallas_call(
            y_body, name=name + "_y", grid=(s // tm, n // tn),
            in_specs=[pl.BlockSpec((tm, k), lambda i, j: (i, 0)) for k in widths] + [pl.BlockSpec((edges[-1], tn), lambda i, j: (0, j))],
            out_specs=pl.BlockSpec((tm, tn), lambda i, j: (i, j)), out_shape=_sds((s, n)), compiler_params=_params(2))(*xs, w)
        return y, args

    def backward(args, dy):
        *xs, w = args

        def dx_body(dy_ref, w_ref, *dx_refs):
            d_all = _bdot(dy_ref[...], w_ref[...], "nt")
            for dx_ref, (a, b) in zip(dx_refs, spans):
                dx_ref[...] = d_all[:, a:b]

        dxs = pl.pallas_call(
            dx_body, name=name + "_dx", grid=(s // tm,), in_specs=[pl.BlockSpec((tm, n), lambda i: (i, 0)), _whole(w.shape)],
            out_specs=[pl.BlockSpec((tm, k), lambda i: (i, 0)) for k in widths], out_shape=[_sds((s, k)) for k in widths],
            compiler_params=_params(1))(dy, w)

        def dw_body(*refs):
            *x_refs, dy_ref, dw_ref = refs
            dy_tile = dy_ref[...]
            dw_ref[...] = jnp.concatenate([_bdot(x_ref[...], dy_tile, "tn") for x_ref in x_refs], axis=0).astype(bf16)

        dw = pl.pallas_call(
            dw_body, name=name + "_dw", grid=(n // tn,),
            in_specs=[_whole((s, k)) for k in widths] + [pl.BlockSpec((s, tn), lambda j: (0, j))],
            out_specs=pl.BlockSpec((edges[-1], tn), lambda j: (0, j)), out_shape=jax.ShapeDtypeStruct(w.shape, bf16),
            compiler_params=_params(1))(*xs, dy)
        return (*dxs, dw)

    @jax.custom_vjp
    def op(*args):
        return forward(*args)[0]

    op.defvjp(forward, backward)
    return op(*parts, w)


def _split_cols(x, widths):
    edges = [sum(widths[:i]) for i in range(len(widths) + 1)]

    def cut(x):
        return tuple(x[:, a:b] for a, b in zip(edges[:-1], edges[1:]))

    @jax.custom_vjp
    def split(x):
        return cut(x)

    split.defvjp(lambda x: (cut(x), None), lambda _, cts: (jnp.concatenate(cts, axis=1),))
    return split(x)


def _block_op(name, f, grid, in_specs, out_defs, arrays, diff, acc=None, gdefs=None):
    acc, gdefs = acc or {}, gdefs or {}
    n_in, n_out, n_grid = len(in_specs), len(out_defs), len(grid)

    def fwd_call(*xs):
        def body(*refs):
            outs = f(*[r[...] for r in refs[:n_in]])
            for r, o in zip(refs[n_in:], outs):
                r[...] = o.astype(r.dtype)

        return pl.pallas_call(
            body, name=name + "_fwd", grid=grid, in_specs=in_specs, out_specs=[d[1] for d in out_defs],
            out_shape=[d[0] for d in out_defs], compiler_params=_params(n_grid))(*xs)

    def bwd_call(*xs_and_cts):
        def body(*refs):
            xs = [r[...] for r in refs[:n_in]]
            cts = tuple(r[...] for r in refs[n_in:n_in + n_out])

            def of_diff(*dx):
                full = list(xs)
                for i, v in zip(diff, dx):
                    full[i] = v
                return tuple(f(*full))

            _, vjp = jax.vjp(of_diff, *[xs[i] for i in diff])
            grads = vjp(cts)
            for i, g, r in zip(diff, grads, refs[n_in + n_out:]):
                if i in acc:
                    first = functools.reduce(jnp.logical_and, [pl.program_id(a) == 0 for a in acc[i]])

                    @pl.when(first)
                    def _(r=r):
                        r[...] = jnp.zeros_like(r)

                    r[...] += g.astype(r.dtype)
                else:
                    r[...] = g.astype(r.dtype)

        g_defs = [gdefs.get(i, (jax.ShapeDtypeStruct(arrays[i].shape, f32), in_specs[i])) for i in diff]
        return pl.pallas_call(
            body, name=name + "_bwd", grid=grid, in_specs=list(in_specs) + [d[1] for d in out_defs],
            out_specs=[d[1] for d in g_defs], out_shape=[d[0] for d in g_defs], compiler_params=_params(n_grid))(*xs_and_cts)

    return fwd_call, bwd_call


def _simple_op(name, f, grid, in_specs, out_defs, arrays, diff, acc=None):
    fwd_call, bwd_call = _block_op(name, f, grid, in_specs, out_defs, arrays, diff, acc)

    @jax.custom_vjp
    def op(*xs):
        return tuple(fwd_call(*xs))

    def op_f(*xs):
        return tuple(fwd_call(*xs)), xs

    def op_b(xs, cts):
        grads = bwd_call(*xs, *cts)
        out = [jnp.zeros_like(x) for x in xs]
        for i, g in zip(diff, grads):
            out[i] = g
        return tuple(out)

    op.defvjp(op_f, op_b)
    return op(*arrays)


def _rows(width, tile=ROW_TILE):
    return pl.BlockSpec((tile, width), lambda i: (i, 0))


def _whole(shape):
    return pl.BlockSpec(shape, lambda *_: (0,) * len(shape))


def _sds(shape):
    return jax.ShapeDtypeStruct(shape, f32)


def _rms(x, w):
    return x * lax.rsqrt(jnp.mean(x * x, axis=-1, keepdims=True) + EPS) * w


def rms_norm(x, w, name):
    r, d = x.shape
    tile = min(ROW_TILE, r)
    return _simple_op(name, lambda x, w: (_rms(x, w),), (r // tile,), [_rows(d, tile), _whole((1, d))],
                      [(_sds((r, d)), _rows(d, tile))], (x, w), (0, 1), {1: (0,)})[0]


def add_norm(h, y, w, name):
    r, d = h.shape
    return _simple_op(name, lambda h, y, w: (h + _rms(y, w),), (r // ROW_TILE,), [_rows(d), _rows(d), _whole((1, d))],
                      [(_sds((r, d)), _rows(d))], (h, y, w), (0, 1, 2), {2: (0,)})[0]


def add_norm_then_norm(h, y, w_post, w_pre, name):
    r, d = h.shape

    def f(h, y, w_post, w_pre):
        h_new = h + _rms(y, w_post)
        return h_new, _rms(h_new, w_pre)

    return _simple_op(name, f, (r // ROW_TILE,), [_rows(d), _rows(d), _whole((1, d)), _whole((1, d))],
                      [(_sds((r, d)), _rows(d))] * 2, (h, y, w_post, w_pre), (0, 1, 2, 3), {2: (0,), 3: (0,)})


def _swap8(x):
    def raw(x):
        lane = lax.broadcasted_iota(jnp.int32, x.shape, 1) % ATTN_HEAD_DIM
        half = ROPE_DIM // 2
        up = pltpu.roll(x, x.shape[1] - half, axis=1)
        down = pltpu.roll(x, half, axis=1)
        return jnp.where(lane < half, up, jnp.where(lane < ROPE_DIM, down, 0.0))

    @jax.custom_vjp
    def swap(x):
        return raw(x)

    swap.defvjp(lambda x: (raw(x), None), lambda _, g: (raw(g),))
    return swap(x)


def rope(x, cos_t, sin_t, scale, name):
    r, d = x.shape
    return _simple_op(name, lambda x, c, s: ((x * c + _swap8(x) * s) * scale,), (r // ROW_TILE,), [_rows(d)] * 3,
                      [(_sds((r, d)), _rows(d))], (x, cos_t, sin_t), (0,))[0]


def _shift_rows(x, k):
    n = x.shape[0]

    def down(x):
        row = lax.broadcasted_iota(jnp.int32, x.shape, 0)
        return jnp.where(row >= k, pltpu.roll(x, k, axis=0), 0.0)

    def up(x):
        row = lax.broadcasted_iota(jnp.int32, x.shape, 0)
        return jnp.where(row < n - k, pltpu.roll(x, n - k, axis=0), 0.0)

    @jax.custom_vjp
    def shift(x):
        return down(x)

    shift.defvjp(lambda x: (down(x), None), lambda _, g: (up(g),))
    return shift(x)


def _causal_conv(x, w):
    taps = w.shape[0]
    y = x * w[taps - 1:taps, :]
    for j in range(taps - 1):
        y = y + _shift_rows(x, taps - 1 - j) * w[j:j + 1, :]
    return y


def _cols(rows, at=0):
    return pl.BlockSpec((rows, LANES), lambda j: (0, at + j))


def short_conv(cb, cc, cx, w, name):
    s, c = cb.shape
    taps = w.shape[0]
    return _simple_op(name, lambda b, c_, x, w: (b * _causal_conv(c_ * x, w),), (c // LANES,),
                      [_cols(s)] * 3 + [_cols(taps)], [(_sds((s, c)), _cols(s))], (cb, cc, cx, w), (0, 1, 2, 3))[0]


def gdn_pre(qkv, w, name):
    s, c = qkv.shape
    taps = w.shape[0]

    def f(x, w):
        j = pl.program_id(0)
        y = jax.nn.silu(_causal_conv(x, w))
        normed = y * lax.rsqrt(jnp.sum(y * y, axis=-1, keepdims=True) + EPS)
        scale = jnp.where(j < GDN_HEADS, GDN_HEAD_DIM ** -0.5, 1.0).astype(f32)
        return (jnp.where(j < 2 * GDN_HEADS, normed * scale, y),)

    return _simple_op(name, f, (c // LANES,), [_cols(s), _cols(taps)], [(_sds((s, c)), _cols(s))], (qkv, w), (0, 1))[0]


def gate_beta(ab, pv, name):
    s = ab.shape[0]

    def f(ab, pv):
        lane = lax.broadcasted_iota(jnp.int32, ab.shape, 1)
        g = -jnp.exp(pv[0:1, :]) * jax.nn.softplus(ab + pv[1:2, :])
        return (jnp.where(lane < GDN_HEADS, g, jnp.where(lane < 2 * GDN_HEADS, jax.nn.sigmoid(ab), 0.0)),)

    return _simple_op(name, f, (s // ROW_TILE,), [_rows(LANES), _whole((8, LANES))], [(_sds((s, LANES)), _rows(LANES))],
                      (ab, pv), (0, 1), {1: (0,)})[0]


def gdn_post(o, gate, w, name):
    s, c = o.shape

    def f(o, g, w):
        heads = [slice(hd * LANES, (hd + 1) * LANES) for hd in range(c // LANES)]
        return (jnp.concatenate([_rms(o[:, hd], w) * jax.nn.silu(g[:, hd]) for hd in heads], axis=1),)

    return _simple_op(name, f, (s // ROW_TILE,), [_rows(c), _rows(c), _whole((1, LANES))], [(_sds((s, c)), _rows(c))],
                      (o, gate, w), (0, 1, 2), {2: (0,)})[0]


def attn_merge(outs, lses, name):
    s, c = outs[0].shape

    def f(o1, o2, o3, l1, l2, l3):
        m = lax.stop_gradient(jnp.maximum(jnp.maximum(l1, l2), l3))
        e1, e2, e3 = jnp.exp(l1 - m), jnp.exp(l2 - m), jnp.exp(l3 - m)
        return ((e1 * o1 + e2 * o2 + e3 * o3) / (e1 + e2 + e3),)

    return _simple_op(name, f, (s // ROW_TILE,), [_rows(c)] * 6, [(_sds((s, c)), _rows(c))], (*outs, *lses), tuple(range(6)))[0]


def loss_rows(y, target, name):
    s, d = y.shape
    nt = s // ROW_TILE

    def f(y, t):
        e = y - t
        part = 0.5 * jnp.sum(jnp.mean(e * e, axis=-1, keepdims=True), axis=0, keepdims=True)
        return (jnp.broadcast_to(part * (1.0 / (8 * LANES)), (8, LANES)),)

    out = _simple_op(name, f, (nt,), [_rows(d)] * 2, [(_sds((nt * 8, LANES)), pl.BlockSpec((8, LANES), lambda i: (i, 0)))],
                     (y, target), (0,))[0]
    return jnp.sum(out)


def _mxu(a, b, form):
    dims = {"nn": ((1,), (0,)), "nt": ((1,), (1,)), "tn": ((0,), (0,))}

    def raw(a, b, form):
        return lax.dot_general(a.astype(bf16), b.astype(bf16), (dims[form], ((), ())), preferred_element_type=f32)

    @jax.custom_vjp
    def prod(a, b):
        return raw(a, b, form)

    def prod_b(res, ct):
        a, b = res
        if form == "nn":
            return raw(ct, b, "nt"), raw(a, ct, "tn")
        if form == "nt":
            return raw(ct, b, "nn"), raw(ct, a, "tn")
        return raw(b, ct, "nt"), raw(a, ct, "nn")

    prod.defvjp(lambda a, b: (raw(a, b, form), (a, b)), prod_b)
    return prod(a, b)


def _masked_heads_attention(q, keys, values, seen):
    dh = ATTN_HEAD_DIM
    outs, lses = [], []
    for hd in range(q.shape[1] // dh):
        at = slice(hd * dh, (hd + 1) * dh)
        sc = jnp.where(seen, _mxu(q[:, at], keys[:, at], "nt"), -jnp.inf)
        m = lax.stop_gradient(jnp.max(sc, axis=-1, keepdims=True))
        p = jnp.exp(sc - m)
        l = jnp.sum(p, axis=-1, keepdims=True)
        outs.append(_mxu(p / l, values[:, at], "nn"))
        lses.append(jnp.broadcast_to(m + jnp.log(l), (q.shape[0], dh)))
    return jnp.concatenate(outs, axis=1), jnp.concatenate(lses, axis=1)


def band_attention(q, k, v, nb, name):
    r, qb, width = q.shape

    def f(q, kp, kc, vp, vc):
        has_prev = (pl.program_id(0) % nb) > 0
        i = lax.broadcasted_iota(jnp.int32, (qb, 2 * qb), 0)
        j = lax.broadcasted_iota(jnp.int32, (qb, 2 * qb), 1)
        seen = jnp.logical_or(jnp.logical_and(jnp.logical_and(j < qb, j >= i), has_prev), jnp.logical_and(j >= qb, j - qb <= i))
        return _masked_heads_attention(q, jnp.concatenate([kp, kc], axis=0), jnp.concatenate([vp, vc], axis=0), seen)

    blk = (None, qb, width)
    cur = pl.BlockSpec(blk, lambda b: (b, 0, 0))
    prev = pl.BlockSpec(blk, lambda b: (jnp.maximum(b - 1, 0), 0, 0))
    shape = _sds((r, qb, width))
    fwd_call, bwd_call = _block_op(name, f, (r,), [cur, prev, cur, prev, cur], [(shape, cur), (shape, cur)],
                                   (q, k, k, v, v), (0, 1, 2, 3, 4), gdefs={1: (shape, cur), 3: (shape, cur)})

    def to_prev(g):
        return jnp.concatenate([g[1:], jnp.zeros_like(g[:1])], axis=0)

    @jax.custom_vjp
    def op(q, k, v):
        return tuple(fwd_call(q, k, k, v, v))

    def op_b(res, cts):
        q, k, v = res
        dq, dkp, dkc, dvp, dvc = bwd_call(q, k, k, v, v, *cts)
        return dq, dkc + to_prev(dkp), dvc + to_prev(dvp)

    op.defvjp(lambda q, k, v: (tuple(fwd_call(q, k, k, v, v)), (q, k, v)), op_b)
    return op(q, k, v)


def dilated_attention(q, k, v, name):
    s = q.shape[0]
    outs, lses = [], []
    for d in DILATIONS:
        length = s // d
        nb = length // QB
        def to_residue(t):
            return t.reshape(length, d, ATTN_WIDTH).transpose(1, 0, 2).reshape(d * nb, QB, ATTN_WIDTH)

        def from_residue(t):
            return t.reshape(d, length, ATTN_WIDTH).transpose(1, 0, 2).reshape(s, ATTN_WIDTH)

        o, lse = band_attention(to_residue(q), to_residue(k), to_residue(v), nb, f"{name}_d{d}")
        outs.append(from_residue(o))
        lses.append(from_residue(lse))
    return attn_merge(outs, lses, name + "_merge")


def cross_attention(q, kv, name):
    s = q.shape[0]
    m = kv.shape[0]
    width = XATTN_HEADS * XATTN_HEAD_DIM
    tq = 512

    def f(q, k, v):
        sc = _mxu(q, k, "nt") * (XATTN_HEAD_DIM ** -0.5)
        mx = lax.stop_gradient(jnp.max(sc, axis=-1, keepdims=True))
        p = jnp.exp(sc - mx)
        return (_mxu(p / jnp.sum(p, axis=-1, keepdims=True), v, "nn"),)

    q_spec = pl.BlockSpec((tq, XATTN_HEAD_DIM), lambda a, i: (i, a))
    k_spec = pl.BlockSpec((m, XATTN_HEAD_DIM), lambda a, i: (0, a))
    v_spec = pl.BlockSpec((m, XATTN_HEAD_DIM), lambda a, i: (0, a + XATTN_HEADS))
    half = _sds((m, width))
    fwd_call, bwd_call = _block_op(name, f, (XATTN_HEADS, s // tq), [q_spec, k_spec, v_spec], [(_sds((s, width)), q_spec)],
                                   (q, kv, kv), (0, 1, 2), acc={1: (1,), 2: (1,)}, gdefs={1: (half, k_spec), 2: (half, k_spec)})

    @jax.custom_vjp
    def op(q, kv):
        return fwd_call(q, kv, kv)[0]

    def op_b(res, ct):
        q, kv = res
        dq, dk, dv = bwd_call(q, kv, kv, ct)
        return dq, jnp.concatenate([dk, dv], axis=1)

    op.defvjp(lambda q, kv: (fwd_call(q, kv, kv)[0], (q, kv)), op_b)
    return op(q, kv)


def _hi(a, b, form="nn"):
    dims = {"nn": ((1,), (0,)), "nt": ((1,), (1,)), "tn": ((0,), (0,))}[form]
    return lax.dot_general(a, b, (dims, ((), ())), precision=lax.Precision.HIGH, preferred_element_type=f32)


def _running_sum(g):
    def raw(x, form):
        c = x.shape[0]
        tri = (lax.broadcasted_iota(jnp.int32, (c, c), 0) >= lax.broadcasted_iota(jnp.int32, (c, c), 1)).astype(bf16)
        hi = x.astype(bf16)
        rest = x - hi.astype(f32)
        mid = rest.astype(bf16)
        low = (rest - mid.astype(f32)).astype(bf16)
        dims = (((1,) if form == "nn" else (0,), (0,)), ((), ()))
        return sum(lax.dot_general(tri, part, dims, preferred_element_type=f32) for part in (hi, mid, low))

    @jax.custom_vjp
    def run(x):
        return raw(x, "nn")

    run.defvjp(lambda x: (raw(x, "nn"), None), lambda _, ct: (raw(ct, "tn"),))
    return run(g)


def _unit_lower_inverse(a):
    c = a.shape[0]
    eye = (lax.broadcasted_iota(jnp.int32, (c, c), 0) == lax.broadcasted_iota(jnp.int32, (c, c), 1)).astype(f32)
    inv, power = eye - a, -a
    for _ in range(c.bit_length() - 2):
        power = _hi(power, power)
        inv = inv + _hi(inv, power)
    return inv


def _known_inverse(a, t):
    @jax.custom_vjp
    def inv(a, t):
        return t

    def inv_b(t, ct):
        return -_hi(_hi(t, ct, "tn"), t, "nt"), jnp.zeros_like(t)

    inv.defvjp(lambda a, t: (t, t), inv_b)
    return inv(a, t)


def _delta_chunk(q, k, v, g, beta, s0, known_inv=None):
    c = q.shape[0]
    i = lax.broadcasted_iota(jnp.int32, (c, c), 0)
    j = lax.broadcasted_iota(jnp.int32, (c, c), 1)
    causal, strict = i >= j, i > j
    dec = _running_sum(g)
    dec_i = dec[:, :c]
    rel = jnp.exp(jnp.where(causal, dec_i - dec_i.T, -jnp.inf))
    k_beta = k * beta
    on_k = _mxu(jnp.concatenate([k_beta, q], axis=0), k, "nt")
    a = jnp.where(strict, on_k[:c] * rel, 0.0)
    attn = jnp.where(causal, on_k[c:] * rel, 0.0)
    inv = _unit_lower_inverse(a) if known_inv is None else _known_inverse(a, known_inv)
    e_dec = jnp.exp(dec)
    solved = _hi(inv, jnp.concatenate([v * beta, k_beta * e_dec], axis=1))
    u, w = solved[:, :v.shape[1]], solved[:, v.shape[1]:]
    total = jnp.sum(g, axis=0, keepdims=True)
    on_state = _mxu(jnp.concatenate([w, q * e_dec], axis=0), s0, "nn")
    v_new = u - on_state[:c]
    o = on_state[c:] + _mxu(attn, v_new, "nn")
    s1 = s0 * jnp.exp(total) + _mxu(k * jnp.exp(total - dec), v_new, "tn")
    return o, s1, inv


def _delta_rule_call(name, walk, n, in_specs, out_specs, out_shape, operands, exchange):
    n_in, n_out = len(in_specs), len(out_specs)
    carried = len(exchange.operands) if exchange else 0

    def body(*refs):
        ins, refs = refs[:n_in], refs[n_in:]
        x_refs, refs = refs[:carried], refs[carried:]
        outs, refs = refs[:n_out], refs[n_out:]
        land_refs, (state, *sems) = refs[:carried], refs[carried:]
        step = pl.program_id(0)
        if exchange:
            start, finish = exchange.bind(x_refs, land_refs, sems)
            pl.when(step == 0)(start)

        @pl.when(step == 0)
        def _():
            state[...] = jnp.zeros_like(state)

        walk(ins, outs, state)
        if exchange:
            pl.when(step == n - 1)(finish)

    return pl.pallas_call(
        body, name=name, grid=(n,), in_specs=list(in_specs) + [ANY] * carried, out_specs=list(out_specs) + [ANY] * carried,
        out_shape=list(out_shape) + (exchange.out_shapes if exchange else []),
        scratch_shapes=[pltpu.VMEM((GDN_HEAD_DIM, GDN_WIDTH), f32)] + (exchange.scratch if exchange else []),
        compiler_params=_params(1))(*operands, *(exchange.operands if exchange else []))


def _delta_heads():
    heads = [slice(hd * GDN_HEAD_DIM, (hd + 1) * GDN_HEAD_DIM) for hd in range(GDN_HEADS)]
    inv_at = [slice(hd * GDN_CHUNK, (hd + 1) * GDN_CHUNK) for hd in range(GDN_HEADS)]
    return heads, inv_at


def _head_chunk(q, k, v, gates, s0, head, known_inv=None):
    g = jnp.broadcast_to(gates[:, head:head + 1], q.shape)
    beta = jnp.broadcast_to(gates[:, GDN_HEADS + head:GDN_HEADS + head + 1], q.shape)
    return _delta_chunk(q, k, v, g, beta, s0, known_inv)


def delta_rule_fwd(q, k, v, gates, name, exchange=None):
    s, width = q.shape
    c, dk = GDN_CHUNK, GDN_HEAD_DIM
    n = s // c
    heads, inv_at = _delta_heads()

    def walk(ins, outs, state):
        q_ref, k_ref, v_ref, gates_ref = ins
        o_ref, s_in_ref, inv_ref = outs
        s_in_ref[...] = state[...]
        gates = gates_ref[...]
        xs = [[r[:, hd] for r in (q_ref, k_ref, v_ref)] + [gates, state[:, hd], i] for i, hd in enumerate(heads)]
        ys = [_head_chunk(*x) for x in xs]
        for hd, at, (o, s1, inv) in zip(heads, inv_at, ys):
            o_ref[:, hd], state[:, hd], inv_ref[:, at] = o, s1, inv

    blk = pl.BlockSpec((c, width), lambda t: (t, 0))
    gt = pl.BlockSpec((c, LANES), lambda t: (t, 0))
    st = pl.BlockSpec((dk, width), lambda t: (t, 0))
    iv = pl.BlockSpec((c, GDN_HEADS * c), lambda t: (t, 0))
    return _delta_rule_call(name, walk, n, [blk] * 3 + [gt], [blk, st, iv],
                            [_sds((s, width)), _sds((n * dk, width)), _sds((s, GDN_HEADS * c))], (q, k, v, gates), exchange)


def delta_rule_bwd(q, k, v, gates, s_in, inv, do, name, exchange=None):
    s, width = q.shape
    c, dk = GDN_CHUNK, GDN_HEAD_DIM
    n = s // c
    heads, inv_at = _delta_heads()

    def walk(ins, outs, dstate):
        q_ref, k_ref, v_ref, gates_ref, s_ref, inv_ref, do_ref = ins
        dq_ref, dk_ref, dv_ref, dgates_ref = outs
        gates = gates_ref[...]
        xs = [[r[:, hd] for r in (q_ref, k_ref, v_ref)] + [gates, s_ref[:, hd]] for hd in heads]
        known = [inv_ref[:, at] for at in inv_at]
        cts = [(do_ref[:, hd], dstate[:, hd]) for hd in heads]
        grads = []
        for i, (x, t, ct) in enumerate(zip(xs, known, cts)):
            _, vjp = jax.vjp(lambda *y, t=t, i=i: _head_chunk(*y, i, known_inv=t)[:2], *x)
            grads.append(vjp(ct))
        dgates = grads[0][3]
        for g in grads[1:]:
            dgates = dgates + g[3]
        dgates_ref[...] = dgates
        for hd, (dq, dk_, dv, _, ds0) in zip(heads, grads):
            dq_ref[:, hd], dk_ref[:, hd], dv_ref[:, hd], dstate[:, hd] = dq, dk_, dv, ds0

    blk = pl.BlockSpec((c, width), lambda t: (n - 1 - t, 0))
    gt = pl.BlockSpec((c, LANES), lambda t: (n - 1 - t, 0))
    st = pl.BlockSpec((dk, width), lambda t: (n - 1 - t, 0))
    iv = pl.BlockSpec((c, GDN_HEADS * c), lambda t: (n - 1 - t, 0))
    return _delta_rule_call(name, walk, n, [blk] * 3 + [gt, st, iv, blk], [blk] * 3 + [gt],
                            [_sds((s, width))] * 3 + [_sds((s, LANES))], (q, k, v, gates, s_in, inv, do), exchange)


def adamw(w, g, m, v, name):
    shape = w.shape
    if len(shape) == 2:
        grid, spec = (1,), pl.BlockSpec(shape, lambda i: (0, 0))
    else:
        tile = shape[1] if shape[1] <= 512 else _pick(shape[1], (512, 256, 128))
        grid, spec = (shape[0], shape[1] // tile), pl.BlockSpec((None, tile, shape[2]), lambda layer, i: (layer, i, 0))

    def body(w_ref, g_ref, m_ref, v_ref, d_ref, nm_ref, nv_ref):
        grad = g_ref[...]
        nm = ADAM_B1 * m_ref[...] + (1.0 - ADAM_B1) * grad
        nv = ADAM_B2 * v_ref[...] + (1.0 - ADAM_B2) * (grad * grad)
        m_hat = nm / (1.0 - ADAM_B1 ** ADAM_STEP)
        v_hat = nv / (1.0 - ADAM_B2 ** ADAM_STEP)
        d_ref[...] = -ADAM_LR * (m_hat / (jnp.sqrt(v_hat) + ADAM_EPS) + ADAM_WD * w_ref[...])
        nm_ref[...] = nm
        nv_ref[...] = nv

    return tuple(pl.pallas_call(body, name=name, grid=grid, in_specs=[spec] * 4, out_specs=[spec] * 3,
                                out_shape=[_sds(shape)] * 3, compiler_params=_params(len(grid)))(w, g, m, v))


def _place():
    return lax.axis_index("x"), lax.axis_index("y"), lax.axis_index("c")


def _flip(p, bits):
    return tuple(1 - v if (bits >> s) & 1 else v for v, s in zip(p, (2, 1, 0)))


def _slot(p):
    return 4 * p[0] + 2 * p[1] + p[2]


def _chip_of(p):
    return 2 * p[0] + p[1]


ANY = pl.BlockSpec(memory_space=pl.ANY)


class Gather:
    scratch = (pltpu.SemaphoreType.DMA((7,)), pltpu.SemaphoreType.DMA((7,)), pltpu.SemaphoreType.DMA)

    def __init__(self, shard):
        self.operand = shard
        self.out_shape = jax.ShapeDtypeStruct((N_DEV,) + shard.shape, shard.dtype)

    def bind(self, x_ref, out_ref, send_sems, recv_sems, local_sem):
        me = _place()
        sibling = _flip(me, 1)
        chips = [_flip(me, 4), _flip(me, 2), _flip(me, 6)]

        def copy(k, block, to, src=None):
            return pltpu.make_async_remote_copy(
                src_ref=out_ref.at[_slot(block)] if src is None else src, dst_ref=out_ref.at[_slot(block)],
                send_sem=send_sems.at[k], recv_sem=recv_sems.at[k], device_id=to, device_id_type=MESH)

        mine = pltpu.make_async_copy(x_ref, out_ref.at[_slot(me)], local_sem)
        first = [copy(0, me, sibling, src=x_ref)] + [copy(1 + j, me, chip, src=x_ref) for j, chip in enumerate(chips)]
        passed = [copy(4 + j, chip, sibling) for j, chip in enumerate(chips)]

        def start():
            mine.start()
            for cp in first:
                cp.start()

        def finish():
            for j, chip in enumerate(chips):
                copy(1 + j, chip, me).wait_recv()
                passed[j].start()
            copy(0, sibling, me).wait_recv()
            for j, chip in enumerate(chips):
                copy(4 + j, _flip(chip, 1), me).wait_recv()
            for cp in first + passed:
                cp.wait_send()
            mine.wait()

        return start, finish


class ChipExchange:
    scratch = (pltpu.SemaphoreType.DMA((3,)), pltpu.SemaphoreType.DMA((3,)), pltpu.SemaphoreType.DMA)

    def __init__(self, blocks):
        self.operand = blocks
        self.out_shape = jax.ShapeDtypeStruct(blocks.shape, blocks.dtype)

    def bind(self, x_ref, out_ref, send_sems, recv_sems, local_sem):
        me = _place()
        peers = [_flip(me, 4), _flip(me, 2), _flip(me, 6)]
        mine = pltpu.make_async_copy(x_ref.at[_chip_of(me)], out_ref.at[_chip_of(me)], local_sem)

        def copy(j, src_chip, dst_chip):
            return pltpu.make_async_remote_copy(
                src_ref=x_ref.at[src_chip], dst_ref=out_ref.at[dst_chip], send_sem=send_sems.at[j],
                recv_sem=recv_sems.at[j], device_id=peers[j], device_id_type=MESH)

        sends = [copy(j, _chip_of(peer), _chip_of(me)) for j, peer in enumerate(peers)]

        def start():
            mine.start()
            for cp in sends:
                cp.start()

        def finish():
            for j, peer in enumerate(peers):
                copy(j, _chip_of(me), _chip_of(peer)).wait_recv()
            for cp in sends:
                cp.wait_send()
            mine.wait()

        return start, finish


class Together:
    def __init__(self, *parts):
        self.parts = parts
        self.operands = [p.operand for p in parts]
        self.out_shapes = [p.out_shape for p in parts]
        self.scratch = [s for p in parts for s in p.scratch]

    def bind(self, x_refs, out_refs, sems):
        bound, at = [], 0
        for p, x_ref, out_ref in zip(self.parts, x_refs, out_refs):
            bound.append(p.bind(x_ref, out_ref, *sems[at:at + len(p.scratch)]))
            at += len(p.scratch)

        def start():
            for s, _ in bound:
                s()

        def finish():
            for _, f in bound:
                f()

        return start, finish


def _row_tile(rows):
    return max([t for t in range(16, min(rows, 1024) + 1, 16) if rows % t == 0] or [rows])


def pair_exchange(blocks, name):
    n = len(blocks)

    def body(*refs):
        x_refs, theirs_refs, (send_sems, recv_sems) = refs[:n], refs[n:2 * n], refs[2 * n:]
        me = _place()
        remote = [pltpu.make_async_remote_copy(
            src_ref=x_refs[t].at[2 * q + 1 - me[2]], dst_ref=theirs_refs[t].at[q], send_sem=send_sems.at[4 * t + q],
            recv_sem=recv_sems.at[4 * t + q], device_id=_flip(me, 1), device_id_type=MESH) for t in range(n) for q in range(4)]
        for cp in remote:
            cp.start()
        for cp in remote:
            cp.wait()

    return pl.pallas_call(
        body, name=name, out_shape=[jax.ShapeDtypeStruct((4,) + b.shape[1:], b.dtype) for b in blocks], in_specs=[ANY] * n,
        out_specs=[ANY] * n, scratch_shapes=[pltpu.SemaphoreType.DMA((4 * n,)), pltpu.SemaphoreType.DMA((4 * n,))])(*blocks)


def pair_add(blocks, theirs, name):
    n, rows, width = theirs.shape
    tile = _row_tile(rows)
    spec = pl.BlockSpec((None, tile, width), lambda q, i: (q, i, 0))
    south = pl.BlockSpec((None, None, tile, width), lambda q, i: (q, 0, i, 0))
    north = pl.BlockSpec((None, None, tile, width), lambda q, i: (q, 1, i, 0))

    def body(s_ref, n_ref, b_ref, o_ref):
        mine = jnp.where(lax.axis_index("c") == 0, s_ref[...], n_ref[...])
        o_ref[...] = (mine.astype(f32) + b_ref[...].astype(f32)).astype(o_ref.dtype)

    by_core = blocks.reshape(n, 2, rows, width)
    return pl.pallas_call(body, name=name, grid=(n, rows // tile), in_specs=[south, north, spec], out_specs=spec,
                          out_shape=jax.ShapeDtypeStruct(theirs.shape, theirs.dtype), compiler_params=_params(2))(by_core, by_core, theirs)


def sum_slots(blocks, name):
    n, rows, width = blocks.shape
    tile = _row_tile(rows)

    def body(x_ref, o_ref):
        total = x_ref[0].astype(f32)
        for s in range(1, n):
            total = total + x_ref[s].astype(f32)
        o_ref[...] = total

    return pl.pallas_call(
        body, name=name, grid=(rows // tile,), in_specs=[pl.BlockSpec((n, tile, width), lambda i: (0, i, 0))],
        out_specs=pl.BlockSpec((tile, width), lambda i: (i, 0)), out_shape=_sds((rows, width)), compiler_params=_params(1))(blocks)


def all_reduce_small(x, name, exchange=None):
    rows, width = x.shape
    carried = len(exchange.operands) if exchange else 0

    def body(x_ref, *refs):
        x_refs, refs = refs[:carried], refs[carried:]
        o_ref, refs = refs[0], refs[1:]
        land_refs, (land, send_sems, recv_sems, *sems) = refs[:carried], refs[carried:]
        me = _place()
        copies = []
        for k in range(1, N_DEV):
            peer = _flip(me, k)
            copies.append(pltpu.make_async_remote_copy(
                src_ref=x_ref, dst_ref=land.at[_slot(me)], send_sem=send_sems.at[k - 1], recv_sem=recv_sems.at[k - 1],
                device_id=peer, device_id_type=MESH))
        for cp in copies:
            cp.start()
        if exchange:
            start, finish = exchange.bind(x_refs, land_refs, sems)
            start()
        land[_slot(me)] = x_ref[...]
        for k in range(1, N_DEV):
            peer = _flip(me, k)
            pltpu.make_async_remote_copy(
                src_ref=x_ref, dst_ref=land.at[_slot(peer)], send_sem=send_sems.at[k - 1], recv_sem=recv_sems.at[k - 1],
                device_id=peer, device_id_type=MESH).wait_recv()
        total = land[0]
        for s in range(1, N_DEV):
            total = total + land[s]
        o_ref[...] = total
        for cp in copies:
            cp.wait_send()
        if exchange:
            finish()

    in_vmem = pl.BlockSpec(memory_space=pltpu.VMEM)
    total, *landed = pl.pallas_call(
        body, name=name, out_shape=[_sds((rows, width))] + (exchange.out_shapes if exchange else []),
        in_specs=[in_vmem] + [ANY] * carried, out_specs=[in_vmem] + [ANY] * carried,
        scratch_shapes=[pltpu.VMEM((N_DEV, rows, width), f32), pltpu.SemaphoreType.DMA((7,)), pltpu.SemaphoreType.DMA((7,))]
        + (exchange.scratch if exchange else []),
    )(x, *(exchange.operands if exchange else []))
    return total, landed


def _pack_big(shards):
    packed = {name: shards[name].astype(bf16) for name in COL_SHARDED}
    packed["rows"] = jnp.concatenate([shards[name].astype(bf16) for name, _ in ROW_SHARDED], axis=1)
    return packed


def _unpack_gathered(gathered):
    full = {}
    for name, part in gathered.items():
        if name == "w_gate_up":
            full[name] = part
        elif name in COL_SHARDED:
            full[name] = part.transpose(1, 0, 2).reshape(D_MODEL, N_DEV * part.shape[2])
        else:
            at = 0
            for weight, rows in ROW_SHARDED:
                full[weight] = part[:, at:at + rows, :].reshape(N_DEV * rows, D_MODEL)
                at += rows
    return full


def _pack_grads(grads, group):
    packed = {}
    for name in group:
        if name == "w_gate_up":
            packed[name] = grads[name]
        elif name == "rows":
            packed[name] = jnp.concatenate([grads[weight].reshape(N_DEV, rows, D_MODEL) for weight, rows in ROW_SHARDED], axis=1)
        else:
            packed[name] = grads[name].reshape(D_MODEL, N_DEV, grads[name].shape[1] // N_DEV).transpose(1, 0, 2)
    return packed


def _unpack_shard(layers):
    out = {name: jnp.stack([layer[name] for layer in layers]) for name in COL_SHARDED}
    rows_pack, at = jnp.stack([layer["rows"] for layer in layers]), 0
    for weight, rows in ROW_SHARDED:
        out[weight] = rows_pack[:, at:at + rows, :]
        at += rows
    return out


def _rows_of(flat_len):
    return -(-flat_len // (8 * D_MODEL)) * 8


def _pack_small(parts):
    flat = jnp.concatenate([p.reshape(-1) for p in parts])
    rows = _rows_of(flat.shape[0])
    flat = jnp.pad(flat, (0, rows * D_MODEL - flat.shape[0]))
    return flat.reshape(rows, D_MODEL)


def _unpack_small(packed, like):
    flat, out, at = packed.reshape(-1), [], 0
    for p in like:
        out.append(flat[at:at + p.size].reshape(p.shape))
        at += p.size
    return out


def _rope_tables(positions):
    inv_freq = jnp.float32(ROPE_THETA) ** (-jnp.arange(0, ROPE_DIM, 2, dtype=f32) / ROPE_DIM)
    ang = positions.astype(f32)[:, None] * inv_freq
    cos, sin = jnp.cos(ang), jnp.sin(ang)
    rest = ATTN_HEAD_DIM - ROPE_DIM
    cos_h = jnp.concatenate([cos, cos, jnp.ones((cos.shape[0], rest), f32)], axis=1)
    sin_h = jnp.concatenate([-sin, sin, jnp.zeros((sin.shape[0], rest), f32)], axis=1)
    return jnp.tile(cos_h, (1, ATTN_HEADS)), jnp.tile(sin_h, (1, ATTN_HEADS))


HEAD_SMALL = ("norm_mix_pre", "conv_short", "conv_gdn", "gdn_a_log", "gdn_dt_bias")


def _layer_head(h, p, cos_t, sin_t):
    hn = rms_norm(h, p["norm_mix_pre"][None], "norm_mix_pre")
    aw, cw, gw = ATTN_WIDTH, CONV_WIDTH, GDN_WIDTH
    aq, ak, av, cb, cc, cx, gqkv, ab, gate = _split_cols(_linear(hn, p["w_in"], "w_in"),
                                                         (aw, aw, aw, cw, cw, cw, 3 * gw, 2 * GDN_HEADS, gw))
    ab = jnp.pad(ab, ((0, 0), (0, LANES - 2 * GDN_HEADS)))
    y_attn = dilated_attention(rope(aq, cos_t, sin_t, ATTN_HEAD_DIM ** -0.5, "rope_q"), rope(ak, cos_t, sin_t, 1.0, "rope_k"),
                               av, "attn")
    y_conv = short_conv(cb, cc, cx, p["conv_short"], "short_conv")
    qkv = gdn_pre(gqkv, p["conv_gdn"], "gdn_pre")
    pv = jnp.zeros((8, LANES), f32).at[0, :GDN_HEADS].set(p["gdn_a_log"]).at[1, :GDN_HEADS].set(p["gdn_dt_bias"])
    return (*_split_cols(qkv, (gw, gw, gw)), gate_beta(ab, pv, "gate_beta")), (gate, y_attn, y_conv)


MID_PARAMS = ("gdn_norm", "w_out", "norm_mix_post", "norm_xattn_pre", "w_xq", "norm_mem", "w_xkv", "w_xo", "norm_xattn_post",
              "norm_ffn_pre")


def _layer_mid(h, o, gate, y_attn, y_conv, p, mem):
    y_gdn = gdn_post(o, gate, p["gdn_norm"][None], "gdn_post")
    mix = _joined_linear((y_attn, y_conv, y_gdn), p["w_out"], "w_out")
    h, hn = add_norm_then_norm(h, mix, p["norm_mix_post"][None], p["norm_xattn_pre"][None], "norm_mix_xattn")
    qx = _linear(hn, p["w_xq"], "w_xq")
    kv = _linear(rms_norm(mem, p["norm_mem"][None], "norm_mem"), p["w_xkv"], "w_xkv")
    xa = _linear(cross_attention(qx, kv, "xattn"), p["w_xo"], "w_xo")
    return add_norm_then_norm(h, xa, p["norm_xattn_post"][None], p["norm_ffn_pre"][None], "norm_xattn_ffn")


def _pair_summed(grads, group, name):
    blocks = _pack_grads(grads, group)
    theirs = pair_exchange([blocks[n] for n in group], name + "_pair_exchange")
    return [pair_add(blocks[n], t, f"{name}_pair_add_{n}") for n, t in zip(group, theirs)]


def _forward_backward(x, packed, head_gathered, small, mem, cos_t, sin_t, target):
    def gathers(group, layer):
        return [Gather(packed[n][layer]) for n in group]

    h = x
    saved = []
    for layer in range(DEPTH):
        at_layer = {n: t[layer] for n, t in small.items()}
        head_p = {**_unpack_gathered(dict(zip(HEAD_GROUP, head_gathered))), **{n: at_layer[n] for n in HEAD_SMALL}}
        (rule_in, rest), head_vjp = jax.vjp(lambda h, hp: _layer_head(h, hp, cos_t, sin_t), h, head_p)
        carried = gathers(TAIL_GROUP, layer) + (gathers(HEAD_GROUP, layer + 1) if layer + 1 < DEPTH else [])
        o, s_in, inv, *landed = delta_rule_fwd(*rule_in, "delta_rule_fwd", Together(*carried))
        head_gathered = landed[len(TAIL_GROUP):]
        tail_p = {**_unpack_gathered(dict(zip(TAIL_GROUP, landed))), **at_layer}
        mid_p = {n: tail_p[n] for n in MID_PARAMS}
        (h, hn), mid_vjp = jax.vjp(lambda h, o, rest, mp: _layer_mid(h, o, *rest, mp, mem), h, o, rest, mid_p)
        y, ffn_saved = ffn_forward(hn, tail_p["w_gate_up"], tail_p["w_down"], "ffn")
        h, last_vjp = jax.vjp(lambda h, y, w: add_norm(h, y, w[None], "norm_ffn_post"), h, y, tail_p["norm_ffn_post"])
        saved.append((head_vjp, mid_vjp, last_vjp, ffn_saved, rule_in, s_in, inv))

    loss, dh = jax.value_and_grad(lambda y: loss_rows(y, target, "loss"))(h)

    def summed(group, landed):
        return {n: sum_slots(t, "sum_grads_" + n) for n, t in zip(group, landed)}

    big_grads, small_grads, head_pending = [{} for _ in range(DEPTH)], [None] * DEPTH, []
    for layer in reversed(range(DEPTH)):
        head_vjp, mid_vjp, last_vjp, ffn_saved, rule_in, s_in, inv = saved[layer]
        dh, dy, d_norm_ffn_post = last_vjp(dh)
        dhn, d_gate_up, d_down, landed = ffn_backward(
            ffn_saved, dy, "ffn", Together(*[ChipExchange(t) for t in head_pending]) if head_pending else None)
        if head_pending:
            big_grads[layer + 1].update(summed(HEAD_GROUP, landed))
        dh_mid, do, d_rest, d_mid_p = mid_vjp((dh, dhn))
        d_tail_p = {**d_mid_p, "w_gate_up": d_gate_up, "w_down": d_down, "norm_ffn_post": d_norm_ffn_post}
        carried = Together(*[ChipExchange(t) for t in _pair_summed(d_tail_p, TAIL_GROUP, "tail")])
        *d_rule_in, = delta_rule_bwd(*rule_in, s_in, inv, do, "delta_rule_bwd", carried)
        big_grads[layer].update(summed(TAIL_GROUP, d_rule_in[4:]))
        dh_head, d_head_p = head_vjp((tuple(d_rule_in[:4]), d_rest))
        dh = dh_mid + dh_head
        small_grads[layer] = {n: t for n, t in {**d_head_p, **d_tail_p}.items() if n in small}
        head_pending = _pair_summed(d_head_p, HEAD_GROUP, "head")
    return loss, dh, big_grads, small_grads, head_pending


def kernel(x, mem, positions, norm_mix_pre, norm_mix_post, w_in, conv_short, conv_gdn, gdn_a_log, gdn_dt_bias, gdn_norm, w_out, norm_mem, norm_xattn_pre, norm_xattn_post, w_xq, w_xkv, w_xo, norm_ffn_pre, norm_ffn_post, w_gate_up, w_down, loss_target, m_norm_mix_pre, m_norm_mix_post, m_w_in, m_conv_short, m_conv_gdn, m_gdn_a_log, m_gdn_dt_bias, m_gdn_norm, m_w_out, m_norm_mem, m_norm_xattn_pre, m_norm_xattn_post, m_w_xq, m_w_xkv, m_w_xo, m_norm_ffn_pre, m_norm_ffn_post, m_w_gate_up, m_w_down, v_norm_mix_pre, v_norm_mix_post, v_w_in, v_conv_short, v_conv_gdn, v_gdn_a_log, v_gdn_dt_bias, v_gdn_norm, v_w_out, v_norm_mem, v_norm_xattn_pre, v_norm_xattn_post, v_w_xq, v_w_xkv, v_w_xo, v_norm_ffn_pre, v_norm_ffn_post, v_w_gate_up, v_w_down):
    given = dict(locals())
    weights = {n: given[n] for n in WEIGHTS}
    me = _slot(_place())

    def in_place(shard):
        full = jnp.zeros(shard.shape[:-1] + (shard.shape[-1] * N_DEV,), f32)
        return lax.dynamic_update_slice_in_dim(full, shard, me * shard.shape[-1], axis=shard.ndim - 1)

    placed = [in_place(conv_short), in_place(conv_gdn)]
    packed = _pack_big(weights)
    gathered_conv, head_gathered = all_reduce_small(
        _pack_small(placed), "gather_conv", Together(*[Gather(packed[n][0]) for n in HEAD_GROUP]))
    conv_short_full, conv_gdn_full = _unpack_small(gathered_conv, placed)
    small = {n: weights[n] for n in NORMS + ("gdn_a_log", "gdn_dt_bias", "gdn_norm")}
    small["conv_short"], small["conv_gdn"] = conv_short_full, conv_gdn_full

    cos_t, sin_t = _rope_tables(positions[0])
    loss, grad_x, big_layers, small_layers, head_pending = _forward_backward(
        x[0], packed, head_gathered, small, mem[0], cos_t, sin_t, loss_target[0])

    names = sorted(small)
    parts = [jnp.stack([layer[n] for layer in small_layers]) for n in names] + [loss.reshape(1)]
    reduced, landed = all_reduce_small(_pack_small(parts), "reduce_small", Together(*[ChipExchange(t) for t in head_pending]))
    big_layers[0].update({n: sum_slots(t, "sum_grads_" + n) for n, t in zip(HEAD_GROUP, landed)})
    grads = _unpack_shard(big_layers)
    reduced = _unpack_small(reduced, parts)
    loss = reduced[-1][0]
    for n, g in zip(names, reduced[:-1]):
        if n in ("conv_short", "conv_gdn"):
            width = weights[n].shape[-1]
            g = lax.dynamic_slice_in_dim(g, me * width, width, axis=g.ndim - 1)
        grads[n] = g

    delta, new_m, new_v = {}, {}, {}
    for n in WEIGHTS:
        delta[n], new_m[n], new_v[n] = adamw(weights[n], grads[n], given["m_" + n], given["v_" + n], "adamw_" + n)
    return (loss, grad_x[None], *[grads[n] for n in WEIGHTS], *[delta[n] for n in WEIGHTS],
            *[new_m[n] for n in WEIGHTS], *[new_v[n] for n in WEIGHTS])
```

```python
import functools

import jax
import jax.numpy as jnp
from jax import lax
from jax.experimental import pallas as pl
from jax.experimental.pallas import tpu as pltpu

f32 = jnp.float32
bf16 = jnp.bfloat16
MESH = pl.DeviceIdType.MESH

N_DEV = 8
DEPTH = 4
D_MODEL = 1024
EPS = 1e-6
ATTN_HEADS, ATTN_HEAD_DIM = 4, 64
ATTN_WIDTH = ATTN_HEADS * ATTN_HEAD_DIM
DILATIONS = (1, 4, 16)
QB = 128
ROPE_THETA = 500000.0
ROPE_DIM = ATTN_HEAD_DIM // 4
CONV_WIDTH = 256
GDN_HEADS, GDN_HEAD_DIM = 4, 128
GDN_WIDTH = GDN_HEADS * GDN_HEAD_DIM
GDN_CHUNK = 64
XATTN_HEADS, XATTN_HEAD_DIM = 4, 256
LANES = 128
ROW_TILE = 512
VMEM_LIMIT = 56 * 1024 * 1024

ADAM_LR, ADAM_B1, ADAM_B2, ADAM_EPS, ADAM_WD, ADAM_STEP = 0.001, 0.9, 0.999, 1e-08, 0.01, 10

COL_SHARDED = ("w_in", "w_xkv", "w_gate_up")
ROW_SHARDED = (("w_out", 128), ("w_xq", 128), ("w_xo", 128), ("w_down", 352))
HEAD_GROUP = ("w_in",)
TAIL_GROUP = ("w_gate_up", "w_xkv", "rows")
NORMS = ("norm_mix_pre", "norm_mix_post", "norm_mem", "norm_xattn_pre", "norm_xattn_post", "norm_ffn_pre", "norm_ffn_post")
WEIGHTS = ("norm_mix_pre", "norm_mix_post", "w_in", "conv_short", "conv_gdn", "gdn_a_log", "gdn_dt_bias", "gdn_norm", "w_out",
           "norm_mem", "norm_xattn_pre", "norm_xattn_post", "w_xq", "w_xkv", "w_xo", "norm_ffn_pre", "norm_ffn_post",
           "w_gate_up", "w_down")


def _params(n_grid):
    return pltpu.CompilerParams(dimension_semantics=("arbitrary",) * n_grid, vmem_limit_bytes=VMEM_LIMIT)


def _pick(n, cands):
    for c in cands:
        if n % c == 0:
            return c
    return n


MXU_FLOPS = 9.0e14
HBM_BYTES_PER_S = 2.5e12
VMEM_RMW_BYTES_PER_S = 7.0e12
STEP_S = 0.4e-6
MATMUL_VMEM = 44 * 1024 * 1024


def _tiles(m, n, k, sa, sb, so):
    def divisors(d):
        return sorted({d} | {d // s for s in range(1, d // LANES + 1) if d % s == 0 and (d // s) % LANES == 0}, reverse=True)

    best = None
    for tk in divisors(k):
        nk = k // tk
        for tm in divisors(m):
            for tn_ in divisors(n):
                per_step = tm * tk * sa + tk * tn_ * sb + tm * tn_ * so
                vmem = 2 * per_step + (tm * tn_ * 4 if nk > 1 else 0)
                vmem += (tm * tk * 2 if sa == 4 else 0) + (tk * tn_ * 2 if sb == 4 else 0) + tm * tn_ * 4
                if vmem > MATMUL_VMEM:
                    continue
                moved = m * k * sa * (1 if nk == 1 else n // tn_) + k * n * sb * (1 if nk == 1 and n == tn_ else m // tm) + m * n * so
                busy = 2 * m * n * k / MXU_FLOPS + (m * n * 8 * nk / VMEM_RMW_BYTES_PER_S if nk > 1 else 0)
                cost = max(moved / HBM_BYTES_PER_S, busy) + per_step / HBM_BYTES_PER_S + (m // tm) * (n // tn_) * nk * STEP_S
                if best is None or cost < best[0]:
                    best = (cost, tm, tn_, tk)
    return best[1:]


def _mm(a, b, ta, tb, out_dtype, name):
    m, k = (a.shape[1], a.shape[0]) if ta else a.shape
    n = b.shape[0] if tb else b.shape[1]
    tm, tn, tk = _tiles(m, n, k, a.dtype.itemsize, b.dtype.itemsize, jnp.dtype(out_dtype).itemsize)
    nk = k // tk
    a_spec = pl.BlockSpec((tk, tm), lambda i, j, kk: (kk, i)) if ta else pl.BlockSpec((tm, tk), lambda i, j, kk: (i, kk))
    b_spec = pl.BlockSpec((tn, tk), lambda i, j, kk: (j, kk)) if tb else pl.BlockSpec((tk, tn), lambda i, j, kk: (kk, j))
    dims = (((0 if ta else 1,), (1 if tb else 0,)), ((), ()))

    def body(a_ref, b_ref, o_ref, *acc):
        kk = pl.program_id(2)
        p = lax.dot_general(a_ref[...].astype(bf16), b_ref[...].astype(bf16), dims, preferred_element_type=f32)
        if nk == 1:
            o_ref[...] = p.astype(o_ref.dtype)
            return
        acc_ref, = acc

        @pl.when(kk == 0)
        def _():
            acc_ref[...] = p

        @pl.when(kk > 0)
        def _():
            acc_ref[...] += p

        @pl.when(kk == nk - 1)
        def _():
            o_ref[...] = acc_ref[...].astype(o_ref.dtype)

    return pl.pallas_call(
        body, name=name, grid=(m // tm, n // tn, nk), in_specs=[a_spec, b_spec],
        out_specs=pl.BlockSpec((tm, tn), lambda i, j, kk: (i, j)), out_shape=jax.ShapeDtypeStruct((m, n), out_dtype),
        scratch_shapes=[pltpu.VMEM((tm, tn), f32)] if nk > 1 else [], compiler_params=_params(3))(a, b)


def _linear(x, w, name):
    @jax.custom_vjp
    def lin(x, w):
        return _mm(x, w, False, False, f32, name + "_y")

    def lin_f(x, w):
        return _mm(x, w, False, False, f32, name + "_y"), (x, w)

    def lin_b(res, dy):
        x, w = res
        return _mm(dy, w, False, True, f32, name + "_dx"), _mm(x, dy, True, False, bf16, name + "_dw")

    lin.defvjp(lin_f, lin_b)
    return lin(x, w)


def _bdot(a, b, form):
    dims = {"nn": ((1,), (0,)), "nt": ((1,), (1,)), "tn": ((0,), (0,))}[form]
    return lax.dot_general(a.astype(bf16), b.astype(bf16), (dims, ((), ())), preferred_element_type=f32)


def ffn_forward(hn, w_gate_up, w_down, name):
    s, k = hn.shape
    n_blocks, _, width = w_gate_up.shape
    half = n_blocks // 2
    tm = 1024
    blocked = jax.ShapeDtypeStruct((half, s, width), bf16)

    def act_body(x_ref, wg_ref, wu_ref, gate_ref, up_ref, act_ref):
        x = x_ref[...]
        gate, up = _bdot(x, wg_ref[...], "nn"), _bdot(x, wu_ref[...], "nn")
        gate_ref[...], up_ref[...] = gate.astype(bf16), up.astype(bf16)
        act_ref[...] = (jax.nn.silu(gate) * up).astype(bf16)

    tile = pl.BlockSpec((None, tm, width), lambda i, d: (d, i, 0))
    gate, up, act = pl.pallas_call(
        act_body, name=name + "_act", grid=(s // tm, half),
        in_specs=[pl.BlockSpec((tm, k), lambda i, d: (i, 0)), pl.BlockSpec((None, k, width), lambda i, d: (d, 0, 0)),
                  pl.BlockSpec((None, k, width), lambda i, d: (d + half, 0, 0))],
        out_specs=[tile] * 3, out_shape=[blocked] * 3, compiler_params=_params(2))(hn, w_gate_up, w_gate_up)

    n = w_down.shape[1]
    tn = 512

    def y_body(act_ref, w_ref, y_ref):
        y_ref[...] = sum(_bdot(act_ref[d], w_ref[d * width:(d + 1) * width, :], "nn") for d in range(half))

    y = pl.pallas_call(
        y_body, name=name + "_y", grid=(s // tm, n // tn),
        in_specs=[pl.BlockSpec((half, tm, width), lambda i, j: (0, i, 0)), pl.BlockSpec((half * width, tn), lambda i, j: (0, j))],
        out_specs=pl.BlockSpec((tm, tn), lambda i, j: (i, j)), out_shape=_sds((s, n)), compiler_params=_params(2))(act, w_down)
    return y, (hn, w_gate_up, w_down, gate, up, act)


def ffn_backward(saved, dy, name, exchange=None):
    hn, w_gate_up, w_down, gate, up, act = saved
    s, k = hn.shape
    n_blocks, _, width = w_gate_up.shape
    half = n_blocks // 2
    n = w_down.shape[1]
    tm = 1024
    blocked = jax.ShapeDtypeStruct((half, s, width), bf16)
    carried = len(exchange.operands) if exchange else 0
    steps = (s // tm, half)

    def dact_body(dy_ref, w_ref, gate_ref, up_ref, *refs):
        x_refs, refs = refs[:carried], refs[carried:]
        (dgate_ref, dup_ref), refs = refs[:2], refs[2:]
        if exchange:
            at = pl.program_id(0) * steps[1] + pl.program_id(1)
            start, wait = exchange.bind(x_refs, refs[:carried], refs[carried:])
            pl.when(at == 0)(start)
        d_act = _bdot(dy_ref[...], w_ref[...], "nt")
        g, u = gate_ref[...].astype(f32), up_ref[...].astype(f32)
        sig = jax.nn.sigmoid(g)
        dgate_ref[...] = (d_act * u * sig * (1.0 + g * (1.0 - sig))).astype(bf16)
        dup_ref[...] = (d_act * g * sig).astype(bf16)
        if exchange:
            pl.when(at == steps[0] * steps[1] - 1)(wait)

    tile = pl.BlockSpec((None, tm, width), lambda i, d: (d, i, 0))
    d_gate, d_up, *landed = pl.pallas_call(
        dact_body, name=name + "_dact", grid=steps,
        in_specs=[pl.BlockSpec((tm, n), lambda i, d: (i, 0)), pl.BlockSpec((width, n), lambda i, d: (d, 0)), tile, tile] + [ANY] * carried,
        out_specs=[tile, tile] + [ANY] * carried, out_shape=[blocked, blocked] + (exchange.out_shapes if exchange else []),
        scratch_shapes=exchange.scratch if exchange else [],
        compiler_params=_params(2))(dy, w_down, gate, up, *(exchange.operands if exchange else []))

    def dx_body(dg_ref, du_ref, w_ref, dx_ref):
        dx_ref[...] = sum(_bdot(dg_ref[d], w_ref[d], "nt") + _bdot(du_ref[d], w_ref[d + half], "nt") for d in range(half))

    tx = 512
    rows = pl.BlockSpec((half, tx, width), lambda i: (0, i, 0))
    dx = pl.pallas_call(
        dx_body, name=name + "_dx", grid=(s // tx,), in_specs=[rows, rows, _whole(w_gate_up.shape)],
        out_specs=pl.BlockSpec((tx, k), lambda i: (i, 0)), out_shape=_sds((s, k)), compiler_params=_params(1))(d_gate, d_up, w_gate_up)

    def dw1_body(x_ref, dg_ref, du_ref, dw_ref):
        d_block = jnp.where(pl.program_id(0) < half, dg_ref[...], du_ref[...])
        dw_ref[...] = _bdot(x_ref[...], d_block, "tn").astype(bf16)

    d_w_gate_up = pl.pallas_call(
        dw1_body, name=name + "_dw1", grid=(n_blocks,),
        in_specs=[_whole((s, k)), pl.BlockSpec((None, s, width), lambda b: (jnp.minimum(b, half - 1), 0, 0)),
                  pl.BlockSpec((None, s, width), lambda b: (jnp.maximum(b - half, 0), 0, 0))],
        out_specs=pl.BlockSpec((None, k, width), lambda b: (b, 0, 0)), out_shape=jax.ShapeDtypeStruct(w_gate_up.shape, bf16),
        compiler_params=_params(1))(hn, d_gate, d_up)

    tn = 512

    def dw2_body(act_ref, dy_ref, dw_ref):
        dw_ref[...] = _bdot(act_ref[...], dy_ref[...], "tn").astype(bf16)

    d_w_down = pl.pallas_call(
        dw2_body, name=name + "_dw2", grid=(half, n // tn),
        in_specs=[pl.BlockSpec((None, s, width), lambda d, j: (d, 0, 0)), pl.BlockSpec((s, tn), lambda d, j: (0, j))],
        out_specs=pl.BlockSpec((width, tn), lambda d, j: (d, j)), out_shape=jax.ShapeDtypeStruct(w_down.shape, bf16),
        compiler_params=_params(2))(act, dy)
    return dx, d_w_gate_up, d_w_down, landed


def _joined_linear(parts, w, name):
    s, n = parts[0].shape[0], w.shape[1]
    widths = [p.shape[1] for p in parts]
    edges = [sum(widths[:i]) for i in range(len(widths) + 1)]
    spans = list(zip(edges[:-1], edges[1:]))
    tm, tn = 512, 512

    def forward(*args):
        *xs, w = args

        def y_body(*refs):
            *x_refs, w_ref, y_ref = refs
            y_ref[...] = sum(_bdot(x_ref[...], w_ref[a:b, :], "nn") for x_ref, (a, b) in zip(x_refs, spans))

        y = pl.pallas_call(
            y_body, name=name + "_y", grid=(s // tm, n // tn),
            in_specs=[pl.BlockSpec((tm, k), lambda i, j: (i, 0)) for k in widths] + [pl.BlockSpec((edges[-1], tn), lambda i, j: (0, j))],
            out_specs=pl.BlockSpec((tm, tn), lambda i, j: (i, j)), out_shape=_sds((s, n)), compiler_params=_params(2))(*xs, w)
        return y, args

    def backward(args, dy):
        *xs, w = args

        def dx_body(dy_ref, w_ref, *dx_refs):
            d_all = _bdot(dy_ref[...], w_ref[...], "nt")
            for dx_ref, (a, b) in zip(dx_refs, spans):
                dx_ref[...] = d_all[:, a:b]

        dxs = pl.pallas_call(
            dx_body, name=name + "_dx", grid=(s // tm,), in_specs=[pl.BlockSpec((tm, n), lambda i: (i, 0)), _whole(w.shape)],
            out_specs=[pl.BlockSpec((tm, k), lambda i: (i, 0)) for k in widths], out_shape=[_sds((s, k)) for k in widths],
            compiler_params=_params(1))(dy, w)

        def dw_body(*refs):
            *x_refs, dy_ref, dw_ref = refs
            dy_tile = dy_ref[...]
            dw_ref[...] = jnp.concatenate([_bdot(x_ref[...], dy_tile, "tn") for x_ref in x_refs], axis=0).astype(bf16)

        dw = pl.pallas_call(
            dw_body, name=name + "_dw", grid=(n // tn,),
            in_specs=[_whole((s, k)) for k in widths] + [pl.BlockSpec((s, tn), lambda j: (0, j))],
            out_specs=pl.BlockSpec((edges[-1], tn), lambda j: (0, j)), out_shape=jax.ShapeDtypeStruct(w.shape, bf16),
            compiler_params=_params(1))(*xs, dy)
        return (*dxs, dw)

    @jax.custom_vjp
    def op(*args):
        return forward(*args)[0]

    op.defvjp(forward, backward)
    return op(*parts, w)


def _split_cols(x, widths):
    edges = [sum(widths[:i]) for i in range(len(widths) + 1)]

    def cut(x):
        return tuple(x[:, a:b] for a, b in zip(edges[:-1], edges[1:]))

    @jax.custom_vjp
    def split(x):
        return cut(x)

    split.defvjp(lambda x: (cut(x), None), lambda _, cts: (jnp.concatenate(cts, axis=1),))
    return split(x)


def _block_op(name, f, grid, in_specs, out_defs, arrays, diff, acc=None, gdefs=None):
    acc, gdefs = acc or {}, gdefs or {}
    n_in, n_out, n_grid = len(in_specs), len(out_defs), len(grid)

    def fwd_call(*xs):
        def body(*refs):
            outs = f(*[r[...] for r in refs[:n_in]])
            for r, o in zip(refs[n_in:], outs):
                r[...] = o.astype(r.dtype)

        return pl.pallas_call(
            body, name=name + "_fwd", grid=grid, in_specs=in_specs, out_specs=[d[1] for d in out_defs],
            out_shape=[d[0] for d in out_defs], compiler_params=_params(n_grid))(*xs)

    def bwd_call(*xs_and_cts):
        def body(*refs):
            xs = [r[...] for r in refs[:n_in]]
            cts = tuple(r[...] for r in refs[n_in:n_in + n_out])

            def of_diff(*dx):
                full = list(xs)
                for i, v in zip(diff, dx):
                    full[i] = v
                return tuple(f(*full))

            _, vjp = jax.vjp(of_diff, *[xs[i] for i in diff])
            grads = vjp(cts)
            for i, g, r in zip(diff, grads, refs[n_in + n_out:]):
                if i in acc:
                    first = functools.reduce(jnp.logical_and, [pl.program_id(a) == 0 for a in acc[i]])

                    @pl.when(first)
                    def _(r=r):
                        r[...] = jnp.zeros_like(r)

                    r[...] += g.astype(r.dtype)
                else:
                    r[...] = g.astype(r.dtype)

        g_defs = [gdefs.get(i, (jax.ShapeDtypeStruct(arrays[i].shape, f32), in_specs[i])) for i in diff]
        return pl.pallas_call(
            body, name=name + "_bwd", grid=grid, in_specs=list(in_specs) + [d[1] for d in out_defs],
            out_specs=[d[1] for d in g_defs], out_shape=[d[0] for d in g_defs], compiler_params=_params(n_grid))(*xs_and_cts)

    return fwd_call, bwd_call


def _simple_op(name, f, grid, in_specs, out_defs, arrays, diff, acc=None):
    fwd_call, bwd_call = _block_op(name, f, grid, in_specs, out_defs, arrays, diff, acc)

    @jax.custom_vjp
    def op(*xs):
        return tuple(fwd_call(*xs))

    def op_f(*xs):
        return tuple(fwd_call(*xs)), xs

    def op_b(xs, cts):
        grads = bwd_call(*xs, *cts)
        out = [jnp.zeros_like(x) for x in xs]
        for i, g in zip(diff, grads):
            out[i] = g
        return tuple(out)

    op.defvjp(op_f, op_b)
    return op(*arrays)


def _rows(width, tile=ROW_TILE):
    return pl.BlockSpec((tile, width), lambda i: (i, 0))


def _whole(shape):
    return pl.BlockSpec(shape, lambda *_: (0,) * len(shape))


def _sds(shape):
    return jax.ShapeDtypeStruct(shape, f32)


def _rms(x, w):
    return x * lax.rsqrt(jnp.mean(x * x, axis=-1, keepdims=True) + EPS) * w


def rms_norm(x, w, name):
    r, d = x.shape
    tile = min(ROW_TILE, r)
    return _simple_op(name, lambda x, w: (_rms(x, w),), (r // tile,), [_rows(d, tile), _whole((1, d))],
                      [(_sds((r, d)), _rows(d, tile))], (x, w), (0, 1), {1: (0,)})[0]


def add_norm(h, y, w, name):
    r, d = h.shape
    return _simple_op(name, lambda h, y, w: (h + _rms(y, w),), (r // ROW_TILE,), [_rows(d), _rows(d), _whole((1, d))],
                      [(_sds((r, d)), _rows(d))], (h, y, w), (0, 1, 2), {2: (0,)})[0]


def add_norm_then_norm(h, y, w_post, w_pre, name):
    r, d = h.shape

    def f(h, y, w_post, w_pre):
        h_new = h + _rms(y, w_post)
        return h_new, _rms(h_new, w_pre)

    return _simple_op(name, f, (r // ROW_TILE,), [_rows(d), _rows(d), _whole((1, d)), _whole((1, d))],
                      [(_sds((r, d)), _rows(d))] * 2, (h, y, w_post, w_pre), (0, 1, 2, 3), {2: (0,), 3: (0,)})


def _swap8(x):
    def raw(x):
        lane = lax.broadcasted_iota(jnp.int32, x.shape, 1) % ATTN_HEAD_DIM
        half = ROPE_DIM // 2
        up = pltpu.roll(x, x.shape[1] - half, axis=1)
        down = pltpu.roll(x, half, axis=1)
        return jnp.where(lane < half, up, jnp.where(lane < ROPE_DIM, down, 0.0))

    @jax.custom_vjp
    def swap(x):
        return raw(x)

    swap.defvjp(lambda x: (raw(x), None), lambda _, g: (raw(g),))
    return swap(x)


def rope(x, cos_t, sin_t, scale, name):
    r, d = x.shape
    return _simple_op(name, lambda x, c, s: ((x * c + _swap8(x) * s) * scale,), (r // ROW_TILE,), [_rows(d)] * 3,
                      [(_sds((r, d)), _rows(d))], (x, cos_t, sin_t), (0,))[0]


def _shift_rows(x, k):
    n = x.shape[0]

    def down(x):
        row = lax.broadcasted_iota(jnp.int32, x.shape, 0)
        return jnp.where(row >= k, pltpu.roll(x, k, axis=0), 0.0)

    def up(x):
        row = lax.broadcasted_iota(jnp.int32, x.shape, 0)
        return jnp.where(row < n - k, pltpu.roll(x, n - k, axis=0), 0.0)

    @jax.custom_vjp
    def shift(x):
        return down(x)

    shift.defvjp(lambda x: (down(x), None), lambda _, g: (up(g),))
    return shift(x)


def _causal_conv(x, w):
    taps = w.shape[0]
    y = x * w[taps - 1:taps, :]
    for j in range(taps - 1):
        y = y + _shift_rows(x, taps - 1 - j) * w[j:j + 1, :]
    return y


def _cols(rows, at=0):
    return pl.BlockSpec((rows, LANES), lambda j: (0, at + j))


def short_conv(cb, cc, cx, w, name):
    s, c = cb.shape
    taps = w.shape[0]
    return _simple_op(name, lambda b, c_, x, w: (b * _causal_conv(c_ * x, w),), (c // LANES,),
                      [_cols(s)] * 3 + [_cols(taps)], [(_sds((s, c)), _cols(s))], (cb, cc, cx, w), (0, 1, 2, 3))[0]


def gdn_pre(qkv, w, name):
    s, c = qkv.shape
    taps = w.shape[0]

    def f(x, w):
        j = pl.program_id(0)
        y = jax.nn.silu(_causal_conv(x, w))
        normed = y * lax.rsqrt(jnp.sum(y * y, axis=-1, keepdims=True) + EPS)
        scale = jnp.where(j < GDN_HEADS, GDN_HEAD_DIM ** -0.5, 1.0).astype(f32)
        return (jnp.where(j < 2 * GDN_HEADS, normed * scale, y),)

    return _simple_op(name, f, (c // LANES,), [_cols(s), _cols(taps)], [(_sds((s, c)), _cols(s))], (qkv, w), (0, 1))[0]


def gate_beta(ab, pv, name):
    s = ab.shape[0]

    def f(ab, pv):
        lane = lax.broadcasted_iota(jnp.int32, ab.shape, 1)
        g = -jnp.exp(pv[0:1, :]) * jax.nn.softplus(ab + pv[1:2, :])
        return (jnp.where(lane < GDN_HEADS, g, jnp.where(lane < 2 * GDN_HEADS, jax.nn.sigmoid(ab), 0.0)),)

    return _simple_op(name, f, (s // ROW_TILE,), [_rows(LANES), _whole((8, LANES))], [(_sds((s, LANES)), _rows(LANES))],
                      (ab, pv), (0, 1), {1: (0,)})[0]


def gdn_post(o, gate, w, name):
    s, c = o.shape

    def f(o, g, w):
        heads = [slice(hd * LANES, (hd + 1) * LANES) for hd in range(c // LANES)]
        return (jnp.concatenate([_rms(o[:, hd], w) * jax.nn.silu(g[:, hd]) for hd in heads], axis=1),)

    return _simple_op(name, f, (s // ROW_TILE,), [_rows(c), _rows(c), _whole((1, LANES))], [(_sds((s, c)), _rows(c))],
                      (o, gate, w), (0, 1, 2), {2: (0,)})[0]


def attn_merge(outs, lses, name):
    s, c = outs[0].shape

    def f(o1, o2, o3, l1, l2, l3):
        m = lax.stop_gradient(jnp.maximum(jnp.maximum(l1, l2), l3))
        e1, e2, e3 = jnp.exp(l1 - m), jnp.exp(l2 - m), jnp.exp(l3 - m)
        return ((e1 * o1 + e2 * o2 + e3 * o3) / (e1 + e2 + e3),)

    return _simple_op(name, f, (s // ROW_TILE,), [_rows(c)] * 6, [(_sds((s, c)), _rows(c))], (*outs, *lses), tuple(range(6)))[0]


def loss_rows(y, target, name):
    s, d = y.shape
    nt = s // ROW_TILE

    def f(y, t):
        e = y - t
        part = 0.5 * jnp.sum(jnp.mean(e * e, axis=-1, keepdims=True), axis=0, keepdims=True)
        return (jnp.broadcast_to(part * (1.0 / (8 * LANES)), (8, LANES)),)

    out = _simple_op(name, f, (nt,), [_rows(d)] * 2, [(_sds((nt * 8, LANES)), pl.BlockSpec((8, LANES), lambda i: (i, 0)))],
                     (y, target), (0,))[0]
    return jnp.sum(out)


def _mxu(a, b, form):
    dims = {"nn": ((1,), (0,)), "nt": ((1,), (1,)), "tn": ((0,), (0,))}

    def raw(a, b, form):
        return lax.dot_general(a.astype(bf16), b.astype(bf16), (dims[form], ((), ())), preferred_element_type=f32)

    @jax.custom_vjp
    def prod(a, b):
        return raw(a, b, form)

    def prod_b(res, ct):
        a, b = res
        if form == "nn":
            return raw(ct, b, "nt"), raw(a, ct, "tn")
        if form == "nt":
            return raw(ct, b, "nn"), raw(ct, a, "tn")
        return raw(b, ct, "nt"), raw(a, ct, "nn")

    prod.defvjp(lambda a, b: (raw(a, b, form), (a, b)), prod_b)
    return prod(a, b)


def _masked_heads_attention(q, keys, values, seen):
    dh = ATTN_HEAD_DIM
    outs, lses = [], []
    for hd in range(q.shape[1] // dh):
        at = slice(hd * dh, (hd + 1) * dh)
        sc = jnp.where(seen, _mxu(q[:, at], keys[:, at], "nt"), -jnp.inf)
        m = lax.stop_gradient(jnp.max(sc, axis=-1, keepdims=True))
        p = jnp.exp(sc - m)
        l = jnp.sum(p, axis=-1, keepdims=True)
        outs.append(_mxu(p / l, values[:, at], "nn"))
        lses.append(jnp.broadcast_to(m + jnp.log(l), (q.shape[0], dh)))
    return jnp.concatenate(outs, axis=1), jnp.concatenate(lses, axis=1)


def band_attention(q, k, v, nb, name):
    r, qb, width = q.shape

    def f(q, kp, kc, vp, vc):
        has_prev = (pl.program_id(0) % nb) > 0
        i = lax.broadcasted_iota(jnp.int32, (qb, 2 * qb), 0)
        j = lax.broadcasted_iota(jnp.int32, (qb, 2 * qb), 1)
        seen = jnp.logical_or(jnp.logical_and(jnp.logical_and(j < qb, j >= i), has_prev), jnp.logical_and(j >= qb, j - qb <= i))
        return _masked_heads_attention(q, jnp.concatenate([kp, kc], axis=0), jnp.concatenate([vp, vc], axis=0), seen)

    blk = (None, qb, width)
    cur = pl.BlockSpec(blk, lambda b: (b, 0, 0))
    prev = pl.BlockSpec(blk, lambda b: (jnp.maximum(b - 1, 0), 0, 0))
    shape = _sds((r, qb, width))
    fwd_call, bwd_call = _block_op(name, f, (r,), [cur, prev, cur, prev, cur], [(shape, cur), (shape, cur)],
                                   (q, k, k, v, v), (0, 1, 2, 3, 4), gdefs={1: (shape, cur), 3: (shape, cur)})

    def to_prev(g):
        return jnp.concatenate([g[1:], jnp.zeros_like(g[:1])], axis=0)

    @jax.custom_vjp
    def op(q, k, v):
        return tuple(fwd_call(q, k, k, v, v))

    def op_b(res, cts):
        q, k, v = res
        dq, dkp, dkc, dvp, dvc = bwd_call(q, k, k, v, v, *cts)
        return dq, dkc + to_prev(dkp), dvc + to_prev(dvp)

    op.defvjp(lambda q, k, v: (tuple(fwd_call(q, k, k, v, v)), (q, k, v)), op_b)
    return op(q, k, v)


def dilated_attention(q, k, v, name):
    s = q.shape[0]
    outs, lses = [], []
    for d in DILATIONS:
        length = s // d
        nb = length // QB
        def to_residue(t):
            return t.reshape(length, d, ATTN_WIDTH).transpose(1, 0, 2).reshape(d * nb, QB, ATTN_WIDTH)

        def from_residue(t):
            return t.reshape(d, length, ATTN_WIDTH).transpose(1, 0, 2).reshape(s, ATTN_WIDTH)

        o, lse = band_attention(to_residue(q), to_residue(k), to_residue(v), nb, f"{name}_d{d}")
        outs.append(from_residue(o))
        lses.append(from_residue(lse))
    return attn_merge(outs, lses, name + "_merge")


def cross_attention(q, kv, name):
    s = q.shape[0]
    m = kv.shape[0]
    width = XATTN_HEADS * XATTN_HEAD_DIM
    tq = 512

    def f(q, k, v):
        sc = _mxu(q, k, "nt") * (XATTN_HEAD_DIM ** -0.5)
        mx = lax.stop_gradient(jnp.max(sc, axis=-1, keepdims=True))
        p = jnp.exp(sc - mx)
        return (_mxu(p / jnp.sum(p, axis=-1, keepdims=True), v, "nn"),)

    q_spec = pl.BlockSpec((tq, XATTN_HEAD_DIM), lambda a, i: (i, a))
    k_spec = pl.BlockSpec((m, XATTN_HEAD_DIM), lambda a, i: (0, a))
    v_spec = pl.BlockSpec((m, XATTN_HEAD_DIM), lambda a, i: (0, a + XATTN_HEADS))
    half = _sds((m, width))
    fwd_call, bwd_call = _block_op(name, f, (XATTN_HEADS, s // tq), [q_spec, k_spec, v_spec], [(_sds((s, width)), q_spec)],
                                   (q, kv, kv), (0, 1, 2), acc={1: (1,), 2: (1,)}, gdefs={1: (half, k_spec), 2: (half, k_spec)})

    @jax.custom_vjp
    def op(q, kv):
        return fwd_call(q, kv, kv)[0]

    def op_b(res, ct):
        q, kv = res
        dq, dk, dv = bwd_call(q, kv, kv, ct)
        return dq, jnp.concatenate([dk, dv], axis=1)

    op.defvjp(lambda q, kv: (fwd_call(q, kv, kv)[0], (q, kv)), op_b)
    return op(q, kv)


def _hi(a, b, form="nn"):
    dims = {"nn": ((1,), (0,)), "nt": ((1,), (1,)), "tn": ((0,), (0,))}[form]
    return lax.dot_general(a, b, (dims, ((), ())), precision=lax.Precision.HIGH, preferred_element_type=f32)


def _running_sum(g):
    def raw(x, form):
        c = x.shape[0]
        tri = (lax.broadcasted_iota(jnp.int32, (c, c), 0) >= lax.broadcasted_iota(jnp.int32, (c, c), 1)).astype(bf16)
        hi = x.astype(bf16)
        rest = x - hi.astype(f32)
        mid = rest.astype(bf16)
        low = (rest - mid.astype(f32)).astype(bf16)
        dims = (((1,) if form == "nn" else (0,), (0,)), ((), ()))
        return sum(lax.dot_general(tri, part, dims, preferred_element_type=f32) for part in (hi, mid, low))

    @jax.custom_vjp
    def run(x):
        return raw(x, "nn")

    run.defvjp(lambda x: (raw(x, "nn"), None), lambda _, ct: (raw(ct, "tn"),))
    return run(g)


def _unit_lower_inverse(a):
    c = a.shape[0]
    eye = (lax.broadcasted_iota(jnp.int32, (c, c), 0) == lax.broadcasted_iota(jnp.int32, (c, c), 1)).astype(f32)
    inv, power = eye - a, -a
    for _ in range(c.bit_length() - 2):
        power = _hi(power, power)
        inv = inv + _hi(inv, power)
    return inv


def _known_inverse(a, t):
    @jax.custom_vjp
    def inv(a, t):
        return t

    def inv_b(t, ct):
        return -_hi(_hi(t, ct, "tn"), t, "nt"), jnp.zeros_like(t)

    inv.defvjp(lambda a, t: (t, t), inv_b)
    return inv(a, t)


def _delta_chunk(q, k, v, g, beta, s0, known_inv=None):
    c = q.shape[0]
    i = lax.broadcasted_iota(jnp.int32, (c, c), 0)
    j = lax.broadcasted_iota(jnp.int32, (c, c), 1)
    causal, strict = i >= j, i > j
    dec = _running_sum(g)
    dec_i = dec[:, :c]
    rel = jnp.exp(jnp.where(causal, dec_i - dec_i.T, -jnp.inf))
    k_beta = k * beta
    on_k = _mxu(jnp.concatenate([k_beta, q], axis=0), k, "nt")
    a = jnp.where(strict, on_k[:c] * rel, 0.0)
    attn = jnp.where(causal, on_k[c:] * rel, 0.0)
    inv = _unit_lower_inverse(a) if known_inv is None else _known_inverse(a, known_inv)
    e_dec = jnp.exp(dec)
    solved = _hi(inv, jnp.concatenate([v * beta, k_beta * e_dec], axis=1))
    u, w = solved[:, :v.shape[1]], solved[:, v.shape[1]:]
    total = jnp.sum(g, axis=0, keepdims=True)
    on_state = _mxu(jnp.concatenate([w, q * e_dec], axis=0), s0, "nn")
    v_new = u - on_state[:c]
    o = on_state[c:] + _mxu(attn, v_new, "nn")
    s1 = s0 * jnp.exp(total) + _mxu(k * jnp.exp(total - dec), v_new, "tn")
    return o, s1, inv


def _delta_rule_call(name, walk, n, in_specs, out_specs, out_shape, operands, exchange):
    n_in, n_out = len(in_specs), len(out_specs)
    carried = len(exchange.operands) if exchange else 0

    def body(*refs):
        ins, refs = refs[:n_in], refs[n_in:]
        x_refs, refs = refs[:carried], refs[carried:]
        outs, refs = refs[:n_out], refs[n_out:]
        land_refs, (state, *sems) = refs[:carried], refs[carried:]
        step = pl.program_id(0)
        if exchange:
            start, finish = exchange.bind(x_refs, land_refs, sems)
            pl.when(step == 0)(start)

        @pl.when(step == 0)
        def _():
            state[...] = jnp.zeros_like(state)

        walk(ins, outs, state)
        if exchange:
            pl.when(step == n - 1)(finish)

    return pl.pallas_call(
        body, name=name, grid=(n,), in_specs=list(in_specs) + [ANY] * carried, out_specs=list(out_specs) + [ANY] * carried,
        out_shape=list(out_shape) + (exchange.out_shapes if exchange else []),
        scratch_shapes=[pltpu.VMEM((GDN_HEAD_DIM, GDN_WIDTH), f32)] + (exchange.scratch if exchange else []),
        compiler_params=_params(1))(*operands, *(exchange.operands if exchange else []))


def _delta_heads():
    heads = [slice(hd * GDN_HEAD_DIM, (hd + 1) * GDN_HEAD_DIM) for hd in range(GDN_HEADS)]
    inv_at = [slice(hd * GDN_CHUNK, (hd + 1) * GDN_CHUNK) for hd in range(GDN_HEADS)]
    return heads, inv_at


def _head_chunk(q, k, v, gates, s0, head, known_inv=None):
    g = jnp.broadcast_to(gates[:, head:head + 1], q.shape)
    beta = jnp.broadcast_to(gates[:, GDN_HEADS + head:GDN_HEADS + head + 1], q.shape)
    return _delta_chunk(q, k, v, g, beta, s0, known_inv)


def delta_rule_fwd(q, k, v, gates, name, exchange=None):
    s, width = q.shape
    c, dk = GDN_CHUNK, GDN_HEAD_DIM
    n = s // c
    heads, inv_at = _delta_heads()

    def walk(ins, outs, state):
        q_ref, k_ref, v_ref, gates_ref = ins
        o_ref, s_in_ref, inv_ref = outs
        s_in_ref[...] = state[...]
        gates = gates_ref[...]
        xs = [[r[:, hd] for r in (q_ref, k_ref, v_ref)] + [gates, state[:, hd], i] for i, hd in enumerate(heads)]
        ys = [_head_chunk(*x) for x in xs]
        for hd, at, (o, s1, inv) in zip(heads, inv_at, ys):
            o_ref[:, hd], state[:, hd], inv_ref[:, at] = o, s1, inv

    blk = pl.BlockSpec((c, width), lambda t: (t, 0))
    gt = pl.BlockSpec((c, LANES), lambda t: (t, 0))
    st = pl.BlockSpec((dk, width), lambda t: (t, 0))
    iv = pl.BlockSpec((c, GDN_HEADS * c), lambda t: (t, 0))
    return _delta_rule_call(name, walk, n, [blk] * 3 + [gt], [blk, st, iv],
                            [_sds((s, width)), _sds((n * dk, width)), _sds((s, GDN_HEADS * c))], (q, k, v, gates), exchange)


def delta_rule_bwd(q, k, v, gates, s_in, inv, do, name, exchange=None):
    s, width = q.shape
    c, dk = GDN_CHUNK, GDN_HEAD_DIM
    n = s // c
    heads, inv_at = _delta_heads()

    def walk(ins, outs, dstate):
        q_ref, k_ref, v_ref, gates_ref, s_ref, inv_ref, do_ref = ins
        dq_ref, dk_ref, dv_ref, dgates_ref = outs
        gates = gates_ref[...]
        xs = [[r[:, hd] for r in (q_ref, k_ref, v_ref)] + [gates, s_ref[:, hd]] for hd in heads]
        known = [inv_ref[:, at] for at in inv_at]
        cts = [(do_ref[:, hd], dstate[:, hd]) for hd in heads]
        grads = []
        for i, (x, t, ct) in enumerate(zip(xs, known, cts)):
            _, vjp = jax.vjp(lambda *y, t=t, i=i: _head_chunk(*y, i, known_inv=t)[:2], *x)
            grads.append(vjp(ct))
        dgates = grads[0][3]
        for g in grads[1:]:
            dgates = dgates + g[3]
        dgates_ref[...] = dgates
        for hd, (dq, dk_, dv, _, ds0) in zip(heads, grads):
            dq_ref[:, hd], dk_ref[:, hd], dv_ref[:, hd], dstate[:, hd] = dq, dk_, dv, ds0

    blk = pl.BlockSpec((c, width), lambda t: (n - 1 - t, 0))
    gt = pl.BlockSpec((c, LANES), lambda t: (n - 1 - t, 0))
    st = pl.BlockSpec((dk, width), lambda t: (n - 1 - t, 0))
    iv = pl.BlockSpec((c, GDN_HEADS * c), lambda t: (n - 1 - t, 0))
    return _delta_rule_call(name, walk, n, [blk] * 3 + [gt, st, iv, blk], [blk] * 3 + [gt],
                            [_sds((s, width))] * 3 + [_sds((s, LANES))], (q, k, v, gates, s_in, inv, do), exchange)


def adamw(w, g, m, v, name):
    shape = w.shape
    if len(shape) == 2:
        grid, spec = (1,), pl.BlockSpec(shape, lambda i: (0, 0))
    else:
        tile = shape[1] if shape[1] <= 512 else _pick(shape[1], (512, 256, 128))
        grid, spec = (shape[0], shape[1] // tile), pl.BlockSpec((None, tile, shape[2]), lambda layer, i: (layer, i, 0))

    def body(w_ref, g_ref, m_ref, v_ref, d_ref, nm_ref, nv_ref):
        grad = g_ref[...]
        nm = ADAM_B1 * m_ref[...] + (1.0 - ADAM_B1) * grad
        nv = ADAM_B2 * v_ref[...] + (1.0 - ADAM_B2) * (grad * grad)
        m_hat = nm / (1.0 - ADAM_B1 ** ADAM_STEP)
        v_hat = nv / (1.0 - ADAM_B2 ** ADAM_STEP)
        d_ref[...] = -ADAM_LR * (m_hat / (jnp.sqrt(v_hat) + ADAM_EPS) + ADAM_WD * w_ref[...])
        nm_ref[...] = nm
        nv_ref[...] = nv

    return tuple(pl.pallas_call(body, name=name, grid=grid, in_specs=[spec] * 4, out_specs=[spec] * 3,
                                out_shape=[_sds(shape)] * 3, compiler_params=_params(len(grid)))(w, g, m, v))


def _place():
    return lax.axis_index("x"), lax.axis_index("y"), lax.axis_index("c")


def _flip(p, bits):
    return tuple(1 - v if (bits >> s) & 1 else v for v, s in zip(p, (2, 1, 0)))


def _slot(p):
    return 4 * p[0] + 2 * p[1] + p[2]


def _chip_of(p):
    return 2 * p[0] + p[1]


ANY = pl.BlockSpec(memory_space=pl.ANY)


class Gather:
    scratch = (pltpu.SemaphoreType.DMA((7,)), pltpu.SemaphoreType.DMA((7,)), pltpu.SemaphoreType.DMA)

    def __init__(self, shard):
        self.operand = shard
        self.out_shape = jax.ShapeDtypeStruct((N_DEV,) + shard.shape, shard.dtype)

    def bind(self, x_ref, out_ref, send_sems, recv_sems, local_sem):
        me = _place()
        sibling = _flip(me, 1)
        chips = [_flip(me, 4), _flip(me, 2), _flip(me, 6)]

        def copy(k, block, to, src=None):
            return pltpu.make_async_remote_copy(
                src_ref=out_ref.at[_slot(block)] if src is None else src, dst_ref=out_ref.at[_slot(block)],
                send_sem=send_sems.at[k], recv_sem=recv_sems.at[k], device_id=to, device_id_type=MESH)

        mine = pltpu.make_async_copy(x_ref, out_ref.at[_slot(me)], local_sem)
        first = [copy(0, me, sibling, src=x_ref)] + [copy(1 + j, me, chip, src=x_ref) for j, chip in enumerate(chips)]
        passed = [copy(4 + j, chip, sibling) for j, chip in enumerate(chips)]

        def start():
            mine.start()
            for cp in first:
                cp.start()

        def finish():
            for j, chip in enumerate(chips):
                copy(1 + j, chip, me).wait_recv()
                passed[j].start()
            copy(0, sibling, me).wait_recv()
            for j, chip in enumerate(chips):
                copy(4 + j, _flip(chip, 1), me).wait_recv()
            for cp in first + passed:
                cp.wait_send()
            mine.wait()

        return start, finish


class ChipExchange:
    scratch = (pltpu.SemaphoreType.DMA((3,)), pltpu.SemaphoreType.DMA((3,)), pltpu.SemaphoreType.DMA)

    def __init__(self, blocks):
        self.operand = blocks
        self.out_shape = jax.ShapeDtypeStruct(blocks.shape, blocks.dtype)

    def bind(self, x_ref, out_ref, send_sems, recv_sems, local_sem):
        me = _place()
        peers = [_flip(me, 4), _flip(me, 2), _flip(me, 6)]
        mine = pltpu.make_async_copy(x_ref.at[_chip_of(me)], out_ref.at[_chip_of(me)], local_sem)

        def copy(j, src_chip, dst_chip):
            return pltpu.make_async_remote_copy(
                src_ref=x_ref.at[src_chip], dst_ref=out_ref.at[dst_chip], send_sem=send_sems.at[j],
                recv_sem=recv_sems.at[j], device_id=peers[j], device_id_type=MESH)

        sends = [copy(j, _chip_of(peer), _chip_of(me)) for j, peer in enumerate(peers)]

        def start():
            mine.start()
            for cp in sends:
                cp.start()

        def finish():
            for j, peer in enumerate(peers):
                copy(j, _chip_of(me), _chip_of(peer)).wait_recv()
            for cp in sends:
                cp.wait_send()
            mine.wait()

        return start, finish


class Together:
    def __init__(self, *parts):
        self.parts = parts
        self.operands = [p.operand for p in parts]
        self.out_shapes = [p.out_shape for p in parts]
        self.scratch = [s for p in parts for s in p.scratch]

    def bind(self, x_refs, out_refs, sems):
        bound, at = [], 0
        for p, x_ref, out_ref in zip(self.parts, x_refs, out_refs):
            bound.append(p.bind(x_ref, out_ref, *sems[at:at + len(p.scratch)]))
            at += len(p.scratch)

        def start():
            for s, _ in bound:
                s()

        def finish():
            for _, f in bound:
                f()

        return start, finish


def exchange_alone(exchange, name):
    n = len(exchange.operands)

    def body(*refs):
        start, finish = exchange.bind(refs[:n], refs[n:2 * n], refs[2 * n:])
        start()
        finish()

    return pl.pallas_call(body, name=name, out_shape=exchange.out_shapes, in_specs=[ANY] * n, out_specs=[ANY] * n,
                          scratch_shapes=exchange.scratch)(*exchange.operands)


def _row_tile(rows):
    return max([t for t in range(16, min(rows, 1024) + 1, 16) if rows % t == 0] or [rows])


def pair_exchange(blocks, name):
    n = len(blocks)

    def body(*refs):
        x_refs, theirs_refs, (send_sems, recv_sems) = refs[:n], refs[n:2 * n], refs[2 * n:]
        me = _place()
        remote = [pltpu.make_async_remote_copy(
            src_ref=x_refs[t].at[2 * q + 1 - me[2]], dst_ref=theirs_refs[t].at[q], send_sem=send_sems.at[4 * t + q],
            recv_sem=recv_sems.at[4 * t + q], device_id=_flip(me, 1), device_id_type=MESH) for t in range(n) for q in range(4)]
        for cp in remote:
            cp.start()
        for cp in remote:
            cp.wait()

    return pl.pallas_call(
        body, name=name, out_shape=[jax.ShapeDtypeStruct((4,) + b.shape[1:], b.dtype) for b in blocks], in_specs=[ANY] * n,
        out_specs=[ANY] * n, scratch_shapes=[pltpu.SemaphoreType.DMA((4 * n,)), pltpu.SemaphoreType.DMA((4 * n,))])(*blocks)


def pair_add(blocks, theirs, name):
    n, rows, width = theirs.shape
    tile = _row_tile(rows)
    spec = pl.BlockSpec((None, tile, width), lambda q, i: (q, i, 0))
    south = pl.BlockSpec((None, None, tile, width), lambda q, i: (q, 0, i, 0))
    north = pl.BlockSpec((None, None, tile, width), lambda q, i: (q, 1, i, 0))

    def body(s_ref, n_ref, b_ref, o_ref):
        mine = jnp.where(lax.axis_index("c") == 0, s_ref[...], n_ref[...])
        o_ref[...] = (mine.astype(f32) + b_ref[...].astype(f32)).astype(o_ref.dtype)

    by_core = blocks.reshape(n, 2, rows, width)
    return pl.pallas_call(body, name=name, grid=(n, rows // tile), in_specs=[south, north, spec], out_specs=spec,
                          out_shape=jax.ShapeDtypeStruct(theirs.shape, theirs.dtype), compiler_params=_params(2))(by_core, by_core, theirs)


def sum_slots(blocks, name):
    n, rows, width = blocks.shape
    tile = _row_tile(rows)

    def body(x_ref, o_ref):
        total = x_ref[0].astype(f32)
        for s in range(1, n):
            total = total + x_ref[s].astype(f32)
        o_ref[...] = total

    return pl.pallas_call(
        body, name=name, grid=(rows // tile,), in_specs=[pl.BlockSpec((n, tile, width), lambda i: (0, i, 0))],
        out_specs=pl.BlockSpec((tile, width), lambda i: (i, 0)), out_shape=_sds((rows, width)), compiler_params=_params(1))(blocks)


def all_reduce_small(x, name):
    rows, width = x.shape

    def body(x_ref, o_ref, land, send_sems, recv_sems):
        me = _place()
        copies = []
        for k in range(1, N_DEV):
            peer = _flip(me, k)
            copies.append(pltpu.make_async_remote_copy(
                src_ref=x_ref, dst_ref=land.at[_slot(me)], send_sem=send_sems.at[k - 1], recv_sem=recv_sems.at[k - 1],
                device_id=peer, device_id_type=MESH))
        for cp in copies:
            cp.start()
        land[_slot(me)] = x_ref[...]
        for k in range(1, N_DEV):
            peer = _flip(me, k)
            pltpu.make_async_remote_copy(
                src_ref=x_ref, dst_ref=land.at[_slot(peer)], send_sem=send_sems.at[k - 1], recv_sem=recv_sems.at[k - 1],
                device_id=peer, device_id_type=MESH).wait_recv()
        total = land[0]
        for s in range(1, N_DEV):
            total = total + land[s]
        o_ref[...] = total
        for cp in copies:
            cp.wait_send()

    return pl.pallas_call(
        body, name=name, out_shape=_sds((rows, width)), in_specs=[pl.BlockSpec(memory_space=pltpu.VMEM)],
        out_specs=pl.BlockSpec(memory_space=pltpu.VMEM),
        scratch_shapes=[pltpu.VMEM((N_DEV, rows, width), f32), pltpu.SemaphoreType.DMA((7,)), pltpu.SemaphoreType.DMA((7,))],
    )(x)


def _pack_big(shards):
    packed = {name: shards[name].astype(bf16) for name in COL_SHARDED}
    packed["rows"] = jnp.concatenate([shards[name].astype(bf16) for name, _ in ROW_SHARDED], axis=1)
    return packed


def _unpack_gathered(gathered):
    full = {}
    for name, part in gathered.items():
        if name == "w_gate_up":
            full[name] = part
        elif name in COL_SHARDED:
            full[name] = part.transpose(1, 0, 2).reshape(D_MODEL, N_DEV * part.shape[2])
        else:
            at = 0
            for weight, rows in ROW_SHARDED:
                full[weight] = part[:, at:at + rows, :].reshape(N_DEV * rows, D_MODEL)
                at += rows
    return full


def _pack_grads(grads, group):
    packed = {}
    for name in group:
        if name == "w_gate_up":
            packed[name] = grads[name]
        elif name == "rows":
            packed[name] = jnp.concatenate([grads[weight].reshape(N_DEV, rows, D_MODEL) for weight, rows in ROW_SHARDED], axis=1)
        else:
            packed[name] = grads[name].reshape(D_MODEL, N_DEV, grads[name].shape[1] // N_DEV).transpose(1, 0, 2)
    return packed


def _unpack_shard(layers):
    out = {name: jnp.stack([layer[name] for layer in layers]) for name in COL_SHARDED}
    rows_pack, at = jnp.stack([layer["rows"] for layer in layers]), 0
    for weight, rows in ROW_SHARDED:
        out[weight] = rows_pack[:, at:at + rows, :]
        at += rows
    return out


def _rows_of(flat_len):
    return -(-flat_len // (8 * D_MODEL)) * 8


def _pack_small(parts):
    flat = jnp.concatenate([p.reshape(-1) for p in parts])
    rows = _rows_of(flat.shape[0])
    flat = jnp.pad(flat, (0, rows * D_MODEL - flat.shape[0]))
    return flat.reshape(rows, D_MODEL)


def _unpack_small(packed, like):
    flat, out, at = packed.reshape(-1), [], 0
    for p in like:
        out.append(flat[at:at + p.size].reshape(p.shape))
        at += p.size
    return out


def _rope_tables(positions):
    inv_freq = jnp.float32(ROPE_THETA) ** (-jnp.arange(0, ROPE_DIM, 2, dtype=f32) / ROPE_DIM)
    ang = positions.astype(f32)[:, None] * inv_freq
    cos, sin = jnp.cos(ang), jnp.sin(ang)
    rest = ATTN_HEAD_DIM - ROPE_DIM
    cos_h = jnp.concatenate([cos, cos, jnp.ones((cos.shape[0], rest), f32)], axis=1)
    sin_h = jnp.concatenate([-sin, sin, jnp.zeros((sin.shape[0], rest), f32)], axis=1)
    return jnp.tile(cos_h, (1, ATTN_HEADS)), jnp.tile(sin_h, (1, ATTN_HEADS))


HEAD_SMALL = ("norm_mix_pre", "conv_short", "conv_gdn", "gdn_a_log", "gdn_dt_bias")


def _layer_head(h, p, cos_t, sin_t):
    hn = rms_norm(h, p["norm_mix_pre"][None], "norm_mix_pre")
    aw, cw, gw = ATTN_WIDTH, CONV_WIDTH, GDN_WIDTH
    aq, ak, av, cb, cc, cx, gqkv, ab, gate = _split_cols(_linear(hn, p["w_in"], "w_in"),
                                                         (aw, aw, aw, cw, cw, cw, 3 * gw, 2 * GDN_HEADS, gw))
    ab = jnp.pad(ab, ((0, 0), (0, LANES - 2 * GDN_HEADS)))
    y_attn = dilated_attention(rope(aq, cos_t, sin_t, ATTN_HEAD_DIM ** -0.5, "rope_q"), rope(ak, cos_t, sin_t, 1.0, "rope_k"),
                               av, "attn")
    y_conv = short_conv(cb, cc, cx, p["conv_short"], "short_conv")
    qkv = gdn_pre(gqkv, p["conv_gdn"], "gdn_pre")
    pv = jnp.zeros((8, LANES), f32).at[0, :GDN_HEADS].set(p["gdn_a_log"]).at[1, :GDN_HEADS].set(p["gdn_dt_bias"])
    return (*_split_cols(qkv, (gw, gw, gw)), gate_beta(ab, pv, "gate_beta")), (gate, y_attn, y_conv)


MID_PARAMS = ("gdn_norm", "w_out", "norm_mix_post", "norm_xattn_pre", "w_xq", "norm_mem", "w_xkv", "w_xo", "norm_xattn_post",
              "norm_ffn_pre")


def _layer_mid(h, o, gate, y_attn, y_conv, p, mem):
    y_gdn = gdn_post(o, gate, p["gdn_norm"][None], "gdn_post")
    mix = _joined_linear((y_attn, y_conv, y_gdn), p["w_out"], "w_out")
    h, hn = add_norm_then_norm(h, mix, p["norm_mix_post"][None], p["norm_xattn_pre"][None], "norm_mix_xattn")
    qx = _linear(hn, p["w_xq"], "w_xq")
    kv = _linear(rms_norm(mem, p["norm_mem"][None], "norm_mem"), p["w_xkv"], "w_xkv")
    xa = _linear(cross_attention(qx, kv, "xattn"), p["w_xo"], "w_xo")
    return add_norm_then_norm(h, xa, p["norm_xattn_post"][None], p["norm_ffn_pre"][None], "norm_xattn_ffn")


def _pair_summed(grads, group, name):
    blocks = _pack_grads(grads, group)
    theirs = pair_exchange([blocks[n] for n in group], name + "_pair_exchange")
    return [pair_add(blocks[n], t, f"{name}_pair_add_{n}") for n, t in zip(group, theirs)]


def _forward_backward(x, packed, small, mem, cos_t, sin_t, target):
    def gathers(group, layer):
        return [Gather(packed[n][layer]) for n in group]

    h = x
    head_gathered = exchange_alone(Together(*gathers(HEAD_GROUP, 0)), "gather_first")
    saved = []
    for layer in range(DEPTH):
        at_layer = {n: t[layer] for n, t in small.items()}
        head_p = {**_unpack_gathered(dict(zip(HEAD_GROUP, head_gathered))), **{n: at_layer[n] for n in HEAD_SMALL}}
        (rule_in, rest), head_vjp = jax.vjp(lambda h, hp: _layer_head(h, hp, cos_t, sin_t), h, head_p)
        carried = gathers(TAIL_GROUP, layer) + (gathers(HEAD_GROUP, layer + 1) if layer + 1 < DEPTH else [])
        o, s_in, inv, *landed = delta_rule_fwd(*rule_in, "delta_rule_fwd", Together(*carried))
        head_gathered = landed[len(TAIL_GROUP):]
        tail_p = {**_unpack_gathered(dict(zip(TAIL_GROUP, landed))), **at_layer}
        mid_p = {n: tail_p[n] for n in MID_PARAMS}
        (h, hn), mid_vjp = jax.vjp(lambda h, o, rest, mp: _layer_mid(h, o, *rest, mp, mem), h, o, rest, mid_p)
        y, ffn_saved = ffn_forward(hn, tail_p["w_gate_up"], tail_p["w_down"], "ffn")
        h, last_vjp = jax.vjp(lambda h, y, w: add_norm(h, y, w[None], "norm_ffn_post"), h, y, tail_p["norm_ffn_post"])
        saved.append((head_vjp, mid_vjp, last_vjp, ffn_saved, rule_in, s_in, inv))

    loss, dh = jax.value_and_grad(lambda y: loss_rows(y, target, "loss"))(h)

    def summed(group, landed):
        return {n: sum_slots(t, "sum_grads_" + n) for n, t in zip(group, landed)}

    big_grads, small_grads, head_pending = [{} for _ in range(DEPTH)], [None] * DEPTH, []
    for layer in reversed(range(DEPTH)):
        head_vjp, mid_vjp, last_vjp, ffn_saved, rule_in, s_in, inv = saved[layer]
        dh, dy, d_norm_ffn_post = last_vjp(dh)
        dhn, d_gate_up, d_down, landed = ffn_backward(
            ffn_saved, dy, "ffn", Together(*[ChipExchange(t) for t in head_pending]) if head_pending else None)
        if head_pending:
            big_grads[layer + 1].update(summed(HEAD_GROUP, landed))
        dh_mid, do, d_rest, d_mid_p = mid_vjp((dh, dhn))
        d_tail_p = {**d_mid_p, "w_gate_up": d_gate_up, "w_down": d_down, "norm_ffn_post": d_norm_ffn_post}
        carried = Together(*[ChipExchange(t) for t in _pair_summed(d_tail_p, TAIL_GROUP, "tail")])
        *d_rule_in, = delta_rule_bwd(*rule_in, s_in, inv, do, "delta_rule_bwd", carried)
        big_grads[layer].update(summed(TAIL_GROUP, d_rule_in[4:]))
        dh_head, d_head_p = head_vjp((tuple(d_rule_in[:4]), d_rest))
        dh = dh_mid + dh_head
        small_grads[layer] = {n: t for n, t in {**d_head_p, **d_tail_p}.items() if n in small}
        head_pending = _pair_summed(d_head_p, HEAD_GROUP, "head")
    landed = exchange_alone(Together(*[ChipExchange(t) for t in head_pending]), "exchange_last")
    big_grads[0].update(summed(HEAD_GROUP, landed))
    return loss, dh, big_grads, small_grads


def kernel(x, mem, positions, norm_mix_pre, norm_mix_post, w_in, conv_short, conv_gdn, gdn_a_log, gdn_dt_bias, gdn_norm, w_out, norm_mem, norm_xattn_pre, norm_xattn_post, w_xq, w_xkv, w_xo, norm_ffn_pre, norm_ffn_post, w_gate_up, w_down, loss_target, m_norm_mix_pre, m_norm_mix_post, m_w_in, m_conv_short, m_conv_gdn, m_gdn_a_log, m_gdn_dt_bias, m_gdn_norm, m_w_out, m_norm_mem, m_norm_xattn_pre, m_norm_xattn_post, m_w_xq, m_w_xkv, m_w_xo, m_norm_ffn_pre, m_norm_ffn_post, m_w_gate_up, m_w_down, v_norm_mix_pre, v_norm_mix_post, v_w_in, v_conv_short, v_conv_gdn, v_gdn_a_log, v_gdn_dt_bias, v_gdn_norm, v_w_out, v_norm_mem, v_norm_xattn_pre, v_norm_xattn_post, v_w_xq, v_w_xkv, v_w_xo, v_norm_ffn_pre, v_norm_ffn_post, v_w_gate_up, v_w_down):
    given = dict(locals())
    weights = {n: given[n] for n in WEIGHTS}
    me = _slot(_place())

    def in_place(shard):
        full = jnp.zeros(shard.shape[:-1] + (shard.shape[-1] * N_DEV,), f32)
        return lax.dynamic_update_slice_in_dim(full, shard, me * shard.shape[-1], axis=shard.ndim - 1)

    placed = [in_place(conv_short), in_place(conv_gdn)]
    conv_short_full, conv_gdn_full = _unpack_small(all_reduce_small(_pack_small(placed), "gather_conv"), placed)
    small = {n: weights[n] for n in NORMS + ("gdn_a_log", "gdn_dt_bias", "gdn_norm")}
    small["conv_short"], small["conv_gdn"] = conv_short_full, conv_gdn_full

    cos_t, sin_t = _rope_tables(positions[0])
    loss, grad_x, big_layers, small_layers = _forward_backward(
        x[0], _pack_big(weights), small, mem[0], cos_t, sin_t, loss_target[0])
    grads = _unpack_shard(big_layers)

    names = sorted(small)
    parts = [jnp.stack([layer[n] for layer in small_layers]) for n in names] + [loss.reshape(1)]
    reduced = _unpack_small(all_reduce_small(_pack_small(parts), "reduce_small"), parts)
    loss = reduced[-1][0]
    for n, g in zip(names, reduced[:-1]):
        if n in ("conv_short", "conv_gdn"):
            width = weights[n].shape[-1]
            g = lax.dynamic_slice_in_dim(g, me * width, width, axis=g.ndim - 1)
        grads[n] = g

    delta, new_m, new_v = {}, {}, {}
    for n in WEIGHTS:
        delta[n], new_m[n], new_v[n] = adamw(weights[n], grads[n], given["m_" + n], given["v_" + n], "adamw_" + n)
    return (loss, grad_x[None], *[grads[n] for n in WEIGHTS], *[delta[n] for n in WEIGHTS],
            *[new_m[n] for n in WEIGHTS], *[new_v[n] for n in WEIGHTS])
```

```python
import functools

import jax
import jax.numpy as jnp
from jax import lax
from jax.experimental import pallas as pl
from jax.experimental.pallas import tpu as pltpu

f32 = jnp.float32
bf16 = jnp.bfloat16
MESH = pl.DeviceIdType.MESH

N_DEV = 8
DEPTH = 4
D_MODEL = 1024
EPS = 1e-6
ATTN_HEADS, ATTN_HEAD_DIM = 4, 64
ATTN_WIDTH = ATTN_HEADS * ATTN_HEAD_DIM
DILATIONS = (1, 4, 16)
QB = 128
ROPE_THETA = 500000.0
ROPE_DIM = ATTN_HEAD_DIM // 4
CONV_WIDTH = 256
GDN_HEADS, GDN_HEAD_DIM = 4, 128
GDN_WIDTH = GDN_HEADS * GDN_HEAD_DIM
GDN_CHUNK = 64
XATTN_HEADS, XATTN_HEAD_DIM = 4, 256
LANES = 128
ROW_TILE = 512
VMEM_LIMIT = 56 * 1024 * 1024

ADAM_LR, ADAM_B1, ADAM_B2, ADAM_EPS, ADAM_WD, ADAM_STEP = 0.001, 0.9, 0.999, 1e-08, 0.01, 10

COL_SHARDED = ("w_in", "w_xkv", "w_gate_up")
ROW_SHARDED = (("w_out", 128), ("w_xq", 128), ("w_xo", 128), ("w_down", 352))
HEAD_GROUP = ("w_in",)
TAIL_GROUP = ("w_gate_up", "w_xkv", "rows")
NORMS = ("norm_mix_pre", "norm_mix_post", "norm_mem", "norm_xattn_pre", "norm_xattn_post", "norm_ffn_pre", "norm_ffn_post")
WEIGHTS = ("norm_mix_pre", "norm_mix_post", "w_in", "conv_short", "conv_gdn", "gdn_a_log", "gdn_dt_bias", "gdn_norm", "w_out",
           "norm_mem", "norm_xattn_pre", "norm_xattn_post", "w_xq", "w_xkv", "w_xo", "norm_ffn_pre", "norm_ffn_post",
           "w_gate_up", "w_down")


def _params(n_grid):
    return pltpu.CompilerParams(dimension_semantics=("arbitrary",) * n_grid, vmem_limit_bytes=VMEM_LIMIT)


def _pick(n, cands):
    for c in cands:
        if n % c == 0:
            return c
    return n


MXU_FLOPS = 9.0e14
HBM_BYTES_PER_S = 2.5e12
VMEM_RMW_BYTES_PER_S = 7.0e12
STEP_S = 1.0e-6
MATMUL_VMEM = 44 * 1024 * 1024


def _tiles(m, n, k, sa, sb, so):
    def divisors(d):
        return sorted({d} | {d // s for s in range(1, d // LANES + 1) if d % s == 0 and (d // s) % LANES == 0}, reverse=True)

    best = None
    for tk in divisors(k):
        nk = k // tk
        for tm in divisors(m):
            for tn_ in divisors(n):
                per_step = tm * tk * sa + tk * tn_ * sb + tm * tn_ * so
                vmem = 2 * per_step + (tm * tn_ * 4 if nk > 1 else 0)
                vmem += (tm * tk * 2 if sa == 4 else 0) + (tk * tn_ * 2 if sb == 4 else 0) + tm * tn_ * 4
                if vmem > MATMUL_VMEM:
                    continue
                moved = m * k * sa * (1 if nk == 1 else n // tn_) + k * n * sb * (1 if nk == 1 and n == tn_ else m // tm) + m * n * so
                busy = 2 * m * n * k / MXU_FLOPS + (m * n * 8 * nk / VMEM_RMW_BYTES_PER_S if nk > 1 else 0)
                cost = max(moved / HBM_BYTES_PER_S, busy) + per_step / HBM_BYTES_PER_S + (m // tm) * (n // tn_) * nk * STEP_S
                if best is None or cost < best[0]:
                    best = (cost, tm, tn_, tk)
    return best[1:]


def _mm(a, b, ta, tb, out_dtype, name):
    m, k = (a.shape[1], a.shape[0]) if ta else a.shape
    n = b.shape[0] if tb else b.shape[1]
    tm, tn, tk = _tiles(m, n, k, a.dtype.itemsize, b.dtype.itemsize, jnp.dtype(out_dtype).itemsize)
    nk = k // tk
    a_spec = pl.BlockSpec((tk, tm), lambda i, j, kk: (kk, i)) if ta else pl.BlockSpec((tm, tk), lambda i, j, kk: (i, kk))
    b_spec = pl.BlockSpec((tn, tk), lambda i, j, kk: (j, kk)) if tb else pl.BlockSpec((tk, tn), lambda i, j, kk: (kk, j))
    dims = (((0 if ta else 1,), (1 if tb else 0,)), ((), ()))

    def body(a_ref, b_ref, o_ref, *acc):
        kk = pl.program_id(2)
        p = lax.dot_general(a_ref[...].astype(bf16), b_ref[...].astype(bf16), dims, preferred_element_type=f32)
        if nk == 1:
            o_ref[...] = p.astype(o_ref.dtype)
            return
        acc_ref, = acc

        @pl.when(kk == 0)
        def _():
            acc_ref[...] = p

        @pl.when(kk > 0)
        def _():
            acc_ref[...] += p

        @pl.when(kk == nk - 1)
        def _():
            o_ref[...] = acc_ref[...].astype(o_ref.dtype)

    return pl.pallas_call(
        body, name=name, grid=(m // tm, n // tn, nk), in_specs=[a_spec, b_spec],
        out_specs=pl.BlockSpec((tm, tn), lambda i, j, kk: (i, j)), out_shape=jax.ShapeDtypeStruct((m, n), out_dtype),
        scratch_shapes=[pltpu.VMEM((tm, tn), f32)] if nk > 1 else [], compiler_params=_params(3))(a, b)


def _linear(x, w, name):
    @jax.custom_vjp
    def lin(x, w):
        return _mm(x, w, False, False, f32, name + "_y")

    def lin_f(x, w):
        return _mm(x, w, False, False, f32, name + "_y"), (x, w)

    def lin_b(res, dy):
        x, w = res
        return _mm(dy, w, False, True, f32, name + "_dx"), _mm(x, dy, True, False, bf16, name + "_dw")

    lin.defvjp(lin_f, lin_b)
    return lin(x, w)


def _bdot(a, b, form):
    dims = {"nn": ((1,), (0,)), "nt": ((1,), (1,)), "tn": ((0,), (0,))}[form]
    return lax.dot_general(a.astype(bf16), b.astype(bf16), (dims, ((), ())), preferred_element_type=f32)


def ffn_forward(hn, w_gate_up, w_down, name):
    s, k = hn.shape
    n_blocks, _, width = w_gate_up.shape
    half = n_blocks // 2
    tm = 1024
    blocked = jax.ShapeDtypeStruct((half, s, width), bf16)

    def act_body(x_ref, wg_ref, wu_ref, gate_ref, up_ref, act_ref):
        x = x_ref[...]
        gate, up = _bdot(x, wg_ref[...], "nn"), _bdot(x, wu_ref[...], "nn")
        gate_ref[...], up_ref[...] = gate.astype(bf16), up.astype(bf16)
        act_ref[...] = (jax.nn.silu(gate) * up).astype(bf16)

    tile = pl.BlockSpec((None, tm, width), lambda i, d: (d, i, 0))
    gate, up, act = pl.pallas_call(
        act_body, name=name + "_act", grid=(s // tm, half),
        in_specs=[pl.BlockSpec((tm, k), lambda i, d: (i, 0)), pl.BlockSpec((None, k, width), lambda i, d: (d, 0, 0)),
                  pl.BlockSpec((None, k, width), lambda i, d: (d + half, 0, 0))],
        out_specs=[tile] * 3, out_shape=[blocked] * 3, compiler_params=_params(2))(hn, w_gate_up, w_gate_up)

    n = w_down.shape[1]
    tn = 512

    def y_body(act_ref, w_ref, y_ref):
        y_ref[...] = sum(_bdot(act_ref[d], w_ref[d * width:(d + 1) * width, :], "nn") for d in range(half))

    y = pl.pallas_call(
        y_body, name=name + "_y", grid=(s // tm, n // tn),
        in_specs=[pl.BlockSpec((half, tm, width), lambda i, j: (0, i, 0)), pl.BlockSpec((half * width, tn), lambda i, j: (0, j))],
        out_specs=pl.BlockSpec((tm, tn), lambda i, j: (i, j)), out_shape=_sds((s, n)), compiler_params=_params(2))(act, w_down)
    return y, (hn, w_gate_up, w_down, gate, up, act)


def ffn_backward(saved, dy, name, exchange=None):
    hn, w_gate_up, w_down, gate, up, act = saved
    s, k = hn.shape
    n_blocks, _, width = w_gate_up.shape
    half = n_blocks // 2
    n = w_down.shape[1]
    tm = 1024
    blocked = jax.ShapeDtypeStruct((half, s, width), bf16)
    carried = len(exchange.operands) if exchange else 0
    steps = (s // tm, half)

    def dact_body(dy_ref, w_ref, gate_ref, up_ref, *refs):
        x_refs, refs = refs[:carried], refs[carried:]
        (dgate_ref, dup_ref), refs = refs[:2], refs[2:]
        if exchange:
            at = pl.program_id(0) * steps[1] + pl.program_id(1)
            start, wait = exchange.bind(x_refs, refs[:carried], refs[carried:])
            pl.when(at == 0)(start)
        d_act = _bdot(dy_ref[...], w_ref[...], "nt")
        g, u = gate_ref[...].astype(f32), up_ref[...].astype(f32)
        sig = jax.nn.sigmoid(g)
        dgate_ref[...] = (d_act * u * sig * (1.0 + g * (1.0 - sig))).astype(bf16)
        dup_ref[...] = (d_act * g * sig).astype(bf16)
        if exchange:
            pl.when(at == steps[0] * steps[1] - 1)(wait)

    tile = pl.BlockSpec((None, tm, width), lambda i, d: (d, i, 0))
    d_gate, d_up, *landed = pl.pallas_call(
        dact_body, name=name + "_dact", grid=steps,
        in_specs=[pl.BlockSpec((tm, n), lambda i, d: (i, 0)), pl.BlockSpec((width, n), lambda i, d: (d, 0)), tile, tile] + [ANY] * carried,
        out_specs=[tile, tile] + [ANY] * carried, out_shape=[blocked, blocked] + (exchange.out_shapes if exchange else []),
        scratch_shapes=exchange.scratch if exchange else [],
        compiler_params=_params(2))(dy, w_down, gate, up, *(exchange.operands if exchange else []))

    def dx_body(dg_ref, du_ref, w_ref, dx_ref):
        dx_ref[...] = sum(_bdot(dg_ref[d], w_ref[d], "nt") + _bdot(du_ref[d], w_ref[d + half], "nt") for d in range(half))

    tx = 512
    rows = pl.BlockSpec((half, tx, width), lambda i: (0, i, 0))
    dx = pl.pallas_call(
        dx_body, name=name + "_dx", grid=(s // tx,), in_specs=[rows, rows, _whole(w_gate_up.shape)],
        out_specs=pl.BlockSpec((tx, k), lambda i: (i, 0)), out_shape=_sds((s, k)), compiler_params=_params(1))(d_gate, d_up, w_gate_up)

    def dw1_body(x_ref, dg_ref, du_ref, dw_ref):
        d_block = jnp.where(pl.program_id(0) < half, dg_ref[...], du_ref[...])
        dw_ref[...] = _bdot(x_ref[...], d_block, "tn").astype(bf16)

    d_w_gate_up = pl.pallas_call(
        dw1_body, name=name + "_dw1", grid=(n_blocks,),
        in_specs=[_whole((s, k)), pl.BlockSpec((None, s, width), lambda b: (jnp.minimum(b, half - 1), 0, 0)),
                  pl.BlockSpec((None, s, width), lambda b: (jnp.maximum(b - half, 0), 0, 0))],
        out_specs=pl.BlockSpec((None, k, width), lambda b: (b, 0, 0)), out_shape=jax.ShapeDtypeStruct(w_gate_up.shape, bf16),
        compiler_params=_params(1))(hn, d_gate, d_up)

    tn = 512

    def dw2_body(act_ref, dy_ref, dw_ref):
        dw_ref[...] = _bdot(act_ref[...], dy_ref[...], "tn").astype(bf16)

    d_w_down = pl.pallas_call(
        dw2_body, name=name + "_dw2", grid=(half, n // tn),
        in_specs=[pl.BlockSpec((None, s, width), lambda d, j: (d, 0, 0)), pl.BlockSpec((s, tn), lambda d, j: (0, j))],
        out_specs=pl.BlockSpec((width, tn), lambda d, j: (d, j)), out_shape=jax.ShapeDtypeStruct(w_down.shape, bf16),
        compiler_params=_params(2))(act, dy)
    return dx, d_w_gate_up, d_w_down, landed


def _joined_linear(parts, w, name):
    s, n = parts[0].shape[0], w.shape[1]
    widths = [p.shape[1] for p in parts]
    edges = [sum(widths[:i]) for i in range(len(widths) + 1)]
    spans = list(zip(edges[:-1], edges[1:]))
    tm, tn = 1024, 512

    def forward(*args):
        *xs, w = args

        def y_body(*refs):
            *x_refs, w_ref, y_ref = refs
            y_ref[...] = sum(_bdot(x_ref[...], w_ref[a:b, :], "nn") for x_ref, (a, b) in zip(x_refs, spans))

        y = pl.pallas_call(
            y_body, name=name + "_y", grid=(s // tm, n // tn),
            in_specs=[pl.BlockSpec((tm, k), lambda i, j: (i, 0)) for k in widths] + [pl.BlockSpec((edges[-1], tn), lambda i, j: (0, j))],
            out_specs=pl.BlockSpec((tm, tn), lambda i, j: (i, j)), out_shape=_sds((s, n)), compiler_params=_params(2))(*xs, w)
        return y, args

    def backward(args, dy):
        *xs, w = args

        def dx_body(dy_ref, w_ref, *dx_refs):
            d_all = _bdot(dy_ref[...], w_ref[...], "nt")
            for dx_ref, (a, b) in zip(dx_refs, spans):
                dx_ref[...] = d_all[:, a:b]

        dxs = pl.pallas_call(
            dx_body, name=name + "_dx", grid=(s // tm,), in_specs=[pl.BlockSpec((tm, n), lambda i: (i, 0)), _whole(w.shape)],
            out_specs=[pl.BlockSpec((tm, k), lambda i: (i, 0)) for k in widths], out_shape=[_sds((s, k)) for k in widths],
            compiler_params=_params(1))(dy, w)

        def dw_body(*refs):
            *x_refs, dy_ref, dw_ref = refs
            dy_tile = dy_ref[...]
            dw_ref[...] = jnp.concatenate([_bdot(x_ref[...], dy_tile, "tn") for x_ref in x_refs], axis=0).astype(bf16)

        dw = pl.pallas_call(
            dw_body, name=name + "_dw", grid=(n // tn,),
            in_specs=[_whole((s, k)) for k in widths] + [pl.BlockSpec((s, tn), lambda j: (0, j))],
            out_specs=pl.BlockSpec((edges[-1], tn), lambda j: (0, j)), out_shape=jax.ShapeDtypeStruct(w.shape, bf16),
            compiler_params=_params(1))(*xs, dy)
        return (*dxs, dw)

    @jax.custom_vjp
    def op(*args):
        return forward(*args)[0]

    op.defvjp(forward, backward)
    return op(*parts, w)


def _split_cols(x, widths):
    edges = [sum(widths[:i]) for i in range(len(widths) + 1)]

    def cut(x):
        return tuple(x[:, a:b] for a, b in zip(edges[:-1], edges[1:]))

    @jax.custom_vjp
    def split(x):
        return cut(x)

    split.defvjp(lambda x: (cut(x), None), lambda _, cts: (jnp.concatenate(cts, axis=1),))
    return split(x)


def _block_op(name, f, grid, in_specs, out_defs, arrays, diff, acc=None, gdefs=None):
    acc, gdefs = acc or {}, gdefs or {}
    n_in, n_out, n_grid = len(in_specs), len(out_defs), len(grid)

    def fwd_call(*xs):
        def body(*refs):
            outs = f(*[r[...] for r in refs[:n_in]])
            for r, o in zip(refs[n_in:], outs):
                r[...] = o.astype(r.dtype)

        return pl.pallas_call(
            body, name=name + "_fwd", grid=grid, in_specs=in_specs, out_specs=[d[1] for d in out_defs],
            out_shape=[d[0] for d in out_defs], compiler_params=_params(n_grid))(*xs)

    def bwd_call(*xs_and_cts):
        def body(*refs):
            xs = [r[...] for r in refs[:n_in]]
            cts = tuple(r[...] for r in refs[n_in:n_in + n_out])

            def of_diff(*dx):
                full = list(xs)
                for i, v in zip(diff, dx):
                    full[i] = v
                return tuple(f(*full))

            _, vjp = jax.vjp(of_diff, *[xs[i] for i in diff])
            grads = vjp(cts)
            for i, g, r in zip(diff, grads, refs[n_in + n_out:]):
                if i in acc:
                    first = functools.reduce(jnp.logical_and, [pl.program_id(a) == 0 for a in acc[i]])

                    @pl.when(first)
                    def _(r=r):
                        r[...] = jnp.zeros_like(r)

                    r[...] += g.astype(r.dtype)
                else:
                    r[...] = g.astype(r.dtype)

        g_defs = [gdefs.get(i, (jax.ShapeDtypeStruct(arrays[i].shape, f32), in_specs[i])) for i in diff]
        return pl.pallas_call(
            body, name=name + "_bwd", grid=grid, in_specs=list(in_specs) + [d[1] for d in out_defs],
            out_specs=[d[1] for d in g_defs], out_shape=[d[0] for d in g_defs], compiler_params=_params(n_grid))(*xs_and_cts)

    return fwd_call, bwd_call


def _simple_op(name, f, grid, in_specs, out_defs, arrays, diff, acc=None):
    fwd_call, bwd_call = _block_op(name, f, grid, in_specs, out_defs, arrays, diff, acc)

    @jax.custom_vjp
    def op(*xs):
        return tuple(fwd_call(*xs))

    def op_f(*xs):
        return tuple(fwd_call(*xs)), xs

    def op_b(xs, cts):
        grads = bwd_call(*xs, *cts)
        out = [jnp.zeros_like(x) for x in xs]
        for i, g in zip(diff, grads):
            out[i] = g
        return tuple(out)

    op.defvjp(op_f, op_b)
    return op(*arrays)


def _rows(width, tile=ROW_TILE):
    return pl.BlockSpec((tile, width), lambda i: (i, 0))


def _whole(shape):
    return pl.BlockSpec(shape, lambda *_: (0,) * len(shape))


def _sds(shape):
    return jax.ShapeDtypeStruct(shape, f32)


def _rms(x, w):
    return x * lax.rsqrt(jnp.mean(x * x, axis=-1, keepdims=True) + EPS) * w


def rms_norm(x, w, name):
    r, d = x.shape
    tile = min(ROW_TILE, r)
    return _simple_op(name, lambda x, w: (_rms(x, w),), (r // tile,), [_rows(d, tile), _whole((1, d))],
                      [(_sds((r, d)), _rows(d, tile))], (x, w), (0, 1), {1: (0,)})[0]


def add_norm(h, y, w, name):
    r, d = h.shape
    return _simple_op(name, lambda h, y, w: (h + _rms(y, w),), (r // ROW_TILE,), [_rows(d), _rows(d), _whole((1, d))],
                      [(_sds((r, d)), _rows(d))], (h, y, w), (0, 1, 2), {2: (0,)})[0]


def add_norm_then_norm(h, y, w_post, w_pre, name):
    r, d = h.shape

    def f(h, y, w_post, w_pre):
        h_new = h + _rms(y, w_post)
        return h_new, _rms(h_new, w_pre)

    return _simple_op(name, f, (r // ROW_TILE,), [_rows(d), _rows(d), _whole((1, d)), _whole((1, d))],
                      [(_sds((r, d)), _rows(d))] * 2, (h, y, w_post, w_pre), (0, 1, 2, 3), {2: (0,), 3: (0,)})


def _swap8(x):
    def raw(x):
        lane = lax.broadcasted_iota(jnp.int32, x.shape, 1) % ATTN_HEAD_DIM
        half = ROPE_DIM // 2
        up = pltpu.roll(x, x.shape[1] - half, axis=1)
        down = pltpu.roll(x, half, axis=1)
        return jnp.where(lane < half, up, jnp.where(lane < ROPE_DIM, down, 0.0))

    @jax.custom_vjp
    def swap(x):
        return raw(x)

    swap.defvjp(lambda x: (raw(x), None), lambda _, g: (raw(g),))
    return swap(x)


def rope(x, cos_t, sin_t, scale, name):
    r, d = x.shape
    return _simple_op(name, lambda x, c, s: ((x * c + _swap8(x) * s) * scale,), (r // ROW_TILE,), [_rows(d)] * 3,
                      [(_sds((r, d)), _rows(d))], (x, cos_t, sin_t), (0,))[0]


def _shift_rows(x, k):
    n = x.shape[0]

    def down(x):
        row = lax.broadcasted_iota(jnp.int32, x.shape, 0)
        return jnp.where(row >= k, pltpu.roll(x, k, axis=0), 0.0)

    def up(x):
        row = lax.broadcasted_iota(jnp.int32, x.shape, 0)
        return jnp.where(row < n - k, pltpu.roll(x, n - k, axis=0), 0.0)

    @jax.custom_vjp
    def shift(x):
        return down(x)

    shift.defvjp(lambda x: (down(x), None), lambda _, g: (up(g),))
    return shift(x)


def _causal_conv(x, w):
    taps = w.shape[0]
    y = x * w[taps - 1:taps, :]
    for j in range(taps - 1):
        y = y + _shift_rows(x, taps - 1 - j) * w[j:j + 1, :]
    return y


def _cols(rows, at=0):
    return pl.BlockSpec((rows, LANES), lambda j: (0, at + j))


def short_conv(cb, cc, cx, w, name):
    s, c = cb.shape
    taps = w.shape[0]
    return _simple_op(name, lambda b, c_, x, w: (b * _causal_conv(c_ * x, w),), (c // LANES,),
                      [_cols(s)] * 3 + [_cols(taps)], [(_sds((s, c)), _cols(s))], (cb, cc, cx, w), (0, 1, 2, 3))[0]


def gdn_pre(qkv, w, name):
    s, c = qkv.shape
    taps = w.shape[0]

    def f(x, w):
        j = pl.program_id(0)
        y = jax.nn.silu(_causal_conv(x, w))
        normed = y * lax.rsqrt(jnp.sum(y * y, axis=-1, keepdims=True) + EPS)
        scale = jnp.where(j < GDN_HEADS, GDN_HEAD_DIM ** -0.5, 1.0).astype(f32)
        return (jnp.where(j < 2 * GDN_HEADS, normed * scale, y),)

    return _simple_op(name, f, (c // LANES,), [_cols(s), _cols(taps)], [(_sds((s, c)), _cols(s))], (qkv, w), (0, 1))[0]


def gate_beta(ab, pv, name):
    s = ab.shape[0]

    def f(ab, pv):
        lane = lax.broadcasted_iota(jnp.int32, ab.shape, 1)
        g = -jnp.exp(pv[0:1, :]) * jax.nn.softplus(ab + pv[1:2, :])
        return (jnp.where(lane < GDN_HEADS, g, jnp.where(lane < 2 * GDN_HEADS, jax.nn.sigmoid(ab), 0.0)),)

    return _simple_op(name, f, (s // ROW_TILE,), [_rows(LANES), _whole((8, LANES))], [(_sds((s, LANES)), _rows(LANES))],
                      (ab, pv), (0, 1), {1: (0,)})[0]


def gdn_post(o, gate, w, name):
    s, c = o.shape

    def f(o, g, w):
        heads = [slice(hd * LANES, (hd + 1) * LANES) for hd in range(c // LANES)]
        return (jnp.concatenate([_rms(o[:, hd], w) * jax.nn.silu(g[:, hd]) for hd in heads], axis=1),)

    return _simple_op(name, f, (s // ROW_TILE,), [_rows(c), _rows(c), _whole((1, LANES))], [(_sds((s, c)), _rows(c))],
                      (o, gate, w), (0, 1, 2), {2: (0,)})[0]


def attn_merge(outs, lses, name):
    s, c = outs[0].shape

    def f(o1, o2, o3, l1, l2, l3):
        m = lax.stop_gradient(jnp.maximum(jnp.maximum(l1, l2), l3))
        e1, e2, e3 = jnp.exp(l1 - m), jnp.exp(l2 - m), jnp.exp(l3 - m)
        return ((e1 * o1 + e2 * o2 + e3 * o3) / (e1 + e2 + e3),)

    return _simple_op(name, f, (s // ROW_TILE,), [_rows(c)] * 6, [(_sds((s, c)), _rows(c))], (*outs, *lses), tuple(range(6)))[0]


def loss_rows(y, target, name):
    s, d = y.shape
    nt = s // ROW_TILE

    def f(y, t):
        e = y - t
        part = 0.5 * jnp.sum(jnp.mean(e * e, axis=-1, keepdims=True), axis=0, keepdims=True)
        return (jnp.broadcast_to(part * (1.0 / (8 * LANES)), (8, LANES)),)

    out = _simple_op(name, f, (nt,), [_rows(d)] * 2, [(_sds((nt * 8, LANES)), pl.BlockSpec((8, LANES), lambda i: (i, 0)))],
                     (y, target), (0,))[0]
    return jnp.sum(out)


def _mxu(a, b, form):
    dims = {"nn": ((1,), (0,)), "nt": ((1,), (1,)), "tn": ((0,), (0,))}

    def raw(a, b, form):
        return lax.dot_general(a.astype(bf16), b.astype(bf16), (dims[form], ((), ())), preferred_element_type=f32)

    @jax.custom_vjp
    def prod(a, b):
        return raw(a, b, form)

    def prod_b(res, ct):
        a, b = res
        if form == "nn":
            return raw(ct, b, "nt"), raw(a, ct, "tn")
        if form == "nt":
            return raw(ct, b, "nn"), raw(ct, a, "tn")
        return raw(b, ct, "nt"), raw(a, ct, "nn")

    prod.defvjp(lambda a, b: (raw(a, b, form), (a, b)), prod_b)
    return prod(a, b)


def _masked_heads_attention(q, keys, values, seen):
    dh = ATTN_HEAD_DIM
    outs, lses = [], []
    for hd in range(q.shape[1] // dh):
        at = slice(hd * dh, (hd + 1) * dh)
        sc = jnp.where(seen, _mxu(q[:, at], keys[:, at], "nt"), -jnp.inf)
        m = lax.stop_gradient(jnp.max(sc, axis=-1, keepdims=True))
        p = jnp.exp(sc - m)
        l = jnp.sum(p, axis=-1, keepdims=True)
        outs.append(_mxu(p / l, values[:, at], "nn"))
        lses.append(jnp.broadcast_to(m + jnp.log(l), (q.shape[0], dh)))
    return jnp.concatenate(outs, axis=1), jnp.concatenate(lses, axis=1)


def band_attention(q, k, v, nb, name):
    r, qb, width = q.shape

    def f(q, kp, kc, vp, vc):
        has_prev = (pl.program_id(0) % nb) > 0
        i = lax.broadcasted_iota(jnp.int32, (qb, 2 * qb), 0)
        j = lax.broadcasted_iota(jnp.int32, (qb, 2 * qb), 1)
        seen = jnp.logical_or(jnp.logical_and(jnp.logical_and(j < qb, j >= i), has_prev), jnp.logical_and(j >= qb, j - qb <= i))
        return _masked_heads_attention(q, jnp.concatenate([kp, kc], axis=0), jnp.concatenate([vp, vc], axis=0), seen)

    blk = (None, qb, width)
    cur = pl.BlockSpec(blk, lambda b: (b, 0, 0))
    prev = pl.BlockSpec(blk, lambda b: (jnp.maximum(b - 1, 0), 0, 0))
    shape = _sds((r, qb, width))
    fwd_call, bwd_call = _block_op(name, f, (r,), [cur, prev, cur, prev, cur], [(shape, cur), (shape, cur)],
                                   (q, k, k, v, v), (0, 1, 2, 3, 4), gdefs={1: (shape, cur), 3: (shape, cur)})

    def to_prev(g):
        return jnp.concatenate([g[1:], jnp.zeros_like(g[:1])], axis=0)

    @jax.custom_vjp
    def op(q, k, v):
        return tuple(fwd_call(q, k, k, v, v))

    def op_b(res, cts):
        q, k, v = res
        dq, dkp, dkc, dvp, dvc = bwd_call(q, k, k, v, v, *cts)
        return dq, dkc + to_prev(dkp), dvc + to_prev(dvp)

    op.defvjp(lambda q, k, v: (tuple(fwd_call(q, k, k, v, v)), (q, k, v)), op_b)
    return op(q, k, v)


def dilated_attention(q, k, v, name):
    s = q.shape[0]
    outs, lses = [], []
    for d in DILATIONS:
        length = s // d
        nb = length // QB
        def to_residue(t):
            return t.reshape(length, d, ATTN_WIDTH).transpose(1, 0, 2).reshape(d * nb, QB, ATTN_WIDTH)

        def from_residue(t):
            return t.reshape(d, length, ATTN_WIDTH).transpose(1, 0, 2).reshape(s, ATTN_WIDTH)

        o, lse = band_attention(to_residue(q), to_residue(k), to_residue(v), nb, f"{name}_d{d}")
        outs.append(from_residue(o))
        lses.append(from_residue(lse))
    return attn_merge(outs, lses, name + "_merge")


def cross_attention(q, kv, name):
    s = q.shape[0]
    m = kv.shape[0]
    width = XATTN_HEADS * XATTN_HEAD_DIM
    tq = 512

    def f(q, k, v):
        sc = _mxu(q, k, "nt") * (XATTN_HEAD_DIM ** -0.5)
        mx = lax.stop_gradient(jnp.max(sc, axis=-1, keepdims=True))
        p = jnp.exp(sc - mx)
        return (_mxu(p / jnp.sum(p, axis=-1, keepdims=True), v, "nn"),)

    q_spec = pl.BlockSpec((tq, XATTN_HEAD_DIM), lambda a, i: (i, a))
    k_spec = pl.BlockSpec((m, XATTN_HEAD_DIM), lambda a, i: (0, a))
    v_spec = pl.BlockSpec((m, XATTN_HEAD_DIM), lambda a, i: (0, a + XATTN_HEADS))
    half = _sds((m, width))
    fwd_call, bwd_call = _block_op(name, f, (XATTN_HEADS, s // tq), [q_spec, k_spec, v_spec], [(_sds((s, width)), q_spec)],
                                   (q, kv, kv), (0, 1, 2), acc={1: (1,), 2: (1,)}, gdefs={1: (half, k_spec), 2: (half, k_spec)})

    @jax.custom_vjp
    def op(q, kv):
        return fwd_call(q, kv, kv)[0]

    def op_b(res, ct):
        q, kv = res
        dq, dk, dv = bwd_call(q, kv, kv, ct)
        return dq, jnp.concatenate([dk, dv], axis=1)

    op.defvjp(lambda q, kv: (fwd_call(q, kv, kv)[0], (q, kv)), op_b)
    return op(q, kv)


def _hi(a, b, form="nn"):
    dims = {"nn": ((1,), (0,)), "nt": ((1,), (1,)), "tn": ((0,), (0,))}[form]
    return lax.dot_general(a, b, (dims, ((), ())), precision=lax.Precision.HIGH, preferred_element_type=f32)


def _running_sum(g):
    def raw(x, form):
        c = x.shape[0]
        tri = (lax.broadcasted_iota(jnp.int32, (c, c), 0) >= lax.broadcasted_iota(jnp.int32, (c, c), 1)).astype(bf16)
        hi = x.astype(bf16)
        rest = x - hi.astype(f32)
        mid = rest.astype(bf16)
        low = (rest - mid.astype(f32)).astype(bf16)
        dims = (((1,) if form == "nn" else (0,), (0,)), ((), ()))
        return sum(lax.dot_general(tri, part, dims, preferred_element_type=f32) for part in (hi, mid, low))

    @jax.custom_vjp
    def run(x):
        return raw(x, "nn")

    run.defvjp(lambda x: (raw(x, "nn"), None), lambda _, ct: (raw(ct, "tn"),))
    return run(g)


def _unit_lower_inverse(a):
    c = a.shape[0]
    eye = (lax.broadcasted_iota(jnp.int32, (c, c), 0) == lax.broadcasted_iota(jnp.int32, (c, c), 1)).astype(f32)
    inv, power = eye - a, -a
    for _ in range(c.bit_length() - 2):
        power = _hi(power, power)
        inv = inv + _hi(inv, power)
    return inv


def _known_inverse(a, t):
    @jax.custom_vjp
    def inv(a, t):
        return t

    def inv_b(t, ct):
        return -_hi(_hi(t, ct, "tn"), t, "nt"), jnp.zeros_like(t)

    inv.defvjp(lambda a, t: (t, t), inv_b)
    return inv(a, t)


def _delta_chunk(q, k, v, g, beta, s0, known_inv=None):
    c = q.shape[0]
    i = lax.broadcasted_iota(jnp.int32, (c, c), 0)
    j = lax.broadcasted_iota(jnp.int32, (c, c), 1)
    causal, strict = i >= j, i > j
    dec = _running_sum(g)
    dec_i = dec[:, :c]
    rel = jnp.exp(jnp.where(causal, dec_i - dec_i.T, -jnp.inf))
    k_beta = k * beta
    on_k = _mxu(jnp.concatenate([k_beta, q], axis=0), k, "nt")
    a = jnp.where(strict, on_k[:c] * rel, 0.0)
    attn = jnp.where(causal, on_k[c:] * rel, 0.0)
    inv = _unit_lower_inverse(a) if known_inv is None else _known_inverse(a, known_inv)
    e_dec = jnp.exp(dec)
    solved = _hi(inv, jnp.concatenate([v * beta, k_beta * e_dec], axis=1))
    u, w = solved[:, :v.shape[1]], solved[:, v.shape[1]:]
    total = jnp.sum(g, axis=0, keepdims=True)
    on_state = _mxu(jnp.concatenate([w, q * e_dec], axis=0), s0, "nn")
    v_new = u - on_state[:c]
    o = on_state[c:] + _mxu(attn, v_new, "nn")
    s1 = s0 * jnp.exp(total) + _mxu(k * jnp.exp(total - dec), v_new, "tn")
    return o, s1, inv


def _delta_rule_call(name, walk, n, in_specs, out_specs, out_shape, operands, exchange):
    n_in, n_out = len(in_specs), len(out_specs)
    carried = len(exchange.operands) if exchange else 0

    def body(*refs):
        ins, refs = refs[:n_in], refs[n_in:]
        x_refs, refs = refs[:carried], refs[carried:]
        outs, refs = refs[:n_out], refs[n_out:]
        land_refs, (state, *sems) = refs[:carried], refs[carried:]
        step = pl.program_id(0)
        if exchange:
            start, finish = exchange.bind(x_refs, land_refs, sems)
            pl.when(step == 0)(start)

        @pl.when(step == 0)
        def _():
            state[...] = jnp.zeros_like(state)

        walk(ins, outs, state)
        if exchange:
            pl.when(step == n - 1)(finish)

    return pl.pallas_call(
        body, name=name, grid=(n,), in_specs=list(in_specs) + [ANY] * carried, out_specs=list(out_specs) + [ANY] * carried,
        out_shape=list(out_shape) + (exchange.out_shapes if exchange else []),
        scratch_shapes=[pltpu.VMEM((GDN_HEAD_DIM, GDN_WIDTH), f32)] + (exchange.scratch if exchange else []),
        compiler_params=_params(1))(*operands, *(exchange.operands if exchange else []))


def _delta_heads():
    heads = [slice(hd * GDN_HEAD_DIM, (hd + 1) * GDN_HEAD_DIM) for hd in range(GDN_HEADS)]
    inv_at = [slice(hd * GDN_CHUNK, (hd + 1) * GDN_CHUNK) for hd in range(GDN_HEADS)]
    return heads, inv_at


def _head_chunk(q, k, v, gates, s0, head, known_inv=None):
    g = jnp.broadcast_to(gates[:, head:head + 1], q.shape)
    beta = jnp.broadcast_to(gates[:, GDN_HEADS + head:GDN_HEADS + head + 1], q.shape)
    return _delta_chunk(q, k, v, g, beta, s0, known_inv)


def delta_rule_fwd(q, k, v, gates, name, exchange=None):
    s, width = q.shape
    c, dk = GDN_CHUNK, GDN_HEAD_DIM
    n = s // c
    heads, inv_at = _delta_heads()

    def walk(ins, outs, state):
        q_ref, k_ref, v_ref, gates_ref = ins
        o_ref, s_in_ref, inv_ref = outs
        s_in_ref[...] = state[...]
        gates = gates_ref[...]
        xs = [[r[:, hd] for r in (q_ref, k_ref, v_ref)] + [gates, state[:, hd], i] for i, hd in enumerate(heads)]
        ys = [_head_chunk(*x) for x in xs]
        for hd, at, (o, s1, inv) in zip(heads, inv_at, ys):
            o_ref[:, hd], state[:, hd], inv_ref[:, at] = o, s1, inv

    blk = pl.BlockSpec((c, width), lambda t: (t, 0))
    gt = pl.BlockSpec((c, LANES), lambda t: (t, 0))
    st = pl.BlockSpec((dk, width), lambda t: (t, 0))
    iv = pl.BlockSpec((c, GDN_HEADS * c), lambda t: (t, 0))
    return _delta_rule_call(name, walk, n, [blk] * 3 + [gt], [blk, st, iv],
                            [_sds((s, width)), _sds((n * dk, width)), _sds((s, GDN_HEADS * c))], (q, k, v, gates), exchange)


def delta_rule_bwd(q, k, v, gates, s_in, inv, do, name, exchange=None):
    s, width = q.shape
    c, dk = GDN_CHUNK, GDN_HEAD_DIM
    n = s // c
    heads, inv_at = _delta_heads()

    def walk(ins, outs, dstate):
        q_ref, k_ref, v_ref, gates_ref, s_ref, inv_ref, do_ref = ins
        dq_ref, dk_ref, dv_ref, dgates_ref = outs
        gates = gates_ref[...]
        xs = [[r[:, hd] for r in (q_ref, k_ref, v_ref)] + [gates, s_ref[:, hd]] for hd in heads]
        known = [inv_ref[:, at] for at in inv_at]
        cts = [(do_ref[:, hd], dstate[:, hd]) for hd in heads]
        grads = []
        for i, (x, t, ct) in enumerate(zip(xs, known, cts)):
            _, vjp = jax.vjp(lambda *y, t=t, i=i: _head_chunk(*y, i, known_inv=t)[:2], *x)
            grads.append(vjp(ct))
        dgates = grads[0][3]
        for g in grads[1:]:
            dgates = dgates + g[3]
        dgates_ref[...] = dgates
        for hd, (dq, dk_, dv, _, ds0) in zip(heads, grads):
            dq_ref[:, hd], dk_ref[:, hd], dv_ref[:, hd], dstate[:, hd] = dq, dk_, dv, ds0

    blk = pl.BlockSpec((c, width), lambda t: (n - 1 - t, 0))
    gt = pl.BlockSpec((c, LANES), lambda t: (n - 1 - t, 0))
    st = pl.BlockSpec((dk, width), lambda t: (n - 1 - t, 0))
    iv = pl.BlockSpec((c, GDN_HEADS * c), lambda t: (n - 1 - t, 0))
    return _delta_rule_call(name, walk, n, [blk] * 3 + [gt, st, iv, blk], [blk] * 3 + [gt],
                            [_sds((s, width))] * 3 + [_sds((s, LANES))], (q, k, v, gates, s_in, inv, do), exchange)


def adamw(w, g, m, v, name):
    shape = w.shape
    if len(shape) == 2:
        grid, spec = (1,), pl.BlockSpec(shape, lambda i: (0, 0))
    else:
        tile = shape[1] if shape[1] <= 512 else _pick(shape[1], (512, 256, 128))
        grid, spec = (shape[0], shape[1] // tile), pl.BlockSpec((None, tile, shape[2]), lambda layer, i: (layer, i, 0))

    def body(w_ref, g_ref, m_ref, v_ref, d_ref, nm_ref, nv_ref):
        grad = g_ref[...]
        nm = ADAM_B1 * m_ref[...] + (1.0 - ADAM_B1) * grad
        nv = ADAM_B2 * v_ref[...] + (1.0 - ADAM_B2) * (grad * grad)
        m_hat = nm / (1.0 - ADAM_B1 ** ADAM_STEP)
        v_hat = nv / (1.0 - ADAM_B2 ** ADAM_STEP)
        d_ref[...] = -ADAM_LR * (m_hat / (jnp.sqrt(v_hat) + ADAM_EPS) + ADAM_WD * w_ref[...])
        nm_ref[...] = nm
        nv_ref[...] = nv

    return tuple(pl.pallas_call(body, name=name, grid=grid, in_specs=[spec] * 4, out_specs=[spec] * 3,
                                out_shape=[_sds(shape)] * 3, compiler_params=_params(len(grid)))(w, g, m, v))


def _place():
    return lax.axis_index("x"), lax.axis_index("y"), lax.axis_index("c")


def _flip(p, bits):
    return tuple(1 - v if (bits >> s) & 1 else v for v, s in zip(p, (2, 1, 0)))


def _slot(p):
    return 4 * p[0] + 2 * p[1] + p[2]


def _chip_of(p):
    return 2 * p[0] + p[1]


ANY = pl.BlockSpec(memory_space=pl.ANY)


class Gather:
    scratch = (pltpu.SemaphoreType.DMA((7,)), pltpu.SemaphoreType.DMA((7,)), pltpu.SemaphoreType.DMA)

    def __init__(self, shard):
        self.operand = shard
        self.out_shape = jax.ShapeDtypeStruct((N_DEV,) + shard.shape, shard.dtype)

    def bind(self, x_ref, out_ref, send_sems, recv_sems, local_sem):
        me = _place()
        sibling = _flip(me, 1)
        chips = [_flip(me, 4), _flip(me, 2), _flip(me, 6)]

        def copy(k, block, to, src=None):
            return pltpu.make_async_remote_copy(
                src_ref=out_ref.at[_slot(block)] if src is None else src, dst_ref=out_ref.at[_slot(block)],
                send_sem=send_sems.at[k], recv_sem=recv_sems.at[k], device_id=to, device_id_type=MESH)

        mine = pltpu.make_async_copy(x_ref, out_ref.at[_slot(me)], local_sem)
        first = [copy(0, me, sibling, src=x_ref)] + [copy(1 + j, me, chip, src=x_ref) for j, chip in enumerate(chips)]
        passed = [copy(4 + j, chip, sibling) for j, chip in enumerate(chips)]

        def start():
            mine.start()
            for cp in first:
                cp.start()

        def finish():
            for j, chip in enumerate(chips):
                copy(1 + j, chip, me).wait_recv()
                passed[j].start()
            copy(0, sibling, me).wait_recv()
            for j, chip in enumerate(chips):
                copy(4 + j, _flip(chip, 1), me).wait_recv()
            for cp in first + passed:
                cp.wait_send()
            mine.wait()

        return start, finish


class ChipExchange:
    scratch = (pltpu.SemaphoreType.DMA((3,)), pltpu.SemaphoreType.DMA((3,)), pltpu.SemaphoreType.DMA)

    def __init__(self, blocks):
        self.operand = blocks
        self.out_shape = jax.ShapeDtypeStruct(blocks.shape, blocks.dtype)

    def bind(self, x_ref, out_ref, send_sems, recv_sems, local_sem):
        me = _place()
        peers = [_flip(me, 4), _flip(me, 2), _flip(me, 6)]
        mine = pltpu.make_async_copy(x_ref.at[_chip_of(me)], out_ref.at[_chip_of(me)], local_sem)

        def copy(j, src_chip, dst_chip):
            return pltpu.make_async_remote_copy(
                src_ref=x_ref.at[src_chip], dst_ref=out_ref.at[dst_chip], send_sem=send_sems.at[j],
                recv_sem=recv_sems.at[j], device_id=peers[j], device_id_type=MESH)

        sends = [copy(j, _chip_of(peer), _chip_of(me)) for j, peer in enumerate(peers)]

        def start():
            mine.start()
            for cp in sends:
                cp.start()

        def finish():
            for j, peer in enumerate(peers):
                copy(j, _chip_of(me), _chip_of(peer)).wait_recv()
            for cp in sends:
                cp.wait_send()
            mine.wait()

        return start, finish


class Together:
    def __init__(self, *parts):
        self.parts = parts
        self.operands = [p.operand for p in parts]
        self.out_shapes = [p.out_shape for p in parts]
        self.scratch = [s for p in parts for s in p.scratch]

    def bind(self, x_refs, out_refs, sems):
        bound, at = [], 0
        for p, x_ref, out_ref in zip(self.parts, x_refs, out_refs):
            bound.append(p.bind(x_ref, out_ref, *sems[at:at + len(p.scratch)]))
            at += len(p.scratch)

        def start():
            for s, _ in bound:
                s()

        def finish():
            for _, f in bound:
                f()

        return start, finish


def exchange_alone(exchange, name):
    n = len(exchange.operands)

    def body(*refs):
        start, finish = exchange.bind(refs[:n], refs[n:2 * n], refs[2 * n:])
        start()
        finish()

    return pl.pallas_call(body, name=name, out_shape=exchange.out_shapes, in_specs=[ANY] * n, out_specs=[ANY] * n,
                          scratch_shapes=exchange.scratch)(*exchange.operands)


def _row_tile(rows):
    return max([t for t in range(16, min(rows, 1024) + 1, 16) if rows % t == 0] or [rows])


def pair_exchange(blocks, name):
    n = len(blocks)

    def body(*refs):
        x_refs, theirs_refs, (send_sems, recv_sems) = refs[:n], refs[n:2 * n], refs[2 * n:]
        me = _place()
        remote = [pltpu.make_async_remote_copy(
            src_ref=x_refs[t].at[2 * q + 1 - me[2]], dst_ref=theirs_refs[t].at[q], send_sem=send_sems.at[4 * t + q],
            recv_sem=recv_sems.at[4 * t + q], device_id=_flip(me, 1), device_id_type=MESH) for t in range(n) for q in range(4)]
        for cp in remote:
            cp.start()
        for cp in remote:
            cp.wait()

    return pl.pallas_call(
        body, name=name, out_shape=[jax.ShapeDtypeStruct((4,) + b.shape[1:], b.dtype) for b in blocks], in_specs=[ANY] * n,
        out_specs=[ANY] * n, scratch_shapes=[pltpu.SemaphoreType.DMA((4 * n,)), pltpu.SemaphoreType.DMA((4 * n,))])(*blocks)


def pair_add(blocks, theirs, name):
    n, rows, width = theirs.shape
    tile = _row_tile(rows)
    spec = pl.BlockSpec((None, tile, width), lambda q, i: (q, i, 0))
    south = pl.BlockSpec((None, None, tile, width), lambda q, i: (q, 0, i, 0))
    north = pl.BlockSpec((None, None, tile, width), lambda q, i: (q, 1, i, 0))

    def body(s_ref, n_ref, b_ref, o_ref):
        mine = jnp.where(lax.axis_index("c") == 0, s_ref[...], n_ref[...])
        o_ref[...] = (mine.astype(f32) + b_ref[...].astype(f32)).astype(o_ref.dtype)

    by_core = blocks.reshape(n, 2, rows, width)
    return pl.pallas_call(body, name=name, grid=(n, rows // tile), in_specs=[south, north, spec], out_specs=spec,
                          out_shape=jax.ShapeDtypeStruct(theirs.shape, theirs.dtype), compiler_params=_params(2))(by_core, by_core, theirs)


def sum_slots(blocks, name):
    n, rows, width = blocks.shape
    tile = _row_tile(rows)

    def body(x_ref, o_ref):
        total = x_ref[0].astype(f32)
        for s in range(1, n):
            total = total + x_ref[s].astype(f32)
        o_ref[...] = total

    return pl.pallas_call(
        body, name=name, grid=(rows // tile,), in_specs=[pl.BlockSpec((n, tile, width), lambda i: (0, i, 0))],
        out_specs=pl.BlockSpec((tile, width), lambda i: (i, 0)), out_shape=_sds((rows, width)), compiler_params=_params(1))(blocks)


def all_reduce_small(x, name):
    rows, width = x.shape

    def body(x_ref, o_ref, land, send_sems, recv_sems):
        me = _place()
        copies = []
        for k in range(1, N_DEV):
            peer = _flip(me, k)
            copies.append(pltpu.make_async_remote_copy(
                src_ref=x_ref, dst_ref=land.at[_slot(me)], send_sem=send_sems.at[k - 1], recv_sem=recv_sems.at[k - 1],
                device_id=peer, device_id_type=MESH))
        for cp in copies:
            cp.start()
        land[_slot(me)] = x_ref[...]
        for k in range(1, N_DEV):
            peer = _flip(me, k)
            pltpu.make_async_remote_copy(
                src_ref=x_ref, dst_ref=land.at[_slot(peer)], send_sem=send_sems.at[k - 1], recv_sem=recv_sems.at[k - 1],
                device_id=peer, device_id_type=MESH).wait_recv()
        total = land[0]
        for s in range(1, N_DEV):
            total = total + land[s]
        o_ref[...] = total
        for cp in copies:
            cp.wait_send()

    return pl.pallas_call(
        body, name=name, out_shape=_sds((rows, width)), in_specs=[pl.BlockSpec(memory_space=pltpu.VMEM)],
        out_specs=pl.BlockSpec(memory_space=pltpu.VMEM),
        scratch_shapes=[pltpu.VMEM((N_DEV, rows, width), f32), pltpu.SemaphoreType.DMA((7,)), pltpu.SemaphoreType.DMA((7,))],
    )(x)


def _pack_big(shards):
    packed = {name: shards[name].astype(bf16) for name in COL_SHARDED}
    packed["rows"] = jnp.concatenate([shards[name].astype(bf16) for name, _ in ROW_SHARDED], axis=1)
    return packed


def _unpack_gathered(gathered):
    full = {}
    for name, part in gathered.items():
        if name == "w_gate_up":
            full[name] = part
        elif name in COL_SHARDED:
            full[name] = part.transpose(1, 0, 2).reshape(D_MODEL, N_DEV * part.shape[2])
        else:
            at = 0
            for weight, rows in ROW_SHARDED:
                full[weight] = part[:, at:at + rows, :].reshape(N_DEV * rows, D_MODEL)
                at += rows
    return full


def _pack_grads(grads, group):
    packed = {}
    for name in group:
        if name == "w_gate_up":
            packed[name] = grads[name]
        elif name == "rows":
            packed[name] = jnp.concatenate([grads[weight].reshape(N_DEV, rows, D_MODEL) for weight, rows in ROW_SHARDED], axis=1)
        else:
            packed[name] = grads[name].reshape(D_MODEL, N_DEV, grads[name].shape[1] // N_DEV).transpose(1, 0, 2)
    return packed


def _unpack_shard(layers):
    out = {name: jnp.stack([layer[name] for layer in layers]) for name in COL_SHARDED}
    rows_pack, at = jnp.stack([layer["rows"] for layer in layers]), 0
    for weight, rows in ROW_SHARDED:
        out[weight] = rows_pack[:, at:at + rows, :]
        at += rows
    return out


def _rows_of(flat_len):
    return -(-flat_len // (8 * D_MODEL)) * 8


def _pack_small(parts):
    flat = jnp.concatenate([p.reshape(-1) for p in parts])
    rows = _rows_of(flat.shape[0])
    flat = jnp.pad(flat, (0, rows * D_MODEL - flat.shape[0]))
    return flat.reshape(rows, D_MODEL)


def _unpack_small(packed, like):
    flat, out, at = packed.reshape(-1), [], 0
    for p in like:
        out.append(flat[at:at + p.size].reshape(p.shape))
        at += p.size
    return out


def _rope_tables(positions):
    inv_freq = jnp.float32(ROPE_THETA) ** (-jnp.arange(0, ROPE_DIM, 2, dtype=f32) / ROPE_DIM)
    ang = positions.astype(f32)[:, None] * inv_freq
    cos, sin = jnp.cos(ang), jnp.sin(ang)
    rest = ATTN_HEAD_DIM - ROPE_DIM
    cos_h = jnp.concatenate([cos, cos, jnp.ones((cos.shape[0], rest), f32)], axis=1)
    sin_h = jnp.concatenate([-sin, sin, jnp.zeros((sin.shape[0], rest), f32)], axis=1)
    return jnp.tile(cos_h, (1, ATTN_HEADS)), jnp.tile(sin_h, (1, ATTN_HEADS))


HEAD_SMALL = ("norm_mix_pre", "conv_short", "conv_gdn", "gdn_a_log", "gdn_dt_bias")


def _layer_head(h, p, cos_t, sin_t):
    hn = rms_norm(h, p["norm_mix_pre"][None], "norm_mix_pre")
    aw, cw, gw = ATTN_WIDTH, CONV_WIDTH, GDN_WIDTH
    aq, ak, av, cb, cc, cx, gqkv, ab, gate = _split_cols(_linear(hn, p["w_in"], "w_in"),
                                                         (aw, aw, aw, cw, cw, cw, 3 * gw, 2 * GDN_HEADS, gw))
    ab = jnp.pad(ab, ((0, 0), (0, LANES - 2 * GDN_HEADS)))
    y_attn = dilated_attention(rope(aq, cos_t, sin_t, ATTN_HEAD_DIM ** -0.5, "rope_q"), rope(ak, cos_t, sin_t, 1.0, "rope_k"),
                               av, "attn")
    y_conv = short_conv(cb, cc, cx, p["conv_short"], "short_conv")
    qkv = gdn_pre(gqkv, p["conv_gdn"], "gdn_pre")
    pv = jnp.zeros((8, LANES), f32).at[0, :GDN_HEADS].set(p["gdn_a_log"]).at[1, :GDN_HEADS].set(p["gdn_dt_bias"])
    return (*_split_cols(qkv, (gw, gw, gw)), gate_beta(ab, pv, "gate_beta")), (gate, y_attn, y_conv)


MID_PARAMS = ("gdn_norm", "w_out", "norm_mix_post", "norm_xattn_pre", "w_xq", "norm_mem", "w_xkv", "w_xo", "norm_xattn_post",
              "norm_ffn_pre")


def _layer_mid(h, o, gate, y_attn, y_conv, p, mem):
    y_gdn = gdn_post(o, gate, p["gdn_norm"][None], "gdn_post")
    mix = _joined_linear((y_attn, y_conv, y_gdn), p["w_out"], "w_out")
    h, hn = add_norm_then_norm(h, mix, p["norm_mix_post"][None], p["norm_xattn_pre"][None], "norm_mix_xattn")
    qx = _linear(hn, p["w_xq"], "w_xq")
    kv = _linear(rms_norm(mem, p["norm_mem"][None], "norm_mem"), p["w_xkv"], "w_xkv")
    xa = _linear(cross_attention(qx, kv, "xattn"), p["w_xo"], "w_xo")
    return add_norm_then_norm(h, xa, p["norm_xattn_post"][None], p["norm_ffn_pre"][None], "norm_xattn_ffn")


def _pair_summed(grads, group, name):
    blocks = _pack_grads(grads, group)
    theirs = pair_exchange([blocks[n] for n in group], name + "_pair_exchange")
    return [pair_add(blocks[n], t, f"{name}_pair_add_{n}") for n, t in zip(group, theirs)]


def _forward_backward(x, packed, small, mem, cos_t, sin_t, target):
    def gathers(group, layer):
        return [Gather(packed[n][layer]) for n in group]

    h = x
    head_gathered = exchange_alone(Together(*gathers(HEAD_GROUP, 0)), "gather_first")
    saved = []
    for layer in range(DEPTH):
        at_layer = {n: t[layer] for n, t in small.items()}
        head_p = {**_unpack_gathered(dict(zip(HEAD_GROUP, head_gathered))), **{n: at_layer[n] for n in HEAD_SMALL}}
        (rule_in, rest), head_vjp = jax.vjp(lambda h, hp: _layer_head(h, hp, cos_t, sin_t), h, head_p)
        carried = gathers(TAIL_GROUP, layer) + (gathers(HEAD_GROUP, layer + 1) if layer + 1 < DEPTH else [])
        o, s_in, inv, *landed = delta_rule_fwd(*rule_in, "delta_rule_fwd", Together(*carried))
        head_gathered = landed[len(TAIL_GROUP):]
        tail_p = {**_unpack_gathered(dict(zip(TAIL_GROUP, landed))), **at_layer}
        mid_p = {n: tail_p[n] for n in MID_PARAMS}
        (h, hn), mid_vjp = jax.vjp(lambda h, o, rest, mp: _layer_mid(h, o, *rest, mp, mem), h, o, rest, mid_p)
        y, ffn_saved = ffn_forward(hn, tail_p["w_gate_up"], tail_p["w_down"], "ffn")
        h, last_vjp = jax.vjp(lambda h, y, w: add_norm(h, y, w[None], "norm_ffn_post"), h, y, tail_p["norm_ffn_post"])
        saved.append((head_vjp, mid_vjp, last_vjp, ffn_saved, rule_in, s_in, inv))

    loss, dh = jax.value_and_grad(lambda y: loss_rows(y, target, "loss"))(h)

    def summed(group, landed):
        return {n: sum_slots(t, "sum_grads_" + n) for n, t in zip(group, landed)}

    big_grads, small_grads, head_pending = [{} for _ in range(DEPTH)], [None] * DEPTH, []
    for layer in reversed(range(DEPTH)):
        head_vjp, mid_vjp, last_vjp, ffn_saved, rule_in, s_in, inv = saved[layer]
        dh, dy, d_norm_ffn_post = last_vjp(dh)
        dhn, d_gate_up, d_down, landed = ffn_backward(
            ffn_saved, dy, "ffn", Together(*[ChipExchange(t) for t in head_pending]) if head_pending else None)
        if head_pending:
            big_grads[layer + 1].update(summed(HEAD_GROUP, landed))
        dh_mid, do, d_rest, d_mid_p = mid_vjp((dh, dhn))
        d_tail_p = {**d_mid_p, "w_gate_up": d_gate_up, "w_down": d_down, "norm_ffn_post": d_norm_ffn_post}
        carried = Together(*[ChipExchange(t) for t in _pair_summed(d_tail_p, TAIL_GROUP, "tail")])
        *d_rule_in, = delta_rule_bwd(*rule_in, s_in, inv, do, "delta_rule_bwd", carried)
        big_grads[layer].update(summed(TAIL_GROUP, d_rule_in[4:]))
        dh_head, d_head_p = head_vjp((tuple(d_rule_in[:4]), d_rest))
        dh = dh_mid + dh_head
        small_grads[layer] = {n: t for n, t in {**d_head_p, **d_tail_p}.items() if n in small}
        head_pending = _pair_summed(d_head_p, HEAD_GROUP, "head")
    landed = exchange_alone(Together(*[ChipExchange(t) for t in head_pending]), "exchange_last")
    big_grads[0].update(summed(HEAD_GROUP, landed))
    return loss, dh, big_grads, small_grads


def kernel(x, mem, positions, norm_mix_pre, norm_mix_post, w_in, conv_short, conv_gdn, gdn_a_log, gdn_dt_bias, gdn_norm, w_out, norm_mem, norm_xattn_pre, norm_xattn_post, w_xq, w_xkv, w_xo, norm_ffn_pre, norm_ffn_post, w_gate_up, w_down, loss_target, m_norm_mix_pre, m_norm_mix_post, m_w_in, m_conv_short, m_conv_gdn, m_gdn_a_log, m_gdn_dt_bias, m_gdn_norm, m_w_out, m_norm_mem, m_norm_xattn_pre, m_norm_xattn_post, m_w_xq, m_w_xkv, m_w_xo, m_norm_ffn_pre, m_norm_ffn_post, m_w_gate_up, m_w_down, v_norm_mix_pre, v_norm_mix_post, v_w_in, v_conv_short, v_conv_gdn, v_gdn_a_log, v_gdn_dt_bias, v_gdn_norm, v_w_out, v_norm_mem, v_norm_xattn_pre, v_norm_xattn_post, v_w_xq, v_w_xkv, v_w_xo, v_norm_ffn_pre, v_norm_ffn_post, v_w_gate_up, v_w_down):
    given = dict(locals())
    weights = {n: given[n] for n in WEIGHTS}
    me = _slot(_place())

    def in_place(shard):
        full = jnp.zeros(shard.shape[:-1] + (shard.shape[-1] * N_DEV,), f32)
        return lax.dynamic_update_slice_in_dim(full, shard, me * shard.shape[-1], axis=shard.ndim - 1)

    placed = [in_place(conv_short), in_place(conv_gdn)]
    conv_short_full, conv_gdn_full = _unpack_small(all_reduce_small(_pack_small(placed), "gather_conv"), placed)
    small = {n: weights[n] for n in NORMS + ("gdn_a_log", "gdn_dt_bias", "gdn_norm")}
    small["conv_short"], small["conv_gdn"] = conv_short_full, conv_gdn_full

    cos_t, sin_t = _rope_tables(positions[0])
    loss, grad_x, big_layers, small_layers = _forward_backward(
        x[0], _pack_big(weights), small, mem[0], cos_t, sin_t, loss_target[0])
    grads = _unpack_shard(big_layers)

    names = sorted(small)
    parts = [jnp.stack([layer[n] for layer in small_layers]) for n in names] + [loss.reshape(1)]
    reduced = _unpack_small(all_reduce_small(_pack_small(parts), "reduce_small"), parts)
    loss = reduced[-1][0]
    for n, g in zip(names, reduced[:-1]):
        if n in ("conv_short", "conv_gdn"):
            width = weights[n].shape[-1]
            g = lax.dynamic_slice_in_dim(g, me * width, width, axis=g.ndim - 1)
        grads[n] = g

    delta, new_m, new_v = {}, {}, {}
    for n in WEIGHTS:
        delta[n], new_m[n], new_v[n] = adamw(weights[n], grads[n], given["m_" + n], given["v_" + n], "adamw_" + n)
    return (loss, grad_x[None], *[grads[n] for n in WEIGHTS], *[delta[n] for n in WEIGHTS],
            *[new_m[n] for n in WEIGHTS], *[new_v[n] for n in WEIGHTS])
```

```python
import functools

import jax
import jax.numpy as jnp
from jax import lax
from jax.experimental import pallas as pl
from jax.experimental.pallas import tpu as pltpu

f32 = jnp.float32
bf16 = jnp.bfloat16
MESH = pl.DeviceIdType.MESH

N_DEV = 8
DEPTH = 4
D_MODEL = 1024
EPS = 1e-6
ATTN_HEADS, ATTN_HEAD_DIM = 4, 64
ATTN_WIDTH = ATTN_HEADS * ATTN_HEAD_DIM
DILATIONS = (1, 4, 16)
QB = 128
ROPE_THETA = 500000.0
ROPE_DIM = ATTN_HEAD_DIM // 4
CONV_WIDTH = 256
GDN_HEADS, GDN_HEAD_DIM = 4, 128
GDN_WIDTH = GDN_HEADS * GDN_HEAD_DIM
GDN_CHUNK = 64
XATTN_HEADS, XATTN_HEAD_DIM = 4, 256
LANES = 128
ROW_TILE = 512
VMEM_LIMIT = 56 * 1024 * 1024

ADAM_LR, ADAM_B1, ADAM_B2, ADAM_EPS, ADAM_WD, ADAM_STEP = 0.001, 0.9, 0.999, 1e-08, 0.01, 10

COL_SHARDED = ("w_in", "w_xkv", "w_gate_up")
ROW_SHARDED = (("w_out", 128), ("w_xq", 128), ("w_xo", 128), ("w_down", 352))
HEAD_GROUP = ("w_in",)
TAIL_GROUP = ("w_gate_up", "w_xkv", "rows")
NORMS = ("norm_mix_pre", "norm_mix_post", "norm_mem", "norm_xattn_pre", "norm_xattn_post", "norm_ffn_pre", "norm_ffn_post")
WEIGHTS = ("norm_mix_pre", "norm_mix_post", "w_in", "conv_short", "conv_gdn", "gdn_a_log", "gdn_dt_bias", "gdn_norm", "w_out",
           "norm_mem", "norm_xattn_pre", "norm_xattn_post", "w_xq", "w_xkv", "w_xo", "norm_ffn_pre", "norm_ffn_post",
           "w_gate_up", "w_down")


def _params(n_grid):
    return pltpu.CompilerParams(dimension_semantics=("arbitrary",) * n_grid, vmem_limit_bytes=VMEM_LIMIT)


def _pick(n, cands):
    for c in cands:
        if n % c == 0:
            return c
    return n


MXU_FLOPS = 9.0e14
HBM_BYTES_PER_S = 2.5e12
VMEM_RMW_BYTES_PER_S = 7.0e12
STEP_S = 0.4e-6
MATMUL_VMEM = 44 * 1024 * 1024


def _tiles(m, n, k, sa, sb, so):
    def divisors(d):
        return sorted({d} | {d // s for s in range(1, d // LANES + 1) if d % s == 0 and (d // s) % LANES == 0}, reverse=True)

    best = None
    for tk in divisors(k):
        nk = k // tk
        for tm in divisors(m):
            for tn_ in divisors(n):
                per_step = tm * tk * sa + tk * tn_ * sb + tm * tn_ * so
                vmem = 2 * per_step + (tm * tn_ * 4 if nk > 1 else 0)
                vmem += (tm * tk * 2 if sa == 4 else 0) + (tk * tn_ * 2 if sb == 4 else 0) + tm * tn_ * 4
                if vmem > MATMUL_VMEM:
                    continue
                moved = m * k * sa * (1 if nk == 1 else n // tn_) + k * n * sb * (1 if nk == 1 and n == tn_ else m // tm) + m * n * so
                busy = 2 * m * n * k / MXU_FLOPS + (m * n * 8 * nk / VMEM_RMW_BYTES_PER_S if nk > 1 else 0)
                cost = max(moved / HBM_BYTES_PER_S, busy) + per_step / HBM_BYTES_PER_S + (m // tm) * (n // tn_) * nk * STEP_S
                if best is None or cost < best[0]:
                    best = (cost, tm, tn_, tk)
    return best[1:]


def _mm(a, b, ta, tb, out_dtype, name):
    m, k = (a.shape[1], a.shape[0]) if ta else a.shape
    n = b.shape[0] if tb else b.shape[1]
    tm, tn, tk = _tiles(m, n, k, a.dtype.itemsize, b.dtype.itemsize, jnp.dtype(out_dtype).itemsize)
    nk = k // tk
    a_spec = pl.BlockSpec((tk, tm), lambda i, j, kk: (kk, i)) if ta else pl.BlockSpec((tm, tk), lambda i, j, kk: (i, kk))
    b_spec = pl.BlockSpec((tn, tk), lambda i, j, kk: (j, kk)) if tb else pl.BlockSpec((tk, tn), lambda i, j, kk: (kk, j))
    dims = (((0 if ta else 1,), (1 if tb else 0,)), ((), ()))

    def body(a_ref, b_ref, o_ref, *acc):
        kk = pl.program_id(2)
        p = lax.dot_general(a_ref[...].astype(bf16), b_ref[...].astype(bf16), dims, preferred_element_type=f32)
        if nk == 1:
            o_ref[...] = p.astype(o_ref.dtype)
            return
        acc_ref, = acc

        @pl.when(kk == 0)
        def _():
            acc_ref[...] = p

        @pl.when(kk > 0)
        def _():
            acc_ref[...] += p

        @pl.when(kk == nk - 1)
        def _():
            o_ref[...] = acc_ref[...].astype(o_ref.dtype)

    return pl.pallas_call(
        body, name=name, grid=(m // tm, n // tn, nk), in_specs=[a_spec, b_spec],
        out_specs=pl.BlockSpec((tm, tn), lambda i, j, kk: (i, j)), out_shape=jax.ShapeDtypeStruct((m, n), out_dtype),
        scratch_shapes=[pltpu.VMEM((tm, tn), f32)] if nk > 1 else [], compiler_params=_params(3))(a, b)


def _linear(x, w, name):
    @jax.custom_vjp
    def lin(x, w):
        return _mm(x, w, False, False, f32, name + "_y")

    def lin_f(x, w):
        return _mm(x, w, False, False, f32, name + "_y"), (x, w)

    def lin_b(res, dy):
        x, w = res
        return _mm(dy, w, False, True, f32, name + "_dx"), _mm(x, dy, True, False, bf16, name + "_dw")

    lin.defvjp(lin_f, lin_b)
    return lin(x, w)


def _bdot(a, b, form):
    dims = {"nn": ((1,), (0,)), "nt": ((1,), (1,)), "tn": ((0,), (0,))}[form]
    return lax.dot_general(a.astype(bf16), b.astype(bf16), (dims, ((), ())), preferred_element_type=f32)


def ffn_forward(hn, w_gate_up, w_down, name):
    s, k = hn.shape
    n_blocks, _, width = w_gate_up.shape
    half = n_blocks // 2
    tm = 1024
    blocked = jax.ShapeDtypeStruct((half, s, width), bf16)

    def act_body(x_ref, wg_ref, wu_ref, gate_ref, up_ref, act_ref):
        x = x_ref[...]
        gate, up = _bdot(x, wg_ref[...], "nn"), _bdot(x, wu_ref[...], "nn")
        gate_ref[...], up_ref[...] = gate.astype(bf16), up.astype(bf16)
        act_ref[...] = (jax.nn.silu(gate) * up).astype(bf16)

    tile = pl.BlockSpec((None, tm, width), lambda i, d: (d, i, 0))
    gate, up, act = pl.pallas_call(
        act_body, name=name + "_act", grid=(s // tm, half),
        in_specs=[pl.BlockSpec((tm, k), lambda i, d: (i, 0)), pl.BlockSpec((None, k, width), lambda i, d: (d, 0, 0)),
                  pl.BlockSpec((None, k, width), lambda i, d: (d + half, 0, 0))],
        out_specs=[tile] * 3, out_shape=[blocked] * 3, compiler_params=_params(2))(hn, w_gate_up, w_gate_up)

    n = w_down.shape[1]
    tn = 512

    def y_body(act_ref, w_ref, y_ref):
        y_ref[...] = sum(_bdot(act_ref[d], w_ref[d * width:(d + 1) * width, :], "nn") for d in range(half))

    y = pl.pallas_call(
        y_body, name=name + "_y", grid=(s // tm, n // tn),
        in_specs=[pl.BlockSpec((half, tm, width), lambda i, j: (0, i, 0)), pl.BlockSpec((half * width, tn), lambda i, j: (0, j))],
        out_specs=pl.BlockSpec((tm, tn), lambda i, j: (i, j)), out_shape=_sds((s, n)), compiler_params=_params(2))(act, w_down)
    return y, (hn, w_gate_up, w_down, gate, up, act)


def ffn_backward(saved, dy, name, exchange=None):
    hn, w_gate_up, w_down, gate, up, act = saved
    s, k = hn.shape
    n_blocks, _, width = w_gate_up.shape
    half = n_blocks // 2
    n = w_down.shape[1]
    tm = 1024
    blocked = jax.ShapeDtypeStruct((half, s, width), bf16)
    carried = len(exchange.operands) if exchange else 0
    steps = (s // tm, half)

    def dact_body(dy_ref, w_ref, gate_ref, up_ref, *refs):
        x_refs, refs = refs[:carried], refs[carried:]
        (dgate_ref, dup_ref), refs = refs[:2], refs[2:]
        if exchange:
            at = pl.program_id(0) * steps[1] + pl.program_id(1)
            start, wait = exchange.bind(x_refs, refs[:carried], refs[carried:])
            pl.when(at == 0)(start)
        d_act = _bdot(dy_ref[...], w_ref[...], "nt")
        g, u = gate_ref[...].astype(f32), up_ref[...].astype(f32)
        sig = jax.nn.sigmoid(g)
        dgate_ref[...] = (d_act * u * sig * (1.0 + g * (1.0 - sig))).astype(bf16)
        dup_ref[...] = (d_act * g * sig).astype(bf16)
        if exchange:
            pl.when(at == steps[0] * steps[1] - 1)(wait)

    tile = pl.BlockSpec((None, tm, width), lambda i, d: (d, i, 0))
    d_gate, d_up, *landed = pl.pallas_call(
        dact_body, name=name + "_dact", grid=steps,
        in_specs=[pl.BlockSpec((tm, n), lambda i, d: (i, 0)), pl.BlockSpec((width, n), lambda i, d: (d, 0)), tile, tile] + [ANY] * carried,
        out_specs=[tile, tile] + [ANY] * carried, out_shape=[blocked, blocked] + (exchange.out_shapes if exchange else []),
        scratch_shapes=exchange.scratch if exchange else [],
        compiler_params=_params(2))(dy, w_down, gate, up, *(exchange.operands if exchange else []))

    def dx_body(dg_ref, du_ref, w_ref, dx_ref):
        dx_ref[...] = sum(_bdot(dg_ref[d], w_ref[d], "nt") + _bdot(du_ref[d], w_ref[d + half], "nt") for d in range(half))

    tx = 512
    rows = pl.BlockSpec((half, tx, width), lambda i: (0, i, 0))
    dx = pl.pallas_call(
        dx_body, name=name + "_dx", grid=(s // tx,), in_specs=[rows, rows, _whole(w_gate_up.shape)],
        out_specs=pl.BlockSpec((tx, k), lambda i: (i, 0)), out_shape=_sds((s, k)), compiler_params=_params(1))(d_gate, d_up, w_gate_up)

    def dw1_body(x_ref, dg_ref, du_ref, dw_ref):
        d_block = jnp.where(pl.program_id(0) < half, dg_ref[...], du_ref[...])
        dw_ref[...] = _bdot(x_ref[...], d_block, "tn").astype(bf16)

    d_w_gate_up = pl.pallas_call(
        dw1_body, name=name + "_dw1", grid=(n_blocks,),
        in_specs=[_whole((s, k)), pl.BlockSpec((None, s, width), lambda b: (jnp.minimum(b, half - 1), 0, 0)),
                  pl.BlockSpec((None, s, width), lambda b: (jnp.maximum(b - half, 0), 0, 0))],
        out_specs=pl.BlockSpec((None, k, width), lambda b: (b, 0, 0)), out_shape=jax.ShapeDtypeStruct(w_gate_up.shape, bf16),
        compiler_params=_params(1))(hn, d_gate, d_up)

    tn = 512

    def dw2_body(act_ref, dy_ref, dw_ref):
        dw_ref[...] = _bdot(act_ref[...], dy_ref[...], "tn").astype(bf16)

    d_w_down = pl.pallas_call(
        dw2_body, name=name + "_dw2", grid=(half, n // tn),
        in_specs=[pl.BlockSpec((None, s, width), lambda d, j: (d, 0, 0)), pl.BlockSpec((s, tn), lambda d, j: (0, j))],
        out_specs=pl.BlockSpec((width, tn), lambda d, j: (d, j)), out_shape=jax.ShapeDtypeStruct(w_down.shape, bf16),
        compiler_params=_params(2))(act, dy)
    return dx, d_w_gate_up, d_w_down, landed


def _joined_linear(parts, w, name):
    s, n = parts[0].shape[0], w.shape[1]
    widths = [p.shape[1] for p in parts]
    edges = [sum(widths[:i]) for i in range(len(widths) + 1)]
    spans = list(zip(edges[:-1], edges[1:]))
    tm, tn = 512, 512

    def forward(*args):
        *xs, w = args

        def y_body(*refs):
            *x_refs, w_ref, y_ref = refs
            y_ref[...] = sum(_bdot(x_ref[...], w_ref[a:b, :], "nn") for x_ref, (a, b) in zip(x_refs, spans))

        y = pl.pallas_call(
            y_body, name=name + "_y", grid=(s // tm, n // tn),
            in_specs=[pl.BlockSpec((tm, k), lambda i, j: (i, 0)) for k in widths] + [pl.BlockSpec((edges[-1], tn), lambda i, j: (0, j))],
            out_specs=pl.BlockSpec((tm, tn), lambda i, j: (i, j)), out_shape=_sds((s, n)), compiler_params=_params(2))(*xs, w)
        return y, args

    def backward(args, dy):
        *xs, w = args

        def dx_body(dy_ref, w_ref, *dx_refs):
            d_all = _bdot(dy_ref[...], w_ref[...], "nt")
            for dx_ref, (a, b) in zip(dx_refs, spans):
                dx_ref[...] = d_all[:, a:b]

        dxs = pl.pallas_call(
            dx_body, name=name + "_dx", grid=(s // tm,), in_specs=[pl.BlockSpec((tm, n), lambda i: (i, 0)), _whole(w.shape)],
            out_specs=[pl.BlockSpec((tm, k), lambda i: (i, 0)) for k in widths], out_shape=[_sds((s, k)) for k in widths],
            compiler_params=_params(1))(dy, w)

        def dw_body(*refs):
            *x_refs, dy_ref, dw_ref = refs
            dy_tile = dy_ref[...]
            dw_ref[...] = jnp.concatenate([_bdot(x_ref[...], dy_tile, "tn") for x_ref in x_refs], axis=0).astype(bf16)

        dw = pl.pallas_call(
            dw_body, name=name + "_dw", grid=(n // tn,),
            in_specs=[_whole((s, k)) for k in widths] + [pl.BlockSpec((s, tn), lambda j: (0, j))],
            out_specs=pl.BlockSpec((edges[-1], tn), lambda j: (0, j)), out_shape=jax.ShapeDtypeStruct(w.shape, bf16),
            compiler_params=_params(1))(*xs, dy)
        return (*dxs, dw)

    @jax.custom_vjp
    def op(*args):
        return forward(*args)[0]

    op.defvjp(forward, backward)
    return op(*parts, w)


def _split_cols(x, widths):
    edges = [sum(widths[:i]) for i in range(len(widths) + 1)]

    def cut(x):
        return tuple(x[:, a:b] for a, b in zip(edges[:-1], edges[1:]))

    @jax.custom_vjp
    def split(x):
        return cut(x)

    split.defvjp(lambda x: (cut(x), None), lambda _, cts: (jnp.concatenate(cts, axis=1),))
    return split(x)


def _block_op(name, f, grid, in_specs, out_defs, arrays, diff, acc=None, gdefs=None):
    acc, gdefs = acc or {}, gdefs or {}
    n_in, n_out, n_grid = len(in_specs), len(out_defs), len(grid)

    def fwd_call(*xs):
        def body(*refs):
            outs = f(*[r[...] for r in refs[:n_in]])
            for r, o in zip(refs[n_in:], outs):
                r[...] = o.astype(r.dtype)

        return pl.pallas_call(
            body, name=name + "_fwd", grid=grid, in_specs=in_specs, out_specs=[d[1] for d in out_defs],
            out_shape=[d[0] for d in out_defs], compiler_params=_params(n_grid))(*xs)

    def bwd_call(*xs_and_cts):
        def body(*refs):
            xs = [r[...] for r in refs[:n_in]]
            cts = tuple(r[...] for r in refs[n_in:n_in + n_out])

            def of_diff(*dx):
                full = list(xs)
                for i, v in zip(diff, dx):
                    full[i] = v
                return tuple(f(*full))

            _, vjp = jax.vjp(of_diff, *[xs[i] for i in diff])
            grads = vjp(cts)
            for i, g, r in zip(diff, grads, refs[n_in + n_out:]):
                if i in acc:
                    first = functools.reduce(jnp.logical_and, [pl.program_id(a) == 0 for a in acc[i]])

                    @pl.when(first)
                    def _(r=r):
                        r[...] = jnp.zeros_like(r)

                    r[...] += g.astype(r.dtype)
                else:
                    r[...] = g.astype(r.dtype)

        g_defs = [gdefs.get(i, (jax.ShapeDtypeStruct(arrays[i].shape, f32), in_specs[i])) for i in diff]
        return pl.pallas_call(
            body, name=name + "_bwd", grid=grid, in_specs=list(in_specs) + [d[1] for d in out_defs],
            out_specs=[d[1] for d in g_defs], out_shape=[d[0] for d in g_defs], compiler_params=_params(n_grid))(*xs_and_cts)

    return fwd_call, bwd_call


def _simple_op(name, f, grid, in_specs, out_defs, arrays, diff, acc=None):
    fwd_call, bwd_call = _block_op(name, f, grid, in_specs, out_defs, arrays, diff, acc)

    @jax.custom_vjp
    def op(*xs):
        return tuple(fwd_call(*xs))

    def op_f(*xs):
        return tuple(fwd_call(*xs)), xs

    def op_b(xs, cts):
        grads = bwd_call(*xs, *cts)
        out = [jnp.zeros_like(x) for x in xs]
        for i, g in zip(diff, grads):
            out[i] = g
        return tuple(out)

    op.defvjp(op_f, op_b)
    return op(*arrays)


def _rows(width, tile=ROW_TILE):
    return pl.BlockSpec((tile, width), lambda i: (i, 0))


def _whole(shape):
    return pl.BlockSpec(shape, lambda *_: (0,) * len(shape))


def _sds(shape):
    return jax.ShapeDtypeStruct(shape, f32)


def _rms(x, w):
    return x * lax.rsqrt(jnp.mean(x * x, axis=-1, keepdims=True) + EPS) * w


def rms_norm(x, w, name):
    r, d = x.shape
    tile = min(ROW_TILE, r)
    return _simple_op(name, lambda x, w: (_rms(x, w),), (r // tile,), [_rows(d, tile), _whole((1, d))],
                      [(_sds((r, d)), _rows(d, tile))], (x, w), (0, 1), {1: (0,)})[0]


def add_norm(h, y, w, name):
    r, d = h.shape
    return _simple_op(name, lambda h, y, w: (h + _rms(y, w),), (r // ROW_TILE,), [_rows(d), _rows(d), _whole((1, d))],
                      [(_sds((r, d)), _rows(d))], (h, y, w), (0, 1, 2), {2: (0,)})[0]


def add_norm_then_norm(h, y, w_post, w_pre, name):
    r, d = h.shape

    def f(h, y, w_post, w_pre):
        h_new = h + _rms(y, w_post)
        return h_new, _rms(h_new, w_pre)

    return _simple_op(name, f, (r // ROW_TILE,), [_rows(d), _rows(d), _whole((1, d)), _whole((1, d))],
                      [(_sds((r, d)), _rows(d))] * 2, (h, y, w_post, w_pre), (0, 1, 2, 3), {2: (0,), 3: (0,)})


def _swap8(x):
    def raw(x):
        lane = lax.broadcasted_iota(jnp.int32, x.shape, 1) % ATTN_HEAD_DIM
        half = ROPE_DIM // 2
        up = pltpu.roll(x, x.shape[1] - half, axis=1)
        down = pltpu.roll(x, half, axis=1)
        return jnp.where(lane < half, up, jnp.where(lane < ROPE_DIM, down, 0.0))

    @jax.custom_vjp
    def swap(x):
        return raw(x)

    swap.defvjp(lambda x: (raw(x), None), lambda _, g: (raw(g),))
    return swap(x)


def rope(x, cos_t, sin_t, scale, name):
    r, d = x.shape
    return _simple_op(name, lambda x, c, s: ((x * c + _swap8(x) * s) * scale,), (r // ROW_TILE,), [_rows(d)] * 3,
                      [(_sds((r, d)), _rows(d))], (x, cos_t, sin_t), (0,))[0]


def _shift_rows(x, k):
    n = x.shape[0]

    def down(x):
        row = lax.broadcasted_iota(jnp.int32, x.shape, 0)
        return jnp.where(row >= k, pltpu.roll(x, k, axis=0), 0.0)

    def up(x):
        row = lax.broadcasted_iota(jnp.int32, x.shape, 0)
        return jnp.where(row < n - k, pltpu.roll(x, n - k, axis=0), 0.0)

    @jax.custom_vjp
    def shift(x):
        return down(x)

    shift.defvjp(lambda x: (down(x), None), lambda _, g: (up(g),))
    return shift(x)


def _causal_conv(x, w):
    taps = w.shape[0]
    y = x * w[taps - 1:taps, :]
    for j in range(taps - 1):
        y = y + _shift_rows(x, taps - 1 - j) * w[j:j + 1, :]
    return y


def _cols(rows, at=0):
    return pl.BlockSpec((rows, LANES), lambda j: (0, at + j))


def short_conv(cb, cc, cx, w, name):
    s, c = cb.shape
    taps = w.shape[0]
    return _simple_op(name, lambda b, c_, x, w: (b * _causal_conv(c_ * x, w),), (c // LANES,),
                      [_cols(s)] * 3 + [_cols(taps)], [(_sds((s, c)), _cols(s))], (cb, cc, cx, w), (0, 1, 2, 3))[0]


def gdn_pre(qkv, w, name):
    s, c = qkv.shape
    taps = w.shape[0]

    def f(x, w):
        j = pl.program_id(0)
        y = jax.nn.silu(_causal_conv(x, w))
        normed = y * lax.rsqrt(jnp.sum(y * y, axis=-1, keepdims=True) + EPS)
        scale = jnp.where(j < GDN_HEADS, GDN_HEAD_DIM ** -0.5, 1.0).astype(f32)
        return (jnp.where(j < 2 * GDN_HEADS, normed * scale, y),)

    return _simple_op(name, f, (c // LANES,), [_cols(s), _cols(taps)], [(_sds((s, c)), _cols(s))], (qkv, w), (0, 1))[0]


def gate_beta(ab, pv, name):
    s = ab.shape[0]

    def f(ab, pv):
        lane = lax.broadcasted_iota(jnp.int32, ab.shape, 1)
        g = -jnp.exp(pv[0:1, :]) * jax.nn.softplus(ab + pv[1:2, :])
        return (jnp.where(lane < GDN_HEADS, g, jnp.where(lane < 2 * GDN_HEADS, jax.nn.sigmoid(ab), 0.0)),)

    return _simple_op(name, f, (s // ROW_TILE,), [_rows(LANES), _whole((8, LANES))], [(_sds((s, LANES)), _rows(LANES))],
                      (ab, pv), (0, 1), {1: (0,)})[0]


def gdn_post(o, gate, w, name):
    s, c = o.shape

    def f(o, g, w):
        heads = [slice(hd * LANES, (hd + 1) * LANES) for hd in range(c // LANES)]
        return (jnp.concatenate([_rms(o[:, hd], w) * jax.nn.silu(g[:, hd]) for hd in heads], axis=1),)

    return _simple_op(name, f, (s // ROW_TILE,), [_rows(c), _rows(c), _whole((1, LANES))], [(_sds((s, c)), _rows(c))],
                      (o, gate, w), (0, 1, 2), {2: (0,)})[0]


def attn_merge(outs, lses, name):
    s, c = outs[0].shape

    def f(o1, o2, o3, l1, l2, l3):
        m = lax.stop_gradient(jnp.maximum(jnp.maximum(l1, l2), l3))
        e1, e2, e3 = jnp.exp(l1 - m), jnp.exp(l2 - m), jnp.exp(l3 - m)
        return ((e1 * o1 + e2 * o2 + e3 * o3) / (e1 + e2 + e3),)

    return _simple_op(name, f, (s // ROW_TILE,), [_rows(c)] * 6, [(_sds((s, c)), _rows(c))], (*outs, *lses), tuple(range(6)))[0]


def loss_rows(y, target, name):
    s, d = y.shape
    nt = s // ROW_TILE

    def f(y, t):
        e = y - t
        part = 0.5 * jnp.sum(jnp.mean(e * e, axis=-1, keepdims=True), axis=0, keepdims=True)
        return (jnp.broadcast_to(part * (1.0 / (8 * LANES)), (8, LANES)),)

    out = _simple_op(name, f, (nt,), [_rows(d)] * 2, [(_sds((nt * 8, LANES)), pl.BlockSpec((8, LANES), lambda i: (i, 0)))],
                     (y, target), (0,))[0]
    return jnp.sum(out)


def _mxu(a, b, form):
    dims = {"nn": ((1,), (0,)), "nt": ((1,), (1,)), "tn": ((0,), (0,))}

    def raw(a, b, form):
        return lax.dot_general(a.astype(bf16), b.astype(bf16), (dims[form], ((), ())), preferred_element_type=f32)

    @jax.custom_vjp
    def prod(a, b):
        return raw(a, b, form)

    def prod_b(res, ct):
        a, b = res
        if form == "nn":
            return raw(ct, b, "nt"), raw(a, ct, "tn")
        if form == "nt":
            return raw(ct, b, "nn"), raw(ct, a, "tn")
        return raw(b, ct, "nt"), raw(a, ct, "nn")

    prod.defvjp(lambda a, b: (raw(a, b, form), (a, b)), prod_b)
    return prod(a, b)


def _masked_heads_attention(q, keys, values, seen):
    dh = ATTN_HEAD_DIM
    outs, lses = [], []
    for hd in range(q.shape[1] // dh):
        at = slice(hd * dh, (hd + 1) * dh)
        sc = jnp.where(seen, _mxu(q[:, at], keys[:, at], "nt"), -jnp.inf)
        m = lax.stop_gradient(jnp.max(sc, axis=-1, keepdims=True))
        p = jnp.exp(sc - m)
        l = jnp.sum(p, axis=-1, keepdims=True)
        outs.append(_mxu(p / l, values[:, at], "nn"))
        lses.append(jnp.broadcast_to(m + jnp.log(l), (q.shape[0], dh)))
    return jnp.concatenate(outs, axis=1), jnp.concatenate(lses, axis=1)


def band_attention(q, k, v, nb, name):
    r, qb, width = q.shape

    def f(q, kp, kc, vp, vc):
        has_prev = (pl.program_id(0) % nb) > 0
        i = lax.broadcasted_iota(jnp.int32, (qb, 2 * qb), 0)
        j = lax.broadcasted_iota(jnp.int32, (qb, 2 * qb), 1)
        seen = jnp.logical_or(jnp.logical_and(jnp.logical_and(j < qb, j >= i), has_prev), jnp.logical_and(j >= qb, j - qb <= i))
        return _masked_heads_attention(q, jnp.concatenate([kp, kc], axis=0), jnp.concatenate([vp, vc], axis=0), seen)

    blk = (None, qb, width)
    cur = pl.BlockSpec(blk, lambda b: (b, 0, 0))
    prev = pl.BlockSpec(blk, lambda b: (jnp.maximum(b - 1, 0), 0, 0))
    shape = _sds((r, qb, width))
    fwd_call, bwd_call = _block_op(name, f, (r,), [cur, prev, cur, prev, cur], [(shape, cur), (shape, cur)],
                                   (q, k, k, v, v), (0, 1, 2, 3, 4), gdefs={1: (shape, cur), 3: (shape, cur)})

    def to_prev(g):
        return jnp.concatenate([g[1:], jnp.zeros_like(g[:1])], axis=0)

    @jax.custom_vjp
    def op(q, k, v):
        return tuple(fwd_call(q, k, k, v, v))

    def op_b(res, cts):
        q, k, v = res
        dq, dkp, dkc, dvp, dvc = bwd_call(q, k, k, v, v, *cts)
        return dq, dkc + to_prev(dkp), dvc + to_prev(dvp)

    op.defvjp(lambda q, k, v: (tuple(fwd_call(q, k, k, v, v)), (q, k, v)), op_b)
    return op(q, k, v)


def dilated_attention(q, k, v, name):
    s = q.shape[0]
    outs, lses = [], []
    for d in DILATIONS:
        length = s // d
        nb = length // QB
        def to_residue(t):
            return t.reshape(length, d, ATTN_WIDTH).transpose(1, 0, 2).reshape(d * nb, QB, ATTN_WIDTH)

        def from_residue(t):
            return t.reshape(d, length, ATTN_WIDTH).transpose(1, 0, 2).reshape(s, ATTN_WIDTH)

        o, lse = band_attention(to_residue(q), to_residue(k), to_residue(v), nb, f"{name}_d{d}")
        outs.append(from_residue(o))
        lses.append(from_residue(lse))
    return attn_merge(outs, lses, name + "_merge")


def cross_attention(q, kv, name):
    s = q.shape[0]
    m = kv.shape[0]
    width = XATTN_HEADS * XATTN_HEAD_DIM
    tq = 512

    def f(q, k, v):
        sc = _mxu(q, k, "nt") * (XATTN_HEAD_DIM ** -0.5)
        mx = lax.stop_gradient(jnp.max(sc, axis=-1, keepdims=True))
        p = jnp.exp(sc - mx)
        return (_mxu(p / jnp.sum(p, axis=-1, keepdims=True), v, "nn"),)

    q_spec = pl.BlockSpec((tq, XATTN_HEAD_DIM), lambda a, i: (i, a))
    k_spec = pl.BlockSpec((m, XATTN_HEAD_DIM), lambda a, i: (0, a))
    v_spec = pl.BlockSpec((m, XATTN_HEAD_DIM), lambda a, i: (0, a + XATTN_HEADS))
    half = _sds((m, width))
    fwd_call, bwd_call = _block_op(name, f, (XATTN_HEADS, s // tq), [q_spec, k_spec, v_spec], [(_sds((s, width)), q_spec)],
                                   (q, kv, kv), (0, 1, 2), acc={1: (1,), 2: (1,)}, gdefs={1: (half, k_spec), 2: (half, k_spec)})

    @jax.custom_vjp
    def op(q, kv):
        return fwd_call(q, kv, kv)[0]

    def op_b(res, ct):
        q, kv = res
        dq, dk, dv = bwd_call(q, kv, kv, ct)
        return dq, jnp.concatenate([dk, dv], axis=1)

    op.defvjp(lambda q, kv: (fwd_call(q, kv, kv)[0], (q, kv)), op_b)
    return op(q, kv)


def _hi(a, b, form="nn"):
    dims = {"nn": ((1,), (0,)), "nt": ((1,), (1,)), "tn": ((0,), (0,))}[form]
    return lax.dot_general(a, b, (dims, ((), ())), precision=lax.Precision.HIGH, preferred_element_type=f32)


def _running_sum(g):
    def raw(x, form):
        c = x.shape[0]
        tri = (lax.broadcasted_iota(jnp.int32, (c, c), 0) >= lax.broadcasted_iota(jnp.int32, (c, c), 1)).astype(bf16)
        hi = x.astype(bf16)
        rest = x - hi.astype(f32)
        mid = rest.astype(bf16)
        low = (rest - mid.astype(f32)).astype(bf16)
        dims = (((1,) if form == "nn" else (0,), (0,)), ((), ()))
        return sum(lax.dot_general(tri, part, dims, preferred_element_type=f32) for part in (hi, mid, low))

    @jax.custom_vjp
    def run(x):
        return raw(x, "nn")

    run.defvjp(lambda x: (raw(x, "nn"), None), lambda _, ct: (raw(ct, "tn"),))
    return run(g)


def _unit_lower_inverse(a):
    c = a.shape[0]
    eye = (lax.broadcasted_iota(jnp.int32, (c, c), 0) == lax.broadcasted_iota(jnp.int32, (c, c), 1)).astype(f32)
    inv, power = eye - a, -a
    for _ in range(c.bit_length() - 2):
        power = _hi(power, power)
        inv = inv + _hi(inv, power)
    return inv


def _known_inverse(a, t):
    @jax.custom_vjp
    def inv(a, t):
        return t

    def inv_b(t, ct):
        return -_hi(_hi(t, ct, "tn"), t, "nt"), jnp.zeros_like(t)

    inv.defvjp(lambda a, t: (t, t), inv_b)
    return inv(a, t)


def _chunk_prepare(q, k, v, g, beta, known_inv=None):
    c = q.shape[0]
    i = lax.broadcasted_iota(jnp.int32, (c, c), 0)
    j = lax.broadcasted_iota(jnp.int32, (c, c), 1)
    causal, strict = i >= j, i > j
    dec = _running_sum(g)
    dec_i = dec[:, :c]
    rel = jnp.exp(jnp.where(causal, dec_i - dec_i.T, -jnp.inf))
    k_beta = k * beta
    on_k = _mxu(jnp.concatenate([k_beta, q], axis=0), k, "nt")
    a = jnp.where(strict, on_k[:c] * rel, 0.0)
    attn = jnp.where(causal, on_k[c:] * rel, 0.0)
    inv = _unit_lower_inverse(a) if known_inv is None else _known_inverse(a, known_inv)
    e_dec = jnp.exp(dec)
    solved = _hi(inv, jnp.concatenate([v * beta, k_beta * e_dec], axis=1))
    u, w = solved[:, :v.shape[1]], solved[:, v.shape[1]:]
    total = jnp.sum(g, axis=0, keepdims=True)
    return inv, (u, w, attn, q * e_dec, k * jnp.exp(total - dec), total)


def _chunk_advance(u, w, attn, q_dec, k_dec, total, s0):
    c = u.shape[0]
    on_state = _mxu(jnp.concatenate([w, q_dec], axis=0), s0, "nn")
    v_new = u - on_state[:c]
    return on_state[c:] + _mxu(attn, v_new, "nn"), s0 * jnp.exp(total) + _mxu(k_dec, v_new, "tn")


def _delta_chunk(q, k, v, g, beta, s0, known_inv=None):
    inv, ready = _chunk_prepare(q, k, v, g, beta, known_inv)
    return (*_chunk_advance(*ready, s0), inv)


def _delta_rule_call(name, walk, steps, in_specs, out_specs, out_shape, operands, exchange, scratch=()):
    n_in, n_out = len(in_specs), len(out_specs)
    carried = len(exchange.operands) if exchange else 0

    def body(*refs):
        ins, refs = refs[:n_in], refs[n_in:]
        x_refs, refs = refs[:carried], refs[carried:]
        outs, refs = refs[:n_out], refs[n_out:]
        land_refs, (state, *refs) = refs[:carried], refs[carried:]
        own, sems = refs[:len(scratch)], refs[len(scratch):]
        step = pl.program_id(0)
        if exchange:
            start, finish = exchange.bind(x_refs, land_refs, sems)
            pl.when(step == 0)(start)

        @pl.when(step == 0)
        def _():
            for r in (state, *own):
                r[...] = jnp.zeros_like(r)

        walk(ins, outs, state, *own)
        if exchange:
            pl.when(step == steps - 1)(finish)

    return pl.pallas_call(
        body, name=name, grid=(steps,), in_specs=list(in_specs) + [ANY] * carried, out_specs=list(out_specs) + [ANY] * carried,
        out_shape=list(out_shape) + (exchange.out_shapes if exchange else []),
        scratch_shapes=[pltpu.VMEM((GDN_HEAD_DIM, GDN_WIDTH), f32), *scratch] + (exchange.scratch if exchange else []),
        compiler_params=_params(1))(*operands, *(exchange.operands if exchange else []))


def _delta_heads():
    heads = [slice(hd * GDN_HEAD_DIM, (hd + 1) * GDN_HEAD_DIM) for hd in range(GDN_HEADS)]
    inv_at = [slice(hd * GDN_CHUNK, (hd + 1) * GDN_CHUNK) for hd in range(GDN_HEADS)]
    return heads, inv_at


def _head_chunk(q, k, v, gates, s0, head, known_inv=None):
    g = jnp.broadcast_to(gates[:, head:head + 1], q.shape)
    beta = jnp.broadcast_to(gates[:, GDN_HEADS + head:GDN_HEADS + head + 1], q.shape)
    return _delta_chunk(q, k, v, g, beta, s0, known_inv)


def delta_rule_fwd(q, k, v, gates, name, exchange=None):
    s, width = q.shape
    c, dk = GDN_CHUNK, GDN_HEAD_DIM
    n = s // c
    heads, inv_at = _delta_heads()

    def walk(ins, outs, state, u_s, w_s, q_s, k_s, attn_s, total_s):
        q_ref, k_ref, v_ref, gates_ref = ins
        o_ref, s_in_ref, inv_ref = outs
        s_in_ref[...] = state[...]
        gates = gates_ref[...]
        ready = [[r[:, hd] for r in (u_s, w_s)] + [attn_s[:, at]] + [r[:, hd] for r in (q_s, k_s)] + [total_s[0:1, hd], state[:, hd]]
                 for hd, at in zip(heads, inv_at)]
        fresh = []
        for i, hd in enumerate(heads):
            g = jnp.broadcast_to(gates[:, i:i + 1], (c, dk))
            beta = jnp.broadcast_to(gates[:, GDN_HEADS + i:GDN_HEADS + i + 1], (c, dk))
            fresh.append(_chunk_prepare(q_ref[:, hd], k_ref[:, hd], v_ref[:, hd], g, beta))
        advanced = [_chunk_advance(*x) for x in ready]
        for hd, at, (o, s1), (inv, (u, w, attn, q_dec, k_dec, total)) in zip(heads, inv_at, advanced, fresh):
            o_ref[:, hd], state[:, hd], inv_ref[:, at] = o, s1, inv
            u_s[:, hd], w_s[:, hd], q_s[:, hd], k_s[:, hd], attn_s[:, at], total_s[0:1, hd] = u, w, q_dec, k_dec, attn, total

    blk = pl.BlockSpec((c, width), lambda t: (jnp.minimum(t, n - 1), 0))
    gt = pl.BlockSpec((c, LANES), lambda t: (jnp.minimum(t, n - 1), 0))
    iv = pl.BlockSpec((c, GDN_HEADS * c), lambda t: (jnp.minimum(t, n - 1), 0))
    out = pl.BlockSpec((c, width), lambda t: (jnp.maximum(t - 1, 0), 0))
    st = pl.BlockSpec((dk, width), lambda t: (jnp.maximum(t - 1, 0), 0))
    rows = pltpu.VMEM((c, width), f32)
    return _delta_rule_call(name, walk, n + 1, [blk] * 3 + [gt], [out, st, iv],
                            [_sds((s, width)), _sds((n * dk, width)), _sds((s, GDN_HEADS * c))], (q, k, v, gates), exchange,
                            scratch=(rows, rows, rows, rows, pltpu.VMEM((c, GDN_HEADS * c), f32), pltpu.VMEM((8, width), f32)))


def delta_rule_bwd(q, k, v, gates, s_in, inv, do, name, exchange=None):
    s, width = q.shape
    c, dk = GDN_CHUNK, GDN_HEAD_DIM
    n = s // c
    heads, inv_at = _delta_heads()

    def walk(ins, outs, dstate):
        q_ref, k_ref, v_ref, gates_ref, s_ref, inv_ref, do_ref = ins
        dq_ref, dk_ref, dv_ref, dgates_ref = outs
        gates = gates_ref[...]
        xs = [[r[:, hd] for r in (q_ref, k_ref, v_ref)] + [gates, s_ref[:, hd]] for hd in heads]
        known = [inv_ref[:, at] for at in inv_at]
        cts = [(do_ref[:, hd], dstate[:, hd]) for hd in heads]
        grads = []
        for i, (x, t, ct) in enumerate(zip(xs, known, cts)):
            _, vjp = jax.vjp(lambda *y, t=t, i=i: _head_chunk(*y, i, known_inv=t)[:2], *x)
            grads.append(vjp(ct))
        dgates = grads[0][3]
        for g in grads[1:]:
            dgates = dgates + g[3]
        dgates_ref[...] = dgates
        for hd, (dq, dk_, dv, _, ds0) in zip(heads, grads):
            dq_ref[:, hd], dk_ref[:, hd], dv_ref[:, hd], dstate[:, hd] = dq, dk_, dv, ds0

    blk = pl.BlockSpec((c, width), lambda t: (n - 1 - t, 0))
    gt = pl.BlockSpec((c, LANES), lambda t: (n - 1 - t, 0))
    st = pl.BlockSpec((dk, width), lambda t: (n - 1 - t, 0))
    iv = pl.BlockSpec((c, GDN_HEADS * c), lambda t: (n - 1 - t, 0))
    return _delta_rule_call(name, walk, n, [blk] * 3 + [gt, st, iv, blk], [blk] * 3 + [gt],
                            [_sds((s, width))] * 3 + [_sds((s, LANES))], (q, k, v, gates, s_in, inv, do), exchange)


def adamw(w, g, m, v, name):
    shape = w.shape
    if len(shape) == 2:
        grid, spec = (1,), pl.BlockSpec(shape, lambda i: (0, 0))
    else:
        tile = shape[1] if shape[1] <= 512 else _pick(shape[1], (512, 256, 128))
        grid, spec = (shape[0], shape[1] // tile), pl.BlockSpec((None, tile, shape[2]), lambda layer, i: (layer, i, 0))

    def body(w_ref, g_ref, m_ref, v_ref, d_ref, nm_ref, nv_ref):
        grad = g_ref[...]
        nm = ADAM_B1 * m_ref[...] + (1.0 - ADAM_B1) * grad
        nv = ADAM_B2 * v_ref[...] + (1.0 - ADAM_B2) * (grad * grad)
        m_hat = nm / (1.0 - ADAM_B1 ** ADAM_STEP)
        v_hat = nv / (1.0 - ADAM_B2 ** ADAM_STEP)
        d_ref[...] = -ADAM_LR * (m_hat / (jnp.sqrt(v_hat) + ADAM_EPS) + ADAM_WD * w_ref[...])
        nm_ref[...] = nm
        nv_ref[...] = nv

    return tuple(pl.pallas_call(body, name=name, grid=grid, in_specs=[spec] * 4, out_specs=[spec] * 3,
                                out_shape=[_sds(shape)] * 3, compiler_params=_params(len(grid)))(w, g, m, v))


def _place():
    return lax.axis_index("x"), lax.axis_index("y"), lax.axis_index("c")


def _flip(p, bits):
    return tuple(1 - v if (bits >> s) & 1 else v for v, s in zip(p, (2, 1, 0)))


def _slot(p):
    return 4 * p[0] + 2 * p[1] + p[2]


def _chip_of(p):
    return 2 * p[0] + p[1]


ANY = pl.BlockSpec(memory_space=pl.ANY)


class Gather:
    scratch = (pltpu.SemaphoreType.DMA((7,)), pltpu.SemaphoreType.DMA((7,)), pltpu.SemaphoreType.DMA)

    def __init__(self, shard):
        self.operand = shard
        self.out_shape = jax.ShapeDtypeStruct((N_DEV,) + shard.shape, shard.dtype)

    def bind(self, x_ref, out_ref, send_sems, recv_sems, local_sem):
        me = _place()
        sibling = _flip(me, 1)
        chips = [_flip(me, 4), _flip(me, 2), _flip(me, 6)]

        def copy(k, block, to, src=None):
            return pltpu.make_async_remote_copy(
                src_ref=out_ref.at[_slot(block)] if src is None else src, dst_ref=out_ref.at[_slot(block)],
                send_sem=send_sems.at[k], recv_sem=recv_sems.at[k], device_id=to, device_id_type=MESH)

        mine = pltpu.make_async_copy(x_ref, out_ref.at[_slot(me)], local_sem)
        first = [copy(0, me, sibling, src=x_ref)] + [copy(1 + j, me, chip, src=x_ref) for j, chip in enumerate(chips)]
        passed = [copy(4 + j, chip, sibling) for j, chip in enumerate(chips)]

        def start():
            mine.start()
            for cp in first:
                cp.start()

        def finish():
            for j, chip in enumerate(chips):
                copy(1 + j, chip, me).wait_recv()
                passed[j].start()
            copy(0, sibling, me).wait_recv()
            for j, chip in enumerate(chips):
                copy(4 + j, _flip(chip, 1), me).wait_recv()
            for cp in first + passed:
                cp.wait_send()
            mine.wait()

        return start, finish


class ChipExchange:
    scratch = (pltpu.SemaphoreType.DMA((3,)), pltpu.SemaphoreType.DMA((3,)), pltpu.SemaphoreType.DMA)

    def __init__(self, blocks):
        self.operand = blocks
        self.out_shape = jax.ShapeDtypeStruct(blocks.shape, blocks.dtype)

    def bind(self, x_ref, out_ref, send_sems, recv_sems, local_sem):
        me = _place()
        peers = [_flip(me, 4), _flip(me, 2), _flip(me, 6)]
        mine = pltpu.make_async_copy(x_ref.at[_chip_of(me)], out_ref.at[_chip_of(me)], local_sem)

        def copy(j, src_chip, dst_chip):
            return pltpu.make_async_remote_copy(
                src_ref=x_ref.at[src_chip], dst_ref=out_ref.at[dst_chip], send_sem=send_sems.at[j],
                recv_sem=recv_sems.at[j], device_id=peers[j], device_id_type=MESH)

        sends = [copy(j, _chip_of(peer), _chip_of(me)) for j, peer in enumerate(peers)]

        def start():
            mine.start()
            for cp in sends:
                cp.start()

        def finish():
            for j, peer in enumerate(peers):
                copy(j, _chip_of(me), _chip_of(peer)).wait_recv()
            for cp in sends:
                cp.wait_send()
            mine.wait()

        return start, finish


class Together:
    def __init__(self, *parts):
        self.parts = parts
        self.operands = [p.operand for p in parts]
        self.out_shapes = [p.out_shape for p in parts]
        self.scratch = [s for p in parts for s in p.scratch]

    def bind(self, x_refs, out_refs, sems):
        bound, at = [], 0
        for p, x_ref, out_ref in zip(self.parts, x_refs, out_refs):
            bound.append(p.bind(x_ref, out_ref, *sems[at:at + len(p.scratch)]))
            at += len(p.scratch)

        def start():
            for s, _ in bound:
                s()

        def finish():
            for _, f in bound:
                f()

        return start, finish


def exchange_alone(exchange, name):
    n = len(exchange.operands)

    def body(*refs):
        start, finish = exchange.bind(refs[:n], refs[n:2 * n], refs[2 * n:])
        start()
        finish()

    return pl.pallas_call(body, name=name, out_shape=exchange.out_shapes, in_specs=[ANY] * n, out_specs=[ANY] * n,
                          scratch_shapes=exchange.scratch)(*exchange.operands)


def _row_tile(rows):
    return max([t for t in range(16, min(rows, 1024) + 1, 16) if rows % t == 0] or [rows])


def pair_exchange(blocks, name):
    n = len(blocks)

    def body(*refs):
        x_refs, theirs_refs, (send_sems, recv_sems) = refs[:n], refs[n:2 * n], refs[2 * n:]
        me = _place()
        remote = [pltpu.make_async_remote_copy(
            src_ref=x_refs[t].at[2 * q + 1 - me[2]], dst_ref=theirs_refs[t].at[q], send_sem=send_sems.at[4 * t + q],
            recv_sem=recv_sems.at[4 * t + q], device_id=_flip(me, 1), device_id_type=MESH) for t in range(n) for q in range(4)]
        for cp in remote:
            cp.start()
        for cp in remote:
            cp.wait()

    return pl.pallas_call(
        body, name=name, out_shape=[jax.ShapeDtypeStruct((4,) + b.shape[1:], b.dtype) for b in blocks], in_specs=[ANY] * n,
        out_specs=[ANY] * n, scratch_shapes=[pltpu.SemaphoreType.DMA((4 * n,)), pltpu.SemaphoreType.DMA((4 * n,))])(*blocks)


def pair_add(blocks, theirs, name):
    n, rows, width = theirs.shape
    tile = _row_tile(rows)
    spec = pl.BlockSpec((None, tile, width), lambda q, i: (q, i, 0))
    south = pl.BlockSpec((None, None, tile, width), lambda q, i: (q, 0, i, 0))
    north = pl.BlockSpec((None, None, tile, width), lambda q, i: (q, 1, i, 0))

    def body(s_ref, n_ref, b_ref, o_ref):
        mine = jnp.where(lax.axis_index("c") == 0, s_ref[...], n_ref[...])
        o_ref[...] = (mine.astype(f32) + b_ref[...].astype(f32)).astype(o_ref.dtype)

    by_core = blocks.reshape(n, 2, rows, width)
    return pl.pallas_call(body, name=name, grid=(n, rows // tile), in_specs=[south, north, spec], out_specs=spec,
                          out_shape=jax.ShapeDtypeStruct(theirs.shape, theirs.dtype), compiler_params=_params(2))(by_core, by_core, theirs)


def sum_slots(blocks, name):
    n, rows, width = blocks.shape
    tile = _row_tile(rows)

    def body(x_ref, o_ref):
        total = x_ref[0].astype(f32)
        for s in range(1, n):
            total = total + x_ref[s].astype(f32)
        o_ref[...] = total

    return pl.pallas_call(
        body, name=name, grid=(rows // tile,), in_specs=[pl.BlockSpec((n, tile, width), lambda i: (0, i, 0))],
        out_specs=pl.BlockSpec((tile, width), lambda i: (i, 0)), out_shape=_sds((rows, width)), compiler_params=_params(1))(blocks)


def all_reduce_small(x, name):
    rows, width = x.shape

    def body(x_ref, o_ref, land, send_sems, recv_sems):
        me = _place()
        copies = []
        for k in range(1, N_DEV):
            peer = _flip(me, k)
            copies.append(pltpu.make_async_remote_copy(
                src_ref=x_ref, dst_ref=land.at[_slot(me)], send_sem=send_sems.at[k - 1], recv_sem=recv_sems.at[k - 1],
                device_id=peer, device_id_type=MESH))
        for cp in copies:
            cp.start()
        land[_slot(me)] = x_ref[...]
        for k in range(1, N_DEV):
            peer = _flip(me, k)
            pltpu.make_async_remote_copy(
                src_ref=x_ref, dst_ref=land.at[_slot(peer)], send_sem=send_sems.at[k - 1], recv_sem=recv_sems.at[k - 1],
                device_id=peer, device_id_type=MESH).wait_recv()
        total = land[0]
        for s in range(1, N_DEV):
            total = total + land[s]
        o_ref[...] = total
        for cp in copies:
            cp.wait_send()

    return pl.pallas_call(
        body, name=name, out_shape=_sds((rows, width)), in_specs=[pl.BlockSpec(memory_space=pltpu.VMEM)],
        out_specs=pl.BlockSpec(memory_space=pltpu.VMEM),
        scratch_shapes=[pltpu.VMEM((N_DEV, rows, width), f32), pltpu.SemaphoreType.DMA((7,)), pltpu.SemaphoreType.DMA((7,))],
    )(x)


def _pack_big(shards):
    packed = {name: shards[name].astype(bf16) for name in COL_SHARDED}
    packed["rows"] = jnp.concatenate([shards[name].astype(bf16) for name, _ in ROW_SHARDED], axis=1)
    return packed


def _unpack_gathered(gathered):
    full = {}
    for name, part in gathered.items():
        if name == "w_gate_up":
            full[name] = part
        elif name in COL_SHARDED:
            full[name] = part.transpose(1, 0, 2).reshape(D_MODEL, N_DEV * part.shape[2])
        else:
            at = 0
            for weight, rows in ROW_SHARDED:
                full[weight] = part[:, at:at + rows, :].reshape(N_DEV * rows, D_MODEL)
                at += rows
    return full


def _pack_grads(grads, group):
    packed = {}
    for name in group:
        if name == "w_gate_up":
            packed[name] = grads[name]
        elif name == "rows":
            packed[name] = jnp.concatenate([grads[weight].reshape(N_DEV, rows, D_MODEL) for weight, rows in ROW_SHARDED], axis=1)
        else:
            packed[name] = grads[name].reshape(D_MODEL, N_DEV, grads[name].shape[1] // N_DEV).transpose(1, 0, 2)
    return packed


def _unpack_shard(layers):
    out = {name: jnp.stack([layer[name] for layer in layers]) for name in COL_SHARDED}
    rows_pack, at = jnp.stack([layer["rows"] for layer in layers]), 0
    for weight, rows in ROW_SHARDED:
        out[weight] = rows_pack[:, at:at + rows, :]
        at += rows
    return out


def _rows_of(flat_len):
    return -(-flat_len // (8 * D_MODEL)) * 8


def _pack_small(parts):
    flat = jnp.concatenate([p.reshape(-1) for p in parts])
    rows = _rows_of(flat.shape[0])
    flat = jnp.pad(flat, (0, rows * D_MODEL - flat.shape[0]))
    return flat.reshape(rows, D_MODEL)


def _unpack_small(packed, like):
    flat, out, at = packed.reshape(-1), [], 0
    for p in like:
        out.append(flat[at:at + p.size].reshape(p.shape))
        at += p.size
    return out


def _rope_tables(positions):
    inv_freq = jnp.float32(ROPE_THETA) ** (-jnp.arange(0, ROPE_DIM, 2, dtype=f32) / ROPE_DIM)
    ang = positions.astype(f32)[:, None] * inv_freq
    cos, sin = jnp.cos(ang), jnp.sin(ang)
    rest = ATTN_HEAD_DIM - ROPE_DIM
    cos_h = jnp.concatenate([cos, cos, jnp.ones((cos.shape[0], rest), f32)], axis=1)
    sin_h = jnp.concatenate([-sin, sin, jnp.zeros((sin.shape[0], rest), f32)], axis=1)
    return jnp.tile(cos_h, (1, ATTN_HEADS)), jnp.tile(sin_h, (1, ATTN_HEADS))


HEAD_SMALL = ("norm_mix_pre", "conv_short", "conv_gdn", "gdn_a_log", "gdn_dt_bias")


def _layer_head(h, p, cos_t, sin_t):
    hn = rms_norm(h, p["norm_mix_pre"][None], "norm_mix_pre")
    aw, cw, gw = ATTN_WIDTH, CONV_WIDTH, GDN_WIDTH
    aq, ak, av, cb, cc, cx, gqkv, ab, gate = _split_cols(_linear(hn, p["w_in"], "w_in"),
                                                         (aw, aw, aw, cw, cw, cw, 3 * gw, 2 * GDN_HEADS, gw))
    ab = jnp.pad(ab, ((0, 0), (0, LANES - 2 * GDN_HEADS)))
    y_attn = dilated_attention(rope(aq, cos_t, sin_t, ATTN_HEAD_DIM ** -0.5, "rope_q"), rope(ak, cos_t, sin_t, 1.0, "rope_k"),
                               av, "attn")
    y_conv = short_conv(cb, cc, cx, p["conv_short"], "short_conv")
    qkv = gdn_pre(gqkv, p["conv_gdn"], "gdn_pre")
    pv = jnp.zeros((8, LANES), f32).at[0, :GDN_HEADS].set(p["gdn_a_log"]).at[1, :GDN_HEADS].set(p["gdn_dt_bias"])
    return (*_split_cols(qkv, (gw, gw, gw)), gate_beta(ab, pv, "gate_beta")), (gate, y_attn, y_conv)


MID_PARAMS = ("gdn_norm", "w_out", "norm_mix_post", "norm_xattn_pre", "w_xq", "norm_mem", "w_xkv", "w_xo", "norm_xattn_post",
              "norm_ffn_pre")


def _layer_mid(h, o, gate, y_attn, y_conv, p, mem):
    y_gdn = gdn_post(o, gate, p["gdn_norm"][None], "gdn_post")
    mix = _joined_linear((y_attn, y_conv, y_gdn), p["w_out"], "w_out")
    h, hn = add_norm_then_norm(h, mix, p["norm_mix_post"][None], p["norm_xattn_pre"][None], "norm_mix_xattn")
    qx = _linear(hn, p["w_xq"], "w_xq")
    kv = _linear(rms_norm(mem, p["norm_mem"][None], "norm_mem"), p["w_xkv"], "w_xkv")
    xa = _linear(cross_attention(qx, kv, "xattn"), p["w_xo"], "w_xo")
    return add_norm_then_norm(h, xa, p["norm_xattn_post"][None], p["norm_ffn_pre"][None], "norm_xattn_ffn")


def _pair_summed(grads, group, name):
    blocks = _pack_grads(grads, group)
    theirs = pair_exchange([blocks[n] for n in group], name + "_pair_exchange")
    return [pair_add(blocks[n], t, f"{name}_pair_add_{n}") for n, t in zip(group, theirs)]


def _forward_backward(x, packed, small, mem, cos_t, sin_t, target):
    def gathers(group, layer):
        return [Gather(packed[n][layer]) for n in group]

    h = x
    head_gathered = exchange_alone(Together(*gathers(HEAD_GROUP, 0)), "gather_first")
    saved = []
    for layer in range(DEPTH):
        at_layer = {n: t[layer] for n, t in small.items()}
        head_p = {**_unpack_gathered(dict(zip(HEAD_GROUP, head_gathered))), **{n: at_layer[n] for n in HEAD_SMALL}}
        (rule_in, rest), head_vjp = jax.vjp(lambda h, hp: _layer_head(h, hp, cos_t, sin_t), h, head_p)
        carried = gathers(TAIL_GROUP, layer) + (gathers(HEAD_GROUP, layer + 1) if layer + 1 < DEPTH else [])
        o, s_in, inv, *landed = delta_rule_fwd(*rule_in, "delta_rule_fwd", Together(*carried))
        head_gathered = landed[len(TAIL_GROUP):]
        tail_p = {**_unpack_gathered(dict(zip(TAIL_GROUP, landed))), **at_layer}
        mid_p = {n: tail_p[n] for n in MID_PARAMS}
        (h, hn), mid_vjp = jax.vjp(lambda h, o, rest, mp: _layer_mid(h, o, *rest, mp, mem), h, o, rest, mid_p)
        y, ffn_saved = ffn_forward(hn, tail_p["w_gate_up"], tail_p["w_down"], "ffn")
        h, last_vjp = jax.vjp(lambda h, y, w: add_norm(h, y, w[None], "norm_ffn_post"), h, y, tail_p["norm_ffn_post"])
        saved.append((head_vjp, mid_vjp, last_vjp, ffn_saved, rule_in, s_in, inv))

    loss, dh = jax.value_and_grad(lambda y: loss_rows(y, target, "loss"))(h)

    def summed(group, landed):
        return {n: sum_slots(t, "sum_grads_" + n) for n, t in zip(group, landed)}

    big_grads, small_grads, head_pending = [{} for _ in range(DEPTH)], [None] * DEPTH, []
    for layer in reversed(range(DEPTH)):
        head_vjp, mid_vjp, last_vjp, ffn_saved, rule_in, s_in, inv = saved[layer]
        dh, dy, d_norm_ffn_post = last_vjp(dh)
        dhn, d_gate_up, d_down, landed = ffn_backward(
            ffn_saved, dy, "ffn", Together(*[ChipExchange(t) for t in head_pending]) if head_pending else None)
        if head_pending:
            big_grads[layer + 1].update(summed(HEAD_GROUP, landed))
        dh_mid, do, d_rest, d_mid_p = mid_vjp((dh, dhn))
        d_tail_p = {**d_mid_p, "w_gate_up": d_gate_up, "w_down": d_down, "norm_ffn_post": d_norm_ffn_post}
        carried = Together(*[ChipExchange(t) for t in _pair_summed(d_tail_p, TAIL_GROUP, "tail")])
        *d_rule_in, = delta_rule_bwd(*rule_in, s_in, inv, do, "delta_rule_bwd", carried)
        big_grads[layer].update(summed(TAIL_GROUP, d_rule_in[4:]))
        dh_head, d_head_p = head_vjp((tuple(d_rule_in[:4]), d_rest))
        dh = dh_mid + dh_head
        small_grads[layer] = {n: t for n, t in {**d_head_p, **d_tail_p}.items() if n in small}
        head_pending = _pair_summed(d_head_p, HEAD_GROUP, "head")
    landed = exchange_alone(Together(*[ChipExchange(t) for t in head_pending]), "exchange_last")
    big_grads[0].update(summed(HEAD_GROUP, landed))
    return loss, dh, big_grads, small_grads


def kernel(x, mem, positions, norm_mix_pre, norm_mix_post, w_in, conv_short, conv_gdn, gdn_a_log, gdn_dt_bias, gdn_norm, w_out, norm_mem, norm_xattn_pre, norm_xattn_post, w_xq, w_xkv, w_xo, norm_ffn_pre, norm_ffn_post, w_gate_up, w_down, loss_target, m_norm_mix_pre, m_norm_mix_post, m_w_in, m_conv_short, m_conv_gdn, m_gdn_a_log, m_gdn_dt_bias, m_gdn_norm, m_w_out, m_norm_mem, m_norm_xattn_pre, m_norm_xattn_post, m_w_xq, m_w_xkv, m_w_xo, m_norm_ffn_pre, m_norm_ffn_post, m_w_gate_up, m_w_down, v_norm_mix_pre, v_norm_mix_post, v_w_in, v_conv_short, v_conv_gdn, v_gdn_a_log, v_gdn_dt_bias, v_gdn_norm, v_w_out, v_norm_mem, v_norm_xattn_pre, v_norm_xattn_post, v_w_xq, v_w_xkv, v_w_xo, v_norm_ffn_pre, v_norm_ffn_post, v_w_gate_up, v_w_down):
    given = dict(locals())
    weights = {n: given[n] for n in WEIGHTS}
    me = _slot(_place())

    def in_place(shard):
        full = jnp.zeros(shard.shape[:-1] + (shard.shape[-1] * N_DEV,), f32)
        return lax.dynamic_update_slice_in_dim(full, shard, me * shard.shape[-1], axis=shard.ndim - 1)

    placed = [in_place(conv_short), in_place(conv_gdn)]
    conv_short_full, conv_gdn_full = _unpack_small(all_reduce_small(_pack_small(placed), "gather_conv"), placed)
    small = {n: weights[n] for n in NORMS + ("gdn_a_log", "gdn_dt_bias", "gdn_norm")}
    small["conv_short"], small["conv_gdn"] = conv_short_full, conv_gdn_full

    cos_t, sin_t = _rope_tables(positions[0])
    loss, grad_x, big_layers, small_layers = _forward_backward(
        x[0], _pack_big(weights), small, mem[0], cos_t, sin_t, loss_target[0])
    grads = _unpack_shard(big_layers)

    names = sorted(small)
    parts = [jnp.stack([layer[n] for layer in small_layers]) for n in names] + [loss.reshape(1)]
    reduced = _unpack_small(all_reduce_small(_pack_small(parts), "reduce_small"), parts)
    loss = reduced[-1][0]
    for n, g in zip(names, reduced[:-1]):
        if n in ("conv_short", "conv_gdn"):
            width = weights[n].shape[-1]
            g = lax.dynamic_slice_in_dim(g, me * width, width, axis=g.ndim - 1)
        grads[n] = g

    delta, new_m, new_v = {}, {}, {}
    for n in WEIGHTS:
        delta[n], new_m[n], new_v[n] = adamw(weights[n], grads[n], given["m_" + n], given["v_" + n], "adamw_" + n)
    return (loss, grad_x[None], *[grads[n] for n in WEIGHTS], *[delta[n] for n in WEIGHTS],
            *[new_m[n] for n in WEIGHTS], *[new_v[n] for n in WEIGHTS])
```

```python
import functools

import jax
import jax.numpy as jnp
from jax import lax
from jax.experimental import pallas as pl
from jax.experimental.pallas import tpu as pltpu

f32 = jnp.float32
bf16 = jnp.bfloat16
MESH = pl.DeviceIdType.MESH

N_DEV = 8
DEPTH = 4
D_MODEL = 1024
EPS = 1e-6
ATTN_HEADS, ATTN_HEAD_DIM = 4, 64
ATTN_WIDTH = ATTN_HEADS * ATTN_HEAD_DIM
DILATIONS = (1, 4, 16)
QB = 128
ROPE_THETA = 500000.0
ROPE_DIM = ATTN_HEAD_DIM // 4
CONV_WIDTH = 256
GDN_HEADS, GDN_HEAD_DIM = 4, 128
GDN_WIDTH = GDN_HEADS * GDN_HEAD_DIM
GDN_CHUNK = 64
XATTN_HEADS, XATTN_HEAD_DIM = 4, 256
LANES = 128
ROW_TILE = 512
VMEM_LIMIT = 56 * 1024 * 1024

ADAM_LR, ADAM_B1, ADAM_B2, ADAM_EPS, ADAM_WD, ADAM_STEP = 0.001, 0.9, 0.999, 1e-08, 0.01, 10

COL_SHARDED = ("w_in", "w_xkv", "w_gate_up")
ROW_SHARDED = (("w_out", 128), ("w_xq", 128), ("w_xo", 128), ("w_down", 352))
HEAD_GROUP = ("w_in",)
TAIL_GROUP = ("w_gate_up", "w_xkv", "rows")
NORMS = ("norm_mix_pre", "norm_mix_post", "norm_mem", "norm_xattn_pre", "norm_xattn_post", "norm_ffn_pre", "norm_ffn_post")
WEIGHTS = ("norm_mix_pre", "norm_mix_post", "w_in", "conv_short", "conv_gdn", "gdn_a_log", "gdn_dt_bias", "gdn_norm", "w_out",
           "norm_mem", "norm_xattn_pre", "norm_xattn_post", "w_xq", "w_xkv", "w_xo", "norm_ffn_pre", "norm_ffn_post",
           "w_gate_up", "w_down")


def _params(n_grid):
    return pltpu.CompilerParams(dimension_semantics=("arbitrary",) * n_grid, vmem_limit_bytes=VMEM_LIMIT)


def _pick(n, cands):
    for c in cands:
        if n % c == 0:
            return c
    return n


MXU_FLOPS = 9.0e14
HBM_BYTES_PER_S = 2.5e12
VMEM_RMW_BYTES_PER_S = 7.0e12
STEP_S = 0.4e-6
MATMUL_VMEM = 44 * 1024 * 1024


def _tiles(m, n, k, sa, sb, so):
    def divisors(d):
        return sorted({d} | {d // s for s in range(1, d // LANES + 1) if d % s == 0 and (d // s) % LANES == 0}, reverse=True)

    best = None
    for tk in divisors(k):
        nk = k // tk
        for tm in divisors(m):
            for tn_ in divisors(n):
                per_step = tm * tk * sa + tk * tn_ * sb + tm * tn_ * so
                vmem = 2 * per_step + (tm * tn_ * 4 if nk > 1 else 0)
                vmem += (tm * tk * 2 if sa == 4 else 0) + (tk * tn_ * 2 if sb == 4 else 0) + tm * tn_ * 4
                if vmem > MATMUL_VMEM:
                    continue
                moved = m * k * sa * (1 if nk == 1 else n // tn_) + k * n * sb * (1 if nk == 1 and n == tn_ else m // tm) + m * n * so
                busy = 2 * m * n * k / MXU_FLOPS + (m * n * 8 * nk / VMEM_RMW_BYTES_PER_S if nk > 1 else 0)
                cost = max(moved / HBM_BYTES_PER_S, busy) + per_step / HBM_BYTES_PER_S + (m // tm) * (n // tn_) * nk * STEP_S
                if best is None or cost < best[0]:
                    best = (cost, tm, tn_, tk)
    return best[1:]


def _mm(a, b, ta, tb, out_dtype, name):
    m, k = (a.shape[1], a.shape[0]) if ta else a.shape
    n = b.shape[0] if tb else b.shape[1]
    tm, tn, tk = _tiles(m, n, k, a.dtype.itemsize, b.dtype.itemsize, jnp.dtype(out_dtype).itemsize)
    nk = k // tk
    a_spec = pl.BlockSpec((tk, tm), lambda i, j, kk: (kk, i)) if ta else pl.BlockSpec((tm, tk), lambda i, j, kk: (i, kk))
    b_spec = pl.BlockSpec((tn, tk), lambda i, j, kk: (j, kk)) if tb else pl.BlockSpec((tk, tn), lambda i, j, kk: (kk, j))
    dims = (((0 if ta else 1,), (1 if tb else 0,)), ((), ()))

    def body(a_ref, b_ref, o_ref, *acc):
        kk = pl.program_id(2)
        p = lax.dot_general(a_ref[...].astype(bf16), b_ref[...].astype(bf16), dims, preferred_element_type=f32)
        if nk == 1:
            o_ref[...] = p.astype(o_ref.dtype)
            return
        acc_ref, = acc

        @pl.when(kk == 0)
        def _():
            acc_ref[...] = p

        @pl.when(kk > 0)
        def _():
            acc_ref[...] += p

        @pl.when(kk == nk - 1)
        def _():
            o_ref[...] = acc_ref[...].astype(o_ref.dtype)

    return pl.pallas_call(
        body, name=name, grid=(m // tm, n // tn, nk), in_specs=[a_spec, b_spec],
        out_specs=pl.BlockSpec((tm, tn), lambda i, j, kk: (i, j)), out_shape=jax.ShapeDtypeStruct((m, n), out_dtype),
        scratch_shapes=[pltpu.VMEM((tm, tn), f32)] if nk > 1 else [], compiler_params=_params(3))(a, b)


def _linear(x, w, name):
    @jax.custom_vjp
    def lin(x, w):
        return _mm(x, w, False, False, f32, name + "_y")

    def lin_f(x, w):
        return _mm(x, w, False, False, f32, name + "_y"), (x, w)

    def lin_b(res, dy):
        x, w = res
        return _mm(dy, w, False, True, f32, name + "_dx"), _mm(x, dy, True, False, bf16, name + "_dw")

    lin.defvjp(lin_f, lin_b)
    return lin(x, w)


def _bdot(a, b, form):
    dims = {"nn": ((1,), (0,)), "nt": ((1,), (1,)), "tn": ((0,), (0,))}[form]
    return lax.dot_general(a.astype(bf16), b.astype(bf16), (dims, ((), ())), preferred_element_type=f32)


def ffn_forward(hn, w_gate_up, w_down, name):
    s, k = hn.shape
    n_blocks, _, width = w_gate_up.shape
    half = n_blocks // 2
    tm = 1024
    blocked = jax.ShapeDtypeStruct((half, s, width), bf16)

    def act_body(x_ref, wg_ref, wu_ref, gate_ref, up_ref, act_ref):
        x = x_ref[...]
        gate, up = _bdot(x, wg_ref[...], "nn"), _bdot(x, wu_ref[...], "nn")
        gate_ref[...], up_ref[...] = gate.astype(bf16), up.astype(bf16)
        act_ref[...] = (jax.nn.silu(gate) * up).astype(bf16)

    tile = pl.BlockSpec((None, tm, width), lambda i, d: (d, i, 0))
    gate, up, act = pl.pallas_call(
        act_body, name=name + "_act", grid=(s // tm, half),
        in_specs=[pl.BlockSpec((tm, k), lambda i, d: (i, 0)), pl.BlockSpec((None, k, width), lambda i, d: (d, 0, 0)),
                  pl.BlockSpec((None, k, width), lambda i, d: (d + half, 0, 0))],
        out_specs=[tile] * 3, out_shape=[blocked] * 3, compiler_params=_params(2))(hn, w_gate_up, w_gate_up)

    n = w_down.shape[1]
    tn = 512

    def y_body(act_ref, w_ref, y_ref):
        y_ref[...] = sum(_bdot(act_ref[d], w_ref[d * width:(d + 1) * width, :], "nn") for d in range(half))

    y = pl.pallas_call(
        y_body, name=name + "_y", grid=(s // tm, n // tn),
        in_specs=[pl.BlockSpec((half, tm, width), lambda i, j: (0, i, 0)), pl.BlockSpec((half * width, tn), lambda i, j: (0, j))],
        out_specs=pl.BlockSpec((tm, tn), lambda i, j: (i, j)), out_shape=_sds((s, n)), compiler_params=_params(2))(act, w_down)
    return y, (hn, w_gate_up, w_down, gate, up, act)


def ffn_backward(saved, dy, name, exchange=None):
    hn, w_gate_up, w_down, gate, up, act = saved
    s, k = hn.shape
    n_blocks, _, width = w_gate_up.shape
    half = n_blocks // 2
    n = w_down.shape[1]
    tm = 1024
    blocked = jax.ShapeDtypeStruct((half, s, width), bf16)
    carried = len(exchange.operands) if exchange else 0
    steps = (s // tm, half)

    def dact_body(dy_ref, w_ref, gate_ref, up_ref, *refs):
        x_refs, refs = refs[:carried], refs[carried:]
        (dgate_ref, dup_ref), refs = refs[:2], refs[2:]
        if exchange:
            at = pl.program_id(0) * steps[1] + pl.program_id(1)
            start, wait = exchange.bind(x_refs, refs[:carried], refs[carried:])
            pl.when(at == 0)(start)
        d_act = _bdot(dy_ref[...], w_ref[...], "nt")
        g, u = gate_ref[...].astype(f32), up_ref[...].astype(f32)
        sig = jax.nn.sigmoid(g)
        dgate_ref[...] = (d_act * u * sig * (1.0 + g * (1.0 - sig))).astype(bf16)
        dup_ref[...] = (d_act * g * sig).astype(bf16)
        if exchange:
            pl.when(at == steps[0] * steps[1] - 1)(wait)

    tile = pl.BlockSpec((None, tm, width), lambda i, d: (d, i, 0))
    d_gate, d_up, *landed = pl.pallas_call(
        dact_body, name=name + "_dact", grid=steps,
        in_specs=[pl.BlockSpec((tm, n), lambda i, d: (i, 0)), pl.BlockSpec((width, n), lambda i, d: (d, 0)), tile, tile] + [ANY] * carried,
        out_specs=[tile, tile] + [ANY] * carried, out_shape=[blocked, blocked] + (exchange.out_shapes if exchange else []),
        scratch_shapes=exchange.scratch if exchange else [],
        compiler_params=_params(2))(dy, w_down, gate, up, *(exchange.operands if exchange else []))

    def dx_body(dg_ref, du_ref, w_ref, dx_ref):
        dx_ref[...] = sum(_bdot(dg_ref[d], w_ref[d], "nt") + _bdot(du_ref[d], w_ref[d + half], "nt") for d in range(half))

    tx = 512
    rows = pl.BlockSpec((half, tx, width), lambda i: (0, i, 0))
    dx = pl.pallas_call(
        dx_body, name=name + "_dx", grid=(s // tx,), in_specs=[rows, rows, _whole(w_gate_up.shape)],
        out_specs=pl.BlockSpec((tx, k), lambda i: (i, 0)), out_shape=_sds((s, k)), compiler_params=_params(1))(d_gate, d_up, w_gate_up)

    def dw1_body(x_ref, dg_ref, du_ref, dw_ref):
        d_block = jnp.where(pl.program_id(0) < half, dg_ref[...], du_ref[...])
        dw_ref[...] = _bdot(x_ref[...], d_block, "tn").astype(bf16)

    d_w_gate_up = pl.pallas_call(
        dw1_body, name=name + "_dw1", grid=(n_blocks,),
        in_specs=[_whole((s, k)), pl.BlockSpec((None, s, width), lambda b: (jnp.minimum(b, half - 1), 0, 0)),
                  pl.BlockSpec((None, s, width), lambda b: (jnp.maximum(b - half, 0), 0, 0))],
        out_specs=pl.BlockSpec((None, k, width), lambda b: (b, 0, 0)), out_shape=jax.ShapeDtypeStruct(w_gate_up.shape, bf16),
        compiler_params=_params(1))(hn, d_gate, d_up)

    tn = 512

    def dw2_body(act_ref, dy_ref, dw_ref):
        dw_ref[...] = _bdot(act_ref[...], dy_ref[...], "tn").astype(bf16)

    d_w_down = pl.pallas_call(
        dw2_body, name=name + "_dw2", grid=(half, n // tn),
        in_specs=[pl.BlockSpec((None, s, width), lambda d, j: (d, 0, 0)), pl.BlockSpec((s, tn), lambda d, j: (0, j))],
        out_specs=pl.BlockSpec((width, tn), lambda d, j: (d, j)), out_shape=jax.ShapeDtypeStruct(w_down.shape, bf16),
        compiler_params=_params(2))(act, dy)
    return dx, d_w_gate_up, d_w_down, landed


def _joined_linear(parts, w, name):
    s, n = parts[0].shape[0], w.shape[1]
    widths = [p.shape[1] for p in parts]
    edges = [sum(widths[:i]) for i in range(len(widths) + 1)]
    spans = list(zip(edges[:-1], edges[1:]))
    tm, tn = 512, 512

    def forward(*args):
        *xs, w = args

        def y_body(*refs):
            *x_refs, w_ref, y_ref = refs
            y_ref[...] = sum(_bdot(x_ref[...], w_ref[a:b, :], "nn") for x_ref, (a, b) in zip(x_refs, spans))

        y = pl.pallas_call(
            y_body, name=name + "_y", grid=(s // tm, n // tn),
            in_specs=[pl.BlockSpec((tm, k), lambda i, j: (i, 0)) for k in widths] + [pl.BlockSpec((edges[-1], tn), lambda i, j: (0, j))],
            out_specs=pl.BlockSpec((tm, tn), lambda i, j: (i, j)), out_shape=_sds((s, n)), compiler_params=_params(2))(*xs, w)
        return y, args

    def backward(args, dy):
        *xs, w = args

        def dx_body(dy_ref, w_ref, *dx_refs):
            d_all = _bdot(dy_ref[...], w_ref[...], "nt")
            for dx_ref, (a, b) in zip(dx_refs, spans):
                dx_ref[...] = d_all[:, a:b]

        dxs = pl.pallas_call(
            dx_body, name=name + "_dx", grid=(s // tm,), in_specs=[pl.BlockSpec((tm, n), lambda i: (i, 0)), _whole(w.shape)],
            out_specs=[pl.BlockSpec((tm, k), lambda i: (i, 0)) for k in widths], out_shape=[_sds((s, k)) for k in widths],
            compiler_params=_params(1))(dy, w)

        def dw_body(*refs):
            *x_refs, dy_ref, dw_ref = refs
            dy_tile = dy_ref[...]
            dw_ref[...] = jnp.concatenate([_bdot(x_ref[...], dy_tile, "tn") for x_ref in x_refs], axis=0).astype(bf16)

        dw = pl.pallas_call(
            dw_body, name=name + "_dw", grid=(n // tn,),
            in_specs=[_whole((s, k)) for k in widths] + [pl.BlockSpec((s, tn), lambda j: (0, j))],
            out_specs=pl.BlockSpec((edges[-1], tn), lambda j: (0, j)), out_shape=jax.ShapeDtypeStruct(w.shape, bf16),
            compiler_params=_params(1))(*xs, dy)
        return (*dxs, dw)

    @jax.custom_vjp
    def op(*args):
        return forward(*args)[0]

    op.defvjp(forward, backward)
    return op(*parts, w)


def _split_cols(x, widths):
    edges = [sum(widths[:i]) for i in range(len(widths) + 1)]

    def cut(x):
        return tuple(x[:, a:b] for a, b in zip(edges[:-1], edges[1:]))

    @jax.custom_vjp
    def split(x):
        return cut(x)

    split.defvjp(lambda x: (cut(x), None), lambda _, cts: (jnp.concatenate(cts, axis=1),))
    return split(x)


def _block_op(name, f, grid, in_specs, out_defs, arrays, diff, acc=None, gdefs=None):
    acc, gdefs = acc or {}, gdefs or {}
    n_in, n_out, n_grid = len(in_specs), len(out_defs), len(grid)

    def fwd_call(*xs):
        def body(*refs):
            outs = f(*[r[...] for r in refs[:n_in]])
            for r, o in zip(refs[n_in:], outs):
                r[...] = o.astype(r.dtype)

        return pl.pallas_call(
            body, name=name + "_fwd", grid=grid, in_specs=in_specs, out_specs=[d[1] for d in out_defs],
            out_shape=[d[0] for d in out_defs], compiler_params=_params(n_grid))(*xs)

    def bwd_call(*xs_and_cts):
        def body(*refs):
            xs = [r[...] for r in refs[:n_in]]
            cts = tuple(r[...] for r in refs[n_in:n_in + n_out])

            def of_diff(*dx):
                full = list(xs)
                for i, v in zip(diff, dx):
                    full[i] = v
                return tuple(f(*full))

            _, vjp = jax.vjp(of_diff, *[xs[i] for i in diff])
            grads = vjp(cts)
            for i, g, r in zip(diff, grads, refs[n_in + n_out:]):
                if i in acc:
                    first = functools.reduce(jnp.logical_and, [pl.program_id(a) == 0 for a in acc[i]])

                    @pl.when(first)
                    def _(r=r):
                        r[...] = jnp.zeros_like(r)

                    r[...] += g.astype(r.dtype)
                else:
                    r[...] = g.astype(r.dtype)

        g_defs = [gdefs.get(i, (jax.ShapeDtypeStruct(arrays[i].shape, f32), in_specs[i])) for i in diff]
        return pl.pallas_call(
            body, name=name + "_bwd", grid=grid, in_specs=list(in_specs) + [d[1] for d in out_defs],
            out_specs=[d[1] for d in g_defs], out_shape=[d[0] for d in g_defs], compiler_params=_params(n_grid))(*xs_and_cts)

    return fwd_call, bwd_call


def _simple_op(name, f, grid, in_specs, out_defs, arrays, diff, acc=None):
    fwd_call, bwd_call = _block_op(name, f, grid, in_specs, out_defs, arrays, diff, acc)

    @jax.custom_vjp
    def op(*xs):
        return tuple(fwd_call(*xs))

    def op_f(*xs):
        return tuple(fwd_call(*xs)), xs

    def op_b(xs, cts):
        grads = bwd_call(*xs, *cts)
        out = [jnp.zeros_like(x) for x in xs]
        for i, g in zip(diff, grads):
            out[i] = g
        return tuple(out)

    op.defvjp(op_f, op_b)
    return op(*arrays)


def _rows(width, tile=ROW_TILE):
    return pl.BlockSpec((tile, width), lambda i: (i, 0))


def _whole(shape):
    return pl.BlockSpec(shape, lambda *_: (0,) * len(shape))


def _sds(shape):
    return jax.ShapeDtypeStruct(shape, f32)


def _rms(x, w):
    return x * lax.rsqrt(jnp.mean(x * x, axis=-1, keepdims=True) + EPS) * w


def rms_norm(x, w, name):
    r, d = x.shape
    tile = min(ROW_TILE, r)
    return _simple_op(name, lambda x, w: (_rms(x, w),), (r // tile,), [_rows(d, tile), _whole((1, d))],
                      [(_sds((r, d)), _rows(d, tile))], (x, w), (0, 1), {1: (0,)})[0]


def add_norm(h, y, w, name):
    r, d = h.shape
    return _simple_op(name, lambda h, y, w: (h + _rms(y, w),), (r // ROW_TILE,), [_rows(d), _rows(d), _whole((1, d))],
                      [(_sds((r, d)), _rows(d))], (h, y, w), (0, 1, 2), {2: (0,)})[0]


def add_norm_then_norm(h, y, w_post, w_pre, name):
    r, d = h.shape

    def f(h, y, w_post, w_pre):
        h_new = h + _rms(y, w_post)
        return h_new, _rms(h_new, w_pre)

    return _simple_op(name, f, (r // ROW_TILE,), [_rows(d), _rows(d), _whole((1, d)), _whole((1, d))],
                      [(_sds((r, d)), _rows(d))] * 2, (h, y, w_post, w_pre), (0, 1, 2, 3), {2: (0,), 3: (0,)})


def _swap8(x):
    def raw(x):
        lane = lax.broadcasted_iota(jnp.int32, x.shape, 1) % ATTN_HEAD_DIM
        half = ROPE_DIM // 2
        up = pltpu.roll(x, x.shape[1] - half, axis=1)
        down = pltpu.roll(x, half, axis=1)
        return jnp.where(lane < half, up, jnp.where(lane < ROPE_DIM, down, 0.0))

    @jax.custom_vjp
    def swap(x):
        return raw(x)

    swap.defvjp(lambda x: (raw(x), None), lambda _, g: (raw(g),))
    return swap(x)


def rope(x, cos_t, sin_t, scale, name):
    r, d = x.shape
    return _simple_op(name, lambda x, c, s: ((x * c + _swap8(x) * s) * scale,), (r // ROW_TILE,), [_rows(d)] * 3,
                      [(_sds((r, d)), _rows(d))], (x, cos_t, sin_t), (0,))[0]


def _shift_rows(x, k):
    n = x.shape[0]

    def down(x):
        row = lax.broadcasted_iota(jnp.int32, x.shape, 0)
        return jnp.where(row >= k, pltpu.roll(x, k, axis=0), 0.0)

    def up(x):
        row = lax.broadcasted_iota(jnp.int32, x.shape, 0)
        return jnp.where(row < n - k, pltpu.roll(x, n - k, axis=0), 0.0)

    @jax.custom_vjp
    def shift(x):
        return down(x)

    shift.defvjp(lambda x: (down(x), None), lambda _, g: (up(g),))
    return shift(x)


def _causal_conv(x, w):
    taps = w.shape[0]
    y = x * w[taps - 1:taps, :]
    for j in range(taps - 1):
        y = y + _shift_rows(x, taps - 1 - j) * w[j:j + 1, :]
    return y


def _cols(rows, at=0):
    return pl.BlockSpec((rows, LANES), lambda j: (0, at + j))


def short_conv(cb, cc, cx, w, name):
    s, c = cb.shape
    taps = w.shape[0]
    return _simple_op(name, lambda b, c_, x, w: (b * _causal_conv(c_ * x, w),), (c // LANES,),
                      [_cols(s)] * 3 + [_cols(taps)], [(_sds((s, c)), _cols(s))], (cb, cc, cx, w), (0, 1, 2, 3))[0]


def gdn_pre(qkv, w, name):
    s, c = qkv.shape
    taps = w.shape[0]

    def f(x, w):
        j = pl.program_id(0)
        y = jax.nn.silu(_causal_conv(x, w))
        normed = y * lax.rsqrt(jnp.sum(y * y, axis=-1, keepdims=True) + EPS)
        scale = jnp.where(j < GDN_HEADS, GDN_HEAD_DIM ** -0.5, 1.0).astype(f32)
        return (jnp.where(j < 2 * GDN_HEADS, normed * scale, y),)

    return _simple_op(name, f, (c // LANES,), [_cols(s), _cols(taps)], [(_sds((s, c)), _cols(s))], (qkv, w), (0, 1))[0]


def gate_beta(ab, pv, name):
    s = ab.shape[0]

    def f(ab, pv):
        lane = lax.broadcasted_iota(jnp.int32, ab.shape, 1)
        g = -jnp.exp(pv[0:1, :]) * jax.nn.softplus(ab + pv[1:2, :])
        return (jnp.where(lane < GDN_HEADS, g, jnp.where(lane < 2 * GDN_HEADS, jax.nn.sigmoid(ab), 0.0)),)

    return _simple_op(name, f, (s // ROW_TILE,), [_rows(LANES), _whole((8, LANES))], [(_sds((s, LANES)), _rows(LANES))],
                      (ab, pv), (0, 1), {1: (0,)})[0]


def gdn_post(o, gate, w, name):
    s, c = o.shape

    def f(o, g, w):
        heads = [slice(hd * LANES, (hd + 1) * LANES) for hd in range(c // LANES)]
        return (jnp.concatenate([_rms(o[:, hd], w) * jax.nn.silu(g[:, hd]) for hd in heads], axis=1),)

    return _simple_op(name, f, (s // ROW_TILE,), [_rows(c), _rows(c), _whole((1, LANES))], [(_sds((s, c)), _rows(c))],
                      (o, gate, w), (0, 1, 2), {2: (0,)})[0]


def attn_merge(outs, lses, name):
    s, c = outs[0].shape

    def f(o1, o2, o3, l1, l2, l3):
        m = lax.stop_gradient(jnp.maximum(jnp.maximum(l1, l2), l3))
        e1, e2, e3 = jnp.exp(l1 - m), jnp.exp(l2 - m), jnp.exp(l3 - m)
        return ((e1 * o1 + e2 * o2 + e3 * o3) / (e1 + e2 + e3),)

    return _simple_op(name, f, (s // ROW_TILE,), [_rows(c)] * 6, [(_sds((s, c)), _rows(c))], (*outs, *lses), tuple(range(6)))[0]


def loss_rows(y, target, name):
    s, d = y.shape
    nt = s // ROW_TILE

    def f(y, t):
        e = y - t
        part = 0.5 * jnp.sum(jnp.mean(e * e, axis=-1, keepdims=True), axis=0, keepdims=True)
        return (jnp.broadcast_to(part * (1.0 / (8 * LANES)), (8, LANES)),)

    out = _simple_op(name, f, (nt,), [_rows(d)] * 2, [(_sds((nt * 8, LANES)), pl.BlockSpec((8, LANES), lambda i: (i, 0)))],
                     (y, target), (0,))[0]
    return jnp.sum(out)


def _mxu(a, b, form):
    dims = {"nn": ((1,), (0,)), "nt": ((1,), (1,)), "tn": ((0,), (0,))}

    def raw(a, b, form):
        return lax.dot_general(a.astype(bf16), b.astype(bf16), (dims[form], ((), ())), preferred_element_type=f32)

    @jax.custom_vjp
    def prod(a, b):
        return raw(a, b, form)

    def prod_b(res, ct):
        a, b = res
        if form == "nn":
            return raw(ct, b, "nt"), raw(a, ct, "tn")
        if form == "nt":
            return raw(ct, b, "nn"), raw(ct, a, "tn")
        return raw(b, ct, "nt"), raw(a, ct, "nn")

    prod.defvjp(lambda a, b: (raw(a, b, form), (a, b)), prod_b)
    return prod(a, b)


def _masked_heads_attention(q, keys, values, seen):
    dh = ATTN_HEAD_DIM
    outs, lses = [], []
    for hd in range(q.shape[1] // dh):
        at = slice(hd * dh, (hd + 1) * dh)
        sc = jnp.where(seen, _mxu(q[:, at], keys[:, at], "nt"), -jnp.inf)
        m = lax.stop_gradient(jnp.max(sc, axis=-1, keepdims=True))
        p = jnp.exp(sc - m)
        l = jnp.sum(p, axis=-1, keepdims=True)
        outs.append(_mxu(p / l, values[:, at], "nn"))
        lses.append(jnp.broadcast_to(m + jnp.log(l), (q.shape[0], dh)))
    return jnp.concatenate(outs, axis=1), jnp.concatenate(lses, axis=1)


def band_attention(q, k, v, nb, name):
    r, qb, width = q.shape

    def f(q, kp, kc, vp, vc):
        has_prev = (pl.program_id(0) % nb) > 0
        i = lax.broadcasted_iota(jnp.int32, (qb, 2 * qb), 0)
        j = lax.broadcasted_iota(jnp.int32, (qb, 2 * qb), 1)
        seen = jnp.logical_or(jnp.logical_and(jnp.logical_and(j < qb, j >= i), has_prev), jnp.logical_and(j >= qb, j - qb <= i))
        return _masked_heads_attention(q, jnp.concatenate([kp, kc], axis=0), jnp.concatenate([vp, vc], axis=0), seen)

    blk = (None, qb, width)
    cur = pl.BlockSpec(blk, lambda b: (b, 0, 0))
    prev = pl.BlockSpec(blk, lambda b: (jnp.maximum(b - 1, 0), 0, 0))
    shape = _sds((r, qb, width))
    fwd_call, bwd_call = _block_op(name, f, (r,), [cur, prev, cur, prev, cur], [(shape, cur), (shape, cur)],
                                   (q, k, k, v, v), (0, 1, 2, 3, 4), gdefs={1: (shape, cur), 3: (shape, cur)})

    def to_prev(g):
        return jnp.concatenate([g[1:], jnp.zeros_like(g[:1])], axis=0)

    @jax.custom_vjp
    def op(q, k, v):
        return tuple(fwd_call(q, k, k, v, v))

    def op_b(res, cts):
        q, k, v = res
        dq, dkp, dkc, dvp, dvc = bwd_call(q, k, k, v, v, *cts)
        return dq, dkc + to_prev(dkp), dvc + to_prev(dvp)

    op.defvjp(lambda q, k, v: (tuple(fwd_call(q, k, k, v, v)), (q, k, v)), op_b)
    return op(q, k, v)


def dilated_attention(q, k, v, name):
    s = q.shape[0]
    outs, lses = [], []
    for d in DILATIONS:
        length = s // d
        nb = length // QB
        def to_residue(t):
            return t.reshape(length, d, ATTN_WIDTH).transpose(1, 0, 2).reshape(d * nb, QB, ATTN_WIDTH)

        def from_residue(t):
            return t.reshape(d, length, ATTN_WIDTH).transpose(1, 0, 2).reshape(s, ATTN_WIDTH)

        o, lse = band_attention(to_residue(q), to_residue(k), to_residue(v), nb, f"{name}_d{d}")
        outs.append(from_residue(o))
        lses.append(from_residue(lse))
    return attn_merge(outs, lses, name + "_merge")


def cross_attention(q, kv, name):
    s = q.shape[0]
    m = kv.shape[0]
    width = XATTN_HEADS * XATTN_HEAD_DIM
    tq = 512

    def f(q, k, v):
        sc = _mxu(q, k, "nt") * (XATTN_HEAD_DIM ** -0.5)
        mx = lax.stop_gradient(jnp.max(sc, axis=-1, keepdims=True))
        p = jnp.exp(sc - mx)
        return (_mxu(p / jnp.sum(p, axis=-1, keepdims=True), v, "nn"),)

    q_spec = pl.BlockSpec((tq, XATTN_HEAD_DIM), lambda a, i: (i, a))
    k_spec = pl.BlockSpec((m, XATTN_HEAD_DIM), lambda a, i: (0, a))
    v_spec = pl.BlockSpec((m, XATTN_HEAD_DIM), lambda a, i: (0, a + XATTN_HEADS))
    half = _sds((m, width))
    fwd_call, bwd_call = _block_op(name, f, (XATTN_HEADS, s // tq), [q_spec, k_spec, v_spec], [(_sds((s, width)), q_spec)],
                                   (q, kv, kv), (0, 1, 2), acc={1: (1,), 2: (1,)}, gdefs={1: (half, k_spec), 2: (half, k_spec)})

    @jax.custom_vjp
    def op(q, kv):
        return fwd_call(q, kv, kv)[0]

    def op_b(res, ct):
        q, kv = res
        dq, dk, dv = bwd_call(q, kv, kv, ct)
        return dq, jnp.concatenate([dk, dv], axis=1)

    op.defvjp(lambda q, kv: (fwd_call(q, kv, kv)[0], (q, kv)), op_b)
    return op(q, kv)


def _hi(a, b, form="nn"):
    dims = {"nn": ((1,), (0,)), "nt": ((1,), (1,)), "tn": ((0,), (0,))}[form]
    return lax.dot_general(a, b, (dims, ((), ())), precision=lax.Precision.HIGH, preferred_element_type=f32)


def _running_sum(g):
    def raw(x, form):
        c = x.shape[0]
        tri = (lax.broadcasted_iota(jnp.int32, (c, c), 0) >= lax.broadcasted_iota(jnp.int32, (c, c), 1)).astype(bf16)
        hi = x.astype(bf16)
        rest = x - hi.astype(f32)
        mid = rest.astype(bf16)
        low = (rest - mid.astype(f32)).astype(bf16)
        dims = (((1,) if form == "nn" else (0,), (0,)), ((), ()))
        return sum(lax.dot_general(tri, part, dims, preferred_element_type=f32) for part in (hi, mid, low))

    @jax.custom_vjp
    def run(x):
        return raw(x, "nn")

    run.defvjp(lambda x: (raw(x, "nn"), None), lambda _, ct: (raw(ct, "tn"),))
    return run(g)


def _unit_lower_inverse(a):
    c = a.shape[0]
    eye = (lax.broadcasted_iota(jnp.int32, (c, c), 0) == lax.broadcasted_iota(jnp.int32, (c, c), 1)).astype(f32)
    inv, power = eye - a, -a
    for _ in range(c.bit_length() - 2):
        power = _hi(power, power)
        inv = inv + _hi(inv, power)
    return inv


def _known_inverse(a, t):
    @jax.custom_vjp
    def inv(a, t):
        return t

    def inv_b(t, ct):
        return -_hi(_hi(t, ct, "tn"), t, "nt"), jnp.zeros_like(t)

    inv.defvjp(lambda a, t: (t, t), inv_b)
    return inv(a, t)


def _chunk_prepare(q, k, v, g, beta, known_inv=None):
    c = q.shape[0]
    i = lax.broadcasted_iota(jnp.int32, (c, c), 0)
    j = lax.broadcasted_iota(jnp.int32, (c, c), 1)
    causal, strict = i >= j, i > j
    dec = _running_sum(g)
    dec_i = dec[:, :c]
    rel = jnp.exp(jnp.where(causal, dec_i - dec_i.T, -jnp.inf))
    k_beta = k * beta
    on_k = _mxu(jnp.concatenate([k_beta, q], axis=0), k, "nt")
    a = jnp.where(strict, on_k[:c] * rel, 0.0)
    attn = jnp.where(causal, on_k[c:] * rel, 0.0)
    inv = _unit_lower_inverse(a) if known_inv is None else _known_inverse(a, known_inv)
    e_dec = jnp.exp(dec)
    solved = _hi(inv, jnp.concatenate([v * beta, k_beta * e_dec], axis=1))
    u, w = solved[:, :v.shape[1]], solved[:, v.shape[1]:]
    total = jnp.sum(g, axis=0, keepdims=True)
    return inv, (u, w, attn, q * e_dec, k * jnp.exp(total - dec), total)


def _chunk_advance(u, w, attn, q_dec, k_dec, total, s0):
    c = u.shape[0]
    on_state = _mxu(jnp.concatenate([w, q_dec], axis=0), s0, "nn")
    v_new = u - on_state[:c]
    return on_state[c:] + _mxu(attn, v_new, "nn"), s0 * jnp.exp(total) + _mxu(k_dec, v_new, "tn")


def _delta_chunk(q, k, v, g, beta, s0, known_inv=None):
    inv, ready = _chunk_prepare(q, k, v, g, beta, known_inv)
    return (*_chunk_advance(*ready, s0), inv)


def _delta_rule_call(name, walk, steps, in_specs, out_specs, out_shape, operands, exchange, scratch=()):
    n_in, n_out = len(in_specs), len(out_specs)
    carried = len(exchange.operands) if exchange else 0

    def body(*refs):
        ins, refs = refs[:n_in], refs[n_in:]
        x_refs, refs = refs[:carried], refs[carried:]
        outs, refs = refs[:n_out], refs[n_out:]
        land_refs, (state, *refs) = refs[:carried], refs[carried:]
        own, sems = refs[:len(scratch)], refs[len(scratch):]
        step = pl.program_id(0)
        if exchange:
            start, early, finish = exchange.bind_with_early(x_refs, land_refs, sems, 2)
            pl.when(step == 0)(start)

        @pl.when(step == 0)
        def _():
            for r in (state, *own):
                r[...] = jnp.zeros_like(r)

        walk(ins, outs, state, *own)
        if exchange:
            pl.when(step == (3 * steps) // 4)(early)
            pl.when(step == steps - 1)(finish)

    return pl.pallas_call(
        body, name=name, grid=(steps,), in_specs=list(in_specs) + [ANY] * carried, out_specs=list(out_specs) + [ANY] * carried,
        out_shape=list(out_shape) + (exchange.out_shapes if exchange else []),
        scratch_shapes=[pltpu.VMEM((GDN_HEAD_DIM, GDN_WIDTH), f32), *scratch] + (exchange.scratch if exchange else []),
        compiler_params=_params(1))(*operands, *(exchange.operands if exchange else []))


def _delta_heads():
    heads = [slice(hd * GDN_HEAD_DIM, (hd + 1) * GDN_HEAD_DIM) for hd in range(GDN_HEADS)]
    inv_at = [slice(hd * GDN_CHUNK, (hd + 1) * GDN_CHUNK) for hd in range(GDN_HEADS)]
    return heads, inv_at


def _head_chunk(q, k, v, gates, s0, head, known_inv=None):
    g = jnp.broadcast_to(gates[:, head:head + 1], q.shape)
    beta = jnp.broadcast_to(gates[:, GDN_HEADS + head:GDN_HEADS + head + 1], q.shape)
    return _delta_chunk(q, k, v, g, beta, s0, known_inv)


def delta_rule_fwd(q, k, v, gates, name, exchange=None):
    s, width = q.shape
    c, dk = GDN_CHUNK, GDN_HEAD_DIM
    n = s // c
    heads, inv_at = _delta_heads()

    def walk(ins, outs, state, u_s, w_s, q_s, k_s, attn_s, total_s):
        q_ref, k_ref, v_ref, gates_ref = ins
        o_ref, s_in_ref, inv_ref = outs
        s_in_ref[...] = state[...]
        gates = gates_ref[...]
        ready = [[r[:, hd] for r in (u_s, w_s)] + [attn_s[:, at]] + [r[:, hd] for r in (q_s, k_s)] + [total_s[0:1, hd], state[:, hd]]
                 for hd, at in zip(heads, inv_at)]
        fresh = []
        for i, hd in enumerate(heads):
            g = jnp.broadcast_to(gates[:, i:i + 1], (c, dk))
            beta = jnp.broadcast_to(gates[:, GDN_HEADS + i:GDN_HEADS + i + 1], (c, dk))
            fresh.append(_chunk_prepare(q_ref[:, hd], k_ref[:, hd], v_ref[:, hd], g, beta))
        advanced = [_chunk_advance(*x) for x in ready]
        for hd, at, (o, s1), (inv, (u, w, attn, q_dec, k_dec, total)) in zip(heads, inv_at, advanced, fresh):
            o_ref[:, hd], state[:, hd], inv_ref[:, at] = o, s1, inv
            u_s[:, hd], w_s[:, hd], q_s[:, hd], k_s[:, hd], attn_s[:, at], total_s[0:1, hd] = u, w, q_dec, k_dec, attn, total

    blk = pl.BlockSpec((c, width), lambda t: (jnp.minimum(t, n - 1), 0))
    gt = pl.BlockSpec((c, LANES), lambda t: (jnp.minimum(t, n - 1), 0))
    iv = pl.BlockSpec((c, GDN_HEADS * c), lambda t: (jnp.minimum(t, n - 1), 0))
    out = pl.BlockSpec((c, width), lambda t: (jnp.maximum(t - 1, 0), 0))
    st = pl.BlockSpec((dk, width), lambda t: (jnp.maximum(t - 1, 0), 0))
    rows = pltpu.VMEM((c, width), f32)
    return _delta_rule_call(name, walk, n + 1, [blk] * 3 + [gt], [out, st, iv],
                            [_sds((s, width)), _sds((n * dk, width)), _sds((s, GDN_HEADS * c))], (q, k, v, gates), exchange,
                            scratch=(rows, rows, rows, rows, pltpu.VMEM((c, GDN_HEADS * c), f32), pltpu.VMEM((8, width), f32)))


def delta_rule_bwd(q, k, v, gates, s_in, inv, do, name, exchange=None):
    s, width = q.shape
    c, dk = GDN_CHUNK, GDN_HEAD_DIM
    n = s // c
    heads, inv_at = _delta_heads()

    def walk(ins, outs, dstate):
        q_ref, k_ref, v_ref, gates_ref, s_ref, inv_ref, do_ref = ins
        dq_ref, dk_ref, dv_ref, dgates_ref = outs
        gates = gates_ref[...]
        xs = [[r[:, hd] for r in (q_ref, k_ref, v_ref)] + [gates, s_ref[:, hd]] for hd in heads]
        known = [inv_ref[:, at] for at in inv_at]
        cts = [(do_ref[:, hd], dstate[:, hd]) for hd in heads]
        grads = []
        for i, (x, t, ct) in enumerate(zip(xs, known, cts)):
            _, vjp = jax.vjp(lambda *y, t=t, i=i: _head_chunk(*y, i, known_inv=t)[:2], *x)
            grads.append(vjp(ct))
        dgates = grads[0][3]
        for g in grads[1:]:
            dgates = dgates + g[3]
        dgates_ref[...] = dgates
        for hd, (dq, dk_, dv, _, ds0) in zip(heads, grads):
            dq_ref[:, hd], dk_ref[:, hd], dv_ref[:, hd], dstate[:, hd] = dq, dk_, dv, ds0

    blk = pl.BlockSpec((c, width), lambda t: (n - 1 - t, 0))
    gt = pl.BlockSpec((c, LANES), lambda t: (n - 1 - t, 0))
    st = pl.BlockSpec((dk, width), lambda t: (n - 1 - t, 0))
    iv = pl.BlockSpec((c, GDN_HEADS * c), lambda t: (n - 1 - t, 0))
    return _delta_rule_call(name, walk, n, [blk] * 3 + [gt, st, iv, blk], [blk] * 3 + [gt],
                            [_sds((s, width))] * 3 + [_sds((s, LANES))], (q, k, v, gates, s_in, inv, do), exchange)


def adamw(w, g, m, v, name):
    shape = w.shape
    if len(shape) == 2:
        grid, spec = (1,), pl.BlockSpec(shape, lambda i: (0, 0))
    else:
        tile = shape[1] if shape[1] <= 512 else _pick(shape[1], (512, 256, 128))
        grid, spec = (shape[0], shape[1] // tile), pl.BlockSpec((None, tile, shape[2]), lambda layer, i: (layer, i, 0))

    def body(w_ref, g_ref, m_ref, v_ref, d_ref, nm_ref, nv_ref):
        grad = g_ref[...]
        nm = ADAM_B1 * m_ref[...] + (1.0 - ADAM_B1) * grad
        nv = ADAM_B2 * v_ref[...] + (1.0 - ADAM_B2) * (grad * grad)
        m_hat = nm / (1.0 - ADAM_B1 ** ADAM_STEP)
        v_hat = nv / (1.0 - ADAM_B2 ** ADAM_STEP)
        d_ref[...] = -ADAM_LR * (m_hat / (jnp.sqrt(v_hat) + ADAM_EPS) + ADAM_WD * w_ref[...])
        nm_ref[...] = nm
        nv_ref[...] = nv

    return tuple(pl.pallas_call(body, name=name, grid=grid, in_specs=[spec] * 4, out_specs=[spec] * 3,
                                out_shape=[_sds(shape)] * 3, compiler_params=_params(len(grid)))(w, g, m, v))


def _place():
    return lax.axis_index("x"), lax.axis_index("y"), lax.axis_index("c")


def _flip(p, bits):
    return tuple(1 - v if (bits >> s) & 1 else v for v, s in zip(p, (2, 1, 0)))


def _slot(p):
    return 4 * p[0] + 2 * p[1] + p[2]


def _chip_of(p):
    return 2 * p[0] + p[1]


ANY = pl.BlockSpec(memory_space=pl.ANY)


class Gather:
    scratch = (pltpu.SemaphoreType.DMA((7,)), pltpu.SemaphoreType.DMA((7,)), pltpu.SemaphoreType.DMA)

    def __init__(self, shard):
        self.operand = shard
        self.out_shape = jax.ShapeDtypeStruct((N_DEV,) + shard.shape, shard.dtype)

    def bind(self, x_ref, out_ref, send_sems, recv_sems, local_sem):
        me = _place()
        sibling = _flip(me, 1)
        chips = [_flip(me, 4), _flip(me, 2), _flip(me, 6)]

        def copy(k, block, to, src=None):
            return pltpu.make_async_remote_copy(
                src_ref=out_ref.at[_slot(block)] if src is None else src, dst_ref=out_ref.at[_slot(block)],
                send_sem=send_sems.at[k], recv_sem=recv_sems.at[k], device_id=to, device_id_type=MESH)

        mine = pltpu.make_async_copy(x_ref, out_ref.at[_slot(me)], local_sem)
        first = [copy(0, me, sibling, src=x_ref)] + [copy(1 + j, me, chip, src=x_ref) for j, chip in enumerate(chips)]
        passed = [copy(4 + j, chip, sibling) for j, chip in enumerate(chips)]

        def start():
            mine.start()
            for cp in first:
                cp.start()

        def pass_on():
            for j, chip in enumerate(chips):
                copy(1 + j, chip, me).wait_recv()
                passed[j].start()

        def settle():
            copy(0, sibling, me).wait_recv()
            for j, chip in enumerate(chips):
                copy(4 + j, _flip(chip, 1), me).wait_recv()
            for cp in first + passed:
                cp.wait_send()
            mine.wait()

        return start, pass_on, settle


class ChipExchange:
    scratch = (pltpu.SemaphoreType.DMA((3,)), pltpu.SemaphoreType.DMA((3,)), pltpu.SemaphoreType.DMA)

    def __init__(self, blocks):
        self.operand = blocks
        self.out_shape = jax.ShapeDtypeStruct(blocks.shape, blocks.dtype)

    def bind(self, x_ref, out_ref, send_sems, recv_sems, local_sem):
        me = _place()
        peers = [_flip(me, 4), _flip(me, 2), _flip(me, 6)]
        mine = pltpu.make_async_copy(x_ref.at[_chip_of(me)], out_ref.at[_chip_of(me)], local_sem)

        def copy(j, src_chip, dst_chip):
            return pltpu.make_async_remote_copy(
                src_ref=x_ref.at[src_chip], dst_ref=out_ref.at[dst_chip], send_sem=send_sems.at[j],
                recv_sem=recv_sems.at[j], device_id=peers[j], device_id_type=MESH)

        sends = [copy(j, _chip_of(peer), _chip_of(me)) for j, peer in enumerate(peers)]

        def start():
            mine.start()
            for cp in sends:
                cp.start()

        def finish():
            for j, peer in enumerate(peers):
                copy(j, _chip_of(me), _chip_of(peer)).wait_recv()
            for cp in sends:
                cp.wait_send()
            mine.wait()

        return start, lambda: None, finish


class Together:
    def __init__(self, *parts):
        self.parts = parts
        self.operands = [p.operand for p in parts]
        self.out_shapes = [p.out_shape for p in parts]
        self.scratch = [s for p in parts for s in p.scratch]

    def bind(self, x_refs, out_refs, sems):
        start, _, finish = self.bind_with_early(x_refs, out_refs, sems, 0)
        return start, finish

    def bind_with_early(self, x_refs, out_refs, sems, n_early):
        bound, at = [], 0
        for p, x_ref, out_ref in zip(self.parts, x_refs, out_refs):
            bound.append(p.bind(x_ref, out_ref, *sems[at:at + len(p.scratch)]))
            at += len(p.scratch)

        def start():
            for s, _, _ in bound:
                s()

        def early():
            for _, pass_on, _ in bound[:n_early]:
                pass_on()

        def finish():
            for _, pass_on, _ in bound[n_early:]:
                pass_on()
            for _, _, settle in bound:
                settle()

        return start, early, finish


def exchange_alone(exchange, name):
    n = len(exchange.operands)

    def body(*refs):
        start, finish = exchange.bind(refs[:n], refs[n:2 * n], refs[2 * n:])
        start()
        finish()

    return pl.pallas_call(body, name=name, out_shape=exchange.out_shapes, in_specs=[ANY] * n, out_specs=[ANY] * n,
                          scratch_shapes=exchange.scratch)(*exchange.operands)


def _row_tile(rows):
    return max([t for t in range(16, min(rows, 1024) + 1, 16) if rows % t == 0] or [rows])


def pair_exchange(blocks, name):
    n = len(blocks)

    def body(*refs):
        x_refs, theirs_refs, (send_sems, recv_sems) = refs[:n], refs[n:2 * n], refs[2 * n:]
        me = _place()
        remote = [pltpu.make_async_remote_copy(
            src_ref=x_refs[t].at[2 * q + 1 - me[2]], dst_ref=theirs_refs[t].at[q], send_sem=send_sems.at[4 * t + q],
            recv_sem=recv_sems.at[4 * t + q], device_id=_flip(me, 1), device_id_type=MESH) for t in range(n) for q in range(4)]
        for cp in remote:
            cp.start()
        for cp in remote:
            cp.wait()

    return pl.pallas_call(
        body, name=name, out_shape=[jax.ShapeDtypeStruct((4,) + b.shape[1:], b.dtype) for b in blocks], in_specs=[ANY] * n,
        out_specs=[ANY] * n, scratch_shapes=[pltpu.SemaphoreType.DMA((4 * n,)), pltpu.SemaphoreType.DMA((4 * n,))])(*blocks)


def pair_add(blocks, theirs, name):
    n, rows, width = theirs.shape
    tile = _row_tile(rows)
    spec = pl.BlockSpec((None, tile, width), lambda q, i: (q, i, 0))
    south = pl.BlockSpec((None, None, tile, width), lambda q, i: (q, 0, i, 0))
    north = pl.BlockSpec((None, None, tile, width), lambda q, i: (q, 1, i, 0))

    def body(s_ref, n_ref, b_ref, o_ref):
        mine = jnp.where(lax.axis_index("c") == 0, s_ref[...], n_ref[...])
        o_ref[...] = (mine.astype(f32) + b_ref[...].astype(f32)).astype(o_ref.dtype)

    by_core = blocks.reshape(n, 2, rows, width)
    return pl.pallas_call(body, name=name, grid=(n, rows // tile), in_specs=[south, north, spec], out_specs=spec,
                          out_shape=jax.ShapeDtypeStruct(theirs.shape, theirs.dtype), compiler_params=_params(2))(by_core, by_core, theirs)


def sum_slots(blocks, name):
    n, rows, width = blocks.shape
    tile = _row_tile(rows)

    def body(x_ref, o_ref):
        total = x_ref[0].astype(f32)
        for s in range(1, n):
            total = total + x_ref[s].astype(f32)
        o_ref[...] = total

    return pl.pallas_call(
        body, name=name, grid=(rows // tile,), in_specs=[pl.BlockSpec((n, tile, width), lambda i: (0, i, 0))],
        out_specs=pl.BlockSpec((tile, width), lambda i: (i, 0)), out_shape=_sds((rows, width)), compiler_params=_params(1))(blocks)


def all_reduce_small(x, name):
    rows, width = x.shape

    def body(x_ref, o_ref, land, send_sems, recv_sems):
        me = _place()
        copies = []
        for k in range(1, N_DEV):
            peer = _flip(me, k)
            copies.append(pltpu.make_async_remote_copy(
                src_ref=x_ref, dst_ref=land.at[_slot(me)], send_sem=send_sems.at[k - 1], recv_sem=recv_sems.at[k - 1],
                device_id=peer, device_id_type=MESH))
        for cp in copies:
            cp.start()
        land[_slot(me)] = x_ref[...]
        for k in range(1, N_DEV):
            peer = _flip(me, k)
            pltpu.make_async_remote_copy(
                src_ref=x_ref, dst_ref=land.at[_slot(peer)], send_sem=send_sems.at[k - 1], recv_sem=recv_sems.at[k - 1],
                device_id=peer, device_id_type=MESH).wait_recv()
        total = land[0]
        for s in range(1, N_DEV):
            total = total + land[s]
        o_ref[...] = total
        for cp in copies:
            cp.wait_send()

    return pl.pallas_call(
        body, name=name, out_shape=_sds((rows, width)), in_specs=[pl.BlockSpec(memory_space=pltpu.VMEM)],
        out_specs=pl.BlockSpec(memory_space=pltpu.VMEM),
        scratch_shapes=[pltpu.VMEM((N_DEV, rows, width), f32), pltpu.SemaphoreType.DMA((7,)), pltpu.SemaphoreType.DMA((7,))],
    )(x)


def _pack_big(shards):
    packed = {name: shards[name].astype(bf16) for name in COL_SHARDED}
    packed["rows"] = jnp.concatenate([shards[name].astype(bf16) for name, _ in ROW_SHARDED], axis=1)
    return packed


def _unpack_gathered(gathered):
    full = {}
    for name, part in gathered.items():
        if name == "w_gate_up":
            full[name] = part
        elif name in COL_SHARDED:
            full[name] = part.transpose(1, 0, 2).reshape(D_MODEL, N_DEV * part.shape[2])
        else:
            at = 0
            for weight, rows in ROW_SHARDED:
                full[weight] = part[:, at:at + rows, :].reshape(N_DEV * rows, D_MODEL)
                at += rows
    return full


def _pack_grads(grads, group):
    packed = {}
    for name in group:
        if name == "w_gate_up":
            packed[name] = grads[name]
        elif name == "rows":
            packed[name] = jnp.concatenate([grads[weight].reshape(N_DEV, rows, D_MODEL) for weight, rows in ROW_SHARDED], axis=1)
        else:
            packed[name] = grads[name].reshape(D_MODEL, N_DEV, grads[name].shape[1] // N_DEV).transpose(1, 0, 2)
    return packed


def _unpack_shard(layers):
    out = {name: jnp.stack([layer[name] for layer in layers]) for name in COL_SHARDED}
    rows_pack, at = jnp.stack([layer["rows"] for layer in layers]), 0
    for weight, rows in ROW_SHARDED:
        out[weight] = rows_pack[:, at:at + rows, :]
        at += rows
    return out


def _rows_of(flat_len):
    return -(-flat_len // (8 * D_MODEL)) * 8


def _pack_small(parts):
    flat = jnp.concatenate([p.reshape(-1) for p in parts])
    rows = _rows_of(flat.shape[0])
    flat = jnp.pad(flat, (0, rows * D_MODEL - flat.shape[0]))
    return flat.reshape(rows, D_MODEL)


def _unpack_small(packed, like):
    flat, out, at = packed.reshape(-1), [], 0
    for p in like:
        out.append(flat[at:at + p.size].reshape(p.shape))
        at += p.size
    return out


def _rope_tables(positions):
    inv_freq = jnp.float32(ROPE_THETA) ** (-jnp.arange(0, ROPE_DIM, 2, dtype=f32) / ROPE_DIM)
    ang = positions.astype(f32)[:, None] * inv_freq
    cos, sin = jnp.cos(ang), jnp.sin(ang)
    rest = ATTN_HEAD_DIM - ROPE_DIM
    cos_h = jnp.concatenate([cos, cos, jnp.ones((cos.shape[0], rest), f32)], axis=1)
    sin_h = jnp.concatenate([-sin, sin, jnp.zeros((sin.shape[0], rest), f32)], axis=1)
    return jnp.tile(cos_h, (1, ATTN_HEADS)), jnp.tile(sin_h, (1, ATTN_HEADS))


HEAD_SMALL = ("norm_mix_pre", "conv_short", "conv_gdn", "gdn_a_log", "gdn_dt_bias")


def _layer_head(h, p, cos_t, sin_t):
    hn = rms_norm(h, p["norm_mix_pre"][None], "norm_mix_pre")
    aw, cw, gw = ATTN_WIDTH, CONV_WIDTH, GDN_WIDTH
    aq, ak, av, cb, cc, cx, gqkv, ab, gate = _split_cols(_linear(hn, p["w_in"], "w_in"),
                                                         (aw, aw, aw, cw, cw, cw, 3 * gw, 2 * GDN_HEADS, gw))
    ab = jnp.pad(ab, ((0, 0), (0, LANES - 2 * GDN_HEADS)))
    y_attn = dilated_attention(rope(aq, cos_t, sin_t, ATTN_HEAD_DIM ** -0.5, "rope_q"), rope(ak, cos_t, sin_t, 1.0, "rope_k"),
                               av, "attn")
    y_conv = short_conv(cb, cc, cx, p["conv_short"], "short_conv")
    qkv = gdn_pre(gqkv, p["conv_gdn"], "gdn_pre")
    pv = jnp.zeros((8, LANES), f32).at[0, :GDN_HEADS].set(p["gdn_a_log"]).at[1, :GDN_HEADS].set(p["gdn_dt_bias"])
    return (*_split_cols(qkv, (gw, gw, gw)), gate_beta(ab, pv, "gate_beta")), (gate, y_attn, y_conv)


MID_PARAMS = ("gdn_norm", "w_out", "norm_mix_post", "norm_xattn_pre", "w_xq", "norm_mem", "w_xkv", "w_xo", "norm_xattn_post",
              "norm_ffn_pre")


def _layer_mid(h, o, gate, y_attn, y_conv, p, mem):
    y_gdn = gdn_post(o, gate, p["gdn_norm"][None], "gdn_post")
    mix = _joined_linear((y_attn, y_conv, y_gdn), p["w_out"], "w_out")
    h, hn = add_norm_then_norm(h, mix, p["norm_mix_post"][None], p["norm_xattn_pre"][None], "norm_mix_xattn")
    qx = _linear(hn, p["w_xq"], "w_xq")
    kv = _linear(rms_norm(mem, p["norm_mem"][None], "norm_mem"), p["w_xkv"], "w_xkv")
    xa = _linear(cross_attention(qx, kv, "xattn"), p["w_xo"], "w_xo")
    return add_norm_then_norm(h, xa, p["norm_xattn_post"][None], p["norm_ffn_pre"][None], "norm_xattn_ffn")


def _pair_summed(grads, group, name):
    blocks = _pack_grads(grads, group)
    theirs = pair_exchange([blocks[n] for n in group], name + "_pair_exchange")
    return [pair_add(blocks[n], t, f"{name}_pair_add_{n}") for n, t in zip(group, theirs)]


def _forward_backward(x, packed, small, mem, cos_t, sin_t, target):
    def gathers(group, layer):
        return [Gather(packed[n][layer]) for n in group]

    h = x
    head_gathered = exchange_alone(Together(*gathers(HEAD_GROUP, 0)), "gather_first")
    saved = []
    for layer in range(DEPTH):
        at_layer = {n: t[layer] for n, t in small.items()}
        head_p = {**_unpack_gathered(dict(zip(HEAD_GROUP, head_gathered))), **{n: at_layer[n] for n in HEAD_SMALL}}
        (rule_in, rest), head_vjp = jax.vjp(lambda h, hp: _layer_head(h, hp, cos_t, sin_t), h, head_p)
        carried = gathers(TAIL_GROUP, layer) + (gathers(HEAD_GROUP, layer + 1) if layer + 1 < DEPTH else [])
        o, s_in, inv, *landed = delta_rule_fwd(*rule_in, "delta_rule_fwd", Together(*carried))
        head_gathered = landed[len(TAIL_GROUP):]
        tail_p = {**_unpack_gathered(dict(zip(TAIL_GROUP, landed))), **at_layer}
        mid_p = {n: tail_p[n] for n in MID_PARAMS}
        (h, hn), mid_vjp = jax.vjp(lambda h, o, rest, mp: _layer_mid(h, o, *rest, mp, mem), h, o, rest, mid_p)
        y, ffn_saved = ffn_forward(hn, tail_p["w_gate_up"], tail_p["w_down"], "ffn")
        h, last_vjp = jax.vjp(lambda h, y, w: add_norm(h, y, w[None], "norm_ffn_post"), h, y, tail_p["norm_ffn_post"])
        saved.append((head_vjp, mid_vjp, last_vjp, ffn_saved, rule_in, s_in, inv))

    loss, dh = jax.value_and_grad(lambda y: loss_rows(y, target, "loss"))(h)

    def summed(group, landed):
        return {n: sum_slots(t, "sum_grads_" + n) for n, t in zip(group, landed)}

    big_grads, small_grads, head_pending = [{} for _ in range(DEPTH)], [None] * DEPTH, []
    for layer in reversed(range(DEPTH)):
        head_vjp, mid_vjp, last_vjp, ffn_saved, rule_in, s_in, inv = saved[layer]
        dh, dy, d_norm_ffn_post = last_vjp(dh)
        dhn, d_gate_up, d_down, landed = ffn_backward(
            ffn_saved, dy, "ffn", Together(*[ChipExchange(t) for t in head_pending]) if head_pending else None)
        if head_pending:
            big_grads[layer + 1].update(summed(HEAD_GROUP, landed))
        dh_mid, do, d_rest, d_mid_p = mid_vjp((dh, dhn))
        d_tail_p = {**d_mid_p, "w_gate_up": d_gate_up, "w_down": d_down, "norm_ffn_post": d_norm_ffn_post}
        carried = Together(*[ChipExchange(t) for t in _pair_summed(d_tail_p, TAIL_GROUP, "tail")])
        *d_rule_in, = delta_rule_bwd(*rule_in, s_in, inv, do, "delta_rule_bwd", carried)
        big_grads[layer].update(summed(TAIL_GROUP, d_rule_in[4:]))
        dh_head, d_head_p = head_vjp((tuple(d_rule_in[:4]), d_rest))
        dh = dh_mid + dh_head
        small_grads[layer] = {n: t for n, t in {**d_head_p, **d_tail_p}.items() if n in small}
        head_pending = _pair_summed(d_head_p, HEAD_GROUP, "head")
    landed = exchange_alone(Together(*[ChipExchange(t) for t in head_pending]), "exchange_last")
    big_grads[0].update(summed(HEAD_GROUP, landed))
    return loss, dh, big_grads, small_grads


def kernel(x, mem, positions, norm_mix_pre, norm_mix_post, w_in, conv_short, conv_gdn, gdn_a_log, gdn_dt_bias, gdn_norm, w_out, norm_mem, norm_xattn_pre, norm_xattn_post, w_xq, w_xkv, w_xo, norm_ffn_pre, norm_ffn_post, w_gate_up, w_down, loss_target, m_norm_mix_pre, m_norm_mix_post, m_w_in, m_conv_short, m_conv_gdn, m_gdn_a_log, m_gdn_dt_bias, m_gdn_norm, m_w_out, m_norm_mem, m_norm_xattn_pre, m_norm_xattn_post, m_w_xq, m_w_xkv, m_w_xo, m_norm_ffn_pre, m_norm_ffn_post, m_w_gate_up, m_w_down, v_norm_mix_pre, v_norm_mix_post, v_w_in, v_conv_short, v_conv_gdn, v_gdn_a_log, v_gdn_dt_bias, v_gdn_norm, v_w_out, v_norm_mem, v_norm_xattn_pre, v_norm_xattn_post, v_w_xq, v_w_xkv, v_w_xo, v_norm_ffn_pre, v_norm_ffn_post, v_w_gate_up, v_w_down):
    given = dict(locals())
    weights = {n: given[n] for n in WEIGHTS}
    me = _slot(_place())

    def in_place(shard):
        full = jnp.zeros(shard.shape[:-1] + (shard.shape[-1] * N_DEV,), f32)
        return lax.dynamic_update_slice_in_dim(full, shard, me * shard.shape[-1], axis=shard.ndim - 1)

    placed = [in_place(conv_short), in_place(conv_gdn)]
    conv_short_full, conv_gdn_full = _unpack_small(all_reduce_small(_pack_small(placed), "gather_conv"), placed)
    small = {n: weights[n] for n in NORMS + ("gdn_a_log", "gdn_dt_bias", "gdn_norm")}
    small["conv_short"], small["conv_gdn"] = conv_short_full, conv_gdn_full

    cos_t, sin_t = _rope_tables(positions[0])
    loss, grad_x, big_layers, small_layers = _forward_backward(
        x[0], _pack_big(weights), small, mem[0], cos_t, sin_t, loss_target[0])
    grads = _unpack_shard(big_layers)

    names = sorted(small)
    parts = [jnp.stack([layer[n] for layer in small_layers]) for n in names] + [loss.reshape(1)]
    reduced = _unpack_small(all_reduce_small(_pack_small(parts), "reduce_small"), parts)
    loss = reduced[-1][0]
    for n, g in zip(names, reduced[:-1]):
        if n in ("conv_short", "conv_gdn"):
            width = weights[n].shape[-1]
            g = lax.dynamic_slice_in_dim(g, me * width, width, axis=g.ndim - 1)
        grads[n] = g

    delta, new_m, new_v = {}, {}, {}
    for n in WEIGHTS:
        delta[n], new_m[n], new_v[n] = adamw(weights[n], grads[n], given["m_" + n], given["v_" + n], "adamw_" + n)
    return (loss, grad_x[None], *[grads[n] for n in WEIGHTS], *[delta[n] for n in WEIGHTS],
            *[new_m[n] for n in WEIGHTS], *[new_v[n] for n in WEIGHTS])
```

```python
import functools

import jax
import jax.numpy as jnp
from jax import lax
from jax.experimental import pallas as pl
from jax.experimental.pallas import tpu as pltpu

f32 = jnp.float32
bf16 = jnp.bfloat16
MESH = pl.DeviceIdType.MESH

N_DEV = 8
DEPTH = 4
D_MODEL = 1024
EPS = 1e-6
ATTN_HEADS, ATTN_HEAD_DIM = 4, 64
ATTN_WIDTH = ATTN_HEADS * ATTN_HEAD_DIM
DILATIONS = (1, 4, 16)
QB = 128
ROPE_THETA = 500000.0
ROPE_DIM = ATTN_HEAD_DIM // 4
CONV_WIDTH = 256
GDN_HEADS, GDN_HEAD_DIM = 4, 128
GDN_WIDTH = GDN_HEADS * GDN_HEAD_DIM
GDN_CHUNK = 64
XATTN_HEADS, XATTN_HEAD_DIM = 4, 256
LANES = 128
ROW_TILE = 512
VMEM_LIMIT = 56 * 1024 * 1024

ADAM_LR, ADAM_B1, ADAM_B2, ADAM_EPS, ADAM_WD, ADAM_STEP = 0.001, 0.9, 0.999, 1e-08, 0.01, 10

COL_SHARDED = ("w_in", "w_xkv", "w_gate_up")
ROW_SHARDED = (("w_out", 128), ("w_xq", 128), ("w_xo", 128), ("w_down", 352))
HEAD_GROUP = ("w_in",)
TAIL_GROUP = ("w_gate_up", "w_xkv", "rows")
NORMS = ("norm_mix_pre", "norm_mix_post", "norm_mem", "norm_xattn_pre", "norm_xattn_post", "norm_ffn_pre", "norm_ffn_post")
WEIGHTS = ("norm_mix_pre", "norm_mix_post", "w_in", "conv_short", "conv_gdn", "gdn_a_log", "gdn_dt_bias", "gdn_norm", "w_out",
           "norm_mem", "norm_xattn_pre", "norm_xattn_post", "w_xq", "w_xkv", "w_xo", "norm_ffn_pre", "norm_ffn_post",
           "w_gate_up", "w_down")


def _params(n_grid):
    return pltpu.CompilerParams(dimension_semantics=("arbitrary",) * n_grid, vmem_limit_bytes=VMEM_LIMIT)


def _pick(n, cands):
    for c in cands:
        if n % c == 0:
            return c
    return n


MXU_FLOPS = 9.0e14
HBM_BYTES_PER_S = 2.5e12
VMEM_RMW_BYTES_PER_S = 7.0e12
STEP_S = 0.4e-6
MATMUL_VMEM = 44 * 1024 * 1024


def _tiles(m, n, k, sa, sb, so):
    def divisors(d):
        return sorted({d} | {d // s for s in range(1, d // LANES + 1) if d % s == 0 and (d // s) % LANES == 0}, reverse=True)

    best = None
    for tk in divisors(k):
        nk = k // tk
        for tm in divisors(m):
            for tn_ in divisors(n):
                per_step = tm * tk * sa + tk * tn_ * sb + tm * tn_ * so
                vmem = 2 * per_step + (tm * tn_ * 4 if nk > 1 else 0)
                vmem += (tm * tk * 2 if sa == 4 else 0) + (tk * tn_ * 2 if sb == 4 else 0) + tm * tn_ * 4
                if vmem > MATMUL_VMEM:
                    continue
                moved = m * k * sa * (1 if nk == 1 else n // tn_) + k * n * sb * (1 if nk == 1 and n == tn_ else m // tm) + m * n * so
                busy = 2 * m * n * k / MXU_FLOPS + (m * n * 8 * nk / VMEM_RMW_BYTES_PER_S if nk > 1 else 0)
                cost = max(moved / HBM_BYTES_PER_S, busy) + per_step / HBM_BYTES_PER_S + (m // tm) * (n // tn_) * nk * STEP_S
                if best is None or cost < best[0]:
                    best = (cost, tm, tn_, tk)
    return best[1:]


def _mm(a, b, ta, tb, out_dtype, name):
    m, k = (a.shape[1], a.shape[0]) if ta else a.shape
    n = b.shape[0] if tb else b.shape[1]
    tm, tn, tk = _tiles(m, n, k, a.dtype.itemsize, b.dtype.itemsize, jnp.dtype(out_dtype).itemsize)
    nk = k // tk
    a_spec = pl.BlockSpec((tk, tm), lambda i, j, kk: (kk, i)) if ta else pl.BlockSpec((tm, tk), lambda i, j, kk: (i, kk))
    b_spec = pl.BlockSpec((tn, tk), lambda i, j, kk: (j, kk)) if tb else pl.BlockSpec((tk, tn), lambda i, j, kk: (kk, j))
    dims = (((0 if ta else 1,), (1 if tb else 0,)), ((), ()))

    def body(a_ref, b_ref, o_ref, *acc):
        kk = pl.program_id(2)
        p = lax.dot_general(a_ref[...].astype(bf16), b_ref[...].astype(bf16), dims, preferred_element_type=f32)
        if nk == 1:
            o_ref[...] = p.astype(o_ref.dtype)
            return
        acc_ref, = acc

        @pl.when(kk == 0)
        def _():
            acc_ref[...] = p

        @pl.when(kk > 0)
        def _():
            acc_ref[...] += p

        @pl.when(kk == nk - 1)
        def _():
            o_ref[...] = acc_ref[...].astype(o_ref.dtype)

    return pl.pallas_call(
        body, name=name, grid=(m // tm, n // tn, nk), in_specs=[a_spec, b_spec],
        out_specs=pl.BlockSpec((tm, tn), lambda i, j, kk: (i, j)), out_shape=jax.ShapeDtypeStruct((m, n), out_dtype),
        scratch_shapes=[pltpu.VMEM((tm, tn), f32)] if nk > 1 else [], compiler_params=_params(3))(a, b)


def _linear(x, w, name):
    @jax.custom_vjp
    def lin(x, w):
        return _mm(x, w, False, False, f32, name + "_y")

    def lin_f(x, w):
        return _mm(x, w, False, False, f32, name + "_y"), (x, w)

    def lin_b(res, dy):
        x, w = res
        return _mm(dy, w, False, True, f32, name + "_dx"), _mm(x, dy, True, False, bf16, name + "_dw")

    lin.defvjp(lin_f, lin_b)
    return lin(x, w)


def _bdot(a, b, form):
    dims = {"nn": ((1,), (0,)), "nt": ((1,), (1,)), "tn": ((0,), (0,))}[form]
    return lax.dot_general(a.astype(bf16), b.astype(bf16), (dims, ((), ())), preferred_element_type=f32)


def ffn_forward(hn, w_gate_up, w_down, name):
    s, k = hn.shape
    n_blocks, _, width = w_gate_up.shape
    half = n_blocks // 2
    tm = 1024
    blocked = jax.ShapeDtypeStruct((half, s, width), bf16)

    def act_body(x_ref, wg_ref, wu_ref, gate_ref, up_ref, act_ref):
        x = x_ref[...]
        gate, up = _bdot(x, wg_ref[...], "nn"), _bdot(x, wu_ref[...], "nn")
        gate_ref[...], up_ref[...] = gate.astype(bf16), up.astype(bf16)
        act_ref[...] = (jax.nn.silu(gate) * up).astype(bf16)

    tile = pl.BlockSpec((None, tm, width), lambda i, d: (d, i, 0))
    gate, up, act = pl.pallas_call(
        act_body, name=name + "_act", grid=(s // tm, half),
        in_specs=[pl.BlockSpec((tm, k), lambda i, d: (i, 0)), pl.BlockSpec((None, k, width), lambda i, d: (d, 0, 0)),
                  pl.BlockSpec((None, k, width), lambda i, d: (d + half, 0, 0))],
        out_specs=[tile] * 3, out_shape=[blocked] * 3, compiler_params=_params(2))(hn, w_gate_up, w_gate_up)

    n = w_down.shape[1]
    tn = 512

    def y_body(act_ref, w_ref, y_ref):
        y_ref[...] = sum(_bdot(act_ref[d], w_ref[d * width:(d + 1) * width, :], "nn") for d in range(half))

    y = pl.pallas_call(
        y_body, name=name + "_y", grid=(s // tm, n // tn),
        in_specs=[pl.BlockSpec((half, tm, width), lambda i, j: (0, i, 0)), pl.BlockSpec((half * width, tn), lambda i, j: (0, j))],
        out_specs=pl.BlockSpec((tm, tn), lambda i, j: (i, j)), out_shape=_sds((s, n)), compiler_params=_params(2))(act, w_down)
    return y, (hn, w_gate_up, w_down, gate, up, act)


def ffn_backward(saved, dy, name, exchange=None):
    hn, w_gate_up, w_down, gate, up, act = saved
    s, k = hn.shape
    n_blocks, _, width = w_gate_up.shape
    half = n_blocks // 2
    n = w_down.shape[1]
    tm = 1024
    blocked = jax.ShapeDtypeStruct((half, s, width), bf16)
    carried = len(exchange.operands) if exchange else 0
    steps = (s // tm, half)

    def dact_body(dy_ref, w_ref, gate_ref, up_ref, *refs):
        x_refs, refs = refs[:carried], refs[carried:]
        (dgate_ref, dup_ref), refs = refs[:2], refs[2:]
        if exchange:
            at = pl.program_id(0) * steps[1] + pl.program_id(1)
            start, wait = exchange.bind(x_refs, refs[:carried], refs[carried:])
            pl.when(at == 0)(start)
        d_act = _bdot(dy_ref[...], w_ref[...], "nt")
        g, u = gate_ref[...].astype(f32), up_ref[...].astype(f32)
        sig = jax.nn.sigmoid(g)
        dgate_ref[...] = (d_act * u * sig * (1.0 + g * (1.0 - sig))).astype(bf16)
        dup_ref[...] = (d_act * g * sig).astype(bf16)
        if exchange:
            pl.when(at == steps[0] * steps[1] - 1)(wait)

    tile = pl.BlockSpec((None, tm, width), lambda i, d: (d, i, 0))
    d_gate, d_up, *landed = pl.pallas_call(
        dact_body, name=name + "_dact", grid=steps,
        in_specs=[pl.BlockSpec((tm, n), lambda i, d: (i, 0)), pl.BlockSpec((width, n), lambda i, d: (d, 0)), tile, tile] + [ANY] * carried,
        out_specs=[tile, tile] + [ANY] * carried, out_shape=[blocked, blocked] + (exchange.out_shapes if exchange else []),
        scratch_shapes=exchange.scratch if exchange else [],
        compiler_params=_params(2))(dy, w_down, gate, up, *(exchange.operands if exchange else []))

    def dx_body(dg_ref, du_ref, w_ref, dx_ref):
        dx_ref[...] = sum(_bdot(dg_ref[d], w_ref[d], "nt") + _bdot(du_ref[d], w_ref[d + half], "nt") for d in range(half))

    tx = 512
    rows = pl.BlockSpec((half, tx, width), lambda i: (0, i, 0))
    dx = pl.pallas_call(
        dx_body, name=name + "_dx", grid=(s // tx,), in_specs=[rows, rows, _whole(w_gate_up.shape)],
        out_specs=pl.BlockSpec((tx, k), lambda i: (i, 0)), out_shape=_sds((s, k)), compiler_params=_params(1))(d_gate, d_up, w_gate_up)

    def dw1_body(x_ref, dg_ref, du_ref, dw_ref):
        d_block = jnp.where(pl.program_id(0) < half, dg_ref[...], du_ref[...])
        dw_ref[...] = _bdot(x_ref[...], d_block, "tn").astype(bf16)

    d_w_gate_up = pl.pallas_call(
        dw1_body, name=name + "_dw1", grid=(n_blocks,),
        in_specs=[_whole((s, k)), pl.BlockSpec((None, s, width), lambda b: (jnp.minimum(b, half - 1), 0, 0)),
                  pl.BlockSpec((None, s, width), lambda b: (jnp.maximum(b - half, 0), 0, 0))],
        out_specs=pl.BlockSpec((None, k, width), lambda b: (b, 0, 0)), out_shape=jax.ShapeDtypeStruct(w_gate_up.shape, bf16),
        compiler_params=_params(1))(hn, d_gate, d_up)

    tn = 512

    def dw2_body(act_ref, dy_ref, dw_ref):
        dw_ref[...] = _bdot(act_ref[...], dy_ref[...], "tn").astype(bf16)

    d_w_down = pl.pallas_call(
        dw2_body, name=name + "_dw2", grid=(half, n // tn),
        in_specs=[pl.BlockSpec((None, s, width), lambda d, j: (d, 0, 0)), pl.BlockSpec((s, tn), lambda d, j: (0, j))],
        out_specs=pl.BlockSpec((width, tn), lambda d, j: (d, j)), out_shape=jax.ShapeDtypeStruct(w_down.shape, bf16),
        compiler_params=_params(2))(act, dy)
    return dx, d_w_gate_up, d_w_down, landed


def _joined_linear(parts, w, name):
    s, n = parts[0].shape[0], w.shape[1]
    widths = [p.shape[1] for p in parts]
    edges = [sum(widths[:i]) for i in range(len(widths) + 1)]
    spans = list(zip(edges[:-1], edges[1:]))
    tm, tn = 512, 512

    def forward(*args):
        *xs, w = args

        def y_body(*refs):
            *x_refs, w_ref, y_ref = refs
            y_ref[...] = sum(_bdot(x_ref[...], w_ref[a:b, :], "nn") for x_ref, (a, b) in zip(x_refs, spans))

        y = pl.pallas_call(
            y_body, name=name + "_y", grid=(s // tm, n // tn),
            in_specs=[pl.BlockSpec((tm, k), lambda i, j: (i, 0)) for k in widths] + [pl.BlockSpec((edges[-1], tn), lambda i, j: (0, j))],
            out_specs=pl.BlockSpec((tm, tn), lambda i, j: (i, j)), out_shape=_sds((s, n)), compiler_params=_params(2))(*xs, w)
        return y, args

    def backward(args, dy):
        *xs, w = args

        def dx_body(dy_ref, w_ref, *dx_refs):
            d_all = _bdot(dy_ref[...], w_ref[...], "nt")
            for dx_ref, (a, b) in zip(dx_refs, spans):
                dx_ref[...] = d_all[:, a:b]

        dxs = pl.pallas_call(
            dx_body, name=name + "_dx", grid=(s // tm,), in_specs=[pl.BlockSpec((tm, n), lambda i: (i, 0)), _whole(w.shape)],
            out_specs=[pl.BlockSpec((tm, k), lambda i: (i, 0)) for k in widths], out_shape=[_sds((s, k)) for k in widths],
            compiler_params=_params(1))(dy, w)

        def dw_body(*refs):
            *x_refs, dy_ref, dw_ref = refs
            dy_tile = dy_ref[...]
            dw_ref[...] = jnp.concatenate([_bdot(x_ref[...], dy_tile, "tn") for x_ref in x_refs], axis=0).astype(bf16)

        dw = pl.pallas_call(
            dw_body, name=name + "_dw", grid=(n // tn,),
            in_specs=[_whole((s, k)) for k in widths] + [pl.BlockSpec((s, tn), lambda j: (0, j))],
            out_specs=pl.BlockSpec((edges[-1], tn), lambda j: (0, j)), out_shape=jax.ShapeDtypeStruct(w.shape, bf16),
            compiler_params=_params(1))(*xs, dy)
        return (*dxs, dw)

    @jax.custom_vjp
    def op(*args):
        return forward(*args)[0]

    op.defvjp(forward, backward)
    return op(*parts, w)


def _split_cols(x, widths):
    edges = [sum(widths[:i]) for i in range(len(widths) + 1)]

    def cut(x):
        return tuple(x[:, a:b] for a, b in zip(edges[:-1], edges[1:]))

    @jax.custom_vjp
    def split(x):
        return cut(x)

    split.defvjp(lambda x: (cut(x), None), lambda _, cts: (jnp.concatenate(cts, axis=1),))
    return split(x)


def _block_op(name, f, grid, in_specs, out_defs, arrays, diff, acc=None, gdefs=None):
    acc, gdefs = acc or {}, gdefs or {}
    n_in, n_out, n_grid = len(in_specs), len(out_defs), len(grid)

    def fwd_call(*xs):
        def body(*refs):
            outs = f(*[r[...] for r in refs[:n_in]])
            for r, o in zip(refs[n_in:], outs):
                r[...] = o.astype(r.dtype)

        return pl.pallas_call(
            body, name=name + "_fwd", grid=grid, in_specs=in_specs, out_specs=[d[1] for d in out_defs],
            out_shape=[d[0] for d in out_defs], compiler_params=_params(n_grid))(*xs)

    def bwd_call(*xs_and_cts):
        def body(*refs):
            xs = [r[...] for r in refs[:n_in]]
            cts = tuple(r[...] for r in refs[n_in:n_in + n_out])

            def of_diff(*dx):
                full = list(xs)
                for i, v in zip(diff, dx):
                    full[i] = v
                return tuple(f(*full))

            _, vjp = jax.vjp(of_diff, *[xs[i] for i in diff])
            grads = vjp(cts)
            for i, g, r in zip(diff, grads, refs[n_in + n_out:]):
                if i in acc:
                    first = functools.reduce(jnp.logical_and, [pl.program_id(a) == 0 for a in acc[i]])

                    @pl.when(first)
                    def _(r=r):
                        r[...] = jnp.zeros_like(r)

                    r[...] += g.astype(r.dtype)
                else:
                    r[...] = g.astype(r.dtype)

        g_defs = [gdefs.get(i, (jax.ShapeDtypeStruct(arrays[i].shape, f32), in_specs[i])) for i in diff]
        return pl.pallas_call(
            body, name=name + "_bwd", grid=grid, in_specs=list(in_specs) + [d[1] for d in out_defs],
            out_specs=[d[1] for d in g_defs], out_shape=[d[0] for d in g_defs], compiler_params=_params(n_grid))(*xs_and_cts)

    return fwd_call, bwd_call


def _simple_op(name, f, grid, in_specs, out_defs, arrays, diff, acc=None):
    fwd_call, bwd_call = _block_op(name, f, grid, in_specs, out_defs, arrays, diff, acc)

    @jax.custom_vjp
    def op(*xs):
        return tuple(fwd_call(*xs))

    def op_f(*xs):
        return tuple(fwd_call(*xs)), xs

    def op_b(xs, cts):
        grads = bwd_call(*xs, *cts)
        out = [jnp.zeros_like(x) for x in xs]
        for i, g in zip(diff, grads):
            out[i] = g
        return tuple(out)

    op.defvjp(op_f, op_b)
    return op(*arrays)


def _rows(width, tile=ROW_TILE):
    return pl.BlockSpec((tile, width), lambda i: (i, 0))


def _whole(shape):
    return pl.BlockSpec(shape, lambda *_: (0,) * len(shape))


def _sds(shape):
    return jax.ShapeDtypeStruct(shape, f32)


def _rms(x, w):
    return x * lax.rsqrt(jnp.mean(x * x, axis=-1, keepdims=True) + EPS) * w


def rms_norm(x, w, name):
    r, d = x.shape
    tile = min(ROW_TILE, r)
    return _simple_op(name, lambda x, w: (_rms(x, w),), (r // tile,), [_rows(d, tile), _whole((1, d))],
                      [(_sds((r, d)), _rows(d, tile))], (x, w), (0, 1), {1: (0,)})[0]


def add_norm(h, y, w, name):
    r, d = h.shape
    return _simple_op(name, lambda h, y, w: (h + _rms(y, w),), (r // ROW_TILE,), [_rows(d), _rows(d), _whole((1, d))],
                      [(_sds((r, d)), _rows(d))], (h, y, w), (0, 1, 2), {2: (0,)})[0]


def add_norm_then_norm(h, y, w_post, w_pre, name):
    r, d = h.shape

    def f(h, y, w_post, w_pre):
        h_new = h + _rms(y, w_post)
        return h_new, _rms(h_new, w_pre)

    return _simple_op(name, f, (r // ROW_TILE,), [_rows(d), _rows(d), _whole((1, d)), _whole((1, d))],
                      [(_sds((r, d)), _rows(d))] * 2, (h, y, w_post, w_pre), (0, 1, 2, 3), {2: (0,), 3: (0,)})


def _swap8(x):
    def raw(x):
        lane = lax.broadcasted_iota(jnp.int32, x.shape, 1) % ATTN_HEAD_DIM
        half = ROPE_DIM // 2
        up = pltpu.roll(x, x.shape[1] - half, axis=1)
        down = pltpu.roll(x, half, axis=1)
        return jnp.where(lane < half, up, jnp.where(lane < ROPE_DIM, down, 0.0))

    @jax.custom_vjp
    def swap(x):
        return raw(x)

    swap.defvjp(lambda x: (raw(x), None), lambda _, g: (raw(g),))
    return swap(x)


def rope(x, cos_t, sin_t, scale, name):
    r, d = x.shape
    return _simple_op(name, lambda x, c, s: ((x * c + _swap8(x) * s) * scale,), (r // ROW_TILE,), [_rows(d)] * 3,
                      [(_sds((r, d)), _rows(d))], (x, cos_t, sin_t), (0,))[0]


def _shift_rows(x, k):
    n = x.shape[0]

    def down(x):
        row = lax.broadcasted_iota(jnp.int32, x.shape, 0)
        return jnp.where(row >= k, pltpu.roll(x, k, axis=0), 0.0)

    def up(x):
        row = lax.broadcasted_iota(jnp.int32, x.shape, 0)
        return jnp.where(row < n - k, pltpu.roll(x, n - k, axis=0), 0.0)

    @jax.custom_vjp
    def shift(x):
        return down(x)

    shift.defvjp(lambda x: (down(x), None), lambda _, g: (up(g),))
    return shift(x)


def _causal_conv(x, w):
    taps = w.shape[0]
    y = x * w[taps - 1:taps, :]
    for j in range(taps - 1):
        y = y + _shift_rows(x, taps - 1 - j) * w[j:j + 1, :]
    return y


def _cols(rows, at=0):
    return pl.BlockSpec((rows, LANES), lambda j: (0, at + j))


def short_conv(cb, cc, cx, w, name):
    s, c = cb.shape
    taps = w.shape[0]
    return _simple_op(name, lambda b, c_, x, w: (b * _causal_conv(c_ * x, w),), (c // LANES,),
                      [_cols(s)] * 3 + [_cols(taps)], [(_sds((s, c)), _cols(s))], (cb, cc, cx, w), (0, 1, 2, 3))[0]


def gdn_pre(qkv, w, name):
    s, c = qkv.shape
    taps = w.shape[0]

    def f(x, w):
        j = pl.program_id(0)
        y = jax.nn.silu(_causal_conv(x, w))
        normed = y * lax.rsqrt(jnp.sum(y * y, axis=-1, keepdims=True) + EPS)
        scale = jnp.where(j < GDN_HEADS, GDN_HEAD_DIM ** -0.5, 1.0).astype(f32)
        return (jnp.where(j < 2 * GDN_HEADS, normed * scale, y),)

    return _simple_op(name, f, (c // LANES,), [_cols(s), _cols(taps)], [(_sds((s, c)), _cols(s))], (qkv, w), (0, 1))[0]


def gate_beta(ab, pv, name):
    s = ab.shape[0]

    def f(ab, pv):
        lane = lax.broadcasted_iota(jnp.int32, ab.shape, 1)
        g = -jnp.exp(pv[0:1, :]) * jax.nn.softplus(ab + pv[1:2, :])
        return (jnp.where(lane < GDN_HEADS, g, jnp.where(lane < 2 * GDN_HEADS, jax.nn.sigmoid(ab), 0.0)),)

    return _simple_op(name, f, (s // ROW_TILE,), [_rows(LANES), _whole((8, LANES))], [(_sds((s, LANES)), _rows(LANES))],
                      (ab, pv), (0, 1), {1: (0,)})[0]


def gdn_post(o, gate, w, name):
    s, c = o.shape

    def f(o, g, w):
        heads = [slice(hd * LANES, (hd + 1) * LANES) for hd in range(c // LANES)]
        return (jnp.concatenate([_rms(o[:, hd], w) * jax.nn.silu(g[:, hd]) for hd in heads], axis=1),)

    return _simple_op(name, f, (s // ROW_TILE,), [_rows(c), _rows(c), _whole((1, LANES))], [(_sds((s, c)), _rows(c))],
                      (o, gate, w), (0, 1, 2), {2: (0,)})[0]


def attn_merge(outs, lses, name):
    s, c = outs[0].shape

    def f(o1, o2, o3, l1, l2, l3):
        m = lax.stop_gradient(jnp.maximum(jnp.maximum(l1, l2), l3))
        e1, e2, e3 = jnp.exp(l1 - m), jnp.exp(l2 - m), jnp.exp(l3 - m)
        return ((e1 * o1 + e2 * o2 + e3 * o3) / (e1 + e2 + e3),)

    return _simple_op(name, f, (s // ROW_TILE,), [_rows(c)] * 6, [(_sds((s, c)), _rows(c))], (*outs, *lses), tuple(range(6)))[0]


def loss_rows(y, target, name):
    s, d = y.shape
    nt = s // ROW_TILE

    def f(y, t):
        e = y - t
        part = 0.5 * jnp.sum(jnp.mean(e * e, axis=-1, keepdims=True), axis=0, keepdims=True)
        return (jnp.broadcast_to(part * (1.0 / (8 * LANES)), (8, LANES)),)

    out = _simple_op(name, f, (nt,), [_rows(d)] * 2, [(_sds((nt * 8, LANES)), pl.BlockSpec((8, LANES), lambda i: (i, 0)))],
                     (y, target), (0,))[0]
    return jnp.sum(out)


def _mxu(a, b, form):
    dims = {"nn": ((1,), (0,)), "nt": ((1,), (1,)), "tn": ((0,), (0,))}

    def raw(a, b, form):
        return lax.dot_general(a.astype(bf16), b.astype(bf16), (dims[form], ((), ())), preferred_element_type=f32)

    @jax.custom_vjp
    def prod(a, b):
        return raw(a, b, form)

    def prod_b(res, ct):
        a, b = res
        if form == "nn":
            return raw(ct, b, "nt"), raw(a, ct, "tn")
        if form == "nt":
            return raw(ct, b, "nn"), raw(ct, a, "tn")
        return raw(b, ct, "nt"), raw(a, ct, "nn")

    prod.defvjp(lambda a, b: (raw(a, b, form), (a, b)), prod_b)
    return prod(a, b)


def _masked_heads_attention(q, keys, values, seen):
    dh = ATTN_HEAD_DIM
    outs, lses = [], []
    for hd in range(q.shape[1] // dh):
        at = slice(hd * dh, (hd + 1) * dh)
        sc = jnp.where(seen, _mxu(q[:, at], keys[:, at], "nt"), -jnp.inf)
        m = lax.stop_gradient(jnp.max(sc, axis=-1, keepdims=True))
        p = jnp.exp(sc - m)
        l = jnp.sum(p, axis=-1, keepdims=True)
        outs.append(_mxu(p / l, values[:, at], "nn"))
        lses.append(jnp.broadcast_to(m + jnp.log(l), (q.shape[0], dh)))
    return jnp.concatenate(outs, axis=1), jnp.concatenate(lses, axis=1)


def band_attention(q, k, v, nb, name):
    r, qb, width = q.shape

    def f(q, kp, kc, vp, vc):
        has_prev = (pl.program_id(0) % nb) > 0
        i = lax.broadcasted_iota(jnp.int32, (qb, 2 * qb), 0)
        j = lax.broadcasted_iota(jnp.int32, (qb, 2 * qb), 1)
        seen = jnp.logical_or(jnp.logical_and(jnp.logical_and(j < qb, j >= i), has_prev), jnp.logical_and(j >= qb, j - qb <= i))
        return _masked_heads_attention(q, jnp.concatenate([kp, kc], axis=0), jnp.concatenate([vp, vc], axis=0), seen)

    blk = (None, qb, width)
    cur = pl.BlockSpec(blk, lambda b: (b, 0, 0))
    prev = pl.BlockSpec(blk, lambda b: (jnp.maximum(b - 1, 0), 0, 0))
    shape = _sds((r, qb, width))
    fwd_call, bwd_call = _block_op(name, f, (r,), [cur, prev, cur, prev, cur], [(shape, cur), (shape, cur)],
                                   (q, k, k, v, v), (0, 1, 2, 3, 4), gdefs={1: (shape, cur), 3: (shape, cur)})

    def to_prev(g):
        return jnp.concatenate([g[1:], jnp.zeros_like(g[:1])], axis=0)

    @jax.custom_vjp
    def op(q, k, v):
        return tuple(fwd_call(q, k, k, v, v))

    def op_b(res, cts):
        q, k, v = res
        dq, dkp, dkc, dvp, dvc = bwd_call(q, k, k, v, v, *cts)
        return dq, dkc + to_prev(dkp), dvc + to_prev(dvp)

    op.defvjp(lambda q, k, v: (tuple(fwd_call(q, k, k, v, v)), (q, k, v)), op_b)
    return op(q, k, v)


def dilated_attention(q, k, v, name):
    s = q.shape[0]
    outs, lses = [], []
    for d in DILATIONS:
        length = s // d
        nb = length // QB
        def to_residue(t):
            return t.reshape(length, d, ATTN_WIDTH).transpose(1, 0, 2).reshape(d * nb, QB, ATTN_WIDTH)

        def from_residue(t):
            return t.reshape(d, length, ATTN_WIDTH).transpose(1, 0, 2).reshape(s, ATTN_WIDTH)

        o, lse = band_attention(to_residue(q), to_residue(k), to_residue(v), nb, f"{name}_d{d}")
        outs.append(from_residue(o))
        lses.append(from_residue(lse))
    return attn_merge(outs, lses, name + "_merge")


def cross_attention(q, kv, name):
    s = q.shape[0]
    m = kv.shape[0]
    width = XATTN_HEADS * XATTN_HEAD_DIM
    tq = 512

    def f(q, k, v):
        sc = _mxu(q, k, "nt") * (XATTN_HEAD_DIM ** -0.5)
        mx = lax.stop_gradient(jnp.max(sc, axis=-1, keepdims=True))
        p = jnp.exp(sc - mx)
        return (_mxu(p / jnp.sum(p, axis=-1, keepdims=True), v, "nn"),)

    q_spec = pl.BlockSpec((tq, XATTN_HEAD_DIM), lambda a, i: (i, a))
    k_spec = pl.BlockSpec((m, XATTN_HEAD_DIM), lambda a, i: (0, a))
    v_spec = pl.BlockSpec((m, XATTN_HEAD_DIM), lambda a, i: (0, a + XATTN_HEADS))
    half = _sds((m, width))
    fwd_call, bwd_call = _block_op(name, f, (XATTN_HEADS, s // tq), [q_spec, k_spec, v_spec], [(_sds((s, width)), q_spec)],
                                   (q, kv, kv), (0, 1, 2), acc={1: (1,), 2: (1,)}, gdefs={1: (half, k_spec), 2: (half, k_spec)})

    @jax.custom_vjp
    def op(q, kv):
        return fwd_call(q, kv, kv)[0]

    def op_b(res, ct):
        q, kv = res
        dq, dk, dv = bwd_call(q, kv, kv, ct)
        return dq, jnp.concatenate([dk, dv], axis=1)

    op.defvjp(lambda q, kv: (fwd_call(q, kv, kv)[0], (q, kv)), op_b)
    return op(q, kv)


def _hi(a, b, form="nn"):
    dims = {"nn": ((1,), (0,)), "nt": ((1,), (1,)), "tn": ((0,), (0,))}[form]
    return lax.dot_general(a, b, (dims, ((), ())), precision=lax.Precision.HIGH, preferred_element_type=f32)


def _running_sum(g):
    def raw(x, form):
        c = x.shape[0]
        tri = (lax.broadcasted_iota(jnp.int32, (c, c), 0) >= lax.broadcasted_iota(jnp.int32, (c, c), 1)).astype(bf16)
        hi = x.astype(bf16)
        rest = x - hi.astype(f32)
        mid = rest.astype(bf16)
        low = (rest - mid.astype(f32)).astype(bf16)
        dims = (((1,) if form == "nn" else (0,), (0,)), ((), ()))
        return sum(lax.dot_general(tri, part, dims, preferred_element_type=f32) for part in (hi, mid, low))

    @jax.custom_vjp
    def run(x):
        return raw(x, "nn")

    run.defvjp(lambda x: (raw(x, "nn"), None), lambda _, ct: (raw(ct, "tn"),))
    return run(g)


def _unit_lower_inverse(a):
    c = a.shape[0]
    eye = (lax.broadcasted_iota(jnp.int32, (c, c), 0) == lax.broadcasted_iota(jnp.int32, (c, c), 1)).astype(f32)
    inv, power = eye - a, -a
    for _ in range(c.bit_length() - 2):
        power = _hi(power, power)
        inv = inv + _hi(inv, power)
    return inv


def _known_inverse(a, t):
    @jax.custom_vjp
    def inv(a, t):
        return t

    def inv_b(t, ct):
        return -_hi(_hi(t, ct, "tn"), t, "nt"), jnp.zeros_like(t)

    inv.defvjp(lambda a, t: (t, t), inv_b)
    return inv(a, t)


def _chunk_prepare(q, k, v, g, beta, known_inv=None):
    c = q.shape[0]
    i = lax.broadcasted_iota(jnp.int32, (c, c), 0)
    j = lax.broadcasted_iota(jnp.int32, (c, c), 1)
    causal, strict = i >= j, i > j
    dec = _running_sum(g)
    dec_i = dec[:, :c]
    rel = jnp.exp(jnp.where(causal, dec_i - dec_i.T, -jnp.inf))
    k_beta = k * beta
    on_k = _mxu(jnp.concatenate([k_beta, q], axis=0), k, "nt")
    a = jnp.where(strict, on_k[:c] * rel, 0.0)
    attn = jnp.where(causal, on_k[c:] * rel, 0.0)
    inv = _unit_lower_inverse(a) if known_inv is None else _known_inverse(a, known_inv)
    e_dec = jnp.exp(dec)
    solved = _hi(inv, jnp.concatenate([v * beta, k_beta * e_dec], axis=1))
    u, w = solved[:, :v.shape[1]], solved[:, v.shape[1]:]
    total = jnp.sum(g, axis=0, keepdims=True)
    return inv, (u, w, attn, q * e_dec, k * jnp.exp(total - dec), total)


def _chunk_advance(u, w, attn, q_dec, k_dec, total, s0):
    c = u.shape[0]
    on_state = _mxu(jnp.concatenate([w, q_dec], axis=0), s0, "nn")
    v_new = u - on_state[:c]
    return on_state[c:] + _mxu(attn, v_new, "nn"), s0 * jnp.exp(total) + _mxu(k_dec, v_new, "tn")


def _delta_chunk(q, k, v, g, beta, s0, known_inv=None):
    inv, ready = _chunk_prepare(q, k, v, g, beta, known_inv)
    return (*_chunk_advance(*ready, s0), inv)


def _delta_rule_call(name, walk, steps, in_specs, out_specs, out_shape, operands, exchange, scratch=()):
    n_in, n_out = len(in_specs), len(out_specs)
    carried = len(exchange.operands) if exchange else 0

    def body(*refs):
        ins, refs = refs[:n_in], refs[n_in:]
        x_refs, refs = refs[:carried], refs[carried:]
        outs, refs = refs[:n_out], refs[n_out:]
        land_refs, (state, *refs) = refs[:carried], refs[carried:]
        own, sems = refs[:len(scratch)], refs[len(scratch):]
        step = pl.program_id(0)
        if exchange:
            start, early, finish = exchange.bind_with_early(x_refs, land_refs, sems, 3)
            pl.when(step == 0)(start)

        @pl.when(step == 0)
        def _():
            for r in (state, *own):
                r[...] = jnp.zeros_like(r)

        walk(ins, outs, state, *own)
        if exchange:
            pl.when(step == (3 * steps) // 4)(early)
            pl.when(step == steps - 1)(finish)

    return pl.pallas_call(
        body, name=name, grid=(steps,), in_specs=list(in_specs) + [ANY] * carried, out_specs=list(out_specs) + [ANY] * carried,
        out_shape=list(out_shape) + (exchange.out_shapes if exchange else []),
        scratch_shapes=[pltpu.VMEM((GDN_HEAD_DIM, GDN_WIDTH), f32), *scratch] + (exchange.scratch if exchange else []),
        compiler_params=_params(1))(*operands, *(exchange.operands if exchange else []))


def _delta_heads():
    heads = [slice(hd * GDN_HEAD_DIM, (hd + 1) * GDN_HEAD_DIM) for hd in range(GDN_HEADS)]
    inv_at = [slice(hd * GDN_CHUNK, (hd + 1) * GDN_CHUNK) for hd in range(GDN_HEADS)]
    return heads, inv_at


def _head_chunk(q, k, v, gates, s0, head, known_inv=None):
    g = jnp.broadcast_to(gates[:, head:head + 1], q.shape)
    beta = jnp.broadcast_to(gates[:, GDN_HEADS + head:GDN_HEADS + head + 1], q.shape)
    return _delta_chunk(q, k, v, g, beta, s0, known_inv)


def delta_rule_fwd(q, k, v, gates, name, exchange=None):
    s, width = q.shape
    c, dk = GDN_CHUNK, GDN_HEAD_DIM
    n = s // c
    heads, inv_at = _delta_heads()

    def walk(ins, outs, state, u_s, w_s, q_s, k_s, attn_s, total_s):
        q_ref, k_ref, v_ref, gates_ref = ins
        o_ref, s_in_ref, inv_ref = outs
        s_in_ref[...] = state[...]
        gates = gates_ref[...]
        ready = [[r[:, hd] for r in (u_s, w_s)] + [attn_s[:, at]] + [r[:, hd] for r in (q_s, k_s)] + [total_s[0:1, hd], state[:, hd]]
                 for hd, at in zip(heads, inv_at)]
        fresh = []
        for i, hd in enumerate(heads):
            g = jnp.broadcast_to(gates[:, i:i + 1], (c, dk))
            beta = jnp.broadcast_to(gates[:, GDN_HEADS + i:GDN_HEADS + i + 1], (c, dk))
            fresh.append(_chunk_prepare(q_ref[:, hd], k_ref[:, hd], v_ref[:, hd], g, beta))
        advanced = [_chunk_advance(*x) for x in ready]
        for hd, at, (o, s1), (inv, (u, w, attn, q_dec, k_dec, total)) in zip(heads, inv_at, advanced, fresh):
            o_ref[:, hd], state[:, hd], inv_ref[:, at] = o, s1, inv
            u_s[:, hd], w_s[:, hd], q_s[:, hd], k_s[:, hd], attn_s[:, at], total_s[0:1, hd] = u, w, q_dec, k_dec, attn, total

    blk = pl.BlockSpec((c, width), lambda t: (jnp.minimum(t, n - 1), 0))
    gt = pl.BlockSpec((c, LANES), lambda t: (jnp.minimum(t, n - 1), 0))
    iv = pl.BlockSpec((c, GDN_HEADS * c), lambda t: (jnp.minimum(t, n - 1), 0))
    out = pl.BlockSpec((c, width), lambda t: (jnp.maximum(t - 1, 0), 0))
    st = pl.BlockSpec((dk, width), lambda t: (jnp.maximum(t - 1, 0), 0))
    rows = pltpu.VMEM((c, width), f32)
    return _delta_rule_call(name, walk, n + 1, [blk] * 3 + [gt], [out, st, iv],
                            [_sds((s, width)), _sds((n * dk, width)), _sds((s, GDN_HEADS * c))], (q, k, v, gates), exchange,
                            scratch=(rows, rows, rows, rows, pltpu.VMEM((c, GDN_HEADS * c), f32), pltpu.VMEM((8, width), f32)))


def delta_rule_bwd(q, k, v, gates, s_in, inv, do, name, exchange=None):
    s, width = q.shape
    c, dk = GDN_CHUNK, GDN_HEAD_DIM
    n = s // c
    heads, inv_at = _delta_heads()

    def walk(ins, outs, dstate):
        q_ref, k_ref, v_ref, gates_ref, s_ref, inv_ref, do_ref = ins
        dq_ref, dk_ref, dv_ref, dgates_ref = outs
        gates = gates_ref[...]
        xs = [[r[:, hd] for r in (q_ref, k_ref, v_ref)] + [gates, s_ref[:, hd]] for hd in heads]
        known = [inv_ref[:, at] for at in inv_at]
        cts = [(do_ref[:, hd], dstate[:, hd]) for hd in heads]
        grads = []
        for i, (x, t, ct) in enumerate(zip(xs, known, cts)):
            _, vjp = jax.vjp(lambda *y, t=t, i=i: _head_chunk(*y, i, known_inv=t)[:2], *x)
            grads.append(vjp(ct))
        dgates = grads[0][3]
        for g in grads[1:]:
            dgates = dgates + g[3]
        dgates_ref[...] = dgates
        for hd, (dq, dk_, dv, _, ds0) in zip(heads, grads):
            dq_ref[:, hd], dk_ref[:, hd], dv_ref[:, hd], dstate[:, hd] = dq, dk_, dv, ds0

    blk = pl.BlockSpec((c, width), lambda t: (n - 1 - t, 0))
    gt = pl.BlockSpec((c, LANES), lambda t: (n - 1 - t, 0))
    st = pl.BlockSpec((dk, width), lambda t: (n - 1 - t, 0))
    iv = pl.BlockSpec((c, GDN_HEADS * c), lambda t: (n - 1 - t, 0))
    return _delta_rule_call(name, walk, n, [blk] * 3 + [gt, st, iv, blk], [blk] * 3 + [gt],
                            [_sds((s, width))] * 3 + [_sds((s, LANES))], (q, k, v, gates, s_in, inv, do), exchange)


def adamw(w, g, m, v, name):
    shape = w.shape
    if len(shape) == 2:
        grid, spec = (1,), pl.BlockSpec(shape, lambda i: (0, 0))
    else:
        tile = shape[1] if shape[1] <= 512 else _pick(shape[1], (512, 256, 128))
        grid, spec = (shape[0], shape[1] // tile), pl.BlockSpec((None, tile, shape[2]), lambda layer, i: (layer, i, 0))

    def body(w_ref, g_ref, m_ref, v_ref, d_ref, nm_ref, nv_ref):
        grad = g_ref[...]
        nm = ADAM_B1 * m_ref[...] + (1.0 - ADAM_B1) * grad
        nv = ADAM_B2 * v_ref[...] + (1.0 - ADAM_B2) * (grad * grad)
        m_hat = nm / (1.0 - ADAM_B1 ** ADAM_STEP)
        v_hat = nv / (1.0 - ADAM_B2 ** ADAM_STEP)
        d_ref[...] = -ADAM_LR * (m_hat / (jnp.sqrt(v_hat) + ADAM_EPS) + ADAM_WD * w_ref[...])
        nm_ref[...] = nm
        nv_ref[...] = nv

    return tuple(pl.pallas_call(body, name=name, grid=grid, in_specs=[spec] * 4, out_specs=[spec] * 3,
                                out_shape=[_sds(shape)] * 3, compiler_params=_params(len(grid)))(w, g, m, v))


def _place():
    return lax.axis_index("x"), lax.axis_index("y"), lax.axis_index("c")


def _flip(p, bits):
    return tuple(1 - v if (bits >> s) & 1 else v for v, s in zip(p, (2, 1, 0)))


def _slot(p):
    return 4 * p[0] + 2 * p[1] + p[2]


def _chip_of(p):
    return 2 * p[0] + p[1]


ANY = pl.BlockSpec(memory_space=pl.ANY)


class Gather:
    scratch = (pltpu.SemaphoreType.DMA((7,)), pltpu.SemaphoreType.DMA((7,)), pltpu.SemaphoreType.DMA)

    def __init__(self, shard):
        self.operand = shard
        self.out_shape = jax.ShapeDtypeStruct((N_DEV,) + shard.shape, shard.dtype)

    def bind(self, x_ref, out_ref, send_sems, recv_sems, local_sem):
        me = _place()
        sibling = _flip(me, 1)
        chips = [_flip(me, 4), _flip(me, 2), _flip(me, 6)]

        def copy(k, block, to, src=None):
            return pltpu.make_async_remote_copy(
                src_ref=out_ref.at[_slot(block)] if src is None else src, dst_ref=out_ref.at[_slot(block)],
                send_sem=send_sems.at[k], recv_sem=recv_sems.at[k], device_id=to, device_id_type=MESH)

        mine = pltpu.make_async_copy(x_ref, out_ref.at[_slot(me)], local_sem)
        first = [copy(0, me, sibling, src=x_ref)] + [copy(1 + j, me, chip, src=x_ref) for j, chip in enumerate(chips)]
        passed = [copy(4 + j, chip, sibling) for j, chip in enumerate(chips)]

        def start():
            mine.start()
            for cp in first:
                cp.start()

        def pass_on():
            for j, chip in enumerate(chips):
                copy(1 + j, chip, me).wait_recv()
                passed[j].start()

        def settle():
            copy(0, sibling, me).wait_recv()
            for j, chip in enumerate(chips):
                copy(4 + j, _flip(chip, 1), me).wait_recv()
            for cp in first + passed:
                cp.wait_send()
            mine.wait()

        return start, pass_on, settle


class ChipExchange:
    scratch = (pltpu.SemaphoreType.DMA((3,)), pltpu.SemaphoreType.DMA((3,)), pltpu.SemaphoreType.DMA)

    def __init__(self, blocks):
        self.operand = blocks
        self.out_shape = jax.ShapeDtypeStruct(blocks.shape, blocks.dtype)

    def bind(self, x_ref, out_ref, send_sems, recv_sems, local_sem):
        me = _place()
        peers = [_flip(me, 4), _flip(me, 2), _flip(me, 6)]
        mine = pltpu.make_async_copy(x_ref.at[_chip_of(me)], out_ref.at[_chip_of(me)], local_sem)

        def copy(j, src_chip, dst_chip):
            return pltpu.make_async_remote_copy(
                src_ref=x_ref.at[src_chip], dst_ref=out_ref.at[dst_chip], send_sem=send_sems.at[j],
                recv_sem=recv_sems.at[j], device_id=peers[j], device_id_type=MESH)

        sends = [copy(j, _chip_of(peer), _chip_of(me)) for j, peer in enumerate(peers)]

        def start():
            mine.start()
            for cp in sends:
                cp.start()

        def finish():
            for j, peer in enumerate(peers):
                copy(j, _chip_of(me), _chip_of(peer)).wait_recv()
            for cp in sends:
                cp.wait_send()
            mine.wait()

        return start, lambda: None, finish


class Together:
    def __init__(self, *parts):
        self.parts = parts
        self.operands = [p.operand for p in parts]
        self.out_shapes = [p.out_shape for p in parts]
        self.scratch = [s for p in parts for s in p.scratch]

    def bind(self, x_refs, out_refs, sems):
        start, _, finish = self.bind_with_early(x_refs, out_refs, sems, 0)
        return start, finish

    def bind_with_early(self, x_refs, out_refs, sems, n_early):
        bound, at = [], 0
        for p, x_ref, out_ref in zip(self.parts, x_refs, out_refs):
            bound.append(p.bind(x_ref, out_ref, *sems[at:at + len(p.scratch)]))
            at += len(p.scratch)

        def start():
            for s, _, _ in bound:
                s()

        def early():
            for _, pass_on, _ in bound[:n_early]:
                pass_on()

        def finish():
            for _, pass_on, _ in bound[n_early:]:
                pass_on()
            for _, _, settle in bound:
                settle()

        return start, early, finish


def exchange_alone(exchange, name):
    n = len(exchange.operands)

    def body(*refs):
        start, finish = exchange.bind(refs[:n], refs[n:2 * n], refs[2 * n:])
        start()
        finish()

    return pl.pallas_call(body, name=name, out_shape=exchange.out_shapes, in_specs=[ANY] * n, out_specs=[ANY] * n,
                          scratch_shapes=exchange.scratch)(*exchange.operands)


def _row_tile(rows):
    return max([t for t in range(16, min(rows, 1024) + 1, 16) if rows % t == 0] or [rows])


def pair_exchange(blocks, name):
    n = len(blocks)

    def body(*refs):
        x_refs, theirs_refs, (send_sems, recv_sems) = refs[:n], refs[n:2 * n], refs[2 * n:]
        me = _place()
        remote = [pltpu.make_async_remote_copy(
            src_ref=x_refs[t].at[2 * q + 1 - me[2]], dst_ref=theirs_refs[t].at[q], send_sem=send_sems.at[4 * t + q],
            recv_sem=recv_sems.at[4 * t + q], device_id=_flip(me, 1), device_id_type=MESH) for t in range(n) for q in range(4)]
        for cp in remote:
            cp.start()
        for cp in remote:
            cp.wait()

    return pl.pallas_call(
        body, name=name, out_shape=[jax.ShapeDtypeStruct((4,) + b.shape[1:], b.dtype) for b in blocks], in_specs=[ANY] * n,
        out_specs=[ANY] * n, scratch_shapes=[pltpu.SemaphoreType.DMA((4 * n,)), pltpu.SemaphoreType.DMA((4 * n,))])(*blocks)


def pair_add(blocks, theirs, name):
    n, rows, width = theirs.shape
    tile = _row_tile(rows)
    spec = pl.BlockSpec((None, tile, width), lambda q, i: (q, i, 0))
    south = pl.BlockSpec((None, None, tile, width), lambda q, i: (q, 0, i, 0))
    north = pl.BlockSpec((None, None, tile, width), lambda q, i: (q, 1, i, 0))

    def body(s_ref, n_ref, b_ref, o_ref):
        mine = jnp.where(lax.axis_index("c") == 0, s_ref[...], n_ref[...])
        o_ref[...] = (mine.astype(f32) + b_ref[...].astype(f32)).astype(o_ref.dtype)

    by_core = blocks.reshape(n, 2, rows, width)
    return pl.pallas_call(body, name=name, grid=(n, rows // tile), in_specs=[south, north, spec], out_specs=spec,
                          out_shape=jax.ShapeDtypeStruct(theirs.shape, theirs.dtype), compiler_params=_params(2))(by_core, by_core, theirs)


def sum_slots(blocks, name):
    n, rows, width = blocks.shape
    tile = _row_tile(rows)

    def body(x_ref, o_ref):
        total = x_ref[0].astype(f32)
        for s in range(1, n):
            total = total + x_ref[s].astype(f32)
        o_ref[...] = total

    return pl.pallas_call(
        body, name=name, grid=(rows // tile,), in_specs=[pl.BlockSpec((n, tile, width), lambda i: (0, i, 0))],
        out_specs=pl.BlockSpec((tile, width), lambda i: (i, 0)), out_shape=_sds((rows, width)), compiler_params=_params(1))(blocks)


def all_reduce_small(x, name):
    rows, width = x.shape

    def body(x_ref, o_ref, land, send_sems, recv_sems):
        me = _place()
        copies = []
        for k in range(1, N_DEV):
            peer = _flip(me, k)
            copies.append(pltpu.make_async_remote_copy(
                src_ref=x_ref, dst_ref=land.at[_slot(me)], send_sem=send_sems.at[k - 1], recv_sem=recv_sems.at[k - 1],
                device_id=peer, device_id_type=MESH))
        for cp in copies:
            cp.start()
        land[_slot(me)] = x_ref[...]
        for k in range(1, N_DEV):
            peer = _flip(me, k)
            pltpu.make_async_remote_copy(
                src_ref=x_ref, dst_ref=land.at[_slot(peer)], send_sem=send_sems.at[k - 1], recv_sem=recv_sems.at[k - 1],
                device_id=peer, device_id_type=MESH).wait_recv()
        total = land[0]
        for s in range(1, N_DEV):
            total = total + land[s]
        o_ref[...] = total
        for cp in copies:
            cp.wait_send()

    return pl.pallas_call(
        body, name=name, out_shape=_sds((rows, width)), in_specs=[pl.BlockSpec(memory_space=pltpu.VMEM)],
        out_specs=pl.BlockSpec(memory_space=pltpu.VMEM),
        scratch_shapes=[pltpu.VMEM((N_DEV, rows, width), f32), pltpu.SemaphoreType.DMA((7,)), pltpu.SemaphoreType.DMA((7,))],
    )(x)


def _pack_big(shards):
    packed = {name: shards[name].astype(bf16) for name in COL_SHARDED}
    packed["rows"] = jnp.concatenate([shards[name].astype(bf16) for name, _ in ROW_SHARDED], axis=1)
    return packed


def _unpack_gathered(gathered):
    full = {}
    for name, part in gathered.items():
        if name == "w_gate_up":
            full[name] = part
        elif name in COL_SHARDED:
            full[name] = part.transpose(1, 0, 2).reshape(D_MODEL, N_DEV * part.shape[2])
        else:
            at = 0
            for weight, rows in ROW_SHARDED:
                full[weight] = part[:, at:at + rows, :].reshape(N_DEV * rows, D_MODEL)
                at += rows
    return full


def _pack_grads(grads, group):
    packed = {}
    for name in group:
        if name == "w_gate_up":
            packed[name] = grads[name]
        elif name == "rows":
            packed[name] = jnp.concatenate([grads[weight].reshape(N_DEV, rows, D_MODEL) for weight, rows in ROW_SHARDED], axis=1)
        else:
            packed[name] = grads[name].reshape(D_MODEL, N_DEV, grads[name].shape[1] // N_DEV).transpose(1, 0, 2)
    return packed


def _unpack_shard(layers):
    out = {name: jnp.stack([layer[name] for layer in layers]) for name in COL_SHARDED}
    rows_pack, at = jnp.stack([layer["rows"] for layer in layers]), 0
    for weight, rows in ROW_SHARDED:
        out[weight] = rows_pack[:, at:at + rows, :]
        at += rows
    return out


def _rows_of(flat_len):
    return -(-flat_len // (8 * D_MODEL)) * 8


def _pack_small(parts):
    flat = jnp.concatenate([p.reshape(-1) for p in parts])
    rows = _rows_of(flat.shape[0])
    flat = jnp.pad(flat, (0, rows * D_MODEL - flat.shape[0]))
    return flat.reshape(rows, D_MODEL)


def _unpack_small(packed, like):
    flat, out, at = packed.reshape(-1), [], 0
    for p in like:
        out.append(flat[at:at + p.size].reshape(p.shape))
        at += p.size
    return out


def _rope_tables(positions):
    inv_freq = jnp.float32(ROPE_THETA) ** (-jnp.arange(0, ROPE_DIM, 2, dtype=f32) / ROPE_DIM)
    ang = positions.astype(f32)[:, None] * inv_freq
    cos, sin = jnp.cos(ang), jnp.sin(ang)
    rest = ATTN_HEAD_DIM - ROPE_DIM
    cos_h = jnp.concatenate([cos, cos, jnp.ones((cos.shape[0], rest), f32)], axis=1)
    sin_h = jnp.concatenate([-sin, sin, jnp.zeros((sin.shape[0], rest), f32)], axis=1)
    return jnp.tile(cos_h, (1, ATTN_HEADS)), jnp.tile(sin_h, (1, ATTN_HEADS))


HEAD_SMALL = ("norm_mix_pre", "conv_short", "conv_gdn", "gdn_a_log", "gdn_dt_bias")


def _layer_head(h, p, cos_t, sin_t):
    hn = rms_norm(h, p["norm_mix_pre"][None], "norm_mix_pre")
    aw, cw, gw = ATTN_WIDTH, CONV_WIDTH, GDN_WIDTH
    aq, ak, av, cb, cc, cx, gqkv, ab, gate = _split_cols(_linear(hn, p["w_in"], "w_in"),
                                                         (aw, aw, aw, cw, cw, cw, 3 * gw, 2 * GDN_HEADS, gw))
    ab = jnp.pad(ab, ((0, 0), (0, LANES - 2 * GDN_HEADS)))
    y_attn = dilated_attention(rope(aq, cos_t, sin_t, ATTN_HEAD_DIM ** -0.5, "rope_q"), rope(ak, cos_t, sin_t, 1.0, "rope_k"),
                               av, "attn")
    y_conv = short_conv(cb, cc, cx, p["conv_short"], "short_conv")
    qkv = gdn_pre(gqkv, p["conv_gdn"], "gdn_pre")
    pv = jnp.zeros((8, LANES), f32).at[0, :GDN_HEADS].set(p["gdn_a_log"]).at[1, :GDN_HEADS].set(p["gdn_dt_bias"])
    return (*_split_cols(qkv, (gw, gw, gw)), gate_beta(ab, pv, "gate_beta")), (gate, y_attn, y_conv)


MID_PARAMS = ("gdn_norm", "w_out", "norm_mix_post", "norm_xattn_pre", "w_xq", "norm_mem", "w_xkv", "w_xo", "norm_xattn_post",
              "norm_ffn_pre")


def _layer_mid(h, o, gate, y_attn, y_conv, p, mem):
    y_gdn = gdn_post(o, gate, p["gdn_norm"][None], "gdn_post")
    mix = _joined_linear((y_attn, y_conv, y_gdn), p["w_out"], "w_out")
    h, hn = add_norm_then_norm(h, mix, p["norm_mix_post"][None], p["norm_xattn_pre"][None], "norm_mix_xattn")
    qx = _linear(hn, p["w_xq"], "w_xq")
    kv = _linear(rms_norm(mem, p["norm_mem"][None], "norm_mem"), p["w_xkv"], "w_xkv")
    xa = _linear(cross_attention(qx, kv, "xattn"), p["w_xo"], "w_xo")
    return add_norm_then_norm(h, xa, p["norm_xattn_post"][None], p["norm_ffn_pre"][None], "norm_xattn_ffn")


def _pair_summed(grads, group, name):
    blocks = _pack_grads(grads, group)
    theirs = pair_exchange([blocks[n] for n in group], name + "_pair_exchange")
    return [pair_add(blocks[n], t, f"{name}_pair_add_{n}") for n, t in zip(group, theirs)]


def _forward_backward(x, packed, small, mem, cos_t, sin_t, target):
    def gathers(group, layer):
        return [Gather(packed[n][layer]) for n in group]

    h = x
    head_gathered = exchange_alone(Together(*gathers(HEAD_GROUP, 0)), "gather_first")
    saved = []
    for layer in range(DEPTH):
        at_layer = {n: t[layer] for n, t in small.items()}
        head_p = {**_unpack_gathered(dict(zip(HEAD_GROUP, head_gathered))), **{n: at_layer[n] for n in HEAD_SMALL}}
        (rule_in, rest), head_vjp = jax.vjp(lambda h, hp: _layer_head(h, hp, cos_t, sin_t), h, head_p)
        carried = gathers(TAIL_GROUP, layer) + (gathers(HEAD_GROUP, layer + 1) if layer + 1 < DEPTH else [])
        o, s_in, inv, *landed = delta_rule_fwd(*rule_in, "delta_rule_fwd", Together(*carried))
        head_gathered = landed[len(TAIL_GROUP):]
        tail_p = {**_unpack_gathered(dict(zip(TAIL_GROUP, landed))), **at_layer}
        mid_p = {n: tail_p[n] for n in MID_PARAMS}
        (h, hn), mid_vjp = jax.vjp(lambda h, o, rest, mp: _layer_mid(h, o, *rest, mp, mem), h, o, rest, mid_p)
        y, ffn_saved = ffn_forward(hn, tail_p["w_gate_up"], tail_p["w_down"], "ffn")
        h, last_vjp = jax.vjp(lambda h, y, w: add_norm(h, y, w[None], "norm_ffn_post"), h, y, tail_p["norm_ffn_post"])
        saved.append((head_vjp, mid_vjp, last_vjp, ffn_saved, rule_in, s_in, inv))

    loss, dh = jax.value_and_grad(lambda y: loss_rows(y, target, "loss"))(h)

    def summed(group, landed):
        return {n: sum_slots(t, "sum_grads_" + n) for n, t in zip(group, landed)}

    big_grads, small_grads, head_pending = [{} for _ in range(DEPTH)], [None] * DEPTH, []
    for layer in reversed(range(DEPTH)):
        head_vjp, mid_vjp, last_vjp, ffn_saved, rule_in, s_in, inv = saved[layer]
        dh, dy, d_norm_ffn_post = last_vjp(dh)
        dhn, d_gate_up, d_down, landed = ffn_backward(
            ffn_saved, dy, "ffn", Together(*[ChipExchange(t) for t in head_pending]) if head_pending else None)
        if head_pending:
            big_grads[layer + 1].update(summed(HEAD_GROUP, landed))
        dh_mid, do, d_rest, d_mid_p = mid_vjp((dh, dhn))
        d_tail_p = {**d_mid_p, "w_gate_up": d_gate_up, "w_down": d_down, "norm_ffn_post": d_norm_ffn_post}
        carried = Together(*[ChipExchange(t) for t in _pair_summed(d_tail_p, TAIL_GROUP, "tail")])
        *d_rule_in, = delta_rule_bwd(*rule_in, s_in, inv, do, "delta_rule_bwd", carried)
        big_grads[layer].update(summed(TAIL_GROUP, d_rule_in[4:]))
        dh_head, d_head_p = head_vjp((tuple(d_rule_in[:4]), d_rest))
        dh = dh_mid + dh_head
        small_grads[layer] = {n: t for n, t in {**d_head_p, **d_tail_p}.items() if n in small}
        head_pending = _pair_summed(d_head_p, HEAD_GROUP, "head")
    landed = exchange_alone(Together(*[ChipExchange(t) for t in head_pending]), "exchange_last")
    big_grads[0].update(summed(HEAD_GROUP, landed))
    return loss, dh, big_grads, small_grads


def kernel(x, mem, positions, norm_mix_pre, norm_mix_post, w_in, conv_short, conv_gdn, gdn_a_log, gdn_dt_bias, gdn_norm, w_out, norm_mem, norm_xattn_pre, norm_xattn_post, w_xq, w_xkv, w_xo, norm_ffn_pre, norm_ffn_post, w_gate_up, w_down, loss_target, m_norm_mix_pre, m_norm_mix_post, m_w_in, m_conv_short, m_conv_gdn, m_gdn_a_log, m_gdn_dt_bias, m_gdn_norm, m_w_out, m_norm_mem, m_norm_xattn_pre, m_norm_xattn_post, m_w_xq, m_w_xkv, m_w_xo, m_norm_ffn_pre, m_norm_ffn_post, m_w_gate_up, m_w_down, v_norm_mix_pre, v_norm_mix_post, v_w_in, v_conv_short, v_conv_gdn, v_gdn_a_log, v_gdn_dt_bias, v_gdn_norm, v_w_out, v_norm_mem, v_norm_xattn_pre, v_norm_xattn_post, v_w_xq, v_w_xkv, v_w_xo, v_norm_ffn_pre, v_norm_ffn_post, v_w_gate_up, v_w_down):
    given = dict(locals())
    weights = {n: given[n] for n in WEIGHTS}
    me = _slot(_place())

    def in_place(shard):
        full = jnp.zeros(shard.shape[:-1] + (shard.shape[-1] * N_DEV,), f32)
        return lax.dynamic_update_slice_in_dim(full, shard, me * shard.shape[-1], axis=shard.ndim - 1)

    placed = [in_place(conv_short), in_place(conv_gdn)]
    conv_short_full, conv_gdn_full = _unpack_small(all_reduce_small(_pack_small(placed), "gather_conv"), placed)
    small = {n: weights[n] for n in NORMS + ("gdn_a_log", "gdn_dt_bias", "gdn_norm")}
    small["conv_short"], small["conv_gdn"] = conv_short_full, conv_gdn_full

    cos_t, sin_t = _rope_tables(positions[0])
    loss, grad_x, big_layers, small_layers = _forward_backward(
        x[0], _pack_big(weights), small, mem[0], cos_t, sin_t, loss_target[0])
    grads = _unpack_shard(big_layers)

    names = sorted(small)
    parts = [jnp.stack([layer[n] for layer in small_layers]) for n in names] + [loss.reshape(1)]
    reduced = _unpack_small(all_reduce_small(_pack_small(parts), "reduce_small"), parts)
    loss = reduced[-1][0]
    for n, g in zip(names, reduced[:-1]):
        if n in ("conv_short", "conv_gdn"):
            width = weights[n].shape[-1]
            g = lax.dynamic_slice_in_dim(g, me * width, width, axis=g.ndim - 1)
        grads[n] = g

    delta, new_m, new_v = {}, {}, {}
    for n in WEIGHTS:
        delta[n], new_m[n], new_v[n] = adamw(weights[n], grads[n], given["m_" + n], given["v_" + n], "adamw_" + n)
    return (loss, grad_x[None], *[grads[n] for n in WEIGHTS], *[delta[n] for n in WEIGHTS],
            *[new_m[n] for n in WEIGHTS], *[new_v[n] for n in WEIGHTS])
```
